```python
import math
import jax, jax.numpy as jnp
from jax import lax
import numpy as np

D_MODEL = 1024
BATCH = 32
SEQ = 256
DEPTH = 2
DEC_BATCH = 2
DEC_SEQ = 2048
PAST_LEN = 512

GRID_W = 64
Q_BLOCK = 128
ROPE_THETA = 10000.0
EPS = 1e-6

MLA_HEADS = 8
MLA_NOPE = 64
MLA_ROPE = 32
MLA_V = 64
MLA_Q_LORA = 256
MLA_KV_LORA = 128
MLA_WIDTH = MLA_HEADS * MLA_V
MLA_SCALE = (MLA_NOPE + MLA_ROPE) ** -0.5
DIFF_HEADS = 4
DIFF_HEAD_DIM = 64
DIFF_V = 2 * DIFF_HEAD_DIM
DIFF_WIDTH = DIFF_HEADS * DIFF_V
DIFF_SCALE = DIFF_HEAD_DIM ** -0.5
GQA_HEADS = 8
GQA_KV_HEADS = 2
GQA_HEAD_DIM = 64
GQA_GROUP = GQA_HEADS // GQA_KV_HEADS
GQA_WIDTH = GQA_HEADS * GQA_HEAD_DIM
GQA_SCALE = GQA_HEAD_DIM ** -0.5
SSM_WIDTH = 512
SSM_GROUP = 16
SSM_GROUPS = SSM_WIDTH // SSM_GROUP
SSM_STATE = 64
SSM_DT_MIN = 1e-3
SSM_DT_MAX = 1e-1

N_BRANCH = 4
BRANCH_WIDTH = 512
IN_SPLITS = (MLA_Q_LORA, MLA_KV_LORA, MLA_ROPE,
             DIFF_HEADS * 2 * DIFF_HEAD_DIM, DIFF_HEADS * 2 * DIFF_HEAD_DIM, DIFF_HEADS * DIFF_V,
             GQA_HEADS * GQA_HEAD_DIM, GQA_KV_HEADS * GQA_HEAD_DIM, GQA_KV_HEADS * GQA_HEAD_DIM,
             SSM_WIDTH,
             N_BRANCH * BRANCH_WIDTH,
             N_BRANCH * D_MODEL)
D_IN = sum(IN_SPLITS)

kernel_name = 'hybrid_prefix_dit_mla_diff_gqa_s5'


def _rms_norm(x, g):
    xf = x.astype(jnp.float32)
    y = xf * lax.rsqrt(jnp.mean(xf * xf, axis=-1, keepdims=True) + EPS)
    return (y * g.astype(jnp.float32)).astype(x.dtype)


def _modulate(x, g, shift, scale):
    return _rms_norm(x, g) * (1.0 + scale) + shift


def _rope_tables(n_tok, rot_dim):
    rows = n_tok // GRID_W
    row = jnp.repeat(jnp.arange(rows, dtype=jnp.float32), GRID_W)
    col = jnp.tile(jnp.arange(GRID_W, dtype=jnp.float32), rows)
    quarter = rot_dim // 4
    inv = ROPE_THETA ** (-jnp.arange(quarter, dtype=jnp.float32) / quarter)
    ang = jnp.concatenate([row[:, None] * inv, col[:, None] * inv], axis=-1)
    return jnp.cos(ang), jnp.sin(ang)


def _apply_rope(x, cos, sin):
    half = x.shape[-1] // 2
    xf = x.astype(jnp.float32)
    x1, x2 = xf[..., :half], xf[..., half:]
    c, s = cos[:, None, :], sin[:, None, :]
    return jnp.concatenate([x1 * c - x2 * s, x1 * s + x2 * c], axis=-1).astype(x.dtype)


def _sweep_query_blocks(fn, qs):
    b, t = qs[0].shape[0], qs[0].shape[1]
    nb = t // Q_BLOCK
    blocked = tuple(jnp.moveaxis(q.reshape((b, nb, Q_BLOCK) + q.shape[2:]), 1, 0) for q in qs)
    out = lax.map(lambda a: fn(*a), blocked)
    out = jnp.moveaxis(out, 0, 1)
    return out.reshape((b, t) + out.shape[3:])


def _softmax_attn(q, k, v, scale):
    s = jnp.einsum('bqhgd,bshd->bhgqs', q, k).astype(jnp.float32) * scale
    p = jax.nn.softmax(s, axis=-1)
    return jnp.einsum('bhgqs,bshe->bqhge', p.astype(v.dtype), v)


def _lin_combine(left, right):
    a1, b1 = left
    a2, b2 = right
    return a2 * a1, a2 * b1 + b2


def _ssm_direction(u, s0, a_re, a_im, log_dt, b_re, b_im, c_re, c_im):
    f32 = jnp.float32
    lam = lax.complex(a_re.astype(f32), a_im.astype(f32))
    dt = jnp.exp(log_dt.astype(f32))[:, None]
    abar = jnp.exp(lam * dt)
    bbar = ((abar - 1.0) / lam)[..., None] * lax.complex(b_re.astype(f32), b_im.astype(f32))
    cmat = lax.complex(c_re.astype(f32), c_im.astype(f32))
    bu = jnp.einsum('btgn,gpn->btgp', u.astype(jnp.complex64), bbar)
    if s0 is not None:
        bu = bu.at[:, 0].add(abar * s0)
    a = jnp.broadcast_to(abar, bu.shape)
    _, hs = lax.associative_scan(_lin_combine, (a, bu), axis=1)
    y = jnp.real(jnp.einsum('btgp,gnp->btgn', hs, cmat))
    return y, hs[:, -1]


def _mixer(h, p, lam_init, rope, ctx):
    b, t, _ = h.shape
    points, acc = [], 0
    for width in IN_SPLITS[:-1]:
        acc += width
        points.append(acc)
    (q_a, kv_a, k_pe, dq, dk, dv, gq, gk, gv, u, gate_in, merge_in) = jnp.split(h @ p['w_in'], points, axis=-1)
    latent = ctx is not None

    q = (_rms_norm(q_a, p['mla_q_norm']) @ p['w_mla_q_b']).reshape(b, t, MLA_HEADS, MLA_NOPE + MLA_ROPE)
    q_nope, q_pe = q[..., :MLA_NOPE], q[..., MLA_NOPE:]
    c_kv = _rms_norm(kv_a, p['mla_kv_norm'])
    k_pe = k_pe[:, :, None, :]
    if latent:
        q_pe = _apply_rope(q_pe, *rope['mla'])
        ckv_all = jnp.concatenate([ctx['mla_ckv'], c_kv], axis=1)
        kpe_all = jnp.concatenate([ctx['mla_krope'][:, :, None, :], _apply_rope(k_pe, *rope['mla'])], axis=1)
    else:
        ckv_all, kpe_all = c_kv, k_pe
    s_len = ckv_all.shape[1]
    kv = (ckv_all @ p['w_mla_kv_b']).reshape(b, s_len, MLA_HEADS, MLA_NOPE + MLA_V)
    k_a = jnp.concatenate([kv[..., :MLA_NOPE], jnp.broadcast_to(kpe_all, (b, s_len, MLA_HEADS, MLA_ROPE))], axis=-1)
    v_a = kv[..., MLA_NOPE:]
    q_full = jnp.concatenate([q_nope, q_pe], axis=-1)[:, :, :, None, :]
    o_a = _sweep_query_blocks(lambda qb: _softmax_attn(qb, k_a, v_a, MLA_SCALE), (q_full,)).reshape(b, t, MLA_WIDTH)

    dq = dq.reshape(b, t, DIFF_HEADS * 2, DIFF_HEAD_DIM)
    dk = dk.reshape(b, t, DIFF_HEADS * 2, DIFF_HEAD_DIM)
    dv = dv.reshape(b, t, DIFF_HEADS, DIFF_V)
    if latent:
        dq = _apply_rope(dq, *rope['diff'])
        dk_all = jnp.concatenate([ctx['diff_k'], _apply_rope(dk, *rope['diff']).reshape(b, t, DIFF_HEADS, 2, DIFF_HEAD_DIM)], axis=1)
        dv_all = jnp.concatenate([ctx['diff_v'], dv], axis=1)
    else:
        dk_all, dv_all = dk.reshape(b, t, DIFF_HEADS, 2, DIFF_HEAD_DIM), dv
    dq = dq.reshape(b, t, DIFF_HEADS, 2, DIFF_HEAD_DIM)
    f32 = jnp.float32
    lam = (jnp.exp(jnp.sum(p['diff_lq1'].astype(f32) * p['diff_lk1'].astype(f32)))
           - jnp.exp(jnp.sum(p['diff_lq2'].astype(f32) * p['diff_lk2'].astype(f32))) + lam_init)

    def diff_block(qb):
        s = jnp.einsum('bqhmd,bshmd->bhmqs', qb, dk_all).astype(f32) * DIFF_SCALE
        pr = jax.nn.softmax(s, axis=-1)
        w = pr[:, :, 0] - lam * pr[:, :, 1]
        return jnp.einsum('bhqs,bshe->bqhe', w.astype(dv_all.dtype), dv_all)

    o_b = _sweep_query_blocks(diff_block, (dq,))
    o_b = (_rms_norm(o_b, p['diff_subln']) * (1.0 - lam_init)).reshape(b, t, DIFF_WIDTH)

    gq = _rms_norm(gq.reshape(b, t, GQA_HEADS, GQA_HEAD_DIM), p['gqa_q_norm'])
    gk = _rms_norm(gk.reshape(b, t, GQA_KV_HEADS, GQA_HEAD_DIM), p['gqa_k_norm'])
    gv = gv.reshape(b, t, GQA_KV_HEADS, GQA_HEAD_DIM)
    if latent:
        gq = _apply_rope(gq, *rope['gqa'])
        gk_all = jnp.concatenate([ctx['gqa_k'], _apply_rope(gk, *rope['gqa'])], axis=1)
        gv_all = jnp.concatenate([ctx['gqa_v'], gv], axis=1)
    else:
        gk_all, gv_all = gk, gv
    gq5 = gq.reshape(b, t, GQA_KV_HEADS, GQA_GROUP, GQA_HEAD_DIM)
    o_c = _sweep_query_blocks(lambda qb: _softmax_attn(qb, gk_all, gv_all, GQA_SCALE), (gq5,)).reshape(b, t, GQA_WIDTH)

    uf = u.astype(f32)
    ug = uf.reshape(b, t, SSM_GROUPS, SSM_GROUP)
    if latent:
        st = ctx['ssm'].astype(f32)
        s0_f = lax.complex(st[:, 0, ..., 0], st[:, 0, ..., 1])
        s0_b = lax.complex(st[:, 1, ..., 0], st[:, 1, ..., 1])
    else:
        s0_f = s0_b = None
    y_f, hf_last = _ssm_direction(ug, s0_f, p['ssm_a_re'][0], p['ssm_a_im'][0], p['ssm_log_dt'][0],
                                  p['ssm_b_re'][0], p['ssm_b_im'][0], p['ssm_c_re'][0], p['ssm_c_im'][0])
    y_b, hb_last = _ssm_direction(jnp.flip(ug, axis=1), s0_b, p['ssm_a_re'][1], p['ssm_a_im'][1], p['ssm_log_dt'][1],
                                  p['ssm_b_re'][1], p['ssm_b_im'][1], p['ssm_c_re'][1], p['ssm_c_im'][1])
    y = (y_f + jnp.flip(y_b, axis=1)).reshape(b, t, SSM_WIDTH) + p['ssm_d'].astype(f32) * uf
    y = jax.nn.gelu(y).astype(h.dtype)
    o_d = y * jax.nn.sigmoid(y @ p['ssm_glu_w'] + p['ssm_glu_b'])

    br = jnp.stack([o_a, o_b, o_c, o_d], axis=2) * jax.nn.silu(gate_in.reshape(b, t, N_BRANCH, BRANCH_WIDTH))
    br = jnp.einsum('btnw,nwd->btnd', br, p['w_branch_out'])
    merged = jnp.sum(jax.nn.sigmoid(merge_in.reshape(b, t, N_BRANCH, D_MODEL)) * br, axis=2)
    out = merged @ p['w_out']
    if latent:
        return out, None
    ssm_state = jnp.stack([hf_last, hb_last], axis=1)
    side = (c_kv, k_pe[:, :, 0, :], dk_all, dv_all, gk_all, gv_all,
            jnp.stack([jnp.real(ssm_state), jnp.imag(ssm_state)], axis=-1))
    return out, side


def setup_inputs(seed: int = 0) -> dict:
    key = jax.random.key(seed)
    ks = iter(jax.random.split(key, 64))
    f32 = jnp.float32

    def nrm(shape, s=1.0):
        return jax.random.normal(next(ks), shape, f32) * s

    hw = (DEPTH, 2, SSM_GROUPS, SSM_STATE)
    return {
        'x_prompt': nrm((BATCH, SEQ, D_MODEL)),
        'x_sample': nrm((DEC_BATCH, DEC_SEQ, D_MODEL)),
        'cache_mla_ckv': nrm((DEC_BATCH, DEPTH, PAST_LEN, MLA_KV_LORA)),
        'cache_mla_krope': nrm((DEC_BATCH, DEPTH, PAST_LEN, MLA_ROPE)),
        'cache_diff_k': nrm((DEC_BATCH, DEPTH, PAST_LEN, DIFF_HEADS, 2, DIFF_HEAD_DIM)),
        'cache_diff_v': nrm((DEC_BATCH, DEPTH, PAST_LEN, DIFF_HEADS, DIFF_V)),
        'cache_gqa_k': nrm((DEC_BATCH, DEPTH, PAST_LEN, GQA_KV_HEADS, GQA_HEAD_DIM)),
        'cache_gqa_v': nrm((DEC_BATCH, DEPTH, PAST_LEN, GQA_KV_HEADS, GQA_HEAD_DIM)),
        'state_ssm': nrm((DEC_BATCH, DEPTH, 2, SSM_GROUPS, SSM_STATE, 2), 0.5),
        'c': nrm((DEC_BATCH, D_MODEL)),
        'c_ctx': nrm((D_MODEL,)),
        'norm_g': 1.0 + nrm((DEPTH, D_MODEL), 0.01),
        'w_mod': nrm((DEPTH, D_MODEL, 3 * D_MODEL), D_MODEL ** -0.5),
        'b_mod': nrm((DEPTH, 3 * D_MODEL), 0.01),
        'w_in': nrm((DEPTH, D_MODEL, D_IN), D_MODEL ** -0.5),
        'mla_q_norm': 1.0 + nrm((DEPTH, MLA_Q_LORA), 0.01),
        'w_mla_q_b': nrm((DEPTH, MLA_Q_LORA, MLA_HEADS * (MLA_NOPE + MLA_ROPE)), MLA_Q_LORA ** -0.5),
        'mla_kv_norm': 1.0 + nrm((DEPTH, MLA_KV_LORA), 0.01),
        'w_mla_kv_b': nrm((DEPTH, MLA_KV_LORA, MLA_HEADS * (MLA_NOPE + MLA_V)), MLA_KV_LORA ** -0.5),
        'diff_lq1': nrm((DEPTH, DIFF_HEAD_DIM), 0.1),
        'diff_lk1': nrm((DEPTH, DIFF_HEAD_DIM), 0.1),
        'diff_lq2': nrm((DEPTH, DIFF_HEAD_DIM), 0.1),
        'diff_lk2': nrm((DEPTH, DIFF_HEAD_DIM), 0.1),
        'diff_subln': 1.0 + nrm((DEPTH, DIFF_V), 0.01),
        'gqa_q_norm': 1.0 + nrm((DEPTH, GQA_HEAD_DIM), 0.01),
        'gqa_k_norm': 1.0 + nrm((DEPTH, GQA_HEAD_DIM), 0.01),
        'ssm_a_re': -0.5 + nrm(hw, 0.01),
        'ssm_a_im': jnp.broadcast_to(math.pi * jnp.arange(SSM_STATE, dtype=f32), hw) + nrm(hw, 0.01),
        'ssm_log_dt': jax.random.uniform(next(ks), (DEPTH, 2, SSM_GROUPS), f32,
                                         minval=math.log(SSM_DT_MIN), maxval=math.log(SSM_DT_MAX)),
        'ssm_b_re': nrm((DEPTH, 2, SSM_GROUPS, SSM_STATE, SSM_GROUP), (2 * SSM_GROUP) ** -0.5),
        'ssm_b_im': nrm((DEPTH, 2, SSM_GROUPS, SSM_STATE, SSM_GROUP), (2 * SSM_GROUP) ** -0.5),
        'ssm_c_re': nrm((DEPTH, 2, SSM_GROUPS, SSM_GROUP, SSM_STATE), (2 * SSM_STATE) ** -0.5),
        'ssm_c_im': nrm((DEPTH, 2, SSM_GROUPS, SSM_GROUP, SSM_STATE), (2 * SSM_STATE) ** -0.5),
        'ssm_d': 1.0 + nrm((DEPTH, SSM_WIDTH), 0.1),
        'ssm_glu_w': nrm((DEPTH, SSM_WIDTH, SSM_WIDTH), SSM_WIDTH ** -0.5),
        'ssm_glu_b': nrm((DEPTH, SSM_WIDTH), 0.01),
        'w_branch_out': nrm((DEPTH, N_BRANCH, BRANCH_WIDTH, D_MODEL), BRANCH_WIDTH ** -0.5),
        'w_out': nrm((DEPTH, D_MODEL, D_MODEL), D_MODEL ** -0.5),
        'final_norm': 1.0 + nrm((D_MODEL,), 0.01),
    }


def reference(x_prompt, x_sample, cache_mla_ckv, cache_mla_krope, cache_diff_k, cache_diff_v,
              cache_gqa_k, cache_gqa_v, state_ssm, c, c_ctx, norm_g, w_mod, b_mod, w_in,
              mla_q_norm, w_mla_q_b, mla_kv_norm, w_mla_kv_b, diff_lq1, diff_lk1, diff_lq2, diff_lk2,
              diff_subln, gqa_q_norm, gqa_k_norm, ssm_a_re, ssm_a_im, ssm_log_dt, ssm_b_re, ssm_b_im,
              ssm_c_re, ssm_c_im, ssm_d, ssm_glu_w, ssm_glu_b, w_branch_out, w_out, final_norm):
    n_lat = x_sample.shape[1]
    rope = {'mla': _rope_tables(n_lat, MLA_ROPE),
            'diff': _rope_tables(n_lat, DIFF_HEAD_DIM),
            'gqa': _rope_tables(n_lat, GQA_HEAD_DIM)}
    xp, xs = x_prompt, x_sample
    sides = []
    for l in range(DEPTH):
        p = dict(w_in=w_in[l], mla_q_norm=mla_q_norm[l], w_mla_q_b=w_mla_q_b[l], mla_kv_norm=mla_kv_norm[l],
                 w_mla_kv_b=w_mla_kv_b[l], diff_lq1=diff_lq1[l], diff_lk1=diff_lk1[l], diff_lq2=diff_lq2[l],
                 diff_lk2=diff_lk2[l], diff_subln=diff_subln[l], gqa_q_norm=gqa_q_norm[l],
                 gqa_k_norm=gqa_k_norm[l], ssm_a_re=ssm_a_re[l], ssm_a_im=ssm_a_im[l],
                 ssm_log_dt=ssm_log_dt[l], ssm_b_re=ssm_b_re[l], ssm_b_im=ssm_b_im[l],
                 ssm_c_re=ssm_c_re[l], ssm_c_im=ssm_c_im[l], ssm_d=ssm_d[l], ssm_glu_w=ssm_glu_w[l],
                 ssm_glu_b=ssm_glu_b[l], w_branch_out=w_branch_out[l], w_out=w_out[l])
        lam_init = 0.8 - 0.6 * math.exp(-0.3 * l)
        sh_c, sc_c, ga_c = jnp.split(jax.nn.silu(c_ctx) @ w_mod[l] + b_mod[l], 3, axis=-1)
        out_p, side = _mixer(_modulate(xp, norm_g[l], sh_c, sc_c), p, lam_init, None, None)
        xp = xp + ga_c * out_p
        sides.append(side)
        sh_s, sc_s, ga_s = jnp.split((jax.nn.silu(c) @ w_mod[l] + b_mod[l])[:, None, :], 3, axis=-1)
        ctx = {'mla_ckv': cache_mla_ckv[:, l], 'mla_krope': cache_mla_krope[:, l],
               'diff_k': cache_diff_k[:, l], 'diff_v': cache_diff_v[:, l],
               'gqa_k': cache_gqa_k[:, l], 'gqa_v': cache_gqa_v[:, l], 'ssm': state_ssm[:, l]}
        out_s, _ = _mixer(_modulate(xs, norm_g[l], sh_s, sc_s), p, lam_init, rope, ctx)
        xs = xs + ga_s * out_s
    y_prompt = _rms_norm(xp, final_norm)
    y_sample = _rms_norm(xs, final_norm)
    new_mla_ckv = jnp.stack([s[0] for s in sides], axis=1)
    new_mla_krope = jnp.stack([s[1] for s in sides], axis=1)
    new_diff_k = jnp.stack([s[2] for s in sides], axis=1)
    new_diff_v = jnp.stack([s[3] for s in sides], axis=1)
    new_gqa_k = jnp.stack([s[4] for s in sides], axis=1)
    new_gqa_v = jnp.stack([s[5] for s in sides], axis=1)
    new_state_ssm = jnp.stack([s[6] for s in sides], axis=1)
    return (y_prompt, y_sample, new_mla_ckv, new_mla_krope, new_diff_k, new_diff_v, new_gqa_k, new_gqa_v, new_state_ssm)
```

```python
import functools
import math

import jax
import jax.numpy as jnp
from jax import lax
from jax.experimental import pallas as pl
from jax.experimental.pallas import tpu as pltpu

F32 = jnp.float32
BF16 = jnp.bfloat16

D_MODEL = 1024
DEPTH = 2
GRID_W = 64
ROPE_THETA = 10000.0
EPS = 1e-6

MLA_HEADS = 8
MLA_NOPE = 64
MLA_ROPE = 32
MLA_V = 64
MLA_Q_LORA = 256
MLA_KV_LORA = 128
MLA_SCALE = (MLA_NOPE + MLA_ROPE) ** -0.5
MLA_HEAD_PAD = 128
DIFF_HEADS = 4
DIFF_HEAD_DIM = 64
DIFF_V = 2 * DIFF_HEAD_DIM
DIFF_SCALE = DIFF_HEAD_DIM ** -0.5
GQA_HEADS = 8
GQA_KV_HEADS = 2
GQA_HEAD_DIM = 64
GQA_GROUP = GQA_HEADS // GQA_KV_HEADS
GQA_SCALE = GQA_HEAD_DIM ** -0.5
SSM_WIDTH = 512
SSM_GROUP = 16
SSM_GROUPS = SSM_WIDTH // SSM_GROUP
SSM_STATE = 64
N_BRANCH = 4
BRANCH_WIDTH = 512

IN_SPLITS = (MLA_Q_LORA, MLA_KV_LORA, MLA_ROPE, 512, 512, 512, 512, 128, 128, SSM_WIDTH,
             N_BRANCH * BRANCH_WIDTH, N_BRANCH * D_MODEL)

V7X_LANES = 128
V7X_SUBLANES = 8
V7X_VMEM_LIMIT_BYTES = 56 * 1024 * 1024

_A_QA = (0, 256)
_A_KVA = (256, 384)
_A_KPE = (384, 512)
_A_DQ = (512, 1024)
_A_DK = (1024, 1536)
_A_DV = (1536, 2048)
_A_GQ = (2048, 2560)
_A_GK = (2560, 2688)
_A_GV = (2688, 2816)
_A_U = (2816, 3328)
_A_WIDTH = 3328

SSM_GBLOCK = 8
SSM_NBLK = SSM_GROUPS // SSM_GBLOCK
SSM_SBLK = SSM_GBLOCK * SSM_STATE
SSM_UBLK = SSM_GBLOCK * SSM_GROUP
SSM_SEQ = 256


def _params(n_axes, vmem=V7X_VMEM_LIMIT_BYTES):
    return pltpu.CompilerParams(dimension_semantics=("arbitrary",) * n_axes, vmem_limit_bytes=vmem)


def _const_spec(shape):
    zeros = (0,) * len(shape)
    return pl.BlockSpec(shape, lambda *_: zeros, pipeline_mode=pl.Buffered(1))


def _mod_spec(n_rows, tm, seq):
    if n_rows == 1:
        return pl.BlockSpec((1, 1, D_MODEL), lambda i: (0, 0, 0))
    return pl.BlockSpec((1, 1, D_MODEL), lambda i: (i * tm // seq, 0, 0))


def _dot(a, b):
    return jnp.dot(a, b, preferred_element_type=F32)


def _dot_nt(a, b):
    return lax.dot_general(a, b, (((1,), (1,)), ((), ())), preferred_element_type=F32)


def _rms(x):
    return x * lax.rsqrt(jnp.mean(x * x, axis=-1, keepdims=True) + EPS)


def _sigmoid(x):
    return 1.0 / (1.0 + jnp.exp(-x))


def _gelu_tanh(x):
    return 0.5 * x * (1.0 + jnp.tanh(math.sqrt(2.0 / math.pi) * (x + 0.044715 * (x * x * x))))


def _group_mean_sq(x, gmat):
    x2 = x * x
    hi = x2.astype(BF16)
    lo = (x2 - hi.astype(F32)).astype(BF16)
    return _dot(hi, gmat) + _dot(lo, gmat)


def _rope_chunk(x, cos, sin, half):
    lane = lax.broadcasted_iota(jnp.int32, x.shape, 1)
    first = (lane % (2 * half)) < half
    swapped = jnp.where(first, pltpu.roll(x, V7X_LANES - half, 1), pltpu.roll(x, half, 1))
    return x * cos + swapped * sin


def _mod_kernel(c_ref, w_ref, b_ref, o_ref):
    c = c_ref[...]
    a = (c * _sigmoid(c)).astype(BF16)
    o_ref[0] = _dot(a, w_ref[0].astype(BF16)) + b_ref[0]


def _modulation(cvec, w_mod, b_mod):
    tn = 512
    return pl.pallas_call(
        _mod_kernel,
        grid=(DEPTH, 3 * D_MODEL // tn),
        in_specs=[pl.BlockSpec((V7X_SUBLANES, D_MODEL), lambda l, j: (0, 0)),
                  pl.BlockSpec((1, D_MODEL, tn), lambda l, j: (l, 0, j)),
                  pl.BlockSpec((1, 1, tn), lambda l, j: (l, 0, j))],
        out_specs=pl.BlockSpec((1, V7X_SUBLANES, tn), lambda l, j: (l, 0, j)),
        out_shape=jax.ShapeDtypeStruct((DEPTH, V7X_SUBLANES, 3 * D_MODEL), F32),
        compiler_params=_params(2),
        name="modulation",
    )(cvec, w_mod, b_mod.reshape(DEPTH, 1, 3 * D_MODEL))


def _modulated(x, g_ref, sh_ref, sc_ref):
    return (_rms(x) * g_ref[...]) * (1.0 + sc_ref[0]) + sh_ref[0]


def _proj_kernel(latent, *refs):
    (x_ref, sh_ref, sc_ref, g_ref, wa_ref, qn_ref, wqb_ref, kvn_ref, wk_ref, wv_ref,
     gqn_ref, gkn_ref, gm_ref) = refs[:13]
    pos = 13
    if latent:
        c_mla, s_mla, c_64, s_64 = refs[pos:pos + 4]
        pos += 4
    (q_o, k_o, v_o, dq_o, dk_o, dv_o, gq_o, gk_o, gv_o, u_o) = refs[pos:pos + 10]
    pos += 10
    if not latent:
        ckv_f, kpe_f, dk_f, dv_f, gk_f, gv_f = refs[pos:pos + 6]

    h = _modulated(x_ref[...], g_ref, sh_ref, sc_ref).astype(BF16)

    def proj(cols):
        return _dot(h, wa_ref[:, cols[0]:cols[1]])

    def rope_mla(x):
        return _rope_chunk(x, c_mla[...], s_mla[...], MLA_ROPE // 2) if latent else x

    def rope_64(x):
        if not latent:
            return x
        c, s = c_64[...], s_64[...]
        chunks = [_rope_chunk(x[:, i:i + V7X_LANES], c, s, DIFF_HEAD_DIM // 2)
                  for i in range(0, x.shape[1], V7X_LANES)]
        return chunks[0] if len(chunks) == 1 else jnp.concatenate(chunks, axis=1)

    qn = (_rms(proj(_A_QA)) * qn_ref[...]).astype(BF16)
    q = _dot(qn, wqb_ref[...])
    for hd in range(MLA_HEADS):
        sl = slice(hd * MLA_HEAD_PAD, (hd + 1) * MLA_HEAD_PAD)
        q_o[:, sl] = (rope_mla(q[:, sl]) * MLA_SCALE).astype(BF16)
    ckv = _rms(proj(_A_KVA)) * kvn_ref[...]
    kpe = proj(_A_KPE)
    if not latent:
        ckv_f[...] = ckv
        kpe_f[...] = kpe[:, MLA_NOPE:MLA_NOPE + MLA_ROPE]
    kpe = rope_mla(kpe)
    ckv_b = ckv.astype(BF16)
    kn = _dot(ckv_b, wk_ref[...])
    for hd in range(MLA_HEADS):
        sl = slice(hd * MLA_HEAD_PAD, (hd + 1) * MLA_HEAD_PAD)
        k_o[:, sl] = (kn[:, sl] + kpe).astype(BF16)
    v_o[...] = _dot(ckv_b, wv_ref[...]).astype(BF16)

    dq_o[...] = (rope_64(proj(_A_DQ)) * DIFF_SCALE).astype(BF16)
    dk = rope_64(proj(_A_DK))
    dv = proj(_A_DV)
    dk_o[...] = dk.astype(BF16)
    dv_o[...] = dv.astype(BF16)

    gq = proj(_A_GQ)
    gq = gq * lax.rsqrt(_group_mean_sq(gq, gm_ref[...]) + EPS) * gqn_ref[...]
    gq_o[...] = (rope_64(gq) * GQA_SCALE).astype(BF16)
    gk = proj(_A_GK)
    kw = GQA_KV_HEADS * GQA_HEAD_DIM
    gk = gk * lax.rsqrt(_group_mean_sq(gk, gm_ref[:kw, :kw]) + EPS) * gkn_ref[...]
    gv = proj(_A_GV)
    if not latent:
        dk_f[...] = dk
        dv_f[...] = dv
        gk_f[...] = gk
        gv_f[...] = gv
    gk_o[...] = rope_64(gk).astype(BF16)
    gv_o[...] = gv.astype(BF16)

    u_o[...] = proj(_A_U)


def _proj(x, shift, scale, seq, lw, rope, latent):
    n = x.shape[0]
    tm = 512
    per_seq = max(seq // tm, 1)
    row = lambda i: (i, 0)
    mod_spec = _mod_spec(shift.shape[0], tm, seq)
    kw = GQA_KV_HEADS * GQA_HEAD_DIM
    in_specs = [pl.BlockSpec((tm, D_MODEL), row), mod_spec, mod_spec,
                _const_spec((1, D_MODEL)), _const_spec((D_MODEL, _A_WIDTH)),
                _const_spec((1, MLA_Q_LORA)), _const_spec((MLA_Q_LORA, MLA_HEADS * MLA_HEAD_PAD)),
                _const_spec((1, MLA_KV_LORA)), _const_spec((MLA_KV_LORA, MLA_HEADS * MLA_HEAD_PAD)),
                _const_spec((MLA_KV_LORA, MLA_HEADS * MLA_V)),
                _const_spec((1, 512)), _const_spec((1, kw)), _const_spec((512, 512))]
    args = [x, shift, scale, lw["norm_g"], lw["wa"], lw["qn"], lw["wqb"], lw["kvn"], lw["wk"], lw["wv"],
            lw["gqn"], lw["gkn"], lw["gmat"]]
    if latent:
        tab = pl.BlockSpec((tm, V7X_LANES), lambda i: (i % per_seq, 0))
        in_specs += [tab] * 4
        args += list(rope)
    widths = [(1024, BF16), (1024, BF16), (512, BF16), (512, BF16), (512, BF16), (512, BF16),
              (512, BF16), (kw, BF16), (kw, BF16), (512, F32)]
    if not latent:
        widths += [(MLA_KV_LORA, F32), (MLA_ROPE, F32), (512, F32), (512, F32), (kw, F32), (kw, F32)]
    out_specs = [pl.BlockSpec((tm, w), row) for w, _ in widths]
    out_shape = [jax.ShapeDtypeStruct((n, w), dt) for w, dt in widths]
    return pl.pallas_call(
        functools.partial(_proj_kernel, latent),
        grid=(n // tm,),
        in_specs=in_specs, out_specs=out_specs, out_shape=out_shape,
        compiler_params=_params(1),
        name="proj_latent" if latent else "proj_context",
    )(*args)


def _ctx_kv_kernel(ckv_ref, kpe_ref, wk_ref, wv_ref, k_o, v_o):
    ckv_b = ckv_ref[...].astype(BF16)
    kn = _dot(ckv_b, wk_ref[...])
    kpe = kpe_ref[...]
    for hd in range(MLA_HEADS):
        sl = slice(hd * MLA_HEAD_PAD, (hd + 1) * MLA_HEAD_PAD)
        k_o[:, sl] = (kn[:, sl] + kpe).astype(BF16)
    v_o[...] = _dot(ckv_b, wv_ref[...]).astype(BF16)


def _ctx_kv(ckv, kpe_pad, lw):
    n = ckv.shape[0]
    tm = 512
    row = lambda i: (i, 0)
    return pl.pallas_call(
        _ctx_kv_kernel,
        grid=(n // tm,),
        in_specs=[pl.BlockSpec((tm, MLA_KV_LORA), row), pl.BlockSpec((tm, MLA_HEAD_PAD), row),
                  _const_spec((MLA_KV_LORA, MLA_HEADS * MLA_HEAD_PAD)),
                  _const_spec((MLA_KV_LORA, MLA_HEADS * MLA_V))],
        out_specs=[pl.BlockSpec((tm, MLA_HEADS * MLA_HEAD_PAD), row), pl.BlockSpec((tm, MLA_HEADS * MLA_V), row)],
        out_shape=[jax.ShapeDtypeStruct((n, MLA_HEADS * MLA_HEAD_PAD), BF16),
                   jax.ShapeDtypeStruct((n, MLA_HEADS * MLA_V), BF16)],
        compiler_params=_params(1),
        name="ctx_kv",
    )(ckv, kpe_pad, lw["wk"], lw["wv"])


def _softmax_parts(s):
    p = jnp.exp(s - jnp.max(s, axis=-1, keepdims=True))
    return p, 1.0 / jnp.sum(p, axis=-1, keepdims=True)


def _attn_heads_kernel(n_heads, q_width, kv_group, q_ref, k_ref, v_ref, o_ref):
    outs = []
    for hd in range(n_heads):
        g = hd // kv_group
        q = q_ref[0, :, hd * q_width:(hd + 1) * q_width]
        k = k_ref[0, :, g * q_width:(g + 1) * q_width]
        v = v_ref[0, :, g * 64:(g + 1) * 64]
        p, inv = _softmax_parts(_dot_nt(q, k))
        outs.append(_dot(p.astype(BF16), v) * inv)
        if hd % 2 == 1:
            o_ref[0, :, (hd - 1) * 64:(hd + 1) * 64] = jnp.concatenate(outs, axis=1).astype(BF16)
            outs = []


def _diff_attn_kernel(lam_init, q_ref, k_ref, v_ref, lp_ref, sub_ref, o_ref):
    lp = lp_ref[...]
    lam = (jnp.exp(jnp.sum(lp[0:1] * lp[1:2], axis=-1, keepdims=True))
           - jnp.exp(jnp.sum(lp[2:3] * lp[3:4], axis=-1, keepdims=True)) + lam_init)
    d = DIFF_HEAD_DIM
    for hd in range(DIFF_HEADS):
        c1, c2 = 2 * hd * d, (2 * hd + 1) * d
        p1, inv1 = _softmax_parts(_dot_nt(q_ref[0, :, c1:c1 + d], k_ref[0, :, c1:c1 + d]))
        p2, inv2 = _softmax_parts(_dot_nt(q_ref[0, :, c2:c2 + d], k_ref[0, :, c2:c2 + d]))
        w = p1 * inv1 - p2 * (lam * inv2)
        o = _dot(w.astype(BF16), v_ref[0, :, hd * DIFF_V:(hd + 1) * DIFF_V])
        o = _rms(o) * sub_ref[...] * (1.0 - lam_init)
        o_ref[0, :, hd * DIFF_V:(hd + 1) * DIFF_V] = o.astype(BF16)


def _attention(kernel, name, q, k, v, extra=()):
    b, t, wq = q.shape
    s = k.shape[1]
    tq = 256
    in_specs = [pl.BlockSpec((1, tq, wq), lambda i, j: (i, j, 0)),
                pl.BlockSpec((1, s, k.shape[2]), lambda i, j: (i, 0, 0)),
                pl.BlockSpec((1, s, v.shape[2]), lambda i, j: (i, 0, 0))]
    in_specs += [_const_spec(e.shape) for e in extra]
    return pl.pallas_call(
        kernel,
        grid=(b, t // tq),
        in_specs=in_specs,
        out_specs=pl.BlockSpec((1, tq, 512), lambda i, j: (i, j, 0)),
        out_shape=jax.ShapeDtypeStruct((b, t, 512), BF16),
        compiler_params=_params(2),
        name=name,
    )(q, k, v, *extra)


def _ssm_kernel(chunked, want_final, *refs):
    u_ref, bm_ref, ab_ref, cm_ref, d_ref = refs[:5]
    pos = 5
    if chunked:
        s0_ref = refs[pos]
        pos += 1
    y_ref = refs[pos]
    pos += 1
    if want_final:
        fin_ref = refs[pos]
        pos += 1
    hs_ref = refs[pos]

    seq, rows, sb = SSM_SEQ, V7X_SUBLANES, SSM_SBLK
    u2 = u_ref[...].reshape(seq * rows, SSM_UBLK)
    u2b = u2.astype(BF16)
    for dr in range(2):
        hs_ref[dr] = _dot(u2b, bm_ref[dr, 0])

    a_parts = []
    for dr in range(2):
        a = ab_ref[dr, 0]
        a_parts.append((jnp.broadcast_to(a[:, :sb], (rows, sb)), jnp.broadcast_to(a[:, sb:], (rows, sb))))

    def row_slices(i):
        return pl.ds(pl.multiple_of(i * rows, rows), rows), pl.ds(pl.multiple_of((seq - 1 - i) * rows, rows), rows)

    def scan_step(i, carry):
        new = []
        for dr, sl in enumerate(row_slices(i)):
            ar, ai = a_parts[dr]
            hr, hi = carry[2 * dr], carry[2 * dr + 1]
            b = hs_ref[dr, sl, :]
            nr = ar * hr - ai * hi + b[:, :sb]
            ni = ar * hi + ai * hr + b[:, sb:]
            hs_ref[dr, sl, :sb] = nr
            hs_ref[dr, sl, sb:] = ni
            new += [nr, ni]
        return tuple(new)

    zero = jnp.zeros((rows, sb), F32)
    fin = lax.fori_loop(0, seq, scan_step, (zero, zero, zero, zero), unroll=2)

    if want_final:
        for dr in range(2):
            fin_ref[dr, 0, :, :sb] = fin[2 * dr]
            fin_ref[dr, 0, :, sb:] = fin[2 * dr + 1]

    if chunked:
        entry = []
        for dr in range(2):
            ar, ai = a_parts[dr]
            pr, pi = ar[0:1], ai[0:1]
            for _ in range(int(math.log2(seq))):
                pr, pi = pr * pr - pi * pi, 2.0 * pr * pi
            fr, fi = fin[2 * dr], fin[2 * dr + 1]
            s0 = s0_ref[dr, 0, 0]
            er, ei = s0[:, :sb], s0[:, sb:]
            order = range(rows) if dr == 0 else range(rows - 1, -1, -1)
            rows_r, rows_i = [None] * rows, [None] * rows
            for c in order:
                rows_r[c], rows_i[c] = er, ei
                er, ei = (pr * er - pi * ei + fr[c:c + 1], pr * ei + pi * er + fi[c:c + 1])
            entry += [jnp.concatenate(rows_r, axis=0), jnp.concatenate(rows_i, axis=0)]

        def fix_step(i, carry):
            new = []
            for dr, sl in enumerate(row_slices(i)):
                ar, ai = a_parts[dr]
                zr, zi = carry[2 * dr], carry[2 * dr + 1]
                nr = ar * zr - ai * zi
                ni = ar * zi + ai * zr
                hs_ref[dr, sl, :sb] += nr
                hs_ref[dr, sl, sb:] += ni
                new += [nr, ni]
            return tuple(new)

        lax.fori_loop(0, seq, fix_step, tuple(entry), unroll=2)

    y = _dot(hs_ref[0].astype(BF16), cm_ref[0, 0]) + _dot(hs_ref[1].astype(BF16), cm_ref[1, 0])
    y = _gelu_tanh(y + d_ref[...] * u2)
    y_ref[...] = y.reshape(seq, rows, SSM_UBLK)


def _ssm(u_t, lw, s0, want_final):
    seq, r, _ = u_t.shape
    chunked = s0 is not None
    rows = V7X_SUBLANES
    in_specs = [pl.BlockSpec((seq, rows, SSM_UBLK), lambda i, j: (0, i, j)),
                pl.BlockSpec((2, 1, SSM_UBLK, 2 * SSM_SBLK), lambda i, j: (0, j, 0, 0)),
                pl.BlockSpec((2, 1, 1, 2 * SSM_SBLK), lambda i, j: (0, j, 0, 0)),
                pl.BlockSpec((2, 1, 2 * SSM_SBLK, SSM_UBLK), lambda i, j: (0, j, 0, 0)),
                pl.BlockSpec((1, SSM_UBLK), lambda i, j: (0, j))]
    args = [u_t, lw["ssm_b"], lw["ssm_a"], lw["ssm_c"], lw["ssm_d"]]
    if chunked:
        in_specs.append(pl.BlockSpec((2, 1, 1, 1, 2 * SSM_SBLK), lambda i, j: (0, i, j, 0, 0)))
        args.append(s0)
    out_specs = [pl.BlockSpec((seq, rows, SSM_UBLK), lambda i, j: (0, i, j))]
    out_shape = [jax.ShapeDtypeStruct((seq, r, SSM_WIDTH), F32)]
    if want_final:
        out_specs.append(pl.BlockSpec((2, 1, rows, 2 * SSM_SBLK), lambda i, j: (0, j, i, 0)))
        out_shape.append(jax.ShapeDtypeStruct((2, SSM_NBLK, r, 2 * SSM_SBLK), F32))
    return pl.pallas_call(
        functools.partial(_ssm_kernel, chunked, want_final),
        grid=(r // rows, SSM_NBLK),
        in_specs=in_specs, out_specs=out_specs, out_shape=out_shape,
        scratch_shapes=[pltpu.VMEM((2, seq * rows, 2 * SSM_SBLK), F32)],
        compiler_params=_params(2),
        name="ssm_latent" if chunked else "ssm_context",
    )(*args)


def _merge_kernel(last, x_ref, sh_ref, sc_ref, ga_ref, g_ref, oa_ref, ob_ref, oc_ref, y_ref,
                  wg_ref, wm_ref, wglu_ref, bglu_ref, wbr_ref, wout_ref, fn_ref, o_ref):
    x = x_ref[...]
    h = _modulated(x, g_ref, sh_ref, sc_ref).astype(BF16)
    y = y_ref[...]
    o_d = y * _sigmoid(_dot(y.astype(BF16), wglu_ref[...]) + bglu_ref[...])
    branches = (oa_ref[...].astype(F32), ob_ref[...].astype(F32), oc_ref[...].astype(F32), o_d)
    acc = None
    for n, o in enumerate(branches):
        gate = _dot(h, wg_ref[:, n * BRANCH_WIDTH:(n + 1) * BRANCH_WIDTH])
        br = _dot((o * (gate * _sigmoid(gate))).astype(BF16), wbr_ref[n])
        term = _sigmoid(_dot(h, wm_ref[:, n * D_MODEL:(n + 1) * D_MODEL])) * br
        acc = term if acc is None else acc + term
    xn = x + ga_ref[0] * _dot(acc.astype(BF16), wout_ref[...])
    if last:
        xn = _rms(xn) * fn_ref[...]
    o_ref[...] = xn


def _merge(x, shift, scale, gate, seq, oa, ob, oc, y, lw, final_norm, last):
    n = x.shape[0]
    tm = 256
    row = lambda i: (i, 0)
    mod_spec = _mod_spec(shift.shape[0], tm, seq)
    in_specs = [pl.BlockSpec((tm, D_MODEL), row), mod_spec, mod_spec, mod_spec, _const_spec((1, D_MODEL)),
                pl.BlockSpec((tm, 512), row), pl.BlockSpec((tm, 512), row), pl.BlockSpec((tm, 512), row),
                pl.BlockSpec((tm, 512), row),
                _const_spec((D_MODEL, N_BRANCH * BRANCH_WIDTH)), _const_spec((D_MODEL, N_BRANCH * D_MODEL)),
                _const_spec((SSM_WIDTH, SSM_WIDTH)), _const_spec((1, SSM_WIDTH)),
                _const_spec((N_BRANCH, BRANCH_WIDTH, D_MODEL)), _const_spec((D_MODEL, D_MODEL)),
                _const_spec((1, D_MODEL))]
    return pl.pallas_call(
        functools.partial(_merge_kernel, last),
        grid=(n // tm,),
        in_specs=in_specs,
        out_specs=pl.BlockSpec((tm, D_MODEL), row),
        out_shape=jax.ShapeDtypeStruct((n, D_MODEL), F32),
        compiler_params=_params(1),
        name="merge",
    )(x, shift, scale, gate, lw["norm_g"], oa, ob, oc, y, lw["wg"], lw["wm"], lw["wglu"], lw["bglu"],
      lw["wbr"], lw["wout"], final_norm)


def _rope_tables(n_tok, rot_dim):
    rows = n_tok // GRID_W
    row = jnp.repeat(jnp.arange(rows, dtype=F32), GRID_W)
    col = jnp.tile(jnp.arange(GRID_W, dtype=F32), rows)
    quarter = rot_dim // 4
    inv = ROPE_THETA ** (-jnp.arange(quarter, dtype=F32) / quarter)
    ang = jnp.concatenate([row[:, None] * inv, col[:, None] * inv], axis=-1)
    return jnp.cos(ang), jnp.sin(ang)


def _rope_lane_tables(n_tok):
    c, s = _rope_tables(n_tok, MLA_ROPE)
    ones = jnp.ones((n_tok, MLA_NOPE), F32)
    pad = MLA_HEAD_PAD - MLA_NOPE - MLA_ROPE
    c_mla = jnp.concatenate([ones, c, c, jnp.ones((n_tok, pad), F32)], axis=1)
    s_mla = jnp.concatenate([0.0 * ones, -s, s, jnp.zeros((n_tok, pad), F32)], axis=1)
    c, s = _rope_tables(n_tok, DIFF_HEAD_DIM)
    c_64 = jnp.concatenate([c, c, c, c], axis=1)
    s_64 = jnp.concatenate([-s, s, -s, s], axis=1)
    return c_mla, s_mla, c_64, s_64


def _block_diag(blocks):
    g = blocks.shape[-3]
    eye = jnp.eye(g, dtype=blocks.dtype)
    out = jnp.einsum("...grc,gh->...grhc", blocks, eye)
    return out.reshape(blocks.shape[:-3] + (g * blocks.shape[-2], g * blocks.shape[-1]))


def _layer_weights(l, w_in, p):
    offs = [0]
    for w in IN_SPLITS:
        offs.append(offs[-1] + w)
    wl = w_in[l]
    col = lambda i: wl[:, offs[i]:offs[i + 1]]
    zeros = lambda w: jnp.zeros((D_MODEL, w), F32)
    kpe_blk = jnp.concatenate([zeros(MLA_NOPE), col(2), zeros(MLA_HEAD_PAD - MLA_NOPE - MLA_ROPE)], axis=1)
    wa = jnp.concatenate([col(0), col(1), kpe_blk, col(3), col(4), col(5), col(6), col(7), col(8), col(9)],
                         axis=1).astype(BF16)

    hq = MLA_NOPE + MLA_ROPE
    wqb = p["w_mla_q_b"][l].reshape(MLA_Q_LORA, MLA_HEADS, hq)
    wqb = jnp.pad(wqb, ((0, 0), (0, 0), (0, MLA_HEAD_PAD - hq))).reshape(MLA_Q_LORA, -1).astype(BF16)
    wkv = p["w_mla_kv_b"][l].reshape(MLA_KV_LORA, MLA_HEADS, MLA_NOPE + MLA_V)
    wk = jnp.pad(wkv[:, :, :MLA_NOPE], ((0, 0), (0, 0), (0, MLA_HEAD_PAD - MLA_NOPE)))
    wk = wk.reshape(MLA_KV_LORA, -1).astype(BF16)
    wv = wkv[:, :, MLA_NOPE:].reshape(MLA_KV_LORA, -1).astype(BF16)

    gmat = _block_diag(jnp.full((GQA_HEADS, GQA_HEAD_DIM, GQA_HEAD_DIM), 1.0 / GQA_HEAD_DIM, F32)).astype(BF16)

    lam = lax.complex(p["ssm_a_re"][l], p["ssm_a_im"][l])
    dt = jnp.exp(p["ssm_log_dt"][l])[..., None]
    abar = jnp.exp(lam * dt)
    bbar = ((abar - 1.0) / lam)[..., None] * lax.complex(p["ssm_b_re"][l], p["ssm_b_im"][l])
    blk = lambda a: a.reshape((2, SSM_NBLK, SSM_GBLOCK) + a.shape[2:])
    b_t = jnp.swapaxes(blk(bbar), -1, -2)
    ssm_b = jnp.concatenate([_block_diag(jnp.real(b_t)), _block_diag(jnp.imag(b_t))], axis=-1).astype(BF16)
    ab = blk(abar).reshape(2, SSM_NBLK, 1, SSM_SBLK)
    ssm_a = jnp.concatenate([jnp.real(ab), jnp.imag(ab)], axis=-1)
    c_re = jnp.swapaxes(blk(p["ssm_c_re"][l]), -1, -2)
    c_im = jnp.swapaxes(blk(p["ssm_c_im"][l]), -1, -2)
    ssm_c = jnp.concatenate([_block_diag(c_re), -_block_diag(c_im)], axis=-2).astype(BF16)

    return dict(
        norm_g=p["norm_g"][l][None], wa=wa, qn=p["mla_q_norm"][l][None], wqb=wqb,
        kvn=p["mla_kv_norm"][l][None], wk=wk, wv=wv,
        gqn=jnp.tile(p["gqa_q_norm"][l], GQA_HEADS)[None], gkn=jnp.tile(p["gqa_k_norm"][l], GQA_KV_HEADS)[None],
        gmat=gmat,
        lam_parts=jnp.stack([p["diff_lq1"][l], p["diff_lk1"][l], p["diff_lq2"][l], p["diff_lk2"][l]]),
        subln=p["diff_subln"][l][None],
        ssm_b=ssm_b, ssm_a=ssm_a, ssm_c=ssm_c, ssm_d=p["ssm_d"][l][None],
        wg=col(10).astype(BF16), wm=col(11).astype(BF16),
        wglu=p["ssm_glu_w"][l].astype(BF16), bglu=p["ssm_glu_b"][l][None],
        wbr=p["w_branch_out"][l].astype(BF16), wout=p["w_out"][l].astype(BF16),
    )


def _mixers(pr, b, t, lw, lam_init, ctx):
    q, k, v, dq, dk, dv, gq, gk, gv, u = pr[:10]
    three = lambda a: a.reshape(b, t, a.shape[-1])
    k, v, dk, dv, gk, gv = (three(a) for a in (k, v, dk, dv, gk, gv))
    if ctx is not None:
        cat = lambda c, a: jnp.concatenate([c, a], axis=1)
        k, v, dk, dv, gk, gv = (cat(c, a) for c, a in zip(ctx["kv"], (k, v, dk, dv, gk, gv)))
    o_a = _attention(functools.partial(_attn_heads_kernel, MLA_HEADS, MLA_HEAD_PAD, 1), "attn_mla",
                     three(q), k, v)
    o_b = _attention(functools.partial(_diff_attn_kernel, lam_init), "attn_diff",
                     three(dq), dk, dv, extra=(lw["lam_parts"], lw["subln"]))
    o_c = _attention(functools.partial(_attn_heads_kernel, GQA_HEADS, GQA_HEAD_DIM, GQA_GROUP), "attn_gqa",
                     three(gq), gk, gv)
    n = b * t
    if ctx is None:
        u_t = jnp.swapaxes(u.reshape(b, t, SSM_WIDTH), 0, 1)
        y_t, fin = _ssm(u_t, lw, None, True)
        y = jnp.swapaxes(y_t, 0, 1).reshape(n, SSM_WIDTH)
    else:
        chunks = t // SSM_SEQ
        u_t = jnp.transpose(u.reshape(b, chunks, SSM_SEQ, SSM_WIDTH), (2, 0, 1, 3)).reshape(SSM_SEQ, b * chunks, SSM_WIDTH)
        (y_t,) = _ssm(u_t, lw, ctx["s0"], False)
        y = jnp.transpose(y_t.reshape(SSM_SEQ, b, chunks, SSM_WIDTH), (1, 2, 0, 3)).reshape(n, SSM_WIDTH)
        fin = None
    flat = lambda a: a.reshape(n, 512)
    return flat(o_a), flat(o_b), flat(o_c), y, fin


def kernel(x_prompt, x_sample, cache_mla_ckv, cache_mla_krope, cache_diff_k, cache_diff_v, cache_gqa_k, cache_gqa_v, state_ssm, c, c_ctx, norm_g, w_mod, b_mod, w_in, mla_q_norm, w_mla_q_b, mla_kv_norm, w_mla_kv_b, diff_lq1, diff_lk1, diff_lq2, diff_lk2, diff_subln, gqa_q_norm, gqa_k_norm, ssm_a_re, ssm_a_im, ssm_log_dt, ssm_b_re, ssm_b_im, ssm_c_re, ssm_c_im, ssm_d, ssm_glu_w, ssm_glu_b, w_branch_out, w_out, final_norm):
    p = dict(norm_g=norm_g, mla_q_norm=mla_q_norm, w_mla_q_b=w_mla_q_b, mla_kv_norm=mla_kv_norm,
             w_mla_kv_b=w_mla_kv_b, diff_lq1=diff_lq1, diff_lk1=diff_lk1, diff_lq2=diff_lq2, diff_lk2=diff_lk2,
             diff_subln=diff_subln, gqa_q_norm=gqa_q_norm, gqa_k_norm=gqa_k_norm, ssm_a_re=ssm_a_re,
             ssm_a_im=ssm_a_im, ssm_log_dt=ssm_log_dt, ssm_b_re=ssm_b_re, ssm_b_im=ssm_b_im, ssm_c_re=ssm_c_re,
             ssm_c_im=ssm_c_im, ssm_d=ssm_d, ssm_glu_w=ssm_glu_w, ssm_glu_b=ssm_glu_b,
             w_branch_out=w_branch_out, w_out=w_out)
    bp, tp, _ = x_prompt.shape
    bs, ts, _ = x_sample.shape
    past = cache_mla_ckv.shape[2]
    rope = _rope_lane_tables(ts)

    cvec = jnp.zeros((V7X_SUBLANES, D_MODEL), F32).at[0].set(c_ctx).at[1:1 + bs].set(c)
    mod = _modulation(cvec, w_mod, b_mod)

    xp = x_prompt.reshape(bp * tp, D_MODEL)
    xs = x_sample.reshape(bs * ts, D_MODEL)
    fn = final_norm[None]
    sides = []
    for l in range(DEPTH):
        lw = _layer_weights(l, w_in, p)
        lam_init = 0.8 - 0.6 * math.exp(-0.3 * l)
        last = l == DEPTH - 1
        sh, sc, ga = (mod[l, :, i * D_MODEL:(i + 1) * D_MODEL][:, None, :] for i in range(3))

        pr = _proj(xp, sh[0:1], sc[0:1], tp, lw, None, False)
        o_a, o_b, o_c, y, fin = _mixers(pr, bp, tp, lw, lam_init, None)
        xp = _merge(xp, sh[0:1], sc[0:1], ga[0:1], tp, o_a, o_b, o_c, y, lw, fn, last)
        ckv_f, kpe_f, dk_f, dv_f, gk_f, gv_f = pr[10:]
        fin = fin.reshape(2, SSM_NBLK, bp, 2, SSM_GBLOCK, SSM_STATE)
        state = jnp.transpose(fin, (2, 0, 1, 4, 5, 3)).reshape(bp, 2, SSM_GROUPS, SSM_STATE, 2)
        sides.append((ckv_f.reshape(bp, tp, MLA_KV_LORA), kpe_f.reshape(bp, tp, MLA_ROPE),
                      dk_f.reshape(bp, tp, DIFF_HEADS, 2, DIFF_HEAD_DIM), dv_f.reshape(bp, tp, DIFF_HEADS, DIFF_V),
                      gk_f.reshape(bp, tp, GQA_KV_HEADS, GQA_HEAD_DIM), gv_f.reshape(bp, tp, GQA_KV_HEADS, GQA_HEAD_DIM),
                      state))

        pad_lo = MLA_NOPE
        pad_hi = MLA_HEAD_PAD - MLA_NOPE - MLA_ROPE
        kpe_ctx = jnp.pad(cache_mla_krope[:, l].reshape(bs * past, MLA_ROPE), ((0, 0), (pad_lo, pad_hi)))
        k_ctx, v_ctx = _ctx_kv(cache_mla_ckv[:, l].reshape(bs * past, MLA_KV_LORA), kpe_ctx, lw)
        s0 = state_ssm[:, l].reshape(bs, 2, SSM_NBLK, SSM_GBLOCK * SSM_STATE, 2)
        s0 = jnp.transpose(s0, (1, 0, 2, 4, 3)).reshape(2, bs, SSM_NBLK, 1, 2 * SSM_SBLK)
        ctx = dict(
            kv=(k_ctx.reshape(bs, past, -1), v_ctx.reshape(bs, past, -1),
                cache_diff_k[:, l].reshape(bs, past, -1).astype(BF16),
                cache_diff_v[:, l].reshape(bs, past, -1).astype(BF16),
                cache_gqa_k[:, l].reshape(bs, past, -1).astype(BF16),
                cache_gqa_v[:, l].reshape(bs, past, -1).astype(BF16)),
            s0=s0)
        pr = _proj(xs, sh[1:1 + bs], sc[1:1 + bs], ts, lw, rope, True)
        o_a, o_b, o_c, y, _ = _mixers(pr, bs, ts, lw, lam_init, ctx)
        xs = _merge(xs, sh[1:1 + bs], sc[1:1 + bs], ga[1:1 + bs], ts, o_a, o_b, o_c, y, lw, fn, last)

    outs = [xp.reshape(bp, tp, D_MODEL), xs.reshape(bs, ts, D_MODEL)]
    outs += [jnp.stack([s[i] for s in sides], axis=1) for i in range(7)]
    return tuple(outs)
```

```python
import functools
import math

import jax
import jax.numpy as jnp
from jax import lax
from jax.experimental import pallas as pl
from jax.experimental.pallas import tpu as pltpu

F32 = jnp.float32
BF16 = jnp.bfloat16

D_MODEL = 1024
DEPTH = 2
GRID_W = 64
ROPE_THETA = 10000.0
EPS = 1e-6

MLA_HEADS = 8
MLA_NOPE = 64
MLA_ROPE = 32
MLA_V = 64
MLA_Q_LORA = 256
MLA_KV_LORA = 128
MLA_SCALE = (MLA_NOPE + MLA_ROPE) ** -0.5
MLA_HEAD_PAD = 128
DIFF_HEADS = 4
DIFF_HEAD_DIM = 64
DIFF_V = 2 * DIFF_HEAD_DIM
DIFF_SCALE = DIFF_HEAD_DIM ** -0.5
GQA_HEADS = 8
GQA_KV_HEADS = 2
GQA_HEAD_DIM = 64
GQA_GROUP = GQA_HEADS // GQA_KV_HEADS
GQA_KV_WIDTH = GQA_KV_HEADS * GQA_HEAD_DIM
GQA_SCALE = GQA_HEAD_DIM ** -0.5
SSM_WIDTH = 512
SSM_GROUP = 16
SSM_GROUPS = SSM_WIDTH // SSM_GROUP
SSM_STATE = 64
N_BRANCH = 4
BRANCH_WIDTH = 512

IN_SPLITS = (MLA_Q_LORA, MLA_KV_LORA, MLA_ROPE, 512, 512, 512, 512, 128, 128, SSM_WIDTH,
             N_BRANCH * BRANCH_WIDTH, N_BRANCH * D_MODEL)
D_IN = sum(IN_SPLITS)

V7X_LANES = 128
V7X_SUBLANES = 8
V7X_VMEM_LIMIT_BYTES = 56 * 1024 * 1024

_H_QA = (0, 256)
_H_KVA = (256, 384)
_H_KPE = (384, 512)
_H_WIDTH = 512
_M_DQ = (0, 512)
_M_DK = (512, 1024)
_M_DV = (1024, 1536)
_M_GQ = (1536, 2048)
_M_GK = (2048, 2176)
_M_GV = (2176, 2304)
_M_U = (2304, 2816)
_M_WIDTH = 2816
_IN_MIX_START = MLA_Q_LORA + MLA_KV_LORA + MLA_ROPE
_IN_GATE_START = _IN_MIX_START + _M_WIDTH
_IN_MERGE_START = _IN_GATE_START + N_BRANCH * BRANCH_WIDTH

SSM_GBLOCK = 8
SSM_NBLK = SSM_GROUPS // SSM_GBLOCK
SSM_SBLK = SSM_GBLOCK * SSM_STATE
SSM_UBLK = SSM_GBLOCK * SSM_GROUP
SSM_SEQ = 256

PROJ_TM = 512
MERGE_TM = 256
ATTN_TQ = 256


def _params(n_axes, vmem=V7X_VMEM_LIMIT_BYTES):
    return pltpu.CompilerParams(dimension_semantics=("arbitrary",) * n_axes, vmem_limit_bytes=vmem)


def _const_spec(shape):
    zeros = (0,) * len(shape)
    return pl.BlockSpec(shape, lambda *_: zeros, pipeline_mode=pl.Buffered(1))


def _layer_spec(shape, l):
    zeros = (0,) * len(shape)
    return pl.BlockSpec((None,) + tuple(shape), lambda *_: (l,) + zeros, pipeline_mode=pl.Buffered(1))


def _mod_spec(n_rows, tm, seq):
    if n_rows == 1:
        return pl.BlockSpec((1, 1, D_MODEL), lambda i: (0, 0, 0))
    return pl.BlockSpec((1, 1, D_MODEL), lambda i: (i * tm // seq, 0, 0))


def _dot(a, b):
    return jnp.dot(a, b, preferred_element_type=F32)


def _dot_nt(a, b):
    return lax.dot_general(a, b, (((1,), (1,)), ((), ())), preferred_element_type=F32)


def _rms(x):
    return x * lax.rsqrt(jnp.mean(x * x, axis=-1, keepdims=True) + EPS)


def _sigmoid(x):
    return 1.0 / (1.0 + jnp.exp(-x))


def _gelu_tanh(x):
    return 0.5 * x * (1.0 + jnp.tanh(math.sqrt(2.0 / math.pi) * (x + 0.044715 * (x * x * x))))


def _group_mean_sq(x, gmat):
    x2 = x * x
    hi = x2.astype(BF16)
    lo = (x2 - hi.astype(F32)).astype(BF16)
    return _dot(hi, gmat) + _dot(lo, gmat)


def _rope_chunk(x, cos, sin, half):
    lane = lax.broadcasted_iota(jnp.int32, x.shape, 1)
    first = (lane % (2 * half)) < half
    swapped = jnp.where(first, pltpu.roll(x, V7X_LANES - half, 1), pltpu.roll(x, half, 1))
    return x * cos + swapped * sin


def _transpose_cast_kernel(w_ref, o_ref):
    o_ref[...] = w_ref[...].T.astype(BF16)


def _transposed_rows(w_t, row0, n_rows, blk):
    return pl.pallas_call(
        _transpose_cast_kernel,
        grid=(DEPTH, n_rows // blk),
        in_specs=[pl.BlockSpec((pl.Element(blk), pl.Element(D_MODEL)),
                               lambda l, j: (pl.multiple_of(l * D_IN + row0 + j * blk, V7X_SUBLANES), 0))],
        out_specs=pl.BlockSpec((None, D_MODEL, blk), lambda l, j: (l, 0, j)),
        out_shape=jax.ShapeDtypeStruct((DEPTH, D_MODEL, n_rows), BF16),
        compiler_params=_params(2),
        name="w_in_columns",
    )(w_t)


def _mod_kernel(c_ref, w_ref, b_ref, o_ref):
    c = c_ref[...]
    a = (c * _sigmoid(c)).astype(BF16)
    o_ref[0] = _dot(a, w_ref[0].astype(BF16)) + b_ref[0]


def _modulation(cvec, w_mod, b_mod):
    tn = 512
    return pl.pallas_call(
        _mod_kernel,
        grid=(DEPTH, 3 * D_MODEL // tn),
        in_specs=[pl.BlockSpec((V7X_SUBLANES, D_MODEL), lambda l, j: (0, 0)),
                  pl.BlockSpec((1, D_MODEL, tn), lambda l, j: (l, 0, j)),
                  pl.BlockSpec((1, 1, tn), lambda l, j: (l, 0, j))],
        out_specs=pl.BlockSpec((1, V7X_SUBLANES, tn), lambda l, j: (l, 0, j)),
        out_shape=jax.ShapeDtypeStruct((DEPTH, V7X_SUBLANES, 3 * D_MODEL), F32),
        compiler_params=_params(2),
        name="modulation",
    )(cvec, w_mod, b_mod.reshape(DEPTH, 1, 3 * D_MODEL))


def _modulated(x, g_ref, sh_ref, sc_ref):
    return (_rms(x) * g_ref[...]) * (1.0 + sc_ref[0]) + sh_ref[0]


def _proj_kernel(latent, n_prev, *refs):
    (x_ref, sh_ref, sc_ref, g_ref, wh_ref, wx_ref, qn_ref, wqb_ref, kvn_ref, wk_ref, wv_ref,
     gqn_ref, gkn_ref, gm_ref) = refs[:14]
    pos = 14
    if latent:
        c_mla, s_mla, c_64, s_64 = refs[pos:pos + 4]
        pos += 4
    pos += n_prev
    (q_o, k_o, v_o, dq_o, dk_o, dv_o, gq_o, gk_o, gv_o, u_o) = refs[pos:pos + 10]
    pos += 10
    if not latent:
        ckv_f, kpe_t, dk_t, dv_f, gk_t, gv_t = refs[pos:pos + 6]

    h = _modulated(x_ref[...], g_ref, sh_ref, sc_ref).astype(BF16)
    tm = h.shape[0]

    def head(cols):
        return _dot(h, wh_ref[:, cols[0]:cols[1]])

    def mix(cols):
        return _dot(h, wx_ref[:, cols[0]:cols[1]])

    def rope_mla(x):
        return _rope_chunk(x, c_mla[...], s_mla[...], MLA_ROPE // 2) if latent else x

    def rope_64(x):
        if not latent:
            return x
        c, s = c_64[...], s_64[...]
        chunks = [_rope_chunk(x[:, i:i + V7X_LANES], c, s, DIFF_HEAD_DIM // 2)
                  for i in range(0, x.shape[1], V7X_LANES)]
        return chunks[0] if len(chunks) == 1 else jnp.concatenate(chunks, axis=1)

    def per_sequence(x):
        return [x[b * SSM_SEQ:(b + 1) * SSM_SEQ] for b in range(tm // SSM_SEQ)]

    qn = (_rms(head(_H_QA)) * qn_ref[...]).astype(BF16)
    q = _dot(qn, wqb_ref[...])
    for hd in range(MLA_HEADS):
        sl = slice(hd * MLA_HEAD_PAD, (hd + 1) * MLA_HEAD_PAD)
        q_o[:, sl] = (rope_mla(q[:, sl]) * MLA_SCALE).astype(BF16)
    ckv = _rms(head(_H_KVA)) * kvn_ref[...]
    kpe = head(_H_KPE)
    if not latent:
        for b, (c_b, k_b) in enumerate(zip(per_sequence(ckv), per_sequence(kpe))):
            ckv_f[b] = c_b
            kpe_t[b] = k_b.T[MLA_NOPE:MLA_NOPE + MLA_ROPE, :]
    kpe = rope_mla(kpe)
    ckv_b = ckv.astype(BF16)
    kn = _dot(ckv_b, wk_ref[...])
    for hd in range(MLA_HEADS):
        sl = slice(hd * MLA_HEAD_PAD, (hd + 1) * MLA_HEAD_PAD)
        k_o[:, sl] = (kn[:, sl] + kpe).astype(BF16)
    v_o[...] = _dot(ckv_b, wv_ref[...]).astype(BF16)

    dq_o[...] = (rope_64(mix(_M_DQ)) * DIFF_SCALE).astype(BF16)
    dk = rope_64(mix(_M_DK))
    dv = mix(_M_DV)
    dk_o[...] = dk.astype(BF16)
    dv_o[...] = dv.astype(BF16)

    gq = mix(_M_GQ)
    gq = gq * lax.rsqrt(_group_mean_sq(gq, gm_ref[...]) + EPS) * gqn_ref[...]
    gq_o[...] = (rope_64(gq) * GQA_SCALE).astype(BF16)
    gk = mix(_M_GK)
    kw = GQA_KV_WIDTH
    gk = gk * lax.rsqrt(_group_mean_sq(gk, gm_ref[:kw, :kw]) + EPS) * gkn_ref[...]
    gv = mix(_M_GV)
    if not latent:
        for b, (dk_b, dv_b, gk_b, gv_b) in enumerate(zip(*(per_sequence(a) for a in (dk, dv, gk, gv)))):
            dk_t[b] = dk_b.T
            for hd in range(DIFF_HEADS):
                dv_f[b, :, hd, :] = dv_b[:, hd * DIFF_V:(hd + 1) * DIFF_V]
            gk_t[b] = gk_b.T
            gv_t[b] = gv_b.T
    gk_o[...] = rope_64(gk).astype(BF16)
    gv_o[...] = gv.astype(BF16)

    u_o[...] = mix(_M_U)


def _side_shapes(n_seq):
    return [((n_seq, DEPTH, SSM_SEQ, MLA_KV_LORA), (SSM_SEQ, MLA_KV_LORA)),
            ((n_seq, DEPTH, MLA_ROPE, SSM_SEQ), (MLA_ROPE, SSM_SEQ)),
            ((n_seq, DEPTH, 512, SSM_SEQ), (512, SSM_SEQ)),
            ((n_seq, DEPTH, SSM_SEQ, DIFF_HEADS, DIFF_V), (SSM_SEQ, DIFF_HEADS, DIFF_V)),
            ((n_seq, DEPTH, GQA_KV_WIDTH, SSM_SEQ), (GQA_KV_WIDTH, SSM_SEQ)),
            ((n_seq, DEPTH, GQA_KV_WIDTH, SSM_SEQ), (GQA_KV_WIDTH, SSM_SEQ))]


def _proj(x, shift, scale, seq, l, w, rope, latent, prev_sides=()):
    n = x.shape[0]
    tm = PROJ_TM
    row = lambda i: (i, 0)
    mod_spec = _mod_spec(shift.shape[0], tm, seq)
    kw = GQA_KV_WIDTH
    in_specs = [pl.BlockSpec((tm, D_MODEL), row), mod_spec, mod_spec,
                _layer_spec((1, D_MODEL), l), _layer_spec((D_MODEL, _H_WIDTH), l), _layer_spec((D_MODEL, _M_WIDTH), l),
                _layer_spec((1, MLA_Q_LORA), l), _layer_spec((MLA_Q_LORA, MLA_HEADS * MLA_HEAD_PAD), l),
                _layer_spec((1, MLA_KV_LORA), l), _layer_spec((MLA_KV_LORA, MLA_HEADS * MLA_HEAD_PAD), l),
                _layer_spec((MLA_KV_LORA, MLA_HEADS * MLA_V), l),
                _layer_spec((1, 512), l), _layer_spec((1, kw), l), _const_spec((512, 512))]
    args = [x, shift, scale, w["norm_g"], w["whead"], w["wmix"], w["qn"], w["wqb"], w["kvn"], w["wk"], w["wv"],
            w["gqn"], w["gkn"], w["gmat"]]
    if latent:
        per_seq = seq // tm
        tab = pl.BlockSpec((tm, V7X_LANES), lambda i: (i % per_seq, 0))
        in_specs += [tab] * 4
        args += list(rope)
    n_fixed = len(args)
    in_specs += [pl.BlockSpec(memory_space=pl.ANY)] * len(prev_sides)
    args += list(prev_sides)
    widths = [(1024, BF16), (1024, BF16), (512, BF16), (512, BF16), (512, BF16), (512, BF16),
              (512, BF16), (kw, BF16), (kw, BF16), (512, F32)]
    out_specs = [pl.BlockSpec((tm, wd), row) for wd, _ in widths]
    out_shape = [jax.ShapeDtypeStruct((n, wd), dt) for wd, dt in widths]
    aliases = {}
    if not latent:
        per_tile = tm // SSM_SEQ
        for full, blk in _side_shapes(n // SSM_SEQ):
            zeros = (0,) * len(blk)
            out_specs.append(pl.BlockSpec((per_tile, None) + blk, lambda i, zeros=zeros: (i, l) + zeros))
            out_shape.append(jax.ShapeDtypeStruct(full, F32))
        aliases = {n_fixed + k: len(widths) + k for k in range(len(prev_sides))}
    return pl.pallas_call(
        functools.partial(_proj_kernel, latent, len(prev_sides)),
        grid=(n // tm,),
        in_specs=in_specs, out_specs=out_specs, out_shape=out_shape,
        input_output_aliases=aliases,
        compiler_params=_params(1),
        name="proj_latent" if latent else "proj_context",
    )(*args)


def _ctx_kv_kernel(ckv_ref, kpe_ref, wk_ref, wv_ref, k_o, v_o):
    ckv_b = ckv_ref[...].astype(BF16)
    kn = _dot(ckv_b, wk_ref[...])
    kpe = kpe_ref[...]
    for hd in range(MLA_HEADS):
        sl = slice(hd * MLA_HEAD_PAD, (hd + 1) * MLA_HEAD_PAD)
        k_o[:, sl] = (kn[:, sl] + kpe).astype(BF16)
    v_o[...] = _dot(ckv_b, wv_ref[...]).astype(BF16)


def _ctx_kv(ckv, kpe_pad, l, w):
    n = ckv.shape[0]
    tm = 512
    row = lambda i: (i, 0)
    return pl.pallas_call(
        _ctx_kv_kernel,
        grid=(n // tm,),
        in_specs=[pl.BlockSpec((tm, MLA_KV_LORA), row), pl.BlockSpec((tm, MLA_HEAD_PAD), row),
                  _layer_spec((MLA_KV_LORA, MLA_HEADS * MLA_HEAD_PAD), l),
                  _layer_spec((MLA_KV_LORA, MLA_HEADS * MLA_V), l)],
        out_specs=[pl.BlockSpec((tm, MLA_HEADS * MLA_HEAD_PAD), row), pl.BlockSpec((tm, MLA_HEADS * MLA_V), row)],
        out_shape=[jax.ShapeDtypeStruct((n, MLA_HEADS * MLA_HEAD_PAD), BF16),
                   jax.ShapeDtypeStruct((n, MLA_HEADS * MLA_V), BF16)],
        compiler_params=_params(1),
        name="ctx_kv",
    )(ckv, kpe_pad, w["wk"], w["wv"])


def _softmax_parts(s):
    p = jnp.exp(s - jnp.max(s, axis=-1, keepdims=True))
    return p, 1.0 / jnp.sum(p, axis=-1, keepdims=True)


def _attn_heads_kernel(n_heads, q_width, kv_group, q_ref, k_ref, v_ref, o_ref):
    outs = []
    for hd in range(n_heads):
        g = hd // kv_group
        q = q_ref[0, :, hd * q_width:(hd + 1) * q_width]
        k = k_ref[0, :, g * q_width:(g + 1) * q_width]
        v = v_ref[0, :, g * 64:(g + 1) * 64]
        p, inv = _softmax_parts(_dot_nt(q, k))
        outs.append(_dot(p.astype(BF16), v) * inv)
        if hd % 2 == 1:
            o_ref[0, :, (hd - 1) * 64:(hd + 1) * 64] = jnp.concatenate(outs, axis=1).astype(BF16)
            outs = []


def _diff_attn_kernel(lam_init, q_ref, k_ref, v_ref, lp_ref, sub_ref, o_ref):
    lp = lp_ref[...]
    lam = (jnp.exp(jnp.sum(lp[0:1] * lp[1:2], axis=-1, keepdims=True))
           - jnp.exp(jnp.sum(lp[2:3] * lp[3:4], axis=-1, keepdims=True)) + lam_init)
    d = DIFF_HEAD_DIM
    for hd in range(DIFF_HEADS):
        c1, c2 = 2 * hd * d, (2 * hd + 1) * d
        p1, inv1 = _softmax_parts(_dot_nt(q_ref[0, :, c1:c1 + d], k_ref[0, :, c1:c1 + d]))
        p2, inv2 = _softmax_parts(_dot_nt(q_ref[0, :, c2:c2 + d], k_ref[0, :, c2:c2 + d]))
        w = p1 * inv1 - p2 * (lam * inv2)
        o = _dot(w.astype(BF16), v_ref[0, :, hd * DIFF_V:(hd + 1) * DIFF_V])
        o = _rms(o) * sub_ref[...] * (1.0 - lam_init)
        o_ref[0, :, hd * DIFF_V:(hd + 1) * DIFF_V] = o.astype(BF16)


def _attention(kernel, name, q, k, v, extra_specs=(), extra=()):
    b, t, wq = q.shape
    s = k.shape[1]
    tq = ATTN_TQ
    in_specs = [pl.BlockSpec((1, tq, wq), lambda i, j: (i, j, 0)),
                pl.BlockSpec((1, s, k.shape[2]), lambda i, j: (i, 0, 0)),
                pl.BlockSpec((1, s, v.shape[2]), lambda i, j: (i, 0, 0))]
    in_specs += list(extra_specs)
    return pl.pallas_call(
        kernel,
        grid=(b, t // tq),
        in_specs=in_specs,
        out_specs=pl.BlockSpec((1, tq, 512), lambda i, j: (i, j, 0)),
        out_shape=jax.ShapeDtypeStruct((b, t, 512), BF16),
        compiler_params=_params(2),
        name=name,
    )(q, k, v, *extra)


def _ssm_kernel(chunked, want_final, *refs):
    u_ref, bm_ref, ab_ref, cm_ref, d_ref = refs[:5]
    pos = 5
    if chunked:
        s0_ref = refs[pos]
        pos += 1
    y_ref = refs[pos]
    pos += 1
    if want_final:
        fin_ref = refs[pos]
        pos += 1
    hs_ref = refs[pos]

    seq, rows, sb = SSM_SEQ, V7X_SUBLANES, SSM_SBLK
    u2 = u_ref[...].reshape(seq * rows, SSM_UBLK)
    u2b = u2.astype(BF16)
    for dr in range(2):
        hs_ref[dr] = _dot(u2b, bm_ref[dr, 0])

    a_parts = []
    for dr in range(2):
        a = ab_ref[dr, 0]
        a_parts.append((jnp.broadcast_to(a[:, :sb], (rows, sb)), jnp.broadcast_to(a[:, sb:], (rows, sb))))

    def row_slices(i):
        return pl.ds(pl.multiple_of(i * rows, rows), rows), pl.ds(pl.multiple_of((seq - 1 - i) * rows, rows), rows)

    def scan_step(i, carry):
        new = []
        for dr, sl in enumerate(row_slices(i)):
            ar, ai = a_parts[dr]
            hr, hi = carry[2 * dr], carry[2 * dr + 1]
            b = hs_ref[dr, sl, :]
            nr = ar * hr - ai * hi + b[:, :sb]
            ni = ar * hi + ai * hr + b[:, sb:]
            hs_ref[dr, sl, :sb] = nr
            hs_ref[dr, sl, sb:] = ni
            new += [nr, ni]
        return tuple(new)

    zero = jnp.zeros((rows, sb), F32)
    fin = lax.fori_loop(0, seq, scan_step, (zero, zero, zero, zero), unroll=2)

    if want_final:
        for dr in range(2):
            fin_ref[dr, 0, :, :sb] = fin[2 * dr]
            fin_ref[dr, 0, :, sb:] = fin[2 * dr + 1]

    if chunked:
        entry = []
        for dr in range(2):
            ar, ai = a_parts[dr]
            pr, pi = ar[0:1], ai[0:1]
            for _ in range(int(math.log2(seq))):
                pr, pi = pr * pr - pi * pi, 2.0 * pr * pi
            fr, fi = fin[2 * dr], fin[2 * dr + 1]
            s0 = s0_ref[dr, 0, 0]
            er, ei = s0[:, :sb], s0[:, sb:]
            order = range(rows) if dr == 0 else range(rows - 1, -1, -1)
            rows_r, rows_i = [None] * rows, [None] * rows
            for c in order:
                rows_r[c], rows_i[c] = er, ei
                er, ei = (pr * er - pi * ei + fr[c:c + 1], pr * ei + pi * er + fi[c:c + 1])
            entry += [jnp.concatenate(rows_r, axis=0), jnp.concatenate(rows_i, axis=0)]

        def fix_step(i, carry):
            new = []
            for dr, sl in enumerate(row_slices(i)):
                ar, ai = a_parts[dr]
                zr, zi = carry[2 * dr], carry[2 * dr + 1]
                nr = ar * zr - ai * zi
                ni = ar * zi + ai * zr
                hs_ref[dr, sl, :sb] += nr
                hs_ref[dr, sl, sb:] += ni
                new += [nr, ni]
            return tuple(new)

        lax.fori_loop(0, seq, fix_step, tuple(entry), unroll=2)

    y = _dot(hs_ref[0].astype(BF16), cm_ref[0, 0]) + _dot(hs_ref[1].astype(BF16), cm_ref[1, 0])
    y = _gelu_tanh(y + d_ref[...] * u2)
    y_ref[...] = y.reshape(seq, rows, SSM_UBLK)


def _ssm(u_t, l, w, s0, want_final):
    seq, r, _ = u_t.shape
    chunked = s0 is not None
    rows = V7X_SUBLANES
    in_specs = [pl.BlockSpec((seq, rows, SSM_UBLK), lambda i, j: (0, i, j)),
                pl.BlockSpec((None, 2, 1, SSM_UBLK, 2 * SSM_SBLK), lambda i, j: (l, 0, j, 0, 0)),
                pl.BlockSpec((None, 2, 1, 1, 2 * SSM_SBLK), lambda i, j: (l, 0, j, 0, 0)),
                pl.BlockSpec((None, 2, 1, 2 * SSM_SBLK, SSM_UBLK), lambda i, j: (l, 0, j, 0, 0)),
                pl.BlockSpec((None, 1, SSM_UBLK), lambda i, j: (l, 0, j))]
    args = [u_t, w["ssm_b"], w["ssm_a"], w["ssm_c"], w["ssm_d"]]
    if chunked:
        in_specs.append(pl.BlockSpec((None, 2, 1, 1, 1, 2 * SSM_SBLK), lambda i, j: (l, 0, i, j, 0, 0)))
        args.append(s0)
    out_specs = [pl.BlockSpec((seq, rows, SSM_UBLK), lambda i, j: (0, i, j))]
    out_shape = [jax.ShapeDtypeStruct((seq, r, SSM_WIDTH), F32)]
    if want_final:
        out_specs.append(pl.BlockSpec((2, 1, rows, 2 * SSM_SBLK), lambda i, j: (0, j, i, 0)))
        out_shape.append(jax.ShapeDtypeStruct((2, SSM_NBLK, r, 2 * SSM_SBLK), F32))
    return pl.pallas_call(
        functools.partial(_ssm_kernel, chunked, want_final),
        grid=(r // rows, SSM_NBLK),
        in_specs=in_specs, out_specs=out_specs, out_shape=out_shape,
        scratch_shapes=[pltpu.VMEM((2, seq * rows, 2 * SSM_SBLK), F32)],
        compiler_params=_params(2),
        name="ssm_latent" if chunked else "ssm_context",
    )(*args)


def _merge_kernel(last, x_ref, sh_ref, sc_ref, ga_ref, g_ref, oa_ref, ob_ref, oc_ref, y_ref,
                  wg_ref, wm_ref, wglu_ref, bglu_ref, wbr_ref, wout_ref, fn_ref, o_ref):
    x = x_ref[...]
    h = _modulated(x, g_ref, sh_ref, sc_ref).astype(BF16)
    y = y_ref[...]
    o_d = y * _sigmoid(_dot(y.astype(BF16), wglu_ref[...]) + bglu_ref[...])
    branches = (oa_ref[...].astype(F32), ob_ref[...].astype(F32), oc_ref[...].astype(F32), o_d)
    acc = None
    for n, o in enumerate(branches):
        gate = _dot(h, wg_ref[:, n * BRANCH_WIDTH:(n + 1) * BRANCH_WIDTH])
        br = _dot((o * (gate * _sigmoid(gate))).astype(BF16), wbr_ref[n])
        term = _sigmoid(_dot(h, wm_ref[:, n * D_MODEL:(n + 1) * D_MODEL])) * br
        acc = term if acc is None else acc + term
    xn = x + ga_ref[0] * _dot(acc.astype(BF16), wout_ref[...])
    if last:
        xn = _rms(xn) * fn_ref[...]
    o_ref[...] = xn


def _merge(x, shift, scale, gate, seq, oa, ob, oc, y, l, w, final_norm, last):
    n = x.shape[0]
    tm = MERGE_TM
    row = lambda i: (i, 0)
    mod_spec = _mod_spec(shift.shape[0], tm, seq)
    in_specs = [pl.BlockSpec((tm, D_MODEL), row), mod_spec, mod_spec, mod_spec, _layer_spec((1, D_MODEL), l),
                pl.BlockSpec((tm, 512), row), pl.BlockSpec((tm, 512), row), pl.BlockSpec((tm, 512), row),
                pl.BlockSpec((tm, 512), row),
                _layer_spec((D_MODEL, N_BRANCH * BRANCH_WIDTH), l), _layer_spec((D_MODEL, N_BRANCH * D_MODEL), l),
                _layer_spec((SSM_WIDTH, SSM_WIDTH), l), _layer_spec((1, SSM_WIDTH), l),
                _layer_spec((N_BRANCH, BRANCH_WIDTH, D_MODEL), l), _layer_spec((D_MODEL, D_MODEL), l),
                _const_spec((1, D_MODEL))]
    return pl.pallas_call(
        functools.partial(_merge_kernel, last),
        grid=(n // tm,),
        in_specs=in_specs,
        out_specs=pl.BlockSpec((tm, D_MODEL), row),
        out_shape=jax.ShapeDtypeStruct((n, D_MODEL), F32),
        compiler_params=_params(1),
        name="merge",
    )(x, shift, scale, gate, w["norm_g"], oa, ob, oc, y, w["wg"], w["wm"], w["wglu"], w["bglu"],
      w["wbr"], w["wout"], final_norm)


def _rope_tables(n_tok, rot_dim):
    rows = n_tok // GRID_W
    row = jnp.repeat(jnp.arange(rows, dtype=F32), GRID_W)
    col = jnp.tile(jnp.arange(GRID_W, dtype=F32), rows)
    quarter = rot_dim // 4
    inv = ROPE_THETA ** (-jnp.arange(quarter, dtype=F32) / quarter)
    ang = jnp.concatenate([row[:, None] * inv, col[:, None] * inv], axis=-1)
    return jnp.cos(ang), jnp.sin(ang)


def _rope_lane_tables(n_tok):
    c, s = _rope_tables(n_tok, MLA_ROPE)
    ones = jnp.ones((n_tok, MLA_NOPE), F32)
    pad = MLA_HEAD_PAD - MLA_NOPE - MLA_ROPE
    c_mla = jnp.concatenate([ones, c, c, jnp.ones((n_tok, pad), F32)], axis=1)
    s_mla = jnp.concatenate([0.0 * ones, -s, s, jnp.zeros((n_tok, pad), F32)], axis=1)
    c, s = _rope_tables(n_tok, DIFF_HEAD_DIM)
    c_64 = jnp.concatenate([c, c, c, c], axis=1)
    s_64 = jnp.concatenate([-s, s, -s, s], axis=1)
    return c_mla, s_mla, c_64, s_64


def _block_diag(blocks):
    g = blocks.shape[-3]
    eye = jnp.eye(g, dtype=blocks.dtype)
    out = jnp.einsum("...grc,gh->...grhc", blocks, eye)
    return out.reshape(blocks.shape[:-3] + (g * blocks.shape[-2], g * blocks.shape[-1]))


def _weights(w_in, p):
    w_t = jnp.swapaxes(w_in, 1, 2).reshape(DEPTH * D_IN, D_MODEL)
    wmix = _transposed_rows(w_t, _IN_MIX_START, _M_WIDTH, _M_WIDTH // 2)
    wg = _transposed_rows(w_t, _IN_GATE_START, N_BRANCH * BRANCH_WIDTH, 1024)
    wm = _transposed_rows(w_t, _IN_MERGE_START, N_BRANCH * D_MODEL, 1024)
    zeros = lambda wd: jnp.zeros((DEPTH, D_MODEL, wd), F32)
    kpe_lo = MLA_Q_LORA + MLA_KV_LORA
    whead = jnp.concatenate([w_in[:, :, :kpe_lo], zeros(MLA_NOPE), w_in[:, :, kpe_lo:_IN_MIX_START],
                             zeros(MLA_HEAD_PAD - MLA_NOPE - MLA_ROPE)], axis=2).astype(BF16)

    hq = MLA_NOPE + MLA_ROPE
    wqb = p["w_mla_q_b"].reshape(DEPTH, MLA_Q_LORA, MLA_HEADS, hq)
    wqb = jnp.pad(wqb, ((0, 0), (0, 0), (0, 0), (0, MLA_HEAD_PAD - hq))).reshape(DEPTH, MLA_Q_LORA, -1).astype(BF16)
    wkv = p["w_mla_kv_b"].reshape(DEPTH, MLA_KV_LORA, MLA_HEADS, MLA_NOPE + MLA_V)
    wk = jnp.pad(wkv[..., :MLA_NOPE], ((0, 0), (0, 0), (0, 0), (0, MLA_HEAD_PAD - MLA_NOPE)))
    wk = wk.reshape(DEPTH, MLA_KV_LORA, -1).astype(BF16)
    wv = wkv[..., MLA_NOPE:].reshape(DEPTH, MLA_KV_LORA, -1).astype(BF16)

    gmat = _block_diag(jnp.full((GQA_HEADS, GQA_HEAD_DIM, GQA_HEAD_DIM), 1.0 / GQA_HEAD_DIM, F32)).astype(BF16)

    lam = lax.complex(p["ssm_a_re"], p["ssm_a_im"])
    dt = jnp.exp(p["ssm_log_dt"])[..., None]
    abar = jnp.exp(lam * dt)
    bbar = ((abar - 1.0) / lam)[..., None] * lax.complex(p["ssm_b_re"], p["ssm_b_im"])
    blk = lambda a: a.reshape((DEPTH, 2, SSM_NBLK, SSM_GBLOCK) + a.shape[3:])
    b_t = jnp.swapaxes(blk(bbar), -1, -2)
    ssm_b = jnp.concatenate([_block_diag(jnp.real(b_t)), _block_diag(jnp.imag(b_t))], axis=-1).astype(BF16)
    ab = blk(abar).reshape(DEPTH, 2, SSM_NBLK, 1, SSM_SBLK)
    ssm_a = jnp.concatenate([jnp.real(ab), jnp.imag(ab)], axis=-1)
    c_re = jnp.swapaxes(blk(p["ssm_c_re"]), -1, -2)
    c_im = jnp.swapaxes(blk(p["ssm_c_im"]), -1, -2)
    ssm_c = jnp.concatenate([_block_diag(c_re), -_block_diag(c_im)], axis=-2).astype(BF16)

    row = lambda a: a[:, None, :]
    return dict(
        norm_g=row(p["norm_g"]), whead=whead, wmix=wmix, qn=row(p["mla_q_norm"]), wqb=wqb,
        kvn=row(p["mla_kv_norm"]), wk=wk, wv=wv,
        gqn=row(jnp.tile(p["gqa_q_norm"], (1, GQA_HEADS))), gkn=row(jnp.tile(p["gqa_k_norm"], (1, GQA_KV_HEADS))),
        gmat=gmat,
        lam_parts=jnp.stack([p["diff_lq1"], p["diff_lk1"], p["diff_lq2"], p["diff_lk2"]], axis=1),
        subln=row(p["diff_subln"]),
        ssm_b=ssm_b, ssm_a=ssm_a, ssm_c=ssm_c, ssm_d=row(p["ssm_d"]),
        wg=wg, wm=wm, wglu=p["ssm_glu_w"].astype(BF16), bglu=row(p["ssm_glu_b"]),
        wbr=p["w_branch_out"].astype(BF16), wout=p["w_out"].astype(BF16),
    )


def _mixers(pr, b, t, l, w, lam_init, ctx):
    q, k, v, dq, dk, dv, gq, gk, gv, u = pr[:10]
    three = lambda a: a.reshape(b, t, a.shape[-1])
    k, v, dk, dv, gk, gv = (three(a) for a in (k, v, dk, dv, gk, gv))
    if ctx is not None:
        cat = lambda c, a: jnp.concatenate([c, a], axis=1)
        k, v, dk, dv, gk, gv = (cat(c, a) for c, a in zip(ctx["kv"], (k, v, dk, dv, gk, gv)))
    o_a = _attention(functools.partial(_attn_heads_kernel, MLA_HEADS, MLA_HEAD_PAD, 1), "attn_mla",
                     three(q), k, v)
    o_b = _attention(functools.partial(_diff_attn_kernel, lam_init), "attn_diff", three(dq), dk, dv,
                     extra_specs=(_layer_spec((4, DIFF_HEAD_DIM), l), _layer_spec((1, DIFF_V), l)),
                     extra=(w["lam_parts"], w["subln"]))
    o_c = _attention(functools.partial(_attn_heads_kernel, GQA_HEADS, GQA_HEAD_DIM, GQA_GROUP), "attn_gqa",
                     three(gq), gk, gv)
    n = b * t
    if ctx is None:
        u_t = jnp.swapaxes(u.reshape(b, t, SSM_WIDTH), 0, 1)
        y_t, fin = _ssm(u_t, l, w, None, True)
        y = jnp.swapaxes(y_t, 0, 1).reshape(n, SSM_WIDTH)
    else:
        chunks = t // SSM_SEQ
        u_t = jnp.transpose(u.reshape(b, chunks, SSM_SEQ, SSM_WIDTH), (2, 0, 1, 3)).reshape(SSM_SEQ, b * chunks, SSM_WIDTH)
        (y_t,) = _ssm(u_t, l, w, ctx["s0"], False)
        y = jnp.transpose(y_t.reshape(SSM_SEQ, b, chunks, SSM_WIDTH), (1, 2, 0, 3)).reshape(n, SSM_WIDTH)
        fin = None
    flat = lambda a: a.reshape(n, 512)
    return flat(o_a), flat(o_b), flat(o_c), y, fin


def kernel(x_prompt, x_sample, cache_mla_ckv, cache_mla_krope, cache_diff_k, cache_diff_v, cache_gqa_k, cache_gqa_v, state_ssm, c, c_ctx, norm_g, w_mod, b_mod, w_in, mla_q_norm, w_mla_q_b, mla_kv_norm, w_mla_kv_b, diff_lq1, diff_lk1, diff_lq2, diff_lk2, diff_subln, gqa_q_norm, gqa_k_norm, ssm_a_re, ssm_a_im, ssm_log_dt, ssm_b_re, ssm_b_im, ssm_c_re, ssm_c_im, ssm_d, ssm_glu_w, ssm_glu_b, w_branch_out, w_out, final_norm):
    p = dict(norm_g=norm_g, mla_q_norm=mla_q_norm, w_mla_q_b=w_mla_q_b, mla_kv_norm=mla_kv_norm,
             w_mla_kv_b=w_mla_kv_b, diff_lq1=diff_lq1, diff_lk1=diff_lk1, diff_lq2=diff_lq2, diff_lk2=diff_lk2,
             diff_subln=diff_subln, gqa_q_norm=gqa_q_norm, gqa_k_norm=gqa_k_norm, ssm_a_re=ssm_a_re,
             ssm_a_im=ssm_a_im, ssm_log_dt=ssm_log_dt, ssm_b_re=ssm_b_re, ssm_b_im=ssm_b_im, ssm_c_re=ssm_c_re,
             ssm_c_im=ssm_c_im, ssm_d=ssm_d, ssm_glu_w=ssm_glu_w, ssm_glu_b=ssm_glu_b,
             w_branch_out=w_branch_out, w_out=w_out)
    bp, tp, _ = x_prompt.shape
    bs, ts, _ = x_sample.shape
    past = cache_mla_ckv.shape[2]
    assert tp == SSM_SEQ and ts % SSM_SEQ == 0 and ts // SSM_SEQ == V7X_SUBLANES
    rope = _rope_lane_tables(ts)
    w = _weights(w_in, p)

    cvec = jnp.zeros((V7X_SUBLANES, D_MODEL), F32).at[0].set(c_ctx).at[1:1 + bs].set(c)
    mod = _modulation(cvec, w_mod, b_mod)

    pad_lo = MLA_NOPE
    pad_hi = MLA_HEAD_PAD - MLA_NOPE - MLA_ROPE
    kpe_ctx = jnp.pad(cache_mla_krope, ((0, 0), (0, 0), (0, 0), (pad_lo, pad_hi)))
    flat_ctx = lambda a: a.reshape(bs, DEPTH, past, -1).astype(BF16)
    dk_ctx, dv_ctx, gk_ctx, gv_ctx = (flat_ctx(a) for a in (cache_diff_k, cache_diff_v, cache_gqa_k, cache_gqa_v))
    s0_all = state_ssm.reshape(bs, DEPTH, 2, SSM_NBLK, SSM_GBLOCK * SSM_STATE, 2)
    s0_all = jnp.transpose(s0_all, (1, 2, 0, 3, 5, 4)).reshape(DEPTH, 2, bs, SSM_NBLK, 1, 2 * SSM_SBLK)

    xp = x_prompt.reshape(bp * tp, D_MODEL)
    xs = x_sample.reshape(bs * ts, D_MODEL)
    fn = final_norm[None]
    sides = ()
    states = []
    for l in range(DEPTH):
        lam_init = 0.8 - 0.6 * math.exp(-0.3 * l)
        last = l == DEPTH - 1
        sh, sc, ga = (mod[l, :, i * D_MODEL:(i + 1) * D_MODEL][:, None, :] for i in range(3))

        pr = _proj(xp, sh[0:1], sc[0:1], tp, l, w, None, False, sides)
        o_a, o_b, o_c, y, fin = _mixers(pr, bp, tp, l, w, lam_init, None)
        xp = _merge(xp, sh[0:1], sc[0:1], ga[0:1], tp, o_a, o_b, o_c, y, l, w, fn, last)
        sides = tuple(pr[10:])
        fin = fin.reshape(2, SSM_NBLK, bp, 2, SSM_GBLOCK, SSM_STATE)
        states.append(jnp.transpose(fin, (2, 0, 1, 4, 5, 3)).reshape(bp, 2, SSM_GROUPS, SSM_STATE, 2))

        k_ctx, v_ctx = _ctx_kv(cache_mla_ckv[:, l].reshape(bs * past, MLA_KV_LORA),
                               kpe_ctx[:, l].reshape(bs * past, MLA_HEAD_PAD), l, w)
        ctx = dict(
            kv=(k_ctx.reshape(bs, past, -1), v_ctx.reshape(bs, past, -1),
                dk_ctx[:, l], dv_ctx[:, l], gk_ctx[:, l], gv_ctx[:, l]),
            s0=s0_all)
        pr = _proj(xs, sh[1:1 + bs], sc[1:1 + bs], ts, l, w, rope, True)
        o_a, o_b, o_c, y, _ = _mixers(pr, bs, ts, l, w, lam_init, ctx)
        xs = _merge(xs, sh[1:1 + bs], sc[1:1 + bs], ga[1:1 + bs], ts, o_a, o_b, o_c, y, l, w, fn, last)

    ckv_s, kpe_t, dk_t, dv_s, gk_t, gv_t = sides
    untranspose = lambda a: jnp.swapaxes(a, 2, 3)
    return (xp.reshape(bp, tp, D_MODEL), xs.reshape(bs, ts, D_MODEL),
            ckv_s, untranspose(kpe_t),
            untranspose(dk_t).reshape(bp, DEPTH, tp, DIFF_HEADS, 2, DIFF_HEAD_DIM), dv_s,
            untranspose(gk_t).reshape(bp, DEPTH, tp, GQA_KV_HEADS, GQA_HEAD_DIM),
            untranspose(gv_t).reshape(bp, DEPTH, tp, GQA_KV_HEADS, GQA_HEAD_DIM),
            jnp.stack(states, axis=1))
```

```python
import functools
import math

import jax
import jax.numpy as jnp
from jax import lax
from jax.experimental import pallas as pl
from jax.experimental.pallas import tpu as pltpu

F32 = jnp.float32
BF16 = jnp.bfloat16

D_MODEL = 1024
DEPTH = 2
GRID_W = 64
ROPE_THETA = 10000.0
EPS = 1e-6

MLA_HEADS = 8
MLA_NOPE = 64
MLA_ROPE = 32
MLA_V = 64
MLA_Q_LORA = 256
MLA_KV_LORA = 128
MLA_SCALE = (MLA_NOPE + MLA_ROPE) ** -0.5
MLA_HEAD_PAD = 128
DIFF_HEADS = 4
DIFF_HEAD_DIM = 64
DIFF_V = 2 * DIFF_HEAD_DIM
DIFF_SCALE = DIFF_HEAD_DIM ** -0.5
GQA_HEADS = 8
GQA_KV_HEADS = 2
GQA_HEAD_DIM = 64
GQA_GROUP = GQA_HEADS // GQA_KV_HEADS
GQA_KV_WIDTH = GQA_KV_HEADS * GQA_HEAD_DIM
GQA_SCALE = GQA_HEAD_DIM ** -0.5
SSM_WIDTH = 512
SSM_GROUP = 16
SSM_GROUPS = SSM_WIDTH // SSM_GROUP
SSM_STATE = 64
N_BRANCH = 4
BRANCH_WIDTH = 512

IN_SPLITS = (MLA_Q_LORA, MLA_KV_LORA, MLA_ROPE, 512, 512, 512, 512, 128, 128, SSM_WIDTH,
             N_BRANCH * BRANCH_WIDTH, N_BRANCH * D_MODEL)
D_IN = sum(IN_SPLITS)

V7X_LANES = 128
V7X_SUBLANES = 8
V7X_VMEM_LIMIT_BYTES = 56 * 1024 * 1024

_H_QA = (0, 256)
_H_KVA = (256, 384)
_H_KPE = (384, 512)
_H_WIDTH = 512
_M_DQ = (0, 512)
_M_DK = (512, 1024)
_M_DV = (1024, 1536)
_M_GQ = (1536, 2048)
_M_GK = (2048, 2176)
_M_GV = (2176, 2304)
_M_U = (2304, 2816)
_M_WIDTH = 2816
_IN_MIX_START = MLA_Q_LORA + MLA_KV_LORA + MLA_ROPE
_IN_GATE_START = _IN_MIX_START + _M_WIDTH
_IN_MERGE_START = _IN_GATE_START + N_BRANCH * BRANCH_WIDTH

SSM_GBLOCK = 8
SSM_NBLK = SSM_GROUPS // SSM_GBLOCK
SSM_SBLK = SSM_GBLOCK * SSM_STATE
SSM_UBLK = SSM_GBLOCK * SSM_GROUP
SSM_SEQ = 256

V_PAD = 128

PROJ_TM = 512
MERGE_TM = 512
ATTN_TQ = 256


def _params(n_axes, vmem=V7X_VMEM_LIMIT_BYTES):
    return pltpu.CompilerParams(dimension_semantics=("arbitrary",) * n_axes, vmem_limit_bytes=vmem)


def _const_spec(shape):
    zeros = (0,) * len(shape)
    return pl.BlockSpec(shape, lambda *_: zeros, pipeline_mode=pl.Buffered(1))


def _layer_spec(shape, l):
    zeros = (0,) * len(shape)
    return pl.BlockSpec((None,) + tuple(shape), lambda *_: (l,) + zeros, pipeline_mode=pl.Buffered(1))


def _mod_spec(n_rows, tm, seq):
    if n_rows == 1:
        return pl.BlockSpec((1, 1, D_MODEL), lambda i: (0, 0, 0))
    return pl.BlockSpec((1, 1, D_MODEL), lambda i: (i * tm // seq, 0, 0))


def _dot(a, b):
    return jnp.dot(a, b, preferred_element_type=F32)


def _dot_nt(a, b):
    return lax.dot_general(a, b, (((1,), (1,)), ((), ())), preferred_element_type=F32)


def _rms(x):
    return x * lax.rsqrt(jnp.mean(x * x, axis=-1, keepdims=True) + EPS)


def _sigmoid(x):
    return 1.0 / (1.0 + jnp.exp(-x))


def _gelu_tanh(x):
    return 0.5 * x * (1.0 + jnp.tanh(math.sqrt(2.0 / math.pi) * (x + 0.044715 * (x * x * x))))


def _group_mean_sq(x, gmat):
    x2 = x * x
    hi = x2.astype(BF16)
    lo = (x2 - hi.astype(F32)).astype(BF16)
    return _dot(hi, gmat) + _dot(lo, gmat)


def _rope_chunk(x, cos, sin, half):
    lane = lax.broadcasted_iota(jnp.int32, x.shape, 1)
    first = (lane % (2 * half)) < half
    swapped = jnp.where(first, pltpu.roll(x, V7X_LANES - half, 1), pltpu.roll(x, half, 1))
    return x * cos + swapped * sin


def _transpose_cast_kernel(w_ref, o_ref):
    o_ref[...] = w_ref[...].T.astype(BF16)


def _transposed_rows(w_t, row0, n_rows, blk):
    return pl.pallas_call(
        _transpose_cast_kernel,
        grid=(DEPTH, n_rows // blk),
        in_specs=[pl.BlockSpec((pl.Element(blk), pl.Element(D_MODEL)),
                               lambda l, j: (pl.multiple_of(l * D_IN + row0 + j * blk, V7X_SUBLANES), 0))],
        out_specs=pl.BlockSpec((None, D_MODEL, blk), lambda l, j: (l, 0, j)),
        out_shape=jax.ShapeDtypeStruct((DEPTH, D_MODEL, n_rows), BF16),
        compiler_params=_params(2),
        name="w_in_columns",
    )(w_t)


def _mod_kernel(c_ref, w_ref, b_ref, o_ref):
    c = c_ref[...]
    a = (c * _sigmoid(c)).astype(BF16)
    o_ref[0] = _dot(a, w_ref[0].astype(BF16)) + b_ref[0]


def _modulation(cvec, w_mod, b_mod):
    tn = 512
    return pl.pallas_call(
        _mod_kernel,
        grid=(DEPTH, 3 * D_MODEL // tn),
        in_specs=[pl.BlockSpec((V7X_SUBLANES, D_MODEL), lambda l, j: (0, 0)),
                  pl.BlockSpec((1, D_MODEL, tn), lambda l, j: (l, 0, j)),
                  pl.BlockSpec((1, 1, tn), lambda l, j: (l, 0, j))],
        out_specs=pl.BlockSpec((1, V7X_SUBLANES, tn), lambda l, j: (l, 0, j)),
        out_shape=jax.ShapeDtypeStruct((DEPTH, V7X_SUBLANES, 3 * D_MODEL), F32),
        compiler_params=_params(2),
        name="modulation",
    )(cvec, w_mod, b_mod.reshape(DEPTH, 1, 3 * D_MODEL))


def _modulated(x, g_ref, sh_ref, sc_ref):
    return (_rms(x) * g_ref[...]) * (1.0 + sc_ref[0]) + sh_ref[0]


def _proj_kernel(latent, n_prev, *refs):
    (x_ref, sh_ref, sc_ref, g_ref, wh_ref, wx_ref, qn_ref, wqb_ref, kvn_ref, wk_ref, wv_ref,
     gqn_ref, gkn_ref, gm_ref, egv_ref) = refs[:15]
    pos = 15
    if latent:
        c_mla, s_mla, c_64, s_64 = refs[pos:pos + 4]
        pos += 4
    pos += n_prev
    (q_o, k_o, v_o, dq_o, dk_o, dv_o, gq_o, gk_o, gv_o, u_o) = refs[pos:pos + 10]
    pos += 10
    if not latent:
        ckv_f, kpe_t, dk_t, dv_f, gk_t, gv_t = refs[pos:pos + 6]

    h = _modulated(x_ref[...], g_ref, sh_ref, sc_ref).astype(BF16)
    tm = h.shape[0]

    def head(cols):
        return _dot(h, wh_ref[:, cols[0]:cols[1]])

    def mix(cols):
        return _dot(h, wx_ref[:, cols[0]:cols[1]])

    def rope_mla(x):
        return _rope_chunk(x, c_mla[...], s_mla[...], MLA_ROPE // 2) if latent else x

    def rope_64(x):
        if not latent:
            return x
        c, s = c_64[...], s_64[...]
        chunks = [_rope_chunk(x[:, i:i + V7X_LANES], c, s, DIFF_HEAD_DIM // 2)
                  for i in range(0, x.shape[1], V7X_LANES)]
        return chunks[0] if len(chunks) == 1 else jnp.concatenate(chunks, axis=1)

    def per_sequence(x):
        return [x[b * SSM_SEQ:(b + 1) * SSM_SEQ] for b in range(tm // SSM_SEQ)]

    qn = (_rms(head(_H_QA)) * qn_ref[...]).astype(BF16)
    q = _dot(qn, wqb_ref[...])
    for hd in range(MLA_HEADS):
        sl = slice(hd * MLA_HEAD_PAD, (hd + 1) * MLA_HEAD_PAD)
        q_o[:, sl] = (rope_mla(q[:, sl]) * MLA_SCALE).astype(BF16)
    ckv = _rms(head(_H_KVA)) * kvn_ref[...]
    kpe = head(_H_KPE)
    if not latent:
        for b, (c_b, k_b) in enumerate(zip(per_sequence(ckv), per_sequence(kpe))):
            ckv_f[b] = c_b
            kpe_t[b] = k_b.T[MLA_NOPE:MLA_NOPE + MLA_ROPE, :]
    kpe = rope_mla(kpe)
    ckv_b = ckv.astype(BF16)
    kn = _dot(ckv_b, wk_ref[...])
    for hd in range(MLA_HEADS):
        sl = slice(hd * MLA_HEAD_PAD, (hd + 1) * MLA_HEAD_PAD)
        k_o[:, sl] = (kn[:, sl] + kpe).astype(BF16)
    v_o[...] = _dot(ckv_b, wv_ref[...]).astype(BF16)

    dq_o[...] = (rope_64(mix(_M_DQ)) * DIFF_SCALE).astype(BF16)
    dk = rope_64(mix(_M_DK))
    dv = mix(_M_DV)
    dk_o[...] = dk.astype(BF16)
    dv_o[...] = dv.astype(BF16)

    gq = mix(_M_GQ)
    gq = gq * lax.rsqrt(_group_mean_sq(gq, gm_ref[...]) + EPS) * gqn_ref[...]
    gq_o[...] = (rope_64(gq) * GQA_SCALE).astype(BF16)
    gk = mix(_M_GK)
    kw = GQA_KV_WIDTH
    gk = gk * lax.rsqrt(_group_mean_sq(gk, gm_ref[:kw, :kw]) + EPS) * gkn_ref[...]
    gv = mix(_M_GV)
    if not latent:
        for b, (dk_b, dv_b, gk_b, gv_b) in enumerate(zip(*(per_sequence(a) for a in (dk, dv, gk, gv)))):
            dk_t[b] = dk_b.T
            for hd in range(DIFF_HEADS):
                dv_f[b, :, hd, :] = dv_b[:, hd * DIFF_V:(hd + 1) * DIFF_V]
            gk_t[b] = gk_b.T
            gv_t[b] = gv_b.T
    gk_o[...] = rope_64(gk).astype(BF16)
    gv_o[...] = _dot(gv.astype(BF16), egv_ref[...]).astype(BF16)

    u_o[...] = mix(_M_U)


def _side_shapes(n_seq):
    return [((n_seq, DEPTH, SSM_SEQ, MLA_KV_LORA), (SSM_SEQ, MLA_KV_LORA)),
            ((n_seq, DEPTH, MLA_ROPE, SSM_SEQ), (MLA_ROPE, SSM_SEQ)),
            ((n_seq, DEPTH, 512, SSM_SEQ), (512, SSM_SEQ)),
            ((n_seq, DEPTH, SSM_SEQ, DIFF_HEADS, DIFF_V), (SSM_SEQ, DIFF_HEADS, DIFF_V)),
            ((n_seq, DEPTH, GQA_KV_WIDTH, SSM_SEQ), (GQA_KV_WIDTH, SSM_SEQ)),
            ((n_seq, DEPTH, GQA_KV_WIDTH, SSM_SEQ), (GQA_KV_WIDTH, SSM_SEQ))]


def _proj(x, shift, scale, seq, l, w, rope, latent, prev_sides=()):
    n = x.shape[0]
    tm = PROJ_TM
    row = lambda i: (i, 0)
    mod_spec = _mod_spec(shift.shape[0], tm, seq)
    kw = GQA_KV_WIDTH
    in_specs = [pl.BlockSpec((tm, D_MODEL), row), mod_spec, mod_spec,
                _layer_spec((1, D_MODEL), l), _layer_spec((D_MODEL, _H_WIDTH), l), _layer_spec((D_MODEL, _M_WIDTH), l),
                _layer_spec((1, MLA_Q_LORA), l), _layer_spec((MLA_Q_LORA, MLA_HEADS * MLA_HEAD_PAD), l),
                _layer_spec((1, MLA_KV_LORA), l), _layer_spec((MLA_KV_LORA, MLA_HEADS * MLA_HEAD_PAD), l),
                _layer_spec((MLA_KV_LORA, MLA_HEADS * V_PAD), l),
                _layer_spec((1, 512), l), _layer_spec((1, kw), l), _const_spec((512, 512)),
                _const_spec((kw, 2 * GQA_KV_HEADS * V_PAD))]
    args = [x, shift, scale, w["norm_g"], w["whead"], w["wmix"], w["qn"], w["wqb"], w["kvn"], w["wk"], w["wv"],
            w["gqn"], w["gkn"], w["gmat"], w["egv"]]
    if latent:
        per_seq = seq // tm
        tab = pl.BlockSpec((tm, V7X_LANES), lambda i: (i % per_seq, 0))
        in_specs += [tab] * 4
        args += list(rope)
    n_fixed = len(args)
    in_specs += [pl.BlockSpec(memory_space=pl.ANY)] * len(prev_sides)
    args += list(prev_sides)
    widths = [(1024, BF16), (1024, BF16), (MLA_HEADS * V_PAD, BF16), (512, BF16), (512, BF16),
              (512, BF16), (512, BF16), (kw, BF16), (2 * GQA_KV_HEADS * V_PAD, BF16), (512, F32)]
    out_specs = [pl.BlockSpec((tm, wd), row) for wd, _ in widths]
    out_shape = [jax.ShapeDtypeStruct((n, wd), dt) for wd, dt in widths]
    aliases = {}
    if not latent:
        per_tile = tm // SSM_SEQ
        for full, blk in _side_shapes(n // SSM_SEQ):
            zeros = (0,) * len(blk)
            out_specs.append(pl.BlockSpec((per_tile, None) + blk, lambda i, zeros=zeros: (i, l) + zeros))
            out_shape.append(jax.ShapeDtypeStruct(full, F32))
        aliases = {n_fixed + k: len(widths) + k for k in range(len(prev_sides))}
    return pl.pallas_call(
        functools.partial(_proj_kernel, latent, len(prev_sides)),
        grid=(n // tm,),
        in_specs=in_specs, out_specs=out_specs, out_shape=out_shape,
        input_output_aliases=aliases,
        compiler_params=_params(1),
        name="proj_latent" if latent else "proj_context",
    )(*args)


def _ctx_kv_kernel(ckv_ref, kpe_ref, wk_ref, wv_ref, k_o, v_o):
    ckv_b = ckv_ref[...].astype(BF16)
    kn = _dot(ckv_b, wk_ref[...])
    kpe = kpe_ref[...]
    for hd in range(MLA_HEADS):
        sl = slice(hd * MLA_HEAD_PAD, (hd + 1) * MLA_HEAD_PAD)
        k_o[:, sl] = (kn[:, sl] + kpe).astype(BF16)
    v_o[...] = _dot(ckv_b, wv_ref[...]).astype(BF16)


def _ctx_kv(ckv, kpe_pad, l, w):
    n = ckv.shape[0]
    tm = 512
    row = lambda i: (i, 0)
    return pl.pallas_call(
        _ctx_kv_kernel,
        grid=(n // tm,),
        in_specs=[pl.BlockSpec((tm, MLA_KV_LORA), row), pl.BlockSpec((tm, MLA_HEAD_PAD), row),
                  _layer_spec((MLA_KV_LORA, MLA_HEADS * MLA_HEAD_PAD), l),
                  _layer_spec((MLA_KV_LORA, MLA_HEADS * V_PAD), l)],
        out_specs=[pl.BlockSpec((tm, MLA_HEADS * MLA_HEAD_PAD), row), pl.BlockSpec((tm, MLA_HEADS * V_PAD), row)],
        out_shape=[jax.ShapeDtypeStruct((n, MLA_HEADS * MLA_HEAD_PAD), BF16),
                   jax.ShapeDtypeStruct((n, MLA_HEADS * V_PAD), BF16)],
        compiler_params=_params(1),
        name="ctx_kv",
    )(ckv, kpe_pad, w["wk"], w["wv"])


def _exp_scores(q, k):
    s = _dot_nt(q, k)
    return jnp.exp(s - jnp.max(s, axis=-1, keepdims=True)).astype(BF16)


def _weighted_values(p, v_block):
    ones = jnp.ones((v_block.shape[0], V7X_LANES), BF16)
    o = _dot(p, jnp.concatenate([v_block, ones], axis=1))
    return o[:, :V7X_LANES], o[:, V7X_LANES:]


def _attn_heads_kernel(heads, q_width, q_ref, k_ref, v_ref, o_ref):
    for pair in range(len(heads) // 2):
        chunk = None
        for hd in (2 * pair, 2 * pair + 1):
            kb, vb = heads[hd]
            q = q_ref[0, :, hd * q_width:(hd + 1) * q_width]
            k = k_ref[0, :, kb * q_width:(kb + 1) * q_width]
            o, total = _weighted_values(_exp_scores(q, k), v_ref[0, :, vb * V_PAD:(vb + 1) * V_PAD])
            term = o * (1.0 / total)
            chunk = term if chunk is None else chunk + term
        o_ref[0, :, pair * V7X_LANES:(pair + 1) * V7X_LANES] = chunk.astype(BF16)


def _diff_attn_kernel(lam_init, q_ref, k_ref, v_ref, lp_ref, sub_ref, o_ref):
    lp = lp_ref[...]
    lam = (jnp.exp(jnp.sum(lp[0:1] * lp[1:2], axis=-1, keepdims=True))
           - jnp.exp(jnp.sum(lp[2:3] * lp[3:4], axis=-1, keepdims=True)) + lam_init)
    d = DIFF_HEAD_DIM
    for hd in range(DIFF_HEADS):
        c1, c2 = 2 * hd * d, (2 * hd + 1) * d
        v_block = v_ref[0, :, hd * DIFF_V:(hd + 1) * DIFF_V]
        o1, t1 = _weighted_values(_exp_scores(q_ref[0, :, c1:c1 + d], k_ref[0, :, c1:c1 + d]), v_block)
        o2, t2 = _weighted_values(_exp_scores(q_ref[0, :, c2:c2 + d], k_ref[0, :, c2:c2 + d]), v_block)
        o = o1 * (1.0 / t1) - o2 * (lam / t2)
        o = _rms(o) * sub_ref[...] * (1.0 - lam_init)
        o_ref[0, :, hd * DIFF_V:(hd + 1) * DIFF_V] = o.astype(BF16)


def _attention(kernel, name, q, k, v, extra_specs=(), extra=()):
    b, t, wq = q.shape
    s = k.shape[1]
    tq = ATTN_TQ
    in_specs = [pl.BlockSpec((1, tq, wq), lambda i, j: (i, j, 0)),
                pl.BlockSpec((1, s, k.shape[2]), lambda i, j: (i, 0, 0)),
                pl.BlockSpec((1, s, v.shape[2]), lambda i, j: (i, 0, 0))]
    in_specs += list(extra_specs)
    return pl.pallas_call(
        kernel,
        grid=(b, t // tq),
        in_specs=in_specs,
        out_specs=pl.BlockSpec((1, tq, 512), lambda i, j: (i, j, 0)),
        out_shape=jax.ShapeDtypeStruct((b, t, 512), BF16),
        compiler_params=_params(2),
        name=name,
    )(q, k, v, *extra)


def _ssm_kernel(chunked, want_final, *refs):
    u_ref, bm_ref, ab_ref, cm_ref, d_ref = refs[:5]
    pos = 5
    if chunked:
        s0_ref = refs[pos]
        pos += 1
    y_ref = refs[pos]
    pos += 1
    if want_final:
        fin_ref = refs[pos]
        pos += 1
    hs_ref = refs[pos]

    seq, rows, sb = SSM_SEQ, V7X_SUBLANES, SSM_SBLK
    u2 = u_ref[...].reshape(seq * rows, SSM_UBLK)
    u2b = u2.astype(BF16)
    for dr in range(2):
        hs_ref[dr] = _dot(u2b, bm_ref[dr, 0])

    a_parts = []
    for dr in range(2):
        a = ab_ref[dr, 0]
        a_parts.append((jnp.broadcast_to(a[:, :sb], (rows, sb)), jnp.broadcast_to(a[:, sb:], (rows, sb))))

    def row_slices(i):
        return pl.ds(pl.multiple_of(i * rows, rows), rows), pl.ds(pl.multiple_of((seq - 1 - i) * rows, rows), rows)

    def scan_step(i, carry):
        new = []
        for dr, sl in enumerate(row_slices(i)):
            ar, ai = a_parts[dr]
            hr, hi = carry[2 * dr], carry[2 * dr + 1]
            b = hs_ref[dr, sl, :]
            nr = ar * hr - ai * hi + b[:, :sb]
            ni = ar * hi + ai * hr + b[:, sb:]
            hs_ref[dr, sl, :sb] = nr
            hs_ref[dr, sl, sb:] = ni
            new += [nr, ni]
        return tuple(new)

    zero = jnp.zeros((rows, sb), F32)
    fin = lax.fori_loop(0, seq, scan_step, (zero, zero, zero, zero), unroll=2)

    if want_final:
        for dr in range(2):
            fin_ref[dr, 0, :, :sb] = fin[2 * dr]
            fin_ref[dr, 0, :, sb:] = fin[2 * dr + 1]

    if chunked:
        entry = []
        for dr in range(2):
            ar, ai = a_parts[dr]
            pr, pi = ar[0:1], ai[0:1]
            for _ in range(int(math.log2(seq))):
                pr, pi = pr * pr - pi * pi, 2.0 * pr * pi
            fr, fi = fin[2 * dr], fin[2 * dr + 1]
            s0 = s0_ref[dr, 0, 0]
            er, ei = s0[:, :sb], s0[:, sb:]
            order = range(rows) if dr == 0 else range(rows - 1, -1, -1)
            rows_r, rows_i = [None] * rows, [None] * rows
            for c in order:
                rows_r[c], rows_i[c] = er, ei
                er, ei = (pr * er - pi * ei + fr[c:c + 1], pr * ei + pi * er + fi[c:c + 1])
            entry += [jnp.concatenate(rows_r, axis=0), jnp.concatenate(rows_i, axis=0)]

        def fix_step(i, carry):
            new = []
            for dr, sl in enumerate(row_slices(i)):
                ar, ai = a_parts[dr]
                zr, zi = carry[2 * dr], carry[2 * dr + 1]
                nr = ar * zr - ai * zi
                ni = ar * zi + ai * zr
                hs_ref[dr, sl, :sb] += nr
                hs_ref[dr, sl, sb:] += ni
                new += [nr, ni]
            return tuple(new)

        lax.fori_loop(0, seq, fix_step, tuple(entry), unroll=2)

    y = _dot(hs_ref[0].astype(BF16), cm_ref[0, 0]) + _dot(hs_ref[1].astype(BF16), cm_ref[1, 0])
    y = _gelu_tanh(y + d_ref[...] * u2)
    y_ref[...] = y.reshape(seq, rows, SSM_UBLK)


def _ssm(u_t, l, w, s0, want_final):
    seq, r, _ = u_t.shape
    chunked = s0 is not None
    rows = V7X_SUBLANES
    in_specs = [pl.BlockSpec((seq, rows, SSM_UBLK), lambda i, j: (0, i, j)),
                pl.BlockSpec((None, 2, 1, SSM_UBLK, 2 * SSM_SBLK), lambda i, j: (l, 0, j, 0, 0)),
                pl.BlockSpec((None, 2, 1, 1, 2 * SSM_SBLK), lambda i, j: (l, 0, j, 0, 0)),
                pl.BlockSpec((None, 2, 1, 2 * SSM_SBLK, SSM_UBLK), lambda i, j: (l, 0, j, 0, 0)),
                pl.BlockSpec((None, 1, SSM_UBLK), lambda i, j: (l, 0, j))]
    args = [u_t, w["ssm_b"], w["ssm_a"], w["ssm_c"], w["ssm_d"]]
    if chunked:
        in_specs.append(pl.BlockSpec((None, 2, 1, 1, 1, 2 * SSM_SBLK), lambda i, j: (l, 0, i, j, 0, 0)))
        args.append(s0)
    out_specs = [pl.BlockSpec((seq, rows, SSM_UBLK), lambda i, j: (0, i, j))]
    out_shape = [jax.ShapeDtypeStruct((seq, r, SSM_WIDTH), F32)]
    if want_final:
        out_specs.append(pl.BlockSpec((2, 1, rows, 2 * SSM_SBLK), lambda i, j: (0, j, i, 0)))
        out_shape.append(jax.ShapeDtypeStruct((2, SSM_NBLK, r, 2 * SSM_SBLK), F32))
    return pl.pallas_call(
        functools.partial(_ssm_kernel, chunked, want_final),
        grid=(r // rows, SSM_NBLK),
        in_specs=in_specs, out_specs=out_specs, out_shape=out_shape,
        scratch_shapes=[pltpu.VMEM((2, seq * rows, 2 * SSM_SBLK), F32)],
        compiler_params=_params(2),
        name="ssm_latent" if chunked else "ssm_context",
    )(*args)


def _merge_kernel(last, x_ref, sh_ref, sc_ref, ga_ref, g_ref, oa_ref, ob_ref, oc_ref, y_ref,
                  wg_ref, wm_ref, wglu_ref, bglu_ref, wbr_ref, wout_ref, fn_ref, o_ref):
    x = x_ref[...]
    h = _modulated(x, g_ref, sh_ref, sc_ref).astype(BF16)
    y = y_ref[...]
    o_d = y * _sigmoid(_dot(y.astype(BF16), wglu_ref[...]) + bglu_ref[...])
    branches = (oa_ref[...].astype(F32), ob_ref[...].astype(F32), oc_ref[...].astype(F32), o_d)
    acc = None
    for n, o in enumerate(branches):
        gate = _dot(h, wg_ref[:, n * BRANCH_WIDTH:(n + 1) * BRANCH_WIDTH])
        br = _dot((o * (gate * _sigmoid(gate))).astype(BF16), wbr_ref[n])
        term = _sigmoid(_dot(h, wm_ref[:, n * D_MODEL:(n + 1) * D_MODEL])) * br
        acc = term if acc is None else acc + term
    xn = x + ga_ref[0] * _dot(acc.astype(BF16), wout_ref[...])
    if last:
        xn = _rms(xn) * fn_ref[...]
    o_ref[...] = xn


def _merge(x, shift, scale, gate, seq, oa, ob, oc, y, l, w, final_norm, last):
    n = x.shape[0]
    tm = MERGE_TM
    row = lambda i: (i, 0)
    mod_spec = _mod_spec(shift.shape[0], tm, seq)
    in_specs = [pl.BlockSpec((tm, D_MODEL), row), mod_spec, mod_spec, mod_spec, _layer_spec((1, D_MODEL), l),
                pl.BlockSpec((tm, 512), row), pl.BlockSpec((tm, 512), row), pl.BlockSpec((tm, 512), row),
                pl.BlockSpec((tm, 512), row),
                _layer_spec((D_MODEL, N_BRANCH * BRANCH_WIDTH), l), _layer_spec((D_MODEL, N_BRANCH * D_MODEL), l),
                _layer_spec((SSM_WIDTH, SSM_WIDTH), l), _layer_spec((1, SSM_WIDTH), l),
                _layer_spec((N_BRANCH, BRANCH_WIDTH, D_MODEL), l), _layer_spec((D_MODEL, D_MODEL), l),
                _const_spec((1, D_MODEL))]
    return pl.pallas_call(
        functools.partial(_merge_kernel, last),
        grid=(n // tm,),
        in_specs=in_specs,
        out_specs=pl.BlockSpec((tm, D_MODEL), row),
        out_shape=jax.ShapeDtypeStruct((n, D_MODEL), F32),
        compiler_params=_params(1),
        name="merge",
    )(x, shift, scale, gate, w["norm_g"], oa, ob, oc, y, w["wg"], w["wm"], w["wglu"], w["bglu"],
      w["wbr"], w["wout"], final_norm)


def _rope_tables(n_tok, rot_dim):
    rows = n_tok // GRID_W
    row = jnp.repeat(jnp.arange(rows, dtype=F32), GRID_W)
    col = jnp.tile(jnp.arange(GRID_W, dtype=F32), rows)
    quarter = rot_dim // 4
    inv = ROPE_THETA ** (-jnp.arange(quarter, dtype=F32) / quarter)
    ang = jnp.concatenate([row[:, None] * inv, col[:, None] * inv], axis=-1)
    return jnp.cos(ang), jnp.sin(ang)


def _rope_lane_tables(n_tok):
    c, s = _rope_tables(n_tok, MLA_ROPE)
    ones = jnp.ones((n_tok, MLA_NOPE), F32)
    pad = MLA_HEAD_PAD - MLA_NOPE - MLA_ROPE
    c_mla = jnp.concatenate([ones, c, c, jnp.ones((n_tok, pad), F32)], axis=1)
    s_mla = jnp.concatenate([0.0 * ones, -s, s, jnp.zeros((n_tok, pad), F32)], axis=1)
    c, s = _rope_tables(n_tok, DIFF_HEAD_DIM)
    c_64 = jnp.concatenate([c, c, c, c], axis=1)
    s_64 = jnp.concatenate([-s, s, -s, s], axis=1)
    return c_mla, s_mla, c_64, s_64


def _block_diag(blocks):
    g = blocks.shape[-3]
    eye = jnp.eye(g, dtype=blocks.dtype)
    out = jnp.einsum("...grc,gh->...grhc", blocks, eye)
    return out.reshape(blocks.shape[:-3] + (g * blocks.shape[-2], g * blocks.shape[-1]))


def _in_output_half(v):
    zeros = jnp.zeros_like(v[..., 0, :])
    return jnp.stack([jnp.concatenate([v[..., 0, :], zeros], axis=-1),
                      jnp.concatenate([zeros, v[..., 1, :]], axis=-1)], axis=-2)


def _weights(w_in, p):
    w_t = jnp.swapaxes(w_in, 1, 2).reshape(DEPTH * D_IN, D_MODEL)
    wmix = _transposed_rows(w_t, _IN_MIX_START, _M_WIDTH, _M_WIDTH // 2)
    wg = _transposed_rows(w_t, _IN_GATE_START, N_BRANCH * BRANCH_WIDTH, 1024)
    wm = _transposed_rows(w_t, _IN_MERGE_START, N_BRANCH * D_MODEL, 1024)
    zeros = lambda wd: jnp.zeros((DEPTH, D_MODEL, wd), F32)
    kpe_lo = MLA_Q_LORA + MLA_KV_LORA
    whead = jnp.concatenate([w_in[:, :, :kpe_lo], zeros(MLA_NOPE), w_in[:, :, kpe_lo:_IN_MIX_START],
                             zeros(MLA_HEAD_PAD - MLA_NOPE - MLA_ROPE)], axis=2).astype(BF16)

    hq = MLA_NOPE + MLA_ROPE
    wqb = p["w_mla_q_b"].reshape(DEPTH, MLA_Q_LORA, MLA_HEADS, hq)
    wqb = jnp.pad(wqb, ((0, 0), (0, 0), (0, 0), (0, MLA_HEAD_PAD - hq))).reshape(DEPTH, MLA_Q_LORA, -1).astype(BF16)
    wkv = p["w_mla_kv_b"].reshape(DEPTH, MLA_KV_LORA, MLA_HEADS, MLA_NOPE + MLA_V)
    wk = jnp.pad(wkv[..., :MLA_NOPE], ((0, 0), (0, 0), (0, 0), (0, MLA_HEAD_PAD - MLA_NOPE)))
    wk = wk.reshape(DEPTH, MLA_KV_LORA, -1).astype(BF16)
    wv = _in_output_half(wkv[..., MLA_NOPE:].reshape(DEPTH, MLA_KV_LORA, MLA_HEADS // 2, 2, MLA_V))
    wv = wv.reshape(DEPTH, MLA_KV_LORA, -1).astype(BF16)
    eye = jnp.eye(GQA_HEAD_DIM, dtype=F32)
    egv = _block_diag(jnp.broadcast_to(
        _in_output_half(jnp.stack([eye, eye], axis=1)).reshape(GQA_HEAD_DIM, 2 * V_PAD),
        (GQA_KV_HEADS, GQA_HEAD_DIM, 2 * V_PAD))).astype(BF16)

    gmat =_block_diag(jnp.full((GQA_HEADS, GQA_HEAD_DIM, GQA_HEAD_DIM), 1.0 / GQA_HEAD_DIM, F32)).astype(BF16)

    lam = lax.complex(p["ssm_a_re"], p["ssm_a_im"])
    dt = jnp.exp(p["ssm_log_dt"])[..., None]
    abar = jnp.exp(lam * dt)
    bbar = ((abar - 1.0) / lam)[..., None] * lax.complex(p["ssm_b_re"], p["ssm_b_im"])
    blk = lambda a: a.reshape((DEPTH, 2, SSM_NBLK, SSM_GBLOCK) + a.shape[3:])
    b_t = jnp.swapaxes(blk(bbar), -1, -2)
    ssm_b = jnp.concatenate([_block_diag(jnp.real(b_t)), _block_diag(jnp.imag(b_t))], axis=-1).astype(BF16)
    ab = blk(abar).reshape(DEPTH, 2, SSM_NBLK, 1, SSM_SBLK)
    ssm_a = jnp.concatenate([jnp.real(ab), jnp.imag(ab)], axis=-1)
    c_re = jnp.swapaxes(blk(p["ssm_c_re"]), -1, -2)
    c_im = jnp.swapaxes(blk(p["ssm_c_im"]), -1, -2)
    ssm_c = jnp.concatenate([_block_diag(c_re), -_block_diag(c_im)], axis=-2).astype(BF16)

    row = lambda a: a[:, None, :]
    return dict(
        norm_g=row(p["norm_g"]), whead=whead, wmix=wmix, qn=row(p["mla_q_norm"]), wqb=wqb,
        kvn=row(p["mla_kv_norm"]), wk=wk, wv=wv,
        gqn=row(jnp.tile(p["gqa_q_norm"], (1, GQA_HEADS))), gkn=row(jnp.tile(p["gqa_k_norm"], (1, GQA_KV_HEADS))),
        gmat=gmat, egv=egv,
        lam_parts=jnp.stack([p["diff_lq1"], p["diff_lk1"], p["diff_lq2"], p["diff_lk2"]], axis=1),
        subln=row(p["diff_subln"]),
        ssm_b=ssm_b, ssm_a=ssm_a, ssm_c=ssm_c, ssm_d=row(p["ssm_d"]),
        wg=wg, wm=wm, wglu=p["ssm_glu_w"].astype(BF16), bglu=row(p["ssm_glu_b"]),
        wbr=p["w_branch_out"].astype(BF16), wout=p["w_out"].astype(BF16),
    )


def _mixers(pr, b, t, l, w, lam_init, ctx):
    q, k, v, dq, dk, dv, gq, gk, gv, u = pr[:10]
    three = lambda a: a.reshape(b, t, a.shape[-1])
    k, v, dk, dv, gk, gv = (three(a) for a in (k, v, dk, dv, gk, gv))
    if ctx is not None:
        cat = lambda c, a: jnp.concatenate([c, a], axis=1)
        k, v, dk, dv, gk, gv = (cat(c, a) for c, a in zip(ctx["kv"], (k, v, dk, dv, gk, gv)))
    mla_heads = tuple((hd, hd) for hd in range(MLA_HEADS))
    gqa_heads = tuple((hd // GQA_GROUP, 2 * (hd // GQA_GROUP) + hd % 2) for hd in range(GQA_HEADS))
    o_a = _attention(functools.partial(_attn_heads_kernel, mla_heads, MLA_HEAD_PAD), "attn_mla",
                     three(q), k, v)
    o_b = _attention(functools.partial(_diff_attn_kernel, lam_init), "attn_diff", three(dq), dk, dv,
                     extra_specs=(_layer_spec((4, DIFF_HEAD_DIM), l), _layer_spec((1, DIFF_V), l)),
                     extra=(w["lam_parts"], w["subln"]))
    o_c = _attention(functools.partial(_attn_heads_kernel, gqa_heads, GQA_HEAD_DIM), "attn_gqa",
                     three(gq), gk, gv)
    n = b * t
    if ctx is None:
        u_t = jnp.swapaxes(u.reshape(b, t, SSM_WIDTH), 0, 1)
        y_t, fin = _ssm(u_t, l, w, None, True)
        y = jnp.swapaxes(y_t, 0, 1).reshape(n, SSM_WIDTH)
    else:
        chunks = t // SSM_SEQ
        u_t = jnp.transpose(u.reshape(b, chunks, SSM_SEQ, SSM_WIDTH), (2, 0, 1, 3)).reshape(SSM_SEQ, b * chunks, SSM_WIDTH)
        (y_t,) = _ssm(u_t, l, w, ctx["s0"], False)
        y = jnp.transpose(y_t.reshape(SSM_SEQ, b, chunks, SSM_WIDTH), (1, 2, 0, 3)).reshape(n, SSM_WIDTH)
        fin = None
    flat = lambda a: a.reshape(n, 512)
    return flat(o_a), flat(o_b), flat(o_c), y, fin


def kernel(x_prompt, x_sample, cache_mla_ckv, cache_mla_krope, cache_diff_k, cache_diff_v, cache_gqa_k, cache_gqa_v, state_ssm, c, c_ctx, norm_g, w_mod, b_mod, w_in, mla_q_norm, w_mla_q_b, mla_kv_norm, w_mla_kv_b, diff_lq1, diff_lk1, diff_lq2, diff_lk2, diff_subln, gqa_q_norm, gqa_k_norm, ssm_a_re, ssm_a_im, ssm_log_dt, ssm_b_re, ssm_b_im, ssm_c_re, ssm_c_im, ssm_d, ssm_glu_w, ssm_glu_b, w_branch_out, w_out, final_norm):
    p = dict(norm_g=norm_g, mla_q_norm=mla_q_norm, w_mla_q_b=w_mla_q_b, mla_kv_norm=mla_kv_norm,
             w_mla_kv_b=w_mla_kv_b, diff_lq1=diff_lq1, diff_lk1=diff_lk1, diff_lq2=diff_lq2, diff_lk2=diff_lk2,
             diff_subln=diff_subln, gqa_q_norm=gqa_q_norm, gqa_k_norm=gqa_k_norm, ssm_a_re=ssm_a_re,
             ssm_a_im=ssm_a_im, ssm_log_dt=ssm_log_dt, ssm_b_re=ssm_b_re, ssm_b_im=ssm_b_im, ssm_c_re=ssm_c_re,
             ssm_c_im=ssm_c_im, ssm_d=ssm_d, ssm_glu_w=ssm_glu_w, ssm_glu_b=ssm_glu_b,
             w_branch_out=w_branch_out, w_out=w_out)
    bp, tp, _ = x_prompt.shape
    bs, ts, _ = x_sample.shape
    past = cache_mla_ckv.shape[2]
    assert tp == SSM_SEQ and ts % SSM_SEQ == 0 and ts // SSM_SEQ == V7X_SUBLANES
    rope = _rope_lane_tables(ts)
    w = _weights(w_in, p)

    cvec = jnp.zeros((V7X_SUBLANES, D_MODEL), F32).at[0].set(c_ctx).at[1:1 + bs].set(c)
    mod = _modulation(cvec, w_mod, b_mod)

    pad_lo = MLA_NOPE
    pad_hi = MLA_HEAD_PAD - MLA_NOPE - MLA_ROPE
    kpe_ctx = jnp.pad(cache_mla_krope, ((0, 0), (0, 0), (0, 0), (pad_lo, pad_hi)))
    flat_ctx = lambda a: a.reshape(bs, DEPTH, past, -1).astype(BF16)
    gv_both = _in_output_half(jnp.stack([cache_gqa_v, cache_gqa_v], axis=-2))
    dk_ctx, dv_ctx, gk_ctx, gv_ctx = (flat_ctx(a) for a in (cache_diff_k, cache_diff_v, cache_gqa_k, gv_both))
    s0_all = state_ssm.reshape(bs, DEPTH, 2, SSM_NBLK, SSM_GBLOCK * SSM_STATE, 2)
    s0_all = jnp.transpose(s0_all, (1, 2, 0, 3, 5, 4)).reshape(DEPTH, 2, bs, SSM_NBLK, 1, 2 * SSM_SBLK)

    xp = x_prompt.reshape(bp * tp, D_MODEL)
    xs = x_sample.reshape(bs * ts, D_MODEL)
    fn = final_norm[None]
    sides = ()
    states = []
    for l in range(DEPTH):
        lam_init = 0.8 - 0.6 * math.exp(-0.3 * l)
        last = l == DEPTH - 1
        sh, sc, ga = (mod[l, :, i * D_MODEL:(i + 1) * D_MODEL][:, None, :] for i in range(3))

        pr = _proj(xp, sh[0:1], sc[0:1], tp, l, w, None, False, sides)
        o_a, o_b, o_c, y, fin = _mixers(pr, bp, tp, l, w, lam_init, None)
        xp = _merge(xp, sh[0:1], sc[0:1], ga[0:1], tp, o_a, o_b, o_c, y, l, w, fn, last)
        sides = tuple(pr[10:])
        fin = fin.reshape(2, SSM_NBLK, bp, 2, SSM_GBLOCK, SSM_STATE)
        states.append(jnp.transpose(fin, (2, 0, 1, 4, 5, 3)).reshape(bp, 2, SSM_GROUPS, SSM_STATE, 2))

        k_ctx, v_ctx = _ctx_kv(cache_mla_ckv[:, l].reshape(bs * past, MLA_KV_LORA),
                               kpe_ctx[:, l].reshape(bs * past, MLA_HEAD_PAD), l, w)
        ctx = dict(
            kv=(k_ctx.reshape(bs, past, -1), v_ctx.reshape(bs, past, -1),
                dk_ctx[:, l], dv_ctx[:, l], gk_ctx[:, l], gv_ctx[:, l]),
            s0=s0_all)
        pr = _proj(xs, sh[1:1 + bs], sc[1:1 + bs], ts, l, w, rope, True)
        o_a, o_b, o_c, y, _ = _mixers(pr, bs, ts, l, w, lam_init, ctx)
        xs = _merge(xs, sh[1:1 + bs], sc[1:1 + bs], ga[1:1 + bs], ts, o_a, o_b, o_c, y, l, w, fn, last)

    ckv_s, kpe_t, dk_t, dv_s, gk_t, gv_t = sides
    untranspose = lambda a: jnp.swapaxes(a, 2, 3)
    return (xp.reshape(bp, tp, D_MODEL), xs.reshape(bs, ts, D_MODEL),
            ckv_s, untranspose(kpe_t),
            untranspose(dk_t).reshape(bp, DEPTH, tp, DIFF_HEADS, 2, DIFF_HEAD_DIM), dv_s,
            untranspose(gk_t).reshape(bp, DEPTH, tp, GQA_KV_HEADS, GQA_HEAD_DIM),
            untranspose(gv_t).reshape(bp, DEPTH, tp, GQA_KV_HEADS, GQA_HEAD_DIM),
            jnp.stack(states, axis=1))
```

```python
import functools
import math

import jax
import jax.numpy as jnp
from jax import lax
from jax.experimental import pallas as pl
from jax.experimental.pallas import tpu as pltpu

F32 = jnp.float32
BF16 = jnp.bfloat16

D_MODEL = 1024
DEPTH = 2
GRID_W = 64
ROPE_THETA = 10000.0
EPS = 1e-6

MLA_HEADS = 8
MLA_NOPE = 64
MLA_ROPE = 32
MLA_V = 64
MLA_Q_LORA = 256
MLA_KV_LORA = 128
MLA_SCALE = (MLA_NOPE + MLA_ROPE) ** -0.5
MLA_HEAD_PAD = 128
DIFF_HEADS = 4
DIFF_HEAD_DIM = 64
DIFF_V = 2 * DIFF_HEAD_DIM
DIFF_SCALE = DIFF_HEAD_DIM ** -0.5
GQA_HEADS = 8
GQA_KV_HEADS = 2
GQA_HEAD_DIM = 64
GQA_GROUP = GQA_HEADS // GQA_KV_HEADS
GQA_KV_WIDTH = GQA_KV_HEADS * GQA_HEAD_DIM
GQA_SCALE = GQA_HEAD_DIM ** -0.5
SSM_WIDTH = 512
SSM_GROUP = 16
SSM_GROUPS = SSM_WIDTH // SSM_GROUP
SSM_STATE = 64
N_BRANCH = 4
BRANCH_WIDTH = 512

IN_SPLITS = (MLA_Q_LORA, MLA_KV_LORA, MLA_ROPE, 512, 512, 512, 512, 128, 128, SSM_WIDTH,
             N_BRANCH * BRANCH_WIDTH, N_BRANCH * D_MODEL)
D_IN = sum(IN_SPLITS)

V7X_LANES = 128
V7X_SUBLANES = 8
V7X_VMEM_LIMIT_BYTES = 56 * 1024 * 1024

_H_QA = (0, 256)
_H_KVA = (256, 384)
_H_KPE = (384, 512)
_H_WIDTH = 512
_M_DQ = (0, 512)
_M_DK = (512, 1024)
_M_DV = (1024, 1536)
_M_GQ = (1536, 2048)
_M_GK = (2048, 2176)
_M_GV = (2176, 2304)
_M_U = (2304, 2816)
_M_WIDTH = 2816
_IN_MIX_START = MLA_Q_LORA + MLA_KV_LORA + MLA_ROPE
_IN_GATE_START = _IN_MIX_START + _M_WIDTH
_IN_MERGE_START = _IN_GATE_START + N_BRANCH * BRANCH_WIDTH

SSM_GBLOCK = 8
SSM_NBLK = SSM_GROUPS // SSM_GBLOCK
SSM_SBLK = SSM_GBLOCK * SSM_STATE
SSM_UBLK = SSM_GBLOCK * SSM_GROUP
SSM_SEQ = 256

V_PAD = 128

PROJ_TM = 512
MERGE_TM = 512
ATTN_TQ = 256
ATTN_SEQ_PER_STEP = 4
ATTN_GROUP_SHORT = 32
ATTN_GROUP_LONG = 4


def _params(n_axes, vmem=V7X_VMEM_LIMIT_BYTES):
    return pltpu.CompilerParams(dimension_semantics=("arbitrary",) * n_axes, vmem_limit_bytes=vmem)


def _const_spec(shape):
    zeros = (0,) * len(shape)
    return pl.BlockSpec(shape, lambda *_: zeros, pipeline_mode=pl.Buffered(1))


def _layer_spec(shape, l):
    zeros = (0,) * len(shape)
    return pl.BlockSpec((None,) + tuple(shape), lambda *_: (l,) + zeros, pipeline_mode=pl.Buffered(1))


def _mod_spec(n_rows, tm, seq):
    if n_rows == 1:
        return pl.BlockSpec((1, 1, D_MODEL), lambda i: (0, 0, 0))
    return pl.BlockSpec((1, 1, D_MODEL), lambda i: (i * tm // seq, 0, 0))


def _dot(a, b):
    return jnp.dot(a, b, preferred_element_type=F32)


def _dot_nt(a, b):
    return lax.dot_general(a, b, (((1,), (1,)), ((), ())), preferred_element_type=F32)


def _rms(x):
    return x * lax.rsqrt(jnp.mean(x * x, axis=-1, keepdims=True) + EPS)


def _sigmoid(x):
    return 1.0 / (1.0 + jnp.exp(-x))


def _gelu_tanh(x):
    return 0.5 * x * (1.0 + jnp.tanh(math.sqrt(2.0 / math.pi) * (x + 0.044715 * (x * x * x))))


def _group_mean_sq(x, gmat):
    return _dot((x * x).astype(BF16), gmat)


def _rope_chunk(x, cos, sin, half):
    lane = lax.broadcasted_iota(jnp.int32, x.shape, 1)
    first = (lane % (2 * half)) < half
    swapped = jnp.where(first, pltpu.roll(x, V7X_LANES - half, 1), pltpu.roll(x, half, 1))
    return x * cos + swapped * sin


def _transpose_cast_kernel(w_ref, o_ref):
    o_ref[...] = w_ref[...].T.astype(BF16)


def _transposed_rows(w_t, row0, n_rows, blk):
    return pl.pallas_call(
        _transpose_cast_kernel,
        grid=(DEPTH, n_rows // blk),
        in_specs=[pl.BlockSpec((pl.Element(blk), pl.Element(D_MODEL)),
                               lambda l, j: (pl.multiple_of(l * D_IN + row0 + j * blk, V7X_SUBLANES), 0))],
        out_specs=pl.BlockSpec((None, D_MODEL, blk), lambda l, j: (l, 0, j)),
        out_shape=jax.ShapeDtypeStruct((DEPTH, D_MODEL, n_rows), BF16),
        compiler_params=_params(2),
        name="w_in_columns",
    )(w_t)


def _mod_kernel(c_ref, w_ref, b_ref, o_ref):
    c = c_ref[...]
    a = (c * _sigmoid(c)).astype(BF16)
    o_ref[0] = _dot(a, w_ref[0].astype(BF16)) + b_ref[0]


def _modulation(cvec, w_mod, b_mod):
    tn = 512
    return pl.pallas_call(
        _mod_kernel,
        grid=(DEPTH, 3 * D_MODEL // tn),
        in_specs=[pl.BlockSpec((V7X_SUBLANES, D_MODEL), lambda l, j: (0, 0)),
                  pl.BlockSpec((1, D_MODEL, tn), lambda l, j: (l, 0, j)),
                  pl.BlockSpec((1, 1, tn), lambda l, j: (l, 0, j))],
        out_specs=pl.BlockSpec((1, V7X_SUBLANES, tn), lambda l, j: (l, 0, j)),
        out_shape=jax.ShapeDtypeStruct((DEPTH, V7X_SUBLANES, 3 * D_MODEL), F32),
        compiler_params=_params(2),
        name="modulation",
    )(cvec, w_mod, b_mod.reshape(DEPTH, 1, 3 * D_MODEL))


def _modulated(x, g_ref, sh_ref, sc_ref):
    return (_rms(x) * g_ref[...]) * (1.0 + sc_ref[0]) + sh_ref[0]


def _proj_kernel(latent, n_prev, *refs):
    (x_ref, sh_ref, sc_ref, g_ref, wh_ref, wx_ref, qn_ref, wqb_ref, kvn_ref, wk_ref, wv_ref,
     gqn_ref, gkn_ref, gm_ref, egv_ref) = refs[:15]
    pos = 15
    if latent:
        c_mla, s_mla, c_64, s_64 = refs[pos:pos + 4]
        pos += 4
    pos += n_prev
    (q_o, k_o, v_o, dq_o, dk_o, dv_o, gq_o, gk_o, gv_o, u_o) = refs[pos:pos + 10]
    pos += 10
    if not latent:
        ckv_f, kpe_t, dk_t, dv_f, gk_t, gv_t = refs[pos:pos + 6]

    h = _modulated(x_ref[...], g_ref, sh_ref, sc_ref).astype(BF16)
    tm = h.shape[0]

    def head(cols):
        return _dot(h, wh_ref[:, cols[0]:cols[1]])

    def mix(cols):
        return _dot(h, wx_ref[:, cols[0]:cols[1]])

    def rope_mla(x):
        return _rope_chunk(x, c_mla[...], s_mla[...], MLA_ROPE // 2) if latent else x

    def rope_64(x):
        if not latent:
            return x
        c, s = c_64[...], s_64[...]
        chunks = [_rope_chunk(x[:, i:i + V7X_LANES], c, s, DIFF_HEAD_DIM // 2)
                  for i in range(0, x.shape[1], V7X_LANES)]
        return chunks[0] if len(chunks) == 1 else jnp.concatenate(chunks, axis=1)

    def per_sequence(x):
        return [x[b * SSM_SEQ:(b + 1) * SSM_SEQ] for b in range(tm // SSM_SEQ)]

    qn = (_rms(head(_H_QA)) * qn_ref[...]).astype(BF16)
    q = _dot(qn, wqb_ref[...])
    for hd in range(MLA_HEADS):
        sl = slice(hd * MLA_HEAD_PAD, (hd + 1) * MLA_HEAD_PAD)
        q_o[:, sl] = (rope_mla(q[:, sl]) * MLA_SCALE).astype(BF16)
    ckv = _rms(head(_H_KVA)) * kvn_ref[...]
    kpe = head(_H_KPE)
    if not latent:
        for b, (c_b, k_b) in enumerate(zip(per_sequence(ckv), per_sequence(kpe))):
            ckv_f[b] = c_b
            kpe_t[b] = k_b.T[MLA_NOPE:MLA_NOPE + MLA_ROPE, :]
    kpe = rope_mla(kpe)
    ckv_b = ckv.astype(BF16)
    kn = _dot(ckv_b, wk_ref[...])
    for hd in range(MLA_HEADS):
        sl = slice(hd * MLA_HEAD_PAD, (hd + 1) * MLA_HEAD_PAD)
        k_o[:, sl] = (kn[:, sl] + kpe).astype(BF16)
    v_o[...] = _dot(ckv_b, wv_ref[...]).astype(BF16)

    dq_o[...] = (rope_64(mix(_M_DQ)) * DIFF_SCALE).astype(BF16)
    dk = rope_64(mix(_M_DK))
    dv = mix(_M_DV)
    dk_o[...] = dk.astype(BF16)
    dv_o[...] = dv.astype(BF16)

    gq = mix(_M_GQ)
    gq = gq * lax.rsqrt(_group_mean_sq(gq, gm_ref[...]) + EPS) * gqn_ref[...]
    gq_o[...] = (rope_64(gq) * GQA_SCALE).astype(BF16)
    gk = mix(_M_GK)
    kw = GQA_KV_WIDTH
    gk = gk * lax.rsqrt(_group_mean_sq(gk, gm_ref[:kw, :kw]) + EPS) * gkn_ref[...]
    gv = mix(_M_GV)
    if not latent:
        for b, (dk_b, dv_b, gk_b, gv_b) in enumerate(zip(*(per_sequence(a) for a in (dk, dv, gk, gv)))):
            dk_t[b] = dk_b.T
            for hd in range(DIFF_HEADS):
                dv_f[b, :, hd, :] = dv_b[:, hd * DIFF_V:(hd + 1) * DIFF_V]
            gk_t[b] = gk_b.T
            gv_t[b] = gv_b.T
    gk_o[...] = rope_64(gk).astype(BF16)
    gv_o[...] = _dot(gv.astype(BF16), egv_ref[...]).astype(BF16)

    u_o[...] = mix(_M_U)


def _side_shapes(n_seq):
    return [((n_seq, DEPTH, SSM_SEQ, MLA_KV_LORA), (SSM_SEQ, MLA_KV_LORA)),
            ((n_seq, DEPTH, MLA_ROPE, SSM_SEQ), (MLA_ROPE, SSM_SEQ)),
            ((n_seq, DEPTH, 512, SSM_SEQ), (512, SSM_SEQ)),
            ((n_seq, DEPTH, SSM_SEQ, DIFF_HEADS, DIFF_V), (SSM_SEQ, DIFF_HEADS, DIFF_V)),
            ((n_seq, DEPTH, GQA_KV_WIDTH, SSM_SEQ), (GQA_KV_WIDTH, SSM_SEQ)),
            ((n_seq, DEPTH, GQA_KV_WIDTH, SSM_SEQ), (GQA_KV_WIDTH, SSM_SEQ))]


def _proj(x, shift, scale, seq, l, w, rope, latent, prev_sides=()):
    n = x.shape[0]
    tm = PROJ_TM
    row = lambda i: (i, 0)
    mod_spec = _mod_spec(shift.shape[0], tm, seq)
    kw = GQA_KV_WIDTH
    in_specs = [pl.BlockSpec((tm, D_MODEL), row), mod_spec, mod_spec,
                _layer_spec((1, D_MODEL), l), _layer_spec((D_MODEL, _H_WIDTH), l), _layer_spec((D_MODEL, _M_WIDTH), l),
                _layer_spec((1, MLA_Q_LORA), l), _layer_spec((MLA_Q_LORA, MLA_HEADS * MLA_HEAD_PAD), l),
                _layer_spec((1, MLA_KV_LORA), l), _layer_spec((MLA_KV_LORA, MLA_HEADS * MLA_HEAD_PAD), l),
                _layer_spec((MLA_KV_LORA, MLA_HEADS * V_PAD), l),
                _layer_spec((1, 512), l), _layer_spec((1, kw), l), _const_spec((512, 512)),
                _const_spec((kw, 2 * GQA_KV_HEADS * V_PAD))]
    args = [x, shift, scale, w["norm_g"], w["whead"], w["wmix"], w["qn"], w["wqb"], w["kvn"], w["wk"], w["wv"],
            w["gqn"], w["gkn"], w["gmat"], w["egv"]]
    if latent:
        per_seq = seq // tm
        tab = pl.BlockSpec((tm, V7X_LANES), lambda i: (i % per_seq, 0))
        in_specs += [tab] * 4
        args += list(rope)
    n_fixed = len(args)
    in_specs += [pl.BlockSpec(memory_space=pl.ANY)] * len(prev_sides)
    args += list(prev_sides)
    widths = [(1024, BF16), (1024, BF16), (MLA_HEADS * V_PAD, BF16), (512, BF16), (512, BF16),
              (512, BF16), (512, BF16), (kw, BF16), (2 * GQA_KV_HEADS * V_PAD, BF16), (512, F32)]
    out_specs = [pl.BlockSpec((tm, wd), row) for wd, _ in widths]
    out_shape = [jax.ShapeDtypeStruct((n, wd), dt) for wd, dt in widths]
    aliases = {}
    if not latent:
        per_tile = tm // SSM_SEQ
        for full, blk in _side_shapes(n // SSM_SEQ):
            zeros = (0,) * len(blk)
            out_specs.append(pl.BlockSpec((per_tile, None) + blk, lambda i, zeros=zeros: (i, l) + zeros))
            out_shape.append(jax.ShapeDtypeStruct(full, F32))
        aliases = {n_fixed + k: len(widths) + k for k in range(len(prev_sides))}
    return pl.pallas_call(
        functools.partial(_proj_kernel, latent, len(prev_sides)),
        grid=(n // tm,),
        in_specs=in_specs, out_specs=out_specs, out_shape=out_shape,
        input_output_aliases=aliases,
        compiler_params=_params(1),
        name="proj_latent" if latent else "proj_context",
    )(*args)


def _ctx_kv_kernel(ckv_ref, kpe_ref, wk_ref, wv_ref, k_o, v_o):
    ckv_b = ckv_ref[...].astype(BF16)
    kn = _dot(ckv_b, wk_ref[...])
    kpe = kpe_ref[...]
    for hd in range(MLA_HEADS):
        sl = slice(hd * MLA_HEAD_PAD, (hd + 1) * MLA_HEAD_PAD)
        k_o[:, sl] = (kn[:, sl] + kpe).astype(BF16)
    v_o[...] = _dot(ckv_b, wv_ref[...]).astype(BF16)


def _ctx_kv(ckv, kpe_pad, l, w):
    n = ckv.shape[0]
    tm = 512
    row = lambda i: (i, 0)
    return pl.pallas_call(
        _ctx_kv_kernel,
        grid=(n // tm,),
        in_specs=[pl.BlockSpec((tm, MLA_KV_LORA), row), pl.BlockSpec((tm, MLA_HEAD_PAD), row),
                  _layer_spec((MLA_KV_LORA, MLA_HEADS * MLA_HEAD_PAD), l),
                  _layer_spec((MLA_KV_LORA, MLA_HEADS * V_PAD), l)],
        out_specs=[pl.BlockSpec((tm, MLA_HEADS * MLA_HEAD_PAD), row), pl.BlockSpec((tm, MLA_HEADS * V_PAD), row)],
        out_shape=[jax.ShapeDtypeStruct((n, MLA_HEADS * MLA_HEAD_PAD), BF16),
                   jax.ShapeDtypeStruct((n, MLA_HEADS * V_PAD), BF16)],
        compiler_params=_params(1),
        name="ctx_kv",
    )(ckv, kpe_pad, w["wk"], w["wv"])


def _softmax_maps(maps):
    scores = [_dot_nt(q, k) for q, k, _ in maps]
    tops = [jnp.max(s, axis=-1, keepdims=True) for s in scores]
    weights = [jnp.exp(s - m).astype(BF16) for s, m in zip(scores, tops)]
    outs = []
    for p, (_, _, v) in zip(weights, maps):
        ones = jnp.ones((v.shape[0], V7X_LANES), BF16)
        o = _dot(p, jnp.concatenate([v, ones], axis=1))
        outs.append((o[:, :V7X_LANES], o[:, V7X_LANES:]))
    return outs


def _map_group(n_keys):
    return ATTN_GROUP_SHORT if n_keys <= ATTN_TQ else ATTN_GROUP_LONG


def _attn_heads_kernel(heads, q_width, q_ref, k_ref, v_ref, o_ref):
    items = [(b, hd) for b in range(q_ref.shape[0]) for hd in range(len(heads))]
    group = _map_group(k_ref.shape[1])
    for g0 in range(0, len(items), group):
        part = items[g0:g0 + group]
        maps = [(q_ref[b, :, hd * q_width:(hd + 1) * q_width],
                 k_ref[b, :, heads[hd][0] * q_width:(heads[hd][0] + 1) * q_width],
                 v_ref[b, :, heads[hd][1] * V_PAD:(heads[hd][1] + 1) * V_PAD]) for b, hd in part]
        terms = [o * (1.0 / total) for o, total in _softmax_maps(maps)]
        for (b, hd), low, high in zip(part[0::2], terms[0::2], terms[1::2]):
            o_ref[b, :, (hd // 2) * V7X_LANES:(hd // 2 + 1) * V7X_LANES] = (low + high).astype(BF16)


def _diff_attn_kernel(lam_init, q_ref, k_ref, v_ref, lp_ref, sub_ref, o_ref):
    lp = lp_ref[...]
    lam = (jnp.exp(jnp.sum(lp[0:1] * lp[1:2], axis=-1, keepdims=True))
           - jnp.exp(jnp.sum(lp[2:3] * lp[3:4], axis=-1, keepdims=True)) + lam_init)
    d = DIFF_HEAD_DIM
    items = [(b, hd) for b in range(q_ref.shape[0]) for hd in range(DIFF_HEADS)]
    group = _map_group(k_ref.shape[1]) // 2
    for g0 in range(0, len(items), group):
        part = items[g0:g0 + group]
        maps = [(q_ref[b, :, c * d:(c + 1) * d], k_ref[b, :, c * d:(c + 1) * d],
                 v_ref[b, :, hd * DIFF_V:(hd + 1) * DIFF_V])
                for b, hd in part for c in (2 * hd, 2 * hd + 1)]
        outs = _softmax_maps(maps)
        mixed = [o1 * (1.0 / t1) - o2 * (lam / t2) for (o1, t1), (o2, t2) in zip(outs[0::2], outs[1::2])]
        normed = [_rms(o) * sub_ref[...] * (1.0 - lam_init) for o in mixed]
        for (b, hd), o in zip(part, normed):
            o_ref[b, :, hd * DIFF_V:(hd + 1) * DIFF_V] = o.astype(BF16)


def _attention(kernel, name, q, k, v, extra_specs=(), extra=()):
    b, t, wq = q.shape
    s = k.shape[1]
    tq = ATTN_TQ
    nb = ATTN_SEQ_PER_STEP if t == tq else 1
    in_specs = [pl.BlockSpec((nb, tq, wq), lambda i, j: (i, j, 0)),
                pl.BlockSpec((nb, s, k.shape[2]), lambda i, j: (i, 0, 0)),
                pl.BlockSpec((nb, s, v.shape[2]), lambda i, j: (i, 0, 0))]
    in_specs += list(extra_specs)
    return pl.pallas_call(
        kernel,
        grid=(b // nb, t // tq),
        in_specs=in_specs,
        out_specs=pl.BlockSpec((nb, tq, 512), lambda i, j: (i, j, 0)),
        out_shape=jax.ShapeDtypeStruct((b, t, 512), BF16),
        compiler_params=_params(2),
        name=name,
    )(q, k, v, *extra)


def _ssm_kernel(chunked, want_final, *refs):
    u_ref, bm_ref, ab_ref, cm_ref, d_ref = refs[:5]
    pos = 5
    if chunked:
        s0_ref = refs[pos]
        pos += 1
    y_ref = refs[pos]
    pos += 1
    if want_final:
        fin_ref = refs[pos]
        pos += 1
    hs_ref = refs[pos]

    seq, rows, sb = SSM_SEQ, V7X_SUBLANES, SSM_SBLK
    u2 = u_ref[...].reshape(seq * rows, SSM_UBLK)
    u2b = u2.astype(BF16)
    for dr in range(2):
        hs_ref[dr] = _dot(u2b, bm_ref[dr, 0])

    a_parts = []
    for dr in range(2):
        a = ab_ref[dr, 0]
        a_parts.append((jnp.broadcast_to(a[:, :sb], (rows, sb)), jnp.broadcast_to(a[:, sb:], (rows, sb))))

    def row_slices(i):
        return pl.ds(pl.multiple_of(i * rows, rows), rows), pl.ds(pl.multiple_of((seq - 1 - i) * rows, rows), rows)

    def scan_step(i, carry):
        new = []
        for dr, sl in enumerate(row_slices(i)):
            ar, ai = a_parts[dr]
            hr, hi = carry[2 * dr], carry[2 * dr + 1]
            b = hs_ref[dr, sl, :]
            nr = ar * hr - ai * hi + b[:, :sb]
            ni = ar * hi + ai * hr + b[:, sb:]
            hs_ref[dr, sl, :sb] = nr
            hs_ref[dr, sl, sb:] = ni
            new += [nr, ni]
        return tuple(new)

    zero = jnp.zeros((rows, sb), F32)
    fin = lax.fori_loop(0, seq, scan_step, (zero, zero, zero, zero), unroll=2)

    if want_final:
        for dr in range(2):
            fin_ref[dr, 0, :, :sb] = fin[2 * dr]
            fin_ref[dr, 0, :, sb:] = fin[2 * dr + 1]

    if chunked:
        entry = []
        for dr in range(2):
            ar, ai = a_parts[dr]
            pr, pi = ar[0:1], ai[0:1]
            for _ in range(int(math.log2(seq))):
                pr, pi = pr * pr - pi * pi, 2.0 * pr * pi
            fr, fi = fin[2 * dr], fin[2 * dr + 1]
            s0 = s0_ref[dr, 0, 0]
            er, ei = s0[:, :sb], s0[:, sb:]
            order = range(rows) if dr == 0 else range(rows - 1, -1, -1)
            rows_r, rows_i = [None] * rows, [None] * rows
            for c in order:
                rows_r[c], rows_i[c] = er, ei
                er, ei = (pr * er - pi * ei + fr[c:c + 1], pr * ei + pi * er + fi[c:c + 1])
            entry += [jnp.concatenate(rows_r, axis=0), jnp.concatenate(rows_i, axis=0)]

        def fix_step(i, carry):
            new = []
            for dr, sl in enumerate(row_slices(i)):
                ar, ai = a_parts[dr]
                zr, zi = carry[2 * dr], carry[2 * dr + 1]
                nr = ar * zr - ai * zi
                ni = ar * zi + ai * zr
                hs_ref[dr, sl, :sb] += nr
                hs_ref[dr, sl, sb:] += ni
                new += [nr, ni]
            return tuple(new)

        lax.fori_loop(0, seq, fix_step, tuple(entry), unroll=2)

    y = _dot(hs_ref[0].astype(BF16), cm_ref[0, 0]) + _dot(hs_ref[1].astype(BF16), cm_ref[1, 0])
    y = _gelu_tanh(y + d_ref[...] * u2)
    y_ref[...] = y.reshape(seq, rows, SSM_UBLK)


def _ssm(u_t, l, w, s0, want_final):
    seq, r, _ = u_t.shape
    chunked = s0 is not None
    rows = V7X_SUBLANES
    in_specs = [pl.BlockSpec((seq, rows, SSM_UBLK), lambda i, j: (0, i, j)),
                pl.BlockSpec((None, 2, 1, SSM_UBLK, 2 * SSM_SBLK), lambda i, j: (l, 0, j, 0, 0)),
                pl.BlockSpec((None, 2, 1, 1, 2 * SSM_SBLK), lambda i, j: (l, 0, j, 0, 0)),
                pl.BlockSpec((None, 2, 1, 2 * SSM_SBLK, SSM_UBLK), lambda i, j: (l, 0, j, 0, 0)),
                pl.BlockSpec((None, 1, SSM_UBLK), lambda i, j: (l, 0, j))]
    args = [u_t, w["ssm_b"], w["ssm_a"], w["ssm_c"], w["ssm_d"]]
    if chunked:
        in_specs.append(pl.BlockSpec((None, 2, 1, 1, 1, 2 * SSM_SBLK), lambda i, j: (l, 0, i, j, 0, 0)))
        args.append(s0)
    out_specs = [pl.BlockSpec((seq, rows, SSM_UBLK), lambda i, j: (0, i, j))]
    out_shape = [jax.ShapeDtypeStruct((seq, r, SSM_WIDTH), F32)]
    if want_final:
        out_specs.append(pl.BlockSpec((2, 1, rows, 2 * SSM_SBLK), lambda i, j: (0, j, i, 0)))
        out_shape.append(jax.ShapeDtypeStruct((2, SSM_NBLK, r, 2 * SSM_SBLK), F32))
    return pl.pallas_call(
        functools.partial(_ssm_kernel, chunked, want_final),
        grid=(r // rows, SSM_NBLK),
        in_specs=in_specs, out_specs=out_specs, out_shape=out_shape,
        scratch_shapes=[pltpu.VMEM((2, seq * rows, 2 * SSM_SBLK), F32)],
        compiler_params=_params(2),
        name="ssm_latent" if chunked else "ssm_context",
    )(*args)


def _merge_kernel(last, x_ref, sh_ref, sc_ref, ga_ref, g_ref, oa_ref, ob_ref, oc_ref, y_ref,
                  wg_ref, wm_ref, wglu_ref, bglu_ref, wbr_ref, wout_ref, fn_ref, o_ref):
    x = x_ref[...]
    h = _modulated(x, g_ref, sh_ref, sc_ref).astype(BF16)
    y = y_ref[...]
    o_d = y * _sigmoid(_dot(y.astype(BF16), wglu_ref[...]) + bglu_ref[...])
    branches = (oa_ref[...].astype(F32), ob_ref[...].astype(F32), oc_ref[...].astype(F32), o_d)
    acc = None
    for n, o in enumerate(branches):
        gate = _dot(h, wg_ref[:, n * BRANCH_WIDTH:(n + 1) * BRANCH_WIDTH])
        br = _dot((o * (gate * _sigmoid(gate))).astype(BF16), wbr_ref[n])
        term = _sigmoid(_dot(h, wm_ref[:, n * D_MODEL:(n + 1) * D_MODEL])) * br
        acc = term if acc is None else acc + term
    xn = x + ga_ref[0] * _dot(acc.astype(BF16), wout_ref[...])
    if last:
        xn = _rms(xn) * fn_ref[...]
    o_ref[...] = xn


def _merge(x, shift, scale, gate, seq, oa, ob, oc, y, l, w, final_norm, last):
    n = x.shape[0]
    tm = MERGE_TM
    row = lambda i: (i, 0)
    mod_spec = _mod_spec(shift.shape[0], tm, seq)
    in_specs = [pl.BlockSpec((tm, D_MODEL), row), mod_spec, mod_spec, mod_spec, _layer_spec((1, D_MODEL), l),
                pl.BlockSpec((tm, 512), row), pl.BlockSpec((tm, 512), row), pl.BlockSpec((tm, 512), row),
                pl.BlockSpec((tm, 512), row),
                _layer_spec((D_MODEL, N_BRANCH * BRANCH_WIDTH), l), _layer_spec((D_MODEL, N_BRANCH * D_MODEL), l),
                _layer_spec((SSM_WIDTH, SSM_WIDTH), l), _layer_spec((1, SSM_WIDTH), l),
                _layer_spec((N_BRANCH, BRANCH_WIDTH, D_MODEL), l), _layer_spec((D_MODEL, D_MODEL), l),
                _const_spec((1, D_MODEL))]
    return pl.pallas_call(
        functools.partial(_merge_kernel, last),
        grid=(n // tm,),
        in_specs=in_specs,
        out_specs=pl.BlockSpec((tm, D_MODEL), row),
        out_shape=jax.ShapeDtypeStruct((n, D_MODEL), F32),
        compiler_params=_params(1),
        name="merge",
    )(x, shift, scale, gate, w["norm_g"], oa, ob, oc, y, w["wg"], w["wm"], w["wglu"], w["bglu"],
      w["wbr"], w["wout"], final_norm)


def _rope_tables(n_tok, rot_dim):
    rows = n_tok // GRID_W
    row = jnp.repeat(jnp.arange(rows, dtype=F32), GRID_W)
    col = jnp.tile(jnp.arange(GRID_W, dtype=F32), rows)
    quarter = rot_dim // 4
    inv = ROPE_THETA ** (-jnp.arange(quarter, dtype=F32) / quarter)
    ang = jnp.concatenate([row[:, None] * inv, col[:, None] * inv], axis=-1)
    return jnp.cos(ang), jnp.sin(ang)


def _rope_lane_tables(n_tok):
    c, s = _rope_tables(n_tok, MLA_ROPE)
    ones = jnp.ones((n_tok, MLA_NOPE), F32)
    pad = MLA_HEAD_PAD - MLA_NOPE - MLA_ROPE
    c_mla = jnp.concatenate([ones, c, c, jnp.ones((n_tok, pad), F32)], axis=1)
    s_mla = jnp.concatenate([0.0 * ones, -s, s, jnp.zeros((n_tok, pad), F32)], axis=1)
    c, s = _rope_tables(n_tok, DIFF_HEAD_DIM)
    c_64 = jnp.concatenate([c, c, c, c], axis=1)
    s_64 = jnp.concatenate([-s, s, -s, s], axis=1)
    return c_mla, s_mla, c_64, s_64


def _block_diag(blocks):
    g = blocks.shape[-3]
    eye = jnp.eye(g, dtype=blocks.dtype)
    out = jnp.einsum("...grc,gh->...grhc", blocks, eye)
    return out.reshape(blocks.shape[:-3] + (g * blocks.shape[-2], g * blocks.shape[-1]))


def _in_output_half(v):
    zeros = jnp.zeros_like(v[..., 0, :])
    return jnp.stack([jnp.concatenate([v[..., 0, :], zeros], axis=-1),
                      jnp.concatenate([zeros, v[..., 1, :]], axis=-1)], axis=-2)


def _weights(w_in, p):
    w_t = jnp.swapaxes(w_in, 1, 2).reshape(DEPTH * D_IN, D_MODEL)
    wmix = _transposed_rows(w_t, _IN_MIX_START, _M_WIDTH, _M_WIDTH // 2)
    wg = _transposed_rows(w_t, _IN_GATE_START, N_BRANCH * BRANCH_WIDTH, 1024)
    wm = _transposed_rows(w_t, _IN_MERGE_START, N_BRANCH * D_MODEL, 1024)
    zeros = lambda wd: jnp.zeros((DEPTH, D_MODEL, wd), F32)
    kpe_lo = MLA_Q_LORA + MLA_KV_LORA
    whead = jnp.concatenate([w_in[:, :, :kpe_lo], zeros(MLA_NOPE), w_in[:, :, kpe_lo:_IN_MIX_START],
                             zeros(MLA_HEAD_PAD - MLA_NOPE - MLA_ROPE)], axis=2).astype(BF16)

    hq = MLA_NOPE + MLA_ROPE
    wqb = p["w_mla_q_b"].reshape(DEPTH, MLA_Q_LORA, MLA_HEADS, hq)
    wqb = jnp.pad(wqb, ((0, 0), (0, 0), (0, 0), (0, MLA_HEAD_PAD - hq))).reshape(DEPTH, MLA_Q_LORA, -1).astype(BF16)
    wkv = p["w_mla_kv_b"].reshape(DEPTH, MLA_KV_LORA, MLA_HEADS, MLA_NOPE + MLA_V)
    wk = jnp.pad(wkv[..., :MLA_NOPE], ((0, 0), (0, 0), (0, 0), (0, MLA_HEAD_PAD - MLA_NOPE)))
    wk = wk.reshape(DEPTH, MLA_KV_LORA, -1).astype(BF16)
    wv = _in_output_half(wkv[..., MLA_NOPE:].reshape(DEPTH, MLA_KV_LORA, MLA_HEADS // 2, 2, MLA_V))
    wv = wv.reshape(DEPTH, MLA_KV_LORA, -1).astype(BF16)
    eye = jnp.eye(GQA_HEAD_DIM, dtype=F32)
    egv = _block_diag(jnp.broadcast_to(
        _in_output_half(jnp.stack([eye, eye], axis=1)).reshape(GQA_HEAD_DIM, 2 * V_PAD),
        (GQA_KV_HEADS, GQA_HEAD_DIM, 2 * V_PAD))).astype(BF16)

    gmat =_block_diag(jnp.full((GQA_HEADS, GQA_HEAD_DIM, GQA_HEAD_DIM), 1.0 / GQA_HEAD_DIM, F32)).astype(BF16)

    lam = lax.complex(p["ssm_a_re"], p["ssm_a_im"])
    dt = jnp.exp(p["ssm_log_dt"])[..., None]
    abar = jnp.exp(lam * dt)
    bbar = ((abar - 1.0) / lam)[..., None] * lax.complex(p["ssm_b_re"], p["ssm_b_im"])
    blk = lambda a: a.reshape((DEPTH, 2, SSM_NBLK, SSM_GBLOCK) + a.shape[3:])
    b_t = jnp.swapaxes(blk(bbar), -1, -2)
    ssm_b = jnp.concatenate([_block_diag(jnp.real(b_t)), _block_diag(jnp.imag(b_t))], axis=-1).astype(BF16)
    ab = blk(abar).reshape(DEPTH, 2, SSM_NBLK, 1, SSM_SBLK)
    ssm_a = jnp.concatenate([jnp.real(ab), jnp.imag(ab)], axis=-1)
    c_re = jnp.swapaxes(blk(p["ssm_c_re"]), -1, -2)
    c_im = jnp.swapaxes(blk(p["ssm_c_im"]), -1, -2)
    ssm_c = jnp.concatenate([_block_diag(c_re), -_block_diag(c_im)], axis=-2).astype(BF16)

    row = lambda a: a[:, None, :]
    return dict(
        norm_g=row(p["norm_g"]), whead=whead, wmix=wmix, qn=row(p["mla_q_norm"]), wqb=wqb,
        kvn=row(p["mla_kv_norm"]), wk=wk, wv=wv,
        gqn=row(jnp.tile(p["gqa_q_norm"], (1, GQA_HEADS))), gkn=row(jnp.tile(p["gqa_k_norm"], (1, GQA_KV_HEADS))),
        gmat=gmat, egv=egv,
        lam_parts=jnp.stack([p["diff_lq1"], p["diff_lk1"], p["diff_lq2"], p["diff_lk2"]], axis=1),
        subln=row(p["diff_subln"]),
        ssm_b=ssm_b, ssm_a=ssm_a, ssm_c=ssm_c, ssm_d=row(p["ssm_d"]),
        wg=wg, wm=wm, wglu=p["ssm_glu_w"].astype(BF16), bglu=row(p["ssm_glu_b"]),
        wbr=p["w_branch_out"].astype(BF16), wout=p["w_out"].astype(BF16),
    )


def _mixers(pr, b, t, l, w, lam_init, ctx):
    q, k, v, dq, dk, dv, gq, gk, gv, u = pr[:10]
    three = lambda a: a.reshape(b, t, a.shape[-1])
    k, v, dk, dv, gk, gv = (three(a) for a in (k, v, dk, dv, gk, gv))
    if ctx is not None:
        cat = lambda c, a: jnp.concatenate([c, a], axis=1)
        k, v, dk, dv, gk, gv = (cat(c, a) for c, a in zip(ctx["kv"], (k, v, dk, dv, gk, gv)))
    mla_heads = tuple((hd, hd) for hd in range(MLA_HEADS))
    gqa_heads = tuple((hd // GQA_GROUP, 2 * (hd // GQA_GROUP) + hd % 2) for hd in range(GQA_HEADS))
    o_a = _attention(functools.partial(_attn_heads_kernel, mla_heads, MLA_HEAD_PAD), "attn_mla",
                     three(q), k, v)
    o_b = _attention(functools.partial(_diff_attn_kernel, lam_init), "attn_diff", three(dq), dk, dv,
                     extra_specs=(_layer_spec((4, DIFF_HEAD_DIM), l), _layer_spec((1, DIFF_V), l)),
                     extra=(w["lam_parts"], w["subln"]))
    o_c = _attention(functools.partial(_attn_heads_kernel, gqa_heads, GQA_HEAD_DIM), "attn_gqa",
                     three(gq), gk, gv)
    n = b * t
    if ctx is None:
        u_t = jnp.swapaxes(u.reshape(b, t, SSM_WIDTH), 0, 1)
        y_t, fin = _ssm(u_t, l, w, None, True)
        y = jnp.swapaxes(y_t, 0, 1).reshape(n, SSM_WIDTH)
    else:
        chunks = t // SSM_SEQ
        u_t = jnp.transpose(u.reshape(b, chunks, SSM_SEQ, SSM_WIDTH), (2, 0, 1, 3)).reshape(SSM_SEQ, b * chunks, SSM_WIDTH)
        (y_t,) = _ssm(u_t, l, w, ctx["s0"], False)
        y = jnp.transpose(y_t.reshape(SSM_SEQ, b, chunks, SSM_WIDTH), (1, 2, 0, 3)).reshape(n, SSM_WIDTH)
        fin = None
    flat = lambda a: a.reshape(n, 512)
    return flat(o_a), flat(o_b), flat(o_c), y, fin


def kernel(x_prompt, x_sample, cache_mla_ckv, cache_mla_krope, cache_diff_k, cache_diff_v, cache_gqa_k, cache_gqa_v, state_ssm, c, c_ctx, norm_g, w_mod, b_mod, w_in, mla_q_norm, w_mla_q_b, mla_kv_norm, w_mla_kv_b, diff_lq1, diff_lk1, diff_lq2, diff_lk2, diff_subln, gqa_q_norm, gqa_k_norm, ssm_a_re, ssm_a_im, ssm_log_dt, ssm_b_re, ssm_b_im, ssm_c_re, ssm_c_im, ssm_d, ssm_glu_w, ssm_glu_b, w_branch_out, w_out, final_norm):
    p = dict(norm_g=norm_g, mla_q_norm=mla_q_norm, w_mla_q_b=w_mla_q_b, mla_kv_norm=mla_kv_norm,
             w_mla_kv_b=w_mla_kv_b, diff_lq1=diff_lq1, diff_lk1=diff_lk1, diff_lq2=diff_lq2, diff_lk2=diff_lk2,
             diff_subln=diff_subln, gqa_q_norm=gqa_q_norm, gqa_k_norm=gqa_k_norm, ssm_a_re=ssm_a_re,
             ssm_a_im=ssm_a_im, ssm_log_dt=ssm_log_dt, ssm_b_re=ssm_b_re, ssm_b_im=ssm_b_im, ssm_c_re=ssm_c_re,
             ssm_c_im=ssm_c_im, ssm_d=ssm_d, ssm_glu_w=ssm_glu_w, ssm_glu_b=ssm_glu_b,
             w_branch_out=w_branch_out, w_out=w_out)
    bp, tp, _ = x_prompt.shape
    bs, ts, _ = x_sample.shape
    past = cache_mla_ckv.shape[2]
    assert tp == SSM_SEQ and ts % SSM_SEQ == 0 and ts // SSM_SEQ == V7X_SUBLANES
    rope = _rope_lane_tables(ts)
    w = _weights(w_in, p)

    cvec = jnp.zeros((V7X_SUBLANES, D_MODEL), F32).at[0].set(c_ctx).at[1:1 + bs].set(c)
    mod = _modulation(cvec, w_mod, b_mod)

    pad_lo = MLA_NOPE
    pad_hi = MLA_HEAD_PAD - MLA_NOPE - MLA_ROPE
    kpe_ctx = jnp.pad(cache_mla_krope, ((0, 0), (0, 0), (0, 0), (pad_lo, pad_hi)))
    flat_ctx = lambda a: a.reshape(bs, DEPTH, past, -1).astype(BF16)
    gv_both = _in_output_half(jnp.stack([cache_gqa_v, cache_gqa_v], axis=-2))
    dk_ctx, dv_ctx, gk_ctx, gv_ctx = (flat_ctx(a) for a in (cache_diff_k, cache_diff_v, cache_gqa_k, gv_both))
    s0_all = state_ssm.reshape(bs, DEPTH, 2, SSM_NBLK, SSM_GBLOCK * SSM_STATE, 2)
    s0_all = jnp.transpose(s0_all, (1, 2, 0, 3, 5, 4)).reshape(DEPTH, 2, bs, SSM_NBLK, 1, 2 * SSM_SBLK)

    xp = x_prompt.reshape(bp * tp, D_MODEL)
    xs = x_sample.reshape(bs * ts, D_MODEL)
    fn = final_norm[None]
    sides = ()
    states = []
    for l in range(DEPTH):
        lam_init = 0.8 - 0.6 * math.exp(-0.3 * l)
        last = l == DEPTH - 1
        sh, sc, ga = (mod[l, :, i * D_MODEL:(i + 1) * D_MODEL][:, None, :] for i in range(3))

        pr = _proj(xp, sh[0:1], sc[0:1], tp, l, w, None, False, sides)
        o_a, o_b, o_c, y, fin = _mixers(pr, bp, tp, l, w, lam_init, None)
        xp = _merge(xp, sh[0:1], sc[0:1], ga[0:1], tp, o_a, o_b, o_c, y, l, w, fn, last)
        sides = tuple(pr[10:])
        fin = fin.reshape(2, SSM_NBLK, bp, 2, SSM_GBLOCK, SSM_STATE)
        states.append(jnp.transpose(fin, (2, 0, 1, 4, 5, 3)).reshape(bp, 2, SSM_GROUPS, SSM_STATE, 2))

        k_ctx, v_ctx = _ctx_kv(cache_mla_ckv[:, l].reshape(bs * past, MLA_KV_LORA),
                               kpe_ctx[:, l].reshape(bs * past, MLA_HEAD_PAD), l, w)
        ctx = dict(
            kv=(k_ctx.reshape(bs, past, -1), v_ctx.reshape(bs, past, -1),
                dk_ctx[:, l], dv_ctx[:, l], gk_ctx[:, l], gv_ctx[:, l]),
            s0=s0_all)
        pr = _proj(xs, sh[1:1 + bs], sc[1:1 + bs], ts, l, w, rope, True)
        o_a, o_b, o_c, y, _ = _mixers(pr, bs, ts, l, w, lam_init, ctx)
        xs = _merge(xs, sh[1:1 + bs], sc[1:1 + bs], ga[1:1 + bs], ts, o_a, o_b, o_c, y, l, w, fn, last)

    ckv_s, kpe_t, dk_t, dv_s, gk_t, gv_t = sides
    untranspose = lambda a: jnp.swapaxes(a, 2, 3)
    return (xp.reshape(bp, tp, D_MODEL), xs.reshape(bs, ts, D_MODEL),
            ckv_s, untranspose(kpe_t),
            untranspose(dk_t).reshape(bp, DEPTH, tp, DIFF_HEADS, 2, DIFF_HEAD_DIM), dv_s,
            untranspose(gk_t).reshape(bp, DEPTH, tp, GQA_KV_HEADS, GQA_HEAD_DIM),
            untranspose(gv_t).reshape(bp, DEPTH, tp, GQA_KV_HEADS, GQA_HEAD_DIM),
            jnp.stack(states, axis=1))
```

```python
import functools
import math

import jax
import jax.numpy as jnp
from jax import lax
from jax.experimental import pallas as pl
from jax.experimental.pallas import tpu as pltpu

F32 = jnp.float32
BF16 = jnp.bfloat16

D_MODEL = 1024
DEPTH = 2
GRID_W = 64
ROPE_THETA = 10000.0
EPS = 1e-6

MLA_HEADS = 8
MLA_NOPE = 64
MLA_ROPE = 32
MLA_V = 64
MLA_Q_LORA = 256
MLA_KV_LORA = 128
MLA_SCALE = (MLA_NOPE + MLA_ROPE) ** -0.5
MLA_HEAD_PAD = 128
DIFF_HEADS = 4
DIFF_HEAD_DIM = 64
DIFF_V = 2 * DIFF_HEAD_DIM
DIFF_SCALE = DIFF_HEAD_DIM ** -0.5
GQA_HEADS = 8
GQA_KV_HEADS = 2
GQA_HEAD_DIM = 64
GQA_GROUP = GQA_HEADS // GQA_KV_HEADS
GQA_KV_WIDTH = GQA_KV_HEADS * GQA_HEAD_DIM
GQA_SCALE = GQA_HEAD_DIM ** -0.5
SSM_WIDTH = 512
SSM_GROUP = 16
SSM_GROUPS = SSM_WIDTH // SSM_GROUP
SSM_STATE = 64
N_BRANCH = 4
BRANCH_WIDTH = 512

IN_SPLITS = (MLA_Q_LORA, MLA_KV_LORA, MLA_ROPE, 512, 512, 512, 512, 128, 128, SSM_WIDTH,
             N_BRANCH * BRANCH_WIDTH, N_BRANCH * D_MODEL)
D_IN = sum(IN_SPLITS)

V7X_LANES = 128
V7X_SUBLANES = 8
V7X_VMEM_LIMIT_BYTES = 56 * 1024 * 1024

_H_QA = (0, 256)
_H_KVA = (256, 384)
_H_KPE = (384, 512)
_H_WIDTH = 512
_M_DQ = (0, 512)
_M_DK = (512, 1024)
_M_DV = (1024, 1536)
_M_GQ = (1536, 2048)
_M_GK = (2048, 2176)
_M_GV = (2176, 2304)
_M_U = (2304, 2816)
_M_WIDTH = 2816
_IN_MIX_START = MLA_Q_LORA + MLA_KV_LORA + MLA_ROPE
_IN_GATE_START = _IN_MIX_START + _M_WIDTH
_IN_MERGE_START = _IN_GATE_START + N_BRANCH * BRANCH_WIDTH

SSM_GBLOCK = 8
SSM_NBLK = SSM_GROUPS // SSM_GBLOCK
SSM_SBLK = SSM_GBLOCK * SSM_STATE
SSM_UBLK = SSM_GBLOCK * SSM_GROUP
SSM_SEQ = 256
SSM_SLICE = 32

V_PAD = 128

PROJ_TM = 512
MERGE_TM = 512
ATTN_TQ = 256
ATTN_SEQ_PER_STEP = 4
ATTN_GROUP_SHORT = 32
ATTN_GROUP_LONG = 4


def _params(n_axes, vmem=V7X_VMEM_LIMIT_BYTES):
    return pltpu.CompilerParams(dimension_semantics=("arbitrary",) * n_axes, vmem_limit_bytes=vmem)


def _const_spec(shape):
    zeros = (0,) * len(shape)
    return pl.BlockSpec(shape, lambda *_: zeros, pipeline_mode=pl.Buffered(1))


def _layer_spec(shape, l):
    zeros = (0,) * len(shape)
    return pl.BlockSpec((None,) + tuple(shape), lambda *_: (l,) + zeros, pipeline_mode=pl.Buffered(1))


def _mod_spec(n_rows, tm, seq):
    if n_rows == 1:
        return pl.BlockSpec((1, 1, D_MODEL), lambda i: (0, 0, 0))
    return pl.BlockSpec((1, 1, D_MODEL), lambda i: (i * tm // seq, 0, 0))


def _dot(a, b):
    return jnp.dot(a, b, preferred_element_type=F32)


def _dot_nt(a, b):
    return lax.dot_general(a, b, (((1,), (1,)), ((), ())), preferred_element_type=F32)


def _rms(x):
    return x * lax.rsqrt(jnp.mean(x * x, axis=-1, keepdims=True) + EPS)


def _sigmoid(x):
    return 1.0 / (1.0 + jnp.exp(-x))


def _gelu_tanh(x):
    return 0.5 * x * (1.0 + jnp.tanh(math.sqrt(2.0 / math.pi) * (x + 0.044715 * (x * x * x))))


def _group_mean_sq(x, gmat):
    return _dot((x * x).astype(BF16), gmat)


def _rope_chunk(x, cos, sin, half):
    lane = lax.broadcasted_iota(jnp.int32, x.shape, 1)
    first = (lane % (2 * half)) < half
    swapped = jnp.where(first, pltpu.roll(x, V7X_LANES - half, 1), pltpu.roll(x, half, 1))
    return x * cos + swapped * sin


def _transpose_cast_kernel(w_ref, o_ref):
    o_ref[...] = w_ref[...].T.astype(BF16)


def _transposed_rows(w_t, row0, n_rows, blk):
    return pl.pallas_call(
        _transpose_cast_kernel,
        grid=(DEPTH, n_rows // blk),
        in_specs=[pl.BlockSpec((pl.Element(blk), pl.Element(D_MODEL)),
                               lambda l, j: (pl.multiple_of(l * D_IN + row0 + j * blk, V7X_SUBLANES), 0))],
        out_specs=pl.BlockSpec((None, D_MODEL, blk), lambda l, j: (l, 0, j)),
        out_shape=jax.ShapeDtypeStruct((DEPTH, D_MODEL, n_rows), BF16),
        compiler_params=_params(2),
        name="w_in_columns",
    )(w_t)


def _mod_kernel(c_ref, w_ref, b_ref, o_ref):
    c = c_ref[...]
    a = (c * _sigmoid(c)).astype(BF16)
    o_ref[0] = _dot(a, w_ref[0].astype(BF16)) + b_ref[0]


def _modulation(cvec, w_mod, b_mod):
    tn = 512
    return pl.pallas_call(
        _mod_kernel,
        grid=(DEPTH, 3 * D_MODEL // tn),
        in_specs=[pl.BlockSpec((V7X_SUBLANES, D_MODEL), lambda l, j: (0, 0)),
                  pl.BlockSpec((1, D_MODEL, tn), lambda l, j: (l, 0, j)),
                  pl.BlockSpec((1, 1, tn), lambda l, j: (l, 0, j))],
        out_specs=pl.BlockSpec((1, V7X_SUBLANES, tn), lambda l, j: (l, 0, j)),
        out_shape=jax.ShapeDtypeStruct((DEPTH, V7X_SUBLANES, 3 * D_MODEL), F32),
        compiler_params=_params(2),
        name="modulation",
    )(cvec, w_mod, b_mod.reshape(DEPTH, 1, 3 * D_MODEL))


def _modulated(x, g_ref, sh_ref, sc_ref):
    return (_rms(x) * g_ref[...]) * (1.0 + sc_ref[0]) + sh_ref[0]


def _proj_kernel(latent, n_prev, first_layer, *refs):
    (x_ref, sh_ref, sc_ref, g_ref, wh_ref, wx_ref, qn_ref, wqb_ref, kvn_ref, wk_ref, wv_ref,
     gqn_ref, gkn_ref, gm_ref, egv_ref) = refs[:15]
    pos = 15
    if latent:
        c_mla, s_mla, c_64, s_64 = refs[pos:pos + 4]
        pos += 4
    pos += n_prev
    (q_o, k_o, v_o, dq_o, dk_o, dv_o, gq_o, gk_o, gv_o, u_o) = refs[pos:pos + 10]
    pos += 10
    if not latent:
        ckv_f, kpe_t, dk_t, dv_f, gk_t, gv_t = refs[pos:pos + 6]

    h = _modulated(x_ref[...], g_ref, sh_ref, sc_ref).astype(BF16)
    tm = h.shape[0]

    def head(cols):
        return _dot(h, wh_ref[:, cols[0]:cols[1]])

    def mix(cols):
        return _dot(h, wx_ref[:, cols[0]:cols[1]])

    def rope_mla(x):
        return _rope_chunk(x, c_mla[...], s_mla[...], MLA_ROPE // 2) if latent else x

    def rope_64(x):
        if not latent:
            return x
        c, s = c_64[...], s_64[...]
        chunks = [_rope_chunk(x[:, i:i + V7X_LANES], c, s, DIFF_HEAD_DIM // 2)
                  for i in range(0, x.shape[1], V7X_LANES)]
        return chunks[0] if len(chunks) == 1 else jnp.concatenate(chunks, axis=1)

    def per_sequence(x):
        return [x[b * SSM_SEQ:(b + 1) * SSM_SEQ] for b in range(tm // SSM_SEQ)]

    def slot(b):
        return (b,) if first_layer is None else (b, first_layer)

    def zero_other_slots(ref):
        if first_layer is not None:
            for b in range(ref.shape[0]):
                for other in range(ref.shape[1]):
                    if other != first_layer:
                        ref[b, other] = jnp.zeros(ref.shape[2:], ref.dtype)

    qn = (_rms(head(_H_QA)) * qn_ref[...]).astype(BF16)
    q = _dot(qn, wqb_ref[...])
    for hd in range(MLA_HEADS):
        sl = slice(hd * MLA_HEAD_PAD, (hd + 1) * MLA_HEAD_PAD)
        q_o[:, sl] = (rope_mla(q[:, sl]) * MLA_SCALE).astype(BF16)
    ckv = _rms(head(_H_KVA)) * kvn_ref[...]
    kpe = head(_H_KPE)
    if not latent:
        for b, (c_b, k_b) in enumerate(zip(per_sequence(ckv), per_sequence(kpe))):
            ckv_f[slot(b)] = c_b
            kpe_t[slot(b)] = k_b.T[MLA_NOPE:MLA_NOPE + MLA_ROPE, :]
        zero_other_slots(ckv_f)
        zero_other_slots(kpe_t)
    kpe = rope_mla(kpe)
    ckv_b = ckv.astype(BF16)
    kn = _dot(ckv_b, wk_ref[...])
    for hd in range(MLA_HEADS):
        sl = slice(hd * MLA_HEAD_PAD, (hd + 1) * MLA_HEAD_PAD)
        k_o[:, sl] = (kn[:, sl] + kpe).astype(BF16)
    v_o[...] = _dot(ckv_b, wv_ref[...]).astype(BF16)

    dq_o[...] = (rope_64(mix(_M_DQ)) * DIFF_SCALE).astype(BF16)
    dk = rope_64(mix(_M_DK))
    dv = mix(_M_DV)
    dk_o[...] = dk.astype(BF16)
    dv_o[...] = dv.astype(BF16)

    gq = mix(_M_GQ)
    gq = gq * lax.rsqrt(_group_mean_sq(gq, gm_ref[...]) + EPS) * gqn_ref[...]
    gq_o[...] = (rope_64(gq) * GQA_SCALE).astype(BF16)
    gk = mix(_M_GK)
    kw = GQA_KV_WIDTH
    gk = gk * lax.rsqrt(_group_mean_sq(gk, gm_ref[:kw, :kw]) + EPS) * gkn_ref[...]
    gv = mix(_M_GV)
    if not latent:
        for b, (dk_b, dv_b, gk_b, gv_b) in enumerate(zip(*(per_sequence(a) for a in (dk, dv, gk, gv)))):
            dk_t[slot(b)] = dk_b.T
            for hd in range(DIFF_HEADS):
                dv_f[slot(b) + (slice(None), hd, slice(None))] = dv_b[:, hd * DIFF_V:(hd + 1) * DIFF_V]
            gk_t[slot(b)] = gk_b.T
            gv_t[slot(b)] = gv_b.T
        for ref in (dk_t, dv_f, gk_t, gv_t):
            zero_other_slots(ref)
    gk_o[...] = rope_64(gk).astype(BF16)
    gv_o[...] = _dot(gv.astype(BF16), egv_ref[...]).astype(BF16)

    u_o[...] = mix(_M_U)


def _side_shapes(n_seq):
    return [((n_seq, DEPTH, SSM_SEQ, MLA_KV_LORA), (SSM_SEQ, MLA_KV_LORA)),
            ((n_seq, DEPTH, MLA_ROPE, SSM_SEQ), (MLA_ROPE, SSM_SEQ)),
            ((n_seq, DEPTH, 512, SSM_SEQ), (512, SSM_SEQ)),
            ((n_seq, DEPTH, SSM_SEQ, DIFF_HEADS, DIFF_V), (SSM_SEQ, DIFF_HEADS, DIFF_V)),
            ((n_seq, DEPTH, GQA_KV_WIDTH, SSM_SEQ), (GQA_KV_WIDTH, SSM_SEQ)),
            ((n_seq, DEPTH, GQA_KV_WIDTH, SSM_SEQ), (GQA_KV_WIDTH, SSM_SEQ))]


def _proj(x, shift, scale, seq, l, w, rope, latent, prev_sides=()):
    n = x.shape[0]
    tm = PROJ_TM
    row = lambda i: (i, 0)
    mod_spec = _mod_spec(shift.shape[0], tm, seq)
    kw = GQA_KV_WIDTH
    in_specs = [pl.BlockSpec((tm, D_MODEL), row), mod_spec, mod_spec,
                _layer_spec((1, D_MODEL), l), _layer_spec((D_MODEL, _H_WIDTH), l), _layer_spec((D_MODEL, _M_WIDTH), l),
                _layer_spec((1, MLA_Q_LORA), l), _layer_spec((MLA_Q_LORA, MLA_HEADS * MLA_HEAD_PAD), l),
                _layer_spec((1, MLA_KV_LORA), l), _layer_spec((MLA_KV_LORA, MLA_HEADS * MLA_HEAD_PAD), l),
                _layer_spec((MLA_KV_LORA, MLA_HEADS * V_PAD), l),
                _layer_spec((1, 512), l), _layer_spec((1, kw), l), _const_spec((512, 512)),
                _const_spec((kw, 2 * GQA_KV_HEADS * V_PAD))]
    args = [x, shift, scale, w["norm_g"], w["whead"], w["wmix"], w["qn"], w["wqb"], w["kvn"], w["wk"], w["wv"],
            w["gqn"], w["gkn"], w["gmat"], w["egv"]]
    if latent:
        per_seq = seq // tm
        tab = pl.BlockSpec((tm, V7X_LANES), lambda i: (i % per_seq, 0))
        in_specs += [tab] * 4
        args += list(rope)
    n_fixed = len(args)
    in_specs += [pl.BlockSpec(memory_space=pl.ANY)] * len(prev_sides)
    args += list(prev_sides)
    widths = [(1024, BF16), (1024, BF16), (MLA_HEADS * V_PAD, BF16), (512, BF16), (512, BF16),
              (512, BF16), (512, BF16), (kw, BF16), (2 * GQA_KV_HEADS * V_PAD, BF16), (512, F32)]
    out_specs = [pl.BlockSpec((tm, wd), row) for wd, _ in widths]
    out_shape = [jax.ShapeDtypeStruct((n, wd), dt) for wd, dt in widths]
    aliases = {}
    first_layer = None
    if not latent:
        per_tile = tm // SSM_SEQ
        first_layer = None if prev_sides else l
        for full, blk in _side_shapes(n // SSM_SEQ):
            zeros = (0,) * len(blk)
            if prev_sides:
                out_specs.append(pl.BlockSpec((per_tile, None) + blk, lambda i, zeros=zeros: (i, l) + zeros))
            else:
                out_specs.append(pl.BlockSpec((per_tile, DEPTH) + blk, lambda i, zeros=zeros: (i, 0) + zeros))
            out_shape.append(jax.ShapeDtypeStruct(full, F32))
        aliases = {n_fixed + k: len(widths) + k for k in range(len(prev_sides))}
    return pl.pallas_call(
        functools.partial(_proj_kernel, latent, len(prev_sides), first_layer),
        grid=(n // tm,),
        in_specs=in_specs, out_specs=out_specs, out_shape=out_shape,
        input_output_aliases=aliases,
        compiler_params=_params(1),
        name="proj_latent" if latent else "proj_context",
    )(*args)


def _ctx_kv_kernel(ckv_ref, kpe_ref, wk_ref, wv_ref, k_o, v_o):
    ckv_b = ckv_ref[...].astype(BF16)
    kn = _dot(ckv_b, wk_ref[...])
    kpe = kpe_ref[...]
    for hd in range(MLA_HEADS):
        sl = slice(hd * MLA_HEAD_PAD, (hd + 1) * MLA_HEAD_PAD)
        k_o[:, sl] = (kn[:, sl] + kpe).astype(BF16)
    v_o[...] = _dot(ckv_b, wv_ref[...]).astype(BF16)


def _ctx_kv(ckv, kpe_pad, l, w):
    n = ckv.shape[0]
    tm = 512
    row = lambda i: (i, 0)
    return pl.pallas_call(
        _ctx_kv_kernel,
        grid=(n // tm,),
        in_specs=[pl.BlockSpec((tm, MLA_KV_LORA), row), pl.BlockSpec((tm, MLA_HEAD_PAD), row),
                  _layer_spec((MLA_KV_LORA, MLA_HEADS * MLA_HEAD_PAD), l),
                  _layer_spec((MLA_KV_LORA, MLA_HEADS * V_PAD), l)],
        out_specs=[pl.BlockSpec((tm, MLA_HEADS * MLA_HEAD_PAD), row), pl.BlockSpec((tm, MLA_HEADS * V_PAD), row)],
        out_shape=[jax.ShapeDtypeStruct((n, MLA_HEADS * MLA_HEAD_PAD), BF16),
                   jax.ShapeDtypeStruct((n, MLA_HEADS * V_PAD), BF16)],
        compiler_params=_params(1),
        name="ctx_kv",
    )(ckv, kpe_pad, w["wk"], w["wv"])


def _softmax_maps(maps):
    scores = [_dot_nt(q, k) for q, k, _ in maps]
    tops = [jnp.max(s, axis=-1, keepdims=True) for s in scores]
    weights = [jnp.exp(s - m).astype(BF16) for s, m in zip(scores, tops)]
    outs = []
    for p, (_, _, v) in zip(weights, maps):
        ones = jnp.ones((v.shape[0], V7X_LANES), BF16)
        o = _dot(p, jnp.concatenate([v, ones], axis=1))
        outs.append((o[:, :V7X_LANES], o[:, V7X_LANES:]))
    return outs


def _map_group(n_keys):
    return ATTN_GROUP_SHORT if n_keys <= ATTN_TQ else ATTN_GROUP_LONG


def _attn_heads_kernel(heads, q_width, q_ref, k_ref, v_ref, o_ref):
    items = [(b, hd) for b in range(q_ref.shape[0]) for hd in range(len(heads))]
    group = _map_group(k_ref.shape[1])
    for g0 in range(0, len(items), group):
        part = items[g0:g0 + group]
        maps = [(q_ref[b, :, hd * q_width:(hd + 1) * q_width],
                 k_ref[b, :, heads[hd][0] * q_width:(heads[hd][0] + 1) * q_width],
                 v_ref[b, :, heads[hd][1] * V_PAD:(heads[hd][1] + 1) * V_PAD]) for b, hd in part]
        terms = [o * (1.0 / total) for o, total in _softmax_maps(maps)]
        for (b, hd), low, high in zip(part[0::2], terms[0::2], terms[1::2]):
            o_ref[b, :, (hd // 2) * V7X_LANES:(hd // 2 + 1) * V7X_LANES] = (low + high).astype(BF16)


def _diff_attn_kernel(lam_init, q_ref, k_ref, v_ref, lp_ref, sub_ref, o_ref):
    lp = lp_ref[...]
    lam = (jnp.exp(jnp.sum(lp[0:1] * lp[1:2], axis=-1, keepdims=True))
           - jnp.exp(jnp.sum(lp[2:3] * lp[3:4], axis=-1, keepdims=True)) + lam_init)
    d = DIFF_HEAD_DIM
    items = [(b, hd) for b in range(q_ref.shape[0]) for hd in range(DIFF_HEADS)]
    group = _map_group(k_ref.shape[1]) // 2
    for g0 in range(0, len(items), group):
        part = items[g0:g0 + group]
        maps = [(q_ref[b, :, c * d:(c + 1) * d], k_ref[b, :, c * d:(c + 1) * d],
                 v_ref[b, :, hd * DIFF_V:(hd + 1) * DIFF_V])
                for b, hd in part for c in (2 * hd, 2 * hd + 1)]
        outs = _softmax_maps(maps)
        mixed = [o1 * (1.0 / t1) - o2 * (lam / t2) for (o1, t1), (o2, t2) in zip(outs[0::2], outs[1::2])]
        normed = [_rms(o) * sub_ref[...] * (1.0 - lam_init) for o in mixed]
        for (b, hd), o in zip(part, normed):
            o_ref[b, :, hd * DIFF_V:(hd + 1) * DIFF_V] = o.astype(BF16)


def _attention(kernel, name, q, k, v, extra_specs=(), extra=()):
    b, t, wq = q.shape
    s = k.shape[1]
    tq = ATTN_TQ
    nb = ATTN_SEQ_PER_STEP if t == tq else 1
    in_specs = [pl.BlockSpec((nb, tq, wq), lambda i, j: (i, j, 0)),
                pl.BlockSpec((nb, s, k.shape[2]), lambda i, j: (i, 0, 0)),
                pl.BlockSpec((nb, s, v.shape[2]), lambda i, j: (i, 0, 0))]
    in_specs += list(extra_specs)
    return pl.pallas_call(
        kernel,
        grid=(b // nb, t // tq),
        in_specs=in_specs,
        out_specs=pl.BlockSpec((nb, tq, 512), lambda i, j: (i, j, 0)),
        out_shape=jax.ShapeDtypeStruct((b, t, 512), BF16),
        compiler_params=_params(2),
        name=name,
    )(q, k, v, *extra)


def _ssm_kernel(chunked, want_final, *refs):
    u_ref, bm_ref, ab_ref, cm_ref, d_ref = refs[:5]
    pos = 5
    if chunked:
        s0_ref = refs[pos]
        pos += 1
    y_ref = refs[pos]
    pos += 1
    if want_final:
        fin_ref = refs[pos]
        pos += 1
    hs_ref, yacc_ref = refs[pos:pos + 2]

    seq, rows, sb = SSM_SEQ, V7X_SUBLANES, SSM_SBLK
    n_slices = seq // SSM_SLICE
    slice_rows = SSM_SLICE * rows

    def visited(dr, stage):
        return stage if dr == 0 else n_slices - 1 - stage

    def rows_of(k):
        return slice(k * slice_rows, (k + 1) * slice_rows)

    def u_rows(k):
        return u_ref[k * SSM_SLICE:(k + 1) * SSM_SLICE].reshape(slice_rows, SSM_UBLK)

    def input_proj(stage):
        for dr in range(2):
            k = visited(dr, stage)
            hs_ref[dr, rows_of(k), :] = _dot(u_rows(k).astype(BF16), bm_ref[dr, 0])

    def output_proj(stage):
        for dr in range(2):
            r = rows_of(visited(dr, stage))
            part = _dot(hs_ref[dr, r, :].astype(BF16), cm_ref[dr, 0])
            if stage < n_slices // 2:
                yacc_ref[r, :] = part
            else:
                yacc_ref[r, :] += part

    a_parts = []
    for dr in range(2):
        a = ab_ref[dr, 0]
        a_parts.append((jnp.broadcast_to(a[:, :sb], (rows, sb)), jnp.broadcast_to(a[:, sb:], (rows, sb))))

    def step_rows(stage, step):
        i = stage * SSM_SLICE + step
        return [slice(t * rows, (t + 1) * rows) for t in (i, seq - 1 - i)]

    def scan_slice(stage, carry):
        for step in range(SSM_SLICE):
            new = []
            for dr, sl in enumerate(step_rows(stage, step)):
                ar, ai = a_parts[dr]
                hr, hi = carry[2 * dr], carry[2 * dr + 1]
                b = hs_ref[dr, sl, :]
                nr = ar * hr - ai * hi + b[:, :sb]
                ni = ar * hi + ai * hr + b[:, sb:]
                hs_ref[dr, sl, :sb] = nr
                hs_ref[dr, sl, sb:] = ni
                new += [nr, ni]
            carry = tuple(new)
        return carry

    def fix_slice(stage, carry):
        for step in range(SSM_SLICE):
            new = []
            for dr, sl in enumerate(step_rows(stage, step)):
                ar, ai = a_parts[dr]
                zr, zi = carry[2 * dr], carry[2 * dr + 1]
                nr = ar * zr - ai * zi
                ni = ar * zi + ai * zr
                hs_ref[dr, sl, :sb] += nr
                hs_ref[dr, sl, sb:] += ni
                new += [nr, ni]
            carry = tuple(new)
        return carry

    zero = jnp.zeros((rows, sb), F32)
    fin = (zero, zero, zero, zero)
    input_proj(0)
    for stage in range(n_slices):
        if stage + 1 < n_slices:
            input_proj(stage + 1)
        fin = scan_slice(stage, fin)
        if not chunked and stage >= 1:
            output_proj(stage - 1)

    if want_final:
        for dr in range(2):
            fin_ref[dr, 0, :, :sb] = fin[2 * dr]
            fin_ref[dr, 0, :, sb:] = fin[2 * dr + 1]

    if chunked:
        entry = []
        for dr in range(2):
            ar, ai = a_parts[dr]
            pr, pi = ar[0:1], ai[0:1]
            for _ in range(int(math.log2(seq))):
                pr, pi = pr * pr - pi * pi, 2.0 * pr * pi
            fr, fi = fin[2 * dr], fin[2 * dr + 1]
            s0 = s0_ref[dr, 0, 0]
            er, ei = s0[:, :sb], s0[:, sb:]
            order = range(rows) if dr == 0 else range(rows - 1, -1, -1)
            rows_r, rows_i = [None] * rows, [None] * rows
            for c in order:
                rows_r[c], rows_i[c] = er, ei
                er, ei = (pr * er - pi * ei + fr[c:c + 1], pr * ei + pi * er + fi[c:c + 1])
            entry += [jnp.concatenate(rows_r, axis=0), jnp.concatenate(rows_i, axis=0)]
        carry = tuple(entry)
        for stage in range(n_slices):
            carry = fix_slice(stage, carry)
            if stage >= 1:
                output_proj(stage - 1)
    output_proj(n_slices - 1)

    u2 = u_ref[...].reshape(seq * rows, SSM_UBLK)
    y = _gelu_tanh(yacc_ref[...] + d_ref[...] * u2)
    y_ref[...] = y.reshape(seq, rows, SSM_UBLK)


def _ssm(u_t, l, w, s0, want_final):
    seq, r, _ = u_t.shape
    chunked = s0 is not None
    rows = V7X_SUBLANES
    in_specs = [pl.BlockSpec((seq, rows, SSM_UBLK), lambda i, j: (0, i, j)),
                pl.BlockSpec((None, 2, 1, SSM_UBLK, 2 * SSM_SBLK), lambda i, j: (l, 0, j, 0, 0)),
                pl.BlockSpec((None, 2, 1, 1, 2 * SSM_SBLK), lambda i, j: (l, 0, j, 0, 0)),
                pl.BlockSpec((None, 2, 1, 2 * SSM_SBLK, SSM_UBLK), lambda i, j: (l, 0, j, 0, 0)),
                pl.BlockSpec((None, 1, SSM_UBLK), lambda i, j: (l, 0, j))]
    args = [u_t, w["ssm_b"], w["ssm_a"], w["ssm_c"], w["ssm_d"]]
    if chunked:
        in_specs.append(pl.BlockSpec((None, 2, 1, 1, 1, 2 * SSM_SBLK), lambda i, j: (l, 0, i, j, 0, 0)))
        args.append(s0)
    out_specs = [pl.BlockSpec((seq, rows, SSM_UBLK), lambda i, j: (0, i, j))]
    out_shape = [jax.ShapeDtypeStruct((seq, r, SSM_WIDTH), F32)]
    if want_final:
        out_specs.append(pl.BlockSpec((2, 1, rows, 2 * SSM_SBLK), lambda i, j: (0, j, i, 0)))
        out_shape.append(jax.ShapeDtypeStruct((2, SSM_NBLK, r, 2 * SSM_SBLK), F32))
    return pl.pallas_call(
        functools.partial(_ssm_kernel, chunked, want_final),
        grid=(r // rows, SSM_NBLK),
        in_specs=in_specs, out_specs=out_specs, out_shape=out_shape,
        scratch_shapes=[pltpu.VMEM((2, seq * rows, 2 * SSM_SBLK), F32), pltpu.VMEM((seq * rows, SSM_UBLK), F32)],
        compiler_params=_params(2),
        name="ssm_latent" if chunked else "ssm_context",
    )(*args)


def _merge_kernel(last, x_ref, sh_ref, sc_ref, ga_ref, g_ref, oa_ref, ob_ref, oc_ref, y_ref,
                  wg_ref, wm_ref, wglu_ref, bglu_ref, wbr_ref, wout_ref, fn_ref, o_ref):
    x = x_ref[...]
    h = _modulated(x, g_ref, sh_ref, sc_ref).astype(BF16)
    y = y_ref[...]
    o_d = y * _sigmoid(_dot(y.astype(BF16), wglu_ref[...]) + bglu_ref[...])
    branches = (oa_ref[...].astype(F32), ob_ref[...].astype(F32), oc_ref[...].astype(F32), o_d)
    acc = None
    for n, o in enumerate(branches):
        gate = _dot(h, wg_ref[:, n * BRANCH_WIDTH:(n + 1) * BRANCH_WIDTH])
        br = _dot((o * (gate * _sigmoid(gate))).astype(BF16), wbr_ref[n])
        term = _sigmoid(_dot(h, wm_ref[:, n * D_MODEL:(n + 1) * D_MODEL])) * br
        acc = term if acc is None else acc + term
    xn = x + ga_ref[0] * _dot(acc.astype(BF16), wout_ref[...])
    if last:
        xn = _rms(xn) * fn_ref[...]
    o_ref[...] = xn


def _merge(x, shift, scale, gate, seq, oa, ob, oc, y, l, w, final_norm, last):
    n = x.shape[0]
    tm = MERGE_TM
    row = lambda i: (i, 0)
    mod_spec = _mod_spec(shift.shape[0], tm, seq)
    in_specs = [pl.BlockSpec((tm, D_MODEL), row), mod_spec, mod_spec, mod_spec, _layer_spec((1, D_MODEL), l),
                pl.BlockSpec((tm, 512), row), pl.BlockSpec((tm, 512), row), pl.BlockSpec((tm, 512), row),
                pl.BlockSpec((tm, 512), row),
                _layer_spec((D_MODEL, N_BRANCH * BRANCH_WIDTH), l), _layer_spec((D_MODEL, N_BRANCH * D_MODEL), l),
                _layer_spec((SSM_WIDTH, SSM_WIDTH), l), _layer_spec((1, SSM_WIDTH), l),
                _layer_spec((N_BRANCH, BRANCH_WIDTH, D_MODEL), l), _layer_spec((D_MODEL, D_MODEL), l),
                _const_spec((1, D_MODEL))]
    return pl.pallas_call(
        functools.partial(_merge_kernel, last),
        grid=(n // tm,),
        in_specs=in_specs,
        out_specs=pl.BlockSpec((tm, D_MODEL), row),
        out_shape=jax.ShapeDtypeStruct((n, D_MODEL), F32),
        compiler_params=_params(1),
        name="merge",
    )(x, shift, scale, gate, w["norm_g"], oa, ob, oc, y, w["wg"], w["wm"], w["wglu"], w["bglu"],
      w["wbr"], w["wout"], final_norm)


def _rope_tables(n_tok, rot_dim):
    rows = n_tok // GRID_W
    row = jnp.repeat(jnp.arange(rows, dtype=F32), GRID_W)
    col = jnp.tile(jnp.arange(GRID_W, dtype=F32), rows)
    quarter = rot_dim // 4
    inv = ROPE_THETA ** (-jnp.arange(quarter, dtype=F32) / quarter)
    ang = jnp.concatenate([row[:, None] * inv, col[:, None] * inv], axis=-1)
    return jnp.cos(ang), jnp.sin(ang)


def _rope_lane_tables(n_tok):
    c, s = _rope_tables(n_tok, MLA_ROPE)
    ones = jnp.ones((n_tok, MLA_NOPE), F32)
    pad = MLA_HEAD_PAD - MLA_NOPE - MLA_ROPE
    c_mla = jnp.concatenate([ones, c, c, jnp.ones((n_tok, pad), F32)], axis=1)
    s_mla = jnp.concatenate([0.0 * ones, -s, s, jnp.zeros((n_tok, pad), F32)], axis=1)
    c, s = _rope_tables(n_tok, DIFF_HEAD_DIM)
    c_64 = jnp.concatenate([c, c, c, c], axis=1)
    s_64 = jnp.concatenate([-s, s, -s, s], axis=1)
    return c_mla, s_mla, c_64, s_64


def _block_diag(blocks):
    g = blocks.shape[-3]
    eye = jnp.eye(g, dtype=blocks.dtype)
    out = jnp.einsum("...grc,gh->...grhc", blocks, eye)
    return out.reshape(blocks.shape[:-3] + (g * blocks.shape[-2], g * blocks.shape[-1]))


def _in_output_half(v):
    zeros = jnp.zeros_like(v[..., 0, :])
    return jnp.stack([jnp.concatenate([v[..., 0, :], zeros], axis=-1),
                      jnp.concatenate([zeros, v[..., 1, :]], axis=-1)], axis=-2)


def _weights(w_in, p):
    w_t = jnp.swapaxes(w_in, 1, 2).reshape(DEPTH * D_IN, D_MODEL)
    wmix = _transposed_rows(w_t, _IN_MIX_START, _M_WIDTH, _M_WIDTH // 2)
    wg = _transposed_rows(w_t, _IN_GATE_START, N_BRANCH * BRANCH_WIDTH, 1024)
    wm = _transposed_rows(w_t, _IN_MERGE_START, N_BRANCH * D_MODEL, 1024)
    zeros = lambda wd: jnp.zeros((DEPTH, D_MODEL, wd), F32)
    kpe_lo = MLA_Q_LORA + MLA_KV_LORA
    whead = jnp.concatenate([w_in[:, :, :kpe_lo], zeros(MLA_NOPE), w_in[:, :, kpe_lo:_IN_MIX_START],
                             zeros(MLA_HEAD_PAD - MLA_NOPE - MLA_ROPE)], axis=2).astype(BF16)

    hq = MLA_NOPE + MLA_ROPE
    wqb = p["w_mla_q_b"].reshape(DEPTH, MLA_Q_LORA, MLA_HEADS, hq)
    wqb = jnp.pad(wqb, ((0, 0), (0, 0), (0, 0), (0, MLA_HEAD_PAD - hq))).reshape(DEPTH, MLA_Q_LORA, -1).astype(BF16)
    wkv = p["w_mla_kv_b"].reshape(DEPTH, MLA_KV_LORA, MLA_HEADS, MLA_NOPE + MLA_V)
    wk = jnp.pad(wkv[..., :MLA_NOPE], ((0, 0), (0, 0), (0, 0), (0, MLA_HEAD_PAD - MLA_NOPE)))
    wk = wk.reshape(DEPTH, MLA_KV_LORA, -1).astype(BF16)
    wv = _in_output_half(wkv[..., MLA_NOPE:].reshape(DEPTH, MLA_KV_LORA, MLA_HEADS // 2, 2, MLA_V))
    wv = wv.reshape(DEPTH, MLA_KV_LORA, -1).astype(BF16)
    eye = jnp.eye(GQA_HEAD_DIM, dtype=F32)
    egv = _block_diag(jnp.broadcast_to(
        _in_output_half(jnp.stack([eye, eye], axis=1)).reshape(GQA_HEAD_DIM, 2 * V_PAD),
        (GQA_KV_HEADS, GQA_HEAD_DIM, 2 * V_PAD))).astype(BF16)

    gmat =_block_diag(jnp.full((GQA_HEADS, GQA_HEAD_DIM, GQA_HEAD_DIM), 1.0 / GQA_HEAD_DIM, F32)).astype(BF16)

    lam = lax.complex(p["ssm_a_re"], p["ssm_a_im"])
    dt = jnp.exp(p["ssm_log_dt"])[..., None]
    abar = jnp.exp(lam * dt)
    bbar = ((abar - 1.0) / lam)[..., None] * lax.complex(p["ssm_b_re"], p["ssm_b_im"])
    blk = lambda a: a.reshape((DEPTH, 2, SSM_NBLK, SSM_GBLOCK) + a.shape[3:])
    b_t = jnp.swapaxes(blk(bbar), -1, -2)
    ssm_b = jnp.concatenate([_block_diag(jnp.real(b_t)), _block_diag(jnp.imag(b_t))], axis=-1).astype(BF16)
    ab = blk(abar).reshape(DEPTH, 2, SSM_NBLK, 1, SSM_SBLK)
    ssm_a = jnp.concatenate([jnp.real(ab), jnp.imag(ab)], axis=-1)
    c_re = jnp.swapaxes(blk(p["ssm_c_re"]), -1, -2)
    c_im = jnp.swapaxes(blk(p["ssm_c_im"]), -1, -2)
    ssm_c = jnp.concatenate([_block_diag(c_re), -_block_diag(c_im)], axis=-2).astype(BF16)

    row = lambda a: a[:, None, :]
    return dict(
        norm_g=row(p["norm_g"]), whead=whead, wmix=wmix, qn=row(p["mla_q_norm"]), wqb=wqb,
        kvn=row(p["mla_kv_norm"]), wk=wk, wv=wv,
        gqn=row(jnp.tile(p["gqa_q_norm"], (1, GQA_HEADS))), gkn=row(jnp.tile(p["gqa_k_norm"], (1, GQA_KV_HEADS))),
        gmat=gmat, egv=egv,
        lam_parts=jnp.stack([p["diff_lq1"], p["diff_lk1"], p["diff_lq2"], p["diff_lk2"]], axis=1),
        subln=row(p["diff_subln"]),
        ssm_b=ssm_b, ssm_a=ssm_a, ssm_c=ssm_c, ssm_d=row(p["ssm_d"]),
        wg=wg, wm=wm, wglu=p["ssm_glu_w"].astype(BF16), bglu=row(p["ssm_glu_b"]),
        wbr=p["w_branch_out"].astype(BF16), wout=p["w_out"].astype(BF16),
    )


def _mixers(pr, b, t, l, w, lam_init, ctx):
    q, k, v, dq, dk, dv, gq, gk, gv, u = pr[:10]
    three = lambda a: a.reshape(b, t, a.shape[-1])
    k, v, dk, dv, gk, gv = (three(a) for a in (k, v, dk, dv, gk, gv))
    if ctx is not None:
        cat = lambda c, a: jnp.concatenate([c, a], axis=1)
        k, v, dk, dv, gk, gv = (cat(c, a) for c, a in zip(ctx["kv"], (k, v, dk, dv, gk, gv)))
    mla_heads = tuple((hd, hd) for hd in range(MLA_HEADS))
    gqa_heads = tuple((hd // GQA_GROUP, 2 * (hd // GQA_GROUP) + hd % 2) for hd in range(GQA_HEADS))
    o_a = _attention(functools.partial(_attn_heads_kernel, mla_heads, MLA_HEAD_PAD), "attn_mla",
                     three(q), k, v)
    o_b = _attention(functools.partial(_diff_attn_kernel, lam_init), "attn_diff", three(dq), dk, dv,
                     extra_specs=(_layer_spec((4, DIFF_HEAD_DIM), l), _layer_spec((1, DIFF_V), l)),
                     extra=(w["lam_parts"], w["subln"]))
    o_c = _attention(functools.partial(_attn_heads_kernel, gqa_heads, GQA_HEAD_DIM), "attn_gqa",
                     three(gq), gk, gv)
    n = b * t
    if ctx is None:
        u_t = jnp.swapaxes(u.reshape(b, t, SSM_WIDTH), 0, 1)
        y_t, fin = _ssm(u_t, l, w, None, True)
        y = jnp.swapaxes(y_t, 0, 1).reshape(n, SSM_WIDTH)
    else:
        chunks = t // SSM_SEQ
        u_t = jnp.transpose(u.reshape(b, chunks, SSM_SEQ, SSM_WIDTH), (2, 0, 1, 3)).reshape(SSM_SEQ, b * chunks, SSM_WIDTH)
        (y_t,) = _ssm(u_t, l, w, ctx["s0"], False)
        y = jnp.transpose(y_t.reshape(SSM_SEQ, b, chunks, SSM_WIDTH), (1, 2, 0, 3)).reshape(n, SSM_WIDTH)
        fin = None
    flat = lambda a: a.reshape(n, 512)
    return flat(o_a), flat(o_b), flat(o_c), y, fin


def kernel(x_prompt, x_sample, cache_mla_ckv, cache_mla_krope, cache_diff_k, cache_diff_v, cache_gqa_k, cache_gqa_v, state_ssm, c, c_ctx, norm_g, w_mod, b_mod, w_in, mla_q_norm, w_mla_q_b, mla_kv_norm, w_mla_kv_b, diff_lq1, diff_lk1, diff_lq2, diff_lk2, diff_subln, gqa_q_norm, gqa_k_norm, ssm_a_re, ssm_a_im, ssm_log_dt, ssm_b_re, ssm_b_im, ssm_c_re, ssm_c_im, ssm_d, ssm_glu_w, ssm_glu_b, w_branch_out, w_out, final_norm):
    p = dict(norm_g=norm_g, mla_q_norm=mla_q_norm, w_mla_q_b=w_mla_q_b, mla_kv_norm=mla_kv_norm,
             w_mla_kv_b=w_mla_kv_b, diff_lq1=diff_lq1, diff_lk1=diff_lk1, diff_lq2=diff_lq2, diff_lk2=diff_lk2,
             diff_subln=diff_subln, gqa_q_norm=gqa_q_norm, gqa_k_norm=gqa_k_norm, ssm_a_re=ssm_a_re,
             ssm_a_im=ssm_a_im, ssm_log_dt=ssm_log_dt, ssm_b_re=ssm_b_re, ssm_b_im=ssm_b_im, ssm_c_re=ssm_c_re,
             ssm_c_im=ssm_c_im, ssm_d=ssm_d, ssm_glu_w=ssm_glu_w, ssm_glu_b=ssm_glu_b,
             w_branch_out=w_branch_out, w_out=w_out)
    bp, tp, _ = x_prompt.shape
    bs, ts, _ = x_sample.shape
    past = cache_mla_ckv.shape[2]
    assert tp == SSM_SEQ and ts % SSM_SEQ == 0 and ts // SSM_SEQ == V7X_SUBLANES
    rope = _rope_lane_tables(ts)
    w = _weights(w_in, p)

    cvec = jnp.zeros((V7X_SUBLANES, D_MODEL), F32).at[0].set(c_ctx).at[1:1 + bs].set(c)
    mod = _modulation(cvec, w_mod, b_mod)

    pad_lo = MLA_NOPE
    pad_hi = MLA_HEAD_PAD - MLA_NOPE - MLA_ROPE
    kpe_ctx = jnp.pad(cache_mla_krope, ((0, 0), (0, 0), (0, 0), (pad_lo, pad_hi)))
    flat_ctx = lambda a: a.reshape(bs, DEPTH, past, -1).astype(BF16)
    gv_both = _in_output_half(jnp.stack([cache_gqa_v, cache_gqa_v], axis=-2))
    dk_ctx, dv_ctx, gk_ctx, gv_ctx = (flat_ctx(a) for a in (cache_diff_k, cache_diff_v, cache_gqa_k, gv_both))
    s0_all = state_ssm.reshape(bs, DEPTH, 2, SSM_NBLK, SSM_GBLOCK * SSM_STATE, 2)
    s0_all = jnp.transpose(s0_all, (1, 2, 0, 3, 5, 4)).reshape(DEPTH, 2, bs, SSM_NBLK, 1, 2 * SSM_SBLK)

    xp = x_prompt.reshape(bp * tp, D_MODEL)
    xs = x_sample.reshape(bs * ts, D_MODEL)
    fn = final_norm[None]
    sides = ()
    states = []
    for l in range(DEPTH):
        lam_init = 0.8 - 0.6 * math.exp(-0.3 * l)
        last = l == DEPTH - 1
        sh, sc, ga = (mod[l, :, i * D_MODEL:(i + 1) * D_MODEL][:, None, :] for i in range(3))

        pr = _proj(xp, sh[0:1], sc[0:1], tp, l, w, None, False, sides)
        o_a, o_b, o_c, y, fin = _mixers(pr, bp, tp, l, w, lam_init, None)
        xp = _merge(xp, sh[0:1], sc[0:1], ga[0:1], tp, o_a, o_b, o_c, y, l, w, fn, last)
        sides = tuple(pr[10:])
        fin = fin.reshape(2, SSM_NBLK, bp, 2, SSM_GBLOCK, SSM_STATE)
        states.append(jnp.transpose(fin, (2, 0, 1, 4, 5, 3)).reshape(bp, 2, SSM_GROUPS, SSM_STATE, 2))

        k_ctx, v_ctx = _ctx_kv(cache_mla_ckv[:, l].reshape(bs * past, MLA_KV_LORA),
                               kpe_ctx[:, l].reshape(bs * past, MLA_HEAD_PAD), l, w)
        ctx = dict(
            kv=(k_ctx.reshape(bs, past, -1), v_ctx.reshape(bs, past, -1),
                dk_ctx[:, l], dv_ctx[:, l], gk_ctx[:, l], gv_ctx[:, l]),
            s0=s0_all)
        pr = _proj(xs, sh[1:1 + bs], sc[1:1 + bs], ts, l, w, rope, True)
        o_a, o_b, o_c, y, _ = _mixers(pr, bs, ts, l, w, lam_init, ctx)
        xs = _merge(xs, sh[1:1 + bs], sc[1:1 + bs], ga[1:1 + bs], ts, o_a, o_b, o_c, y, l, w, fn, last)

    ckv_s, kpe_t, dk_t, dv_s, gk_t, gv_t = sides
    untranspose = lambda a: jnp.swapaxes(a, 2, 3)
    return (xp.reshape(bp, tp, D_MODEL), xs.reshape(bs, ts, D_MODEL),
            ckv_s, untranspose(kpe_t),
            untranspose(dk_t).reshape(bp, DEPTH, tp, DIFF_HEADS, 2, DIFF_HEAD_DIM), dv_s,
            untranspose(gk_t).reshape(bp, DEPTH, tp, GQA_KV_HEADS, GQA_HEAD_DIM),
            untranspose(gv_t).reshape(bp, DEPTH, tp, GQA_KV_HEADS, GQA_HEAD_DIM),
            jnp.stack(states, axis=1))
```

```python
import functools
import math

import jax
import jax.numpy as jnp
from jax import lax
from jax.experimental import pallas as pl
from jax.experimental.pallas import tpu as pltpu

F32 = jnp.float32
BF16 = jnp.bfloat16

D_MODEL = 1024
DEPTH = 2
GRID_W = 64
ROPE_THETA = 10000.0
EPS = 1e-6

MLA_HEADS = 8
MLA_NOPE = 64
MLA_ROPE = 32
MLA_V = 64
MLA_Q_LORA = 256
MLA_KV_LORA = 128
MLA_SCALE = (MLA_NOPE + MLA_ROPE) ** -0.5
MLA_HEAD_PAD = 128
DIFF_HEADS = 4
DIFF_HEAD_DIM = 64
DIFF_V = 2 * DIFF_HEAD_DIM
DIFF_SCALE = DIFF_HEAD_DIM ** -0.5
GQA_HEADS = 8
GQA_KV_HEADS = 2
GQA_HEAD_DIM = 64
GQA_GROUP = GQA_HEADS // GQA_KV_HEADS
GQA_KV_WIDTH = GQA_KV_HEADS * GQA_HEAD_DIM
GQA_SCALE = GQA_HEAD_DIM ** -0.5
SSM_WIDTH = 512
SSM_GROUP = 16
SSM_GROUPS = SSM_WIDTH // SSM_GROUP
SSM_STATE = 64
N_BRANCH = 4
BRANCH_WIDTH = 512

IN_SPLITS = (MLA_Q_LORA, MLA_KV_LORA, MLA_ROPE, 512, 512, 512, 512, 128, 128, SSM_WIDTH,
             N_BRANCH * BRANCH_WIDTH, N_BRANCH * D_MODEL)
D_IN = sum(IN_SPLITS)

V7X_LANES = 128
V7X_SUBLANES = 8
V7X_VMEM_LIMIT_BYTES = 56 * 1024 * 1024

_H_QA = (0, 256)
_H_KVA = (256, 384)
_H_KPE = (384, 512)
_H_WIDTH = 512
_M_DQ = (0, 512)
_M_DK = (512, 1024)
_M_DV = (1024, 1536)
_M_GQ = (1536, 2048)
_M_GK = (2048, 2176)
_M_GV = (2176, 2304)
_M_U = (2304, 2816)
_M_WIDTH = 2816
_IN_MIX_START = MLA_Q_LORA + MLA_KV_LORA + MLA_ROPE
_IN_GATE_START = _IN_MIX_START + _M_WIDTH
_IN_MERGE_START = _IN_GATE_START + N_BRANCH * BRANCH_WIDTH

SSM_GBLOCK = 8
SSM_NBLK = SSM_GROUPS // SSM_GBLOCK
SSM_SBLK = SSM_GBLOCK * SSM_STATE
SSM_UBLK = SSM_GBLOCK * SSM_GROUP
SSM_SEQ = 256
SSM_SLICE = 32

V_PAD = 128

PROJ_TM = 512
MERGE_TM = 512
ATTN_TQ = 256
ATTN_SEQ_PER_STEP = 4
ATTN_GROUP_SHORT = 32
ATTN_GROUP_LONG = 4


def _params(n_axes, vmem=V7X_VMEM_LIMIT_BYTES):
    return pltpu.CompilerParams(dimension_semantics=("arbitrary",) * n_axes, vmem_limit_bytes=vmem)


def _const_spec(shape):
    zeros = (0,) * len(shape)
    return pl.BlockSpec(shape, lambda *_: zeros, pipeline_mode=pl.Buffered(1))


def _layer_spec(shape, l):
    zeros = (0,) * len(shape)
    return pl.BlockSpec((None,) + tuple(shape), lambda *_: (l,) + zeros, pipeline_mode=pl.Buffered(1))


def _mod_spec(n_rows, tm, seq):
    if n_rows == 1:
        return pl.BlockSpec((1, 1, D_MODEL), lambda i: (0, 0, 0))
    return pl.BlockSpec((1, 1, D_MODEL), lambda i: (i * tm // seq, 0, 0))


def _dot(a, b):
    return jnp.dot(a, b, preferred_element_type=F32)


def _dot_nt(a, b):
    return lax.dot_general(a, b, (((1,), (1,)), ((), ())), preferred_element_type=F32)


def _rms(x):
    return x * lax.rsqrt(jnp.mean(x * x, axis=-1, keepdims=True) + EPS)


def _sigmoid(x):
    return 1.0 / (1.0 + jnp.exp(-x))


def _gelu_tanh(x):
    return 0.5 * x * (1.0 + jnp.tanh(math.sqrt(2.0 / math.pi) * (x + 0.044715 * (x * x * x))))


def _group_mean_sq(x, gmat):
    return _dot((x * x).astype(BF16), gmat)


def _rope_chunk(x, cos, sin, half):
    lane = lax.broadcasted_iota(jnp.int32, x.shape, 1)
    first = (lane % (2 * half)) < half
    swapped = jnp.where(first, pltpu.roll(x, V7X_LANES - half, 1), pltpu.roll(x, half, 1))
    return x * cos + swapped * sin


def _transpose_cast_kernel(w_ref, o_ref):
    o_ref[...] = w_ref[...].T.astype(BF16)


def _transposed_rows(w_t, row0, n_rows, blk):
    return pl.pallas_call(
        _transpose_cast_kernel,
        grid=(DEPTH, n_rows // blk),
        in_specs=[pl.BlockSpec((pl.Element(blk), pl.Element(D_MODEL)),
                               lambda l, j: (pl.multiple_of(l * D_IN + row0 + j * blk, V7X_SUBLANES), 0))],
        out_specs=pl.BlockSpec((None, D_MODEL, blk), lambda l, j: (l, 0, j)),
        out_shape=jax.ShapeDtypeStruct((DEPTH, D_MODEL, n_rows), BF16),
        compiler_params=_params(2),
        name="w_in_columns",
    )(w_t)


def _mod_kernel(c_ref, w_ref, b_ref, o_ref):
    c = c_ref[...]
    a = (c * _sigmoid(c)).astype(BF16)
    o_ref[0] = _dot(a, w_ref[0].astype(BF16)) + b_ref[0]


def _modulation(cvec, w_mod, b_mod):
    tn = 512
    return pl.pallas_call(
        _mod_kernel,
        grid=(DEPTH, 3 * D_MODEL // tn),
        in_specs=[pl.BlockSpec((V7X_SUBLANES, D_MODEL), lambda l, j: (0, 0)),
                  pl.BlockSpec((1, D_MODEL, tn), lambda l, j: (l, 0, j)),
                  pl.BlockSpec((1, 1, tn), lambda l, j: (l, 0, j))],
        out_specs=pl.BlockSpec((1, V7X_SUBLANES, tn), lambda l, j: (l, 0, j)),
        out_shape=jax.ShapeDtypeStruct((DEPTH, V7X_SUBLANES, 3 * D_MODEL), F32),
        compiler_params=_params(2),
        name="modulation",
    )(cvec, w_mod, b_mod.reshape(DEPTH, 1, 3 * D_MODEL))


def _modulated(x, g_ref, sh_ref, sc_ref):
    return (_rms(x) * g_ref[...]) * (1.0 + sc_ref[0]) + sh_ref[0]


def _proj_kernel(latent, n_prev, first_layer, *refs):
    (x_ref, sh_ref, sc_ref, g_ref, wh_ref, wx_ref, qn_ref, wqb_ref, kvn_ref, wk_ref, wv_ref,
     gqn_ref, gkn_ref, gm_ref, egv_ref) = refs[:15]
    pos = 15
    if latent:
        c_mla, s_mla, c_64, s_64 = refs[pos:pos + 4]
        cached = refs[pos + 4:pos + 10]
        pos += 10
    pos += n_prev
    (q_o, k_o, v_o, dq_o, dk_o, dv_o, gq_o, gk_o, gv_o, u_o) = refs[pos:pos + 10]
    pos += 10
    if not latent:
        ckv_f, kpe_t, dk_t, dv_f, gk_t, gv_t = refs[pos:pos + 6]

    def mla_keys_values(ckv_b, kpe):
        kn = _dot(ckv_b, wk_ref[...])
        for hd in range(MLA_HEADS):
            sl = slice(hd * MLA_HEAD_PAD, (hd + 1) * MLA_HEAD_PAD)
            k_o[:, sl] = (kn[:, sl] + kpe).astype(BF16)
        v_o[...] = _dot(ckv_b, wv_ref[...]).astype(BF16)

    def cached_tokens():
        ckv_c, kpe_c, dk_c, dv_c, gk_c, gv_c = cached
        mla_keys_values(ckv_c[...].astype(BF16), kpe_c[...])
        dk_o[...] = dk_c[...]
        dv_o[...] = dv_c[...]
        gk_o[...] = gk_c[...]
        gv_o[...] = gv_c[...]

    def new_tokens():
        _proj_new_tokens(latent, first_layer, refs[:15], (c_mla, s_mla, c_64, s_64) if latent else None,
                         refs[pos - 10:pos], refs[pos:pos + 6] if not latent else None, mla_keys_values)

    if latent:
        pl.when(pl.program_id(1) == 0)(cached_tokens)
        pl.when(pl.program_id(1) > 0)(new_tokens)
    else:
        new_tokens()


def _proj_new_tokens(latent, first_layer, in_refs, rope_refs, out_refs, side_refs, mla_keys_values):
    (x_ref, sh_ref, sc_ref, g_ref, wh_ref, wx_ref, qn_ref, wqb_ref, kvn_ref, wk_ref, wv_ref,
     gqn_ref, gkn_ref, gm_ref, egv_ref) = in_refs
    (q_o, k_o, v_o, dq_o, dk_o, dv_o, gq_o, gk_o, gv_o, u_o) = out_refs
    if latent:
        c_mla, s_mla, c_64, s_64 = rope_refs
    else:
        ckv_f, kpe_t, dk_t, dv_f, gk_t, gv_t = side_refs

    h = _modulated(x_ref[...], g_ref, sh_ref, sc_ref).astype(BF16)
    tm = h.shape[0]

    p_head = _dot(h, wh_ref[...])
    p_mix = _dot(h, wx_ref[...])

    def head(cols):
        return p_head[:, cols[0]:cols[1]]

    def mix(cols):
        return p_mix[:, cols[0]:cols[1]]

    def rope_mla(x):
        return _rope_chunk(x, c_mla[...], s_mla[...], MLA_ROPE // 2) if latent else x

    def rope_64(x):
        if not latent:
            return x
        c, s = c_64[...], s_64[...]
        chunks = [_rope_chunk(x[:, i:i + V7X_LANES], c, s, DIFF_HEAD_DIM // 2)
                  for i in range(0, x.shape[1], V7X_LANES)]
        return chunks[0] if len(chunks) == 1 else jnp.concatenate(chunks, axis=1)

    def per_sequence(x):
        return [x[b * SSM_SEQ:(b + 1) * SSM_SEQ] for b in range(tm // SSM_SEQ)]

    def slot(b):
        return (b,) if first_layer is None else (b, first_layer)

    def zero_other_slots(ref):
        if first_layer is not None:
            for b in range(ref.shape[0]):
                for other in range(ref.shape[1]):
                    if other != first_layer:
                        ref[b, other] = jnp.zeros(ref.shape[2:], ref.dtype)

    qn = (_rms(head(_H_QA)) * qn_ref[...]).astype(BF16)
    q = _dot(qn, wqb_ref[...])
    for hd in range(MLA_HEADS):
        sl = slice(hd * MLA_HEAD_PAD, (hd + 1) * MLA_HEAD_PAD)
        q_o[:, sl] = (rope_mla(q[:, sl]) * MLA_SCALE).astype(BF16)
    ckv = _rms(head(_H_KVA)) * kvn_ref[...]
    kpe = head(_H_KPE)
    if not latent:
        for b, (c_b, k_b) in enumerate(zip(per_sequence(ckv), per_sequence(kpe))):
            ckv_f[slot(b)] = c_b
            kpe_t[slot(b)] = k_b.T[MLA_NOPE:MLA_NOPE + MLA_ROPE, :]
        zero_other_slots(ckv_f)
        zero_other_slots(kpe_t)
    mla_keys_values(ckv.astype(BF16), rope_mla(kpe))

    dq_o[...] = (rope_64(mix(_M_DQ)) * DIFF_SCALE).astype(BF16)
    dk = rope_64(mix(_M_DK))
    dv = mix(_M_DV)
    dk_o[...] = dk.astype(BF16)
    dv_o[...] = dv.astype(BF16)

    gq = mix(_M_GQ)
    gq = gq * lax.rsqrt(_group_mean_sq(gq, gm_ref[...]) + EPS) * gqn_ref[...]
    gq_o[...] = (rope_64(gq) * GQA_SCALE).astype(BF16)
    gk = mix(_M_GK)
    kw = GQA_KV_WIDTH
    gk = gk * lax.rsqrt(_group_mean_sq(gk, gm_ref[:kw, :kw]) + EPS) * gkn_ref[...]
    gv = mix(_M_GV)
    if not latent:
        for b, (dk_b, dv_b, gk_b, gv_b) in enumerate(zip(*(per_sequence(a) for a in (dk, dv, gk, gv)))):
            dk_t[slot(b)] = dk_b.T
            for hd in range(DIFF_HEADS):
                dv_f[slot(b) + (slice(None), hd, slice(None))] = dv_b[:, hd * DIFF_V:(hd + 1) * DIFF_V]
            gk_t[slot(b)] = gk_b.T
            gv_t[slot(b)] = gv_b.T
        for ref in (dk_t, dv_f, gk_t, gv_t):
            zero_other_slots(ref)
    gk_o[...] = rope_64(gk).astype(BF16)
    gv_o[...] = _dot(gv.astype(BF16), egv_ref[...]).astype(BF16)

    u_o[...] = mix(_M_U)


def _side_shapes(n_seq):
    return [((n_seq, DEPTH, SSM_SEQ, MLA_KV_LORA), (SSM_SEQ, MLA_KV_LORA)),
            ((n_seq, DEPTH, MLA_ROPE, SSM_SEQ), (MLA_ROPE, SSM_SEQ)),
            ((n_seq, DEPTH, 512, SSM_SEQ), (512, SSM_SEQ)),
            ((n_seq, DEPTH, SSM_SEQ, DIFF_HEADS, DIFF_V), (SSM_SEQ, DIFF_HEADS, DIFF_V)),
            ((n_seq, DEPTH, GQA_KV_WIDTH, SSM_SEQ), (GQA_KV_WIDTH, SSM_SEQ)),
            ((n_seq, DEPTH, GQA_KV_WIDTH, SSM_SEQ), (GQA_KV_WIDTH, SSM_SEQ))]


def _proj(x, shift, scale, seq, l, w, rope=None, cached=None, prev_sides=()):
    n = x.shape[0]
    tm = PROJ_TM
    latent = cached is not None
    kw = GQA_KV_WIDTH
    if latent:
        tiles = seq // tm
        assert cached[0].shape[2] == tm
        grid = (n // seq, 1 + tiles)
        row = lambda b, j: (b * tiles + jnp.maximum(j - 1, 0), 0)
        kv_row = lambda b, j: (b * (1 + tiles) + j, 0)
        n_kv = (n // seq) * (cached[0].shape[2] + seq)
        mod_spec = pl.BlockSpec((1, 1, D_MODEL), lambda b, j: (b, 0, 0))
    else:
        grid = (n // tm,)
        row = kv_row = lambda i: (i, 0)
        n_kv = n
        mod_spec = _mod_spec(shift.shape[0], tm, seq)
    in_specs = [pl.BlockSpec((tm, D_MODEL), row), mod_spec, mod_spec,
                _layer_spec((1, D_MODEL), l), _layer_spec((D_MODEL, _H_WIDTH), l), _layer_spec((D_MODEL, _M_WIDTH), l),
                _layer_spec((1, MLA_Q_LORA), l), _layer_spec((MLA_Q_LORA, MLA_HEADS * MLA_HEAD_PAD), l),
                _layer_spec((1, MLA_KV_LORA), l), _layer_spec((MLA_KV_LORA, MLA_HEADS * MLA_HEAD_PAD), l),
                _layer_spec((MLA_KV_LORA, MLA_HEADS * V_PAD), l),
                _layer_spec((1, 512), l), _layer_spec((1, kw), l), _const_spec((512, 512)),
                _const_spec((kw, 2 * GQA_KV_HEADS * V_PAD))]
    args = [x, shift, scale, w["norm_g"], w["whead"], w["wmix"], w["qn"], w["wqb"], w["kvn"], w["wk"], w["wv"],
            w["gqn"], w["gkn"], w["gmat"], w["egv"]]
    if latent:
        tab = pl.BlockSpec((tm, V7X_LANES), lambda b, j: (jnp.maximum(j - 1, 0), 0))
        in_specs += [tab] * 4
        args += list(rope)
        in_specs += [pl.BlockSpec((None, None) + c.shape[2:], lambda b, j: (b, l, 0, 0)) for c in cached]
        args += list(cached)
    n_fixed = len(args)
    in_specs += [pl.BlockSpec(memory_space=pl.ANY)] * len(prev_sides)
    args += list(prev_sides)
    widths = [(1024, BF16, False), (1024, BF16, True), (MLA_HEADS * V_PAD, BF16, True), (512, BF16, False),
              (512, BF16, True), (512, BF16, True), (512, BF16, False), (kw, BF16, True),
              (2 * GQA_KV_HEADS * V_PAD, BF16, True), (512, F32, False)]
    out_specs = [pl.BlockSpec((tm, wd), kv_row if kv else row) for wd, _, kv in widths]
    out_shape = [jax.ShapeDtypeStruct((n_kv if kv else n, wd), dt) for wd, dt, kv in widths]
    aliases = {}
    first_layer = None
    if not latent:
        per_tile = tm // SSM_SEQ
        first_layer = None if prev_sides else l
        for full, blk in _side_shapes(n // SSM_SEQ):
            zeros = (0,) * len(blk)
            if prev_sides:
                out_specs.append(pl.BlockSpec((per_tile, None) + blk, lambda i, zeros=zeros: (i, l) + zeros))
            else:
                out_specs.append(pl.BlockSpec((per_tile, DEPTH) + blk, lambda i, zeros=zeros: (i, 0) + zeros))
            out_shape.append(jax.ShapeDtypeStruct(full, F32))
        aliases = {n_fixed + k: len(widths) + k for k in range(len(prev_sides))}
    return pl.pallas_call(
        functools.partial(_proj_kernel, latent, len(prev_sides), first_layer),
        grid=grid,
        in_specs=in_specs, out_specs=out_specs, out_shape=out_shape,
        input_output_aliases=aliases,
        compiler_params=_params(len(grid)),
        name="proj_latent" if latent else "proj_context",
    )(*args)


def _softmax_maps(maps):
    scores = [_dot_nt(q, k) for q, k, _ in maps]
    tops = [jnp.max(s, axis=-1, keepdims=True) for s in scores]
    weights = [jnp.exp(s - m).astype(BF16) for s, m in zip(scores, tops)]
    outs = []
    for p, (_, _, v) in zip(weights, maps):
        ones = jnp.ones((v.shape[0], V7X_LANES), BF16)
        o = _dot(p, jnp.concatenate([v, ones], axis=1))
        outs.append((o[:, :V7X_LANES], o[:, V7X_LANES:]))
    return outs


def _map_group(n_keys):
    return ATTN_GROUP_SHORT if n_keys <= ATTN_TQ else ATTN_GROUP_LONG


def _attn_heads_kernel(heads, q_width, q_ref, k_ref, v_ref, o_ref):
    items = [(b, hd) for b in range(q_ref.shape[0]) for hd in range(len(heads))]
    group = _map_group(k_ref.shape[1])
    for g0 in range(0, len(items), group):
        part = items[g0:g0 + group]
        maps = [(q_ref[b, :, hd * q_width:(hd + 1) * q_width],
                 k_ref[b, :, heads[hd][0] * q_width:(heads[hd][0] + 1) * q_width],
                 v_ref[b, :, heads[hd][1] * V_PAD:(heads[hd][1] + 1) * V_PAD]) for b, hd in part]
        terms = [o * (1.0 / total) for o, total in _softmax_maps(maps)]
        for (b, hd), low, high in zip(part[0::2], terms[0::2], terms[1::2]):
            o_ref[b, :, (hd // 2) * V7X_LANES:(hd // 2 + 1) * V7X_LANES] = (low + high).astype(BF16)


def _diff_attn_kernel(lam_init, q_ref, k_ref, v_ref, lp_ref, sub_ref, o_ref):
    lp = lp_ref[...]
    lam = (jnp.exp(jnp.sum(lp[0:1] * lp[1:2], axis=-1, keepdims=True))
           - jnp.exp(jnp.sum(lp[2:3] * lp[3:4], axis=-1, keepdims=True)) + lam_init)
    d = DIFF_HEAD_DIM
    items = [(b, hd) for b in range(q_ref.shape[0]) for hd in range(DIFF_HEADS)]
    group = _map_group(k_ref.shape[1]) // 2
    for g0 in range(0, len(items), group):
        part = items[g0:g0 + group]
        maps = [(q_ref[b, :, c * d:(c + 1) * d], k_ref[b, :, c * d:(c + 1) * d],
                 v_ref[b, :, hd * DIFF_V:(hd + 1) * DIFF_V])
                for b, hd in part for c in (2 * hd, 2 * hd + 1)]
        outs = _softmax_maps(maps)
        mixed = [o1 * (1.0 / t1) - o2 * (lam / t2) for (o1, t1), (o2, t2) in zip(outs[0::2], outs[1::2])]
        normed = [_rms(o) * sub_ref[...] * (1.0 - lam_init) for o in mixed]
        for (b, hd), o in zip(part, normed):
            o_ref[b, :, hd * DIFF_V:(hd + 1) * DIFF_V] = o.astype(BF16)


def _attention(kernel, name, q, k, v, extra_specs=(), extra=()):
    b, t, wq = q.shape
    s = k.shape[1]
    tq = ATTN_TQ
    nb = ATTN_SEQ_PER_STEP if t == tq else 1
    in_specs = [pl.BlockSpec((nb, tq, wq), lambda i, j: (i, j, 0)),
                pl.BlockSpec((nb, s, k.shape[2]), lambda i, j: (i, 0, 0)),
                pl.BlockSpec((nb, s, v.shape[2]), lambda i, j: (i, 0, 0))]
    in_specs += list(extra_specs)
    return pl.pallas_call(
        kernel,
        grid=(b // nb, t // tq),
        in_specs=in_specs,
        out_specs=pl.BlockSpec((nb, tq, 512), lambda i, j: (i, j, 0)),
        out_shape=jax.ShapeDtypeStruct((b, t, 512), BF16),
        compiler_params=_params(2),
        name=name,
    )(q, k, v, *extra)


def _ssm_kernel(chunked, want_final, *refs):
    u_ref, bm_ref, ab_ref, cm_ref, d_ref = refs[:5]
    pos = 5
    if chunked:
        s0_ref = refs[pos]
        pos += 1
    y_ref = refs[pos]
    pos += 1
    if want_final:
        fin_ref = refs[pos]
        pos += 1
    hs_ref, yacc_ref = refs[pos:pos + 2]

    seq, rows, sb = SSM_SEQ, V7X_SUBLANES, SSM_SBLK
    n_slices = seq // SSM_SLICE
    slice_rows = SSM_SLICE * rows

    def visited(dr, stage):
        return stage if dr == 0 else n_slices - 1 - stage

    def rows_of(k):
        return slice(k * slice_rows, (k + 1) * slice_rows)

    def u_rows(k):
        return u_ref[k * SSM_SLICE:(k + 1) * SSM_SLICE].reshape(slice_rows, SSM_UBLK)

    def input_proj(stage):
        for dr in range(2):
            k = visited(dr, stage)
            hs_ref[dr, rows_of(k), :] = _dot(u_rows(k).astype(BF16), bm_ref[dr, 0])

    def output_proj(stage):
        for dr in range(2):
            r = rows_of(visited(dr, stage))
            part = _dot(hs_ref[dr, r, :].astype(BF16), cm_ref[dr, 0])
            if stage < n_slices // 2:
                yacc_ref[r, :] = part
            else:
                yacc_ref[r, :] += part

    a_parts = []
    for dr in range(2):
        a = ab_ref[dr, 0]
        a_parts.append((jnp.broadcast_to(a[:, :sb], (rows, sb)), jnp.broadcast_to(a[:, sb:], (rows, sb))))

    def step_rows(stage, step):
        i = stage * SSM_SLICE + step
        return [slice(t * rows, (t + 1) * rows) for t in (i, seq - 1 - i)]

    def scan_slice(stage, carry):
        for step in range(SSM_SLICE):
            new = []
            for dr, sl in enumerate(step_rows(stage, step)):
                ar, ai = a_parts[dr]
                hr, hi = carry[2 * dr], carry[2 * dr + 1]
                b = hs_ref[dr, sl, :]
                nr = ar * hr - ai * hi + b[:, :sb]
                ni = ar * hi + ai * hr + b[:, sb:]
                hs_ref[dr, sl, :sb] = nr
                hs_ref[dr, sl, sb:] = ni
                new += [nr, ni]
            carry = tuple(new)
        return carry

    def fix_slice(stage, carry):
        for step in range(SSM_SLICE):
            new = []
            for dr, sl in enumerate(step_rows(stage, step)):
                ar, ai = a_parts[dr]
                zr, zi = carry[2 * dr], carry[2 * dr + 1]
                nr = ar * zr - ai * zi
                ni = ar * zi + ai * zr
                hs_ref[dr, sl, :sb] += nr
                hs_ref[dr, sl, sb:] += ni
                new += [nr, ni]
            carry = tuple(new)
        return carry

    zero = jnp.zeros((rows, sb), F32)
    fin = (zero, zero, zero, zero)
    input_proj(0)
    for stage in range(n_slices):
        if stage + 1 < n_slices:
            input_proj(stage + 1)
        fin = scan_slice(stage, fin)
        if not chunked and stage >= 1:
            output_proj(stage - 1)

    if want_final:
        for dr in range(2):
            fin_ref[dr, 0, :, :sb] = fin[2 * dr]
            fin_ref[dr, 0, :, sb:] = fin[2 * dr + 1]

    if chunked:
        entry = []
        for dr in range(2):
            ar, ai = a_parts[dr]
            pr, pi = ar[0:1], ai[0:1]
            for _ in range(int(math.log2(seq))):
                pr, pi = pr * pr - pi * pi, 2.0 * pr * pi
            fr, fi = fin[2 * dr], fin[2 * dr + 1]
            s0 = s0_ref[dr, 0, 0]
            er, ei = s0[:, :sb], s0[:, sb:]
            order = range(rows) if dr == 0 else range(rows - 1, -1, -1)
            rows_r, rows_i = [None] * rows, [None] * rows
            for c in order:
                rows_r[c], rows_i[c] = er, ei
                er, ei = (pr * er - pi * ei + fr[c:c + 1], pr * ei + pi * er + fi[c:c + 1])
            entry += [jnp.concatenate(rows_r, axis=0), jnp.concatenate(rows_i, axis=0)]
        carry = tuple(entry)
        for stage in range(n_slices):
            carry = fix_slice(stage, carry)
            if stage >= 1:
                output_proj(stage - 1)
    output_proj(n_slices - 1)

    u2 = u_ref[...].reshape(seq * rows, SSM_UBLK)
    y = _gelu_tanh(yacc_ref[...] + d_ref[...] * u2)
    y_ref[...] = y.reshape(seq, rows, SSM_UBLK)


def _ssm(u_t, l, w, s0, want_final):
    seq, r, _ = u_t.shape
    chunked = s0 is not None
    rows = V7X_SUBLANES
    in_specs = [pl.BlockSpec((seq, rows, SSM_UBLK), lambda i, j: (0, i, j)),
                pl.BlockSpec((None, 2, 1, SSM_UBLK, 2 * SSM_SBLK), lambda i, j: (l, 0, j, 0, 0)),
                pl.BlockSpec((None, 2, 1, 1, 2 * SSM_SBLK), lambda i, j: (l, 0, j, 0, 0)),
                pl.BlockSpec((None, 2, 1, 2 * SSM_SBLK, SSM_UBLK), lambda i, j: (l, 0, j, 0, 0)),
                pl.BlockSpec((None, 1, SSM_UBLK), lambda i, j: (l, 0, j))]
    args = [u_t, w["ssm_b"], w["ssm_a"], w["ssm_c"], w["ssm_d"]]
    if chunked:
        in_specs.append(pl.BlockSpec((None, 2, 1, 1, 1, 2 * SSM_SBLK), lambda i, j: (l, 0, i, j, 0, 0)))
        args.append(s0)
    out_specs = [pl.BlockSpec((seq, rows, SSM_UBLK), lambda i, j: (0, i, j))]
    out_shape = [jax.ShapeDtypeStruct((seq, r, SSM_WIDTH), F32)]
    if want_final:
        out_specs.append(pl.BlockSpec((2, 1, rows, 2 * SSM_SBLK), lambda i, j: (0, j, i, 0)))
        out_shape.append(jax.ShapeDtypeStruct((2, SSM_NBLK, r, 2 * SSM_SBLK), F32))
    return pl.pallas_call(
        functools.partial(_ssm_kernel, chunked, want_final),
        grid=(r // rows, SSM_NBLK),
        in_specs=in_specs, out_specs=out_specs, out_shape=out_shape,
        scratch_shapes=[pltpu.VMEM((2, seq * rows, 2 * SSM_SBLK), F32), pltpu.VMEM((seq * rows, SSM_UBLK), F32)],
        compiler_params=_params(2),
        name="ssm_latent" if chunked else "ssm_context",
    )(*args)


def _merge_kernel(last, x_ref, sh_ref, sc_ref, ga_ref, g_ref, oa_ref, ob_ref, oc_ref, y_ref,
                  wg_ref, wm_ref, wglu_ref, bglu_ref, wbr_ref, wout_ref, fn_ref, o_ref):
    x = x_ref[...]
    h = _modulated(x, g_ref, sh_ref, sc_ref).astype(BF16)
    y = y_ref[...]
    o_d = y * _sigmoid(_dot(y.astype(BF16), wglu_ref[...]) + bglu_ref[...])
    branches = (oa_ref[...].astype(F32), ob_ref[...].astype(F32), oc_ref[...].astype(F32), o_d)
    acc = None
    for n, o in enumerate(branches):
        gate = _dot(h, wg_ref[:, n * BRANCH_WIDTH:(n + 1) * BRANCH_WIDTH])
        br = _dot((o * (gate * _sigmoid(gate))).astype(BF16), wbr_ref[n])
        term = _sigmoid(_dot(h, wm_ref[:, n * D_MODEL:(n + 1) * D_MODEL])) * br
        acc = term if acc is None else acc + term
    xn = x + ga_ref[0] * _dot(acc.astype(BF16), wout_ref[...])
    if last:
        xn = _rms(xn) * fn_ref[...]
    o_ref[...] = xn


def _merge(x, shift, scale, gate, seq, oa, ob, oc, y, l, w, final_norm, last):
    n = x.shape[0]
    tm = MERGE_TM
    row = lambda i: (i, 0)
    mod_spec = _mod_spec(shift.shape[0], tm, seq)
    in_specs = [pl.BlockSpec((tm, D_MODEL), row), mod_spec, mod_spec, mod_spec, _layer_spec((1, D_MODEL), l),
                pl.BlockSpec((tm, 512), row), pl.BlockSpec((tm, 512), row), pl.BlockSpec((tm, 512), row),
                pl.BlockSpec((tm, 512), row),
                _layer_spec((D_MODEL, N_BRANCH * BRANCH_WIDTH), l), _layer_spec((D_MODEL, N_BRANCH * D_MODEL), l),
                _layer_spec((SSM_WIDTH, SSM_WIDTH), l), _layer_spec((1, SSM_WIDTH), l),
                _layer_spec((N_BRANCH, BRANCH_WIDTH, D_MODEL), l), _layer_spec((D_MODEL, D_MODEL), l),
                _const_spec((1, D_MODEL))]
    return pl.pallas_call(
        functools.partial(_merge_kernel, last),
        grid=(n // tm,),
        in_specs=in_specs,
        out_specs=pl.BlockSpec((tm, D_MODEL), row),
        out_shape=jax.ShapeDtypeStruct((n, D_MODEL), F32),
        compiler_params=_params(1),
        name="merge",
    )(x, shift, scale, gate, w["norm_g"], oa, ob, oc, y, w["wg"], w["wm"], w["wglu"], w["bglu"],
      w["wbr"], w["wout"], final_norm)


def _rope_tables(n_tok, rot_dim):
    rows = n_tok // GRID_W
    row = jnp.repeat(jnp.arange(rows, dtype=F32), GRID_W)
    col = jnp.tile(jnp.arange(GRID_W, dtype=F32), rows)
    quarter = rot_dim // 4
    inv = ROPE_THETA ** (-jnp.arange(quarter, dtype=F32) / quarter)
    ang = jnp.concatenate([row[:, None] * inv, col[:, None] * inv], axis=-1)
    return jnp.cos(ang), jnp.sin(ang)


def _rope_lane_tables(n_tok):
    c, s = _rope_tables(n_tok, MLA_ROPE)
    ones = jnp.ones((n_tok, MLA_NOPE), F32)
    pad = MLA_HEAD_PAD - MLA_NOPE - MLA_ROPE
    c_mla = jnp.concatenate([ones, c, c, jnp.ones((n_tok, pad), F32)], axis=1)
    s_mla = jnp.concatenate([0.0 * ones, -s, s, jnp.zeros((n_tok, pad), F32)], axis=1)
    c, s = _rope_tables(n_tok, DIFF_HEAD_DIM)
    c_64 = jnp.concatenate([c, c, c, c], axis=1)
    s_64 = jnp.concatenate([-s, s, -s, s], axis=1)
    return c_mla, s_mla, c_64, s_64


def _block_diag(blocks):
    g = blocks.shape[-3]
    eye = jnp.eye(g, dtype=blocks.dtype)
    out = jnp.einsum("...grc,gh->...grhc", blocks, eye)
    return out.reshape(blocks.shape[:-3] + (g * blocks.shape[-2], g * blocks.shape[-1]))


def _in_output_half(v):
    zeros = jnp.zeros_like(v[..., 0, :])
    return jnp.stack([jnp.concatenate([v[..., 0, :], zeros], axis=-1),
                      jnp.concatenate([zeros, v[..., 1, :]], axis=-1)], axis=-2)


def _weights(w_in, p):
    w_t = jnp.swapaxes(w_in, 1, 2).reshape(DEPTH * D_IN, D_MODEL)
    wmix = _transposed_rows(w_t, _IN_MIX_START, _M_WIDTH, _M_WIDTH // 2)
    wg = _transposed_rows(w_t, _IN_GATE_START, N_BRANCH * BRANCH_WIDTH, 1024)
    wm = _transposed_rows(w_t, _IN_MERGE_START, N_BRANCH * D_MODEL, 1024)
    zeros = lambda wd: jnp.zeros((DEPTH, D_MODEL, wd), F32)
    kpe_lo = MLA_Q_LORA + MLA_KV_LORA
    whead = jnp.concatenate([w_in[:, :, :kpe_lo], zeros(MLA_NOPE), w_in[:, :, kpe_lo:_IN_MIX_START],
                             zeros(MLA_HEAD_PAD - MLA_NOPE - MLA_ROPE)], axis=2).astype(BF16)

    hq = MLA_NOPE + MLA_ROPE
    wqb = p["w_mla_q_b"].reshape(DEPTH, MLA_Q_LORA, MLA_HEADS, hq)
    wqb = jnp.pad(wqb, ((0, 0), (0, 0), (0, 0), (0, MLA_HEAD_PAD - hq))).reshape(DEPTH, MLA_Q_LORA, -1).astype(BF16)
    wkv = p["w_mla_kv_b"].reshape(DEPTH, MLA_KV_LORA, MLA_HEADS, MLA_NOPE + MLA_V)
    wk = jnp.pad(wkv[..., :MLA_NOPE], ((0, 0), (0, 0), (0, 0), (0, MLA_HEAD_PAD - MLA_NOPE)))
    wk = wk.reshape(DEPTH, MLA_KV_LORA, -1).astype(BF16)
    wv = _in_output_half(wkv[..., MLA_NOPE:].reshape(DEPTH, MLA_KV_LORA, MLA_HEADS // 2, 2, MLA_V))
    wv = wv.reshape(DEPTH, MLA_KV_LORA, -1).astype(BF16)
    eye = jnp.eye(GQA_HEAD_DIM, dtype=F32)
    egv = _block_diag(jnp.broadcast_to(
        _in_output_half(jnp.stack([eye, eye], axis=1)).reshape(GQA_HEAD_DIM, 2 * V_PAD),
        (GQA_KV_HEADS, GQA_HEAD_DIM, 2 * V_PAD))).astype(BF16)

    gmat =_block_diag(jnp.full((GQA_HEADS, GQA_HEAD_DIM, GQA_HEAD_DIM), 1.0 / GQA_HEAD_DIM, F32)).astype(BF16)

    lam = lax.complex(p["ssm_a_re"], p["ssm_a_im"])
    dt = jnp.exp(p["ssm_log_dt"])[..., None]
    abar = jnp.exp(lam * dt)
    bbar = ((abar - 1.0) / lam)[..., None] * lax.complex(p["ssm_b_re"], p["ssm_b_im"])
    blk = lambda a: a.reshape((DEPTH, 2, SSM_NBLK, SSM_GBLOCK) + a.shape[3:])
    b_t = jnp.swapaxes(blk(bbar), -1, -2)
    ssm_b = jnp.concatenate([_block_diag(jnp.real(b_t)), _block_diag(jnp.imag(b_t))], axis=-1).astype(BF16)
    ab = blk(abar).reshape(DEPTH, 2, SSM_NBLK, 1, SSM_SBLK)
    ssm_a = jnp.concatenate([jnp.real(ab), jnp.imag(ab)], axis=-1)
    c_re = jnp.swapaxes(blk(p["ssm_c_re"]), -1, -2)
    c_im = jnp.swapaxes(blk(p["ssm_c_im"]), -1, -2)
    ssm_c = jnp.concatenate([_block_diag(c_re), -_block_diag(c_im)], axis=-2).astype(BF16)

    row = lambda a: a[:, None, :]
    return dict(
        norm_g=row(p["norm_g"]), whead=whead, wmix=wmix, qn=row(p["mla_q_norm"]), wqb=wqb,
        kvn=row(p["mla_kv_norm"]), wk=wk, wv=wv,
        gqn=row(jnp.tile(p["gqa_q_norm"], (1, GQA_HEADS))), gkn=row(jnp.tile(p["gqa_k_norm"], (1, GQA_KV_HEADS))),
        gmat=gmat, egv=egv,
        lam_parts=jnp.stack([p["diff_lq1"], p["diff_lk1"], p["diff_lq2"], p["diff_lk2"]], axis=1),
        subln=row(p["diff_subln"]),
        ssm_b=ssm_b, ssm_a=ssm_a, ssm_c=ssm_c, ssm_d=row(p["ssm_d"]),
        wg=wg, wm=wm, wglu=p["ssm_glu_w"].astype(BF16), bglu=row(p["ssm_glu_b"]),
        wbr=p["w_branch_out"].astype(BF16), wout=p["w_out"].astype(BF16),
    )


def _mixers(pr, b, t, l, w, lam_init, s0):
    q, k, v, dq, dk, dv, gq, gk, gv, u = pr[:10]
    three = lambda a: a.reshape(b, a.shape[0] // b, a.shape[-1])
    k, v, dk, dv, gk, gv = (three(a) for a in (k, v, dk, dv, gk, gv))
    mla_heads = tuple((hd, hd) for hd in range(MLA_HEADS))
    gqa_heads = tuple((hd // GQA_GROUP, 2 * (hd // GQA_GROUP) + hd % 2) for hd in range(GQA_HEADS))
    o_a = _attention(functools.partial(_attn_heads_kernel, mla_heads, MLA_HEAD_PAD), "attn_mla",
                     three(q), k, v)
    o_b = _attention(functools.partial(_diff_attn_kernel, lam_init), "attn_diff", three(dq), dk, dv,
                     extra_specs=(_layer_spec((4, DIFF_HEAD_DIM), l), _layer_spec((1, DIFF_V), l)),
                     extra=(w["lam_parts"], w["subln"]))
    o_c = _attention(functools.partial(_attn_heads_kernel, gqa_heads, GQA_HEAD_DIM), "attn_gqa",
                     three(gq), gk, gv)
    n = b * t
    if s0 is None:
        u_t = jnp.swapaxes(u.reshape(b, t, SSM_WIDTH), 0, 1)
        y_t, fin = _ssm(u_t, l, w, None, True)
        y = jnp.swapaxes(y_t, 0, 1).reshape(n, SSM_WIDTH)
    else:
        chunks = t // SSM_SEQ
        u_t = jnp.transpose(u.reshape(b, chunks, SSM_SEQ, SSM_WIDTH), (2, 0, 1, 3)).reshape(SSM_SEQ, b * chunks, SSM_WIDTH)
        (y_t,) = _ssm(u_t, l, w, s0, False)
        y = jnp.transpose(y_t.reshape(SSM_SEQ, b, chunks, SSM_WIDTH), (1, 2, 0, 3)).reshape(n, SSM_WIDTH)
        fin = None
    flat = lambda a: a.reshape(n, 512)
    return flat(o_a), flat(o_b), flat(o_c), y, fin


def kernel(x_prompt, x_sample, cache_mla_ckv, cache_mla_krope, cache_diff_k, cache_diff_v, cache_gqa_k, cache_gqa_v, state_ssm, c, c_ctx, norm_g, w_mod, b_mod, w_in, mla_q_norm, w_mla_q_b, mla_kv_norm, w_mla_kv_b, diff_lq1, diff_lk1, diff_lq2, diff_lk2, diff_subln, gqa_q_norm, gqa_k_norm, ssm_a_re, ssm_a_im, ssm_log_dt, ssm_b_re, ssm_b_im, ssm_c_re, ssm_c_im, ssm_d, ssm_glu_w, ssm_glu_b, w_branch_out, w_out, final_norm):
    p = dict(norm_g=norm_g, mla_q_norm=mla_q_norm, w_mla_q_b=w_mla_q_b, mla_kv_norm=mla_kv_norm,
             w_mla_kv_b=w_mla_kv_b, diff_lq1=diff_lq1, diff_lk1=diff_lk1, diff_lq2=diff_lq2, diff_lk2=diff_lk2,
             diff_subln=diff_subln, gqa_q_norm=gqa_q_norm, gqa_k_norm=gqa_k_norm, ssm_a_re=ssm_a_re,
             ssm_a_im=ssm_a_im, ssm_log_dt=ssm_log_dt, ssm_b_re=ssm_b_re, ssm_b_im=ssm_b_im, ssm_c_re=ssm_c_re,
             ssm_c_im=ssm_c_im, ssm_d=ssm_d, ssm_glu_w=ssm_glu_w, ssm_glu_b=ssm_glu_b,
             w_branch_out=w_branch_out, w_out=w_out)
    bp, tp, _ = x_prompt.shape
    bs, ts, _ = x_sample.shape
    past = cache_mla_ckv.shape[2]
    assert tp == SSM_SEQ and ts % SSM_SEQ == 0 and ts // SSM_SEQ == V7X_SUBLANES
    rope = _rope_lane_tables(ts)
    w = _weights(w_in, p)

    cvec = jnp.zeros((V7X_SUBLANES, D_MODEL), F32).at[0].set(c_ctx).at[1:1 + bs].set(c)
    mod = _modulation(cvec, w_mod, b_mod)

    pad_lo = MLA_NOPE
    pad_hi = MLA_HEAD_PAD - MLA_NOPE - MLA_ROPE
    kpe_ctx = jnp.pad(cache_mla_krope, ((0, 0), (0, 0), (0, 0), (pad_lo, pad_hi)))
    flat_ctx = lambda a: a.reshape(bs, DEPTH, past, -1).astype(BF16)
    gv_both = _in_output_half(jnp.stack([cache_gqa_v, cache_gqa_v], axis=-2))
    dk_ctx, dv_ctx, gk_ctx, gv_ctx = (flat_ctx(a) for a in (cache_diff_k, cache_diff_v, cache_gqa_k, gv_both))
    s0_all = state_ssm.reshape(bs, DEPTH, 2, SSM_NBLK, SSM_GBLOCK * SSM_STATE, 2)
    s0_all = jnp.transpose(s0_all, (1, 2, 0, 3, 5, 4)).reshape(DEPTH, 2, bs, SSM_NBLK, 1, 2 * SSM_SBLK)

    xp = x_prompt.reshape(bp * tp, D_MODEL)
    xs = x_sample.reshape(bs * ts, D_MODEL)
    fn = final_norm[None]
    sides = ()
    states = []
    for l in range(DEPTH):
        lam_init = 0.8 - 0.6 * math.exp(-0.3 * l)
        last = l == DEPTH - 1
        sh, sc, ga = (mod[l, :, i * D_MODEL:(i + 1) * D_MODEL][:, None, :] for i in range(3))

        pr = _proj(xp, sh[0:1], sc[0:1], tp, l, w, prev_sides=sides)
        o_a, o_b, o_c, y, fin = _mixers(pr, bp, tp, l, w, lam_init, None)
        xp = _merge(xp, sh[0:1], sc[0:1], ga[0:1], tp, o_a, o_b, o_c, y, l, w, fn, last)
        sides = tuple(pr[10:])
        fin = fin.reshape(2, SSM_NBLK, bp, 2, SSM_GBLOCK, SSM_STATE)
        states.append(jnp.transpose(fin, (2, 0, 1, 4, 5, 3)).reshape(bp, 2, SSM_GROUPS, SSM_STATE, 2))

        pr = _proj(xs, sh[1:1 + bs], sc[1:1 + bs], ts, l, w, rope=rope,
                   cached=(cache_mla_ckv, kpe_ctx, dk_ctx, dv_ctx, gk_ctx, gv_ctx))
        o_a, o_b, o_c, y, _ = _mixers(pr, bs, ts, l, w, lam_init, s0_all)
        xs = _merge(xs, sh[1:1 + bs], sc[1:1 + bs], ga[1:1 + bs], ts, o_a, o_b, o_c, y, l, w, fn, last)

    ckv_s, kpe_t, dk_t, dv_s, gk_t, gv_t = sides
    untranspose = lambda a: jnp.swapaxes(a, 2, 3)
    return (xp.reshape(bp, tp, D_MODEL), xs.reshape(bs, ts, D_MODEL),
            ckv_s, untranspose(kpe_t),
            untranspose(dk_t).reshape(bp, DEPTH, tp, DIFF_HEADS, 2, DIFF_HEAD_DIM), dv_s,
            untranspose(gk_t).reshape(bp, DEPTH, tp, GQA_KV_HEADS, GQA_HEAD_DIM),
            untranspose(gv_t).reshape(bp, DEPTH, tp, GQA_KV_HEADS, GQA_HEAD_DIM),
            jnp.stack(states, axis=1))
```

```python
import functools
import math

import jax
import jax.numpy as jnp
from jax import lax
from jax.experimental import pallas as pl
from jax.experimental.pallas import tpu as pltpu

F32 = jnp.float32
BF16 = jnp.bfloat16

D_MODEL = 1024
DEPTH = 2
GRID_W = 64
ROPE_THETA = 10000.0
EPS = 1e-6

MLA_HEADS = 8
MLA_NOPE = 64
MLA_ROPE = 32
MLA_V = 64
MLA_Q_LORA = 256
MLA_KV_LORA = 128
MLA_SCALE = (MLA_NOPE + MLA_ROPE) ** -0.5
MLA_HEAD_PAD = 128
DIFF_HEADS = 4
DIFF_HEAD_DIM = 64
DIFF_V = 2 * DIFF_HEAD_DIM
DIFF_SCALE = DIFF_HEAD_DIM ** -0.5
GQA_HEADS = 8
GQA_KV_HEADS = 2
GQA_HEAD_DIM = 64
GQA_GROUP = GQA_HEADS // GQA_KV_HEADS
GQA_KV_WIDTH = GQA_KV_HEADS * GQA_HEAD_DIM
GQA_SCALE = GQA_HEAD_DIM ** -0.5
SSM_WIDTH = 512
SSM_GROUP = 16
SSM_GROUPS = SSM_WIDTH // SSM_GROUP
SSM_STATE = 64
N_BRANCH = 4
BRANCH_WIDTH = 512

IN_SPLITS = (MLA_Q_LORA, MLA_KV_LORA, MLA_ROPE, 512, 512, 512, 512, 128, 128, SSM_WIDTH,
             N_BRANCH * BRANCH_WIDTH, N_BRANCH * D_MODEL)
D_IN = sum(IN_SPLITS)

V7X_LANES = 128
V7X_SUBLANES = 8
V7X_VMEM_LIMIT_BYTES = 56 * 1024 * 1024

_H_QA = (0, 256)
_H_KVA = (256, 384)
_H_KPE = (384, 512)
_H_WIDTH = 512
_M_DQ = (0, 512)
_M_DK = (512, 1024)
_M_DV = (1024, 1536)
_M_GQ = (1536, 2048)
_M_GK = (2048, 2176)
_M_GV = (2176, 2304)
_M_U = (2304, 2816)
_M_WIDTH = 2816
_IN_MIX_START = MLA_Q_LORA + MLA_KV_LORA + MLA_ROPE
_IN_GATE_START = _IN_MIX_START + _M_WIDTH
_IN_MERGE_START = _IN_GATE_START + N_BRANCH * BRANCH_WIDTH

SSM_GBLOCK = 8
SSM_NBLK = SSM_GROUPS // SSM_GBLOCK
SSM_SBLK = SSM_GBLOCK * SSM_STATE
SSM_UBLK = SSM_GBLOCK * SSM_GROUP
SSM_SEQ = 256
SSM_SLICE = 32

V_PAD = 128

PROJ_TM = 512
MERGE_TM = 512
ATTN_TQ = 256
ATTN_TQ_LONG = 512
ATTN_SEQ_PER_STEP = 4
ATTN_GROUP_SHORT = 32
ATTN_GROUP_LONG = 2


def _params(n_axes, vmem=V7X_VMEM_LIMIT_BYTES):
    return pltpu.CompilerParams(dimension_semantics=("arbitrary",) * n_axes, vmem_limit_bytes=vmem)


def _const_spec(shape):
    zeros = (0,) * len(shape)
    return pl.BlockSpec(shape, lambda *_: zeros, pipeline_mode=pl.Buffered(1))


def _layer_spec(shape, l):
    zeros = (0,) * len(shape)
    return pl.BlockSpec((None,) + tuple(shape), lambda *_: (l,) + zeros, pipeline_mode=pl.Buffered(1))


def _mod_spec(n_rows, tm, seq):
    if n_rows == 1:
        return pl.BlockSpec((1, 1, D_MODEL), lambda i: (0, 0, 0))
    return pl.BlockSpec((1, 1, D_MODEL), lambda i: (i * tm // seq, 0, 0))


def _dot(a, b):
    return jnp.dot(a, b, preferred_element_type=F32)


def _dot_nt(a, b):
    return lax.dot_general(a, b, (((1,), (1,)), ((), ())), preferred_element_type=F32)


def _rms(x):
    return x * lax.rsqrt(jnp.mean(x * x, axis=-1, keepdims=True) + EPS)


def _sigmoid(x):
    return 1.0 / (1.0 + jnp.exp(-x))


def _gelu_tanh(x):
    return 0.5 * x * (1.0 + jnp.tanh(math.sqrt(2.0 / math.pi) * (x + 0.044715 * (x * x * x))))


def _group_mean_sq(x, gmat):
    return _dot((x * x).astype(BF16), gmat)


def _rope_chunk(x, cos, sin, half):
    lane = lax.broadcasted_iota(jnp.int32, x.shape, 1)
    first = (lane % (2 * half)) < half
    swapped = jnp.where(first, pltpu.roll(x, V7X_LANES - half, 1), pltpu.roll(x, half, 1))
    return x * cos + swapped * sin


def _transpose_cast_kernel(w_ref, o_ref):
    o_ref[...] = w_ref[...].T.astype(BF16)


def _transposed_rows(w_t, row0, n_rows, blk):
    return pl.pallas_call(
        _transpose_cast_kernel,
        grid=(DEPTH, n_rows // blk),
        in_specs=[pl.BlockSpec((pl.Element(blk), pl.Element(D_MODEL)),
                               lambda l, j: (pl.multiple_of(l * D_IN + row0 + j * blk, V7X_SUBLANES), 0))],
        out_specs=pl.BlockSpec((None, D_MODEL, blk), lambda l, j: (l, 0, j)),
        out_shape=jax.ShapeDtypeStruct((DEPTH, D_MODEL, n_rows), BF16),
        compiler_params=_params(2),
        name="w_in_columns",
    )(w_t)


def _mod_kernel(c_ref, w_ref, b_ref, o_ref):
    c = c_ref[...]
    a = (c * _sigmoid(c)).astype(BF16)
    o_ref[0] = _dot(a, w_ref[0].astype(BF16)) + b_ref[0]


def _modulation(cvec, w_mod, b_mod):
    tn = 512
    return pl.pallas_call(
        _mod_kernel,
        grid=(DEPTH, 3 * D_MODEL // tn),
        in_specs=[pl.BlockSpec((V7X_SUBLANES, D_MODEL), lambda l, j: (0, 0)),
                  pl.BlockSpec((1, D_MODEL, tn), lambda l, j: (l, 0, j)),
                  pl.BlockSpec((1, 1, tn), lambda l, j: (l, 0, j))],
        out_specs=pl.BlockSpec((1, V7X_SUBLANES, tn), lambda l, j: (l, 0, j)),
        out_shape=jax.ShapeDtypeStruct((DEPTH, V7X_SUBLANES, 3 * D_MODEL), F32),
        compiler_params=_params(2),
        name="modulation",
    )(cvec, w_mod, b_mod.reshape(DEPTH, 1, 3 * D_MODEL))


def _modulated(x, g_ref, sh_ref, sc_ref):
    return (_rms(x) * g_ref[...]) * (1.0 + sc_ref[0]) + sh_ref[0]


def _proj_kernel(latent, n_prev, first_layer, *refs):
    (x_ref, sh_ref, sc_ref, g_ref, wh_ref, wx_ref, qn_ref, wqb_ref, kvn_ref, wk_ref, wv_ref,
     gqn_ref, gkn_ref, gm_ref, egv_ref) = refs[:15]
    pos = 15
    if latent:
        c_mla, s_mla, c_64, s_64 = refs[pos:pos + 4]
        cached = refs[pos + 4:pos + 10]
        pos += 10
    pos += n_prev
    (q_o, k_o, v_o, dq_o, dk_o, dv_o, gq_o, gk_o, gv_o, u_o) = refs[pos:pos + 10]
    pos += 10
    if not latent:
        ckv_f, kpe_t, dk_t, dv_f, gk_t, gv_t = refs[pos:pos + 6]

    def mla_keys_values(ckv_b, kpe):
        kn = _dot(ckv_b, wk_ref[...])
        for hd in range(MLA_HEADS):
            sl = slice(hd * MLA_HEAD_PAD, (hd + 1) * MLA_HEAD_PAD)
            k_o[:, sl] = (kn[:, sl] + kpe).astype(BF16)
        v_o[...] = _dot(ckv_b, wv_ref[...]).astype(BF16)

    def cached_tokens():
        ckv_c, kpe_c, dk_c, dv_c, gk_c, gv_c = cached
        mla_keys_values(ckv_c[...].astype(BF16), kpe_c[...])
        dk_o[...] = dk_c[...]
        dv_o[...] = dv_c[...]
        gk_o[...] = gk_c[...]
        gv_o[...] = gv_c[...]

    def new_tokens():
        _proj_new_tokens(latent, first_layer, refs[:15], (c_mla, s_mla, c_64, s_64) if latent else None,
                         refs[pos - 10:pos], refs[pos:pos + 6] if not latent else None, mla_keys_values)

    if latent:
        pl.when(pl.program_id(1) == 0)(cached_tokens)
        pl.when(pl.program_id(1) > 0)(new_tokens)
    else:
        new_tokens()


def _proj_new_tokens(latent, first_layer, in_refs, rope_refs, out_refs, side_refs, mla_keys_values):
    (x_ref, sh_ref, sc_ref, g_ref, wh_ref, wx_ref, qn_ref, wqb_ref, kvn_ref, wk_ref, wv_ref,
     gqn_ref, gkn_ref, gm_ref, egv_ref) = in_refs
    (q_o, k_o, v_o, dq_o, dk_o, dv_o, gq_o, gk_o, gv_o, u_o) = out_refs
    if latent:
        c_mla, s_mla, c_64, s_64 = rope_refs
    else:
        ckv_f, kpe_t, dk_t, dv_f, gk_t, gv_t = side_refs

    h = _modulated(x_ref[...], g_ref, sh_ref, sc_ref).astype(BF16)
    tm = h.shape[0]

    p_head = _dot(h, wh_ref[...])
    p_mix = _dot(h, wx_ref[...])

    def head(cols):
        return p_head[:, cols[0]:cols[1]]

    def mix(cols):
        return p_mix[:, cols[0]:cols[1]]

    def rope_mla(x):
        return _rope_chunk(x, c_mla[...], s_mla[...], MLA_ROPE // 2) if latent else x

    def rope_64(x):
        if not latent:
            return x
        c, s = c_64[...], s_64[...]
        chunks = [_rope_chunk(x[:, i:i + V7X_LANES], c, s, DIFF_HEAD_DIM // 2)
                  for i in range(0, x.shape[1], V7X_LANES)]
        return chunks[0] if len(chunks) == 1 else jnp.concatenate(chunks, axis=1)

    def per_sequence(x):
        return [x[b * SSM_SEQ:(b + 1) * SSM_SEQ] for b in range(tm // SSM_SEQ)]

    def slot(b):
        return (b,) if first_layer is None else (b, first_layer)

    def zero_other_slots(ref):
        if first_layer is not None:
            for b in range(ref.shape[0]):
                for other in range(ref.shape[1]):
                    if other != first_layer:
                        ref[b, other] = jnp.zeros(ref.shape[2:], ref.dtype)

    qn = (_rms(head(_H_QA)) * qn_ref[...]).astype(BF16)
    q = _dot(qn, wqb_ref[...])
    for hd in range(MLA_HEADS):
        sl = slice(hd * MLA_HEAD_PAD, (hd + 1) * MLA_HEAD_PAD)
        q_o[:, sl] = (rope_mla(q[:, sl]) * MLA_SCALE).astype(BF16)
    ckv = _rms(head(_H_KVA)) * kvn_ref[...]
    kpe = head(_H_KPE)
    if not latent:
        for b, (c_b, k_b) in enumerate(zip(per_sequence(ckv), per_sequence(kpe))):
            ckv_f[slot(b)] = c_b
            kpe_t[slot(b)] = k_b.T[MLA_NOPE:MLA_NOPE + MLA_ROPE, :]
        zero_other_slots(ckv_f)
        zero_other_slots(kpe_t)
    mla_keys_values(ckv.astype(BF16), rope_mla(kpe))

    dq_o[...] = (rope_64(mix(_M_DQ)) * DIFF_SCALE).astype(BF16)
    dk = rope_64(mix(_M_DK))
    dv = mix(_M_DV)
    dk_o[...] = dk.astype(BF16)
    dv_o[...] = dv.astype(BF16)

    gq = mix(_M_GQ)
    gq = gq * lax.rsqrt(_group_mean_sq(gq, gm_ref[...]) + EPS) * gqn_ref[...]
    gq_o[...] = (rope_64(gq) * GQA_SCALE).astype(BF16)
    gk = mix(_M_GK)
    kw = GQA_KV_WIDTH
    gk = gk * lax.rsqrt(_group_mean_sq(gk, gm_ref[:kw, :kw]) + EPS) * gkn_ref[...]
    gv = mix(_M_GV)
    if not latent:
        for b, (dk_b, dv_b, gk_b, gv_b) in enumerate(zip(*(per_sequence(a) for a in (dk, dv, gk, gv)))):
            dk_t[slot(b)] = dk_b.T
            for hd in range(DIFF_HEADS):
                dv_f[slot(b) + (slice(None), hd, slice(None))] = dv_b[:, hd * DIFF_V:(hd + 1) * DIFF_V]
            gk_t[slot(b)] = gk_b.T
            gv_t[slot(b)] = gv_b.T
        for ref in (dk_t, dv_f, gk_t, gv_t):
            zero_other_slots(ref)
    gk_o[...] = rope_64(gk).astype(BF16)
    gv_o[...] = _dot(gv.astype(BF16), egv_ref[...]).astype(BF16)

    u_o[...] = mix(_M_U)


def _side_shapes(n_seq):
    return [((n_seq, DEPTH, SSM_SEQ, MLA_KV_LORA), (SSM_SEQ, MLA_KV_LORA)),
            ((n_seq, DEPTH, MLA_ROPE, SSM_SEQ), (MLA_ROPE, SSM_SEQ)),
            ((n_seq, DEPTH, 512, SSM_SEQ), (512, SSM_SEQ)),
            ((n_seq, DEPTH, SSM_SEQ, DIFF_HEADS, DIFF_V), (SSM_SEQ, DIFF_HEADS, DIFF_V)),
            ((n_seq, DEPTH, GQA_KV_WIDTH, SSM_SEQ), (GQA_KV_WIDTH, SSM_SEQ)),
            ((n_seq, DEPTH, GQA_KV_WIDTH, SSM_SEQ), (GQA_KV_WIDTH, SSM_SEQ))]


def _proj(x, shift, scale, seq, l, w, rope=None, cached=None, prev_sides=()):
    n = x.shape[0]
    tm = PROJ_TM
    latent = cached is not None
    kw = GQA_KV_WIDTH
    if latent:
        tiles = seq // tm
        assert cached[0].shape[2] == tm
        grid = (n // seq, 1 + tiles)
        row = lambda b, j: (b * tiles + jnp.maximum(j - 1, 0), 0)
        kv_row = lambda b, j: (b * (1 + tiles) + j, 0)
        n_kv = (n // seq) * (cached[0].shape[2] + seq)
        mod_spec = pl.BlockSpec((1, 1, D_MODEL), lambda b, j: (b, 0, 0))
    else:
        grid = (n // tm,)
        row = kv_row = lambda i: (i, 0)
        n_kv = n
        mod_spec = _mod_spec(shift.shape[0], tm, seq)
    in_specs = [pl.BlockSpec((tm, D_MODEL), row), mod_spec, mod_spec,
                _layer_spec((1, D_MODEL), l), _layer_spec((D_MODEL, _H_WIDTH), l), _layer_spec((D_MODEL, _M_WIDTH), l),
                _layer_spec((1, MLA_Q_LORA), l), _layer_spec((MLA_Q_LORA, MLA_HEADS * MLA_HEAD_PAD), l),
                _layer_spec((1, MLA_KV_LORA), l), _layer_spec((MLA_KV_LORA, MLA_HEADS * MLA_HEAD_PAD), l),
                _layer_spec((MLA_KV_LORA, MLA_HEADS * V_PAD), l),
                _layer_spec((1, 512), l), _layer_spec((1, kw), l), _const_spec((512, 512)),
                _const_spec((kw, 2 * GQA_KV_HEADS * V_PAD))]
    args = [x, shift, scale, w["norm_g"], w["whead"], w["wmix"], w["qn"], w["wqb"], w["kvn"], w["wk"], w["wv"],
            w["gqn"], w["gkn"], w["gmat"], w["egv"]]
    if latent:
        tab = pl.BlockSpec((tm, V7X_LANES), lambda b, j: (jnp.maximum(j - 1, 0), 0))
        in_specs += [tab] * 4
        args += list(rope)
        in_specs += [pl.BlockSpec((None, None) + c.shape[2:], lambda b, j: (b, l, 0, 0)) for c in cached]
        args += list(cached)
    n_fixed = len(args)
    in_specs += [pl.BlockSpec(memory_space=pl.ANY)] * len(prev_sides)
    args += list(prev_sides)
    widths = [(1024, BF16, False), (1024, BF16, True), (MLA_HEADS * V_PAD, BF16, True), (512, BF16, False),
              (512, BF16, True), (512, BF16, True), (512, BF16, False), (kw, BF16, True),
              (2 * GQA_KV_HEADS * V_PAD, BF16, True), (512, F32, False)]
    out_specs = [pl.BlockSpec((tm, wd), kv_row if kv else row) for wd, _, kv in widths]
    out_shape = [jax.ShapeDtypeStruct((n_kv if kv else n, wd), dt) for wd, dt, kv in widths]
    aliases = {}
    first_layer = None
    if not latent:
        per_tile = tm // SSM_SEQ
        first_layer = None if prev_sides else l
        for full, blk in _side_shapes(n // SSM_SEQ):
            zeros = (0,) * len(blk)
            if prev_sides:
                out_specs.append(pl.BlockSpec((per_tile, None) + blk, lambda i, zeros=zeros: (i, l) + zeros))
            else:
                out_specs.append(pl.BlockSpec((per_tile, DEPTH) + blk, lambda i, zeros=zeros: (i, 0) + zeros))
            out_shape.append(jax.ShapeDtypeStruct(full, F32))
        aliases = {n_fixed + k: len(widths) + k for k in range(len(prev_sides))}
    return pl.pallas_call(
        functools.partial(_proj_kernel, latent, len(prev_sides), first_layer),
        grid=grid,
        in_specs=in_specs, out_specs=out_specs, out_shape=out_shape,
        input_output_aliases=aliases,
        compiler_params=_params(len(grid)),
        name="proj_latent" if latent else "proj_context",
    )(*args)


def _softmax_maps(maps):
    scores = [_dot_nt(q, k) for q, k, _ in maps]
    tops = [jnp.max(s, axis=-1, keepdims=True) for s in scores]
    weights = [jnp.exp(s - m).astype(BF16) for s, m in zip(scores, tops)]
    outs = []
    for p, (_, _, v) in zip(weights, maps):
        ones = jnp.ones((v.shape[0], V7X_LANES), BF16)
        o = _dot(p, jnp.concatenate([v, ones], axis=1))
        outs.append((o[:, :V7X_LANES], o[:, V7X_LANES:]))
    return outs


def _map_group(n_keys):
    return ATTN_GROUP_SHORT if n_keys <= ATTN_TQ else ATTN_GROUP_LONG


def _attn_heads_kernel(heads, q_width, q_ref, k_ref, v_ref, o_ref):
    items = [(b, hd) for b in range(q_ref.shape[0]) for hd in range(len(heads))]
    group = _map_group(k_ref.shape[1])
    for g0 in range(0, len(items), group):
        part = items[g0:g0 + group]
        maps = [(q_ref[b, :, hd * q_width:(hd + 1) * q_width],
                 k_ref[b, :, heads[hd][0] * q_width:(heads[hd][0] + 1) * q_width],
                 v_ref[b, :, heads[hd][1] * V_PAD:(heads[hd][1] + 1) * V_PAD]) for b, hd in part]
        terms = [o * (1.0 / total) for o, total in _softmax_maps(maps)]
        for (b, hd), low, high in zip(part[0::2], terms[0::2], terms[1::2]):
            o_ref[b, :, (hd // 2) * V7X_LANES:(hd // 2 + 1) * V7X_LANES] = (low + high).astype(BF16)


def _diff_attn_kernel(lam_init, q_ref, k_ref, v_ref, lp_ref, sub_ref, o_ref):
    lp = lp_ref[...]
    lam = (jnp.exp(jnp.sum(lp[0:1] * lp[1:2], axis=-1, keepdims=True))
           - jnp.exp(jnp.sum(lp[2:3] * lp[3:4], axis=-1, keepdims=True)) + lam_init)
    d = DIFF_HEAD_DIM
    items = [(b, hd) for b in range(q_ref.shape[0]) for hd in range(DIFF_HEADS)]
    group = _map_group(k_ref.shape[1]) // 2
    for g0 in range(0, len(items), group):
        part = items[g0:g0 + group]
        maps = [(q_ref[b, :, c * d:(c + 1) * d], k_ref[b, :, c * d:(c + 1) * d],
                 v_ref[b, :, hd * DIFF_V:(hd + 1) * DIFF_V])
                for b, hd in part for c in (2 * hd, 2 * hd + 1)]
        outs = _softmax_maps(maps)
        mixed = [o1 * (1.0 / t1) - o2 * (lam / t2) for (o1, t1), (o2, t2) in zip(outs[0::2], outs[1::2])]
        normed = [_rms(o) * sub_ref[...] * (1.0 - lam_init) for o in mixed]
        for (b, hd), o in zip(part, normed):
            o_ref[b, :, hd * DIFF_V:(hd + 1) * DIFF_V] = o.astype(BF16)


def _attention(kernel, name, q, k, v, extra_specs=(), extra=()):
    b, t, wq = q.shape
    s = k.shape[1]
    tq, nb = (ATTN_TQ, ATTN_SEQ_PER_STEP) if t == ATTN_TQ else (ATTN_TQ_LONG, 1)
    in_specs = [pl.BlockSpec((nb, tq, wq), lambda i, j: (i, j, 0)),
                pl.BlockSpec((nb, s, k.shape[2]), lambda i, j: (i, 0, 0)),
                pl.BlockSpec((nb, s, v.shape[2]), lambda i, j: (i, 0, 0))]
    in_specs += list(extra_specs)
    return pl.pallas_call(
        kernel,
        grid=(b // nb, t // tq),
        in_specs=in_specs,
        out_specs=pl.BlockSpec((nb, tq, 512), lambda i, j: (i, j, 0)),
        out_shape=jax.ShapeDtypeStruct((b, t, 512), BF16),
        compiler_params=_params(2),
        name=name,
    )(q, k, v, *extra)


def _ssm_kernel(chunked, want_final, *refs):
    u_ref, bm_ref, ab_ref, cm_ref, d_ref = refs[:5]
    pos = 5
    if chunked:
        s0_ref = refs[pos]
        pos += 1
    y_ref = refs[pos]
    pos += 1
    if want_final:
        fin_ref = refs[pos]
        pos += 1
    hs_ref, yacc_ref, ut_ref = refs[pos:pos + 3]

    seq, rows, sb = SSM_SEQ, V7X_SUBLANES, SSM_SBLK
    n_slices = seq // SSM_SLICE
    slice_rows = SSM_SLICE * rows

    def visited(dr, stage):
        return stage if dr == 0 else n_slices - 1 - stage

    def rows_of(k):
        return slice(k * slice_rows, (k + 1) * slice_rows)

    def u_rows(k, first_visit):
        if first_visit:
            steps = range(k * SSM_SLICE, (k + 1) * SSM_SLICE)
            ut_ref[rows_of(k), :] = jnp.concatenate([u_ref[:, t, :] for t in steps], axis=0)
        return ut_ref[rows_of(k), :]

    def input_proj(stage):
        for dr in range(2):
            k = visited(dr, stage)
            hs_ref[dr, rows_of(k), :] = _dot(u_rows(k, stage < n_slices // 2).astype(BF16), bm_ref[dr, 0])

    def output_proj(stage):
        for dr in range(2):
            r = rows_of(visited(dr, stage))
            part = _dot(hs_ref[dr, r, :].astype(BF16), cm_ref[dr, 0])
            if stage < n_slices // 2:
                yacc_ref[r, :] = part
            else:
                yacc_ref[r, :] += part

    a_parts = []
    for dr in range(2):
        a = ab_ref[dr, 0]
        a_parts.append((jnp.broadcast_to(a[:, :sb], (rows, sb)), jnp.broadcast_to(a[:, sb:], (rows, sb))))

    def step_rows(stage, step):
        i = stage * SSM_SLICE + step
        return [slice(t * rows, (t + 1) * rows) for t in (i, seq - 1 - i)]

    def scan_slice(stage, carry):
        for step in range(SSM_SLICE):
            new = []
            for dr, sl in enumerate(step_rows(stage, step)):
                ar, ai = a_parts[dr]
                hr, hi = carry[2 * dr], carry[2 * dr + 1]
                b = hs_ref[dr, sl, :]
                nr = ar * hr - ai * hi + b[:, :sb]
                ni = ar * hi + ai * hr + b[:, sb:]
                hs_ref[dr, sl, :sb] = nr
                hs_ref[dr, sl, sb:] = ni
                new += [nr, ni]
            carry = tuple(new)
        return carry

    def fix_slice(stage, carry):
        for step in range(SSM_SLICE):
            new = []
            for dr, sl in enumerate(step_rows(stage, step)):
                ar, ai = a_parts[dr]
                zr, zi = carry[2 * dr], carry[2 * dr + 1]
                nr = ar * zr - ai * zi
                ni = ar * zi + ai * zr
                hs_ref[dr, sl, :sb] += nr
                hs_ref[dr, sl, sb:] += ni
                new += [nr, ni]
            carry = tuple(new)
        return carry

    zero = jnp.zeros((rows, sb), F32)
    fin = (zero, zero, zero, zero)
    input_proj(0)
    for stage in range(n_slices):
        if stage + 1 < n_slices:
            input_proj(stage + 1)
        fin = scan_slice(stage, fin)
        if not chunked and stage >= 1:
            output_proj(stage - 1)

    if want_final:
        for dr in range(2):
            fin_ref[dr, 0, :, :sb] = fin[2 * dr]
            fin_ref[dr, 0, :, sb:] = fin[2 * dr + 1]

    if chunked:
        entry = []
        for dr in range(2):
            ar, ai = a_parts[dr]
            pr, pi = ar[0:1], ai[0:1]
            for _ in range(int(math.log2(seq))):
                pr, pi = pr * pr - pi * pi, 2.0 * pr * pi
            fr, fi = fin[2 * dr], fin[2 * dr + 1]
            s0 = s0_ref[dr, 0, 0]
            er, ei = s0[:, :sb], s0[:, sb:]
            order = range(rows) if dr == 0 else range(rows - 1, -1, -1)
            rows_r, rows_i = [None] * rows, [None] * rows
            for c in order:
                rows_r[c], rows_i[c] = er, ei
                er, ei = (pr * er - pi * ei + fr[c:c + 1], pr * ei + pi * er + fi[c:c + 1])
            entry += [jnp.concatenate(rows_r, axis=0), jnp.concatenate(rows_i, axis=0)]
        carry = tuple(entry)
        for stage in range(n_slices):
            carry = fix_slice(stage, carry)
            if stage >= 1:
                output_proj(stage - 1)
    output_proj(n_slices - 1)

    y = _gelu_tanh(yacc_ref[...] + d_ref[...] * ut_ref[...])
    y_ref[...] = y.reshape(seq, rows, SSM_UBLK)


def _ssm(u, l, w, s0, want_final):
    r, seq, _ = u.shape
    chunked = s0 is not None
    rows = V7X_SUBLANES
    in_specs = [pl.BlockSpec((rows, seq, SSM_UBLK), lambda i, j: (i, 0, j)),
                pl.BlockSpec((None, 2, 1, SSM_UBLK, 2 * SSM_SBLK), lambda i, j: (l, 0, j, 0, 0)),
                pl.BlockSpec((None, 2, 1, 1, 2 * SSM_SBLK), lambda i, j: (l, 0, j, 0, 0)),
                pl.BlockSpec((None, 2, 1, 2 * SSM_SBLK, SSM_UBLK), lambda i, j: (l, 0, j, 0, 0)),
                pl.BlockSpec((None, 1, SSM_UBLK), lambda i, j: (l, 0, j))]
    args = [u, w["ssm_b"], w["ssm_a"], w["ssm_c"], w["ssm_d"]]
    if chunked:
        in_specs.append(pl.BlockSpec((None, 2, 1, 1, 1, 2 * SSM_SBLK), lambda i, j: (l, 0, i, j, 0, 0)))
        args.append(s0)
    out_specs = [pl.BlockSpec((seq, rows, SSM_UBLK), lambda i, j: (0, i, j))]
    out_shape = [jax.ShapeDtypeStruct((seq, r, SSM_WIDTH), F32)]
    if want_final:
        out_specs.append(pl.BlockSpec((2, 1, rows, 2 * SSM_SBLK), lambda i, j: (0, j, i, 0)))
        out_shape.append(jax.ShapeDtypeStruct((2, SSM_NBLK, r, 2 * SSM_SBLK), F32))
    return pl.pallas_call(
        functools.partial(_ssm_kernel, chunked, want_final),
        grid=(r // rows, SSM_NBLK),
        in_specs=in_specs, out_specs=out_specs, out_shape=out_shape,
        scratch_shapes=[pltpu.VMEM((2, seq * rows, 2 * SSM_SBLK), F32), pltpu.VMEM((seq * rows, SSM_UBLK), F32),
                        pltpu.VMEM((seq * rows, SSM_UBLK), F32)],
        compiler_params=_params(2),
        name="ssm_latent" if chunked else "ssm_context",
    )(*args)


def _merge_kernel(last, x_ref, sh_ref, sc_ref, ga_ref, g_ref, oa_ref, ob_ref, oc_ref, y_ref,
                  wg_ref, wm_ref, wglu_ref, bglu_ref, wbr_ref, wout_ref, fn_ref, o_ref):
    x = x_ref[...]
    h = _modulated(x, g_ref, sh_ref, sc_ref).astype(BF16)
    y = y_ref[...]
    o_d = y * _sigmoid(_dot(y.astype(BF16), wglu_ref[...]) + bglu_ref[...])
    branches = (oa_ref[...].astype(F32), ob_ref[...].astype(F32), oc_ref[...].astype(F32), o_d)
    acc = None
    for n, o in enumerate(branches):
        gate = _dot(h, wg_ref[:, n * BRANCH_WIDTH:(n + 1) * BRANCH_WIDTH])
        br = _dot((o * (gate * _sigmoid(gate))).astype(BF16), wbr_ref[n])
        term = _sigmoid(_dot(h, wm_ref[:, n * D_MODEL:(n + 1) * D_MODEL])) * br
        acc = term if acc is None else acc + term
    xn = x + ga_ref[0] * _dot(acc.astype(BF16), wout_ref[...])
    if last:
        xn = _rms(xn) * fn_ref[...]
    o_ref[...] = xn


def _merge(x, shift, scale, gate, seq, oa, ob, oc, y, l, w, final_norm, last):
    n = x.shape[0]
    tm = MERGE_TM
    row = lambda i: (i, 0)
    mod_spec = _mod_spec(shift.shape[0], tm, seq)
    in_specs = [pl.BlockSpec((tm, D_MODEL), row), mod_spec, mod_spec, mod_spec, _layer_spec((1, D_MODEL), l),
                pl.BlockSpec((tm, 512), row), pl.BlockSpec((tm, 512), row), pl.BlockSpec((tm, 512), row),
                pl.BlockSpec((tm, 512), row),
                _layer_spec((D_MODEL, N_BRANCH * BRANCH_WIDTH), l), _layer_spec((D_MODEL, N_BRANCH * D_MODEL), l),
                _layer_spec((SSM_WIDTH, SSM_WIDTH), l), _layer_spec((1, SSM_WIDTH), l),
                _layer_spec((N_BRANCH, BRANCH_WIDTH, D_MODEL), l), _layer_spec((D_MODEL, D_MODEL), l),
                _const_spec((1, D_MODEL))]
    return pl.pallas_call(
        functools.partial(_merge_kernel, last),
        grid=(n // tm,),
        in_specs=in_specs,
        out_specs=pl.BlockSpec((tm, D_MODEL), row),
        out_shape=jax.ShapeDtypeStruct((n, D_MODEL), F32),
        compiler_params=_params(1),
        name="merge",
    )(x, shift, scale, gate, w["norm_g"], oa, ob, oc, y, w["wg"], w["wm"], w["wglu"], w["bglu"],
      w["wbr"], w["wout"], final_norm)


def _rope_tables(n_tok, rot_dim):
    rows = n_tok // GRID_W
    row = jnp.repeat(jnp.arange(rows, dtype=F32), GRID_W)
    col = jnp.tile(jnp.arange(GRID_W, dtype=F32), rows)
    quarter = rot_dim // 4
    inv = ROPE_THETA ** (-jnp.arange(quarter, dtype=F32) / quarter)
    ang = jnp.concatenate([row[:, None] * inv, col[:, None] * inv], axis=-1)
    return jnp.cos(ang), jnp.sin(ang)


def _rope_lane_tables(n_tok):
    c, s = _rope_tables(n_tok, MLA_ROPE)
    ones = jnp.ones((n_tok, MLA_NOPE), F32)
    pad = MLA_HEAD_PAD - MLA_NOPE - MLA_ROPE
    c_mla = jnp.concatenate([ones, c, c, jnp.ones((n_tok, pad), F32)], axis=1)
    s_mla = jnp.concatenate([0.0 * ones, -s, s, jnp.zeros((n_tok, pad), F32)], axis=1)
    c, s = _rope_tables(n_tok, DIFF_HEAD_DIM)
    c_64 = jnp.concatenate([c, c, c, c], axis=1)
    s_64 = jnp.concatenate([-s, s, -s, s], axis=1)
    return c_mla, s_mla, c_64, s_64


def _block_diag(blocks):
    g = blocks.shape[-3]
    eye = jnp.eye(g, dtype=blocks.dtype)
    out = jnp.einsum("...grc,gh->...grhc", blocks, eye)
    return out.reshape(blocks.shape[:-3] + (g * blocks.shape[-2], g * blocks.shape[-1]))


def _in_output_half(v):
    zeros = jnp.zeros_like(v[..., 0, :])
    return jnp.stack([jnp.concatenate([v[..., 0, :], zeros], axis=-1),
                      jnp.concatenate([zeros, v[..., 1, :]], axis=-1)], axis=-2)


def _weights(w_in, p):
    w_t = jnp.swapaxes(w_in, 1, 2).reshape(DEPTH * D_IN, D_MODEL)
    wmix = _transposed_rows(w_t, _IN_MIX_START, _M_WIDTH, _M_WIDTH // 2)
    wg = _transposed_rows(w_t, _IN_GATE_START, N_BRANCH * BRANCH_WIDTH, 1024)
    wm = _transposed_rows(w_t, _IN_MERGE_START, N_BRANCH * D_MODEL, 1024)
    zeros = lambda wd: jnp.zeros((DEPTH, D_MODEL, wd), F32)
    kpe_lo = MLA_Q_LORA + MLA_KV_LORA
    whead = jnp.concatenate([w_in[:, :, :kpe_lo], zeros(MLA_NOPE), w_in[:, :, kpe_lo:_IN_MIX_START],
                             zeros(MLA_HEAD_PAD - MLA_NOPE - MLA_ROPE)], axis=2).astype(BF16)

    hq = MLA_NOPE + MLA_ROPE
    wqb = p["w_mla_q_b"].reshape(DEPTH, MLA_Q_LORA, MLA_HEADS, hq)
    wqb = jnp.pad(wqb, ((0, 0), (0, 0), (0, 0), (0, MLA_HEAD_PAD - hq))).reshape(DEPTH, MLA_Q_LORA, -1).astype(BF16)
    wkv = p["w_mla_kv_b"].reshape(DEPTH, MLA_KV_LORA, MLA_HEADS, MLA_NOPE + MLA_V)
    wk = jnp.pad(wkv[..., :MLA_NOPE], ((0, 0), (0, 0), (0, 0), (0, MLA_HEAD_PAD - MLA_NOPE)))
    wk = wk.reshape(DEPTH, MLA_KV_LORA, -1).astype(BF16)
    wv = _in_output_half(wkv[..., MLA_NOPE:].reshape(DEPTH, MLA_KV_LORA, MLA_HEADS // 2, 2, MLA_V))
    wv = wv.reshape(DEPTH, MLA_KV_LORA, -1).astype(BF16)
    eye = jnp.eye(GQA_HEAD_DIM, dtype=F32)
    egv = _block_diag(jnp.broadcast_to(
        _in_output_half(jnp.stack([eye, eye], axis=1)).reshape(GQA_HEAD_DIM, 2 * V_PAD),
        (GQA_KV_HEADS, GQA_HEAD_DIM, 2 * V_PAD))).astype(BF16)

    gmat =_block_diag(jnp.full((GQA_HEADS, GQA_HEAD_DIM, GQA_HEAD_DIM), 1.0 / GQA_HEAD_DIM, F32)).astype(BF16)

    lam = lax.complex(p["ssm_a_re"], p["ssm_a_im"])
    dt = jnp.exp(p["ssm_log_dt"])[..., None]
    abar = jnp.exp(lam * dt)
    bbar = ((abar - 1.0) / lam)[..., None] * lax.complex(p["ssm_b_re"], p["ssm_b_im"])
    blk = lambda a: a.reshape((DEPTH, 2, SSM_NBLK, SSM_GBLOCK) + a.shape[3:])
    b_t = jnp.swapaxes(blk(bbar), -1, -2)
    ssm_b = jnp.concatenate([_block_diag(jnp.real(b_t)), _block_diag(jnp.imag(b_t))], axis=-1).astype(BF16)
    ab = blk(abar).reshape(DEPTH, 2, SSM_NBLK, 1, SSM_SBLK)
    ssm_a = jnp.concatenate([jnp.real(ab), jnp.imag(ab)], axis=-1)
    c_re = jnp.swapaxes(blk(p["ssm_c_re"]), -1, -2)
    c_im = jnp.swapaxes(blk(p["ssm_c_im"]), -1, -2)
    ssm_c = jnp.concatenate([_block_diag(c_re), -_block_diag(c_im)], axis=-2).astype(BF16)

    row = lambda a: a[:, None, :]
    return dict(
        norm_g=row(p["norm_g"]), whead=whead, wmix=wmix, qn=row(p["mla_q_norm"]), wqb=wqb,
        kvn=row(p["mla_kv_norm"]), wk=wk, wv=wv,
        gqn=row(jnp.tile(p["gqa_q_norm"], (1, GQA_HEADS))), gkn=row(jnp.tile(p["gqa_k_norm"], (1, GQA_KV_HEADS))),
        gmat=gmat, egv=egv,
        lam_parts=jnp.stack([p["diff_lq1"], p["diff_lk1"], p["diff_lq2"], p["diff_lk2"]], axis=1),
        subln=row(p["diff_subln"]),
        ssm_b=ssm_b, ssm_a=ssm_a, ssm_c=ssm_c, ssm_d=row(p["ssm_d"]),
        wg=wg, wm=wm, wglu=p["ssm_glu_w"].astype(BF16), bglu=row(p["ssm_glu_b"]),
        wbr=p["w_branch_out"].astype(BF16), wout=p["w_out"].astype(BF16),
    )


def _mixers(pr, b, t, l, w, lam_init, s0):
    q, k, v, dq, dk, dv, gq, gk, gv, u = pr[:10]
    three = lambda a: a.reshape(b, a.shape[0] // b, a.shape[-1])
    k, v, dk, dv, gk, gv = (three(a) for a in (k, v, dk, dv, gk, gv))
    mla_heads = tuple((hd, hd) for hd in range(MLA_HEADS))
    gqa_heads = tuple((hd // GQA_GROUP, 2 * (hd // GQA_GROUP) + hd % 2) for hd in range(GQA_HEADS))
    o_a = _attention(functools.partial(_attn_heads_kernel, mla_heads, MLA_HEAD_PAD), "attn_mla",
                     three(q), k, v)
    o_b = _attention(functools.partial(_diff_attn_kernel, lam_init), "attn_diff", three(dq), dk, dv,
                     extra_specs=(_layer_spec((4, DIFF_HEAD_DIM), l), _layer_spec((1, DIFF_V), l)),
                     extra=(w["lam_parts"], w["subln"]))
    o_c = _attention(functools.partial(_attn_heads_kernel, gqa_heads, GQA_HEAD_DIM), "attn_gqa",
                     three(gq), gk, gv)
    n = b * t
    u_rows = u.reshape(n // SSM_SEQ, SSM_SEQ, SSM_WIDTH)
    if s0 is None:
        y_t, fin = _ssm(u_rows, l, w, None, True)
    else:
        (y_t,) = _ssm(u_rows, l, w, s0, False)
        fin = None
    y = jnp.swapaxes(y_t, 0, 1).reshape(n, SSM_WIDTH)
    flat = lambda a: a.reshape(n, 512)
    return flat(o_a), flat(o_b), flat(o_c), y, fin


def kernel(x_prompt, x_sample, cache_mla_ckv, cache_mla_krope, cache_diff_k, cache_diff_v, cache_gqa_k, cache_gqa_v, state_ssm, c, c_ctx, norm_g, w_mod, b_mod, w_in, mla_q_norm, w_mla_q_b, mla_kv_norm, w_mla_kv_b, diff_lq1, diff_lk1, diff_lq2, diff_lk2, diff_subln, gqa_q_norm, gqa_k_norm, ssm_a_re, ssm_a_im, ssm_log_dt, ssm_b_re, ssm_b_im, ssm_c_re, ssm_c_im, ssm_d, ssm_glu_w, ssm_glu_b, w_branch_out, w_out, final_norm):
    p = dict(norm_g=norm_g, mla_q_norm=mla_q_norm, w_mla_q_b=w_mla_q_b, mla_kv_norm=mla_kv_norm,
             w_mla_kv_b=w_mla_kv_b, diff_lq1=diff_lq1, diff_lk1=diff_lk1, diff_lq2=diff_lq2, diff_lk2=diff_lk2,
             diff_subln=diff_subln, gqa_q_norm=gqa_q_norm, gqa_k_norm=gqa_k_norm, ssm_a_re=ssm_a_re,
             ssm_a_im=ssm_a_im, ssm_log_dt=ssm_log_dt, ssm_b_re=ssm_b_re, ssm_b_im=ssm_b_im, ssm_c_re=ssm_c_re,
             ssm_c_im=ssm_c_im, ssm_d=ssm_d, ssm_glu_w=ssm_glu_w, ssm_glu_b=ssm_glu_b,
             w_branch_out=w_branch_out, w_out=w_out)
    bp, tp, _ = x_prompt.shape
    bs, ts, _ = x_sample.shape
    past = cache_mla_ckv.shape[2]
    assert tp == SSM_SEQ and ts % SSM_SEQ == 0 and ts // SSM_SEQ == V7X_SUBLANES
    rope = _rope_lane_tables(ts)
    w = _weights(w_in, p)

    cvec = jnp.zeros((V7X_SUBLANES, D_MODEL), F32).at[0].set(c_ctx).at[1:1 + bs].set(c)
    mod = _modulation(cvec, w_mod, b_mod)

    pad_lo = MLA_NOPE
    pad_hi = MLA_HEAD_PAD - MLA_NOPE - MLA_ROPE
    kpe_ctx = jnp.pad(cache_mla_krope, ((0, 0), (0, 0), (0, 0), (pad_lo, pad_hi)))
    flat_ctx = lambda a: a.reshape(bs, DEPTH, past, -1).astype(BF16)
    gv_both = _in_output_half(jnp.stack([cache_gqa_v, cache_gqa_v], axis=-2))
    dk_ctx, dv_ctx, gk_ctx, gv_ctx = (flat_ctx(a) for a in (cache_diff_k, cache_diff_v, cache_gqa_k, gv_both))
    s0_all = state_ssm.reshape(bs, DEPTH, 2, SSM_NBLK, SSM_GBLOCK * SSM_STATE, 2)
    s0_all = jnp.transpose(s0_all, (1, 2, 0, 3, 5, 4)).reshape(DEPTH, 2, bs, SSM_NBLK, 1, 2 * SSM_SBLK)

    xp = x_prompt.reshape(bp * tp, D_MODEL)
    xs = x_sample.reshape(bs * ts, D_MODEL)
    fn = final_norm[None]
    sides = ()
    states = []
    for l in range(DEPTH):
        lam_init = 0.8 - 0.6 * math.exp(-0.3 * l)
        last = l == DEPTH - 1
        sh, sc, ga = (mod[l, :, i * D_MODEL:(i + 1) * D_MODEL][:, None, :] for i in range(3))

        pr = _proj(xp, sh[0:1], sc[0:1], tp, l, w, prev_sides=sides)
        o_a, o_b, o_c, y, fin = _mixers(pr, bp, tp, l, w, lam_init, None)
        xp = _merge(xp, sh[0:1], sc[0:1], ga[0:1], tp, o_a, o_b, o_c, y, l, w, fn, last)
        sides = tuple(pr[10:])
        fin = fin.reshape(2, SSM_NBLK, bp, 2, SSM_GBLOCK, SSM_STATE)
        states.append(jnp.transpose(fin, (2, 0, 1, 4, 5, 3)).reshape(bp, 2, SSM_GROUPS, SSM_STATE, 2))

        pr = _proj(xs, sh[1:1 + bs], sc[1:1 + bs], ts, l, w, rope=rope,
                   cached=(cache_mla_ckv, kpe_ctx, dk_ctx, dv_ctx, gk_ctx, gv_ctx))
        o_a, o_b, o_c, y, _ = _mixers(pr, bs, ts, l, w, lam_init, s0_all)
        xs = _merge(xs, sh[1:1 + bs], sc[1:1 + bs], ga[1:1 + bs], ts, o_a, o_b, o_c, y, l, w, fn, last)

    ckv_s, kpe_t, dk_t, dv_s, gk_t, gv_t = sides
    untranspose = lambda a: jnp.swapaxes(a, 2, 3)
    return (xp.reshape(bp, tp, D_MODEL), xs.reshape(bs, ts, D_MODEL),
            ckv_s, untranspose(kpe_t),
            untranspose(dk_t).reshape(bp, DEPTH, tp, DIFF_HEADS, 2, DIFF_HEAD_DIM), dv_s,
            untranspose(gk_t).reshape(bp, DEPTH, tp, GQA_KV_HEADS, GQA_HEAD_DIM),
            untranspose(gv_t).reshape(bp, DEPTH, tp, GQA_KV_HEADS, GQA_HEAD_DIM),
            jnp.stack(states, axis=1))
```

```python
import functools
import math

import jax
import jax.numpy as jnp
from jax import lax
from jax.experimental import pallas as pl
from jax.experimental.pallas import tpu as pltpu

F32 = jnp.float32
BF16 = jnp.bfloat16

D_MODEL = 1024
DEPTH = 2
GRID_W = 64
ROPE_THETA = 10000.0
EPS = 1e-6

MLA_HEADS = 8
MLA_NOPE = 64
MLA_ROPE = 32
MLA_V = 64
MLA_Q_LORA = 256
MLA_KV_LORA = 128
MLA_SCALE = (MLA_NOPE + MLA_ROPE) ** -0.5
MLA_HEAD_PAD = 128
DIFF_HEADS = 4
DIFF_HEAD_DIM = 64
DIFF_V = 2 * DIFF_HEAD_DIM
DIFF_SCALE = DIFF_HEAD_DIM ** -0.5
GQA_HEADS = 8
GQA_KV_HEADS = 2
GQA_HEAD_DIM = 64
GQA_GROUP = GQA_HEADS // GQA_KV_HEADS
GQA_KV_WIDTH = GQA_KV_HEADS * GQA_HEAD_DIM
GQA_SCALE = GQA_HEAD_DIM ** -0.5
SSM_WIDTH = 512
SSM_GROUP = 16
SSM_GROUPS = SSM_WIDTH // SSM_GROUP
SSM_STATE = 64
N_BRANCH = 4
BRANCH_WIDTH = 512

IN_SPLITS = (MLA_Q_LORA, MLA_KV_LORA, MLA_ROPE, 512, 512, 512, 512, 128, 128, SSM_WIDTH,
             N_BRANCH * BRANCH_WIDTH, N_BRANCH * D_MODEL)
D_IN = sum(IN_SPLITS)

V7X_LANES = 128
V7X_SUBLANES = 8
V7X_VMEM_LIMIT_BYTES = 56 * 1024 * 1024

_H_QA = (0, 256)
_H_KVA = (256, 384)
_H_KPE = (384, 512)
_H_WIDTH = 512
_M_DQ = (0, 512)
_M_DK = (512, 1024)
_M_DV = (1024, 1536)
_M_GQ = (1536, 2048)
_M_GK = (2048, 2176)
_M_GV = (2176, 2304)
_M_U = (2304, 2816)
_M_WIDTH = 2816
_IN_MIX_START = MLA_Q_LORA + MLA_KV_LORA + MLA_ROPE
_IN_GATE_START = _IN_MIX_START + _M_WIDTH
_IN_MERGE_START = _IN_GATE_START + N_BRANCH * BRANCH_WIDTH

SSM_GBLOCK = 8
SSM_NBLK = SSM_GROUPS // SSM_GBLOCK
SSM_SBLK = SSM_GBLOCK * SSM_STATE
SSM_UBLK = SSM_GBLOCK * SSM_GROUP
SSM_SEQ = 256
SSM_SLICE = 32

V_PAD = 128

PROJ_TM = 512
MERGE_TM = 512
ATTN_TQ_LONG = 512
ATTN_SEQ_PER_STEP = 2
ATTN_GROUP_LONG = 2


def _params(n_axes, vmem=V7X_VMEM_LIMIT_BYTES):
    return pltpu.CompilerParams(dimension_semantics=("arbitrary",) * n_axes, vmem_limit_bytes=vmem)


def _const_spec(shape):
    zeros = (0,) * len(shape)
    return pl.BlockSpec(shape, lambda *_: zeros, pipeline_mode=pl.Buffered(1))


def _layer_spec(shape, l):
    zeros = (0,) * len(shape)
    return pl.BlockSpec((None,) + tuple(shape), lambda *_: (l,) + zeros, pipeline_mode=pl.Buffered(1))


def _mod_spec(n_rows, tm, seq):
    if n_rows == 1:
        return pl.BlockSpec((1, 1, D_MODEL), lambda i: (0, 0, 0))
    return pl.BlockSpec((1, 1, D_MODEL), lambda i: (i * tm // seq, 0, 0))


def _dot(a, b):
    return jnp.dot(a, b, preferred_element_type=F32)


def _dot_nt(a, b):
    return lax.dot_general(a, b, (((1,), (1,)), ((), ())), preferred_element_type=F32)


def _rms(x):
    return x * lax.rsqrt(jnp.mean(x * x, axis=-1, keepdims=True) + EPS)


def _sigmoid(x):
    return 1.0 / (1.0 + jnp.exp(-x))


def _gelu_tanh(x):
    return 0.5 * x * (1.0 + jnp.tanh(math.sqrt(2.0 / math.pi) * (x + 0.044715 * (x * x * x))))


def _group_mean_sq(x, gmat):
    return _dot((x * x).astype(BF16), gmat)


def _rope_chunk(x, cos, sin, half):
    lane = lax.broadcasted_iota(jnp.int32, x.shape, 1)
    first = (lane % (2 * half)) < half
    swapped = jnp.where(first, pltpu.roll(x, V7X_LANES - half, 1), pltpu.roll(x, half, 1))
    return x * cos + swapped * sin


def _transpose_cast_kernel(w_ref, o_ref):
    o_ref[...] = w_ref[...].T.astype(BF16)


def _transposed_rows(w_t, row0, n_rows, blk):
    return pl.pallas_call(
        _transpose_cast_kernel,
        grid=(DEPTH, n_rows // blk),
        in_specs=[pl.BlockSpec((pl.Element(blk), pl.Element(D_MODEL)),
                               lambda l, j: (pl.multiple_of(l * D_IN + row0 + j * blk, V7X_SUBLANES), 0))],
        out_specs=pl.BlockSpec((None, D_MODEL, blk), lambda l, j: (l, 0, j)),
        out_shape=jax.ShapeDtypeStruct((DEPTH, D_MODEL, n_rows), BF16),
        compiler_params=_params(2),
        name="w_in_columns",
    )(w_t)


def _mod_kernel(c_ref, w_ref, b_ref, o_ref):
    c = c_ref[...]
    a = (c * _sigmoid(c)).astype(BF16)
    o_ref[0] = _dot(a, w_ref[0].astype(BF16)) + b_ref[0]


def _modulation(cvec, w_mod, b_mod):
    tn = 512
    return pl.pallas_call(
        _mod_kernel,
        grid=(DEPTH, 3 * D_MODEL // tn),
        in_specs=[pl.BlockSpec((V7X_SUBLANES, D_MODEL), lambda l, j: (0, 0)),
                  pl.BlockSpec((1, D_MODEL, tn), lambda l, j: (l, 0, j)),
                  pl.BlockSpec((1, 1, tn), lambda l, j: (l, 0, j))],
        out_specs=pl.BlockSpec((1, V7X_SUBLANES, tn), lambda l, j: (l, 0, j)),
        out_shape=jax.ShapeDtypeStruct((DEPTH, V7X_SUBLANES, 3 * D_MODEL), F32),
        compiler_params=_params(2),
        name="modulation",
    )(cvec, w_mod, b_mod.reshape(DEPTH, 1, 3 * D_MODEL))


def _modulated(x, g_ref, sh_ref, sc_ref):
    return (_rms(x) * g_ref[...]) * (1.0 + sc_ref[0]) + sh_ref[0]


def _proj_kernel(latent, n_prev, first_layer, *refs):
    (x_ref, sh_ref, sc_ref, g_ref, wh_ref, wx_ref, qn_ref, wqb_ref, kvn_ref, wk_ref, wv_ref,
     gqn_ref, gkn_ref, gm_ref, egv_ref) = refs[:15]
    pos = 15
    if latent:
        c_mla, s_mla, c_64, s_64 = refs[pos:pos + 4]
        cached = refs[pos + 4:pos + 10]
        pos += 10
    pos += n_prev
    (q_o, k_o, v_o, dq_o, dk_o, dv_o, gq_o, gk_o, gv_o, u_o) = refs[pos:pos + 10]
    pos += 10
    if not latent:
        ckv_f, kpe_t, dk_t, dv_f, gk_t, gv_t = refs[pos:pos + 6]

    def mla_keys_values(ckv_b, kpe):
        kn = _dot(ckv_b, wk_ref[...])
        for hd in range(MLA_HEADS):
            sl = slice(hd * MLA_HEAD_PAD, (hd + 1) * MLA_HEAD_PAD)
            k_o[:, sl] = (kn[:, sl] + kpe).astype(BF16)
        v_o[...] = _dot(ckv_b, wv_ref[...]).astype(BF16)

    def cached_tokens():
        ckv_c, kpe_c, dk_c, dv_c, gk_c, gv_c = cached
        mla_keys_values(ckv_c[...].astype(BF16), kpe_c[...])
        dk_o[...] = dk_c[...]
        dv_o[...] = dv_c[...]
        gk_o[...] = gk_c[...]
        gv_o[...] = gv_c[...]

    def new_tokens():
        _proj_new_tokens(latent, first_layer, refs[:15], (c_mla, s_mla, c_64, s_64) if latent else None,
                         refs[pos - 10:pos], refs[pos:pos + 6] if not latent else None, mla_keys_values)

    if latent:
        pl.when(pl.program_id(1) == 0)(cached_tokens)
        pl.when(pl.program_id(1) > 0)(new_tokens)
    else:
        new_tokens()


def _proj_new_tokens(latent, first_layer, in_refs, rope_refs, out_refs, side_refs, mla_keys_values):
    (x_ref, sh_ref, sc_ref, g_ref, wh_ref, wx_ref, qn_ref, wqb_ref, kvn_ref, wk_ref, wv_ref,
     gqn_ref, gkn_ref, gm_ref, egv_ref) = in_refs
    (q_o, k_o, v_o, dq_o, dk_o, dv_o, gq_o, gk_o, gv_o, u_o) = out_refs
    if latent:
        c_mla, s_mla, c_64, s_64 = rope_refs
    else:
        ckv_f, kpe_t, dk_t, dv_f, gk_t, gv_t = side_refs

    h = _modulated(x_ref[...], g_ref, sh_ref, sc_ref).astype(BF16)
    tm = h.shape[0]

    p_head = _dot(h, wh_ref[...])
    p_mix = _dot(h, wx_ref[...])

    def head(cols):
        return p_head[:, cols[0]:cols[1]]

    def mix(cols):
        return p_mix[:, cols[0]:cols[1]]

    def rope_mla(x):
        return _rope_chunk(x, c_mla[...], s_mla[...], MLA_ROPE // 2) if latent else x

    def rope_64(x):
        if not latent:
            return x
        c, s = c_64[...], s_64[...]
        chunks = [_rope_chunk(x[:, i:i + V7X_LANES], c, s, DIFF_HEAD_DIM // 2)
                  for i in range(0, x.shape[1], V7X_LANES)]
        return chunks[0] if len(chunks) == 1 else jnp.concatenate(chunks, axis=1)

    def per_sequence(x):
        return [x[b * SSM_SEQ:(b + 1) * SSM_SEQ] for b in range(tm // SSM_SEQ)]

    def slot(b):
        return (b,) if first_layer is None else (b, first_layer)

    def zero_other_slots(ref):
        if first_layer is not None:
            for b in range(ref.shape[0]):
                for other in range(ref.shape[1]):
                    if other != first_layer:
                        ref[b, other] = jnp.zeros(ref.shape[2:], ref.dtype)

    qn = (_rms(head(_H_QA)) * qn_ref[...]).astype(BF16)
    q = _dot(qn, wqb_ref[...])
    for hd in range(MLA_HEADS):
        sl = slice(hd * MLA_HEAD_PAD, (hd + 1) * MLA_HEAD_PAD)
        q_o[:, sl] = (rope_mla(q[:, sl]) * MLA_SCALE).astype(BF16)
    ckv = _rms(head(_H_KVA)) * kvn_ref[...]
    kpe = head(_H_KPE)
    if not latent:
        for b, (c_b, k_b) in enumerate(zip(per_sequence(ckv), per_sequence(kpe))):
            ckv_f[slot(b)] = c_b
            kpe_t[slot(b)] = k_b.T[MLA_NOPE:MLA_NOPE + MLA_ROPE, :]
        zero_other_slots(ckv_f)
        zero_other_slots(kpe_t)
    mla_keys_values(ckv.astype(BF16), rope_mla(kpe))

    dq_o[...] = (rope_64(mix(_M_DQ)) * DIFF_SCALE).astype(BF16)
    dk = rope_64(mix(_M_DK))
    dv = mix(_M_DV)
    dk_o[...] = dk.astype(BF16)
    dv_o[...] = dv.astype(BF16)

    gq = mix(_M_GQ)
    gq = gq * lax.rsqrt(_group_mean_sq(gq, gm_ref[...]) + EPS) * gqn_ref[...]
    gq_o[...] = (rope_64(gq) * GQA_SCALE).astype(BF16)
    gk = mix(_M_GK)
    kw = GQA_KV_WIDTH
    gk = gk * lax.rsqrt(_group_mean_sq(gk, gm_ref[:kw, :kw]) + EPS) * gkn_ref[...]
    gv = mix(_M_GV)
    if not latent:
        for b, (dk_b, dv_b, gk_b, gv_b) in enumerate(zip(*(per_sequence(a) for a in (dk, dv, gk, gv)))):
            dk_t[slot(b)] = dk_b.T
            for hd in range(DIFF_HEADS):
                dv_f[slot(b) + (slice(None), hd, slice(None))] = dv_b[:, hd * DIFF_V:(hd + 1) * DIFF_V]
            gk_t[slot(b)] = gk_b.T
            gv_t[slot(b)] = gv_b.T
        for ref in (dk_t, dv_f, gk_t, gv_t):
            zero_other_slots(ref)
    gk_o[...] = rope_64(gk).astype(BF16)
    gv_o[...] = _dot(gv.astype(BF16), egv_ref[...]).astype(BF16)

    u_o[...] = mix(_M_U)


def _side_shapes(n_seq):
    return [((n_seq, DEPTH, SSM_SEQ, MLA_KV_LORA), (SSM_SEQ, MLA_KV_LORA)),
            ((n_seq, DEPTH, MLA_ROPE, SSM_SEQ), (MLA_ROPE, SSM_SEQ)),
            ((n_seq, DEPTH, 512, SSM_SEQ), (512, SSM_SEQ)),
            ((n_seq, DEPTH, SSM_SEQ, DIFF_HEADS, DIFF_V), (SSM_SEQ, DIFF_HEADS, DIFF_V)),
            ((n_seq, DEPTH, GQA_KV_WIDTH, SSM_SEQ), (GQA_KV_WIDTH, SSM_SEQ)),
            ((n_seq, DEPTH, GQA_KV_WIDTH, SSM_SEQ), (GQA_KV_WIDTH, SSM_SEQ))]


def _proj(x, shift, scale, seq, l, w, rope=None, cached=None, prev_sides=()):
    n = x.shape[0]
    tm = PROJ_TM
    latent = cached is not None
    kw = GQA_KV_WIDTH
    if latent:
        tiles = seq // tm
        assert cached[0].shape[2] == tm
        grid = (n // seq, 1 + tiles)
        row = lambda b, j: (b * tiles + jnp.maximum(j - 1, 0), 0)
        kv_row = lambda b, j: (b * (1 + tiles) + j, 0)
        n_kv = (n // seq) * (cached[0].shape[2] + seq)
        mod_spec = pl.BlockSpec((1, 1, D_MODEL), lambda b, j: (b, 0, 0))
    else:
        grid = (n // tm,)
        row = kv_row = lambda i: (i, 0)
        n_kv = n
        mod_spec = _mod_spec(shift.shape[0], tm, seq)
    in_specs = [pl.BlockSpec((tm, D_MODEL), row), mod_spec, mod_spec,
                _layer_spec((1, D_MODEL), l), _layer_spec((D_MODEL, _H_WIDTH), l), _layer_spec((D_MODEL, _M_WIDTH), l),
                _layer_spec((1, MLA_Q_LORA), l), _layer_spec((MLA_Q_LORA, MLA_HEADS * MLA_HEAD_PAD), l),
                _layer_spec((1, MLA_KV_LORA), l), _layer_spec((MLA_KV_LORA, MLA_HEADS * MLA_HEAD_PAD), l),
                _layer_spec((MLA_KV_LORA, MLA_HEADS * V_PAD), l),
                _layer_spec((1, 512), l), _layer_spec((1, kw), l), _const_spec((512, 512)),
                _const_spec((kw, 2 * GQA_KV_HEADS * V_PAD))]
    args = [x, shift, scale, w["norm_g"], w["whead"], w["wmix"], w["qn"], w["wqb"], w["kvn"], w["wk"], w["wv"],
            w["gqn"], w["gkn"], w["gmat"], w["egv"]]
    if latent:
        tab = pl.BlockSpec((tm, V7X_LANES), lambda b, j: (jnp.maximum(j - 1, 0), 0))
        in_specs += [tab] * 4
        args += list(rope)
        in_specs += [pl.BlockSpec((None, None) + c.shape[2:], lambda b, j: (b, l, 0, 0)) for c in cached]
        args += list(cached)
    n_fixed = len(args)
    in_specs += [pl.BlockSpec(memory_space=pl.ANY)] * len(prev_sides)
    args += list(prev_sides)
    widths = [(1024, BF16, False), (1024, BF16, True), (MLA_HEADS * V_PAD, BF16, True), (512, BF16, False),
              (512, BF16, True), (512, BF16, True), (512, BF16, False), (kw, BF16, True),
              (2 * GQA_KV_HEADS * V_PAD, BF16, True), (512, F32, False)]
    out_specs = [pl.BlockSpec((tm, wd), kv_row if kv else row) for wd, _, kv in widths]
    out_shape = [jax.ShapeDtypeStruct((n_kv if kv else n, wd), dt) for wd, dt, kv in widths]
    aliases = {}
    first_layer = None
    if not latent:
        per_tile = tm // SSM_SEQ
        first_layer = None if prev_sides else l
        for full, blk in _side_shapes(n // SSM_SEQ):
            zeros = (0,) * len(blk)
            if prev_sides:
                out_specs.append(pl.BlockSpec((per_tile, None) + blk, lambda i, zeros=zeros: (i, l) + zeros))
            else:
                out_specs.append(pl.BlockSpec((per_tile, DEPTH) + blk, lambda i, zeros=zeros: (i, 0) + zeros))
            out_shape.append(jax.ShapeDtypeStruct(full, F32))
        aliases = {n_fixed + k: len(widths) + k for k in range(len(prev_sides))}
    return pl.pallas_call(
        functools.partial(_proj_kernel, latent, len(prev_sides), first_layer),
        grid=grid,
        in_specs=in_specs, out_specs=out_specs, out_shape=out_shape,
        input_output_aliases=aliases,
        compiler_params=_params(len(grid)),
        name="proj_latent" if latent else "proj_context",
    )(*args)


def _softmax_maps(maps):
    scores = [_dot_nt(q, k) for q, k, _ in maps]
    tops = [jnp.max(s, axis=-1, keepdims=True) for s in scores]
    weights = [jnp.exp(s - m).astype(BF16) for s, m in zip(scores, tops)]
    outs = []
    for p, (_, _, v) in zip(weights, maps):
        ones = jnp.ones((v.shape[0], V7X_LANES), BF16)
        o = _dot(p, jnp.concatenate([v, ones], axis=1))
        outs.append((o[:, :V7X_LANES], o[:, V7X_LANES:]))
    return outs


def _head_maps(heads, q_width, q_ref, k_ref, v_ref, part):
    return [(q_ref[b, :, hd * q_width:(hd + 1) * q_width],
             k_ref[b, :, heads[hd][0] * q_width:(heads[hd][0] + 1) * q_width],
             v_ref[b, :, heads[hd][1] * V_PAD:(heads[hd][1] + 1) * V_PAD]) for b, hd in part]


def _head_outputs(outs, part, o_ref):
    terms = [o * (1.0 / total) for o, total in outs]
    for (b, hd), low, high in zip(part[0::2], terms[0::2], terms[1::2]):
        o_ref[b, :, (hd // 2) * V7X_LANES:(hd // 2 + 1) * V7X_LANES] = (low + high).astype(BF16)


def _diff_maps(q_ref, k_ref, v_ref, part):
    d = DIFF_HEAD_DIM
    return [(q_ref[b, :, c * d:(c + 1) * d], k_ref[b, :, c * d:(c + 1) * d],
             v_ref[b, :, hd * DIFF_V:(hd + 1) * DIFF_V])
            for b, hd in part for c in (2 * hd, 2 * hd + 1)]


def _diff_lambda(lp_ref, lam_init):
    lp = lp_ref[...]
    return (jnp.exp(jnp.sum(lp[0:1] * lp[1:2], axis=-1, keepdims=True))
            - jnp.exp(jnp.sum(lp[2:3] * lp[3:4], axis=-1, keepdims=True)) + lam_init)


def _diff_outputs(outs, part, lam, lam_init, sub_ref, o_ref):
    mixed = [o1 * (1.0 / t1) - o2 * (lam / t2) for (o1, t1), (o2, t2) in zip(outs[0::2], outs[1::2])]
    normed = [_rms(o) * sub_ref[...] * (1.0 - lam_init) for o in mixed]
    for (b, hd), o in zip(part, normed):
        o_ref[b, :, hd * DIFF_V:(hd + 1) * DIFF_V] = o.astype(BF16)


def _items(q_ref, n_heads):
    return [(b, hd) for b in range(q_ref.shape[0]) for hd in range(n_heads)]


def _attn_heads_kernel(heads, q_width, q_ref, k_ref, v_ref, o_ref):
    items = _items(q_ref, len(heads))
    group = ATTN_GROUP_LONG
    for g0 in range(0, len(items), group):
        part = items[g0:g0 + group]
        _head_outputs(_softmax_maps(_head_maps(heads, q_width, q_ref, k_ref, v_ref, part)), part, o_ref)


def _diff_attn_kernel(lam_init, q_ref, k_ref, v_ref, lp_ref, sub_ref, o_ref):
    lam = _diff_lambda(lp_ref, lam_init)
    items = _items(q_ref, DIFF_HEADS)
    group = ATTN_GROUP_LONG // 2
    for g0 in range(0, len(items), group):
        part = items[g0:g0 + group]
        _diff_outputs(_softmax_maps(_diff_maps(q_ref, k_ref, v_ref, part)), part, lam, lam_init, sub_ref, o_ref)


def _attn_short_kernel(lam_init, mla_heads, gqa_heads, qa_ref, ka_ref, va_ref, qb_ref, kb_ref, vb_ref,
                       qc_ref, kc_ref, vc_ref, lp_ref, sub_ref, oa_ref, ob_ref, oc_ref):
    part_a, part_b, part_c = _items(qa_ref, len(mla_heads)), _items(qb_ref, DIFF_HEADS), _items(qc_ref, len(gqa_heads))
    maps_a = _head_maps(mla_heads, MLA_HEAD_PAD, qa_ref, ka_ref, va_ref, part_a)
    maps_b = _diff_maps(qb_ref, kb_ref, vb_ref, part_b)
    maps_c = _head_maps(gqa_heads, GQA_HEAD_DIM, qc_ref, kc_ref, vc_ref, part_c)
    outs = _softmax_maps(maps_a + maps_b + maps_c)
    n_a, n_b = len(maps_a), len(maps_b)
    _head_outputs(outs[:n_a], part_a, oa_ref)
    _diff_outputs(outs[n_a:n_a + n_b], part_b, _diff_lambda(lp_ref, lam_init), lam_init, sub_ref, ob_ref)
    _head_outputs(outs[n_a + n_b:], part_c, oc_ref)


def _attention_short(lam_init, mla_heads, gqa_heads, qkv, extra_specs, extra):
    b, t, _ = qkv[0].shape
    nb = ATTN_SEQ_PER_STEP
    whole = lambda i: (i, 0, 0)
    in_specs = [pl.BlockSpec((nb, t, a.shape[2]), whole) for a in qkv] + list(extra_specs)
    return pl.pallas_call(
        functools.partial(_attn_short_kernel, lam_init, mla_heads, gqa_heads),
        grid=(b // nb,),
        in_specs=in_specs,
        out_specs=[pl.BlockSpec((nb, t, 512), whole)] * 3,
        out_shape=[jax.ShapeDtypeStruct((b, t, 512), BF16)] * 3,
        compiler_params=_params(1),
        name="attn_context",
    )(*qkv, *extra)


def _attention(kernel, name, q, k, v, extra_specs=(), extra=()):
    b, t, wq = q.shape
    s = k.shape[1]
    tq = ATTN_TQ_LONG
    in_specs = [pl.BlockSpec((1, tq, wq), lambda i, j: (i, j, 0)),
                pl.BlockSpec((1, s, k.shape[2]), lambda i, j: (i, 0, 0)),
                pl.BlockSpec((1, s, v.shape[2]), lambda i, j: (i, 0, 0))]
    in_specs += list(extra_specs)
    return pl.pallas_call(
        kernel,
        grid=(b, t // tq),
        in_specs=in_specs,
        out_specs=pl.BlockSpec((1, tq, 512), lambda i, j: (i, j, 0)),
        out_shape=jax.ShapeDtypeStruct((b, t, 512), BF16),
        compiler_params=_params(2),
        name=name,
    )(q, k, v, *extra)


def _ssm_kernel(chunked, want_final, *refs):
    u_ref, bm_ref, ab_ref, cm_ref, d_ref = refs[:5]
    pos = 5
    if chunked:
        s0_ref = refs[pos]
        pos += 1
    y_ref = refs[pos]
    pos += 1
    if want_final:
        fin_ref = refs[pos]
        pos += 1
    hs_ref, yacc_ref, ut_ref = refs[pos:pos + 3]

    seq, rows, sb = SSM_SEQ, V7X_SUBLANES, SSM_SBLK
    n_slices = seq // SSM_SLICE
    slice_rows = SSM_SLICE * rows

    def visited(dr, stage):
        return stage if dr == 0 else n_slices - 1 - stage

    def rows_of(k):
        return slice(k * slice_rows, (k + 1) * slice_rows)

    def u_rows(k, first_visit):
        if first_visit:
            steps = range(k * SSM_SLICE, (k + 1) * SSM_SLICE)
            ut_ref[rows_of(k), :] = jnp.concatenate([u_ref[:, t, :] for t in steps], axis=0)
        return ut_ref[rows_of(k), :]

    def input_proj(stage):
        for dr in range(2):
            k = visited(dr, stage)
            hs_ref[dr, rows_of(k), :] = _dot(u_rows(k, stage < n_slices // 2).astype(BF16), bm_ref[dr, 0])

    def output_proj(stage):
        for dr in range(2):
            r = rows_of(visited(dr, stage))
            part = _dot(hs_ref[dr, r, :].astype(BF16), cm_ref[dr, 0])
            if stage < n_slices // 2:
                yacc_ref[r, :] = part
            else:
                yacc_ref[r, :] += part

    a_parts = []
    for dr in range(2):
        a = ab_ref[dr, 0]
        a_parts.append((jnp.broadcast_to(a[:, :sb], (rows, sb)), jnp.broadcast_to(a[:, sb:], (rows, sb))))

    def step_rows(stage, step):
        i = stage * SSM_SLICE + step
        return [slice(t * rows, (t + 1) * rows) for t in (i, seq - 1 - i)]

    def scan_slice(stage, carry):
        for step in range(SSM_SLICE):
            new = []
            for dr, sl in enumerate(step_rows(stage, step)):
                ar, ai = a_parts[dr]
                hr, hi = carry[2 * dr], carry[2 * dr + 1]
                b = hs_ref[dr, sl, :]
                nr = ar * hr - ai * hi + b[:, :sb]
                ni = ar * hi + ai * hr + b[:, sb:]
                hs_ref[dr, sl, :sb] = nr
                hs_ref[dr, sl, sb:] = ni
                new += [nr, ni]
            carry = tuple(new)
        return carry

    def fix_slice(stage, carry):
        for step in range(SSM_SLICE):
            new = []
            for dr, sl in enumerate(step_rows(stage, step)):
                ar, ai = a_parts[dr]
                zr, zi = carry[2 * dr], carry[2 * dr + 1]
                nr = ar * zr - ai * zi
                ni = ar * zi + ai * zr
                hs_ref[dr, sl, :sb] += nr
                hs_ref[dr, sl, sb:] += ni
                new += [nr, ni]
            carry = tuple(new)
        return carry

    zero = jnp.zeros((rows, sb), F32)
    fin = (zero, zero, zero, zero)
    input_proj(0)
    for stage in range(n_slices):
        if stage + 1 < n_slices:
            input_proj(stage + 1)
        fin = scan_slice(stage, fin)
        if not chunked and stage >= 1:
            output_proj(stage - 1)

    if want_final:
        for dr in range(2):
            fin_ref[dr, 0, :, :sb] = fin[2 * dr]
            fin_ref[dr, 0, :, sb:] = fin[2 * dr + 1]

    if chunked:
        entry = []
        for dr in range(2):
            ar, ai = a_parts[dr]
            pr, pi = ar[0:1], ai[0:1]
            for _ in range(int(math.log2(seq))):
                pr, pi = pr * pr - pi * pi, 2.0 * pr * pi
            fr, fi = fin[2 * dr], fin[2 * dr + 1]
            s0 = s0_ref[dr, 0, 0]
            er, ei = s0[:, :sb], s0[:, sb:]
            order = range(rows) if dr == 0 else range(rows - 1, -1, -1)
            rows_r, rows_i = [None] * rows, [None] * rows
            for c in order:
                rows_r[c], rows_i[c] = er, ei
                er, ei = (pr * er - pi * ei + fr[c:c + 1], pr * ei + pi * er + fi[c:c + 1])
            entry += [jnp.concatenate(rows_r, axis=0), jnp.concatenate(rows_i, axis=0)]
        carry = tuple(entry)
        for stage in range(n_slices):
            carry = fix_slice(stage, carry)
            if stage >= 1:
                output_proj(stage - 1)
    output_proj(n_slices - 1)

    y = _gelu_tanh(yacc_ref[...] + d_ref[...] * ut_ref[...])
    y_ref[...] = y.reshape(seq, rows, SSM_UBLK)


def _ssm(u, l, w, s0, want_final):
    r, seq, _ = u.shape
    chunked = s0 is not None
    rows = V7X_SUBLANES
    in_specs = [pl.BlockSpec((rows, seq, SSM_UBLK), lambda i, j: (i, 0, j)),
                pl.BlockSpec((None, 2, 1, SSM_UBLK, 2 * SSM_SBLK), lambda i, j: (l, 0, j, 0, 0)),
                pl.BlockSpec((None, 2, 1, 1, 2 * SSM_SBLK), lambda i, j: (l, 0, j, 0, 0)),
                pl.BlockSpec((None, 2, 1, 2 * SSM_SBLK, SSM_UBLK), lambda i, j: (l, 0, j, 0, 0)),
                pl.BlockSpec((None, 1, SSM_UBLK), lambda i, j: (l, 0, j))]
    args = [u, w["ssm_b"], w["ssm_a"], w["ssm_c"], w["ssm_d"]]
    if chunked:
        in_specs.append(pl.BlockSpec((None, 2, 1, 1, 1, 2 * SSM_SBLK), lambda i, j: (l, 0, i, j, 0, 0)))
        args.append(s0)
    out_specs = [pl.BlockSpec((seq, rows, SSM_UBLK), lambda i, j: (0, i, j))]
    out_shape = [jax.ShapeDtypeStruct((seq, r, SSM_WIDTH), F32)]
    if want_final:
        out_specs.append(pl.BlockSpec((2, 1, rows, 2 * SSM_SBLK), lambda i, j: (0, j, i, 0)))
        out_shape.append(jax.ShapeDtypeStruct((2, SSM_NBLK, r, 2 * SSM_SBLK), F32))
    return pl.pallas_call(
        functools.partial(_ssm_kernel, chunked, want_final),
        grid=(r // rows, SSM_NBLK),
        in_specs=in_specs, out_specs=out_specs, out_shape=out_shape,
        scratch_shapes=[pltpu.VMEM((2, seq * rows, 2 * SSM_SBLK), F32), pltpu.VMEM((seq * rows, SSM_UBLK), F32),
                        pltpu.VMEM((seq * rows, SSM_UBLK), F32)],
        compiler_params=_params(2),
        name="ssm_latent" if chunked else "ssm_context",
    )(*args)


def _merge_kernel(last, x_ref, sh_ref, sc_ref, ga_ref, g_ref, oa_ref, ob_ref, oc_ref, y_ref,
                  wg_ref, wm_ref, wglu_ref, bglu_ref, wbr_ref, wout_ref, fn_ref, o_ref):
    x = x_ref[...]
    h = _modulated(x, g_ref, sh_ref, sc_ref).astype(BF16)
    y = y_ref[...]
    o_d = y * _sigmoid(_dot(y.astype(BF16), wglu_ref[...]) + bglu_ref[...])
    branches = (oa_ref[...].astype(F32), ob_ref[...].astype(F32), oc_ref[...].astype(F32), o_d)
    acc = None
    for n, o in enumerate(branches):
        gate = _dot(h, wg_ref[:, n * BRANCH_WIDTH:(n + 1) * BRANCH_WIDTH])
        br = _dot((o * (gate * _sigmoid(gate))).astype(BF16), wbr_ref[n])
        term = _sigmoid(_dot(h, wm_ref[:, n * D_MODEL:(n + 1) * D_MODEL])) * br
        acc = term if acc is None else acc + term
    xn = x + ga_ref[0] * _dot(acc.astype(BF16), wout_ref[...])
    if last:
        xn = _rms(xn) * fn_ref[...]
    o_ref[...] = xn


def _merge(x, shift, scale, gate, seq, oa, ob, oc, y, l, w, final_norm, last):
    n = x.shape[0]
    tm = MERGE_TM
    row = lambda i: (i, 0)
    mod_spec = _mod_spec(shift.shape[0], tm, seq)
    in_specs = [pl.BlockSpec((tm, D_MODEL), row), mod_spec, mod_spec, mod_spec, _layer_spec((1, D_MODEL), l),
                pl.BlockSpec((tm, 512), row), pl.BlockSpec((tm, 512), row), pl.BlockSpec((tm, 512), row),
                pl.BlockSpec((tm, 512), row),
                _layer_spec((D_MODEL, N_BRANCH * BRANCH_WIDTH), l), _layer_spec((D_MODEL, N_BRANCH * D_MODEL), l),
                _layer_spec((SSM_WIDTH, SSM_WIDTH), l), _layer_spec((1, SSM_WIDTH), l),
                _layer_spec((N_BRANCH, BRANCH_WIDTH, D_MODEL), l), _layer_spec((D_MODEL, D_MODEL), l),
                _const_spec((1, D_MODEL))]
    return pl.pallas_call(
        functools.partial(_merge_kernel, last),
        grid=(n // tm,),
        in_specs=in_specs,
        out_specs=pl.BlockSpec((tm, D_MODEL), row),
        out_shape=jax.ShapeDtypeStruct((n, D_MODEL), F32),
        compiler_params=_params(1),
        name="merge",
    )(x, shift, scale, gate, w["norm_g"], oa, ob, oc, y, w["wg"], w["wm"], w["wglu"], w["bglu"],
      w["wbr"], w["wout"], final_norm)


def _rope_tables(n_tok, rot_dim):
    rows = n_tok // GRID_W
    row = jnp.repeat(jnp.arange(rows, dtype=F32), GRID_W)
    col = jnp.tile(jnp.arange(GRID_W, dtype=F32), rows)
    quarter = rot_dim // 4
    inv = ROPE_THETA ** (-jnp.arange(quarter, dtype=F32) / quarter)
    ang = jnp.concatenate([row[:, None] * inv, col[:, None] * inv], axis=-1)
    return jnp.cos(ang), jnp.sin(ang)


def _rope_lane_tables(n_tok):
    c, s = _rope_tables(n_tok, MLA_ROPE)
    ones = jnp.ones((n_tok, MLA_NOPE), F32)
    pad = MLA_HEAD_PAD - MLA_NOPE - MLA_ROPE
    c_mla = jnp.concatenate([ones, c, c, jnp.ones((n_tok, pad), F32)], axis=1)
    s_mla = jnp.concatenate([0.0 * ones, -s, s, jnp.zeros((n_tok, pad), F32)], axis=1)
    c, s = _rope_tables(n_tok, DIFF_HEAD_DIM)
    c_64 = jnp.concatenate([c, c, c, c], axis=1)
    s_64 = jnp.concatenate([-s, s, -s, s], axis=1)
    return c_mla, s_mla, c_64, s_64


def _block_diag(blocks):
    g = blocks.shape[-3]
    eye = jnp.eye(g, dtype=blocks.dtype)
    out = jnp.einsum("...grc,gh->...grhc", blocks, eye)
    return out.reshape(blocks.shape[:-3] + (g * blocks.shape[-2], g * blocks.shape[-1]))


def _in_output_half(v):
    zeros = jnp.zeros_like(v[..., 0, :])
    return jnp.stack([jnp.concatenate([v[..., 0, :], zeros], axis=-1),
                      jnp.concatenate([zeros, v[..., 1, :]], axis=-1)], axis=-2)


def _weights(w_in, p):
    w_t = jnp.swapaxes(w_in, 1, 2).reshape(DEPTH * D_IN, D_MODEL)
    wmix = _transposed_rows(w_t, _IN_MIX_START, _M_WIDTH, _M_WIDTH // 2)
    wg = _transposed_rows(w_t, _IN_GATE_START, N_BRANCH * BRANCH_WIDTH, 1024)
    wm = _transposed_rows(w_t, _IN_MERGE_START, N_BRANCH * D_MODEL, 1024)
    zeros = lambda wd: jnp.zeros((DEPTH, D_MODEL, wd), F32)
    kpe_lo = MLA_Q_LORA + MLA_KV_LORA
    whead = jnp.concatenate([w_in[:, :, :kpe_lo], zeros(MLA_NOPE), w_in[:, :, kpe_lo:_IN_MIX_START],
                             zeros(MLA_HEAD_PAD - MLA_NOPE - MLA_ROPE)], axis=2).astype(BF16)

    hq = MLA_NOPE + MLA_ROPE
    wqb = p["w_mla_q_b"].reshape(DEPTH, MLA_Q_LORA, MLA_HEADS, hq)
    wqb = jnp.pad(wqb, ((0, 0), (0, 0), (0, 0), (0, MLA_HEAD_PAD - hq))).reshape(DEPTH, MLA_Q_LORA, -1).astype(BF16)
    wkv = p["w_mla_kv_b"].reshape(DEPTH, MLA_KV_LORA, MLA_HEADS, MLA_NOPE + MLA_V)
    wk = jnp.pad(wkv[..., :MLA_NOPE], ((0, 0), (0, 0), (0, 0), (0, MLA_HEAD_PAD - MLA_NOPE)))
    wk = wk.reshape(DEPTH, MLA_KV_LORA, -1).astype(BF16)
    wv = _in_output_half(wkv[..., MLA_NOPE:].reshape(DEPTH, MLA_KV_LORA, MLA_HEADS // 2, 2, MLA_V))
    wv = wv.reshape(DEPTH, MLA_KV_LORA, -1).astype(BF16)
    eye = jnp.eye(GQA_HEAD_DIM, dtype=F32)
    egv = _block_diag(jnp.broadcast_to(
        _in_output_half(jnp.stack([eye, eye], axis=1)).reshape(GQA_HEAD_DIM, 2 * V_PAD),
        (GQA_KV_HEADS, GQA_HEAD_DIM, 2 * V_PAD))).astype(BF16)

    gmat =_block_diag(jnp.full((GQA_HEADS, GQA_HEAD_DIM, GQA_HEAD_DIM), 1.0 / GQA_HEAD_DIM, F32)).astype(BF16)

    lam = lax.complex(p["ssm_a_re"], p["ssm_a_im"])
    dt = jnp.exp(p["ssm_log_dt"])[..., None]
    abar = jnp.exp(lam * dt)
    bbar = ((abar - 1.0) / lam)[..., None] * lax.complex(p["ssm_b_re"], p["ssm_b_im"])
    blk = lambda a: a.reshape((DEPTH, 2, SSM_NBLK, SSM_GBLOCK) + a.shape[3:])
    b_t = jnp.swapaxes(blk(bbar), -1, -2)
    ssm_b = jnp.concatenate([_block_diag(jnp.real(b_t)), _block_diag(jnp.imag(b_t))], axis=-1).astype(BF16)
    ab = blk(abar).reshape(DEPTH, 2, SSM_NBLK, 1, SSM_SBLK)
    ssm_a = jnp.concatenate([jnp.real(ab), jnp.imag(ab)], axis=-1)
    c_re = jnp.swapaxes(blk(p["ssm_c_re"]), -1, -2)
    c_im = jnp.swapaxes(blk(p["ssm_c_im"]), -1, -2)
    ssm_c = jnp.concatenate([_block_diag(c_re), -_block_diag(c_im)], axis=-2).astype(BF16)

    row = lambda a: a[:, None, :]
    return dict(
        norm_g=row(p["norm_g"]), whead=whead, wmix=wmix, qn=row(p["mla_q_norm"]), wqb=wqb,
        kvn=row(p["mla_kv_norm"]), wk=wk, wv=wv,
        gqn=row(jnp.tile(p["gqa_q_norm"], (1, GQA_HEADS))), gkn=row(jnp.tile(p["gqa_k_norm"], (1, GQA_KV_HEADS))),
        gmat=gmat, egv=egv,
        lam_parts=jnp.stack([p["diff_lq1"], p["diff_lk1"], p["diff_lq2"], p["diff_lk2"]], axis=1),
        subln=row(p["diff_subln"]),
        ssm_b=ssm_b, ssm_a=ssm_a, ssm_c=ssm_c, ssm_d=row(p["ssm_d"]),
        wg=wg, wm=wm, wglu=p["ssm_glu_w"].astype(BF16), bglu=row(p["ssm_glu_b"]),
        wbr=p["w_branch_out"].astype(BF16), wout=p["w_out"].astype(BF16),
    )


def _mixers(pr, b, t, l, w, lam_init, s0):
    q, k, v, dq, dk, dv, gq, gk, gv, u = pr[:10]
    three = lambda a: a.reshape(b, a.shape[0] // b, a.shape[-1])
    k, v, dk, dv, gk, gv = (three(a) for a in (k, v, dk, dv, gk, gv))
    mla_heads = tuple((hd, hd) for hd in range(MLA_HEADS))
    gqa_heads = tuple((hd // GQA_GROUP, 2 * (hd // GQA_GROUP) + hd % 2) for hd in range(GQA_HEADS))
    diff_specs = (_layer_spec((4, DIFF_HEAD_DIM), l), _layer_spec((1, DIFF_V), l))
    diff_extra = (w["lam_parts"], w["subln"])
    if s0 is None:
        o_a, o_b, o_c = _attention_short(lam_init, mla_heads, gqa_heads,
                                         (three(q), k, v, three(dq), dk, dv, three(gq), gk, gv),
                                         diff_specs, diff_extra)
    else:
        o_a = _attention(functools.partial(_attn_heads_kernel, mla_heads, MLA_HEAD_PAD), "attn_mla",
                         three(q), k, v)
        o_b = _attention(functools.partial(_diff_attn_kernel, lam_init), "attn_diff", three(dq), dk, dv,
                         extra_specs=diff_specs, extra=diff_extra)
        o_c = _attention(functools.partial(_attn_heads_kernel, gqa_heads, GQA_HEAD_DIM), "attn_gqa",
                         three(gq), gk, gv)
    n = b * t
    u_rows = u.reshape(n // SSM_SEQ, SSM_SEQ, SSM_WIDTH)
    if s0 is None:
        y_t, fin = _ssm(u_rows, l, w, None, True)
    else:
        (y_t,) = _ssm(u_rows, l, w, s0, False)
        fin = None
    y = jnp.swapaxes(y_t, 0, 1).reshape(n, SSM_WIDTH)
    flat = lambda a: a.reshape(n, 512)
    return flat(o_a), flat(o_b), flat(o_c), y, fin


def kernel(x_prompt, x_sample, cache_mla_ckv, cache_mla_krope, cache_diff_k, cache_diff_v, cache_gqa_k, cache_gqa_v, state_ssm, c, c_ctx, norm_g, w_mod, b_mod, w_in, mla_q_norm, w_mla_q_b, mla_kv_norm, w_mla_kv_b, diff_lq1, diff_lk1, diff_lq2, diff_lk2, diff_subln, gqa_q_norm, gqa_k_norm, ssm_a_re, ssm_a_im, ssm_log_dt, ssm_b_re, ssm_b_im, ssm_c_re, ssm_c_im, ssm_d, ssm_glu_w, ssm_glu_b, w_branch_out, w_out, final_norm):
    p = dict(norm_g=norm_g, mla_q_norm=mla_q_norm, w_mla_q_b=w_mla_q_b, mla_kv_norm=mla_kv_norm,
             w_mla_kv_b=w_mla_kv_b, diff_lq1=diff_lq1, diff_lk1=diff_lk1, diff_lq2=diff_lq2, diff_lk2=diff_lk2,
             diff_subln=diff_subln, gqa_q_norm=gqa_q_norm, gqa_k_norm=gqa_k_norm, ssm_a_re=ssm_a_re,
             ssm_a_im=ssm_a_im, ssm_log_dt=ssm_log_dt, ssm_b_re=ssm_b_re, ssm_b_im=ssm_b_im, ssm_c_re=ssm_c_re,
             ssm_c_im=ssm_c_im, ssm_d=ssm_d, ssm_glu_w=ssm_glu_w, ssm_glu_b=ssm_glu_b,
             w_branch_out=w_branch_out, w_out=w_out)
    bp, tp, _ = x_prompt.shape
    bs, ts, _ = x_sample.shape
    past = cache_mla_ckv.shape[2]
    assert tp == SSM_SEQ and ts % SSM_SEQ == 0 and ts // SSM_SEQ == V7X_SUBLANES
    rope = _rope_lane_tables(ts)
    w = _weights(w_in, p)

    cvec = jnp.zeros((V7X_SUBLANES, D_MODEL), F32).at[0].set(c_ctx).at[1:1 + bs].set(c)
    mod = _modulation(cvec, w_mod, b_mod)

    pad_lo = MLA_NOPE
    pad_hi = MLA_HEAD_PAD - MLA_NOPE - MLA_ROPE
    kpe_ctx = jnp.pad(cache_mla_krope, ((0, 0), (0, 0), (0, 0), (pad_lo, pad_hi)))
    flat_ctx = lambda a: a.reshape(bs, DEPTH, past, -1).astype(BF16)
    gv_both = _in_output_half(jnp.stack([cache_gqa_v, cache_gqa_v], axis=-2))
    dk_ctx, dv_ctx, gk_ctx, gv_ctx = (flat_ctx(a) for a in (cache_diff_k, cache_diff_v, cache_gqa_k, gv_both))
    s0_all = state_ssm.reshape(bs, DEPTH, 2, SSM_NBLK, SSM_GBLOCK * SSM_STATE, 2)
    s0_all = jnp.transpose(s0_all, (1, 2, 0, 3, 5, 4)).reshape(DEPTH, 2, bs, SSM_NBLK, 1, 2 * SSM_SBLK)

    xp = x_prompt.reshape(bp * tp, D_MODEL)
    xs = x_sample.reshape(bs * ts, D_MODEL)
    fn = final_norm[None]
    sides = ()
    states = []
    for l in range(DEPTH):
        lam_init = 0.8 - 0.6 * math.exp(-0.3 * l)
        last = l == DEPTH - 1
        sh, sc, ga = (mod[l, :, i * D_MODEL:(i + 1) * D_MODEL][:, None, :] for i in range(3))

        pr = _proj(xp, sh[0:1], sc[0:1], tp, l, w, prev_sides=sides)
        o_a, o_b, o_c, y, fin = _mixers(pr, bp, tp, l, w, lam_init, None)
        xp = _merge(xp, sh[0:1], sc[0:1], ga[0:1], tp, o_a, o_b, o_c, y, l, w, fn, last)
        sides = tuple(pr[10:])
        fin = fin.reshape(2, SSM_NBLK, bp, 2, SSM_GBLOCK, SSM_STATE)
        states.append(jnp.transpose(fin, (2, 0, 1, 4, 5, 3)).reshape(bp, 2, SSM_GROUPS, SSM_STATE, 2))

        pr = _proj(xs, sh[1:1 + bs], sc[1:1 + bs], ts, l, w, rope=rope,
                   cached=(cache_mla_ckv, kpe_ctx, dk_ctx, dv_ctx, gk_ctx, gv_ctx))
        o_a, o_b, o_c, y, _ = _mixers(pr, bs, ts, l, w, lam_init, s0_all)
        xs = _merge(xs, sh[1:1 + bs], sc[1:1 + bs], ga[1:1 + bs], ts, o_a, o_b, o_c, y, l, w, fn, last)

    ckv_s, kpe_t, dk_t, dv_s, gk_t, gv_t = sides
    untranspose = lambda a: jnp.swapaxes(a, 2, 3)
    return (xp.reshape(bp, tp, D_MODEL), xs.reshape(bs, ts, D_MODEL),
            ckv_s, untranspose(kpe_t),
            untranspose(dk_t).reshape(bp, DEPTH, tp, DIFF_HEADS, 2, DIFF_HEAD_DIM), dv_s,
            untranspose(gk_t).reshape(bp, DEPTH, tp, GQA_KV_HEADS, GQA_HEAD_DIM),
            untranspose(gv_t).reshape(bp, DEPTH, tp, GQA_KV_HEADS, GQA_HEAD_DIM),
            jnp.stack(states, axis=1))
```

```python
import functools
import math

import jax
import jax.numpy as jnp
from jax import lax
from jax.experimental import pallas as pl
from jax.experimental.pallas import tpu as pltpu

F32 = jnp.float32
BF16 = jnp.bfloat16

D_MODEL = 1024
DEPTH = 2
GRID_W = 64
ROPE_THETA = 10000.0
EPS = 1e-6

MLA_HEADS = 8
MLA_NOPE = 64
MLA_ROPE = 32
MLA_V = 64
MLA_Q_LORA = 256
MLA_KV_LORA = 128
MLA_SCALE = (MLA_NOPE + MLA_ROPE) ** -0.5
MLA_HEAD_PAD = 128
DIFF_HEADS = 4
DIFF_HEAD_DIM = 64
DIFF_V = 2 * DIFF_HEAD_DIM
DIFF_SCALE = DIFF_HEAD_DIM ** -0.5
GQA_HEADS = 8
GQA_KV_HEADS = 2
GQA_HEAD_DIM = 64
GQA_GROUP = GQA_HEADS // GQA_KV_HEADS
GQA_KV_WIDTH = GQA_KV_HEADS * GQA_HEAD_DIM
GQA_SCALE = GQA_HEAD_DIM ** -0.5
SSM_WIDTH = 512
SSM_GROUP = 16
SSM_GROUPS = SSM_WIDTH // SSM_GROUP
SSM_STATE = 64
N_BRANCH = 4
BRANCH_WIDTH = 512

IN_SPLITS = (MLA_Q_LORA, MLA_KV_LORA, MLA_ROPE, 512, 512, 512, 512, 128, 128, SSM_WIDTH,
             N_BRANCH * BRANCH_WIDTH, N_BRANCH * D_MODEL)
D_IN = sum(IN_SPLITS)

V7X_LANES = 128
V7X_SUBLANES = 8
V7X_VMEM_LIMIT_BYTES = 56 * 1024 * 1024

_H_QA = (0, 256)
_H_KVA = (256, 384)
_H_KPE = (384, 512)
_H_WIDTH = 512
_M_DQ = (0, 512)
_M_DK = (512, 1024)
_M_DV = (1024, 1536)
_M_GQ = (1536, 2048)
_M_GK = (2048, 2176)
_M_GV = (2176, 2304)
_M_U = (2304, 2816)
_M_WIDTH = 2816
_IN_MIX_START = MLA_Q_LORA + MLA_KV_LORA + MLA_ROPE
_IN_GATE_START = _IN_MIX_START + _M_WIDTH
_IN_MERGE_START = _IN_GATE_START + N_BRANCH * BRANCH_WIDTH

SSM_GBLOCK = 8
SSM_NBLK = SSM_GROUPS // SSM_GBLOCK
SSM_SBLK = SSM_GBLOCK * SSM_STATE
SSM_UBLK = SSM_GBLOCK * SSM_GROUP
SSM_SEQ = 256
SSM_SLICE = 32

V_PAD = 128

PROJ_TM = 512
MERGE_TM = 512
ATTN_TQ_LONG = 512
ATTN_SEQ_PER_STEP = 2
ATTN_GROUP_LONG = 2


def _params(n_axes, vmem=V7X_VMEM_LIMIT_BYTES):
    return pltpu.CompilerParams(dimension_semantics=("arbitrary",) * n_axes, vmem_limit_bytes=vmem)


def _const_spec(shape):
    zeros = (0,) * len(shape)
    return pl.BlockSpec(shape, lambda *_: zeros, pipeline_mode=pl.Buffered(1))


def _layer_spec(shape, l):
    zeros = (0,) * len(shape)
    return pl.BlockSpec((None,) + tuple(shape), lambda *_: (l,) + zeros, pipeline_mode=pl.Buffered(1))


def _mod_spec(n_rows, tm, seq):
    if n_rows == 1:
        return pl.BlockSpec((1, 1, D_MODEL), lambda i: (0, 0, 0))
    return pl.BlockSpec((1, 1, D_MODEL), lambda i: (i * tm // seq, 0, 0))


def _dot(a, b):
    return jnp.dot(a, b, preferred_element_type=F32)


def _dot_nt(a, b):
    return lax.dot_general(a, b, (((1,), (1,)), ((), ())), preferred_element_type=F32)


def _rms(x):
    return x * lax.rsqrt(jnp.mean(x * x, axis=-1, keepdims=True) + EPS)


def _sigmoid(x):
    return 0.5 * jnp.tanh(0.5 * x) + 0.5


def _gelu_tanh(x):
    return 0.5 * x * (1.0 + jnp.tanh(math.sqrt(2.0 / math.pi) * (x + 0.044715 * (x * x * x))))


def _group_mean_sq(x, gmat):
    return _dot((x * x).astype(BF16), gmat)


def _rope_chunk(x, cos, sin, half):
    lane = lax.broadcasted_iota(jnp.int32, x.shape, 1)
    first = (lane % (2 * half)) < half
    swapped = jnp.where(first, pltpu.roll(x, V7X_LANES - half, 1), pltpu.roll(x, half, 1))
    return x * cos + swapped * sin


def _transpose_cast_kernel(w_ref, o_ref):
    o_ref[...] = w_ref[...].T.astype(BF16)


def _transposed_rows(w_t, row0, n_rows, blk):
    return pl.pallas_call(
        _transpose_cast_kernel,
        grid=(DEPTH, n_rows // blk),
        in_specs=[pl.BlockSpec((pl.Element(blk), pl.Element(D_MODEL)),
                               lambda l, j: (pl.multiple_of(l * D_IN + row0 + j * blk, V7X_SUBLANES), 0))],
        out_specs=pl.BlockSpec((None, D_MODEL, blk), lambda l, j: (l, 0, j)),
        out_shape=jax.ShapeDtypeStruct((DEPTH, D_MODEL, n_rows), BF16),
        compiler_params=_params(2),
        name="w_in_columns",
    )(w_t)


def _mod_kernel(c_ref, w_ref, b_ref, o_ref):
    c = c_ref[...]
    a = (c * _sigmoid(c)).astype(BF16)
    o_ref[0] = _dot(a, w_ref[0].astype(BF16)) + b_ref[0]


def _modulation(cvec, w_mod, b_mod):
    tn = 512
    return pl.pallas_call(
        _mod_kernel,
        grid=(DEPTH, 3 * D_MODEL // tn),
        in_specs=[pl.BlockSpec((V7X_SUBLANES, D_MODEL), lambda l, j: (0, 0)),
                  pl.BlockSpec((1, D_MODEL, tn), lambda l, j: (l, 0, j)),
                  pl.BlockSpec((1, 1, tn), lambda l, j: (l, 0, j))],
        out_specs=pl.BlockSpec((1, V7X_SUBLANES, tn), lambda l, j: (l, 0, j)),
        out_shape=jax.ShapeDtypeStruct((DEPTH, V7X_SUBLANES, 3 * D_MODEL), F32),
        compiler_params=_params(2),
        name="modulation",
    )(cvec, w_mod, b_mod.reshape(DEPTH, 1, 3 * D_MODEL))


def _modulated(x, g_ref, sh_ref, sc_ref):
    return (_rms(x) * g_ref[...]) * (1.0 + sc_ref[0]) + sh_ref[0]


def _proj_kernel(latent, n_prev, first_layer, *refs):
    (x_ref, sh_ref, sc_ref, g_ref, wh_ref, wx_ref, qn_ref, wqb_ref, kvn_ref, wk_ref, wv_ref,
     gqn_ref, gkn_ref, gm_ref, egv_ref) = refs[:15]
    pos = 15
    if latent:
        c_mla, s_mla, c_64, s_64 = refs[pos:pos + 4]
        cached = refs[pos + 4:pos + 10]
        pos += 10
    pos += n_prev
    (q_o, k_o, v_o, dq_o, dk_o, dv_o, gq_o, gk_o, gv_o, u_o) = refs[pos:pos + 10]
    pos += 10
    if not latent:
        ckv_f, kpe_t, dk_t, dv_f, gk_t, gv_t = refs[pos:pos + 6]

    def mla_keys_values(ckv_b, kpe):
        kn = _dot(ckv_b, wk_ref[...])
        for hd in range(MLA_HEADS):
            sl = slice(hd * MLA_HEAD_PAD, (hd + 1) * MLA_HEAD_PAD)
            k_o[:, sl] = (kn[:, sl] + kpe).astype(BF16)
        v_o[...] = _dot(ckv_b, wv_ref[...]).astype(BF16)

    def cached_tokens():
        ckv_c, kpe_t, dk_t, dv_c, gk_t, gv_t = cached
        mla_keys_values(ckv_c[...].astype(BF16), kpe_t[...].T)
        dk_o[...] = dk_t[...].T.astype(BF16)
        for hd in range(DIFF_HEADS):
            dv_o[:, hd * DIFF_V:(hd + 1) * DIFF_V] = dv_c[:, hd, :].astype(BF16)
        gk_o[...] = gk_t[...].T.astype(BF16)
        gv_o[...] = _dot(gv_t[...].T.astype(BF16), egv_ref[...]).astype(BF16)

    def new_tokens():
        _proj_new_tokens(latent, first_layer, refs[:15], (c_mla, s_mla, c_64, s_64) if latent else None,
                         refs[pos - 10:pos], refs[pos:pos + 6] if not latent else None, mla_keys_values)

    if latent:
        pl.when(pl.program_id(1) == 0)(cached_tokens)
        pl.when(pl.program_id(1) > 0)(new_tokens)
    else:
        new_tokens()


def _proj_new_tokens(latent, first_layer, in_refs, rope_refs, out_refs, side_refs, mla_keys_values):
    (x_ref, sh_ref, sc_ref, g_ref, wh_ref, wx_ref, qn_ref, wqb_ref, kvn_ref, wk_ref, wv_ref,
     gqn_ref, gkn_ref, gm_ref, egv_ref) = in_refs
    (q_o, k_o, v_o, dq_o, dk_o, dv_o, gq_o, gk_o, gv_o, u_o) = out_refs
    if latent:
        c_mla, s_mla, c_64, s_64 = rope_refs
    else:
        ckv_f, kpe_t, dk_t, dv_f, gk_t, gv_t = side_refs

    h = _modulated(x_ref[...], g_ref, sh_ref, sc_ref).astype(BF16)
    tm = h.shape[0]

    p_head = _dot(h, wh_ref[...])
    p_mix = _dot(h, wx_ref[...])

    def head(cols):
        return p_head[:, cols[0]:cols[1]]

    def mix(cols):
        return p_mix[:, cols[0]:cols[1]]

    def rope_mla(x):
        return _rope_chunk(x, c_mla[...], s_mla[...], MLA_ROPE // 2) if latent else x

    def rope_64(x):
        if not latent:
            return x
        c, s = c_64[...], s_64[...]
        chunks = [_rope_chunk(x[:, i:i + V7X_LANES], c, s, DIFF_HEAD_DIM // 2)
                  for i in range(0, x.shape[1], V7X_LANES)]
        return chunks[0] if len(chunks) == 1 else jnp.concatenate(chunks, axis=1)

    def per_sequence(x):
        return [x[b * SSM_SEQ:(b + 1) * SSM_SEQ] for b in range(tm // SSM_SEQ)]

    def slot(b):
        return (b,) if first_layer is None else (b, first_layer)

    def zero_other_slots(ref):
        if first_layer is not None:
            for b in range(ref.shape[0]):
                for other in range(ref.shape[1]):
                    if other != first_layer:
                        ref[b, other] = jnp.zeros(ref.shape[2:], ref.dtype)

    qn = (_rms(head(_H_QA)) * qn_ref[...]).astype(BF16)
    q = _dot(qn, wqb_ref[...])
    for hd in range(MLA_HEADS):
        sl = slice(hd * MLA_HEAD_PAD, (hd + 1) * MLA_HEAD_PAD)
        q_o[:, sl] = (rope_mla(q[:, sl]) * MLA_SCALE).astype(BF16)
    ckv = _rms(head(_H_KVA)) * kvn_ref[...]
    kpe = head(_H_KPE)
    if not latent:
        for b, (c_b, k_b) in enumerate(zip(per_sequence(ckv), per_sequence(kpe))):
            ckv_f[slot(b)] = c_b
            kpe_t[slot(b)] = k_b.T[MLA_NOPE:MLA_NOPE + MLA_ROPE, :]
        zero_other_slots(ckv_f)
        zero_other_slots(kpe_t)
    mla_keys_values(ckv.astype(BF16), rope_mla(kpe))

    dq_o[...] = (rope_64(mix(_M_DQ)) * DIFF_SCALE).astype(BF16)
    dk = rope_64(mix(_M_DK))
    dv = mix(_M_DV)
    dk_o[...] = dk.astype(BF16)
    dv_o[...] = dv.astype(BF16)

    gq = mix(_M_GQ)
    gq = gq * lax.rsqrt(_group_mean_sq(gq, gm_ref[...]) + EPS) * gqn_ref[...]
    gq_o[...] = (rope_64(gq) * GQA_SCALE).astype(BF16)
    gk = mix(_M_GK)
    kw = GQA_KV_WIDTH
    gk = gk * lax.rsqrt(_group_mean_sq(gk, gm_ref[:kw, :kw]) + EPS) * gkn_ref[...]
    gv = mix(_M_GV)
    if not latent:
        for b, (dk_b, dv_b, gk_b, gv_b) in enumerate(zip(*(per_sequence(a) for a in (dk, dv, gk, gv)))):
            dk_t[slot(b)] = dk_b.T
            for hd in range(DIFF_HEADS):
                dv_f[slot(b) + (slice(None), hd, slice(None))] = dv_b[:, hd * DIFF_V:(hd + 1) * DIFF_V]
            gk_t[slot(b)] = gk_b.T
            gv_t[slot(b)] = gv_b.T
        for ref in (dk_t, dv_f, gk_t, gv_t):
            zero_other_slots(ref)
    gk_o[...] = rope_64(gk).astype(BF16)
    gv_o[...] = _dot(gv.astype(BF16), egv_ref[...]).astype(BF16)

    u_o[...] = mix(_M_U)


def _side_shapes(n_seq):
    return [((n_seq, DEPTH, SSM_SEQ, MLA_KV_LORA), (SSM_SEQ, MLA_KV_LORA)),
            ((n_seq, DEPTH, MLA_ROPE, SSM_SEQ), (MLA_ROPE, SSM_SEQ)),
            ((n_seq, DEPTH, 512, SSM_SEQ), (512, SSM_SEQ)),
            ((n_seq, DEPTH, SSM_SEQ, DIFF_HEADS, DIFF_V), (SSM_SEQ, DIFF_HEADS, DIFF_V)),
            ((n_seq, DEPTH, GQA_KV_WIDTH, SSM_SEQ), (GQA_KV_WIDTH, SSM_SEQ)),
            ((n_seq, DEPTH, GQA_KV_WIDTH, SSM_SEQ), (GQA_KV_WIDTH, SSM_SEQ))]


def _proj(x, shift, scale, seq, l, w, rope=None, cached=None, prev_sides=()):
    n = x.shape[0]
    tm = PROJ_TM
    latent = cached is not None
    kw = GQA_KV_WIDTH
    if latent:
        tiles = seq // tm
        assert cached[0].shape[2] == tm
        grid = (n // seq, 1 + tiles)
        row = lambda b, j: (b * tiles + jnp.maximum(j - 1, 0), 0)
        kv_row = lambda b, j: (b * (1 + tiles) + j, 0)
        n_kv = (n // seq) * (cached[0].shape[2] + seq)
        mod_spec = pl.BlockSpec((1, 1, D_MODEL), lambda b, j: (b, 0, 0))
    else:
        grid = (n // tm,)
        row = kv_row = lambda i: (i, 0)
        n_kv = n
        mod_spec = _mod_spec(shift.shape[0], tm, seq)
    in_specs = [pl.BlockSpec((tm, D_MODEL), row), mod_spec, mod_spec,
                _layer_spec((1, D_MODEL), l), _layer_spec((D_MODEL, _H_WIDTH), l), _layer_spec((D_MODEL, _M_WIDTH), l),
                _layer_spec((1, MLA_Q_LORA), l), _layer_spec((MLA_Q_LORA, MLA_HEADS * MLA_HEAD_PAD), l),
                _layer_spec((1, MLA_KV_LORA), l), _layer_spec((MLA_KV_LORA, MLA_HEADS * MLA_HEAD_PAD), l),
                _layer_spec((MLA_KV_LORA, MLA_HEADS * V_PAD), l),
                _layer_spec((1, 512), l), _layer_spec((1, kw), l), _const_spec((512, 512)),
                _const_spec((kw, 2 * GQA_KV_HEADS * V_PAD))]
    args = [x, shift, scale, w["norm_g"], w["whead"], w["wmix"], w["qn"], w["wqb"], w["kvn"], w["wk"], w["wv"],
            w["gqn"], w["gkn"], w["gmat"], w["egv"]]
    if latent:
        tab = pl.BlockSpec((tm, V7X_LANES), lambda b, j: (jnp.maximum(j - 1, 0), 0))
        in_specs += [tab] * 4
        args += list(rope)
        in_specs += [pl.BlockSpec((None, None) + c.shape[2:], lambda b, j, nd=c.ndim: (b, l) + (0,) * (nd - 2))
                     for c in cached]
        args += list(cached)
    n_fixed = len(args)
    in_specs += [pl.BlockSpec(memory_space=pl.ANY)] * len(prev_sides)
    args += list(prev_sides)
    widths = [(1024, BF16, False), (1024, BF16, True), (MLA_HEADS * V_PAD, BF16, True), (512, BF16, False),
              (512, BF16, True), (512, BF16, True), (512, BF16, False), (kw, BF16, True),
              (2 * GQA_KV_HEADS * V_PAD, BF16, True), (512, F32, False)]
    out_specs = [pl.BlockSpec((tm, wd), kv_row if kv else row) for wd, _, kv in widths]
    out_shape = [jax.ShapeDtypeStruct((n_kv if kv else n, wd), dt) for wd, dt, kv in widths]
    aliases = {}
    first_layer = None
    if not latent:
        per_tile = tm // SSM_SEQ
        first_layer = None if prev_sides else l
        for full, blk in _side_shapes(n // SSM_SEQ):
            zeros = (0,) * len(blk)
            if prev_sides:
                out_specs.append(pl.BlockSpec((per_tile, None) + blk, lambda i, zeros=zeros: (i, l) + zeros))
            else:
                out_specs.append(pl.BlockSpec((per_tile, DEPTH) + blk, lambda i, zeros=zeros: (i, 0) + zeros))
            out_shape.append(jax.ShapeDtypeStruct(full, F32))
        aliases = {n_fixed + k: len(widths) + k for k in range(len(prev_sides))}
    return pl.pallas_call(
        functools.partial(_proj_kernel, latent, len(prev_sides), first_layer),
        grid=grid,
        in_specs=in_specs, out_specs=out_specs, out_shape=out_shape,
        input_output_aliases=aliases,
        compiler_params=_params(len(grid)),
        name="proj_latent" if latent else "proj_context",
    )(*args)


def _softmax_maps(maps):
    scores = [_dot_nt(q, k) for q, k, _ in maps]
    tops = [jnp.max(s, axis=-1, keepdims=True) for s in scores]
    weights = [jnp.exp(s - m).astype(BF16) for s, m in zip(scores, tops)]
    outs = []
    for p, (_, _, v) in zip(weights, maps):
        ones = jnp.ones((v.shape[0], V7X_LANES), BF16)
        o = _dot(p, jnp.concatenate([v, ones], axis=1))
        outs.append((o[:, :V7X_LANES], o[:, V7X_LANES:]))
    return outs


def _head_maps(heads, q_width, q_ref, k_ref, v_ref, part):
    return [(q_ref[b, :, hd * q_width:(hd + 1) * q_width],
             k_ref[b, :, heads[hd][0] * q_width:(heads[hd][0] + 1) * q_width],
             v_ref[b, :, heads[hd][1] * V_PAD:(heads[hd][1] + 1) * V_PAD]) for b, hd in part]


def _head_outputs(outs, part, o_ref):
    terms = [o * (1.0 / total) for o, total in outs]
    for (b, hd), low, high in zip(part[0::2], terms[0::2], terms[1::2]):
        o_ref[b, :, (hd // 2) * V7X_LANES:(hd // 2 + 1) * V7X_LANES] = (low + high).astype(BF16)


def _diff_maps(q_ref, k_ref, v_ref, part):
    d = DIFF_HEAD_DIM
    return [(q_ref[b, :, c * d:(c + 1) * d], k_ref[b, :, c * d:(c + 1) * d],
             v_ref[b, :, hd * DIFF_V:(hd + 1) * DIFF_V])
            for b, hd in part for c in (2 * hd, 2 * hd + 1)]


def _diff_lambda(lp_ref, lam_init):
    lp = lp_ref[...]
    return (jnp.exp(jnp.sum(lp[0:1] * lp[1:2], axis=-1, keepdims=True))
            - jnp.exp(jnp.sum(lp[2:3] * lp[3:4], axis=-1, keepdims=True)) + lam_init)


def _diff_outputs(outs, part, lam, lam_init, sub_ref, o_ref):
    mixed = [o1 * (1.0 / t1) - o2 * (lam / t2) for (o1, t1), (o2, t2) in zip(outs[0::2], outs[1::2])]
    normed = [_rms(o) * sub_ref[...] * (1.0 - lam_init) for o in mixed]
    for (b, hd), o in zip(part, normed):
        o_ref[b, :, hd * DIFF_V:(hd + 1) * DIFF_V] = o.astype(BF16)


def _items(q_ref, n_heads):
    return [(b, hd) for b in range(q_ref.shape[0]) for hd in range(n_heads)]


def _attn_heads_kernel(heads, q_width, q_ref, k_ref, v_ref, o_ref):
    items = _items(q_ref, len(heads))
    group = ATTN_GROUP_LONG
    for g0 in range(0, len(items), group):
        part = items[g0:g0 + group]
        _head_outputs(_softmax_maps(_head_maps(heads, q_width, q_ref, k_ref, v_ref, part)), part, o_ref)


def _diff_attn_kernel(lam_init, q_ref, k_ref, v_ref, lp_ref, sub_ref, o_ref):
    lam = _diff_lambda(lp_ref, lam_init)
    items = _items(q_ref, DIFF_HEADS)
    group = ATTN_GROUP_LONG // 2
    for g0 in range(0, len(items), group):
        part = items[g0:g0 + group]
        _diff_outputs(_softmax_maps(_diff_maps(q_ref, k_ref, v_ref, part)), part, lam, lam_init, sub_ref, o_ref)


def _attn_short_kernel(lam_init, mla_heads, gqa_heads, qa_ref, ka_ref, va_ref, qb_ref, kb_ref, vb_ref,
                       qc_ref, kc_ref, vc_ref, lp_ref, sub_ref, oa_ref, ob_ref, oc_ref):
    part_a, part_b, part_c = _items(qa_ref, len(mla_heads)), _items(qb_ref, DIFF_HEADS), _items(qc_ref, len(gqa_heads))
    maps_a = _head_maps(mla_heads, MLA_HEAD_PAD, qa_ref, ka_ref, va_ref, part_a)
    maps_b = _diff_maps(qb_ref, kb_ref, vb_ref, part_b)
    maps_c = _head_maps(gqa_heads, GQA_HEAD_DIM, qc_ref, kc_ref, vc_ref, part_c)
    outs = _softmax_maps(maps_a + maps_b + maps_c)
    n_a, n_b = len(maps_a), len(maps_b)
    _head_outputs(outs[:n_a], part_a, oa_ref)
    _diff_outputs(outs[n_a:n_a + n_b], part_b, _diff_lambda(lp_ref, lam_init), lam_init, sub_ref, ob_ref)
    _head_outputs(outs[n_a + n_b:], part_c, oc_ref)


def _attention_short(lam_init, mla_heads, gqa_heads, qkv, extra_specs, extra):
    b, t, _ = qkv[0].shape
    nb = ATTN_SEQ_PER_STEP
    whole = lambda i: (i, 0, 0)
    in_specs = [pl.BlockSpec((nb, t, a.shape[2]), whole) for a in qkv] + list(extra_specs)
    return pl.pallas_call(
        functools.partial(_attn_short_kernel, lam_init, mla_heads, gqa_heads),
        grid=(b // nb,),
        in_specs=in_specs,
        out_specs=[pl.BlockSpec((nb, t, 512), whole)] * 3,
        out_shape=[jax.ShapeDtypeStruct((b, t, 512), BF16)] * 3,
        compiler_params=_params(1),
        name="attn_context",
    )(*qkv, *extra)


def _attention(kernel, name, q, k, v, extra_specs=(), extra=()):
    b, t, wq = q.shape
    s = k.shape[1]
    tq = ATTN_TQ_LONG
    in_specs = [pl.BlockSpec((1, tq, wq), lambda i, j: (i, j, 0)),
                pl.BlockSpec((1, s, k.shape[2]), lambda i, j: (i, 0, 0)),
                pl.BlockSpec((1, s, v.shape[2]), lambda i, j: (i, 0, 0))]
    in_specs += list(extra_specs)
    return pl.pallas_call(
        kernel,
        grid=(b, t // tq),
        in_specs=in_specs,
        out_specs=pl.BlockSpec((1, tq, 512), lambda i, j: (i, j, 0)),
        out_shape=jax.ShapeDtypeStruct((b, t, 512), BF16),
        compiler_params=_params(2),
        name=name,
    )(q, k, v, *extra)


def _ssm_kernel(chunked, want_final, *refs):
    u_ref, bm_ref, ab_ref, cm_ref, d_ref = refs[:5]
    pos = 5
    if chunked:
        s0_ref = refs[pos]
        pos += 1
    y_ref = refs[pos]
    pos += 1
    if want_final:
        fin_ref = refs[pos]
        pos += 1
    hs_ref, yacc_ref, ut_ref = refs[pos:pos + 3]

    seq, rows, sb = SSM_SEQ, V7X_SUBLANES, SSM_SBLK
    n_slices = seq // SSM_SLICE
    slice_rows = SSM_SLICE * rows

    def visited(dr, stage):
        return stage if dr == 0 else n_slices - 1 - stage

    def rows_of(k):
        return slice(k * slice_rows, (k + 1) * slice_rows)

    def u_rows(k, first_visit):
        if first_visit:
            steps = range(k * SSM_SLICE, (k + 1) * SSM_SLICE)
            ut_ref[rows_of(k), :] = jnp.concatenate([u_ref[:, t, :] for t in steps], axis=0)
        return ut_ref[rows_of(k), :]

    def input_proj(stage):
        for dr in range(2):
            k = visited(dr, stage)
            hs_ref[dr, rows_of(k), :] = _dot(u_rows(k, stage < n_slices // 2).astype(BF16), bm_ref[dr, 0])

    def output_proj(stage):
        for dr in range(2):
            r = rows_of(visited(dr, stage))
            part = _dot(hs_ref[dr, r, :].astype(BF16), cm_ref[dr, 0])
            if stage < n_slices // 2:
                yacc_ref[r, :] = part
            else:
                yacc_ref[r, :] += part

    a_parts = []
    for dr in range(2):
        a = ab_ref[dr, 0]
        a_parts.append((jnp.broadcast_to(a[:, :sb], (rows, sb)), jnp.broadcast_to(a[:, sb:], (rows, sb))))

    def step_rows(stage, step):
        i = stage * SSM_SLICE + step
        return [slice(t * rows, (t + 1) * rows) for t in (i, seq - 1 - i)]

    def scan_slice(stage, carry):
        for step in range(SSM_SLICE):
            new = []
            for dr, sl in enumerate(step_rows(stage, step)):
                ar, ai = a_parts[dr]
                hr, hi = carry[2 * dr], carry[2 * dr + 1]
                b = hs_ref[dr, sl, :]
                nr = ar * hr - ai * hi + b[:, :sb]
                ni = ar * hi + ai * hr + b[:, sb:]
                hs_ref[dr, sl, :sb] = nr
                hs_ref[dr, sl, sb:] = ni
                new += [nr, ni]
            carry = tuple(new)
        return carry

    def fix_slice(stage, carry):
        for step in range(SSM_SLICE):
            new = []
            for dr, sl in enumerate(step_rows(stage, step)):
                ar, ai = a_parts[dr]
                zr, zi = carry[2 * dr], carry[2 * dr + 1]
                nr = ar * zr - ai * zi
                ni = ar * zi + ai * zr
                hs_ref[dr, sl, :sb] += nr
                hs_ref[dr, sl, sb:] += ni
                new += [nr, ni]
            carry = tuple(new)
        return carry

    zero = jnp.zeros((rows, sb), F32)
    fin = (zero, zero, zero, zero)
    input_proj(0)
    for stage in range(n_slices):
        if stage + 1 < n_slices:
            input_proj(stage + 1)
        fin = scan_slice(stage, fin)
        if not chunked and stage >= 1:
            output_proj(stage - 1)

    if want_final:
        for dr in range(2):
            fin_ref[dr, 0, :, :sb] = fin[2 * dr]
            fin_ref[dr, 0, :, sb:] = fin[2 * dr + 1]

    if chunked:
        entry = []
        for dr in range(2):
            ar, ai = a_parts[dr]
            pr, pi = ar[0:1], ai[0:1]
            for _ in range(int(math.log2(seq))):
                pr, pi = pr * pr - pi * pi, 2.0 * pr * pi
            fr, fi = fin[2 * dr], fin[2 * dr + 1]
            s0 = s0_ref[dr, 0, 0]
            er, ei = s0[:, :sb], s0[:, sb:]
            order = range(rows) if dr == 0 else range(rows - 1, -1, -1)
            rows_r, rows_i = [None] * rows, [None] * rows
            for c in order:
                rows_r[c], rows_i[c] = er, ei
                er, ei = (pr * er - pi * ei + fr[c:c + 1], pr * ei + pi * er + fi[c:c + 1])
            entry += [jnp.concatenate(rows_r, axis=0), jnp.concatenate(rows_i, axis=0)]
        carry = tuple(entry)
        for stage in range(n_slices):
            carry = fix_slice(stage, carry)
            if stage >= 1:
                output_proj(stage - 1)
    output_proj(n_slices - 1)

    y = _gelu_tanh(yacc_ref[...] + d_ref[...] * ut_ref[...])
    y_ref[...] = y.reshape(seq, rows, SSM_UBLK)


def _ssm(u, l, w, s0, want_final):
    r, seq, _ = u.shape
    chunked = s0 is not None
    rows = V7X_SUBLANES
    in_specs = [pl.BlockSpec((rows, seq, SSM_UBLK), lambda i, j: (i, 0, j)),
                pl.BlockSpec((None, 2, 1, SSM_UBLK, 2 * SSM_SBLK), lambda i, j: (l, 0, j, 0, 0)),
                pl.BlockSpec((None, 2, 1, 1, 2 * SSM_SBLK), lambda i, j: (l, 0, j, 0, 0)),
                pl.BlockSpec((None, 2, 1, 2 * SSM_SBLK, SSM_UBLK), lambda i, j: (l, 0, j, 0, 0)),
                pl.BlockSpec((None, 1, SSM_UBLK), lambda i, j: (l, 0, j))]
    args = [u, w["ssm_b"], w["ssm_a"], w["ssm_c"], w["ssm_d"]]
    if chunked:
        in_specs.append(pl.BlockSpec((None, 2, 1, 1, 1, 2 * SSM_SBLK), lambda i, j: (l, 0, i, j, 0, 0)))
        args.append(s0)
    out_specs = [pl.BlockSpec((seq, rows, SSM_UBLK), lambda i, j: (0, i, j))]
    out_shape = [jax.ShapeDtypeStruct((seq, r, SSM_WIDTH), F32)]
    if want_final:
        out_specs.append(pl.BlockSpec((2, 1, rows, 2 * SSM_SBLK), lambda i, j: (0, j, i, 0)))
        out_shape.append(jax.ShapeDtypeStruct((2, SSM_NBLK, r, 2 * SSM_SBLK), F32))
    return pl.pallas_call(
        functools.partial(_ssm_kernel, chunked, want_final),
        grid=(r // rows, SSM_NBLK),
        in_specs=in_specs, out_specs=out_specs, out_shape=out_shape,
        scratch_shapes=[pltpu.VMEM((2, seq * rows, 2 * SSM_SBLK), F32), pltpu.VMEM((seq * rows, SSM_UBLK), F32),
                        pltpu.VMEM((seq * rows, SSM_UBLK), F32)],
        compiler_params=_params(2),
        name="ssm_latent" if chunked else "ssm_context",
    )(*args)


def _merge_kernel(last, x_ref, sh_ref, sc_ref, ga_ref, g_ref, oa_ref, ob_ref, oc_ref, y_ref,
                  wg_ref, wm_ref, wglu_ref, bglu_ref, wbr_ref, wout_ref, fn_ref, o_ref):
    x = x_ref[...]
    h = _modulated(x, g_ref, sh_ref, sc_ref).astype(BF16)
    y = y_ref[...]
    o_d = y * _sigmoid(_dot(y.astype(BF16), wglu_ref[...]) + bglu_ref[...])
    branches = (oa_ref[...].astype(F32), ob_ref[...].astype(F32), oc_ref[...].astype(F32), o_d)
    acc = None
    for n, o in enumerate(branches):
        gate = _dot(h, wg_ref[:, n * BRANCH_WIDTH:(n + 1) * BRANCH_WIDTH])
        br = _dot((o * (gate * _sigmoid(gate))).astype(BF16), wbr_ref[n])
        term = _sigmoid(_dot(h, wm_ref[:, n * D_MODEL:(n + 1) * D_MODEL])) * br
        acc = term if acc is None else acc + term
    xn = x + ga_ref[0] * _dot(acc.astype(BF16), wout_ref[...])
    if last:
        xn = _rms(xn) * fn_ref[...]
    o_ref[...] = xn


def _merge(x, shift, scale, gate, seq, oa, ob, oc, y, l, w, final_norm, last):
    n = x.shape[0]
    tm = MERGE_TM
    row = lambda i: (i, 0)
    mod_spec = _mod_spec(shift.shape[0], tm, seq)
    in_specs = [pl.BlockSpec((tm, D_MODEL), row), mod_spec, mod_spec, mod_spec, _layer_spec((1, D_MODEL), l),
                pl.BlockSpec((tm, 512), row), pl.BlockSpec((tm, 512), row), pl.BlockSpec((tm, 512), row),
                pl.BlockSpec((tm, 512), row),
                _layer_spec((D_MODEL, N_BRANCH * BRANCH_WIDTH), l), _layer_spec((D_MODEL, N_BRANCH * D_MODEL), l),
                _layer_spec((SSM_WIDTH, SSM_WIDTH), l), _layer_spec((1, SSM_WIDTH), l),
                _layer_spec((N_BRANCH, BRANCH_WIDTH, D_MODEL), l), _layer_spec((D_MODEL, D_MODEL), l),
                _const_spec((1, D_MODEL))]
    return pl.pallas_call(
        functools.partial(_merge_kernel, last),
        grid=(n // tm,),
        in_specs=in_specs,
        out_specs=pl.BlockSpec((tm, D_MODEL), row),
        out_shape=jax.ShapeDtypeStruct((n, D_MODEL), F32),
        compiler_params=_params(1),
        name="merge",
    )(x, shift, scale, gate, w["norm_g"], oa, ob, oc, y, w["wg"], w["wm"], w["wglu"], w["bglu"],
      w["wbr"], w["wout"], final_norm)


def _rope_tables(n_tok, rot_dim):
    rows = n_tok // GRID_W
    row = jnp.repeat(jnp.arange(rows, dtype=F32), GRID_W)
    col = jnp.tile(jnp.arange(GRID_W, dtype=F32), rows)
    quarter = rot_dim // 4
    inv = ROPE_THETA ** (-jnp.arange(quarter, dtype=F32) / quarter)
    ang = jnp.concatenate([row[:, None] * inv, col[:, None] * inv], axis=-1)
    return jnp.cos(ang), jnp.sin(ang)


def _rope_lane_tables(n_tok):
    c, s = _rope_tables(n_tok, MLA_ROPE)
    ones = jnp.ones((n_tok, MLA_NOPE), F32)
    pad = MLA_HEAD_PAD - MLA_NOPE - MLA_ROPE
    c_mla = jnp.concatenate([ones, c, c, jnp.ones((n_tok, pad), F32)], axis=1)
    s_mla = jnp.concatenate([0.0 * ones, -s, s, jnp.zeros((n_tok, pad), F32)], axis=1)
    c, s = _rope_tables(n_tok, DIFF_HEAD_DIM)
    c_64 = jnp.concatenate([c, c, c, c], axis=1)
    s_64 = jnp.concatenate([-s, s, -s, s], axis=1)
    return c_mla, s_mla, c_64, s_64


def _block_diag(blocks):
    g = blocks.shape[-3]
    eye = jnp.eye(g, dtype=blocks.dtype)
    out = jnp.einsum("...grc,gh->...grhc", blocks, eye)
    return out.reshape(blocks.shape[:-3] + (g * blocks.shape[-2], g * blocks.shape[-1]))


def _in_output_half(v):
    zeros = jnp.zeros_like(v[..., 0, :])
    return jnp.stack([jnp.concatenate([v[..., 0, :], zeros], axis=-1),
                      jnp.concatenate([zeros, v[..., 1, :]], axis=-1)], axis=-2)


def _weights(w_in, p):
    w_t = jnp.swapaxes(w_in, 1, 2).reshape(DEPTH * D_IN, D_MODEL)
    wmix = _transposed_rows(w_t, _IN_MIX_START, _M_WIDTH, _M_WIDTH // 2)
    wg = _transposed_rows(w_t, _IN_GATE_START, N_BRANCH * BRANCH_WIDTH, 1024)
    wm = _transposed_rows(w_t, _IN_MERGE_START, N_BRANCH * D_MODEL, 1024)
    zeros = lambda wd: jnp.zeros((DEPTH, D_MODEL, wd), F32)
    kpe_lo = MLA_Q_LORA + MLA_KV_LORA
    whead = jnp.concatenate([w_in[:, :, :kpe_lo], zeros(MLA_NOPE), w_in[:, :, kpe_lo:_IN_MIX_START],
                             zeros(MLA_HEAD_PAD - MLA_NOPE - MLA_ROPE)], axis=2).astype(BF16)

    hq = MLA_NOPE + MLA_ROPE
    wqb = p["w_mla_q_b"].reshape(DEPTH, MLA_Q_LORA, MLA_HEADS, hq)
    wqb = jnp.pad(wqb, ((0, 0), (0, 0), (0, 0), (0, MLA_HEAD_PAD - hq))).reshape(DEPTH, MLA_Q_LORA, -1).astype(BF16)
    wkv = p["w_mla_kv_b"].reshape(DEPTH, MLA_KV_LORA, MLA_HEADS, MLA_NOPE + MLA_V)
    wk = jnp.pad(wkv[..., :MLA_NOPE], ((0, 0), (0, 0), (0, 0), (0, MLA_HEAD_PAD - MLA_NOPE)))
    wk = wk.reshape(DEPTH, MLA_KV_LORA, -1).astype(BF16)
    wv = _in_output_half(wkv[..., MLA_NOPE:].reshape(DEPTH, MLA_KV_LORA, MLA_HEADS // 2, 2, MLA_V))
    wv = wv.reshape(DEPTH, MLA_KV_LORA, -1).astype(BF16)
    eye = jnp.eye(GQA_HEAD_DIM, dtype=F32)
    egv = _block_diag(jnp.broadcast_to(
        _in_output_half(jnp.stack([eye, eye], axis=1)).reshape(GQA_HEAD_DIM, 2 * V_PAD),
        (GQA_KV_HEADS, GQA_HEAD_DIM, 2 * V_PAD))).astype(BF16)

    gmat =_block_diag(jnp.full((GQA_HEADS, GQA_HEAD_DIM, GQA_HEAD_DIM), 1.0 / GQA_HEAD_DIM, F32)).astype(BF16)

    lam = lax.complex(p["ssm_a_re"], p["ssm_a_im"])
    dt = jnp.exp(p["ssm_log_dt"])[..., None]
    abar = jnp.exp(lam * dt)
    bbar = ((abar - 1.0) / lam)[..., None] * lax.complex(p["ssm_b_re"], p["ssm_b_im"])
    blk = lambda a: a.reshape((DEPTH, 2, SSM_NBLK, SSM_GBLOCK) + a.shape[3:])
    b_t = jnp.swapaxes(blk(bbar), -1, -2)
    ssm_b = jnp.concatenate([_block_diag(jnp.real(b_t)), _block_diag(jnp.imag(b_t))], axis=-1).astype(BF16)
    ab = blk(abar).reshape(DEPTH, 2, SSM_NBLK, 1, SSM_SBLK)
    ssm_a = jnp.concatenate([jnp.real(ab), jnp.imag(ab)], axis=-1)
    c_re = jnp.swapaxes(blk(p["ssm_c_re"]), -1, -2)
    c_im = jnp.swapaxes(blk(p["ssm_c_im"]), -1, -2)
    ssm_c = jnp.concatenate([_block_diag(c_re), -_block_diag(c_im)], axis=-2).astype(BF16)

    row = lambda a: a[:, None, :]
    return dict(
        norm_g=row(p["norm_g"]), whead=whead, wmix=wmix, qn=row(p["mla_q_norm"]), wqb=wqb,
        kvn=row(p["mla_kv_norm"]), wk=wk, wv=wv,
        gqn=row(jnp.tile(p["gqa_q_norm"], (1, GQA_HEADS))), gkn=row(jnp.tile(p["gqa_k_norm"], (1, GQA_KV_HEADS))),
        gmat=gmat, egv=egv,
        lam_parts=jnp.stack([p["diff_lq1"], p["diff_lk1"], p["diff_lq2"], p["diff_lk2"]], axis=1),
        subln=row(p["diff_subln"]),
        ssm_b=ssm_b, ssm_a=ssm_a, ssm_c=ssm_c, ssm_d=row(p["ssm_d"]),
        wg=wg, wm=wm, wglu=p["ssm_glu_w"].astype(BF16), bglu=row(p["ssm_glu_b"]),
        wbr=p["w_branch_out"].astype(BF16), wout=p["w_out"].astype(BF16),
    )


def _mixers(pr, b, t, l, w, lam_init, s0):
    q, k, v, dq, dk, dv, gq, gk, gv, u = pr[:10]
    three = lambda a: a.reshape(b, a.shape[0] // b, a.shape[-1])
    k, v, dk, dv, gk, gv = (three(a) for a in (k, v, dk, dv, gk, gv))
    mla_heads = tuple((hd, hd) for hd in range(MLA_HEADS))
    gqa_heads = tuple((hd // GQA_GROUP, 2 * (hd // GQA_GROUP) + hd % 2) for hd in range(GQA_HEADS))
    diff_specs = (_layer_spec((4, DIFF_HEAD_DIM), l), _layer_spec((1, DIFF_V), l))
    diff_extra = (w["lam_parts"], w["subln"])
    if s0 is None:
        o_a, o_b, o_c = _attention_short(lam_init, mla_heads, gqa_heads,
                                         (three(q), k, v, three(dq), dk, dv, three(gq), gk, gv),
                                         diff_specs, diff_extra)
    else:
        o_a = _attention(functools.partial(_attn_heads_kernel, mla_heads, MLA_HEAD_PAD), "attn_mla",
                         three(q), k, v)
        o_b = _attention(functools.partial(_diff_attn_kernel, lam_init), "attn_diff", three(dq), dk, dv,
                         extra_specs=diff_specs, extra=diff_extra)
        o_c = _attention(functools.partial(_attn_heads_kernel, gqa_heads, GQA_HEAD_DIM), "attn_gqa",
                         three(gq), gk, gv)
    n = b * t
    u_rows = u.reshape(n // SSM_SEQ, SSM_SEQ, SSM_WIDTH)
    if s0 is None:
        y_t, fin = _ssm(u_rows, l, w, None, True)
    else:
        (y_t,) = _ssm(u_rows, l, w, s0, False)
        fin = None
    y = jnp.swapaxes(y_t, 0, 1).reshape(n, SSM_WIDTH)
    flat = lambda a: a.reshape(n, 512)
    return flat(o_a), flat(o_b), flat(o_c), y, fin


def kernel(x_prompt, x_sample, cache_mla_ckv, cache_mla_krope, cache_diff_k, cache_diff_v, cache_gqa_k, cache_gqa_v, state_ssm, c, c_ctx, norm_g, w_mod, b_mod, w_in, mla_q_norm, w_mla_q_b, mla_kv_norm, w_mla_kv_b, diff_lq1, diff_lk1, diff_lq2, diff_lk2, diff_subln, gqa_q_norm, gqa_k_norm, ssm_a_re, ssm_a_im, ssm_log_dt, ssm_b_re, ssm_b_im, ssm_c_re, ssm_c_im, ssm_d, ssm_glu_w, ssm_glu_b, w_branch_out, w_out, final_norm):
    p = dict(norm_g=norm_g, mla_q_norm=mla_q_norm, w_mla_q_b=w_mla_q_b, mla_kv_norm=mla_kv_norm,
             w_mla_kv_b=w_mla_kv_b, diff_lq1=diff_lq1, diff_lk1=diff_lk1, diff_lq2=diff_lq2, diff_lk2=diff_lk2,
             diff_subln=diff_subln, gqa_q_norm=gqa_q_norm, gqa_k_norm=gqa_k_norm, ssm_a_re=ssm_a_re,
             ssm_a_im=ssm_a_im, ssm_log_dt=ssm_log_dt, ssm_b_re=ssm_b_re, ssm_b_im=ssm_b_im, ssm_c_re=ssm_c_re,
             ssm_c_im=ssm_c_im, ssm_d=ssm_d, ssm_glu_w=ssm_glu_w, ssm_glu_b=ssm_glu_b,
             w_branch_out=w_branch_out, w_out=w_out)
    bp, tp, _ = x_prompt.shape
    bs, ts, _ = x_sample.shape
    past = cache_mla_ckv.shape[2]
    assert tp == SSM_SEQ and ts % SSM_SEQ == 0 and ts // SSM_SEQ == V7X_SUBLANES
    rope = _rope_lane_tables(ts)
    w = _weights(w_in, p)

    cvec = jnp.zeros((V7X_SUBLANES, D_MODEL), F32).at[0].set(c_ctx).at[1:1 + bs].set(c)
    mod = _modulation(cvec, w_mod, b_mod)

    feature_major = lambda a: jnp.moveaxis(a, 2, -1).reshape(bs, DEPTH, -1, past)
    pad_lo = MLA_NOPE
    pad_hi = MLA_HEAD_PAD - MLA_NOPE - MLA_ROPE
    kpe_ctx = jnp.pad(feature_major(cache_mla_krope), ((0, 0), (0, 0), (pad_lo, pad_hi), (0, 0)))
    dk_ctx, gk_ctx, gv_ctx = (feature_major(a) for a in (cache_diff_k, cache_gqa_k, cache_gqa_v))
    dv_ctx = cache_diff_v
    s0_all = state_ssm.reshape(bs, DEPTH, 2, SSM_NBLK, SSM_GBLOCK * SSM_STATE, 2)
    s0_all = jnp.transpose(s0_all, (1, 2, 0, 3, 5, 4)).reshape(DEPTH, 2, bs, SSM_NBLK, 1, 2 * SSM_SBLK)

    xp = x_prompt.reshape(bp * tp, D_MODEL)
    xs = x_sample.reshape(bs * ts, D_MODEL)
    fn = final_norm[None]
    sides = ()
    states = []
    for l in range(DEPTH):
        lam_init = 0.8 - 0.6 * math.exp(-0.3 * l)
        last = l == DEPTH - 1
        sh, sc, ga = (mod[l, :, i * D_MODEL:(i + 1) * D_MODEL][:, None, :] for i in range(3))

        pr = _proj(xp, sh[0:1], sc[0:1], tp, l, w, prev_sides=sides)
        o_a, o_b, o_c, y, fin = _mixers(pr, bp, tp, l, w, lam_init, None)
        xp = _merge(xp, sh[0:1], sc[0:1], ga[0:1], tp, o_a, o_b, o_c, y, l, w, fn, last)
        sides = tuple(pr[10:])
        fin = fin.reshape(2, SSM_NBLK, bp, 2, SSM_GBLOCK, SSM_STATE)
        states.append(jnp.transpose(fin, (2, 0, 1, 4, 5, 3)).reshape(bp, 2, SSM_GROUPS, SSM_STATE, 2))

        pr = _proj(xs, sh[1:1 + bs], sc[1:1 + bs], ts, l, w, rope=rope,
                   cached=(cache_mla_ckv, kpe_ctx, dk_ctx, dv_ctx, gk_ctx, gv_ctx))
        o_a, o_b, o_c, y, _ = _mixers(pr, bs, ts, l, w, lam_init, s0_all)
        xs = _merge(xs, sh[1:1 + bs], sc[1:1 + bs], ga[1:1 + bs], ts, o_a, o_b, o_c, y, l, w, fn, last)

    ckv_s, kpe_t, dk_t, dv_s, gk_t, gv_t = sides
    untranspose = lambda a: jnp.swapaxes(a, 2, 3)
    return (xp.reshape(bp, tp, D_MODEL), xs.reshape(bs, ts, D_MODEL),
            ckv_s, untranspose(kpe_t),
            untranspose(dk_t).reshape(bp, DEPTH, tp, DIFF_HEADS, 2, DIFF_HEAD_DIM), dv_s,
            untranspose(gk_t).reshape(bp, DEPTH, tp, GQA_KV_HEADS, GQA_HEAD_DIM),
            untranspose(gv_t).reshape(bp, DEPTH, tp, GQA_KV_HEADS, GQA_HEAD_DIM),
            jnp.stack(states, axis=1))
```

```python
import functools
import math

import jax
import jax.numpy as jnp
import numpy as np
from jax import lax
from jax.experimental import pallas as pl
from jax.experimental.pallas import tpu as pltpu

F32 = jnp.float32
BF16 = jnp.bfloat16

D_MODEL = 1024
DEPTH = 2
GRID_W = 64
ROPE_THETA = 10000.0
EPS = 1e-6

MLA_HEADS = 8
MLA_NOPE = 64
MLA_ROPE = 32
MLA_V = 64
MLA_Q_LORA = 256
MLA_KV_LORA = 128
MLA_SCALE = (MLA_NOPE + MLA_ROPE) ** -0.5
MLA_HEAD_PAD = 128
DIFF_HEADS = 4
DIFF_HEAD_DIM = 64
DIFF_V = 2 * DIFF_HEAD_DIM
DIFF_SCALE = DIFF_HEAD_DIM ** -0.5
GQA_HEADS = 8
GQA_KV_HEADS = 2
GQA_HEAD_DIM = 64
GQA_GROUP = GQA_HEADS // GQA_KV_HEADS
GQA_KV_WIDTH = GQA_KV_HEADS * GQA_HEAD_DIM
GQA_SCALE = GQA_HEAD_DIM ** -0.5
SSM_WIDTH = 512
SSM_GROUP = 16
SSM_GROUPS = SSM_WIDTH // SSM_GROUP
SSM_STATE = 64
N_BRANCH = 4
BRANCH_WIDTH = 512

IN_SPLITS = (MLA_Q_LORA, MLA_KV_LORA, MLA_ROPE, 512, 512, 512, 512, 128, 128, SSM_WIDTH,
             N_BRANCH * BRANCH_WIDTH, N_BRANCH * D_MODEL)
D_IN = sum(IN_SPLITS)

V7X_LANES = 128
V7X_SUBLANES = 8
V7X_VMEM_LIMIT_BYTES = 56 * 1024 * 1024

_H_QA = (0, 256)
_H_KVA = (256, 384)
_H_KPE = (384, 512)
_H_WIDTH = 512
_M_DQ = (0, 512)
_M_DK = (512, 1024)
_M_DV = (1024, 1536)
_M_GQ = (1536, 2048)
_M_GK = (2048, 2176)
_M_GV = (2176, 2304)
_M_U = (2304, 2816)
_M_WIDTH = 2816
_IN_MIX_START = MLA_Q_LORA + MLA_KV_LORA + MLA_ROPE
_IN_GATE_START = _IN_MIX_START + _M_WIDTH
_IN_MERGE_START = _IN_GATE_START + N_BRANCH * BRANCH_WIDTH

SSM_GBLOCK = 8
SSM_NBLK = SSM_GROUPS // SSM_GBLOCK
SSM_SBLK = SSM_GBLOCK * SSM_STATE
SSM_UBLK = SSM_GBLOCK * SSM_GROUP
SSM_SEQ = 256
SSM_SLICE = 32

V_PAD = 128

PROJ_TM = 512
MERGE_TM = 512
ATTN_TQ_LONG = 512
ATTN_SEQ_PER_STEP = 2
ATTN_GROUP_LONG = 2


def _params(n_axes, vmem=V7X_VMEM_LIMIT_BYTES):
    return pltpu.CompilerParams(dimension_semantics=("arbitrary",) * n_axes, vmem_limit_bytes=vmem)


def _const_spec(shape):
    zeros = (0,) * len(shape)
    return pl.BlockSpec(shape, lambda *_: zeros, pipeline_mode=pl.Buffered(1))


def _layer_spec(shape, l):
    zeros = (0,) * len(shape)
    return pl.BlockSpec((None,) + tuple(shape), lambda *_: (l,) + zeros, pipeline_mode=pl.Buffered(1))


def _mod_spec(n_rows, tm, seq):
    if n_rows == 1:
        return pl.BlockSpec((1, 1, D_MODEL), lambda i: (0, 0, 0))
    return pl.BlockSpec((1, 1, D_MODEL), lambda i: (i * tm // seq, 0, 0))


def _dot(a, b):
    return jnp.dot(a, b, preferred_element_type=F32)


def _dot_nt(a, b):
    return lax.dot_general(a, b, (((1,), (1,)), ((), ())), preferred_element_type=F32)


def _rms(x):
    return x * lax.rsqrt(jnp.mean(x * x, axis=-1, keepdims=True) + EPS)


def _sigmoid(x):
    return 0.5 * jnp.tanh(0.5 * x) + 0.5


def _gelu_tanh(x):
    return 0.5 * x * (1.0 + jnp.tanh(math.sqrt(2.0 / math.pi) * (x + 0.044715 * (x * x * x))))


def _group_mean_sq(x, gmat):
    return _dot((x * x).astype(BF16), gmat)


def _rope_chunk(x, cos, sin, half):
    lane = lax.broadcasted_iota(jnp.int32, x.shape, 1)
    first = (lane % (2 * half)) < half
    swapped = jnp.where(first, pltpu.roll(x, V7X_LANES - half, 1), pltpu.roll(x, half, 1))
    return x * cos + swapped * sin


def _transpose_cast_kernel(w_ref, o_ref):
    o_ref[...] = w_ref[...].T.astype(BF16)


def _transposed_rows(w_t, row0, n_rows, blk):
    return pl.pallas_call(
        _transpose_cast_kernel,
        grid=(DEPTH, n_rows // blk),
        in_specs=[pl.BlockSpec((pl.Element(blk), pl.Element(D_MODEL)),
                               lambda l, j: (pl.multiple_of(l * D_IN + row0 + j * blk, V7X_SUBLANES), 0))],
        out_specs=pl.BlockSpec((None, D_MODEL, blk), lambda l, j: (l, 0, j)),
        out_shape=jax.ShapeDtypeStruct((DEPTH, D_MODEL, n_rows), BF16),
        compiler_params=_params(2),
        name="w_in_columns",
    )(w_t)


def _mod_kernel(c_ref, w_ref, b_ref, o_ref):
    c = c_ref[...]
    a = (c * _sigmoid(c)).astype(BF16)
    o_ref[0] = _dot(a, w_ref[0].astype(BF16)) + b_ref[0]


def _modulation(cvec, w_mod, b_mod):
    tn = 512
    return pl.pallas_call(
        _mod_kernel,
        grid=(DEPTH, 3 * D_MODEL // tn),
        in_specs=[pl.BlockSpec((V7X_SUBLANES, D_MODEL), lambda l, j: (0, 0)),
                  pl.BlockSpec((1, D_MODEL, tn), lambda l, j: (l, 0, j)),
                  pl.BlockSpec((1, 1, tn), lambda l, j: (l, 0, j))],
        out_specs=pl.BlockSpec((1, V7X_SUBLANES, tn), lambda l, j: (l, 0, j)),
        out_shape=jax.ShapeDtypeStruct((DEPTH, V7X_SUBLANES, 3 * D_MODEL), F32),
        compiler_params=_params(2),
        name="modulation",
    )(cvec, w_mod, b_mod.reshape(DEPTH, 1, 3 * D_MODEL))


def _modulated(x, g_ref, sh_ref, sc_ref):
    return (_rms(x) * g_ref[...]) * (1.0 + sc_ref[0]) + sh_ref[0]


def _proj_kernel(latent, n_prev, first_layer, *refs):
    (x_ref, sh_ref, sc_ref, g_ref, wh_ref, wx_ref, qn_ref, wqb_ref, kvn_ref, wk_ref, wv_ref,
     gqn_ref, gkn_ref, gm_ref, egv_ref) = refs[:15]
    pos = 15
    if latent:
        c_mla, s_mla, c_64, s_64 = refs[pos:pos + 4]
        cached = refs[pos + 4:pos + 10]
        pos += 10
    pos += n_prev
    (q_o, k_o, v_o, dq_o, dk_o, dv_o, gq_o, gk_o, gv_o, u_o) = refs[pos:pos + 10]
    pos += 10
    if not latent:
        ckv_f, kpe_t, dk_t, dv_f, gk_t, gv_t = refs[pos:pos + 6]

    def mla_keys_values(ckv_b, kpe):
        kn = _dot(ckv_b, wk_ref[...])
        for hd in range(MLA_HEADS):
            sl = slice(hd * MLA_HEAD_PAD, (hd + 1) * MLA_HEAD_PAD)
            k_o[:, sl] = (kn[:, sl] + kpe).astype(BF16)
        v_o[...] = _dot(ckv_b, wv_ref[...]).astype(BF16)

    def cached_tokens():
        ckv_c, kpe_t, dk_t, dv_c, gk_t, gv_t = cached
        mla_keys_values(ckv_c[...].astype(BF16), kpe_t[...].T)
        dk_o[...] = dk_t[...].T.astype(BF16)
        for hd in range(DIFF_HEADS):
            dv_o[:, hd * DIFF_V:(hd + 1) * DIFF_V] = dv_c[:, hd, :].astype(BF16)
        gk_o[...] = gk_t[...].T.astype(BF16)
        gv_o[...] = _dot(gv_t[...].T.astype(BF16), egv_ref[...]).astype(BF16)

    def new_tokens():
        _proj_new_tokens(latent, first_layer, refs[:15], (c_mla, s_mla, c_64, s_64) if latent else None,
                         refs[pos - 10:pos], refs[pos:pos + 6] if not latent else None, mla_keys_values)

    if latent:
        pl.when(pl.program_id(1) == 0)(cached_tokens)
        pl.when(pl.program_id(1) > 0)(new_tokens)
    else:
        new_tokens()


def _proj_new_tokens(latent, first_layer, in_refs, rope_refs, out_refs, side_refs, mla_keys_values):
    (x_ref, sh_ref, sc_ref, g_ref, wh_ref, wx_ref, qn_ref, wqb_ref, kvn_ref, wk_ref, wv_ref,
     gqn_ref, gkn_ref, gm_ref, egv_ref) = in_refs
    (q_o, k_o, v_o, dq_o, dk_o, dv_o, gq_o, gk_o, gv_o, u_o) = out_refs
    if latent:
        c_mla, s_mla, c_64, s_64 = rope_refs
    else:
        ckv_f, kpe_t, dk_t, dv_f, gk_t, gv_t = side_refs

    h = _modulated(x_ref[...], g_ref, sh_ref, sc_ref).astype(BF16)
    tm = h.shape[0]

    p_head = _dot(h, wh_ref[...])
    p_mix = _dot(h, wx_ref[...])

    def head(cols):
        return p_head[:, cols[0]:cols[1]]

    def mix(cols):
        return p_mix[:, cols[0]:cols[1]]

    def rope_mla(x):
        return _rope_chunk(x, c_mla[...], s_mla[...], MLA_ROPE // 2) if latent else x

    def rope_64(x):
        if not latent:
            return x
        c, s = c_64[...], s_64[...]
        chunks = [_rope_chunk(x[:, i:i + V7X_LANES], c, s, DIFF_HEAD_DIM // 2)
                  for i in range(0, x.shape[1], V7X_LANES)]
        return chunks[0] if len(chunks) == 1 else jnp.concatenate(chunks, axis=1)

    def per_sequence(x):
        return [x[b * SSM_SEQ:(b + 1) * SSM_SEQ] for b in range(tm // SSM_SEQ)]

    def slot(b):
        return (b,) if first_layer is None else (b, first_layer)

    def zero_other_slots(ref):
        if first_layer is not None:
            for b in range(ref.shape[0]):
                for other in range(ref.shape[1]):
                    if other != first_layer:
                        ref[b, other] = jnp.zeros(ref.shape[2:], ref.dtype)

    qn = (_rms(head(_H_QA)) * qn_ref[...]).astype(BF16)
    q = _dot(qn, wqb_ref[...])
    for hd in range(MLA_HEADS):
        sl = slice(hd * MLA_HEAD_PAD, (hd + 1) * MLA_HEAD_PAD)
        q_o[:, sl] = (rope_mla(q[:, sl]) * MLA_SCALE).astype(BF16)
    ckv = _rms(head(_H_KVA)) * kvn_ref[...]
    kpe = head(_H_KPE)
    if not latent:
        for b, (c_b, k_b) in enumerate(zip(per_sequence(ckv), per_sequence(kpe))):
            ckv_f[slot(b)] = c_b
            kpe_t[slot(b)] = k_b.T[MLA_NOPE:MLA_NOPE + MLA_ROPE, :]
        zero_other_slots(ckv_f)
        zero_other_slots(kpe_t)
    mla_keys_values(ckv.astype(BF16), rope_mla(kpe))

    dq_o[...] = (rope_64(mix(_M_DQ)) * DIFF_SCALE).astype(BF16)
    dk = rope_64(mix(_M_DK))
    dv = mix(_M_DV)
    dk_o[...] = dk.astype(BF16)
    dv_o[...] = dv.astype(BF16)

    gq = mix(_M_GQ)
    gq = gq * lax.rsqrt(_group_mean_sq(gq, gm_ref[...]) + EPS) * gqn_ref[...]
    gq_o[...] = (rope_64(gq) * GQA_SCALE).astype(BF16)
    gk = mix(_M_GK)
    kw = GQA_KV_WIDTH
    gk = gk * lax.rsqrt(_group_mean_sq(gk, gm_ref[:kw, :kw]) + EPS) * gkn_ref[...]
    gv = mix(_M_GV)
    if not latent:
        for b, (dk_b, dv_b, gk_b, gv_b) in enumerate(zip(*(per_sequence(a) for a in (dk, dv, gk, gv)))):
            dk_t[slot(b)] = dk_b.T
            for hd in range(DIFF_HEADS):
                dv_f[slot(b) + (slice(None), hd, slice(None))] = dv_b[:, hd * DIFF_V:(hd + 1) * DIFF_V]
            gk_t[slot(b)] = gk_b.T
            gv_t[slot(b)] = gv_b.T
        for ref in (dk_t, dv_f, gk_t, gv_t):
            zero_other_slots(ref)
    gk_o[...] = rope_64(gk).astype(BF16)
    gv_o[...] = _dot(gv.astype(BF16), egv_ref[...]).astype(BF16)

    u_o[...] = mix(_M_U)


def _side_shapes(n_seq):
    return [((n_seq, DEPTH, SSM_SEQ, MLA_KV_LORA), (SSM_SEQ, MLA_KV_LORA)),
            ((n_seq, DEPTH, MLA_ROPE, SSM_SEQ), (MLA_ROPE, SSM_SEQ)),
            ((n_seq, DEPTH, 512, SSM_SEQ), (512, SSM_SEQ)),
            ((n_seq, DEPTH, SSM_SEQ, DIFF_HEADS, DIFF_V), (SSM_SEQ, DIFF_HEADS, DIFF_V)),
            ((n_seq, DEPTH, GQA_KV_WIDTH, SSM_SEQ), (GQA_KV_WIDTH, SSM_SEQ)),
            ((n_seq, DEPTH, GQA_KV_WIDTH, SSM_SEQ), (GQA_KV_WIDTH, SSM_SEQ))]


def _proj(x, shift, scale, seq, l, w, rope=None, cached=None, prev_sides=()):
    n = x.shape[0]
    tm = PROJ_TM
    latent = cached is not None
    kw = GQA_KV_WIDTH
    if latent:
        tiles = seq // tm
        assert cached[0].shape[2] == tm
        grid = (n // seq, 1 + tiles)
        row = lambda b, j: (b * tiles + jnp.maximum(j - 1, 0), 0)
        kv_row = lambda b, j: (b * (1 + tiles) + j, 0)
        n_kv = (n // seq) * (cached[0].shape[2] + seq)
        mod_spec = pl.BlockSpec((1, 1, D_MODEL), lambda b, j: (b, 0, 0))
    else:
        grid = (n // tm,)
        row = kv_row = lambda i: (i, 0)
        n_kv = n
        mod_spec = _mod_spec(shift.shape[0], tm, seq)
    in_specs = [pl.BlockSpec((tm, D_MODEL), row), mod_spec, mod_spec,
                _layer_spec((1, D_MODEL), l), _layer_spec((D_MODEL, _H_WIDTH), l), _layer_spec((D_MODEL, _M_WIDTH), l),
                _layer_spec((1, MLA_Q_LORA), l), _layer_spec((MLA_Q_LORA, MLA_HEADS * MLA_HEAD_PAD), l),
                _layer_spec((1, MLA_KV_LORA), l), _layer_spec((MLA_KV_LORA, MLA_HEADS * MLA_HEAD_PAD), l),
                _layer_spec((MLA_KV_LORA, MLA_HEADS * V_PAD), l),
                _layer_spec((1, 512), l), _layer_spec((1, kw), l), _const_spec((512, 512)),
                _const_spec((kw, 2 * GQA_KV_HEADS * V_PAD))]
    args = [x, shift, scale, w["norm_g"], w["whead"], w["wmix"], w["qn"], w["wqb"], w["kvn"], w["wk"], w["wv"],
            w["gqn"], w["gkn"], w["gmat"], w["egv"]]
    if latent:
        tab = pl.BlockSpec((tm, V7X_LANES), lambda b, j: (jnp.maximum(j - 1, 0), 0))
        in_specs += [tab] * 4
        args += list(rope)
        in_specs += [pl.BlockSpec((None, None) + c.shape[2:], lambda b, j, nd=c.ndim: (b, l) + (0,) * (nd - 2))
                     for c in cached]
        args += list(cached)
    n_fixed = len(args)
    in_specs += [pl.BlockSpec(memory_space=pl.ANY)] * len(prev_sides)
    args += list(prev_sides)
    widths = [(1024, BF16, False), (1024, BF16, True), (MLA_HEADS * V_PAD, BF16, True), (512, BF16, False),
              (512, BF16, True), (512, BF16, True), (512, BF16, False), (kw, BF16, True),
              (2 * GQA_KV_HEADS * V_PAD, BF16, True), (512, F32, False)]
    out_specs = [pl.BlockSpec((tm, wd), kv_row if kv else row) for wd, _, kv in widths]
    out_shape = [jax.ShapeDtypeStruct((n_kv if kv else n, wd), dt) for wd, dt, kv in widths]
    aliases = {}
    first_layer = None
    if not latent:
        per_tile = tm // SSM_SEQ
        first_layer = None if prev_sides else l
        for full, blk in _side_shapes(n // SSM_SEQ):
            zeros = (0,) * len(blk)
            if prev_sides:
                out_specs.append(pl.BlockSpec((per_tile, None) + blk, lambda i, zeros=zeros: (i, l) + zeros))
            else:
                out_specs.append(pl.BlockSpec((per_tile, DEPTH) + blk, lambda i, zeros=zeros: (i, 0) + zeros))
            out_shape.append(jax.ShapeDtypeStruct(full, F32))
        aliases = {n_fixed + k: len(widths) + k for k in range(len(prev_sides))}
    return pl.pallas_call(
        functools.partial(_proj_kernel, latent, len(prev_sides), first_layer),
        grid=grid,
        in_specs=in_specs, out_specs=out_specs, out_shape=out_shape,
        input_output_aliases=aliases,
        compiler_params=_params(len(grid)),
        name="proj_latent" if latent else "proj_context",
    )(*args)


def _softmax_maps(maps):
    scores = [_dot_nt(q, k) for q, k, _ in maps]
    tops = [jnp.max(s, axis=-1, keepdims=True) for s in scores]
    weights = [jnp.exp(s - m).astype(BF16) for s, m in zip(scores, tops)]
    outs = []
    for p, (_, _, v) in zip(weights, maps):
        ones = jnp.ones((v.shape[0], V7X_LANES), BF16)
        o = _dot(p, jnp.concatenate([v, ones], axis=1))
        outs.append((o[:, :V7X_LANES], o[:, V7X_LANES:]))
    return outs


def _head_maps(heads, q_width, q_ref, k_ref, v_ref, part):
    return [(q_ref[b, :, hd * q_width:(hd + 1) * q_width],
             k_ref[b, :, heads[hd][0] * q_width:(heads[hd][0] + 1) * q_width],
             v_ref[b, :, heads[hd][1] * V_PAD:(heads[hd][1] + 1) * V_PAD]) for b, hd in part]


def _head_outputs(outs, part, o_ref):
    terms = [o * (1.0 / total) for o, total in outs]
    for (b, hd), low, high in zip(part[0::2], terms[0::2], terms[1::2]):
        o_ref[b, :, (hd // 2) * V7X_LANES:(hd // 2 + 1) * V7X_LANES] = (low + high).astype(BF16)


def _diff_maps(q_ref, k_ref, v_ref, part):
    d = DIFF_HEAD_DIM
    return [(q_ref[b, :, c * d:(c + 1) * d], k_ref[b, :, c * d:(c + 1) * d],
             v_ref[b, :, hd * DIFF_V:(hd + 1) * DIFF_V])
            for b, hd in part for c in (2 * hd, 2 * hd + 1)]


def _diff_lambda(lp_ref, lam_init):
    lp = lp_ref[...]
    return (jnp.exp(jnp.sum(lp[0:1] * lp[1:2], axis=-1, keepdims=True))
            - jnp.exp(jnp.sum(lp[2:3] * lp[3:4], axis=-1, keepdims=True)) + lam_init)


def _diff_outputs(outs, part, lam, lam_init, sub_ref, o_ref):
    mixed = [o1 * (1.0 / t1) - o2 * (lam / t2) for (o1, t1), (o2, t2) in zip(outs[0::2], outs[1::2])]
    normed = [_rms(o) * sub_ref[...] * (1.0 - lam_init) for o in mixed]
    for (b, hd), o in zip(part, normed):
        o_ref[b, :, hd * DIFF_V:(hd + 1) * DIFF_V] = o.astype(BF16)


def _items(q_ref, n_heads):
    return [(b, hd) for b in range(q_ref.shape[0]) for hd in range(n_heads)]


def _attn_heads_kernel(heads, q_width, q_ref, k_ref, v_ref, o_ref):
    items = _items(q_ref, len(heads))
    group = ATTN_GROUP_LONG
    for g0 in range(0, len(items), group):
        part = items[g0:g0 + group]
        _head_outputs(_softmax_maps(_head_maps(heads, q_width, q_ref, k_ref, v_ref, part)), part, o_ref)


def _diff_attn_kernel(lam_init, q_ref, k_ref, v_ref, lp_ref, sub_ref, o_ref):
    lam = _diff_lambda(lp_ref, lam_init)
    items = _items(q_ref, DIFF_HEADS)
    group = ATTN_GROUP_LONG // 2
    for g0 in range(0, len(items), group):
        part = items[g0:g0 + group]
        _diff_outputs(_softmax_maps(_diff_maps(q_ref, k_ref, v_ref, part)), part, lam, lam_init, sub_ref, o_ref)


def _attn_short_kernel(lam_init, mla_heads, gqa_heads, qa_ref, ka_ref, va_ref, qb_ref, kb_ref, vb_ref,
                       qc_ref, kc_ref, vc_ref, lp_ref, sub_ref, oa_ref, ob_ref, oc_ref):
    part_a, part_b, part_c = _items(qa_ref, len(mla_heads)), _items(qb_ref, DIFF_HEADS), _items(qc_ref, len(gqa_heads))
    maps_a = _head_maps(mla_heads, MLA_HEAD_PAD, qa_ref, ka_ref, va_ref, part_a)
    maps_b = _diff_maps(qb_ref, kb_ref, vb_ref, part_b)
    maps_c = _head_maps(gqa_heads, GQA_HEAD_DIM, qc_ref, kc_ref, vc_ref, part_c)
    outs = _softmax_maps(maps_a + maps_b + maps_c)
    n_a, n_b = len(maps_a), len(maps_b)
    _head_outputs(outs[:n_a], part_a, oa_ref)
    _diff_outputs(outs[n_a:n_a + n_b], part_b, _diff_lambda(lp_ref, lam_init), lam_init, sub_ref, ob_ref)
    _head_outputs(outs[n_a + n_b:], part_c, oc_ref)


def _attention_short(lam_init, mla_heads, gqa_heads, qkv, extra_specs, extra):
    b, t, _ = qkv[0].shape
    nb = ATTN_SEQ_PER_STEP
    whole = lambda i: (i, 0, 0)
    in_specs = [pl.BlockSpec((nb, t, a.shape[2]), whole) for a in qkv] + list(extra_specs)
    return pl.pallas_call(
        functools.partial(_attn_short_kernel, lam_init, mla_heads, gqa_heads),
        grid=(b // nb,),
        in_specs=in_specs,
        out_specs=[pl.BlockSpec((nb, t, 512), whole)] * 3,
        out_shape=[jax.ShapeDtypeStruct((b, t, 512), BF16)] * 3,
        compiler_params=_params(1),
        name="attn_context",
    )(*qkv, *extra)


def _attention(kernel, name, q, k, v, extra_specs=(), extra=()):
    b, t, wq = q.shape
    s = k.shape[1]
    tq = ATTN_TQ_LONG
    in_specs = [pl.BlockSpec((1, tq, wq), lambda i, j: (i, j, 0)),
                pl.BlockSpec((1, s, k.shape[2]), lambda i, j: (i, 0, 0)),
                pl.BlockSpec((1, s, v.shape[2]), lambda i, j: (i, 0, 0))]
    in_specs += list(extra_specs)
    return pl.pallas_call(
        kernel,
        grid=(b, t // tq),
        in_specs=in_specs,
        out_specs=pl.BlockSpec((1, tq, 512), lambda i, j: (i, j, 0)),
        out_shape=jax.ShapeDtypeStruct((b, t, 512), BF16),
        compiler_params=_params(2),
        name=name,
    )(q, k, v, *extra)


def _ssm_kernel(chunked, want_final, *refs):
    u_ref, bg_ref, ab_ref, cg_ref, d_ref = refs[:5]
    pos = 5
    if chunked:
        s0_ref = refs[pos]
        pos += 1
    y_ref = refs[pos]
    pos += 1
    if want_final:
        fin_ref = refs[pos]
        pos += 1
    hs_ref, yacc_ref, ut_ref, bm_ref, cm_ref = refs[pos:pos + 5]

    @pl.when(pl.program_id(1) == 0)
    def _():
        bm_ref[...] = jnp.zeros(bm_ref.shape, bm_ref.dtype)
        cm_ref[...] = jnp.zeros(cm_ref.shape, cm_ref.dtype)
        for dr in range(2):
            for g in range(SSM_GBLOCK):
                ins = slice(g * SSM_GROUP, (g + 1) * SSM_GROUP)
                for part in range(2):
                    states = slice(part * SSM_SBLK + g * SSM_STATE, part * SSM_SBLK + (g + 1) * SSM_STATE)
                    bm_ref[dr, ins, states] = bg_ref[dr, 0, g, :, part * SSM_STATE:(part + 1) * SSM_STATE]
                    cm_ref[dr, states, ins] = cg_ref[dr, 0, g, part * SSM_STATE:(part + 1) * SSM_STATE, :]

    seq, rows, sb = SSM_SEQ, V7X_SUBLANES, SSM_SBLK
    n_slices = seq // SSM_SLICE
    slice_rows = SSM_SLICE * rows

    def visited(dr, stage):
        return stage if dr == 0 else n_slices - 1 - stage

    def rows_of(k):
        return slice(k * slice_rows, (k + 1) * slice_rows)

    def u_rows(k, first_visit):
        if first_visit:
            steps = range(k * SSM_SLICE, (k + 1) * SSM_SLICE)
            ut_ref[rows_of(k), :] = jnp.concatenate([u_ref[:, t, :] for t in steps], axis=0)
        return ut_ref[rows_of(k), :]

    def input_proj(stage):
        for dr in range(2):
            k = visited(dr, stage)
            hs_ref[dr, rows_of(k), :] = _dot(u_rows(k, stage < n_slices // 2).astype(BF16), bm_ref[dr])

    def output_proj(stage):
        for dr in range(2):
            r = rows_of(visited(dr, stage))
            part = _dot(hs_ref[dr, r, :].astype(BF16), cm_ref[dr])
            if stage < n_slices // 2:
                yacc_ref[r, :] = part
            else:
                yacc_ref[r, :] += part

    a_parts = []
    for dr in range(2):
        a = ab_ref[dr, 0]
        a_parts.append((jnp.broadcast_to(a[:, :sb], (rows, sb)), jnp.broadcast_to(a[:, sb:], (rows, sb))))

    def step_rows(stage, step):
        i = stage * SSM_SLICE + step
        return [slice(t * rows, (t + 1) * rows) for t in (i, seq - 1 - i)]

    def scan_slice(stage, carry):
        for step in range(SSM_SLICE):
            new = []
            for dr, sl in enumerate(step_rows(stage, step)):
                ar, ai = a_parts[dr]
                hr, hi = carry[2 * dr], carry[2 * dr + 1]
                b = hs_ref[dr, sl, :]
                nr = ar * hr - ai * hi + b[:, :sb]
                ni = ar * hi + ai * hr + b[:, sb:]
                hs_ref[dr, sl, :sb] = nr
                hs_ref[dr, sl, sb:] = ni
                new += [nr, ni]
            carry = tuple(new)
        return carry

    def fix_slice(stage, carry):
        for step in range(SSM_SLICE):
            new = []
            for dr, sl in enumerate(step_rows(stage, step)):
                ar, ai = a_parts[dr]
                zr, zi = carry[2 * dr], carry[2 * dr + 1]
                nr = ar * zr - ai * zi
                ni = ar * zi + ai * zr
                hs_ref[dr, sl, :sb] += nr
                hs_ref[dr, sl, sb:] += ni
                new += [nr, ni]
            carry = tuple(new)
        return carry

    zero = jnp.zeros((rows, sb), F32)
    fin = (zero, zero, zero, zero)
    input_proj(0)
    for stage in range(n_slices):
        if stage + 1 < n_slices:
            input_proj(stage + 1)
        fin = scan_slice(stage, fin)
        if not chunked and stage >= 1:
            output_proj(stage - 1)

    if want_final:
        for dr in range(2):
            fin_ref[dr, 0, :, :sb] = fin[2 * dr]
            fin_ref[dr, 0, :, sb:] = fin[2 * dr + 1]

    if chunked:
        entry = []
        for dr in range(2):
            ar, ai = a_parts[dr]
            pr, pi = ar[0:1], ai[0:1]
            for _ in range(int(math.log2(seq))):
                pr, pi = pr * pr - pi * pi, 2.0 * pr * pi
            fr, fi = fin[2 * dr], fin[2 * dr + 1]
            s0 = s0_ref[dr, 0, 0]
            er, ei = s0[:, :sb], s0[:, sb:]
            order = range(rows) if dr == 0 else range(rows - 1, -1, -1)
            rows_r, rows_i = [None] * rows, [None] * rows
            for c in order:
                rows_r[c], rows_i[c] = er, ei
                er, ei = (pr * er - pi * ei + fr[c:c + 1], pr * ei + pi * er + fi[c:c + 1])
            entry += [jnp.concatenate(rows_r, axis=0), jnp.concatenate(rows_i, axis=0)]
        carry = tuple(entry)
        for stage in range(n_slices):
            carry = fix_slice(stage, carry)
            if stage >= 1:
                output_proj(stage - 1)
    output_proj(n_slices - 1)

    y = _gelu_tanh(yacc_ref[...] + d_ref[...] * ut_ref[...])
    y_ref[...] = y.reshape(seq, rows, SSM_UBLK)


def _ssm(u, l, w, s0, want_final):
    r, seq, _ = u.shape
    chunked = s0 is not None
    rows = V7X_SUBLANES
    in_specs = [pl.BlockSpec((rows, seq, SSM_UBLK), lambda j, i: (i, 0, j)),
                pl.BlockSpec((None, 2, 1, SSM_GBLOCK, SSM_GROUP, 2 * SSM_STATE), lambda j, i: (l, 0, j, 0, 0, 0)),
                pl.BlockSpec((None, 2, 1, 1, 2 * SSM_SBLK), lambda j, i: (l, 0, j, 0, 0)),
                pl.BlockSpec((None, 2, 1, SSM_GBLOCK, 2 * SSM_STATE, SSM_GROUP), lambda j, i: (l, 0, j, 0, 0, 0)),
                pl.BlockSpec((None, 1, SSM_UBLK), lambda j, i: (l, 0, j))]
    args = [u, w["ssm_b"], w["ssm_a"], w["ssm_c"], w["ssm_d"]]
    if chunked:
        in_specs.append(pl.BlockSpec((None, 2, 1, 1, 1, 2 * SSM_SBLK), lambda j, i: (l, 0, i, j, 0, 0)))
        args.append(s0)
    out_specs = [pl.BlockSpec((seq, rows, SSM_UBLK), lambda j, i: (0, i, j))]
    out_shape = [jax.ShapeDtypeStruct((seq, r, SSM_WIDTH), F32)]
    if want_final:
        out_specs.append(pl.BlockSpec((2, 1, rows, 2 * SSM_SBLK), lambda j, i: (0, j, i, 0)))
        out_shape.append(jax.ShapeDtypeStruct((2, SSM_NBLK, r, 2 * SSM_SBLK), F32))
    return pl.pallas_call(
        functools.partial(_ssm_kernel, chunked, want_final),
        grid=(SSM_NBLK, r // rows),
        in_specs=in_specs, out_specs=out_specs, out_shape=out_shape,
        scratch_shapes=[pltpu.VMEM((2, seq * rows, 2 * SSM_SBLK), F32), pltpu.VMEM((seq * rows, SSM_UBLK), F32),
                        pltpu.VMEM((seq * rows, SSM_UBLK), F32),
                        pltpu.VMEM((2, SSM_UBLK, 2 * SSM_SBLK), BF16), pltpu.VMEM((2, 2 * SSM_SBLK, SSM_UBLK), BF16)],
        compiler_params=_params(2),
        name="ssm_latent" if chunked else "ssm_context",
    )(*args)


def _merge_kernel(last, x_ref, sh_ref, sc_ref, ga_ref, g_ref, oa_ref, ob_ref, oc_ref, y_ref,
                  wg_ref, wm_ref, wglu_ref, bglu_ref, wbr_ref, wout_ref, fn_ref, o_ref):
    x = x_ref[...]
    h = _modulated(x, g_ref, sh_ref, sc_ref).astype(BF16)
    y = y_ref[...]
    o_d = y * _sigmoid(_dot(y.astype(BF16), wglu_ref[...]) + bglu_ref[...])
    branches = (oa_ref[...].astype(F32), ob_ref[...].astype(F32), oc_ref[...].astype(F32), o_d)
    acc = None
    for n, o in enumerate(branches):
        gate = _dot(h, wg_ref[:, n * BRANCH_WIDTH:(n + 1) * BRANCH_WIDTH])
        br = _dot((o * (gate * _sigmoid(gate))).astype(BF16), wbr_ref[n])
        term = _sigmoid(_dot(h, wm_ref[:, n * D_MODEL:(n + 1) * D_MODEL])) * br
        acc = term if acc is None else acc + term
    xn = x + ga_ref[0] * _dot(acc.astype(BF16), wout_ref[...])
    if last:
        xn = _rms(xn) * fn_ref[...]
    o_ref[...] = xn


def _merge(x, shift, scale, gate, seq, oa, ob, oc, y, l, w, final_norm, last):
    n = x.shape[0]
    tm = MERGE_TM
    row = lambda i: (i, 0)
    mod_spec = _mod_spec(shift.shape[0], tm, seq)
    in_specs = [pl.BlockSpec((tm, D_MODEL), row), mod_spec, mod_spec, mod_spec, _layer_spec((1, D_MODEL), l),
                pl.BlockSpec((tm, 512), row), pl.BlockSpec((tm, 512), row), pl.BlockSpec((tm, 512), row),
                pl.BlockSpec((tm, 512), row),
                _layer_spec((D_MODEL, N_BRANCH * BRANCH_WIDTH), l), _layer_spec((D_MODEL, N_BRANCH * D_MODEL), l),
                _layer_spec((SSM_WIDTH, SSM_WIDTH), l), _layer_spec((1, SSM_WIDTH), l),
                _layer_spec((N_BRANCH, BRANCH_WIDTH, D_MODEL), l), _layer_spec((D_MODEL, D_MODEL), l),
                _const_spec((1, D_MODEL))]
    return pl.pallas_call(
        functools.partial(_merge_kernel, last),
        grid=(n // tm,),
        in_specs=in_specs,
        out_specs=pl.BlockSpec((tm, D_MODEL), row),
        out_shape=jax.ShapeDtypeStruct((n, D_MODEL), F32),
        compiler_params=_params(1),
        name="merge",
    )(x, shift, scale, gate, w["norm_g"], oa, ob, oc, y, w["wg"], w["wm"], w["wglu"], w["bglu"],
      w["wbr"], w["wout"], final_norm)


def _rope_tables(n_tok, rot_dim):
    f32 = np.float32
    rows = n_tok // GRID_W
    row = np.repeat(np.arange(rows, dtype=f32), GRID_W)
    col = np.tile(np.arange(GRID_W, dtype=f32), rows)
    quarter = rot_dim // 4
    inv = (f32(ROPE_THETA) ** (-np.arange(quarter, dtype=f32) / f32(quarter))).astype(f32)
    ang = np.concatenate([row[:, None] * inv, col[:, None] * inv], axis=-1).astype(f32)
    return np.cos(ang).astype(f32), np.sin(ang).astype(f32)


def _rope_lane_tables(n_tok):
    c, s = _rope_tables(n_tok, MLA_ROPE)
    ones = np.ones((n_tok, MLA_NOPE), np.float32)
    pad = MLA_HEAD_PAD - MLA_NOPE - MLA_ROPE
    c_mla = np.concatenate([ones, c, c, np.ones((n_tok, pad), np.float32)], axis=1)
    s_mla = np.concatenate([0.0 * ones, -s, s, np.zeros((n_tok, pad), np.float32)], axis=1)
    c, s = _rope_tables(n_tok, DIFF_HEAD_DIM)
    c_64 = np.concatenate([c, c, c, c], axis=1)
    s_64 = np.concatenate([-s, s, -s, s], axis=1)
    return tuple(jnp.asarray(a, F32) for a in (c_mla, s_mla, c_64, s_64))


def _in_output_half(v):
    zeros = jnp.zeros_like(v[..., 0, :])
    return jnp.stack([jnp.concatenate([v[..., 0, :], zeros], axis=-1),
                      jnp.concatenate([zeros, v[..., 1, :]], axis=-1)], axis=-2)


def _weights(w_in, p):
    w_t = jnp.swapaxes(w_in, 1, 2).reshape(DEPTH * D_IN, D_MODEL)
    wmix = _transposed_rows(w_t, _IN_MIX_START, _M_WIDTH, _M_WIDTH // 2)
    wg = _transposed_rows(w_t, _IN_GATE_START, N_BRANCH * BRANCH_WIDTH, 1024)
    wm = _transposed_rows(w_t, _IN_MERGE_START, N_BRANCH * D_MODEL, 1024)
    zeros = lambda wd: jnp.zeros((DEPTH, D_MODEL, wd), F32)
    kpe_lo = MLA_Q_LORA + MLA_KV_LORA
    whead = jnp.concatenate([w_in[:, :, :kpe_lo], zeros(MLA_NOPE), w_in[:, :, kpe_lo:_IN_MIX_START],
                             zeros(MLA_HEAD_PAD - MLA_NOPE - MLA_ROPE)], axis=2).astype(BF16)

    hq = MLA_NOPE + MLA_ROPE
    wqb = p["w_mla_q_b"].reshape(DEPTH, MLA_Q_LORA, MLA_HEADS, hq)
    wqb = jnp.pad(wqb, ((0, 0), (0, 0), (0, 0), (0, MLA_HEAD_PAD - hq))).reshape(DEPTH, MLA_Q_LORA, -1).astype(BF16)
    wkv = p["w_mla_kv_b"].reshape(DEPTH, MLA_KV_LORA, MLA_HEADS, MLA_NOPE + MLA_V)
    wk = jnp.pad(wkv[..., :MLA_NOPE], ((0, 0), (0, 0), (0, 0), (0, MLA_HEAD_PAD - MLA_NOPE)))
    wk = wk.reshape(DEPTH, MLA_KV_LORA, -1).astype(BF16)
    wv = _in_output_half(wkv[..., MLA_NOPE:].reshape(DEPTH, MLA_KV_LORA, MLA_HEADS // 2, 2, MLA_V))
    wv = wv.reshape(DEPTH, MLA_KV_LORA, -1).astype(BF16)
    egv = np.zeros((GQA_KV_WIDTH, 2 * GQA_KV_HEADS * V_PAD), np.float32)
    for g in range(GQA_KV_HEADS):
        for d in range(GQA_HEAD_DIM):
            egv[g * GQA_HEAD_DIM + d, (2 * g) * V_PAD + d] = 1.0
            egv[g * GQA_HEAD_DIM + d, (2 * g + 1) * V_PAD + GQA_HEAD_DIM + d] = 1.0
    egv = jnp.asarray(egv, BF16)
    gmat = jnp.asarray(np.kron(np.eye(GQA_HEADS), np.full((GQA_HEAD_DIM, GQA_HEAD_DIM), 1.0 / GQA_HEAD_DIM)), BF16)

    lam = lax.complex(p["ssm_a_re"], p["ssm_a_im"])
    dt = jnp.exp(p["ssm_log_dt"])[..., None]
    abar = jnp.exp(lam * dt)
    bbar = ((abar - 1.0) / lam)[..., None] * lax.complex(p["ssm_b_re"], p["ssm_b_im"])
    blk = lambda a: a.reshape((DEPTH, 2, SSM_NBLK, SSM_GBLOCK) + a.shape[3:])
    b_t = jnp.swapaxes(blk(bbar), -1, -2)
    ssm_b = jnp.concatenate([jnp.real(b_t), jnp.imag(b_t)], axis=-1).astype(BF16)
    ab = blk(abar).reshape(DEPTH, 2, SSM_NBLK, 1, SSM_SBLK)
    ssm_a = jnp.concatenate([jnp.real(ab), jnp.imag(ab)], axis=-1)
    c_re = jnp.swapaxes(blk(p["ssm_c_re"]), -1, -2)
    c_im = jnp.swapaxes(blk(p["ssm_c_im"]), -1, -2)
    ssm_c = jnp.concatenate([c_re, -c_im], axis=-2).astype(BF16)

    row = lambda a: a[:, None, :]
    return dict(
        norm_g=row(p["norm_g"]), whead=whead, wmix=wmix, qn=row(p["mla_q_norm"]), wqb=wqb,
        kvn=row(p["mla_kv_norm"]), wk=wk, wv=wv,
        gqn=row(jnp.tile(p["gqa_q_norm"], (1, GQA_HEADS))), gkn=row(jnp.tile(p["gqa_k_norm"], (1, GQA_KV_HEADS))),
        gmat=gmat, egv=egv,
        lam_parts=jnp.stack([p["diff_lq1"], p["diff_lk1"], p["diff_lq2"], p["diff_lk2"]], axis=1),
        subln=row(p["diff_subln"]),
        ssm_b=ssm_b, ssm_a=ssm_a, ssm_c=ssm_c, ssm_d=row(p["ssm_d"]),
        wg=wg, wm=wm, wglu=p["ssm_glu_w"].astype(BF16), bglu=row(p["ssm_glu_b"]),
        wbr=p["w_branch_out"].astype(BF16), wout=p["w_out"].astype(BF16),
    )


def _mixers(pr, b, t, l, w, lam_init, s0):
    q, k, v, dq, dk, dv, gq, gk, gv, u = pr[:10]
    three = lambda a: a.reshape(b, a.shape[0] // b, a.shape[-1])
    k, v, dk, dv, gk, gv = (three(a) for a in (k, v, dk, dv, gk, gv))
    mla_heads = tuple((hd, hd) for hd in range(MLA_HEADS))
    gqa_heads = tuple((hd // GQA_GROUP, 2 * (hd // GQA_GROUP) + hd % 2) for hd in range(GQA_HEADS))
    diff_specs = (_layer_spec((4, DIFF_HEAD_DIM), l), _layer_spec((1, DIFF_V), l))
    diff_extra = (w["lam_parts"], w["subln"])
    if s0 is None:
        o_a, o_b, o_c = _attention_short(lam_init, mla_heads, gqa_heads,
                                         (three(q), k, v, three(dq), dk, dv, three(gq), gk, gv),
                                         diff_specs, diff_extra)
    else:
        o_a = _attention(functools.partial(_attn_heads_kernel, mla_heads, MLA_HEAD_PAD), "attn_mla",
                         three(q), k, v)
        o_b = _attention(functools.partial(_diff_attn_kernel, lam_init), "attn_diff", three(dq), dk, dv,
                         extra_specs=diff_specs, extra=diff_extra)
        o_c = _attention(functools.partial(_attn_heads_kernel, gqa_heads, GQA_HEAD_DIM), "attn_gqa",
                         three(gq), gk, gv)
    n = b * t
    u_rows = u.reshape(n // SSM_SEQ, SSM_SEQ, SSM_WIDTH)
    if s0 is None:
        y_t, fin = _ssm(u_rows, l, w, None, True)
    else:
        (y_t,) = _ssm(u_rows, l, w, s0, False)
        fin = None
    y = jnp.swapaxes(y_t, 0, 1).reshape(n, SSM_WIDTH)
    flat = lambda a: a.reshape(n, 512)
    return flat(o_a), flat(o_b), flat(o_c), y, fin


def kernel(x_prompt, x_sample, cache_mla_ckv, cache_mla_krope, cache_diff_k, cache_diff_v, cache_gqa_k, cache_gqa_v, state_ssm, c, c_ctx, norm_g, w_mod, b_mod, w_in, mla_q_norm, w_mla_q_b, mla_kv_norm, w_mla_kv_b, diff_lq1, diff_lk1, diff_lq2, diff_lk2, diff_subln, gqa_q_norm, gqa_k_norm, ssm_a_re, ssm_a_im, ssm_log_dt, ssm_b_re, ssm_b_im, ssm_c_re, ssm_c_im, ssm_d, ssm_glu_w, ssm_glu_b, w_branch_out, w_out, final_norm):
    p = dict(norm_g=norm_g, mla_q_norm=mla_q_norm, w_mla_q_b=w_mla_q_b, mla_kv_norm=mla_kv_norm,
             w_mla_kv_b=w_mla_kv_b, diff_lq1=diff_lq1, diff_lk1=diff_lk1, diff_lq2=diff_lq2, diff_lk2=diff_lk2,
             diff_subln=diff_subln, gqa_q_norm=gqa_q_norm, gqa_k_norm=gqa_k_norm, ssm_a_re=ssm_a_re,
             ssm_a_im=ssm_a_im, ssm_log_dt=ssm_log_dt, ssm_b_re=ssm_b_re, ssm_b_im=ssm_b_im, ssm_c_re=ssm_c_re,
             ssm_c_im=ssm_c_im, ssm_d=ssm_d, ssm_glu_w=ssm_glu_w, ssm_glu_b=ssm_glu_b,
             w_branch_out=w_branch_out, w_out=w_out)
    bp, tp, _ = x_prompt.shape
    bs, ts, _ = x_sample.shape
    past = cache_mla_ckv.shape[2]
    assert tp == SSM_SEQ and ts % SSM_SEQ == 0 and ts // SSM_SEQ == V7X_SUBLANES
    rope = _rope_lane_tables(ts)
    w = _weights(w_in, p)

    cvec = jnp.concatenate([c_ctx[None], c, jnp.zeros((V7X_SUBLANES - 1 - bs, D_MODEL), F32)], axis=0)
    mod = _modulation(cvec, w_mod, b_mod)

    feature_major = lambda a: jnp.moveaxis(a, 2, -1).reshape(bs, DEPTH, -1, past)
    pad_lo = MLA_NOPE
    pad_hi = MLA_HEAD_PAD - MLA_NOPE - MLA_ROPE
    kpe_ctx = jnp.pad(feature_major(cache_mla_krope), ((0, 0), (0, 0), (pad_lo, pad_hi), (0, 0)))
    dk_ctx, gk_ctx, gv_ctx = (feature_major(a) for a in (cache_diff_k, cache_gqa_k, cache_gqa_v))
    dv_ctx = cache_diff_v
    s0_all = state_ssm.reshape(bs, DEPTH, 2, SSM_NBLK, SSM_GBLOCK * SSM_STATE, 2)
    s0_all = jnp.transpose(s0_all, (1, 2, 0, 3, 5, 4)).reshape(DEPTH, 2, bs, SSM_NBLK, 1, 2 * SSM_SBLK)

    xp = x_prompt.reshape(bp * tp, D_MODEL)
    xs = x_sample.reshape(bs * ts, D_MODEL)
    fn = final_norm[None]
    sides = ()
    states = []
    for l in range(DEPTH):
        lam_init = 0.8 - 0.6 * math.exp(-0.3 * l)
        last = l == DEPTH - 1
        sh, sc, ga = (mod[l, :, i * D_MODEL:(i + 1) * D_MODEL][:, None, :] for i in range(3))

        pr = _proj(xp, sh[0:1], sc[0:1], tp, l, w, prev_sides=sides)
        o_a, o_b, o_c, y, fin = _mixers(pr, bp, tp, l, w, lam_init, None)
        xp = _merge(xp, sh[0:1], sc[0:1], ga[0:1], tp, o_a, o_b, o_c, y, l, w, fn, last)
        sides = tuple(pr[10:])
        fin = fin.reshape(2, SSM_NBLK, bp, 2, SSM_GBLOCK, SSM_STATE)
        states.append(jnp.transpose(fin, (2, 0, 1, 4, 5, 3)).reshape(bp, 2, SSM_GROUPS, SSM_STATE, 2))

        pr = _proj(xs, sh[1:1 + bs], sc[1:1 + bs], ts, l, w, rope=rope,
                   cached=(cache_mla_ckv, kpe_ctx, dk_ctx, dv_ctx, gk_ctx, gv_ctx))
        o_a, o_b, o_c, y, _ = _mixers(pr, bs, ts, l, w, lam_init, s0_all)
        xs = _merge(xs, sh[1:1 + bs], sc[1:1 + bs], ga[1:1 + bs], ts, o_a, o_b, o_c, y, l, w, fn, last)

    ckv_s, kpe_t, dk_t, dv_s, gk_t, gv_t = sides
    untranspose = lambda a: jnp.swapaxes(a, 2, 3)
    return (xp.reshape(bp, tp, D_MODEL), xs.reshape(bs, ts, D_MODEL),
            ckv_s, untranspose(kpe_t),
            untranspose(dk_t).reshape(bp, DEPTH, tp, DIFF_HEADS, 2, DIFF_HEAD_DIM), dv_s,
            untranspose(gk_t).reshape(bp, DEPTH, tp, GQA_KV_HEADS, GQA_HEAD_DIM),
            untranspose(gv_t).reshape(bp, DEPTH, tp, GQA_KV_HEADS, GQA_HEAD_DIM),
            jnp.stack(states, axis=1))
```

```python
import functools
import math

import jax
import jax.numpy as jnp
import numpy as np
from jax import lax
from jax.experimental import pallas as pl
from jax.experimental.pallas import tpu as pltpu

F32 = jnp.float32
BF16 = jnp.bfloat16

D_MODEL = 1024
DEPTH = 2
GRID_W = 64
ROPE_THETA = 10000.0
EPS = 1e-6

MLA_HEADS = 8
MLA_NOPE = 64
MLA_ROPE = 32
MLA_V = 64
MLA_Q_LORA = 256
MLA_KV_LORA = 128
MLA_SCALE = (MLA_NOPE + MLA_ROPE) ** -0.5
MLA_HEAD_PAD = 128
DIFF_HEADS = 4
DIFF_HEAD_DIM = 64
DIFF_V = 2 * DIFF_HEAD_DIM
DIFF_SCALE = DIFF_HEAD_DIM ** -0.5
GQA_HEADS = 8
GQA_KV_HEADS = 2
GQA_HEAD_DIM = 64
GQA_GROUP = GQA_HEADS // GQA_KV_HEADS
GQA_KV_WIDTH = GQA_KV_HEADS * GQA_HEAD_DIM
GQA_SCALE = GQA_HEAD_DIM ** -0.5
SSM_WIDTH = 512
SSM_GROUP = 16
SSM_GROUPS = SSM_WIDTH // SSM_GROUP
SSM_STATE = 64
N_BRANCH = 4
BRANCH_WIDTH = 512

IN_SPLITS = (MLA_Q_LORA, MLA_KV_LORA, MLA_ROPE, 512, 512, 512, 512, 128, 128, SSM_WIDTH,
             N_BRANCH * BRANCH_WIDTH, N_BRANCH * D_MODEL)
D_IN = sum(IN_SPLITS)

V7X_LANES = 128
V7X_SUBLANES = 8
V7X_VMEM_LIMIT_BYTES = 56 * 1024 * 1024

_H_QA = (0, 256)
_H_KVA = (256, 384)
_H_KPE = (384, 512)
_H_WIDTH = 512
_M_DQ = (0, 512)
_M_DK = (512, 1024)
_M_DV = (1024, 1536)
_M_GQ = (1536, 2048)
_M_GK = (2048, 2176)
_M_GV = (2176, 2304)
_M_U = (2304, 2816)
_M_WIDTH = 2816
_IN_MIX_START = MLA_Q_LORA + MLA_KV_LORA + MLA_ROPE
_IN_GATE_START = _IN_MIX_START + _M_WIDTH
_IN_MERGE_START = _IN_GATE_START + N_BRANCH * BRANCH_WIDTH

SSM_GBLOCK = 8
SSM_NBLK = SSM_GROUPS // SSM_GBLOCK
SSM_SBLK = SSM_GBLOCK * SSM_STATE
SSM_UBLK = SSM_GBLOCK * SSM_GROUP
SSM_SEQ = 256
SSM_SLICE = 32

V_PAD = 128

W_IN_BLOCK = 512
PROJ_TM = 512
MERGE_TM = 512
ATTN_TQ_LONG = 512
ATTN_SEQ_PER_STEP = 2
ATTN_GROUP_LONG = 2


def _params(n_axes, vmem=V7X_VMEM_LIMIT_BYTES):
    return pltpu.CompilerParams(dimension_semantics=("arbitrary",) * n_axes, vmem_limit_bytes=vmem)


def _const_spec(shape):
    zeros = (0,) * len(shape)
    return pl.BlockSpec(shape, lambda *_: zeros, pipeline_mode=pl.Buffered(1))


def _layer_spec(shape, l):
    zeros = (0,) * len(shape)
    return pl.BlockSpec((None,) + tuple(shape), lambda *_: (l,) + zeros, pipeline_mode=pl.Buffered(1))


def _mod_spec(n_rows, tm, seq):
    if n_rows == 1:
        return pl.BlockSpec((1, 1, D_MODEL), lambda i: (0, 0, 0))
    return pl.BlockSpec((1, 1, D_MODEL), lambda i: (i * tm // seq, 0, 0))


def _dot(a, b):
    return jnp.dot(a, b, preferred_element_type=F32)


def _dot_nt(a, b):
    return lax.dot_general(a, b, (((1,), (1,)), ((), ())), preferred_element_type=F32)


def _rms(x):
    return x * lax.rsqrt(jnp.mean(x * x, axis=-1, keepdims=True) + EPS)


def _sigmoid(x):
    return 0.5 * jnp.tanh(0.5 * x) + 0.5


def _gelu_tanh(x):
    return 0.5 * x * (1.0 + jnp.tanh(math.sqrt(2.0 / math.pi) * (x + 0.044715 * (x * x * x))))


def _group_mean_sq(x, gmat):
    return _dot((x * x).astype(BF16), gmat)


def _rope_chunk(x, cos, sin, half):
    lane = lax.broadcasted_iota(jnp.int32, x.shape, 1)
    first = (lane % (2 * half)) < half
    swapped = jnp.where(first, pltpu.roll(x, V7X_LANES - half, 1), pltpu.roll(x, half, 1))
    return x * cos + swapped * sin


def _transpose_cast_kernel(w_ref, o_ref):
    o_ref[...] = w_ref[...].T.astype(BF16)


def _transposed_rows(w_t, row0, n_rows, blk):
    return pl.pallas_call(
        _transpose_cast_kernel,
        grid=(DEPTH, n_rows // blk),
        in_specs=[pl.BlockSpec((pl.Element(blk), pl.Element(D_MODEL)),
                               lambda l, j: (pl.multiple_of(l * D_IN + row0 + j * blk, V7X_SUBLANES), 0))],
        out_specs=pl.BlockSpec((None, D_MODEL, blk), lambda l, j: (l, 0, j)),
        out_shape=jax.ShapeDtypeStruct((DEPTH, D_MODEL, n_rows), BF16),
        compiler_params=_params(2),
        name="w_in_columns",
    )(w_t)


def _mod_kernel(c_ref, w_ref, b_ref, o_ref):
    c = c_ref[...]
    a = (c * _sigmoid(c)).astype(BF16)
    o_ref[0] = _dot(a, w_ref[0].astype(BF16)) + b_ref[0]


def _modulation(cvec, w_mod, b_mod):
    tn = 512
    return pl.pallas_call(
        _mod_kernel,
        grid=(DEPTH, 3 * D_MODEL // tn),
        in_specs=[pl.BlockSpec((V7X_SUBLANES, D_MODEL), lambda l, j: (0, 0)),
                  pl.BlockSpec((1, D_MODEL, tn), lambda l, j: (l, 0, j)),
                  pl.BlockSpec((1, 1, tn), lambda l, j: (l, 0, j))],
        out_specs=pl.BlockSpec((1, V7X_SUBLANES, tn), lambda l, j: (l, 0, j)),
        out_shape=jax.ShapeDtypeStruct((DEPTH, V7X_SUBLANES, 3 * D_MODEL), F32),
        compiler_params=_params(2),
        name="modulation",
    )(cvec, w_mod, b_mod.reshape(DEPTH, 1, 3 * D_MODEL))


def _modulated(x, g_ref, sh_ref, sc_ref):
    return (_rms(x) * g_ref[...]) * (1.0 + sc_ref[0]) + sh_ref[0]


def _proj_kernel(latent, n_prev, first_layer, *refs):
    (x_ref, sh_ref, sc_ref, g_ref, wh_ref, wx_ref, qn_ref, wqb_ref, kvn_ref, wk_ref, wv_ref,
     gqn_ref, gkn_ref, gm_ref, egv_ref) = refs[:15]
    pos = 15
    if latent:
        c_mla, s_mla, c_64, s_64 = refs[pos:pos + 4]
        cached = refs[pos + 4:pos + 10]
        pos += 10
    pos += n_prev
    (q_o, k_o, v_o, dq_o, dk_o, dv_o, gq_o, gk_o, gv_o, u_o) = refs[pos:pos + 10]
    pos += 10
    if not latent:
        ckv_f, kpe_t, dk_t, dv_f, gk_t, gv_t = refs[pos:pos + 6]

    def mla_keys_values(ckv_b, kpe):
        kn = _dot(ckv_b, wk_ref[...])
        for hd in range(MLA_HEADS):
            sl = slice(hd * MLA_HEAD_PAD, (hd + 1) * MLA_HEAD_PAD)
            k_o[:, sl] = (kn[:, sl] + kpe).astype(BF16)
        v_o[...] = _dot(ckv_b, wv_ref[...]).astype(BF16)

    def cached_tokens():
        ckv_c, kpe_t, dk_t, dv_c, gk_t, gv_t = cached
        mla_keys_values(ckv_c[...].astype(BF16), kpe_t[...].T)
        dk_o[...] = dk_t[...].T.astype(BF16)
        for hd in range(DIFF_HEADS):
            dv_o[:, hd * DIFF_V:(hd + 1) * DIFF_V] = dv_c[:, hd, :].astype(BF16)
        gk_o[...] = gk_t[...].T.astype(BF16)
        gv_o[...] = _dot(gv_t[...].T.astype(BF16), egv_ref[...]).astype(BF16)

    def new_tokens():
        _proj_new_tokens(latent, first_layer, refs[:15], (c_mla, s_mla, c_64, s_64) if latent else None,
                         refs[pos - 10:pos], refs[pos:pos + 6] if not latent else None, mla_keys_values)

    if latent:
        pl.when(pl.program_id(1) == 0)(cached_tokens)
        pl.when(pl.program_id(1) > 0)(new_tokens)
    else:
        new_tokens()


def _proj_new_tokens(latent, first_layer, in_refs, rope_refs, out_refs, side_refs, mla_keys_values):
    (x_ref, sh_ref, sc_ref, g_ref, wh_ref, wx_ref, qn_ref, wqb_ref, kvn_ref, wk_ref, wv_ref,
     gqn_ref, gkn_ref, gm_ref, egv_ref) = in_refs
    (q_o, k_o, v_o, dq_o, dk_o, dv_o, gq_o, gk_o, gv_o, u_o) = out_refs
    if latent:
        c_mla, s_mla, c_64, s_64 = rope_refs
    else:
        ckv_f, kpe_t, dk_t, dv_f, gk_t, gv_t = side_refs

    h = _modulated(x_ref[...], g_ref, sh_ref, sc_ref).astype(BF16)
    tm = h.shape[0]

    p_head = _dot(h, wh_ref[...])
    p_mix = _dot(h, wx_ref[...])

    def head(cols):
        return p_head[:, cols[0]:cols[1]]

    def mix(cols):
        return p_mix[:, cols[0]:cols[1]]

    def rope_mla(x):
        return _rope_chunk(x, c_mla[...], s_mla[...], MLA_ROPE // 2) if latent else x

    def rope_64(x):
        if not latent:
            return x
        c, s = c_64[...], s_64[...]
        chunks = [_rope_chunk(x[:, i:i + V7X_LANES], c, s, DIFF_HEAD_DIM // 2)
                  for i in range(0, x.shape[1], V7X_LANES)]
        return chunks[0] if len(chunks) == 1 else jnp.concatenate(chunks, axis=1)

    def per_sequence(x):
        return [x[b * SSM_SEQ:(b + 1) * SSM_SEQ] for b in range(tm // SSM_SEQ)]

    def slot(b):
        return (b,) if first_layer is None else (b, first_layer)

    def zero_other_slots(ref):
        if first_layer is not None:
            for b in range(ref.shape[0]):
                for other in range(ref.shape[1]):
                    if other != first_layer:
                        ref[b, other] = jnp.zeros(ref.shape[2:], ref.dtype)

    qn = (_rms(head(_H_QA)) * qn_ref[...]).astype(BF16)
    q = _dot(qn, wqb_ref[...])
    for hd in range(MLA_HEADS):
        sl = slice(hd * MLA_HEAD_PAD, (hd + 1) * MLA_HEAD_PAD)
        q_o[:, sl] = (rope_mla(q[:, sl]) * MLA_SCALE).astype(BF16)
    ckv = _rms(head(_H_KVA)) * kvn_ref[...]
    kpe = head(_H_KPE)
    if not latent:
        for b, (c_b, k_b) in enumerate(zip(per_sequence(ckv), per_sequence(kpe))):
            ckv_f[slot(b)] = c_b
            kpe_t[slot(b)] = k_b.T[MLA_NOPE:MLA_NOPE + MLA_ROPE, :]
        zero_other_slots(ckv_f)
        zero_other_slots(kpe_t)
    mla_keys_values(ckv.astype(BF16), rope_mla(kpe))

    dq_o[...] = (rope_64(mix(_M_DQ)) * DIFF_SCALE).astype(BF16)
    dk = rope_64(mix(_M_DK))
    dv = mix(_M_DV)
    dk_o[...] = dk.astype(BF16)
    dv_o[...] = dv.astype(BF16)

    gq = mix(_M_GQ)
    gq = gq * lax.rsqrt(_group_mean_sq(gq, gm_ref[...]) + EPS) * gqn_ref[...]
    gq_o[...] = (rope_64(gq) * GQA_SCALE).astype(BF16)
    gk = mix(_M_GK)
    kw = GQA_KV_WIDTH
    gk = gk * lax.rsqrt(_group_mean_sq(gk, gm_ref[:kw, :kw]) + EPS) * gkn_ref[...]
    gv = mix(_M_GV)
    if not latent:
        for b, (dk_b, dv_b, gk_b, gv_b) in enumerate(zip(*(per_sequence(a) for a in (dk, dv, gk, gv)))):
            dk_t[slot(b)] = dk_b.T
            for hd in range(DIFF_HEADS):
                dv_f[slot(b) + (slice(None), hd, slice(None))] = dv_b[:, hd * DIFF_V:(hd + 1) * DIFF_V]
            gk_t[slot(b)] = gk_b.T
            gv_t[slot(b)] = gv_b.T
        for ref in (dk_t, dv_f, gk_t, gv_t):
            zero_other_slots(ref)
    gk_o[...] = rope_64(gk).astype(BF16)
    gv_o[...] = _dot(gv.astype(BF16), egv_ref[...]).astype(BF16)

    u_o[...] = mix(_M_U)


def _side_shapes(n_seq):
    return [((n_seq, DEPTH, SSM_SEQ, MLA_KV_LORA), (SSM_SEQ, MLA_KV_LORA)),
            ((n_seq, DEPTH, MLA_ROPE, SSM_SEQ), (MLA_ROPE, SSM_SEQ)),
            ((n_seq, DEPTH, 512, SSM_SEQ), (512, SSM_SEQ)),
            ((n_seq, DEPTH, SSM_SEQ, DIFF_HEADS, DIFF_V), (SSM_SEQ, DIFF_HEADS, DIFF_V)),
            ((n_seq, DEPTH, GQA_KV_WIDTH, SSM_SEQ), (GQA_KV_WIDTH, SSM_SEQ)),
            ((n_seq, DEPTH, GQA_KV_WIDTH, SSM_SEQ), (GQA_KV_WIDTH, SSM_SEQ))]


def _proj(x, shift, scale, seq, l, w, rope=None, cached=None, prev_sides=()):
    n = x.shape[0]
    tm = PROJ_TM
    latent = cached is not None
    kw = GQA_KV_WIDTH
    if latent:
        tiles = seq // tm
        assert cached[0].shape[2] == tm
        grid = (n // seq, 1 + tiles)
        row = lambda b, j: (b * tiles + jnp.maximum(j - 1, 0), 0)
        kv_row = lambda b, j: (b * (1 + tiles) + j, 0)
        n_kv = (n // seq) * (cached[0].shape[2] + seq)
        mod_spec = pl.BlockSpec((1, 1, D_MODEL), lambda b, j: (b, 0, 0))
    else:
        grid = (n // tm,)
        row = kv_row = lambda i: (i, 0)
        n_kv = n
        mod_spec = _mod_spec(shift.shape[0], tm, seq)
    in_specs = [pl.BlockSpec((tm, D_MODEL), row), mod_spec, mod_spec,
                _layer_spec((1, D_MODEL), l), _layer_spec((D_MODEL, _H_WIDTH), l), _layer_spec((D_MODEL, _M_WIDTH), l),
                _layer_spec((1, MLA_Q_LORA), l), _layer_spec((MLA_Q_LORA, MLA_HEADS * MLA_HEAD_PAD), l),
                _layer_spec((1, MLA_KV_LORA), l), _layer_spec((MLA_KV_LORA, MLA_HEADS * MLA_HEAD_PAD), l),
                _layer_spec((MLA_KV_LORA, MLA_HEADS * V_PAD), l),
                _layer_spec((1, 512), l), _layer_spec((1, kw), l), _const_spec((512, 512)),
                _const_spec((kw, 2 * GQA_KV_HEADS * V_PAD))]
    args = [x, shift, scale, w["norm_g"], w["whead"], w["wmix"], w["qn"], w["wqb"], w["kvn"], w["wk"], w["wv"],
            w["gqn"], w["gkn"], w["gmat"], w["egv"]]
    if latent:
        tab = pl.BlockSpec((tm, V7X_LANES), lambda b, j: (jnp.maximum(j - 1, 0), 0))
        in_specs += [tab] * 4
        args += list(rope)
        in_specs += [pl.BlockSpec((None, None) + c.shape[2:], lambda b, j, nd=c.ndim: (b, l) + (0,) * (nd - 2))
                     for c in cached]
        args += list(cached)
    n_fixed = len(args)
    in_specs += [pl.BlockSpec(memory_space=pl.ANY)] * len(prev_sides)
    args += list(prev_sides)
    widths = [(1024, BF16, False), (1024, BF16, True), (MLA_HEADS * V_PAD, BF16, True), (512, BF16, False),
              (512, BF16, True), (512, BF16, True), (512, BF16, False), (kw, BF16, True),
              (2 * GQA_KV_HEADS * V_PAD, BF16, True), (512, F32, False)]
    out_specs = [pl.BlockSpec((tm, wd), kv_row if kv else row) for wd, _, kv in widths]
    out_shape = [jax.ShapeDtypeStruct((n_kv if kv else n, wd), dt) for wd, dt, kv in widths]
    aliases = {}
    first_layer = None
    if not latent:
        per_tile = tm // SSM_SEQ
        first_layer = None if prev_sides else l
        for full, blk in _side_shapes(n // SSM_SEQ):
            zeros = (0,) * len(blk)
            if prev_sides:
                out_specs.append(pl.BlockSpec((per_tile, None) + blk, lambda i, zeros=zeros: (i, l) + zeros))
            else:
                out_specs.append(pl.BlockSpec((per_tile, DEPTH) + blk, lambda i, zeros=zeros: (i, 0) + zeros))
            out_shape.append(jax.ShapeDtypeStruct(full, F32))
        aliases = {n_fixed + k: len(widths) + k for k in range(len(prev_sides))}
    return pl.pallas_call(
        functools.partial(_proj_kernel, latent, len(prev_sides), first_layer),
        grid=grid,
        in_specs=in_specs, out_specs=out_specs, out_shape=out_shape,
        input_output_aliases=aliases,
        compiler_params=_params(len(grid)),
        name="proj_latent" if latent else "proj_context",
    )(*args)


def _softmax_maps(maps):
    scores = [_dot_nt(q, k) for q, k, _ in maps]
    tops = [jnp.max(s, axis=-1, keepdims=True) for s in scores]
    weights = [jnp.exp(s - m).astype(BF16) for s, m in zip(scores, tops)]
    outs = []
    for p, (_, _, v) in zip(weights, maps):
        ones = jnp.ones((v.shape[0], V7X_LANES), BF16)
        o = _dot(p, jnp.concatenate([v, ones], axis=1))
        outs.append((o[:, :V7X_LANES], o[:, V7X_LANES:]))
    return outs


def _head_maps(heads, q_width, q_ref, k_ref, v_ref, part):
    return [(q_ref[b, :, hd * q_width:(hd + 1) * q_width],
             k_ref[b, :, heads[hd][0] * q_width:(heads[hd][0] + 1) * q_width],
             v_ref[b, :, heads[hd][1] * V_PAD:(heads[hd][1] + 1) * V_PAD]) for b, hd in part]


def _head_outputs(outs, part, o_ref):
    terms = [o * (1.0 / total) for o, total in outs]
    for (b, hd), low, high in zip(part[0::2], terms[0::2], terms[1::2]):
        o_ref[b, :, (hd // 2) * V7X_LANES:(hd // 2 + 1) * V7X_LANES] = (low + high).astype(BF16)


def _diff_maps(q_ref, k_ref, v_ref, part):
    d = DIFF_HEAD_DIM
    return [(q_ref[b, :, c * d:(c + 1) * d], k_ref[b, :, c * d:(c + 1) * d],
             v_ref[b, :, hd * DIFF_V:(hd + 1) * DIFF_V])
            for b, hd in part for c in (2 * hd, 2 * hd + 1)]


def _diff_lambda(lp_ref, lam_init):
    lp = lp_ref[...]
    return (jnp.exp(jnp.sum(lp[0:1] * lp[1:2], axis=-1, keepdims=True))
            - jnp.exp(jnp.sum(lp[2:3] * lp[3:4], axis=-1, keepdims=True)) + lam_init)


def _diff_outputs(outs, part, lam, lam_init, sub_ref, o_ref):
    mixed = [o1 * (1.0 / t1) - o2 * (lam / t2) for (o1, t1), (o2, t2) in zip(outs[0::2], outs[1::2])]
    normed = [_rms(o) * sub_ref[...] * (1.0 - lam_init) for o in mixed]
    for (b, hd), o in zip(part, normed):
        o_ref[b, :, hd * DIFF_V:(hd + 1) * DIFF_V] = o.astype(BF16)


def _items(q_ref, n_heads):
    return [(b, hd) for b in range(q_ref.shape[0]) for hd in range(n_heads)]


def _attn_heads_kernel(heads, q_width, q_ref, k_ref, v_ref, o_ref):
    items = _items(q_ref, len(heads))
    group = ATTN_GROUP_LONG
    for g0 in range(0, len(items), group):
        part = items[g0:g0 + group]
        _head_outputs(_softmax_maps(_head_maps(heads, q_width, q_ref, k_ref, v_ref, part)), part, o_ref)


def _diff_attn_kernel(lam_init, q_ref, k_ref, v_ref, lp_ref, sub_ref, o_ref):
    lam = _diff_lambda(lp_ref, lam_init)
    items = _items(q_ref, DIFF_HEADS)
    group = ATTN_GROUP_LONG // 2
    for g0 in range(0, len(items), group):
        part = items[g0:g0 + group]
        _diff_outputs(_softmax_maps(_diff_maps(q_ref, k_ref, v_ref, part)), part, lam, lam_init, sub_ref, o_ref)


def _attn_short_kernel(lam_init, mla_heads, gqa_heads, qa_ref, ka_ref, va_ref, qb_ref, kb_ref, vb_ref,
                       qc_ref, kc_ref, vc_ref, lp_ref, sub_ref, oa_ref, ob_ref, oc_ref):
    part_a, part_b, part_c = _items(qa_ref, len(mla_heads)), _items(qb_ref, DIFF_HEADS), _items(qc_ref, len(gqa_heads))
    maps_a = _head_maps(mla_heads, MLA_HEAD_PAD, qa_ref, ka_ref, va_ref, part_a)
    maps_b = _diff_maps(qb_ref, kb_ref, vb_ref, part_b)
    maps_c = _head_maps(gqa_heads, GQA_HEAD_DIM, qc_ref, kc_ref, vc_ref, part_c)
    outs = _softmax_maps(maps_a + maps_b + maps_c)
    n_a, n_b = len(maps_a), len(maps_b)
    _head_outputs(outs[:n_a], part_a, oa_ref)
    _diff_outputs(outs[n_a:n_a + n_b], part_b, _diff_lambda(lp_ref, lam_init), lam_init, sub_ref, ob_ref)
    _head_outputs(outs[n_a + n_b:], part_c, oc_ref)


def _attention_short(lam_init, mla_heads, gqa_heads, qkv, extra_specs, extra):
    b, t, _ = qkv[0].shape
    nb = ATTN_SEQ_PER_STEP
    whole = lambda i: (i, 0, 0)
    in_specs = [pl.BlockSpec((nb, t, a.shape[2]), whole) for a in qkv] + list(extra_specs)
    return pl.pallas_call(
        functools.partial(_attn_short_kernel, lam_init, mla_heads, gqa_heads),
        grid=(b // nb,),
        in_specs=in_specs,
        out_specs=[pl.BlockSpec((nb, t, 512), whole)] * 3,
        out_shape=[jax.ShapeDtypeStruct((b, t, 512), BF16)] * 3,
        compiler_params=_params(1),
        name="attn_context",
    )(*qkv, *extra)


def _attention(kernel, name, q, k, v, extra_specs=(), extra=()):
    b, t, wq = q.shape
    s = k.shape[1]
    tq = ATTN_TQ_LONG
    in_specs = [pl.BlockSpec((1, tq, wq), lambda i, j: (i, j, 0)),
                pl.BlockSpec((1, s, k.shape[2]), lambda i, j: (i, 0, 0)),
                pl.BlockSpec((1, s, v.shape[2]), lambda i, j: (i, 0, 0))]
    in_specs += list(extra_specs)
    return pl.pallas_call(
        kernel,
        grid=(b, t // tq),
        in_specs=in_specs,
        out_specs=pl.BlockSpec((1, tq, 512), lambda i, j: (i, j, 0)),
        out_shape=jax.ShapeDtypeStruct((b, t, 512), BF16),
        compiler_params=_params(2),
        name=name,
    )(q, k, v, *extra)


def _ssm_kernel(chunked, want_final, *refs):
    u_ref, bg_ref, ab_ref, cg_ref, d_ref = refs[:5]
    pos = 5
    if chunked:
        s0_ref = refs[pos]
        pos += 1
    y_ref = refs[pos]
    pos += 1
    if want_final:
        fin_ref = refs[pos]
        pos += 1
    hs_ref, yacc_ref, ut_ref, bm_ref, cm_ref = refs[pos:pos + 5]

    @pl.when(pl.program_id(1) == 0)
    def _():
        bm_ref[...] = jnp.zeros(bm_ref.shape, bm_ref.dtype)
        cm_ref[...] = jnp.zeros(cm_ref.shape, cm_ref.dtype)
        for dr in range(2):
            for g in range(SSM_GBLOCK):
                ins = slice(g * SSM_GROUP, (g + 1) * SSM_GROUP)
                for part in range(2):
                    states = slice(part * SSM_SBLK + g * SSM_STATE, part * SSM_SBLK + (g + 1) * SSM_STATE)
                    bm_ref[dr, ins, states] = bg_ref[dr, 0, g, :, part * SSM_STATE:(part + 1) * SSM_STATE]
                    cm_ref[dr, states, ins] = cg_ref[dr, 0, g, part * SSM_STATE:(part + 1) * SSM_STATE, :]

    seq, rows, sb = SSM_SEQ, V7X_SUBLANES, SSM_SBLK
    n_slices = seq // SSM_SLICE
    slice_rows = SSM_SLICE * rows

    def visited(dr, stage):
        return stage if dr == 0 else n_slices - 1 - stage

    def rows_of(k):
        return slice(k * slice_rows, (k + 1) * slice_rows)

    def u_rows(k, first_visit):
        if first_visit:
            steps = range(k * SSM_SLICE, (k + 1) * SSM_SLICE)
            ut_ref[rows_of(k), :] = jnp.concatenate([u_ref[:, t, :] for t in steps], axis=0)
        return ut_ref[rows_of(k), :]

    def input_proj(stage):
        for dr in range(2):
            k = visited(dr, stage)
            hs_ref[dr, rows_of(k), :] = _dot(u_rows(k, stage < n_slices // 2).astype(BF16), bm_ref[dr])

    def output_proj(stage):
        for dr in range(2):
            r = rows_of(visited(dr, stage))
            part = _dot(hs_ref[dr, r, :].astype(BF16), cm_ref[dr])
            if stage < n_slices // 2:
                yacc_ref[r, :] = part
            else:
                yacc_ref[r, :] += part

    a_parts = []
    for dr in range(2):
        a = ab_ref[dr, 0]
        a_parts.append((jnp.broadcast_to(a[:, :sb], (rows, sb)), jnp.broadcast_to(a[:, sb:], (rows, sb))))

    def step_rows(stage, step):
        i = stage * SSM_SLICE + step
        return [slice(t * rows, (t + 1) * rows) for t in (i, seq - 1 - i)]

    def scan_slice(stage, carry):
        for step in range(SSM_SLICE):
            new = []
            for dr, sl in enumerate(step_rows(stage, step)):
                ar, ai = a_parts[dr]
                hr, hi = carry[2 * dr], carry[2 * dr + 1]
                b = hs_ref[dr, sl, :]
                nr = ar * hr - ai * hi + b[:, :sb]
                ni = ar * hi + ai * hr + b[:, sb:]
                hs_ref[dr, sl, :sb] = nr
                hs_ref[dr, sl, sb:] = ni
                new += [nr, ni]
            carry = tuple(new)
        return carry

    def fix_slice(stage, carry):
        for step in range(SSM_SLICE):
            new = []
            for dr, sl in enumerate(step_rows(stage, step)):
                ar, ai = a_parts[dr]
                zr, zi = carry[2 * dr], carry[2 * dr + 1]
                nr = ar * zr - ai * zi
                ni = ar * zi + ai * zr
                hs_ref[dr, sl, :sb] += nr
                hs_ref[dr, sl, sb:] += ni
                new += [nr, ni]
            carry = tuple(new)
        return carry

    zero = jnp.zeros((rows, sb), F32)
    fin = (zero, zero, zero, zero)
    input_proj(0)
    for stage in range(n_slices):
        if stage + 1 < n_slices:
            input_proj(stage + 1)
        fin = scan_slice(stage, fin)
        if not chunked and stage >= 1:
            output_proj(stage - 1)

    if want_final:
        for dr in range(2):
            fin_ref[dr, 0, :, :sb] = fin[2 * dr]
            fin_ref[dr, 0, :, sb:] = fin[2 * dr + 1]

    if chunked:
        entry = []
        for dr in range(2):
            ar, ai = a_parts[dr]
            pr, pi = ar[0:1], ai[0:1]
            for _ in range(int(math.log2(seq))):
                pr, pi = pr * pr - pi * pi, 2.0 * pr * pi
            fr, fi = fin[2 * dr], fin[2 * dr + 1]
            s0 = s0_ref[dr, 0, 0]
            er, ei = s0[:, :sb], s0[:, sb:]
            order = range(rows) if dr == 0 else range(rows - 1, -1, -1)
            rows_r, rows_i = [None] * rows, [None] * rows
            for c in order:
                rows_r[c], rows_i[c] = er, ei
                er, ei = (pr * er - pi * ei + fr[c:c + 1], pr * ei + pi * er + fi[c:c + 1])
            entry += [jnp.concatenate(rows_r, axis=0), jnp.concatenate(rows_i, axis=0)]
        carry = tuple(entry)
        for stage in range(n_slices):
            carry = fix_slice(stage, carry)
            if stage >= 1:
                output_proj(stage - 1)
    output_proj(n_slices - 1)

    y = _gelu_tanh(yacc_ref[...] + d_ref[...] * ut_ref[...])
    y_ref[...] = y.reshape(seq, rows, SSM_UBLK)


def _ssm(u, l, w, s0, want_final):
    r, seq, _ = u.shape
    chunked = s0 is not None
    rows = V7X_SUBLANES
    in_specs = [pl.BlockSpec((rows, seq, SSM_UBLK), lambda j, i: (i, 0, j)),
                pl.BlockSpec((None, 2, 1, SSM_GBLOCK, SSM_GROUP, 2 * SSM_STATE), lambda j, i: (l, 0, j, 0, 0, 0)),
                pl.BlockSpec((None, 2, 1, 1, 2 * SSM_SBLK), lambda j, i: (l, 0, j, 0, 0)),
                pl.BlockSpec((None, 2, 1, SSM_GBLOCK, 2 * SSM_STATE, SSM_GROUP), lambda j, i: (l, 0, j, 0, 0, 0)),
                pl.BlockSpec((None, 1, SSM_UBLK), lambda j, i: (l, 0, j))]
    args = [u, w["ssm_b"], w["ssm_a"], w["ssm_c"], w["ssm_d"]]
    if chunked:
        in_specs.append(pl.BlockSpec((None, 2, 1, 1, 1, 2 * SSM_SBLK), lambda j, i: (l, 0, i, j, 0, 0)))
        args.append(s0)
    out_specs = [pl.BlockSpec((seq, rows, SSM_UBLK), lambda j, i: (0, i, j))]
    out_shape = [jax.ShapeDtypeStruct((seq, r, SSM_WIDTH), F32)]
    if want_final:
        out_specs.append(pl.BlockSpec((2, 1, rows, 2 * SSM_SBLK), lambda j, i: (0, j, i, 0)))
        out_shape.append(jax.ShapeDtypeStruct((2, SSM_NBLK, r, 2 * SSM_SBLK), F32))
    return pl.pallas_call(
        functools.partial(_ssm_kernel, chunked, want_final),
        grid=(SSM_NBLK, r // rows),
        in_specs=in_specs, out_specs=out_specs, out_shape=out_shape,
        scratch_shapes=[pltpu.VMEM((2, seq * rows, 2 * SSM_SBLK), F32), pltpu.VMEM((seq * rows, SSM_UBLK), F32),
                        pltpu.VMEM((seq * rows, SSM_UBLK), F32),
                        pltpu.VMEM((2, SSM_UBLK, 2 * SSM_SBLK), BF16), pltpu.VMEM((2, 2 * SSM_SBLK, SSM_UBLK), BF16)],
        compiler_params=_params(2),
        name="ssm_latent" if chunked else "ssm_context",
    )(*args)


def _merge_kernel(last, x_ref, sh_ref, sc_ref, ga_ref, g_ref, oa_ref, ob_ref, oc_ref, y_ref,
                  wg_ref, wm_ref, wglu_ref, bglu_ref, wbr_ref, wout_ref, fn_ref, o_ref):
    x = x_ref[...]
    h = _modulated(x, g_ref, sh_ref, sc_ref).astype(BF16)
    y = y_ref[...]
    o_d = y * _sigmoid(_dot(y.astype(BF16), wglu_ref[...]) + bglu_ref[...])
    branches = (oa_ref[...].astype(F32), ob_ref[...].astype(F32), oc_ref[...].astype(F32), o_d)
    acc = None
    for n, o in enumerate(branches):
        gate = _dot(h, wg_ref[:, n * BRANCH_WIDTH:(n + 1) * BRANCH_WIDTH])
        br = _dot((o * (gate * _sigmoid(gate))).astype(BF16), wbr_ref[n])
        term = _sigmoid(_dot(h, wm_ref[:, n * D_MODEL:(n + 1) * D_MODEL])) * br
        acc = term if acc is None else acc + term
    xn = x + ga_ref[0] * _dot(acc.astype(BF16), wout_ref[...])
    if last:
        xn = _rms(xn) * fn_ref[...]
    o_ref[...] = xn


def _merge(x, shift, scale, gate, seq, oa, ob, oc, y, l, w, final_norm, last):
    n = x.shape[0]
    tm = MERGE_TM
    row = lambda i: (i, 0)
    mod_spec = _mod_spec(shift.shape[0], tm, seq)
    in_specs = [pl.BlockSpec((tm, D_MODEL), row), mod_spec, mod_spec, mod_spec, _layer_spec((1, D_MODEL), l),
                pl.BlockSpec((tm, 512), row), pl.BlockSpec((tm, 512), row), pl.BlockSpec((tm, 512), row),
                pl.BlockSpec((tm, 512), row),
                _layer_spec((D_MODEL, N_BRANCH * BRANCH_WIDTH), l), _layer_spec((D_MODEL, N_BRANCH * D_MODEL), l),
                _layer_spec((SSM_WIDTH, SSM_WIDTH), l), _layer_spec((1, SSM_WIDTH), l),
                _layer_spec((N_BRANCH, BRANCH_WIDTH, D_MODEL), l), _layer_spec((D_MODEL, D_MODEL), l),
                _const_spec((1, D_MODEL))]
    return pl.pallas_call(
        functools.partial(_merge_kernel, last),
        grid=(n // tm,),
        in_specs=in_specs,
        out_specs=pl.BlockSpec((tm, D_MODEL), row),
        out_shape=jax.ShapeDtypeStruct((n, D_MODEL), F32),
        compiler_params=_params(1),
        name="merge",
    )(x, shift, scale, gate, w["norm_g"], oa, ob, oc, y, w["wg"], w["wm"], w["wglu"], w["bglu"],
      w["wbr"], w["wout"], final_norm)


def _rope_tables(n_tok, rot_dim):
    f32 = np.float32
    rows = n_tok // GRID_W
    row = np.repeat(np.arange(rows, dtype=f32), GRID_W)
    col = np.tile(np.arange(GRID_W, dtype=f32), rows)
    quarter = rot_dim // 4
    inv = (f32(ROPE_THETA) ** (-np.arange(quarter, dtype=f32) / f32(quarter))).astype(f32)
    ang = np.concatenate([row[:, None] * inv, col[:, None] * inv], axis=-1).astype(f32)
    return np.cos(ang).astype(f32), np.sin(ang).astype(f32)


def _rope_lane_tables(n_tok):
    c, s = _rope_tables(n_tok, MLA_ROPE)
    ones = np.ones((n_tok, MLA_NOPE), np.float32)
    pad = MLA_HEAD_PAD - MLA_NOPE - MLA_ROPE
    c_mla = np.concatenate([ones, c, c, np.ones((n_tok, pad), np.float32)], axis=1)
    s_mla = np.concatenate([0.0 * ones, -s, s, np.zeros((n_tok, pad), np.float32)], axis=1)
    c, s = _rope_tables(n_tok, DIFF_HEAD_DIM)
    c_64 = np.concatenate([c, c, c, c], axis=1)
    s_64 = np.concatenate([-s, s, -s, s], axis=1)
    return tuple(jnp.asarray(a, F32) for a in (c_mla, s_mla, c_64, s_64))


def _in_output_half(v):
    zeros = jnp.zeros_like(v[..., 0, :])
    return jnp.stack([jnp.concatenate([v[..., 0, :], zeros], axis=-1),
                      jnp.concatenate([zeros, v[..., 1, :]], axis=-1)], axis=-2)


def _weights(w_in, p):
    w_t = jnp.swapaxes(w_in, 1, 2).reshape(DEPTH * D_IN, D_MODEL)
    wmix = _transposed_rows(w_t, _IN_MIX_START, _M_WIDTH, W_IN_BLOCK // 2)
    wg = _transposed_rows(w_t, _IN_GATE_START, N_BRANCH * BRANCH_WIDTH, W_IN_BLOCK)
    wm = _transposed_rows(w_t, _IN_MERGE_START, N_BRANCH * D_MODEL, W_IN_BLOCK)
    zeros = lambda wd: jnp.zeros((DEPTH, D_MODEL, wd), F32)
    kpe_lo = MLA_Q_LORA + MLA_KV_LORA
    whead = jnp.concatenate([w_in[:, :, :kpe_lo], zeros(MLA_NOPE), w_in[:, :, kpe_lo:_IN_MIX_START],
                             zeros(MLA_HEAD_PAD - MLA_NOPE - MLA_ROPE)], axis=2).astype(BF16)

    hq = MLA_NOPE + MLA_ROPE
    wqb = p["w_mla_q_b"].reshape(DEPTH, MLA_Q_LORA, MLA_HEADS, hq)
    wqb = jnp.pad(wqb, ((0, 0), (0, 0), (0, 0), (0, MLA_HEAD_PAD - hq))).reshape(DEPTH, MLA_Q_LORA, -1).astype(BF16)
    wkv = p["w_mla_kv_b"].reshape(DEPTH, MLA_KV_LORA, MLA_HEADS, MLA_NOPE + MLA_V)
    wk = jnp.pad(wkv[..., :MLA_NOPE], ((0, 0), (0, 0), (0, 0), (0, MLA_HEAD_PAD - MLA_NOPE)))
    wk = wk.reshape(DEPTH, MLA_KV_LORA, -1).astype(BF16)
    wv = _in_output_half(wkv[..., MLA_NOPE:].reshape(DEPTH, MLA_KV_LORA, MLA_HEADS // 2, 2, MLA_V))
    wv = wv.reshape(DEPTH, MLA_KV_LORA, -1).astype(BF16)
    egv = np.zeros((GQA_KV_WIDTH, 2 * GQA_KV_HEADS * V_PAD), np.float32)
    for g in range(GQA_KV_HEADS):
        for d in range(GQA_HEAD_DIM):
            egv[g * GQA_HEAD_DIM + d, (2 * g) * V_PAD + d] = 1.0
            egv[g * GQA_HEAD_DIM + d, (2 * g + 1) * V_PAD + GQA_HEAD_DIM + d] = 1.0
    egv = jnp.asarray(egv, BF16)
    gmat = jnp.asarray(np.kron(np.eye(GQA_HEADS), np.full((GQA_HEAD_DIM, GQA_HEAD_DIM), 1.0 / GQA_HEAD_DIM)), BF16)

    lam = lax.complex(p["ssm_a_re"], p["ssm_a_im"])
    dt = jnp.exp(p["ssm_log_dt"])[..., None]
    abar = jnp.exp(lam * dt)
    bbar = ((abar - 1.0) / lam)[..., None] * lax.complex(p["ssm_b_re"], p["ssm_b_im"])
    blk = lambda a: a.reshape((DEPTH, 2, SSM_NBLK, SSM_GBLOCK) + a.shape[3:])
    b_t = jnp.swapaxes(blk(bbar), -1, -2)
    ssm_b = jnp.concatenate([jnp.real(b_t), jnp.imag(b_t)], axis=-1).astype(BF16)
    ab = blk(abar).reshape(DEPTH, 2, SSM_NBLK, 1, SSM_SBLK)
    ssm_a = jnp.concatenate([jnp.real(ab), jnp.imag(ab)], axis=-1)
    c_re = jnp.swapaxes(blk(p["ssm_c_re"]), -1, -2)
    c_im = jnp.swapaxes(blk(p["ssm_c_im"]), -1, -2)
    ssm_c = jnp.concatenate([c_re, -c_im], axis=-2).astype(BF16)

    row = lambda a: a[:, None, :]
    return dict(
        norm_g=row(p["norm_g"]), whead=whead, wmix=wmix, qn=row(p["mla_q_norm"]), wqb=wqb,
        kvn=row(p["mla_kv_norm"]), wk=wk, wv=wv,
        gqn=row(jnp.tile(p["gqa_q_norm"], (1, GQA_HEADS))), gkn=row(jnp.tile(p["gqa_k_norm"], (1, GQA_KV_HEADS))),
        gmat=gmat, egv=egv,
        lam_parts=jnp.stack([p["diff_lq1"], p["diff_lk1"], p["diff_lq2"], p["diff_lk2"]], axis=1),
        subln=row(p["diff_subln"]),
        ssm_b=ssm_b, ssm_a=ssm_a, ssm_c=ssm_c, ssm_d=row(p["ssm_d"]),
        wg=wg, wm=wm, wglu=p["ssm_glu_w"].astype(BF16), bglu=row(p["ssm_glu_b"]),
        wbr=p["w_branch_out"].astype(BF16), wout=p["w_out"].astype(BF16),
    )


def _mixers(pr, b, t, l, w, lam_init, s0):
    q, k, v, dq, dk, dv, gq, gk, gv, u = pr[:10]
    three = lambda a: a.reshape(b, a.shape[0] // b, a.shape[-1])
    k, v, dk, dv, gk, gv = (three(a) for a in (k, v, dk, dv, gk, gv))
    mla_heads = tuple((hd, hd) for hd in range(MLA_HEADS))
    gqa_heads = tuple((hd // GQA_GROUP, 2 * (hd // GQA_GROUP) + hd % 2) for hd in range(GQA_HEADS))
    diff_specs = (_layer_spec((4, DIFF_HEAD_DIM), l), _layer_spec((1, DIFF_V), l))
    diff_extra = (w["lam_parts"], w["subln"])
    if s0 is None:
        o_a, o_b, o_c = _attention_short(lam_init, mla_heads, gqa_heads,
                                         (three(q), k, v, three(dq), dk, dv, three(gq), gk, gv),
                                         diff_specs, diff_extra)
    else:
        o_a = _attention(functools.partial(_attn_heads_kernel, mla_heads, MLA_HEAD_PAD), "attn_mla",
                         three(q), k, v)
        o_b = _attention(functools.partial(_diff_attn_kernel, lam_init), "attn_diff", three(dq), dk, dv,
                         extra_specs=diff_specs, extra=diff_extra)
        o_c = _attention(functools.partial(_attn_heads_kernel, gqa_heads, GQA_HEAD_DIM), "attn_gqa",
                         three(gq), gk, gv)
    n = b * t
    u_rows = u.reshape(n // SSM_SEQ, SSM_SEQ, SSM_WIDTH)
    if s0 is None:
        y_t, fin = _ssm(u_rows, l, w, None, True)
    else:
        (y_t,) = _ssm(u_rows, l, w, s0, False)
        fin = None
    y = jnp.swapaxes(y_t, 0, 1).reshape(n, SSM_WIDTH)
    flat = lambda a: a.reshape(n, 512)
    return flat(o_a), flat(o_b), flat(o_c), y, fin


def kernel(x_prompt, x_sample, cache_mla_ckv, cache_mla_krope, cache_diff_k, cache_diff_v, cache_gqa_k, cache_gqa_v, state_ssm, c, c_ctx, norm_g, w_mod, b_mod, w_in, mla_q_norm, w_mla_q_b, mla_kv_norm, w_mla_kv_b, diff_lq1, diff_lk1, diff_lq2, diff_lk2, diff_subln, gqa_q_norm, gqa_k_norm, ssm_a_re, ssm_a_im, ssm_log_dt, ssm_b_re, ssm_b_im, ssm_c_re, ssm_c_im, ssm_d, ssm_glu_w, ssm_glu_b, w_branch_out, w_out, final_norm):
    p = dict(norm_g=norm_g, mla_q_norm=mla_q_norm, w_mla_q_b=w_mla_q_b, mla_kv_norm=mla_kv_norm,
             w_mla_kv_b=w_mla_kv_b, diff_lq1=diff_lq1, diff_lk1=diff_lk1, diff_lq2=diff_lq2, diff_lk2=diff_lk2,
             diff_subln=diff_subln, gqa_q_norm=gqa_q_norm, gqa_k_norm=gqa_k_norm, ssm_a_re=ssm_a_re,
             ssm_a_im=ssm_a_im, ssm_log_dt=ssm_log_dt, ssm_b_re=ssm_b_re, ssm_b_im=ssm_b_im, ssm_c_re=ssm_c_re,
             ssm_c_im=ssm_c_im, ssm_d=ssm_d, ssm_glu_w=ssm_glu_w, ssm_glu_b=ssm_glu_b,
             w_branch_out=w_branch_out, w_out=w_out)
    bp, tp, _ = x_prompt.shape
    bs, ts, _ = x_sample.shape
    past = cache_mla_ckv.shape[2]
    assert tp == SSM_SEQ and ts % SSM_SEQ == 0 and ts // SSM_SEQ == V7X_SUBLANES
    rope = _rope_lane_tables(ts)
    w = _weights(w_in, p)

    cvec = jnp.concatenate([c_ctx[None], c, jnp.zeros((V7X_SUBLANES - 1 - bs, D_MODEL), F32)], axis=0)
    mod = _modulation(cvec, w_mod, b_mod)

    feature_major = lambda a: jnp.moveaxis(a, 2, -1).reshape(bs, DEPTH, -1, past)
    pad_lo = MLA_NOPE
    pad_hi = MLA_HEAD_PAD - MLA_NOPE - MLA_ROPE
    kpe_ctx = jnp.pad(feature_major(cache_mla_krope), ((0, 0), (0, 0), (pad_lo, pad_hi), (0, 0)))
    dk_ctx, gk_ctx, gv_ctx = (feature_major(a) for a in (cache_diff_k, cache_gqa_k, cache_gqa_v))
    dv_ctx = cache_diff_v
    s0_all = state_ssm.reshape(bs, DEPTH, 2, SSM_NBLK, SSM_GBLOCK * SSM_STATE, 2)
    s0_all = jnp.transpose(s0_all, (1, 2, 0, 3, 5, 4)).reshape(DEPTH, 2, bs, SSM_NBLK, 1, 2 * SSM_SBLK)

    xp = x_prompt.reshape(bp * tp, D_MODEL)
    xs = x_sample.reshape(bs * ts, D_MODEL)
    fn = final_norm[None]
    sides = ()
    states = []
    for l in range(DEPTH):
        lam_init = 0.8 - 0.6 * math.exp(-0.3 * l)
        last = l == DEPTH - 1
        sh, sc, ga = (mod[l, :, i * D_MODEL:(i + 1) * D_MODEL][:, None, :] for i in range(3))

        pr = _proj(xp, sh[0:1], sc[0:1], tp, l, w, prev_sides=sides)
        o_a, o_b, o_c, y, fin = _mixers(pr, bp, tp, l, w, lam_init, None)
        xp = _merge(xp, sh[0:1], sc[0:1], ga[0:1], tp, o_a, o_b, o_c, y, l, w, fn, last)
        sides = tuple(pr[10:])
        fin = fin.reshape(2, SSM_NBLK, bp, 2, SSM_GBLOCK, SSM_STATE)
        states.append(jnp.transpose(fin, (2, 0, 1, 4, 5, 3)).reshape(bp, 2, SSM_GROUPS, SSM_STATE, 2))

        pr = _proj(xs, sh[1:1 + bs], sc[1:1 + bs], ts, l, w, rope=rope,
                   cached=(cache_mla_ckv, kpe_ctx, dk_ctx, dv_ctx, gk_ctx, gv_ctx))
        o_a, o_b, o_c, y, _ = _mixers(pr, bs, ts, l, w, lam_init, s0_all)
        xs = _merge(xs, sh[1:1 + bs], sc[1:1 + bs], ga[1:1 + bs], ts, o_a, o_b, o_c, y, l, w, fn, last)

    ckv_s, kpe_t, dk_t, dv_s, gk_t, gv_t = sides
    untranspose = lambda a: jnp.swapaxes(a, 2, 3)
    return (xp.reshape(bp, tp, D_MODEL), xs.reshape(bs, ts, D_MODEL),
            ckv_s, untranspose(kpe_t),
            untranspose(dk_t).reshape(bp, DEPTH, tp, DIFF_HEADS, 2, DIFF_HEAD_DIM), dv_s,
            untranspose(gk_t).reshape(bp, DEPTH, tp, GQA_KV_HEADS, GQA_HEAD_DIM),
            untranspose(gv_t).reshape(bp, DEPTH, tp, GQA_KV_HEADS, GQA_HEAD_DIM),
            jnp.stack(states, axis=1))
```

```python
import functools
import math

import jax
import jax.numpy as jnp
import numpy as np
from jax import lax
from jax.experimental import pallas as pl
from jax.experimental.pallas import tpu as pltpu

F32 = jnp.float32
BF16 = jnp.bfloat16

D_MODEL = 1024
DEPTH = 2
GRID_W = 64
ROPE_THETA = 10000.0
EPS = 1e-6

MLA_HEADS = 8
MLA_NOPE = 64
MLA_ROPE = 32
MLA_V = 64
MLA_Q_LORA = 256
MLA_KV_LORA = 128
MLA_SCALE = (MLA_NOPE + MLA_ROPE) ** -0.5
MLA_HEAD_PAD = 128
DIFF_HEADS = 4
DIFF_HEAD_DIM = 64
DIFF_V = 2 * DIFF_HEAD_DIM
DIFF_SCALE = DIFF_HEAD_DIM ** -0.5
GQA_HEADS = 8
GQA_KV_HEADS = 2
GQA_HEAD_DIM = 64
GQA_GROUP = GQA_HEADS // GQA_KV_HEADS
GQA_KV_WIDTH = GQA_KV_HEADS * GQA_HEAD_DIM
GQA_SCALE = GQA_HEAD_DIM ** -0.5
SSM_WIDTH = 512
SSM_GROUP = 16
SSM_GROUPS = SSM_WIDTH // SSM_GROUP
SSM_STATE = 64
N_BRANCH = 4
BRANCH_WIDTH = 512

IN_SPLITS = (MLA_Q_LORA, MLA_KV_LORA, MLA_ROPE, 512, 512, 512, 512, 128, 128, SSM_WIDTH,
             N_BRANCH * BRANCH_WIDTH, N_BRANCH * D_MODEL)
D_IN = sum(IN_SPLITS)

V7X_LANES = 128
V7X_SUBLANES = 8
V7X_VMEM_LIMIT_BYTES = 56 * 1024 * 1024

_H_QA = (0, 256)
_H_KVA = (256, 384)
_H_KPE = (384, 512)
_H_WIDTH = 512
_M_DQ = (0, 512)
_M_DK = (512, 1024)
_M_DV = (1024, 1536)
_M_GQ = (1536, 2048)
_M_GK = (2048, 2176)
_M_GV = (2176, 2304)
_M_U = (2304, 2816)
_M_WIDTH = 2816
_IN_MIX_START = MLA_Q_LORA + MLA_KV_LORA + MLA_ROPE
_IN_GATE_START = _IN_MIX_START + _M_WIDTH
_IN_MERGE_START = _IN_GATE_START + N_BRANCH * BRANCH_WIDTH

SSM_GBLOCK = 8
SSM_NBLK = SSM_GROUPS // SSM_GBLOCK
SSM_SBLK = SSM_GBLOCK * SSM_STATE
SSM_UBLK = SSM_GBLOCK * SSM_GROUP
SSM_SEQ = 256
SSM_SLICE = 32

V_PAD = 128

W_IN_BLOCK = 1024
PROJ_TM = 512
MERGE_TM = 512
ATTN_TQ_LONG = 512
ATTN_SEQ_PER_STEP = 2
ATTN_GROUP_LONG = 2


def _params(n_axes, vmem=V7X_VMEM_LIMIT_BYTES):
    return pltpu.CompilerParams(dimension_semantics=("arbitrary",) * n_axes, vmem_limit_bytes=vmem)


def _const_spec(shape):
    zeros = (0,) * len(shape)
    return pl.BlockSpec(shape, lambda *_: zeros, pipeline_mode=pl.Buffered(1))


def _layer_spec(shape, l):
    zeros = (0,) * len(shape)
    return pl.BlockSpec((None,) + tuple(shape), lambda *_: (l,) + zeros, pipeline_mode=pl.Buffered(1))


def _mod_spec(n_rows, tm, seq):
    if n_rows == 1:
        return pl.BlockSpec((1, 1, D_MODEL), lambda i: (0, 0, 0))
    return pl.BlockSpec((1, 1, D_MODEL), lambda i: (i * tm // seq, 0, 0))


def _dot(a, b):
    return jnp.dot(a, b, preferred_element_type=F32)


def _dot_nt(a, b):
    return lax.dot_general(a, b, (((1,), (1,)), ((), ())), preferred_element_type=F32)


def _rms(x):
    return x * lax.rsqrt(jnp.mean(x * x, axis=-1, keepdims=True) + EPS)


def _sigmoid(x):
    return 0.5 * jnp.tanh(0.5 * x) + 0.5


def _gelu_tanh(x):
    return 0.5 * x * (1.0 + jnp.tanh(math.sqrt(2.0 / math.pi) * (x + 0.044715 * (x * x * x))))


def _group_mean_sq(x, gmat):
    return _dot((x * x).astype(BF16), gmat)


def _rope_chunk(x, cos, sin, half):
    lane = lax.broadcasted_iota(jnp.int32, x.shape, 1)
    first = (lane % (2 * half)) < half
    swapped = jnp.where(first, pltpu.roll(x, V7X_LANES - half, 1), pltpu.roll(x, half, 1))
    return x * cos + swapped * sin


def _transpose_cast_kernel(w_ref, o_ref):
    o_ref[...] = w_ref[...].T.astype(BF16)


def _transposed_rows(w_t, row0, n_rows, blk):
    return pl.pallas_call(
        _transpose_cast_kernel,
        grid=(DEPTH, n_rows // blk),
        in_specs=[pl.BlockSpec((pl.Element(blk), pl.Element(D_MODEL)),
                               lambda l, j: (pl.multiple_of(l * D_IN + row0 + j * blk, V7X_SUBLANES), 0))],
        out_specs=pl.BlockSpec((None, D_MODEL, blk), lambda l, j: (l, 0, j)),
        out_shape=jax.ShapeDtypeStruct((DEPTH, D_MODEL, n_rows), BF16),
        compiler_params=_params(2),
        name="w_in_columns",
    )(w_t)


def _mod_kernel(c_ref, w_ref, b_ref, o_ref):
    c = c_ref[...]
    a = (c * _sigmoid(c)).astype(BF16)
    o_ref[0] = _dot(a, w_ref[0].astype(BF16)) + b_ref[0]


def _modulation(cvec, w_mod, b_mod):
    tn = 512
    return pl.pallas_call(
        _mod_kernel,
        grid=(DEPTH, 3 * D_MODEL // tn),
        in_specs=[pl.BlockSpec((V7X_SUBLANES, D_MODEL), lambda l, j: (0, 0)),
                  pl.BlockSpec((1, D_MODEL, tn), lambda l, j: (l, 0, j)),
                  pl.BlockSpec((1, 1, tn), lambda l, j: (l, 0, j))],
        out_specs=pl.BlockSpec((1, V7X_SUBLANES, tn), lambda l, j: (l, 0, j)),
        out_shape=jax.ShapeDtypeStruct((DEPTH, V7X_SUBLANES, 3 * D_MODEL), F32),
        compiler_params=_params(2),
        name="modulation",
    )(cvec, w_mod, b_mod.reshape(DEPTH, 1, 3 * D_MODEL))


def _modulated(x, g_ref, sh_ref, sc_ref):
    return (_rms(x) * g_ref[...]) * (1.0 + sc_ref[0]) + sh_ref[0]


def _proj_kernel(latent, n_prev, first_layer, *refs):
    (x_ref, sh_ref, sc_ref, g_ref, wh_ref, wx_ref, qn_ref, wqb_ref, kvn_ref, wk_ref, wv_ref,
     gqn_ref, gkn_ref, gm_ref, egv_ref) = refs[:15]
    pos = 15
    if latent:
        c_mla, s_mla, c_64, s_64 = refs[pos:pos + 4]
        cached = refs[pos + 4:pos + 10]
        pos += 10
    pos += n_prev
    (q_o, k_o, v_o, dq_o, dk_o, dv_o, gq_o, gk_o, gv_o, u_o) = refs[pos:pos + 10]
    pos += 10
    if not latent:
        ckv_f, kpe_t, dk_t, dv_f, gk_t, gv_t = refs[pos:pos + 6]

    def mla_keys_values(ckv_b, kpe):
        kn = _dot(ckv_b, wk_ref[...])
        for hd in range(MLA_HEADS):
            sl = slice(hd * MLA_HEAD_PAD, (hd + 1) * MLA_HEAD_PAD)
            k_o[:, sl] = (kn[:, sl] + kpe).astype(BF16)
        v_o[...] = _dot(ckv_b, wv_ref[...]).astype(BF16)

    def cached_tokens():
        ckv_c, kpe_t, dk_t, dv_c, gk_t, gv_t = cached
        mla_keys_values(ckv_c[...].astype(BF16), kpe_t[...].T)
        dk_o[...] = dk_t[...].T.astype(BF16)
        for hd in range(DIFF_HEADS):
            dv_o[:, hd * DIFF_V:(hd + 1) * DIFF_V] = dv_c[:, hd, :].astype(BF16)
        gk_o[...] = gk_t[...].T.astype(BF16)
        gv_o[...] = _dot(gv_t[...].T.astype(BF16), egv_ref[...]).astype(BF16)

    def new_tokens():
        _proj_new_tokens(latent, first_layer, refs[:15], (c_mla, s_mla, c_64, s_64) if latent else None,
                         refs[pos - 10:pos], refs[pos:pos + 6] if not latent else None, mla_keys_values)

    if latent:
        pl.when(pl.program_id(1) == 0)(cached_tokens)
        pl.when(pl.program_id(1) > 0)(new_tokens)
    else:
        new_tokens()


def _proj_new_tokens(latent, first_layer, in_refs, rope_refs, out_refs, side_refs, mla_keys_values):
    (x_ref, sh_ref, sc_ref, g_ref, wh_ref, wx_ref, qn_ref, wqb_ref, kvn_ref, wk_ref, wv_ref,
     gqn_ref, gkn_ref, gm_ref, egv_ref) = in_refs
    (q_o, k_o, v_o, dq_o, dk_o, dv_o, gq_o, gk_o, gv_o, u_o) = out_refs
    if latent:
        c_mla, s_mla, c_64, s_64 = rope_refs
    else:
        ckv_f, kpe_t, dk_t, dv_f, gk_t, gv_t = side_refs

    h = _modulated(x_ref[...], g_ref, sh_ref, sc_ref).astype(BF16)
    tm = h.shape[0]

    p_head = _dot(h, wh_ref[...])
    p_mix = _dot(h, wx_ref[...])

    def head(cols):
        return p_head[:, cols[0]:cols[1]]

    def mix(cols):
        return p_mix[:, cols[0]:cols[1]]

    def rope_mla(x):
        return _rope_chunk(x, c_mla[...], s_mla[...], MLA_ROPE // 2) if latent else x

    def rope_64(x):
        if not latent:
            return x
        c, s = c_64[...], s_64[...]
        chunks = [_rope_chunk(x[:, i:i + V7X_LANES], c, s, DIFF_HEAD_DIM // 2)
                  for i in range(0, x.shape[1], V7X_LANES)]
        return chunks[0] if len(chunks) == 1 else jnp.concatenate(chunks, axis=1)

    def per_sequence(x):
        return [x[b * SSM_SEQ:(b + 1) * SSM_SEQ] for b in range(tm // SSM_SEQ)]

    def slot(b):
        return (b,) if first_layer is None else (b, first_layer)

    def zero_other_slots(ref):
        if first_layer is not None:
            for b in range(ref.shape[0]):
                for other in range(ref.shape[1]):
                    if other != first_layer:
                        ref[b, other] = jnp.zeros(ref.shape[2:], ref.dtype)

    qn = (_rms(head(_H_QA)) * qn_ref[...]).astype(BF16)
    q = _dot(qn, wqb_ref[...])
    for hd in range(MLA_HEADS):
        sl = slice(hd * MLA_HEAD_PAD, (hd + 1) * MLA_HEAD_PAD)
        q_o[:, sl] = (rope_mla(q[:, sl]) * MLA_SCALE).astype(BF16)
    ckv = _rms(head(_H_KVA)) * kvn_ref[...]
    kpe = head(_H_KPE)
    if not latent:
        for b, (c_b, k_b) in enumerate(zip(per_sequence(ckv), per_sequence(kpe))):
            ckv_f[slot(b)] = c_b
            kpe_t[slot(b)] = k_b.T[MLA_NOPE:MLA_NOPE + MLA_ROPE, :]
        zero_other_slots(ckv_f)
        zero_other_slots(kpe_t)
    mla_keys_values(ckv.astype(BF16), rope_mla(kpe))

    dq_o[...] = (rope_64(mix(_M_DQ)) * DIFF_SCALE).astype(BF16)
    dk = rope_64(mix(_M_DK))
    dv = mix(_M_DV)
    dk_o[...] = dk.astype(BF16)
    dv_o[...] = dv.astype(BF16)

    gq = mix(_M_GQ)
    gq = gq * lax.rsqrt(_group_mean_sq(gq, gm_ref[...]) + EPS) * gqn_ref[...]
    gq_o[...] = (rope_64(gq) * GQA_SCALE).astype(BF16)
    gk = mix(_M_GK)
    kw = GQA_KV_WIDTH
    gk = gk * lax.rsqrt(_group_mean_sq(gk, gm_ref[:kw, :kw]) + EPS) * gkn_ref[...]
    gv = mix(_M_GV)
    if not latent:
        for b, (dk_b, dv_b, gk_b, gv_b) in enumerate(zip(*(per_sequence(a) for a in (dk, dv, gk, gv)))):
            dk_t[slot(b)] = dk_b.T
            for hd in range(DIFF_HEADS):
                dv_f[slot(b) + (slice(None), hd, slice(None))] = dv_b[:, hd * DIFF_V:(hd + 1) * DIFF_V]
            gk_t[slot(b)] = gk_b.T
            gv_t[slot(b)] = gv_b.T
        for ref in (dk_t, dv_f, gk_t, gv_t):
            zero_other_slots(ref)
    gk_o[...] = rope_64(gk).astype(BF16)
    gv_o[...] = _dot(gv.astype(BF16), egv_ref[...]).astype(BF16)

    u_o[...] = mix(_M_U)


def _side_shapes(n_seq):
    return [((n_seq, DEPTH, SSM_SEQ, MLA_KV_LORA), (SSM_SEQ, MLA_KV_LORA)),
            ((n_seq, DEPTH, MLA_ROPE, SSM_SEQ), (MLA_ROPE, SSM_SEQ)),
            ((n_seq, DEPTH, 512, SSM_SEQ), (512, SSM_SEQ)),
            ((n_seq, DEPTH, SSM_SEQ, DIFF_HEADS, DIFF_V), (SSM_SEQ, DIFF_HEADS, DIFF_V)),
            ((n_seq, DEPTH, GQA_KV_WIDTH, SSM_SEQ), (GQA_KV_WIDTH, SSM_SEQ)),
            ((n_seq, DEPTH, GQA_KV_WIDTH, SSM_SEQ), (GQA_KV_WIDTH, SSM_SEQ))]


def _proj(x, shift, scale, seq, l, w, rope=None, cached=None, prev_sides=()):
    n = x.shape[0]
    tm = PROJ_TM
    latent = cached is not None
    kw = GQA_KV_WIDTH
    if latent:
        tiles = seq // tm
        assert cached[0].shape[2] == tm
        grid = (n // seq, 1 + tiles)
        row = lambda b, j: (b * tiles + jnp.maximum(j - 1, 0), 0)
        kv_row = lambda b, j: (b * (1 + tiles) + j, 0)
        n_kv = (n // seq) * (cached[0].shape[2] + seq)
        mod_spec = pl.BlockSpec((1, 1, D_MODEL), lambda b, j: (b, 0, 0))
    else:
        grid = (n // tm,)
        row = kv_row = lambda i: (i, 0)
        n_kv = n
        mod_spec = _mod_spec(shift.shape[0], tm, seq)
    in_specs = [pl.BlockSpec((tm, D_MODEL), row), mod_spec, mod_spec,
                _layer_spec((1, D_MODEL), l), _layer_spec((D_MODEL, _H_WIDTH), l), _layer_spec((D_MODEL, _M_WIDTH), l),
                _layer_spec((1, MLA_Q_LORA), l), _layer_spec((MLA_Q_LORA, MLA_HEADS * MLA_HEAD_PAD), l),
                _layer_spec((1, MLA_KV_LORA), l), _layer_spec((MLA_KV_LORA, MLA_HEADS * MLA_HEAD_PAD), l),
                _layer_spec((MLA_KV_LORA, MLA_HEADS * V_PAD), l),
                _layer_spec((1, 512), l), _layer_spec((1, kw), l), _const_spec((512, 512)),
                _const_spec((kw, 2 * GQA_KV_HEADS * V_PAD))]
    args = [x, shift, scale, w["norm_g"], w["whead"], w["wmix"], w["qn"], w["wqb"], w["kvn"], w["wk"], w["wv"],
            w["gqn"], w["gkn"], w["gmat"], w["egv"]]
    if latent:
        tab = pl.BlockSpec((tm, V7X_LANES), lambda b, j: (jnp.maximum(j - 1, 0), 0))
        in_specs += [tab] * 4
        args += list(rope)
        in_specs += [pl.BlockSpec((None, None) + c.shape[2:], lambda b, j, nd=c.ndim: (b, l) + (0,) * (nd - 2))
                     for c in cached]
        args += list(cached)
    n_fixed = len(args)
    in_specs += [pl.BlockSpec(memory_space=pl.ANY)] * len(prev_sides)
    args += list(prev_sides)
    widths = [(1024, BF16, False), (1024, BF16, True), (MLA_HEADS * V_PAD, BF16, True), (512, BF16, False),
              (512, BF16, True), (512, BF16, True), (512, BF16, False), (kw, BF16, True),
              (2 * GQA_KV_HEADS * V_PAD, BF16, True), (512, F32, False)]
    out_specs = [pl.BlockSpec((tm, wd), kv_row if kv else row) for wd, _, kv in widths]
    out_shape = [jax.ShapeDtypeStruct((n_kv if kv else n, wd), dt) for wd, dt, kv in widths]
    aliases = {}
    first_layer = None
    if not latent:
        per_tile = tm // SSM_SEQ
        first_layer = None if prev_sides else l
        for full, blk in _side_shapes(n // SSM_SEQ):
            zeros = (0,) * len(blk)
            if prev_sides:
                out_specs.append(pl.BlockSpec((per_tile, None) + blk, lambda i, zeros=zeros: (i, l) + zeros))
            else:
                out_specs.append(pl.BlockSpec((per_tile, DEPTH) + blk, lambda i, zeros=zeros: (i, 0) + zeros))
            out_shape.append(jax.ShapeDtypeStruct(full, F32))
        aliases = {n_fixed + k: len(widths) + k for k in range(len(prev_sides))}
    return pl.pallas_call(
        functools.partial(_proj_kernel, latent, len(prev_sides), first_layer),
        grid=grid,
        in_specs=in_specs, out_specs=out_specs, out_shape=out_shape,
        input_output_aliases=aliases,
        compiler_params=_params(len(grid)),
        name="proj_latent" if latent else "proj_context",
    )(*args)


def _softmax_maps(maps):
    scores = [_dot_nt(q, k) for q, k, _ in maps]
    tops = [jnp.max(s, axis=-1, keepdims=True) for s in scores]
    weights = [jnp.exp(s - m).astype(BF16) for s, m in zip(scores, tops)]
    outs = []
    for p, (_, _, v) in zip(weights, maps):
        ones = jnp.ones((v.shape[0], V7X_LANES), BF16)
        o = _dot(p, jnp.concatenate([v, ones], axis=1))
        outs.append((o[:, :V7X_LANES], o[:, V7X_LANES:]))
    return outs


def _head_maps(heads, q_width, q_ref, k_ref, v_ref, part):
    return [(q_ref[b, :, hd * q_width:(hd + 1) * q_width],
             k_ref[b, :, heads[hd][0] * q_width:(heads[hd][0] + 1) * q_width],
             v_ref[b, :, heads[hd][1] * V_PAD:(heads[hd][1] + 1) * V_PAD]) for b, hd in part]


def _head_outputs(outs, part, o_ref):
    terms = [o * (1.0 / total) for o, total in outs]
    for (b, hd), low, high in zip(part[0::2], terms[0::2], terms[1::2]):
        o_ref[b, :, (hd // 2) * V7X_LANES:(hd // 2 + 1) * V7X_LANES] = (low + high).astype(BF16)


def _diff_maps(q_ref, k_ref, v_ref, part):
    d = DIFF_HEAD_DIM
    return [(q_ref[b, :, c * d:(c + 1) * d], k_ref[b, :, c * d:(c + 1) * d],
             v_ref[b, :, hd * DIFF_V:(hd + 1) * DIFF_V])
            for b, hd in part for c in (2 * hd, 2 * hd + 1)]


def _diff_lambda(lp_ref, lam_init):
    lp = lp_ref[...]
    return (jnp.exp(jnp.sum(lp[0:1] * lp[1:2], axis=-1, keepdims=True))
            - jnp.exp(jnp.sum(lp[2:3] * lp[3:4], axis=-1, keepdims=True)) + lam_init)


def _diff_outputs(outs, part, lam, lam_init, sub_ref, o_ref):
    mixed = [o1 * (1.0 / t1) - o2 * (lam / t2) for (o1, t1), (o2, t2) in zip(outs[0::2], outs[1::2])]
    normed = [_rms(o) * sub_ref[...] * (1.0 - lam_init) for o in mixed]
    for (b, hd), o in zip(part, normed):
        o_ref[b, :, hd * DIFF_V:(hd + 1) * DIFF_V] = o.astype(BF16)


def _items(q_ref, n_heads):
    return [(b, hd) for b in range(q_ref.shape[0]) for hd in range(n_heads)]


def _attn_heads_kernel(heads, q_width, q_ref, k_ref, v_ref, o_ref):
    items = _items(q_ref, len(heads))
    group = ATTN_GROUP_LONG
    for g0 in range(0, len(items), group):
        part = items[g0:g0 + group]
        _head_outputs(_softmax_maps(_head_maps(heads, q_width, q_ref, k_ref, v_ref, part)), part, o_ref)


def _diff_attn_kernel(lam_init, q_ref, k_ref, v_ref, lp_ref, sub_ref, o_ref):
    lam = _diff_lambda(lp_ref, lam_init)
    items = _items(q_ref, DIFF_HEADS)
    group = ATTN_GROUP_LONG // 2
    for g0 in range(0, len(items), group):
        part = items[g0:g0 + group]
        _diff_outputs(_softmax_maps(_diff_maps(q_ref, k_ref, v_ref, part)), part, lam, lam_init, sub_ref, o_ref)


def _attn_short_kernel(lam_init, mla_heads, gqa_heads, qa_ref, ka_ref, va_ref, qb_ref, kb_ref, vb_ref,
                       qc_ref, kc_ref, vc_ref, lp_ref, sub_ref, oa_ref, ob_ref, oc_ref):
    part_a, part_b, part_c = _items(qa_ref, len(mla_heads)), _items(qb_ref, DIFF_HEADS), _items(qc_ref, len(gqa_heads))
    maps_a = _head_maps(mla_heads, MLA_HEAD_PAD, qa_ref, ka_ref, va_ref, part_a)
    maps_b = _diff_maps(qb_ref, kb_ref, vb_ref, part_b)
    maps_c = _head_maps(gqa_heads, GQA_HEAD_DIM, qc_ref, kc_ref, vc_ref, part_c)
    outs = _softmax_maps(maps_a + maps_b + maps_c)
    n_a, n_b = len(maps_a), len(maps_b)
    _head_outputs(outs[:n_a], part_a, oa_ref)
    _diff_outputs(outs[n_a:n_a + n_b], part_b, _diff_lambda(lp_ref, lam_init), lam_init, sub_ref, ob_ref)
    _head_outputs(outs[n_a + n_b:], part_c, oc_ref)


def _attention_short(lam_init, mla_heads, gqa_heads, qkv, extra_specs, extra):
    b, t, _ = qkv[0].shape
    nb = ATTN_SEQ_PER_STEP
    whole = lambda i: (i, 0, 0)
    in_specs = [pl.BlockSpec((nb, t, a.shape[2]), whole) for a in qkv] + list(extra_specs)
    return pl.pallas_call(
        functools.partial(_attn_short_kernel, lam_init, mla_heads, gqa_heads),
        grid=(b // nb,),
        in_specs=in_specs,
        out_specs=[pl.BlockSpec((nb, t, 512), whole)] * 3,
        out_shape=[jax.ShapeDtypeStruct((b, t, 512), BF16)] * 3,
        compiler_params=_params(1),
        name="attn_context",
    )(*qkv, *extra)


def _attention(kernel, name, q, k, v, extra_specs=(), extra=()):
    b, t, wq = q.shape
    s = k.shape[1]
    tq = ATTN_TQ_LONG
    in_specs = [pl.BlockSpec((1, tq, wq), lambda i, j: (i, j, 0)),
                pl.BlockSpec((1, s, k.shape[2]), lambda i, j: (i, 0, 0)),
                pl.BlockSpec((1, s, v.shape[2]), lambda i, j: (i, 0, 0))]
    in_specs += list(extra_specs)
    return pl.pallas_call(
        kernel,
        grid=(b, t // tq),
        in_specs=in_specs,
        out_specs=pl.BlockSpec((1, tq, 512), lambda i, j: (i, j, 0)),
        out_shape=jax.ShapeDtypeStruct((b, t, 512), BF16),
        compiler_params=_params(2),
        name=name,
    )(q, k, v, *extra)


def _ssm_kernel(chunked, want_final, *refs):
    u_ref, bg_ref, ab_ref, cg_ref, d_ref = refs[:5]
    pos = 5
    if chunked:
        s0_ref = refs[pos]
        pos += 1
    y_ref = refs[pos]
    pos += 1
    if want_final:
        fin_ref = refs[pos]
        pos += 1
    hs_ref, hb_ref, yacc_ref, ut_ref, bm_ref, cm_ref = refs[pos:pos + 6]

    @pl.when(pl.program_id(1) == 0)
    def _():
        bm_ref[...] = jnp.zeros(bm_ref.shape, bm_ref.dtype)
        cm_ref[...] = jnp.zeros(cm_ref.shape, cm_ref.dtype)
        for dr in range(2):
            for g in range(SSM_GBLOCK):
                ins = slice(g * SSM_GROUP, (g + 1) * SSM_GROUP)
                for part in range(2):
                    states = slice(part * SSM_SBLK + g * SSM_STATE, part * SSM_SBLK + (g + 1) * SSM_STATE)
                    bm_ref[dr, ins, states] = bg_ref[dr, 0, g, :, part * SSM_STATE:(part + 1) * SSM_STATE]
                    cm_ref[dr, states, ins] = cg_ref[dr, 0, g, part * SSM_STATE:(part + 1) * SSM_STATE, :]

    seq, rows, sb = SSM_SEQ, V7X_SUBLANES, SSM_SBLK
    n_slices = seq // SSM_SLICE
    slice_rows = SSM_SLICE * rows

    def visited(dr, stage):
        return stage if dr == 0 else n_slices - 1 - stage

    def rows_of(k):
        return slice(k * slice_rows, (k + 1) * slice_rows)

    def u_rows(k, first_visit):
        if first_visit:
            block = u_ref[:, k * SSM_SLICE:(k + 1) * SSM_SLICE, :]
            ut_ref[rows_of(k), :] = jnp.swapaxes(block, 0, 1).reshape(slice_rows, SSM_UBLK)
        return ut_ref[rows_of(k), :]

    def input_proj(stage):
        for dr in range(2):
            k = visited(dr, stage)
            hs_ref[dr, rows_of(k), :] = _dot(u_rows(k, stage < n_slices // 2).astype(BF16), bm_ref[dr])

    def output_proj(stage):
        for dr in range(2):
            r = rows_of(visited(dr, stage))
            part = _dot(hb_ref[dr, r, :], cm_ref[dr])
            if stage < n_slices // 2:
                yacc_ref[r, :] = part
            else:
                yacc_ref[r, :] += part

    a_parts = []
    for dr in range(2):
        a = ab_ref[dr, 0]
        a_parts.append((jnp.broadcast_to(a[:, :sb], (rows, sb)), jnp.broadcast_to(a[:, sb:], (rows, sb))))

    def store_pair(dr, t_low, low, high):
        pair_rows = slice(t_low * rows, (t_low + 2) * rows)
        hb_ref[dr, pair_rows, :sb] = jnp.concatenate([low[0], high[0]], axis=0).astype(BF16)
        hb_ref[dr, pair_rows, sb:] = jnp.concatenate([low[1], high[1]], axis=0).astype(BF16)

    def step_pairs(stage, update, store):
        for step in range(0, SSM_SLICE, 2):
            i0 = stage * SSM_SLICE + step
            states = [[None, None], [None, None]]
            for sub in range(2):
                for dr, t in enumerate((i0 + sub, seq - 1 - i0 - sub)):
                    states[dr][sub] = update(dr, slice(t * rows, (t + 1) * rows))
            if store:
                store_pair(0, i0, states[0][0], states[0][1])
                store_pair(1, seq - 2 - i0, states[1][1], states[1][0])

    def scan_slice(stage, carry):
        carry = list(carry)

        def update(dr, sl):
            ar, ai = a_parts[dr]
            hr, hi = carry[2 * dr], carry[2 * dr + 1]
            b = hs_ref[dr, sl, :]
            nr = ar * hr - ai * hi + b[:, :sb]
            ni = ar * hi + ai * hr + b[:, sb:]
            if chunked:
                hs_ref[dr, sl, :sb] = nr
                hs_ref[dr, sl, sb:] = ni
            carry[2 * dr], carry[2 * dr + 1] = nr, ni
            return nr, ni

        step_pairs(stage, update, store=not chunked)
        return tuple(carry)

    def fix_slice(stage, carry):
        carry = list(carry)

        def update(dr, sl):
            ar, ai = a_parts[dr]
            zr, zi = carry[2 * dr], carry[2 * dr + 1]
            nr = ar * zr - ai * zi
            ni = ar * zi + ai * zr
            carry[2 * dr], carry[2 * dr + 1] = nr, ni
            h = hs_ref[dr, sl, :]
            return h[:, :sb] + nr, h[:, sb:] + ni

        step_pairs(stage, update, store=True)
        return tuple(carry)

    zero = jnp.zeros((rows, sb), F32)
    fin = (zero, zero, zero, zero)
    input_proj(0)
    for stage in range(n_slices):
        if stage + 1 < n_slices:
            input_proj(stage + 1)
        fin = scan_slice(stage, fin)
        if not chunked and stage >= 1:
            output_proj(stage - 1)

    if want_final:
        for dr in range(2):
            fin_ref[dr, 0, :, :sb] = fin[2 * dr]
            fin_ref[dr, 0, :, sb:] = fin[2 * dr + 1]

    if chunked:
        entry = []
        for dr in range(2):
            ar, ai = a_parts[dr]
            pr, pi = ar[0:1], ai[0:1]
            for _ in range(int(math.log2(seq))):
                pr, pi = pr * pr - pi * pi, 2.0 * pr * pi
            fr, fi = fin[2 * dr], fin[2 * dr + 1]
            s0 = s0_ref[dr, 0, 0]
            er, ei = s0[:, :sb], s0[:, sb:]
            order = range(rows) if dr == 0 else range(rows - 1, -1, -1)
            rows_r, rows_i = [None] * rows, [None] * rows
            for c in order:
                rows_r[c], rows_i[c] = er, ei
                er, ei = (pr * er - pi * ei + fr[c:c + 1], pr * ei + pi * er + fi[c:c + 1])
            entry += [jnp.concatenate(rows_r, axis=0), jnp.concatenate(rows_i, axis=0)]
        carry = tuple(entry)
        for stage in range(n_slices):
            carry = fix_slice(stage, carry)
            if stage >= 1:
                output_proj(stage - 1)
    output_proj(n_slices - 1)

    y = _gelu_tanh(yacc_ref[...] + d_ref[...] * ut_ref[...])
    y_ref[...] = y.reshape(seq, rows, SSM_UBLK)


def _ssm(u, l, w, s0, want_final):
    r, seq, _ = u.shape
    chunked = s0 is not None
    rows = V7X_SUBLANES
    in_specs = [pl.BlockSpec((rows, seq, SSM_UBLK), lambda j, i: (i, 0, j)),
                pl.BlockSpec((None, 2, 1, SSM_GBLOCK, SSM_GROUP, 2 * SSM_STATE), lambda j, i: (l, 0, j, 0, 0, 0)),
                pl.BlockSpec((None, 2, 1, 1, 2 * SSM_SBLK), lambda j, i: (l, 0, j, 0, 0)),
                pl.BlockSpec((None, 2, 1, SSM_GBLOCK, 2 * SSM_STATE, SSM_GROUP), lambda j, i: (l, 0, j, 0, 0, 0)),
                pl.BlockSpec((None, 1, SSM_UBLK), lambda j, i: (l, 0, j))]
    args = [u, w["ssm_b"], w["ssm_a"], w["ssm_c"], w["ssm_d"]]
    if chunked:
        in_specs.append(pl.BlockSpec((None, 2, 1, 1, 1, 2 * SSM_SBLK), lambda j, i: (l, 0, i, j, 0, 0)))
        args.append(s0)
    out_specs = [pl.BlockSpec((seq, rows, SSM_UBLK), lambda j, i: (0, i, j))]
    out_shape = [jax.ShapeDtypeStruct((seq, r, SSM_WIDTH), F32)]
    if want_final:
        out_specs.append(pl.BlockSpec((2, 1, rows, 2 * SSM_SBLK), lambda j, i: (0, j, i, 0)))
        out_shape.append(jax.ShapeDtypeStruct((2, SSM_NBLK, r, 2 * SSM_SBLK), F32))
    return pl.pallas_call(
        functools.partial(_ssm_kernel, chunked, want_final),
        grid=(SSM_NBLK, r // rows),
        in_specs=in_specs, out_specs=out_specs, out_shape=out_shape,
        scratch_shapes=[pltpu.VMEM((2, seq * rows, 2 * SSM_SBLK), F32), pltpu.VMEM((2, seq * rows, 2 * SSM_SBLK), BF16),
                        pltpu.VMEM((seq * rows, SSM_UBLK), F32), pltpu.VMEM((seq * rows, SSM_UBLK), F32),
                        pltpu.VMEM((2, SSM_UBLK, 2 * SSM_SBLK), BF16), pltpu.VMEM((2, 2 * SSM_SBLK, SSM_UBLK), BF16)],
        compiler_params=_params(2),
        name="ssm_latent" if chunked else "ssm_context",
    )(*args)


def _merge_kernel(last, x_ref, sh_ref, sc_ref, ga_ref, g_ref, oa_ref, ob_ref, oc_ref, y_ref,
                  wg_ref, wm_ref, wglu_ref, bglu_ref, wbr_ref, wout_ref, fn_ref, o_ref):
    x = x_ref[...]
    h = _modulated(x, g_ref, sh_ref, sc_ref).astype(BF16)
    y = y_ref[...]
    o_d = y * _sigmoid(_dot(y.astype(BF16), wglu_ref[...]) + bglu_ref[...])
    branches = (oa_ref[...].astype(F32), ob_ref[...].astype(F32), oc_ref[...].astype(F32), o_d)
    acc = None
    for n, o in enumerate(branches):
        gate = _dot(h, wg_ref[:, n * BRANCH_WIDTH:(n + 1) * BRANCH_WIDTH])
        br = _dot((o * (gate * _sigmoid(gate))).astype(BF16), wbr_ref[n])
        term = _sigmoid(_dot(h, wm_ref[:, n * D_MODEL:(n + 1) * D_MODEL])) * br
        acc = term if acc is None else acc + term
    xn = x + ga_ref[0] * _dot(acc.astype(BF16), wout_ref[...])
    if last:
        xn = _rms(xn) * fn_ref[...]
    o_ref[...] = xn


def _merge(x, shift, scale, gate, seq, oa, ob, oc, y, l, w, final_norm, last):
    n = x.shape[0]
    tm = MERGE_TM
    row = lambda i: (i, 0)
    mod_spec = _mod_spec(shift.shape[0], tm, seq)
    in_specs = [pl.BlockSpec((tm, D_MODEL), row), mod_spec, mod_spec, mod_spec, _layer_spec((1, D_MODEL), l),
                pl.BlockSpec((tm, 512), row), pl.BlockSpec((tm, 512), row), pl.BlockSpec((tm, 512), row),
                pl.BlockSpec((tm, 512), row),
                _layer_spec((D_MODEL, N_BRANCH * BRANCH_WIDTH), l), _layer_spec((D_MODEL, N_BRANCH * D_MODEL), l),
                _layer_spec((SSM_WIDTH, SSM_WIDTH), l), _layer_spec((1, SSM_WIDTH), l),
                _layer_spec((N_BRANCH, BRANCH_WIDTH, D_MODEL), l), _layer_spec((D_MODEL, D_MODEL), l),
                _const_spec((1, D_MODEL))]
    return pl.pallas_call(
        functools.partial(_merge_kernel, last),
        grid=(n // tm,),
        in_specs=in_specs,
        out_specs=pl.BlockSpec((tm, D_MODEL), row),
        out_shape=jax.ShapeDtypeStruct((n, D_MODEL), F32),
        compiler_params=_params(1),
        name="merge",
    )(x, shift, scale, gate, w["norm_g"], oa, ob, oc, y, w["wg"], w["wm"], w["wglu"], w["bglu"],
      w["wbr"], w["wout"], final_norm)


def _rope_tables(n_tok, rot_dim):
    f32 = np.float32
    rows = n_tok // GRID_W
    row = np.repeat(np.arange(rows, dtype=f32), GRID_W)
    col = np.tile(np.arange(GRID_W, dtype=f32), rows)
    quarter = rot_dim // 4
    inv = (f32(ROPE_THETA) ** (-np.arange(quarter, dtype=f32) / f32(quarter))).astype(f32)
    ang = np.concatenate([row[:, None] * inv, col[:, None] * inv], axis=-1).astype(f32)
    return np.cos(ang).astype(f32), np.sin(ang).astype(f32)


def _rope_lane_tables(n_tok):
    c, s = _rope_tables(n_tok, MLA_ROPE)
    ones = np.ones((n_tok, MLA_NOPE), np.float32)
    pad = MLA_HEAD_PAD - MLA_NOPE - MLA_ROPE
    c_mla = np.concatenate([ones, c, c, np.ones((n_tok, pad), np.float32)], axis=1)
    s_mla = np.concatenate([0.0 * ones, -s, s, np.zeros((n_tok, pad), np.float32)], axis=1)
    c, s = _rope_tables(n_tok, DIFF_HEAD_DIM)
    c_64 = np.concatenate([c, c, c, c], axis=1)
    s_64 = np.concatenate([-s, s, -s, s], axis=1)
    return tuple(jnp.asarray(a, F32) for a in (c_mla, s_mla, c_64, s_64))


def _in_output_half(v):
    zeros = jnp.zeros_like(v[..., 0, :])
    return jnp.stack([jnp.concatenate([v[..., 0, :], zeros], axis=-1),
                      jnp.concatenate([zeros, v[..., 1, :]], axis=-1)], axis=-2)


def _weights(w_in, p):
    w_t = jnp.swapaxes(w_in, 1, 2).reshape(DEPTH * D_IN, D_MODEL)
    wmix = _transposed_rows(w_t, _IN_MIX_START, _M_WIDTH, _M_WIDTH // 2)
    wg = _transposed_rows(w_t, _IN_GATE_START, N_BRANCH * BRANCH_WIDTH, W_IN_BLOCK)
    wm = _transposed_rows(w_t, _IN_MERGE_START, N_BRANCH * D_MODEL, W_IN_BLOCK)
    zeros = lambda wd: jnp.zeros((DEPTH, D_MODEL, wd), F32)
    kpe_lo = MLA_Q_LORA + MLA_KV_LORA
    whead = jnp.concatenate([w_in[:, :, :kpe_lo], zeros(MLA_NOPE), w_in[:, :, kpe_lo:_IN_MIX_START],
                             zeros(MLA_HEAD_PAD - MLA_NOPE - MLA_ROPE)], axis=2).astype(BF16)

    hq = MLA_NOPE + MLA_ROPE
    wqb = p["w_mla_q_b"].reshape(DEPTH, MLA_Q_LORA, MLA_HEADS, hq)
    wqb = jnp.pad(wqb, ((0, 0), (0, 0), (0, 0), (0, MLA_HEAD_PAD - hq))).reshape(DEPTH, MLA_Q_LORA, -1).astype(BF16)
    wkv = p["w_mla_kv_b"].reshape(DEPTH, MLA_KV_LORA, MLA_HEADS, MLA_NOPE + MLA_V)
    wk = jnp.pad(wkv[..., :MLA_NOPE], ((0, 0), (0, 0), (0, 0), (0, MLA_HEAD_PAD - MLA_NOPE)))
    wk = wk.reshape(DEPTH, MLA_KV_LORA, -1).astype(BF16)
    wv = _in_output_half(wkv[..., MLA_NOPE:].reshape(DEPTH, MLA_KV_LORA, MLA_HEADS // 2, 2, MLA_V))
    wv = wv.reshape(DEPTH, MLA_KV_LORA, -1).astype(BF16)
    egv = np.zeros((GQA_KV_WIDTH, 2 * GQA_KV_HEADS * V_PAD), np.float32)
    for g in range(GQA_KV_HEADS):
        for d in range(GQA_HEAD_DIM):
            egv[g * GQA_HEAD_DIM + d, (2 * g) * V_PAD + d] = 1.0
            egv[g * GQA_HEAD_DIM + d, (2 * g + 1) * V_PAD + GQA_HEAD_DIM + d] = 1.0
    egv = jnp.asarray(egv, BF16)
    gmat = jnp.asarray(np.kron(np.eye(GQA_HEADS), np.full((GQA_HEAD_DIM, GQA_HEAD_DIM), 1.0 / GQA_HEAD_DIM)), BF16)

    lam = lax.complex(p["ssm_a_re"], p["ssm_a_im"])
    dt = jnp.exp(p["ssm_log_dt"])[..., None]
    abar = jnp.exp(lam * dt)
    bbar = ((abar - 1.0) / lam)[..., None] * lax.complex(p["ssm_b_re"], p["ssm_b_im"])
    blk = lambda a: a.reshape((DEPTH, 2, SSM_NBLK, SSM_GBLOCK) + a.shape[3:])
    b_t = jnp.swapaxes(blk(bbar), -1, -2)
    ssm_b = jnp.concatenate([jnp.real(b_t), jnp.imag(b_t)], axis=-1).astype(BF16)
    ab = blk(abar).reshape(DEPTH, 2, SSM_NBLK, 1, SSM_SBLK)
    ssm_a = jnp.concatenate([jnp.real(ab), jnp.imag(ab)], axis=-1)
    c_re = jnp.swapaxes(blk(p["ssm_c_re"]), -1, -2)
    c_im = jnp.swapaxes(blk(p["ssm_c_im"]), -1, -2)
    ssm_c = jnp.concatenate([c_re, -c_im], axis=-2).astype(BF16)

    row = lambda a: a[:, None, :]
    return dict(
        norm_g=row(p["norm_g"]), whead=whead, wmix=wmix, qn=row(p["mla_q_norm"]), wqb=wqb,
        kvn=row(p["mla_kv_norm"]), wk=wk, wv=wv,
        gqn=row(jnp.tile(p["gqa_q_norm"], (1, GQA_HEADS))), gkn=row(jnp.tile(p["gqa_k_norm"], (1, GQA_KV_HEADS))),
        gmat=gmat, egv=egv,
        lam_parts=jnp.stack([p["diff_lq1"], p["diff_lk1"], p["diff_lq2"], p["diff_lk2"]], axis=1),
        subln=row(p["diff_subln"]),
        ssm_b=ssm_b, ssm_a=ssm_a, ssm_c=ssm_c, ssm_d=row(p["ssm_d"]),
        wg=wg, wm=wm, wglu=p["ssm_glu_w"].astype(BF16), bglu=row(p["ssm_glu_b"]),
        wbr=p["w_branch_out"].astype(BF16), wout=p["w_out"].astype(BF16),
    )


def _mixers(pr, b, t, l, w, lam_init, s0):
    q, k, v, dq, dk, dv, gq, gk, gv, u = pr[:10]
    three = lambda a: a.reshape(b, a.shape[0] // b, a.shape[-1])
    k, v, dk, dv, gk, gv = (three(a) for a in (k, v, dk, dv, gk, gv))
    mla_heads = tuple((hd, hd) for hd in range(MLA_HEADS))
    gqa_heads = tuple((hd // GQA_GROUP, 2 * (hd // GQA_GROUP) + hd % 2) for hd in range(GQA_HEADS))
    diff_specs = (_layer_spec((4, DIFF_HEAD_DIM), l), _layer_spec((1, DIFF_V), l))
    diff_extra = (w["lam_parts"], w["subln"])
    if s0 is None:
        o_a, o_b, o_c = _attention_short(lam_init, mla_heads, gqa_heads,
                                         (three(q), k, v, three(dq), dk, dv, three(gq), gk, gv),
                                         diff_specs, diff_extra)
    else:
        o_a = _attention(functools.partial(_attn_heads_kernel, mla_heads, MLA_HEAD_PAD), "attn_mla",
                         three(q), k, v)
        o_b = _attention(functools.partial(_diff_attn_kernel, lam_init), "attn_diff", three(dq), dk, dv,
                         extra_specs=diff_specs, extra=diff_extra)
        o_c = _attention(functools.partial(_attn_heads_kernel, gqa_heads, GQA_HEAD_DIM), "attn_gqa",
                         three(gq), gk, gv)
    n = b * t
    u_rows = u.reshape(n // SSM_SEQ, SSM_SEQ, SSM_WIDTH)
    if s0 is None:
        y_t, fin = _ssm(u_rows, l, w, None, True)
    else:
        (y_t,) = _ssm(u_rows, l, w, s0, False)
        fin = None
    y = jnp.swapaxes(y_t, 0, 1).reshape(n, SSM_WIDTH)
    flat = lambda a: a.reshape(n, 512)
    return flat(o_a), flat(o_b), flat(o_c), y, fin


def kernel(x_prompt, x_sample, cache_mla_ckv, cache_mla_krope, cache_diff_k, cache_diff_v, cache_gqa_k, cache_gqa_v, state_ssm, c, c_ctx, norm_g, w_mod, b_mod, w_in, mla_q_norm, w_mla_q_b, mla_kv_norm, w_mla_kv_b, diff_lq1, diff_lk1, diff_lq2, diff_lk2, diff_subln, gqa_q_norm, gqa_k_norm, ssm_a_re, ssm_a_im, ssm_log_dt, ssm_b_re, ssm_b_im, ssm_c_re, ssm_c_im, ssm_d, ssm_glu_w, ssm_glu_b, w_branch_out, w_out, final_norm):
    p = dict(norm_g=norm_g, mla_q_norm=mla_q_norm, w_mla_q_b=w_mla_q_b, mla_kv_norm=mla_kv_norm,
             w_mla_kv_b=w_mla_kv_b, diff_lq1=diff_lq1, diff_lk1=diff_lk1, diff_lq2=diff_lq2, diff_lk2=diff_lk2,
             diff_subln=diff_subln, gqa_q_norm=gqa_q_norm, gqa_k_norm=gqa_k_norm, ssm_a_re=ssm_a_re,
             ssm_a_im=ssm_a_im, ssm_log_dt=ssm_log_dt, ssm_b_re=ssm_b_re, ssm_b_im=ssm_b_im, ssm_c_re=ssm_c_re,
             ssm_c_im=ssm_c_im, ssm_d=ssm_d, ssm_glu_w=ssm_glu_w, ssm_glu_b=ssm_glu_b,
             w_branch_out=w_branch_out, w_out=w_out)
    bp, tp, _ = x_prompt.shape
    bs, ts, _ = x_sample.shape
    past = cache_mla_ckv.shape[2]
    assert tp == SSM_SEQ and ts % SSM_SEQ == 0 and ts // SSM_SEQ == V7X_SUBLANES
    rope = _rope_lane_tables(ts)
    w = _weights(w_in, p)

    cvec = jnp.concatenate([c_ctx[None], c, jnp.zeros((V7X_SUBLANES - 1 - bs, D_MODEL), F32)], axis=0)
    mod = _modulation(cvec, w_mod, b_mod)

    feature_major = lambda a: jnp.moveaxis(a, 2, -1).reshape(bs, DEPTH, -1, past)
    pad_lo = MLA_NOPE
    pad_hi = MLA_HEAD_PAD - MLA_NOPE - MLA_ROPE
    kpe_ctx = jnp.pad(feature_major(cache_mla_krope), ((0, 0), (0, 0), (pad_lo, pad_hi), (0, 0)))
    dk_ctx, gk_ctx, gv_ctx = (feature_major(a) for a in (cache_diff_k, cache_gqa_k, cache_gqa_v))
    dv_ctx = cache_diff_v
    s0_all = state_ssm.reshape(bs, DEPTH, 2, SSM_NBLK, SSM_GBLOCK * SSM_STATE, 2)
    s0_all = jnp.transpose(s0_all, (1, 2, 0, 3, 5, 4)).reshape(DEPTH, 2, bs, SSM_NBLK, 1, 2 * SSM_SBLK)

    xp = x_prompt.reshape(bp * tp, D_MODEL)
    xs = x_sample.reshape(bs * ts, D_MODEL)
    fn = final_norm[None]
    sides = ()
    states = []
    for l in range(DEPTH):
        lam_init = 0.8 - 0.6 * math.exp(-0.3 * l)
        last = l == DEPTH - 1
        sh, sc, ga = (mod[l, :, i * D_MODEL:(i + 1) * D_MODEL][:, None, :] for i in range(3))

        pr = _proj(xp, sh[0:1], sc[0:1], tp, l, w, prev_sides=sides)
        o_a, o_b, o_c, y, fin = _mixers(pr, bp, tp, l, w, lam_init, None)
        xp = _merge(xp, sh[0:1], sc[0:1], ga[0:1], tp, o_a, o_b, o_c, y, l, w, fn, last)
        sides = tuple(pr[10:])
        fin = fin.reshape(2, SSM_NBLK, bp, 2, SSM_GBLOCK, SSM_STATE)
        states.append(jnp.transpose(fin, (2, 0, 1, 4, 5, 3)).reshape(bp, 2, SSM_GROUPS, SSM_STATE, 2))

        pr = _proj(xs, sh[1:1 + bs], sc[1:1 + bs], ts, l, w, rope=rope,
                   cached=(cache_mla_ckv, kpe_ctx, dk_ctx, dv_ctx, gk_ctx, gv_ctx))
        o_a, o_b, o_c, y, _ = _mixers(pr, bs, ts, l, w, lam_init, s0_all)
        xs = _merge(xs, sh[1:1 + bs], sc[1:1 + bs], ga[1:1 + bs], ts, o_a, o_b, o_c, y, l, w, fn, last)

    ckv_s, kpe_t, dk_t, dv_s, gk_t, gv_t = sides
    untranspose = lambda a: jnp.swapaxes(a, 2, 3)
    return (xp.reshape(bp, tp, D_MODEL), xs.reshape(bs, ts, D_MODEL),
            ckv_s, untranspose(kpe_t),
            untranspose(dk_t).reshape(bp, DEPTH, tp, DIFF_HEADS, 2, DIFF_HEAD_DIM), dv_s,
            untranspose(gk_t).reshape(bp, DEPTH, tp, GQA_KV_HEADS, GQA_HEAD_DIM),
            untranspose(gv_t).reshape(bp, DEPTH, tp, GQA_KV_HEADS, GQA_HEAD_DIM),
            jnp.stack(states, axis=1))
```

```python
import functools
import math

import jax
import jax.numpy as jnp
import numpy as np
from jax import lax
from jax.experimental import pallas as pl
from jax.experimental.pallas import tpu as pltpu

F32 = jnp.float32
BF16 = jnp.bfloat16

D_MODEL = 1024
DEPTH = 2
GRID_W = 64
ROPE_THETA = 10000.0
EPS = 1e-6

MLA_HEADS = 8
MLA_NOPE = 64
MLA_ROPE = 32
MLA_V = 64
MLA_Q_LORA = 256
MLA_KV_LORA = 128
MLA_SCALE = (MLA_NOPE + MLA_ROPE) ** -0.5
MLA_HEAD_PAD = 128
DIFF_HEADS = 4
DIFF_HEAD_DIM = 64
DIFF_V = 2 * DIFF_HEAD_DIM
DIFF_SCALE = DIFF_HEAD_DIM ** -0.5
GQA_HEADS = 8
GQA_KV_HEADS = 2
GQA_HEAD_DIM = 64
GQA_GROUP = GQA_HEADS // GQA_KV_HEADS
GQA_KV_WIDTH = GQA_KV_HEADS * GQA_HEAD_DIM
GQA_SCALE = GQA_HEAD_DIM ** -0.5
SSM_WIDTH = 512
SSM_GROUP = 16
SSM_GROUPS = SSM_WIDTH // SSM_GROUP
SSM_STATE = 64
N_BRANCH = 4
BRANCH_WIDTH = 512

IN_SPLITS = (MLA_Q_LORA, MLA_KV_LORA, MLA_ROPE, 512, 512, 512, 512, 128, 128, SSM_WIDTH,
             N_BRANCH * BRANCH_WIDTH, N_BRANCH * D_MODEL)
D_IN = sum(IN_SPLITS)

V7X_LANES = 128
V7X_SUBLANES = 8
V7X_VMEM_LIMIT_BYTES = 56 * 1024 * 1024

_H_QA = (0, 256)
_H_KVA = (256, 384)
_H_KPE = (384, 512)
_H_WIDTH = 512
_M_DQ = (0, 512)
_M_DK = (512, 1024)
_M_DV = (1024, 1536)
_M_GQ = (1536, 2048)
_M_GK = (2048, 2176)
_M_GV = (2176, 2304)
_M_U = (2304, 2816)
_M_WIDTH = 2816
_IN_MIX_START = MLA_Q_LORA + MLA_KV_LORA + MLA_ROPE
_IN_GATE_START = _IN_MIX_START + _M_WIDTH
_IN_MERGE_START = _IN_GATE_START + N_BRANCH * BRANCH_WIDTH

SSM_GBLOCK = 8
SSM_NBLK = SSM_GROUPS // SSM_GBLOCK
SSM_SBLK = SSM_GBLOCK * SSM_STATE
SSM_UBLK = SSM_GBLOCK * SSM_GROUP
SSM_SEQ = 256
SSM_SLICE = 32

V_PAD = 128

W_IN_BLOCK = 1024
PROJ_TM = 512
MERGE_TM = 512
MERGE_SUBTILES = 2
ATTN_TQ_LONG = 512
ATTN_SEQ_PER_STEP = 2
ATTN_GROUP_LONG = 2


def _params(n_axes, vmem=V7X_VMEM_LIMIT_BYTES):
    return pltpu.CompilerParams(dimension_semantics=("arbitrary",) * n_axes, vmem_limit_bytes=vmem)


def _const_spec(shape):
    zeros = (0,) * len(shape)
    return pl.BlockSpec(shape, lambda *_: zeros, pipeline_mode=pl.Buffered(1))


def _layer_spec(shape, l):
    zeros = (0,) * len(shape)
    return pl.BlockSpec((None,) + tuple(shape), lambda *_: (l,) + zeros, pipeline_mode=pl.Buffered(1))


def _mod_spec(n_rows, tm, seq):
    if n_rows == 1:
        return pl.BlockSpec((1, 1, D_MODEL), lambda i: (0, 0, 0))
    return pl.BlockSpec((1, 1, D_MODEL), lambda i: (i * tm // seq, 0, 0))


def _dot(a, b):
    return jnp.dot(a, b, preferred_element_type=F32)


def _dot_nt(a, b):
    return lax.dot_general(a, b, (((1,), (1,)), ((), ())), preferred_element_type=F32)


def _rms(x):
    return x * lax.rsqrt(jnp.mean(x * x, axis=-1, keepdims=True) + EPS)


def _sigmoid(x):
    return 0.5 * jnp.tanh(0.5 * x) + 0.5


def _gelu_tanh(x):
    return 0.5 * x * (1.0 + jnp.tanh(math.sqrt(2.0 / math.pi) * (x + 0.044715 * (x * x * x))))


def _group_mean_sq(x, gmat):
    return _dot((x * x).astype(BF16), gmat)


def _rope_chunk(x, cos, sin, half):
    lane = lax.broadcasted_iota(jnp.int32, x.shape, 1)
    first = (lane % (2 * half)) < half
    swapped = jnp.where(first, pltpu.roll(x, V7X_LANES - half, 1), pltpu.roll(x, half, 1))
    return x * cos + swapped * sin


def _transpose_cast_kernel(w_ref, o_ref):
    o_ref[...] = w_ref[...].T.astype(BF16)


def _transposed_rows(w_t, row0, n_rows, blk):
    return pl.pallas_call(
        _transpose_cast_kernel,
        grid=(DEPTH, n_rows // blk),
        in_specs=[pl.BlockSpec((pl.Element(blk), pl.Element(D_MODEL)),
                               lambda l, j: (pl.multiple_of(l * D_IN + row0 + j * blk, V7X_SUBLANES), 0))],
        out_specs=pl.BlockSpec((None, D_MODEL, blk), lambda l, j: (l, 0, j)),
        out_shape=jax.ShapeDtypeStruct((DEPTH, D_MODEL, n_rows), BF16),
        compiler_params=_params(2),
        name="w_in_columns",
    )(w_t)


def _mod_kernel(c_ref, w_ref, b_ref, o_ref):
    c = c_ref[...]
    a = (c * _sigmoid(c)).astype(BF16)
    o_ref[0] = _dot(a, w_ref[0].astype(BF16)) + b_ref[0]


def _modulation(cvec, w_mod, b_mod):
    tn = 512
    return pl.pallas_call(
        _mod_kernel,
        grid=(DEPTH, 3 * D_MODEL // tn),
        in_specs=[pl.BlockSpec((V7X_SUBLANES, D_MODEL), lambda l, j: (0, 0)),
                  pl.BlockSpec((1, D_MODEL, tn), lambda l, j: (l, 0, j)),
                  pl.BlockSpec((1, 1, tn), lambda l, j: (l, 0, j))],
        out_specs=pl.BlockSpec((1, V7X_SUBLANES, tn), lambda l, j: (l, 0, j)),
        out_shape=jax.ShapeDtypeStruct((DEPTH, V7X_SUBLANES, 3 * D_MODEL), F32),
        compiler_params=_params(2),
        name="modulation",
    )(cvec, w_mod, b_mod.reshape(DEPTH, 1, 3 * D_MODEL))


def _modulated(x, g_ref, sh_ref, sc_ref):
    return (_rms(x) * g_ref[...]) * (1.0 + sc_ref[0]) + sh_ref[0]


def _proj_kernel(latent, n_prev, first_layer, *refs):
    (x_ref, sh_ref, sc_ref, g_ref, wh_ref, wx_ref, qn_ref, wqb_ref, kvn_ref, wk_ref, wv_ref,
     gqn_ref, gkn_ref, gm_ref, egv_ref) = refs[:15]
    pos = 15
    if latent:
        c_mla, s_mla, c_64, s_64 = refs[pos:pos + 4]
        cached = refs[pos + 4:pos + 10]
        pos += 10
    pos += n_prev
    (q_o, k_o, v_o, dq_o, dk_o, dv_o, gq_o, gk_o, gv_o, u_o) = refs[pos:pos + 10]
    pos += 10
    if not latent:
        ckv_f, kpe_t, dk_t, dv_f, gk_t, gv_t = refs[pos:pos + 6]

    def mla_keys_values(ckv_b, kpe, k_out, v_out):
        kn = _dot(ckv_b, wk_ref[...])
        for hd in range(MLA_HEADS):
            sl = slice(hd * MLA_HEAD_PAD, (hd + 1) * MLA_HEAD_PAD)
            k_out[:, sl] = (kn[:, sl] + kpe).astype(BF16)
        v_out[...] = _dot(ckv_b, wv_ref[...]).astype(BF16)

    def cached_tokens():
        ckv_c, kpe_t, dk_t, dv_c, gk_t, gv_t = cached
        mla_keys_values(ckv_c[...].astype(BF16), kpe_t[...].T, k_o, v_o)
        dk_o[...] = dk_t[...].T.astype(BF16)
        for hd in range(DIFF_HEADS):
            dv_o[:, hd * DIFF_V:(hd + 1) * DIFF_V] = dv_c[:, hd, :].astype(BF16)
        gk_o[...] = gk_t[...].T.astype(BF16)
        gv_o[...] = _dot(gv_t[...].T.astype(BF16), egv_ref[...]).astype(BF16)

    def new_tokens():
        _proj_new_tokens(latent, first_layer, refs[:15], (c_mla, s_mla, c_64, s_64) if latent else None,
                         refs[pos - 10:pos], refs[pos:pos + 6] if not latent else None, mla_keys_values)

    if latent:
        pl.when(pl.program_id(1) == 0)(cached_tokens)
        pl.when(pl.program_id(1) > 0)(new_tokens)
    else:
        new_tokens()


def _proj_new_tokens(latent, first_layer, in_refs, rope_refs, out_refs, side_refs, mla_keys_values):
    (x_ref, sh_ref, sc_ref, g_ref, wh_ref, wx_ref, qn_ref, wqb_ref, kvn_ref, wk_ref, wv_ref,
     gqn_ref, gkn_ref, gm_ref, egv_ref) = in_refs
    (q_o, k_o, v_o, dq_o, dk_o, dv_o, gq_o, gk_o, gv_o, u_o) = out_refs
    if latent:
        c_mla, s_mla, c_64, s_64 = rope_refs
    else:
        ckv_f, kpe_t, dk_t, dv_f, gk_t, gv_t = side_refs

    h = _modulated(x_ref[...], g_ref, sh_ref, sc_ref).astype(BF16)
    tm = h.shape[0]

    p_head = _dot(h, wh_ref[...])
    p_mix = _dot(h, wx_ref[...])

    def head(cols):
        return p_head[:, cols[0]:cols[1]]

    def mix(cols):
        return p_mix[:, cols[0]:cols[1]]

    def rope_mla(x):
        return _rope_chunk(x, c_mla[...], s_mla[...], MLA_ROPE // 2) if latent else x

    def rope_64(x):
        if not latent:
            return x
        c, s = c_64[...], s_64[...]
        chunks = [_rope_chunk(x[:, i:i + V7X_LANES], c, s, DIFF_HEAD_DIM // 2)
                  for i in range(0, x.shape[1], V7X_LANES)]
        return chunks[0] if len(chunks) == 1 else jnp.concatenate(chunks, axis=1)

    def per_sequence(x):
        return [x[b * SSM_SEQ:(b + 1) * SSM_SEQ] for b in range(tm // SSM_SEQ)]

    def slot(b):
        return (b,) if first_layer is None else (b, first_layer)

    def zero_other_slots(ref):
        if first_layer is not None:
            for b in range(ref.shape[0]):
                for other in range(ref.shape[1]):
                    if other != first_layer:
                        ref[b, other] = jnp.zeros(ref.shape[2:], ref.dtype)

    qn = (_rms(head(_H_QA)) * qn_ref[...]).astype(BF16)
    q = _dot(qn, wqb_ref[...])
    for hd in range(MLA_HEADS):
        sl = slice(hd * MLA_HEAD_PAD, (hd + 1) * MLA_HEAD_PAD)
        q_o[:, sl] = (rope_mla(q[:, sl]) * MLA_SCALE).astype(BF16)
    ckv = _rms(head(_H_KVA)) * kvn_ref[...]
    kpe = head(_H_KPE)
    if not latent:
        for b, (c_b, k_b) in enumerate(zip(per_sequence(ckv), per_sequence(kpe))):
            ckv_f[slot(b)] = c_b
            kpe_t[slot(b)] = k_b.T[MLA_NOPE:MLA_NOPE + MLA_ROPE, :]
        zero_other_slots(ckv_f)
        zero_other_slots(kpe_t)
    mla_keys_values(ckv.astype(BF16), rope_mla(kpe), k_o, v_o)

    dq_o[...] = (rope_64(mix(_M_DQ)) * DIFF_SCALE).astype(BF16)
    dk = rope_64(mix(_M_DK))
    dv = mix(_M_DV)
    dk_o[...] = dk.astype(BF16)
    dv_o[...] = dv.astype(BF16)

    gq = mix(_M_GQ)
    gq = gq * lax.rsqrt(_group_mean_sq(gq, gm_ref[...]) + EPS) * gqn_ref[...]
    gq_o[...] = (rope_64(gq) * GQA_SCALE).astype(BF16)
    gk = mix(_M_GK)
    kw = GQA_KV_WIDTH
    gk = gk * lax.rsqrt(_group_mean_sq(gk, gm_ref[:kw, :kw]) + EPS) * gkn_ref[...]
    gv = mix(_M_GV)
    if not latent:
        for b, (dk_b, dv_b, gk_b, gv_b) in enumerate(zip(*(per_sequence(a) for a in (dk, dv, gk, gv)))):
            dk_t[slot(b)] = dk_b.T
            for hd in range(DIFF_HEADS):
                dv_f[slot(b) + (slice(None), hd, slice(None))] = dv_b[:, hd * DIFF_V:(hd + 1) * DIFF_V]
            gk_t[slot(b)] = gk_b.T
            gv_t[slot(b)] = gv_b.T
        for ref in (dk_t, dv_f, gk_t, gv_t):
            zero_other_slots(ref)
    gk_o[...] = rope_64(gk).astype(BF16)
    gv_o[...] = _dot(gv.astype(BF16), egv_ref[...]).astype(BF16)

    u_o[...] = mix(_M_U)


def _side_shapes(n_seq):
    return [((n_seq, DEPTH, SSM_SEQ, MLA_KV_LORA), (SSM_SEQ, MLA_KV_LORA)),
            ((n_seq, DEPTH, MLA_ROPE, SSM_SEQ), (MLA_ROPE, SSM_SEQ)),
            ((n_seq, DEPTH, 512, SSM_SEQ), (512, SSM_SEQ)),
            ((n_seq, DEPTH, SSM_SEQ, DIFF_HEADS, DIFF_V), (SSM_SEQ, DIFF_HEADS, DIFF_V)),
            ((n_seq, DEPTH, GQA_KV_WIDTH, SSM_SEQ), (GQA_KV_WIDTH, SSM_SEQ)),
            ((n_seq, DEPTH, GQA_KV_WIDTH, SSM_SEQ), (GQA_KV_WIDTH, SSM_SEQ))]


def _proj(x, shift, scale, seq, l, w, rope=None, cached=None, prev_sides=()):
    n = x.shape[0]
    tm = PROJ_TM
    latent = cached is not None
    kw = GQA_KV_WIDTH
    if latent:
        tiles = seq // tm
        assert cached[0].shape[2] == tm
        grid = (n // seq, 1 + tiles)
        row = lambda b, j: (b * tiles + jnp.maximum(j - 1, 0), 0)
        kv_row = lambda b, j: (b * (1 + tiles) + j, 0)
        n_kv = (n // seq) * (cached[0].shape[2] + seq)
        mod_spec = pl.BlockSpec((1, 1, D_MODEL), lambda b, j: (b, 0, 0))
    else:
        grid = (n // tm,)
        row = kv_row = lambda i: (i, 0)
        n_kv = n
        mod_spec = _mod_spec(shift.shape[0], tm, seq)
    in_specs = [pl.BlockSpec((tm, D_MODEL), row), mod_spec, mod_spec,
                _layer_spec((1, D_MODEL), l), _layer_spec((D_MODEL, _H_WIDTH), l), _layer_spec((D_MODEL, _M_WIDTH), l),
                _layer_spec((1, MLA_Q_LORA), l), _layer_spec((MLA_Q_LORA, MLA_HEADS * MLA_HEAD_PAD), l),
                _layer_spec((1, MLA_KV_LORA), l), _layer_spec((MLA_KV_LORA, MLA_HEADS * MLA_HEAD_PAD), l),
                _layer_spec((MLA_KV_LORA, MLA_HEADS * V_PAD), l),
                _layer_spec((1, 512), l), _layer_spec((1, kw), l), _const_spec((512, 512)),
                _const_spec((kw, 2 * GQA_KV_HEADS * V_PAD))]
    args = [x, shift, scale, w["norm_g"], w["whead"], w["wmix"], w["qn"], w["wqb"], w["kvn"], w["wk"], w["wv"],
            w["gqn"], w["gkn"], w["gmat"], w["egv"]]
    if latent:
        tab = pl.BlockSpec((tm, V7X_LANES), lambda b, j: (jnp.maximum(j - 1, 0), 0))
        in_specs += [tab] * 4
        args += list(rope)
        in_specs += [pl.BlockSpec((None, None) + c.shape[2:], lambda b, j, nd=c.ndim: (b, l) + (0,) * (nd - 2))
                     for c in cached]
        args += list(cached)
    n_fixed = len(args)
    in_specs += [pl.BlockSpec(memory_space=pl.ANY)] * len(prev_sides)
    args += list(prev_sides)
    widths = [(1024, BF16, False), (1024, BF16, True), (MLA_HEADS * V_PAD, BF16, True), (512, BF16, False),
              (512, BF16, True), (512, BF16, True), (512, BF16, False), (kw, BF16, True),
              (2 * GQA_KV_HEADS * V_PAD, BF16, True), (512, F32, False)]
    out_specs = [pl.BlockSpec((tm, wd), kv_row if kv else row) for wd, _, kv in widths]
    out_shape = [jax.ShapeDtypeStruct((n_kv if kv else n, wd), dt) for wd, dt, kv in widths]
    aliases = {}
    first_layer = None
    if not latent:
        per_tile = tm // SSM_SEQ
        first_layer = None if prev_sides else l
        for full, blk in _side_shapes(n // SSM_SEQ):
            zeros = (0,) * len(blk)
            if prev_sides:
                out_specs.append(pl.BlockSpec((per_tile, None) + blk, lambda i, zeros=zeros: (i, l) + zeros))
            else:
                out_specs.append(pl.BlockSpec((per_tile, DEPTH) + blk, lambda i, zeros=zeros: (i, 0) + zeros))
            out_shape.append(jax.ShapeDtypeStruct(full, F32))
        aliases = {n_fixed + k: len(widths) + k for k in range(len(prev_sides))}
    return pl.pallas_call(
        functools.partial(_proj_kernel, latent, len(prev_sides), first_layer),
        grid=grid,
        in_specs=in_specs, out_specs=out_specs, out_shape=out_shape,
        input_output_aliases=aliases,
        compiler_params=_params(len(grid)),
        name="proj_latent" if latent else "proj_context",
    )(*args)


def _softmax_maps(maps):
    scores = [_dot_nt(q, k) for q, k, _ in maps]
    tops = [jnp.max(s, axis=-1, keepdims=True) for s in scores]
    weights = [jnp.exp(s - m).astype(BF16) for s, m in zip(scores, tops)]
    outs = []
    for p, (_, _, v) in zip(weights, maps):
        ones = jnp.ones((v.shape[0], V7X_LANES), BF16)
        o = _dot(p, jnp.concatenate([v, ones], axis=1))
        outs.append((o[:, :V7X_LANES], o[:, V7X_LANES:]))
    return outs


def _head_maps(heads, q_width, q_ref, k_ref, v_ref, part):
    return [(q_ref[b, :, hd * q_width:(hd + 1) * q_width],
             k_ref[b, :, heads[hd][0] * q_width:(heads[hd][0] + 1) * q_width],
             v_ref[b, :, heads[hd][1] * V_PAD:(heads[hd][1] + 1) * V_PAD]) for b, hd in part]


def _head_outputs(outs, part, o_ref):
    terms = [o * (1.0 / total) for o, total in outs]
    for (b, hd), low, high in zip(part[0::2], terms[0::2], terms[1::2]):
        o_ref[b, :, (hd // 2) * V7X_LANES:(hd // 2 + 1) * V7X_LANES] = (low + high).astype(BF16)


def _diff_maps(q_ref, k_ref, v_ref, part):
    d = DIFF_HEAD_DIM
    return [(q_ref[b, :, c * d:(c + 1) * d], k_ref[b, :, c * d:(c + 1) * d],
             v_ref[b, :, hd * DIFF_V:(hd + 1) * DIFF_V])
            for b, hd in part for c in (2 * hd, 2 * hd + 1)]


def _diff_lambda(lp_ref, lam_init):
    lp = lp_ref[...]
    return (jnp.exp(jnp.sum(lp[0:1] * lp[1:2], axis=-1, keepdims=True))
            - jnp.exp(jnp.sum(lp[2:3] * lp[3:4], axis=-1, keepdims=True)) + lam_init)


def _diff_outputs(outs, part, lam, lam_init, sub_ref, o_ref):
    mixed = [o1 * (1.0 / t1) - o2 * (lam / t2) for (o1, t1), (o2, t2) in zip(outs[0::2], outs[1::2])]
    normed = [_rms(o) * sub_ref[...] * (1.0 - lam_init) for o in mixed]
    for (b, hd), o in zip(part, normed):
        o_ref[b, :, hd * DIFF_V:(hd + 1) * DIFF_V] = o.astype(BF16)


def _items(q_ref, n_heads):
    return [(b, hd) for b in range(q_ref.shape[0]) for hd in range(n_heads)]


def _attn_heads_kernel(heads, q_width, q_ref, k_ref, v_ref, o_ref):
    items = _items(q_ref, len(heads))
    group = ATTN_GROUP_LONG
    for g0 in range(0, len(items), group):
        part = items[g0:g0 + group]
        _head_outputs(_softmax_maps(_head_maps(heads, q_width, q_ref, k_ref, v_ref, part)), part, o_ref)


def _diff_attn_kernel(lam_init, q_ref, k_ref, v_ref, lp_ref, sub_ref, o_ref):
    lam = _diff_lambda(lp_ref, lam_init)
    items = _items(q_ref, DIFF_HEADS)
    group = ATTN_GROUP_LONG // 2
    for g0 in range(0, len(items), group):
        part = items[g0:g0 + group]
        _diff_outputs(_softmax_maps(_diff_maps(q_ref, k_ref, v_ref, part)), part, lam, lam_init, sub_ref, o_ref)


def _attn_short_kernel(lam_init, mla_heads, gqa_heads, qa_ref, ka_ref, va_ref, qb_ref, kb_ref, vb_ref,
                       qc_ref, kc_ref, vc_ref, lp_ref, sub_ref, oa_ref, ob_ref, oc_ref):
    part_a, part_b, part_c = _items(qa_ref, len(mla_heads)), _items(qb_ref, DIFF_HEADS), _items(qc_ref, len(gqa_heads))
    maps_a = _head_maps(mla_heads, MLA_HEAD_PAD, qa_ref, ka_ref, va_ref, part_a)
    maps_b = _diff_maps(qb_ref, kb_ref, vb_ref, part_b)
    maps_c = _head_maps(gqa_heads, GQA_HEAD_DIM, qc_ref, kc_ref, vc_ref, part_c)
    outs = _softmax_maps(maps_a + maps_b + maps_c)
    n_a, n_b = len(maps_a), len(maps_b)
    _head_outputs(outs[:n_a], part_a, oa_ref)
    _diff_outputs(outs[n_a:n_a + n_b], part_b, _diff_lambda(lp_ref, lam_init), lam_init, sub_ref, ob_ref)
    _head_outputs(outs[n_a + n_b:], part_c, oc_ref)


def _attention_short(lam_init, mla_heads, gqa_heads, qkv, extra_specs, extra):
    b, t, _ = qkv[0].shape
    nb = ATTN_SEQ_PER_STEP
    whole = lambda i: (i, 0, 0)
    in_specs = [pl.BlockSpec((nb, t, a.shape[2]), whole) for a in qkv] + list(extra_specs)
    return pl.pallas_call(
        functools.partial(_attn_short_kernel, lam_init, mla_heads, gqa_heads),
        grid=(b // nb,),
        in_specs=in_specs,
        out_specs=[pl.BlockSpec((nb, t, 512), whole)] * 3,
        out_shape=[jax.ShapeDtypeStruct((b, t, 512), BF16)] * 3,
        compiler_params=_params(1),
        name="attn_context",
    )(*qkv, *extra)


def _attention(kernel, name, q, k, v, extra_specs=(), extra=()):
    b, t, wq = q.shape
    s = k.shape[1]
    tq = ATTN_TQ_LONG
    in_specs = [pl.BlockSpec((1, tq, wq), lambda i, j: (i, j, 0)),
                pl.BlockSpec((1, s, k.shape[2]), lambda i, j: (i, 0, 0)),
                pl.BlockSpec((1, s, v.shape[2]), lambda i, j: (i, 0, 0))]
    in_specs += list(extra_specs)
    return pl.pallas_call(
        kernel,
        grid=(b, t // tq),
        in_specs=in_specs,
        out_specs=pl.BlockSpec((1, tq, 512), lambda i, j: (i, j, 0)),
        out_shape=jax.ShapeDtypeStruct((b, t, 512), BF16),
        compiler_params=_params(2),
        name=name,
    )(q, k, v, *extra)


def _ssm_kernel(chunked, want_final, *refs):
    u_ref, bg_ref, ab_ref, cg_ref, d_ref = refs[:5]
    pos = 5
    if chunked:
        s0_ref = refs[pos]
        pos += 1
    y_ref = refs[pos]
    pos += 1
    if want_final:
        fin_ref = refs[pos]
        pos += 1
    hs_ref, hb_ref, yacc_ref, ut_ref, bm_ref, cm_ref = refs[pos:pos + 6]

    @pl.when(pl.program_id(1) == 0)
    def _():
        bm_ref[...] = jnp.zeros(bm_ref.shape, bm_ref.dtype)
        cm_ref[...] = jnp.zeros(cm_ref.shape, cm_ref.dtype)
        for dr in range(2):
            for g in range(SSM_GBLOCK):
                ins = slice(g * SSM_GROUP, (g + 1) * SSM_GROUP)
                for part in range(2):
                    states = slice(part * SSM_SBLK + g * SSM_STATE, part * SSM_SBLK + (g + 1) * SSM_STATE)
                    bm_ref[dr, ins, states] = bg_ref[dr, 0, g, :, part * SSM_STATE:(part + 1) * SSM_STATE]
                    cm_ref[dr, states, ins] = cg_ref[dr, 0, g, part * SSM_STATE:(part + 1) * SSM_STATE, :]

    seq, rows, sb = SSM_SEQ, V7X_SUBLANES, SSM_SBLK
    n_slices = seq // SSM_SLICE
    slice_rows = SSM_SLICE * rows

    def visited(dr, stage):
        return stage if dr == 0 else n_slices - 1 - stage

    def rows_of(k):
        return slice(k * slice_rows, (k + 1) * slice_rows)

    def u_rows(k, first_visit):
        if first_visit:
            block = u_ref[:, k * SSM_SLICE:(k + 1) * SSM_SLICE, :]
            ut_ref[rows_of(k), :] = jnp.swapaxes(block, 0, 1).reshape(slice_rows, SSM_UBLK)
        return ut_ref[rows_of(k), :]

    def input_proj(stage):
        for dr in range(2):
            k = visited(dr, stage)
            hs_ref[dr, rows_of(k), :] = _dot(u_rows(k, stage < n_slices // 2).astype(BF16), bm_ref[dr])

    def output_proj(stage):
        for dr in range(2):
            r = rows_of(visited(dr, stage))
            part = _dot(hb_ref[dr, r, :], cm_ref[dr])
            if stage < n_slices // 2:
                yacc_ref[r, :] = part
            else:
                yacc_ref[r, :] += part

    a_parts = []
    for dr in range(2):
        a = ab_ref[dr, 0]
        a_parts.append((jnp.broadcast_to(a[:, :sb], (rows, sb)), jnp.broadcast_to(a[:, sb:], (rows, sb))))

    def store_pair(dr, t_low, low, high):
        pair_rows = slice(t_low * rows, (t_low + 2) * rows)
        hb_ref[dr, pair_rows, :sb] = jnp.concatenate([low[0], high[0]], axis=0).astype(BF16)
        hb_ref[dr, pair_rows, sb:] = jnp.concatenate([low[1], high[1]], axis=0).astype(BF16)

    def step_pairs(stage, update, store):
        for step in range(0, SSM_SLICE, 2):
            i0 = stage * SSM_SLICE + step
            states = [[None, None], [None, None]]
            for sub in range(2):
                for dr, t in enumerate((i0 + sub, seq - 1 - i0 - sub)):
                    states[dr][sub] = update(dr, slice(t * rows, (t + 1) * rows))
            if store:
                store_pair(0, i0, states[0][0], states[0][1])
                store_pair(1, seq - 2 - i0, states[1][1], states[1][0])

    def scan_slice(stage, carry):
        carry = list(carry)

        def update(dr, sl):
            ar, ai = a_parts[dr]
            hr, hi = carry[2 * dr], carry[2 * dr + 1]
            b = hs_ref[dr, sl, :]
            nr = ar * hr - ai * hi + b[:, :sb]
            ni = ar * hi + ai * hr + b[:, sb:]
            if chunked:
                hs_ref[dr, sl, :sb] = nr
                hs_ref[dr, sl, sb:] = ni
            carry[2 * dr], carry[2 * dr + 1] = nr, ni
            return nr, ni

        step_pairs(stage, update, store=not chunked)
        return tuple(carry)

    def fix_slice(stage, carry):
        carry = list(carry)

        def update(dr, sl):
            ar, ai = a_parts[dr]
            zr, zi = carry[2 * dr], carry[2 * dr + 1]
            nr = ar * zr - ai * zi
            ni = ar * zi + ai * zr
            carry[2 * dr], carry[2 * dr + 1] = nr, ni
            h = hs_ref[dr, sl, :]
            return h[:, :sb] + nr, h[:, sb:] + ni

        step_pairs(stage, update, store=True)
        return tuple(carry)

    zero = jnp.zeros((rows, sb), F32)
    fin = (zero, zero, zero, zero)
    input_proj(0)
    for stage in range(n_slices):
        if stage + 1 < n_slices:
            input_proj(stage + 1)
        fin = scan_slice(stage, fin)
        if not chunked and stage >= 1:
            output_proj(stage - 1)

    if want_final:
        for dr in range(2):
            fin_ref[dr, 0, :, :sb] = fin[2 * dr]
            fin_ref[dr, 0, :, sb:] = fin[2 * dr + 1]

    if chunked:
        entry = []
        for dr in range(2):
            ar, ai = a_parts[dr]
            pr, pi = ar[0:1], ai[0:1]
            for _ in range(int(math.log2(seq))):
                pr, pi = pr * pr - pi * pi, 2.0 * pr * pi
            fr, fi = fin[2 * dr], fin[2 * dr + 1]
            s0 = s0_ref[dr, 0, 0]
            er, ei = s0[:, :sb], s0[:, sb:]
            order = range(rows) if dr == 0 else range(rows - 1, -1, -1)
            rows_r, rows_i = [None] * rows, [None] * rows
            for c in order:
                rows_r[c], rows_i[c] = er, ei
                er, ei = (pr * er - pi * ei + fr[c:c + 1], pr * ei + pi * er + fi[c:c + 1])
            entry += [jnp.concatenate(rows_r, axis=0), jnp.concatenate(rows_i, axis=0)]
        carry = tuple(entry)
        for stage in range(n_slices):
            carry = fix_slice(stage, carry)
            if stage >= 1:
                output_proj(stage - 1)
    output_proj(n_slices - 1)

    y = _gelu_tanh(yacc_ref[...] + d_ref[...] * ut_ref[...])
    y_ref[...] = y.reshape(seq, rows, SSM_UBLK)


def _ssm(u, l, w, s0, want_final):
    r, seq, _ = u.shape
    chunked = s0 is not None
    rows = V7X_SUBLANES
    in_specs = [pl.BlockSpec((rows, seq, SSM_UBLK), lambda j, i: (i, 0, j)),
                pl.BlockSpec((None, 2, 1, SSM_GBLOCK, SSM_GROUP, 2 * SSM_STATE), lambda j, i: (l, 0, j, 0, 0, 0)),
                pl.BlockSpec((None, 2, 1, 1, 2 * SSM_SBLK), lambda j, i: (l, 0, j, 0, 0)),
                pl.BlockSpec((None, 2, 1, SSM_GBLOCK, 2 * SSM_STATE, SSM_GROUP), lambda j, i: (l, 0, j, 0, 0, 0)),
                pl.BlockSpec((None, 1, SSM_UBLK), lambda j, i: (l, 0, j))]
    args = [u, w["ssm_b"], w["ssm_a"], w["ssm_c"], w["ssm_d"]]
    if chunked:
        in_specs.append(pl.BlockSpec((None, 2, 1, 1, 1, 2 * SSM_SBLK), lambda j, i: (l, 0, i, j, 0, 0)))
        args.append(s0)
    out_specs = [pl.BlockSpec((seq, rows, SSM_UBLK), lambda j, i: (0, i, j))]
    out_shape = [jax.ShapeDtypeStruct((seq, r, SSM_WIDTH), F32)]
    if want_final:
        out_specs.append(pl.BlockSpec((2, 1, rows, 2 * SSM_SBLK), lambda j, i: (0, j, i, 0)))
        out_shape.append(jax.ShapeDtypeStruct((2, SSM_NBLK, r, 2 * SSM_SBLK), F32))
    return pl.pallas_call(
        functools.partial(_ssm_kernel, chunked, want_final),
        grid=(SSM_NBLK, r // rows),
        in_specs=in_specs, out_specs=out_specs, out_shape=out_shape,
        scratch_shapes=[pltpu.VMEM((2, seq * rows, 2 * SSM_SBLK), F32), pltpu.VMEM((2, seq * rows, 2 * SSM_SBLK), BF16),
                        pltpu.VMEM((seq * rows, SSM_UBLK), F32), pltpu.VMEM((seq * rows, SSM_UBLK), F32),
                        pltpu.VMEM((2, SSM_UBLK, 2 * SSM_SBLK), BF16), pltpu.VMEM((2, 2 * SSM_SBLK, SSM_UBLK), BF16)],
        compiler_params=_params(2),
        name="ssm_latent" if chunked else "ssm_context",
    )(*args)


def _merge_kernel(last, x_ref, sh_ref, sc_ref, ga_ref, g_ref, oa_ref, ob_ref, oc_ref, y_ref,
                  wg_ref, wm_ref, wglu_ref, bglu_ref, wbr_ref, wout_ref, fn_ref, o_ref):
    tm = x_ref.shape[0]
    subs = [slice(r, r + tm // MERGE_SUBTILES) for r in range(0, tm, tm // MERGE_SUBTILES)]
    xs = [x_ref[s, :] for s in subs]
    hs = [_modulated(x, g_ref, sh_ref, sc_ref).astype(BF16) for x in xs]
    ys = [y_ref[s, :] for s in subs]
    o_ds = [y * _sigmoid(_dot(y.astype(BF16), wglu_ref[...]) + bglu_ref[...]) for y in ys]
    accs = [None] * len(subs)
    for n, o_ref_n in enumerate((oa_ref, ob_ref, oc_ref, None)):
        outs = o_ds if o_ref_n is None else [o_ref_n[s, :].astype(F32) for s in subs]
        gates = [_dot(h, wg_ref[:, n * BRANCH_WIDTH:(n + 1) * BRANCH_WIDTH]) for h in hs]
        brs = [_dot((o * (gate * _sigmoid(gate))).astype(BF16), wbr_ref[n]) for o, gate in zip(outs, gates)]
        merges = [_dot(h, wm_ref[:, n * D_MODEL:(n + 1) * D_MODEL]) for h in hs]
        terms = [_sigmoid(m) * br for m, br in zip(merges, brs)]
        accs = [t if a is None else a + t for a, t in zip(accs, terms)]
    for s, x, acc in zip(subs, xs, accs):
        xn = x + ga_ref[0] * _dot(acc.astype(BF16), wout_ref[...])
        if last:
            xn = _rms(xn) * fn_ref[...]
        o_ref[s, :] = xn


def _merge(x, shift, scale, gate, seq, oa, ob, oc, y, l, w, final_norm, last):
    n = x.shape[0]
    tm = MERGE_TM
    row = lambda i: (i, 0)
    mod_spec = _mod_spec(shift.shape[0], tm, seq)
    in_specs = [pl.BlockSpec((tm, D_MODEL), row), mod_spec, mod_spec, mod_spec, _layer_spec((1, D_MODEL), l),
                pl.BlockSpec((tm, 512), row), pl.BlockSpec((tm, 512), row), pl.BlockSpec((tm, 512), row),
                pl.BlockSpec((tm, 512), row),
                _layer_spec((D_MODEL, N_BRANCH * BRANCH_WIDTH), l), _layer_spec((D_MODEL, N_BRANCH * D_MODEL), l),
                _layer_spec((SSM_WIDTH, SSM_WIDTH), l), _layer_spec((1, SSM_WIDTH), l),
                _layer_spec((N_BRANCH, BRANCH_WIDTH, D_MODEL), l), _layer_spec((D_MODEL, D_MODEL), l),
                _const_spec((1, D_MODEL))]
    return pl.pallas_call(
        functools.partial(_merge_kernel, last),
        grid=(n // tm,),
        in_specs=in_specs,
        out_specs=pl.BlockSpec((tm, D_MODEL), row),
        out_shape=jax.ShapeDtypeStruct((n, D_MODEL), F32),
        compiler_params=_params(1),
        name="merge",
    )(x, shift, scale, gate, w["norm_g"], oa, ob, oc, y, w["wg"], w["wm"], w["wglu"], w["bglu"],
      w["wbr"], w["wout"], final_norm)


def _rope_tables(n_tok, rot_dim):
    f32 = np.float32
    rows = n_tok // GRID_W
    row = np.repeat(np.arange(rows, dtype=f32), GRID_W)
    col = np.tile(np.arange(GRID_W, dtype=f32), rows)
    quarter = rot_dim // 4
    inv = (f32(ROPE_THETA) ** (-np.arange(quarter, dtype=f32) / f32(quarter))).astype(f32)
    ang = np.concatenate([row[:, None] * inv, col[:, None] * inv], axis=-1).astype(f32)
    return np.cos(ang).astype(f32), np.sin(ang).astype(f32)


def _rope_lane_tables(n_tok):
    c, s = _rope_tables(n_tok, MLA_ROPE)
    ones = np.ones((n_tok, MLA_NOPE), np.float32)
    pad = MLA_HEAD_PAD - MLA_NOPE - MLA_ROPE
    c_mla = np.concatenate([ones, c, c, np.ones((n_tok, pad), np.float32)], axis=1)
    s_mla = np.concatenate([0.0 * ones, -s, s, np.zeros((n_tok, pad), np.float32)], axis=1)
    c, s = _rope_tables(n_tok, DIFF_HEAD_DIM)
    c_64 = np.concatenate([c, c, c, c], axis=1)
    s_64 = np.concatenate([-s, s, -s, s], axis=1)
    return tuple(jnp.asarray(a, F32) for a in (c_mla, s_mla, c_64, s_64))


def _in_output_half(v):
    zeros = jnp.zeros_like(v[..., 0, :])
    return jnp.stack([jnp.concatenate([v[..., 0, :], zeros], axis=-1),
                      jnp.concatenate([zeros, v[..., 1, :]], axis=-1)], axis=-2)


def _weights(w_in, p):
    w_t = jnp.swapaxes(w_in, 1, 2).reshape(DEPTH * D_IN, D_MODEL)
    wmix = _transposed_rows(w_t, _IN_MIX_START, _M_WIDTH, _M_WIDTH // 2)
    wg = _transposed_rows(w_t, _IN_GATE_START, N_BRANCH * BRANCH_WIDTH, W_IN_BLOCK)
    wm = _transposed_rows(w_t, _IN_MERGE_START, N_BRANCH * D_MODEL, W_IN_BLOCK)
    zeros = lambda wd: jnp.zeros((DEPTH, D_MODEL, wd), F32)
    kpe_lo = MLA_Q_LORA + MLA_KV_LORA
    whead = jnp.concatenate([w_in[:, :, :kpe_lo], zeros(MLA_NOPE), w_in[:, :, kpe_lo:_IN_MIX_START],
                             zeros(MLA_HEAD_PAD - MLA_NOPE - MLA_ROPE)], axis=2).astype(BF16)

    hq = MLA_NOPE + MLA_ROPE
    wqb = p["w_mla_q_b"].reshape(DEPTH, MLA_Q_LORA, MLA_HEADS, hq)
    wqb = jnp.pad(wqb, ((0, 0), (0, 0), (0, 0), (0, MLA_HEAD_PAD - hq))).reshape(DEPTH, MLA_Q_LORA, -1).astype(BF16)
    wkv = p["w_mla_kv_b"].reshape(DEPTH, MLA_KV_LORA, MLA_HEADS, MLA_NOPE + MLA_V)
    wk = jnp.pad(wkv[..., :MLA_NOPE], ((0, 0), (0, 0), (0, 0), (0, MLA_HEAD_PAD - MLA_NOPE)))
    wk = wk.reshape(DEPTH, MLA_KV_LORA, -1).astype(BF16)
    wv = _in_output_half(wkv[..., MLA_NOPE:].reshape(DEPTH, MLA_KV_LORA, MLA_HEADS // 2, 2, MLA_V))
    wv = wv.reshape(DEPTH, MLA_KV_LORA, -1).astype(BF16)
    egv = np.zeros((GQA_KV_WIDTH, 2 * GQA_KV_HEADS * V_PAD), np.float32)
    for g in range(GQA_KV_HEADS):
        for d in range(GQA_HEAD_DIM):
            egv[g * GQA_HEAD_DIM + d, (2 * g) * V_PAD + d] = 1.0
            egv[g * GQA_HEAD_DIM + d, (2 * g + 1) * V_PAD + GQA_HEAD_DIM + d] = 1.0
    egv = jnp.asarray(egv, BF16)
    gmat = jnp.asarray(np.kron(np.eye(GQA_HEADS), np.full((GQA_HEAD_DIM, GQA_HEAD_DIM), 1.0 / GQA_HEAD_DIM)), BF16)

    lam = lax.complex(p["ssm_a_re"], p["ssm_a_im"])
    dt = jnp.exp(p["ssm_log_dt"])[..., None]
    abar = jnp.exp(lam * dt)
    bbar = ((abar - 1.0) / lam)[..., None] * lax.complex(p["ssm_b_re"], p["ssm_b_im"])
    blk = lambda a: a.reshape((DEPTH, 2, SSM_NBLK, SSM_GBLOCK) + a.shape[3:])
    b_t = jnp.swapaxes(blk(bbar), -1, -2)
    ssm_b = jnp.concatenate([jnp.real(b_t), jnp.imag(b_t)], axis=-1).astype(BF16)
    ab = blk(abar).reshape(DEPTH, 2, SSM_NBLK, 1, SSM_SBLK)
    ssm_a = jnp.concatenate([jnp.real(ab), jnp.imag(ab)], axis=-1)
    c_re = jnp.swapaxes(blk(p["ssm_c_re"]), -1, -2)
    c_im = jnp.swapaxes(blk(p["ssm_c_im"]), -1, -2)
    ssm_c = jnp.concatenate([c_re, -c_im], axis=-2).astype(BF16)

    row = lambda a: a[:, None, :]
    return dict(
        norm_g=row(p["norm_g"]), whead=whead, wmix=wmix, qn=row(p["mla_q_norm"]), wqb=wqb,
        kvn=row(p["mla_kv_norm"]), wk=wk, wv=wv,
        gqn=row(jnp.tile(p["gqa_q_norm"], (1, GQA_HEADS))), gkn=row(jnp.tile(p["gqa_k_norm"], (1, GQA_KV_HEADS))),
        gmat=gmat, egv=egv,
        lam_parts=jnp.stack([p["diff_lq1"], p["diff_lk1"], p["diff_lq2"], p["diff_lk2"]], axis=1),
        subln=row(p["diff_subln"]),
        ssm_b=ssm_b, ssm_a=ssm_a, ssm_c=ssm_c, ssm_d=row(p["ssm_d"]),
        wg=wg, wm=wm, wglu=p["ssm_glu_w"].astype(BF16), bglu=row(p["ssm_glu_b"]),
        wbr=p["w_branch_out"].astype(BF16), wout=p["w_out"].astype(BF16),
    )


def _mixers(pr, b, t, l, w, lam_init, s0):
    q, k, v, dq, dk, dv, gq, gk, gv, u = pr[:10]
    three = lambda a: a.reshape(b, a.shape[0] // b, a.shape[-1])
    k, v, dk, dv, gk, gv = (three(a) for a in (k, v, dk, dv, gk, gv))
    mla_heads = tuple((hd, hd) for hd in range(MLA_HEADS))
    gqa_heads = tuple((hd // GQA_GROUP, 2 * (hd // GQA_GROUP) + hd % 2) for hd in range(GQA_HEADS))
    diff_specs = (_layer_spec((4, DIFF_HEAD_DIM), l), _layer_spec((1, DIFF_V), l))
    diff_extra = (w["lam_parts"], w["subln"])
    if s0 is None:
        o_a, o_b, o_c = _attention_short(lam_init, mla_heads, gqa_heads,
                                         (three(q), k, v, three(dq), dk, dv, three(gq), gk, gv),
                                         diff_specs, diff_extra)
    else:
        o_a = _attention(functools.partial(_attn_heads_kernel, mla_heads, MLA_HEAD_PAD), "attn_mla",
                         three(q), k, v)
        o_b = _attention(functools.partial(_diff_attn_kernel, lam_init), "attn_diff", three(dq), dk, dv,
                         extra_specs=diff_specs, extra=diff_extra)
        o_c = _attention(functools.partial(_attn_heads_kernel, gqa_heads, GQA_HEAD_DIM), "attn_gqa",
                         three(gq), gk, gv)
    n = b * t
    u_rows = u.reshape(n // SSM_SEQ, SSM_SEQ, SSM_WIDTH)
    if s0 is None:
        y_t, fin = _ssm(u_rows, l, w, None, True)
    else:
        (y_t,) = _ssm(u_rows, l, w, s0, False)
        fin = None
    y = jnp.swapaxes(y_t, 0, 1).reshape(n, SSM_WIDTH)
    flat = lambda a: a.reshape(n, 512)
    return flat(o_a), flat(o_b), flat(o_c), y, fin


def kernel(x_prompt, x_sample, cache_mla_ckv, cache_mla_krope, cache_diff_k, cache_diff_v, cache_gqa_k, cache_gqa_v, state_ssm, c, c_ctx, norm_g, w_mod, b_mod, w_in, mla_q_norm, w_mla_q_b, mla_kv_norm, w_mla_kv_b, diff_lq1, diff_lk1, diff_lq2, diff_lk2, diff_subln, gqa_q_norm, gqa_k_norm, ssm_a_re, ssm_a_im, ssm_log_dt, ssm_b_re, ssm_b_im, ssm_c_re, ssm_c_im, ssm_d, ssm_glu_w, ssm_glu_b, w_branch_out, w_out, final_norm):
    p = dict(norm_g=norm_g, mla_q_norm=mla_q_norm, w_mla_q_b=w_mla_q_b, mla_kv_norm=mla_kv_norm,
             w_mla_kv_b=w_mla_kv_b, diff_lq1=diff_lq1, diff_lk1=diff_lk1, diff_lq2=diff_lq2, diff_lk2=diff_lk2,
             diff_subln=diff_subln, gqa_q_norm=gqa_q_norm, gqa_k_norm=gqa_k_norm, ssm_a_re=ssm_a_re,
             ssm_a_im=ssm_a_im, ssm_log_dt=ssm_log_dt, ssm_b_re=ssm_b_re, ssm_b_im=ssm_b_im, ssm_c_re=ssm_c_re,
             ssm_c_im=ssm_c_im, ssm_d=ssm_d, ssm_glu_w=ssm_glu_w, ssm_glu_b=ssm_glu_b,
             w_branch_out=w_branch_out, w_out=w_out)
    bp, tp, _ = x_prompt.shape
    bs, ts, _ = x_sample.shape
    past = cache_mla_ckv.shape[2]
    assert tp == SSM_SEQ and ts % SSM_SEQ == 0 and ts // SSM_SEQ == V7X_SUBLANES
    rope = _rope_lane_tables(ts)
    w = _weights(w_in, p)

    cvec = jnp.concatenate([c_ctx[None], c, jnp.zeros((V7X_SUBLANES - 1 - bs, D_MODEL), F32)], axis=0)
    mod = _modulation(cvec, w_mod, b_mod)

    feature_major = lambda a: jnp.moveaxis(a, 2, -1).reshape(bs, DEPTH, -1, past)
    pad_lo = MLA_NOPE
    pad_hi = MLA_HEAD_PAD - MLA_NOPE - MLA_ROPE
    kpe_ctx = jnp.pad(feature_major(cache_mla_krope), ((0, 0), (0, 0), (pad_lo, pad_hi), (0, 0)))
    dk_ctx, gk_ctx, gv_ctx = (feature_major(a) for a in (cache_diff_k, cache_gqa_k, cache_gqa_v))
    dv_ctx = cache_diff_v
    s0_all = state_ssm.reshape(bs, DEPTH, 2, SSM_NBLK, SSM_GBLOCK * SSM_STATE, 2)
    s0_all = jnp.transpose(s0_all, (1, 2, 0, 3, 5, 4)).reshape(DEPTH, 2, bs, SSM_NBLK, 1, 2 * SSM_SBLK)

    xp = x_prompt.reshape(bp * tp, D_MODEL)
    xs = x_sample.reshape(bs * ts, D_MODEL)
    fn = final_norm[None]
    sides = ()
    states = []
    for l in range(DEPTH):
        lam_init = 0.8 - 0.6 * math.exp(-0.3 * l)
        last = l == DEPTH - 1
        sh, sc, ga = (mod[l, :, i * D_MODEL:(i + 1) * D_MODEL][:, None, :] for i in range(3))

        pr = _proj(xp, sh[0:1], sc[0:1], tp, l, w, prev_sides=sides)
        o_a, o_b, o_c, y, fin = _mixers(pr, bp, tp, l, w, lam_init, None)
        xp = _merge(xp, sh[0:1], sc[0:1], ga[0:1], tp, o_a, o_b, o_c, y, l, w, fn, last)
        sides = tuple(pr[10:])
        fin = fin.reshape(2, SSM_NBLK, bp, 2, SSM_GBLOCK, SSM_STATE)
        states.append(jnp.transpose(fin, (2, 0, 1, 4, 5, 3)).reshape(bp, 2, SSM_GROUPS, SSM_STATE, 2))

        pr = _proj(xs, sh[1:1 + bs], sc[1:1 + bs], ts, l, w, rope=rope,
                   cached=(cache_mla_ckv, kpe_ctx, dk_ctx, dv_ctx, gk_ctx, gv_ctx))
        o_a, o_b, o_c, y, _ = _mixers(pr, bs, ts, l, w, lam_init, s0_all)
        xs = _merge(xs, sh[1:1 + bs], sc[1:1 + bs], ga[1:1 + bs], ts, o_a, o_b, o_c, y, l, w, fn, last)

    ckv_s, kpe_t, dk_t, dv_s, gk_t, gv_t = sides
    untranspose = lambda a: jnp.swapaxes(a, 2, 3)
    return (xp.reshape(bp, tp, D_MODEL), xs.reshape(bs, ts, D_MODEL),
            ckv_s, untranspose(kpe_t),
            untranspose(dk_t).reshape(bp, DEPTH, tp, DIFF_HEADS, 2, DIFF_HEAD_DIM), dv_s,
            untranspose(gk_t).reshape(bp, DEPTH, tp, GQA_KV_HEADS, GQA_HEAD_DIM),
            untranspose(gv_t).reshape(bp, DEPTH, tp, GQA_KV_HEADS, GQA_HEAD_DIM),
            jnp.stack(states, axis=1))
```

```python
import functools
import math

import jax
import jax.numpy as jnp
import numpy as np
from jax import lax
from jax.experimental import pallas as pl
from jax.experimental.pallas import tpu as pltpu

F32 = jnp.float32
BF16 = jnp.bfloat16

D_MODEL = 1024
DEPTH = 2
GRID_W = 64
ROPE_THETA = 10000.0
EPS = 1e-6

MLA_HEADS = 8
MLA_NOPE = 64
MLA_ROPE = 32
MLA_V = 64
MLA_Q_LORA = 256
MLA_KV_LORA = 128
MLA_SCALE = (MLA_NOPE + MLA_ROPE) ** -0.5
MLA_HEAD_PAD = 128
DIFF_HEADS = 4
DIFF_HEAD_DIM = 64
DIFF_V = 2 * DIFF_HEAD_DIM
DIFF_SCALE = DIFF_HEAD_DIM ** -0.5
GQA_HEADS = 8
GQA_KV_HEADS = 2
GQA_HEAD_DIM = 64
GQA_GROUP = GQA_HEADS // GQA_KV_HEADS
GQA_KV_WIDTH = GQA_KV_HEADS * GQA_HEAD_DIM
GQA_SCALE = GQA_HEAD_DIM ** -0.5
SSM_WIDTH = 512
SSM_GROUP = 16
SSM_GROUPS = SSM_WIDTH // SSM_GROUP
SSM_STATE = 64
N_BRANCH = 4
BRANCH_WIDTH = 512

IN_SPLITS = (MLA_Q_LORA, MLA_KV_LORA, MLA_ROPE, 512, 512, 512, 512, 128, 128, SSM_WIDTH,
             N_BRANCH * BRANCH_WIDTH, N_BRANCH * D_MODEL)
D_IN = sum(IN_SPLITS)

V7X_LANES = 128
V7X_SUBLANES = 8
V7X_VMEM_LIMIT_BYTES = 56 * 1024 * 1024

_H_QA = (0, 256)
_H_KVA = (256, 384)
_H_KPE = (384, 512)
_H_WIDTH = 512
_M_DQ = (0, 512)
_M_DK = (512, 1024)
_M_DV = (1024, 1536)
_M_GQ = (1536, 2048)
_M_GK = (2048, 2176)
_M_GV = (2176, 2304)
_M_U = (2304, 2816)
_M_WIDTH = 2816
_IN_MIX_START = MLA_Q_LORA + MLA_KV_LORA + MLA_ROPE
_IN_GATE_START = _IN_MIX_START + _M_WIDTH
_IN_MERGE_START = _IN_GATE_START + N_BRANCH * BRANCH_WIDTH

SSM_GBLOCK = 8
SSM_NBLK = SSM_GROUPS // SSM_GBLOCK
SSM_SBLK = SSM_GBLOCK * SSM_STATE
SSM_UBLK = SSM_GBLOCK * SSM_GROUP
SSM_SEQ = 256
SSM_SLICE = 32

V_PAD = 128

W_IN_BLOCK = 1024
PROJ_TM = 512
MERGE_TM = 512
ATTN_TQ_LONG = 512
ATTN_SEQ_PER_STEP = 2
ATTN_GROUP_LONG = 2


def _params(n_axes, vmem=V7X_VMEM_LIMIT_BYTES):
    return pltpu.CompilerParams(dimension_semantics=("arbitrary",) * n_axes, vmem_limit_bytes=vmem)


def _const_spec(shape):
    zeros = (0,) * len(shape)
    return pl.BlockSpec(shape, lambda *_: zeros, pipeline_mode=pl.Buffered(1))


def _layer_spec(shape, l):
    zeros = (0,) * len(shape)
    return pl.BlockSpec((None,) + tuple(shape), lambda *_: (l,) + zeros, pipeline_mode=pl.Buffered(1))


def _mod_spec(n_rows, tm, seq):
    if n_rows == 1:
        return pl.BlockSpec((1, 1, D_MODEL), lambda i: (0, 0, 0))
    return pl.BlockSpec((1, 1, D_MODEL), lambda i: (i * tm // seq, 0, 0))


def _dot(a, b):
    return jnp.dot(a, b, preferred_element_type=F32)


def _dot_nt(a, b):
    return lax.dot_general(a, b, (((1,), (1,)), ((), ())), preferred_element_type=F32)


def _rms(x):
    return x * lax.rsqrt(jnp.mean(x * x, axis=-1, keepdims=True) + EPS)


def _sigmoid(x):
    return 0.5 * jnp.tanh(0.5 * x) + 0.5


def _gelu_tanh(x):
    return 0.5 * x * (1.0 + jnp.tanh(math.sqrt(2.0 / math.pi) * (x + 0.044715 * (x * x * x))))


def _group_mean_sq(x, gmat):
    return _dot((x * x).astype(BF16), gmat)


def _rope_chunk(x, cos, sin, half):
    lane = lax.broadcasted_iota(jnp.int32, x.shape, 1)
    first = (lane % (2 * half)) < half
    swapped = jnp.where(first, pltpu.roll(x, V7X_LANES - half, 1), pltpu.roll(x, half, 1))
    return x * cos + swapped * sin


def _transpose_cast_kernel(w_ref, o_ref):
    o_ref[...] = w_ref[...].astype(BF16).T


def _transposed_rows(w_t, row0, n_rows, blk):
    return pl.pallas_call(
        _transpose_cast_kernel,
        grid=(DEPTH, n_rows // blk),
        in_specs=[pl.BlockSpec((pl.Element(blk), pl.Element(D_MODEL)),
                               lambda l, j: (pl.multiple_of(l * D_IN + row0 + j * blk, V7X_SUBLANES), 0))],
        out_specs=pl.BlockSpec((None, D_MODEL, blk), lambda l, j: (l, 0, j)),
        out_shape=jax.ShapeDtypeStruct((DEPTH, D_MODEL, n_rows), BF16),
        compiler_params=_params(2),
        name="w_in_columns",
    )(w_t)


def _mod_kernel(c_ref, w_ref, b_ref, o_ref):
    c = c_ref[...]
    a = (c * _sigmoid(c)).astype(BF16)
    o_ref[0] = _dot(a, w_ref[0].astype(BF16)) + b_ref[0]


def _modulation(cvec, w_mod, b_mod):
    tn = 512
    return pl.pallas_call(
        _mod_kernel,
        grid=(DEPTH, 3 * D_MODEL // tn),
        in_specs=[pl.BlockSpec((V7X_SUBLANES, D_MODEL), lambda l, j: (0, 0)),
                  pl.BlockSpec((1, D_MODEL, tn), lambda l, j: (l, 0, j)),
                  pl.BlockSpec((1, 1, tn), lambda l, j: (l, 0, j))],
        out_specs=pl.BlockSpec((1, V7X_SUBLANES, tn), lambda l, j: (l, 0, j)),
        out_shape=jax.ShapeDtypeStruct((DEPTH, V7X_SUBLANES, 3 * D_MODEL), F32),
        compiler_params=_params(2),
        name="modulation",
    )(cvec, w_mod, b_mod.reshape(DEPTH, 1, 3 * D_MODEL))


def _modulated(x, g_ref, sh_ref, sc_ref):
    return (_rms(x) * g_ref[...]) * (1.0 + sc_ref[0]) + sh_ref[0]


def _proj_kernel(latent, n_prev, first_layer, *refs):
    (x_ref, sh_ref, sc_ref, g_ref, wh_ref, wx_ref, qn_ref, wqb_ref, kvn_ref, wk_ref, wv_ref,
     gqn_ref, gkn_ref, gm_ref, egv_ref) = refs[:15]
    pos = 15
    if latent:
        c_mla, s_mla, c_64, s_64 = refs[pos:pos + 4]
        cached = refs[pos + 4:pos + 10]
        pos += 10
    pos += n_prev
    (q_o, k_o, v_o, dq_o, dk_o, dv_o, gq_o, gk_o, gv_o, u_o) = refs[pos:pos + 10]
    pos += 10
    if not latent:
        ckv_f, kpe_t, dk_t, dv_f, gk_t, gv_t = refs[pos:pos + 6]

    def mla_keys_values(ckv_b, kpe):
        kn = _dot(ckv_b, wk_ref[...])
        for hd in range(MLA_HEADS):
            sl = slice(hd * MLA_HEAD_PAD, (hd + 1) * MLA_HEAD_PAD)
            k_o[:, sl] = (kn[:, sl] + kpe).astype(BF16)
        v_o[...] = _dot(ckv_b, wv_ref[...]).astype(BF16)

    def cached_tokens():
        ckv_c, kpe_t, dk_t, dv_c, gk_t, gv_t = cached
        mla_keys_values(ckv_c[...].astype(BF16), kpe_t[...].T)
        dk_o[...] = dk_t[...].T.astype(BF16)
        for hd in range(DIFF_HEADS):
            dv_o[:, hd * DIFF_V:(hd + 1) * DIFF_V] = dv_c[:, hd, :].astype(BF16)
        gk_o[...] = gk_t[...].T.astype(BF16)
        gv_o[...] = _dot(gv_t[...].T.astype(BF16), egv_ref[...]).astype(BF16)

    def new_tokens():
        _proj_new_tokens(latent, first_layer, refs[:15], (c_mla, s_mla, c_64, s_64) if latent else None,
                         refs[pos - 10:pos], refs[pos:pos + 6] if not latent else None, mla_keys_values)

    if latent:
        pl.when(pl.program_id(1) == 0)(cached_tokens)
        pl.when(pl.program_id(1) > 0)(new_tokens)
    else:
        new_tokens()


def _proj_new_tokens(latent, first_layer, in_refs, rope_refs, out_refs, side_refs, mla_keys_values):
    (x_ref, sh_ref, sc_ref, g_ref, wh_ref, wx_ref, qn_ref, wqb_ref, kvn_ref, wk_ref, wv_ref,
     gqn_ref, gkn_ref, gm_ref, egv_ref) = in_refs
    (q_o, k_o, v_o, dq_o, dk_o, dv_o, gq_o, gk_o, gv_o, u_o) = out_refs
    if latent:
        c_mla, s_mla, c_64, s_64 = rope_refs
    else:
        ckv_f, kpe_t, dk_t, dv_f, gk_t, gv_t = side_refs

    h = _modulated(x_ref[...], g_ref, sh_ref, sc_ref).astype(BF16)
    tm = h.shape[0]

    p_head = _dot(h, wh_ref[...])
    p_mix = _dot(h, wx_ref[...])

    def head(cols):
        return p_head[:, cols[0]:cols[1]]

    def mix(cols):
        return p_mix[:, cols[0]:cols[1]]

    def rope_mla(x):
        return _rope_chunk(x, c_mla[...], s_mla[...], MLA_ROPE // 2) if latent else x

    def rope_64(x):
        if not latent:
            return x
        c, s = c_64[...], s_64[...]
        chunks = [_rope_chunk(x[:, i:i + V7X_LANES], c, s, DIFF_HEAD_DIM // 2)
                  for i in range(0, x.shape[1], V7X_LANES)]
        return chunks[0] if len(chunks) == 1 else jnp.concatenate(chunks, axis=1)

    def per_sequence(x):
        return [x[b * SSM_SEQ:(b + 1) * SSM_SEQ] for b in range(tm // SSM_SEQ)]

    def slot(b):
        return (b,) if first_layer is None else (b, first_layer)

    def zero_other_slots(ref):
        if first_layer is not None:
            for b in range(ref.shape[0]):
                for other in range(ref.shape[1]):
                    if other != first_layer:
                        ref[b, other] = jnp.zeros(ref.shape[2:], ref.dtype)

    qn = (_rms(head(_H_QA)) * qn_ref[...]).astype(BF16)
    q = _dot(qn, wqb_ref[...])
    for hd in range(MLA_HEADS):
        sl = slice(hd * MLA_HEAD_PAD, (hd + 1) * MLA_HEAD_PAD)
        q_o[:, sl] = (rope_mla(q[:, sl]) * MLA_SCALE).astype(BF16)
    ckv = _rms(head(_H_KVA)) * kvn_ref[...]
    kpe = head(_H_KPE)
    if not latent:
        for b, (c_b, k_b) in enumerate(zip(per_sequence(ckv), per_sequence(kpe))):
            ckv_f[slot(b)] = c_b
            kpe_t[slot(b)] = k_b.T[MLA_NOPE:MLA_NOPE + MLA_ROPE, :]
        zero_other_slots(ckv_f)
        zero_other_slots(kpe_t)
    mla_keys_values(ckv.astype(BF16), rope_mla(kpe))

    dq_o[...] = (rope_64(mix(_M_DQ)) * DIFF_SCALE).astype(BF16)
    dk = rope_64(mix(_M_DK))
    dv = mix(_M_DV)
    dk_o[...] = dk.astype(BF16)
    dv_o[...] = dv.astype(BF16)

    gq = mix(_M_GQ)
    gq = gq * lax.rsqrt(_group_mean_sq(gq, gm_ref[...]) + EPS) * gqn_ref[...]
    gq_o[...] = (rope_64(gq) * GQA_SCALE).astype(BF16)
    gk = mix(_M_GK)
    kw = GQA_KV_WIDTH
    gk = gk * lax.rsqrt(_group_mean_sq(gk, gm_ref[:kw, :kw]) + EPS) * gkn_ref[...]
    gv = mix(_M_GV)
    if not latent:
        for b, (dk_b, dv_b, gk_b, gv_b) in enumerate(zip(*(per_sequence(a) for a in (dk, dv, gk, gv)))):
            dk_t[slot(b)] = dk_b.T
            for hd in range(DIFF_HEADS):
                dv_f[slot(b) + (slice(None), hd, slice(None))] = dv_b[:, hd * DIFF_V:(hd + 1) * DIFF_V]
            gk_t[slot(b)] = gk_b.T
            gv_t[slot(b)] = gv_b.T
        for ref in (dk_t, dv_f, gk_t, gv_t):
            zero_other_slots(ref)
    gk_o[...] = rope_64(gk).astype(BF16)
    gv_o[...] = _dot(gv.astype(BF16), egv_ref[...]).astype(BF16)

    u_o[...] = mix(_M_U)


def _side_shapes(n_seq):
    return [((n_seq, DEPTH, SSM_SEQ, MLA_KV_LORA), (SSM_SEQ, MLA_KV_LORA)),
            ((n_seq, DEPTH, MLA_ROPE, SSM_SEQ), (MLA_ROPE, SSM_SEQ)),
            ((n_seq, DEPTH, 512, SSM_SEQ), (512, SSM_SEQ)),
            ((n_seq, DEPTH, SSM_SEQ, DIFF_HEADS, DIFF_V), (SSM_SEQ, DIFF_HEADS, DIFF_V)),
            ((n_seq, DEPTH, GQA_KV_WIDTH, SSM_SEQ), (GQA_KV_WIDTH, SSM_SEQ)),
            ((n_seq, DEPTH, GQA_KV_WIDTH, SSM_SEQ), (GQA_KV_WIDTH, SSM_SEQ))]


def _proj(x, shift, scale, seq, l, w, rope=None, cached=None, prev_sides=()):
    n = x.shape[0]
    tm = PROJ_TM
    latent = cached is not None
    kw = GQA_KV_WIDTH
    if latent:
        tiles = seq // tm
        assert cached[0].shape[2] == tm
        grid = (n // seq, 1 + tiles)
        row = lambda b, j: (b * tiles + jnp.maximum(j - 1, 0), 0)
        kv_row = lambda b, j: (b * (1 + tiles) + j, 0)
        n_kv = (n // seq) * (cached[0].shape[2] + seq)
        mod_spec = pl.BlockSpec((1, 1, D_MODEL), lambda b, j: (b, 0, 0))
    else:
        grid = (n // tm,)
        row = kv_row = lambda i: (i, 0)
        n_kv = n
        mod_spec = _mod_spec(shift.shape[0], tm, seq)
    in_specs = [pl.BlockSpec((tm, D_MODEL), row), mod_spec, mod_spec,
                _layer_spec((1, D_MODEL), l), _layer_spec((D_MODEL, _H_WIDTH), l), _layer_spec((D_MODEL, _M_WIDTH), l),
                _layer_spec((1, MLA_Q_LORA), l), _layer_spec((MLA_Q_LORA, MLA_HEADS * MLA_HEAD_PAD), l),
                _layer_spec((1, MLA_KV_LORA), l), _layer_spec((MLA_KV_LORA, MLA_HEADS * MLA_HEAD_PAD), l),
                _layer_spec((MLA_KV_LORA, MLA_HEADS * V_PAD), l),
                _layer_spec((1, 512), l), _layer_spec((1, kw), l), _const_spec((512, 512)),
                _const_spec((kw, 2 * GQA_KV_HEADS * V_PAD))]
    args = [x, shift, scale, w["norm_g"], w["whead"], w["wmix"], w["qn"], w["wqb"], w["kvn"], w["wk"], w["wv"],
            w["gqn"], w["gkn"], w["gmat"], w["egv"]]
    if latent:
        tab = pl.BlockSpec((tm, V7X_LANES), lambda b, j: (jnp.maximum(j - 1, 0), 0))
        in_specs += [tab] * 4
        args += list(rope)
        in_specs += [pl.BlockSpec((None, None) + c.shape[2:], lambda b, j, nd=c.ndim: (b, l) + (0,) * (nd - 2))
                     for c in cached]
        args += list(cached)
    n_fixed = len(args)
    in_specs += [pl.BlockSpec(memory_space=pl.ANY)] * len(prev_sides)
    args += list(prev_sides)
    widths = [(1024, BF16, False), (1024, BF16, True), (MLA_HEADS * V_PAD, BF16, True), (512, BF16, False),
              (512, BF16, True), (512, BF16, True), (512, BF16, False), (kw, BF16, True),
              (2 * GQA_KV_HEADS * V_PAD, BF16, True), (512, F32, False)]
    out_specs = [pl.BlockSpec((tm, wd), kv_row if kv else row) for wd, _, kv in widths]
    out_shape = [jax.ShapeDtypeStruct((n_kv if kv else n, wd), dt) for wd, dt, kv in widths]
    aliases = {}
    first_layer = None
    if not latent:
        per_tile = tm // SSM_SEQ
        first_layer = None if prev_sides else l
        for full, blk in _side_shapes(n // SSM_SEQ):
            zeros = (0,) * len(blk)
            if prev_sides:
                out_specs.append(pl.BlockSpec((per_tile, None) + blk, lambda i, zeros=zeros: (i, l) + zeros))
            else:
                out_specs.append(pl.BlockSpec((per_tile, DEPTH) + blk, lambda i, zeros=zeros: (i, 0) + zeros))
            out_shape.append(jax.ShapeDtypeStruct(full, F32))
        aliases = {n_fixed + k: len(widths) + k for k in range(len(prev_sides))}
    return pl.pallas_call(
        functools.partial(_proj_kernel, latent, len(prev_sides), first_layer),
        grid=grid,
        in_specs=in_specs, out_specs=out_specs, out_shape=out_shape,
        input_output_aliases=aliases,
        compiler_params=_params(len(grid)),
        name="proj_latent" if latent else "proj_context",
    )(*args)


def _softmax_maps(maps):
    scores = [_dot_nt(q, k) for q, k, _ in maps]
    tops = [jnp.max(s, axis=-1, keepdims=True) for s in scores]
    weights = [jnp.exp(s - m).astype(BF16) for s, m in zip(scores, tops)]
    outs = []
    for p, (_, _, v) in zip(weights, maps):
        ones = jnp.ones((v.shape[0], V7X_LANES), BF16)
        o = _dot(p, jnp.concatenate([v, ones], axis=1))
        outs.append((o[:, :V7X_LANES], o[:, V7X_LANES:]))
    return outs


def _head_maps(heads, q_width, q_ref, k_ref, v_ref, part):
    return [(q_ref[b, :, hd * q_width:(hd + 1) * q_width],
             k_ref[b, :, heads[hd][0] * q_width:(heads[hd][0] + 1) * q_width],
             v_ref[b, :, heads[hd][1] * V_PAD:(heads[hd][1] + 1) * V_PAD]) for b, hd in part]


def _head_outputs(outs, part, o_ref):
    terms = [o * (1.0 / total) for o, total in outs]
    for (b, hd), low, high in zip(part[0::2], terms[0::2], terms[1::2]):
        o_ref[b, :, (hd // 2) * V7X_LANES:(hd // 2 + 1) * V7X_LANES] = (low + high).astype(BF16)


def _diff_maps(q_ref, k_ref, v_ref, part):
    d = DIFF_HEAD_DIM
    return [(q_ref[b, :, c * d:(c + 1) * d], k_ref[b, :, c * d:(c + 1) * d],
             v_ref[b, :, hd * DIFF_V:(hd + 1) * DIFF_V])
            for b, hd in part for c in (2 * hd, 2 * hd + 1)]


def _diff_lambda(lp_ref, lam_init):
    lp = lp_ref[...]
    return (jnp.exp(jnp.sum(lp[0:1] * lp[1:2], axis=-1, keepdims=True))
            - jnp.exp(jnp.sum(lp[2:3] * lp[3:4], axis=-1, keepdims=True)) + lam_init)


def _diff_outputs(outs, part, lam, lam_init, sub_ref, o_ref):
    mixed = [o1 * (1.0 / t1) - o2 * (lam / t2) for (o1, t1), (o2, t2) in zip(outs[0::2], outs[1::2])]
    normed = [_rms(o) * sub_ref[...] * (1.0 - lam_init) for o in mixed]
    for (b, hd), o in zip(part, normed):
        o_ref[b, :, hd * DIFF_V:(hd + 1) * DIFF_V] = o.astype(BF16)


def _items(q_ref, n_heads):
    return [(b, hd) for b in range(q_ref.shape[0]) for hd in range(n_heads)]


def _attn_heads_kernel(heads, q_width, q_ref, k_ref, v_ref, o_ref):
    items = _items(q_ref, len(heads))
    group = ATTN_GROUP_LONG
    for g0 in range(0, len(items), group):
        part = items[g0:g0 + group]
        _head_outputs(_softmax_maps(_head_maps(heads, q_width, q_ref, k_ref, v_ref, part)), part, o_ref)


def _diff_attn_kernel(lam_init, q_ref, k_ref, v_ref, lp_ref, sub_ref, o_ref):
    lam = _diff_lambda(lp_ref, lam_init)
    items = _items(q_ref, DIFF_HEADS)
    group = ATTN_GROUP_LONG // 2
    for g0 in range(0, len(items), group):
        part = items[g0:g0 + group]
        _diff_outputs(_softmax_maps(_diff_maps(q_ref, k_ref, v_ref, part)), part, lam, lam_init, sub_ref, o_ref)


def _attn_short_kernel(lam_init, mla_heads, gqa_heads, qa_ref, ka_ref, va_ref, qb_ref, kb_ref, vb_ref,
                       qc_ref, kc_ref, vc_ref, lp_ref, sub_ref, oa_ref, ob_ref, oc_ref):
    part_a, part_b, part_c = _items(qa_ref, len(mla_heads)), _items(qb_ref, DIFF_HEADS), _items(qc_ref, len(gqa_heads))
    maps_a = _head_maps(mla_heads, MLA_HEAD_PAD, qa_ref, ka_ref, va_ref, part_a)
    maps_b = _diff_maps(qb_ref, kb_ref, vb_ref, part_b)
    maps_c = _head_maps(gqa_heads, GQA_HEAD_DIM, qc_ref, kc_ref, vc_ref, part_c)
    outs = _softmax_maps(maps_a + maps_b + maps_c)
    n_a, n_b = len(maps_a), len(maps_b)
    _head_outputs(outs[:n_a], part_a, oa_ref)
    _diff_outputs(outs[n_a:n_a + n_b], part_b, _diff_lambda(lp_ref, lam_init), lam_init, sub_ref, ob_ref)
    _head_outputs(outs[n_a + n_b:], part_c, oc_ref)


def _attention_short(lam_init, mla_heads, gqa_heads, qkv, extra_specs, extra):
    b, t, _ = qkv[0].shape
    nb = ATTN_SEQ_PER_STEP
    whole = lambda i: (i, 0, 0)
    in_specs = [pl.BlockSpec((nb, t, a.shape[2]), whole) for a in qkv] + list(extra_specs)
    return pl.pallas_call(
        functools.partial(_attn_short_kernel, lam_init, mla_heads, gqa_heads),
        grid=(b // nb,),
        in_specs=in_specs,
        out_specs=[pl.BlockSpec((nb, t, 512), whole)] * 3,
        out_shape=[jax.ShapeDtypeStruct((b, t, 512), BF16)] * 3,
        compiler_params=_params(1),
        name="attn_context",
    )(*qkv, *extra)


def _attention(kernel, name, q, k, v, extra_specs=(), extra=()):
    b, t, wq = q.shape
    s = k.shape[1]
    tq = ATTN_TQ_LONG
    in_specs = [pl.BlockSpec((1, tq, wq), lambda i, j: (i, j, 0)),
                pl.BlockSpec((1, s, k.shape[2]), lambda i, j: (i, 0, 0)),
                pl.BlockSpec((1, s, v.shape[2]), lambda i, j: (i, 0, 0))]
    in_specs += list(extra_specs)
    return pl.pallas_call(
        kernel,
        grid=(b, t // tq),
        in_specs=in_specs,
        out_specs=pl.BlockSpec((1, tq, 512), lambda i, j: (i, j, 0)),
        out_shape=jax.ShapeDtypeStruct((b, t, 512), BF16),
        compiler_params=_params(2),
        name=name,
    )(q, k, v, *extra)


def _ssm_kernel(chunked, want_final, *refs):
    u_ref, bg_ref, ab_ref, cg_ref, d_ref = refs[:5]
    pos = 5
    if chunked:
        s0_ref = refs[pos]
        pos += 1
    y_ref = refs[pos]
    pos += 1
    if want_final:
        fin_ref = refs[pos]
        pos += 1
    hs_ref, hb_ref, yacc_ref, ut_ref, bm_ref, cm_ref = refs[pos:pos + 6]

    @pl.when(pl.program_id(1) == 0)
    def _():
        bm_ref[...] = jnp.zeros(bm_ref.shape, bm_ref.dtype)
        cm_ref[...] = jnp.zeros(cm_ref.shape, cm_ref.dtype)
        for dr in range(2):
            for g in range(SSM_GBLOCK):
                ins = slice(g * SSM_GROUP, (g + 1) * SSM_GROUP)
                for part in range(2):
                    states = slice(part * SSM_SBLK + g * SSM_STATE, part * SSM_SBLK + (g + 1) * SSM_STATE)
                    bm_ref[dr, ins, states] = bg_ref[dr, 0, g, :, part * SSM_STATE:(part + 1) * SSM_STATE]
                    cm_ref[dr, states, ins] = cg_ref[dr, 0, g, part * SSM_STATE:(part + 1) * SSM_STATE, :]

    seq, rows, sb = SSM_SEQ, V7X_SUBLANES, SSM_SBLK
    n_slices = seq // SSM_SLICE
    slice_rows = SSM_SLICE * rows

    def visited(dr, stage):
        return stage if dr == 0 else n_slices - 1 - stage

    def rows_of(k):
        return slice(k * slice_rows, (k + 1) * slice_rows)

    def u_rows(k, first_visit):
        if first_visit:
            block = u_ref[:, k * SSM_SLICE:(k + 1) * SSM_SLICE, :]
            ut_ref[rows_of(k), :] = jnp.swapaxes(block, 0, 1).reshape(slice_rows, SSM_UBLK)
        return ut_ref[rows_of(k), :]

    def input_proj(stage):
        for dr in range(2):
            k = visited(dr, stage)
            hs_ref[dr, rows_of(k), :] = _dot(u_rows(k, stage < n_slices // 2).astype(BF16), bm_ref[dr])

    def output_proj(stage):
        for dr in range(2):
            r = rows_of(visited(dr, stage))
            part = _dot(hb_ref[dr, r, :], cm_ref[dr])
            if stage < n_slices // 2:
                yacc_ref[r, :] = part
            else:
                yacc_ref[r, :] += part

    a_parts = []
    for dr in range(2):
        a = ab_ref[dr, 0]
        a_parts.append((jnp.broadcast_to(a[:, :sb], (rows, sb)), jnp.broadcast_to(a[:, sb:], (rows, sb))))

    def store_pair(dr, t_low, low, high):
        pair_rows = slice(t_low * rows, (t_low + 2) * rows)
        hb_ref[dr, pair_rows, :sb] = jnp.concatenate([low[0], high[0]], axis=0).astype(BF16)
        hb_ref[dr, pair_rows, sb:] = jnp.concatenate([low[1], high[1]], axis=0).astype(BF16)

    def step_pairs(stage, update, store):
        for step in range(0, SSM_SLICE, 2):
            i0 = stage * SSM_SLICE + step
            states = [[None, None], [None, None]]
            for sub in range(2):
                for dr, t in enumerate((i0 + sub, seq - 1 - i0 - sub)):
                    states[dr][sub] = update(dr, slice(t * rows, (t + 1) * rows))
            if store:
                store_pair(0, i0, states[0][0], states[0][1])
                store_pair(1, seq - 2 - i0, states[1][1], states[1][0])

    def scan_slice(stage, carry):
        carry = list(carry)

        def update(dr, sl):
            ar, ai = a_parts[dr]
            hr, hi = carry[2 * dr], carry[2 * dr + 1]
            b = hs_ref[dr, sl, :]
            nr = ar * hr - ai * hi + b[:, :sb]
            ni = ar * hi + ai * hr + b[:, sb:]
            if chunked:
                hs_ref[dr, sl, :sb] = nr
                hs_ref[dr, sl, sb:] = ni
            carry[2 * dr], carry[2 * dr + 1] = nr, ni
            return nr, ni

        step_pairs(stage, update, store=not chunked)
        return tuple(carry)

    def fix_slice(stage, carry):
        carry = list(carry)

        def update(dr, sl):
            ar, ai = a_parts[dr]
            zr, zi = carry[2 * dr], carry[2 * dr + 1]
            nr = ar * zr - ai * zi
            ni = ar * zi + ai * zr
            carry[2 * dr], carry[2 * dr + 1] = nr, ni
            h = hs_ref[dr, sl, :]
            return h[:, :sb] + nr, h[:, sb:] + ni

        step_pairs(stage, update, store=True)
        return tuple(carry)

    zero = jnp.zeros((rows, sb), F32)
    fin = (zero, zero, zero, zero)
    input_proj(0)
    for stage in range(n_slices):
        if stage + 1 < n_slices:
            input_proj(stage + 1)
        fin = scan_slice(stage, fin)
        if not chunked and stage >= 1:
            output_proj(stage - 1)

    if want_final:
        for dr in range(2):
            fin_ref[dr, 0, :, :sb] = fin[2 * dr]
            fin_ref[dr, 0, :, sb:] = fin[2 * dr + 1]

    if chunked:
        entry = []
        for dr in range(2):
            ar, ai = a_parts[dr]
            pr, pi = ar[0:1], ai[0:1]
            for _ in range(int(math.log2(seq))):
                pr, pi = pr * pr - pi * pi, 2.0 * pr * pi
            fr, fi = fin[2 * dr], fin[2 * dr + 1]
            s0 = s0_ref[dr, 0, 0]
            er, ei = s0[:, :sb], s0[:, sb:]
            order = range(rows) if dr == 0 else range(rows - 1, -1, -1)
            rows_r, rows_i = [None] * rows, [None] * rows
            for c in order:
                rows_r[c], rows_i[c] = er, ei
                er, ei = (pr * er - pi * ei + fr[c:c + 1], pr * ei + pi * er + fi[c:c + 1])
            entry += [jnp.concatenate(rows_r, axis=0), jnp.concatenate(rows_i, axis=0)]
        carry = tuple(entry)
        for stage in range(n_slices):
            carry = fix_slice(stage, carry)
            if stage >= 1:
                output_proj(stage - 1)
    output_proj(n_slices - 1)

    y = _gelu_tanh(yacc_ref[...] + d_ref[...] * ut_ref[...])
    y_ref[...] = y.reshape(seq, rows, SSM_UBLK)


def _ssm(u, l, w, s0, want_final):
    r, seq, _ = u.shape
    chunked = s0 is not None
    rows = V7X_SUBLANES
    in_specs = [pl.BlockSpec((rows, seq, SSM_UBLK), lambda j, i: (i, 0, j)),
                pl.BlockSpec((None, 2, 1, SSM_GBLOCK, SSM_GROUP, 2 * SSM_STATE), lambda j, i: (l, 0, j, 0, 0, 0)),
                pl.BlockSpec((None, 2, 1, 1, 2 * SSM_SBLK), lambda j, i: (l, 0, j, 0, 0)),
                pl.BlockSpec((None, 2, 1, SSM_GBLOCK, 2 * SSM_STATE, SSM_GROUP), lambda j, i: (l, 0, j, 0, 0, 0)),
                pl.BlockSpec((None, 1, SSM_UBLK), lambda j, i: (l, 0, j))]
    args = [u, w["ssm_b"], w["ssm_a"], w["ssm_c"], w["ssm_d"]]
    if chunked:
        in_specs.append(pl.BlockSpec((None, 2, 1, 1, 1, 2 * SSM_SBLK), lambda j, i: (l, 0, i, j, 0, 0)))
        args.append(s0)
    out_specs = [pl.BlockSpec((seq, rows, SSM_UBLK), lambda j, i: (0, i, j))]
    out_shape = [jax.ShapeDtypeStruct((seq, r, SSM_WIDTH), F32)]
    if want_final:
        out_specs.append(pl.BlockSpec((2, 1, rows, 2 * SSM_SBLK), lambda j, i: (0, j, i, 0)))
        out_shape.append(jax.ShapeDtypeStruct((2, SSM_NBLK, r, 2 * SSM_SBLK), F32))
    return pl.pallas_call(
        functools.partial(_ssm_kernel, chunked, want_final),
        grid=(SSM_NBLK, r // rows),
        in_specs=in_specs, out_specs=out_specs, out_shape=out_shape,
        scratch_shapes=[pltpu.VMEM((2, seq * rows, 2 * SSM_SBLK), F32), pltpu.VMEM((2, seq * rows, 2 * SSM_SBLK), BF16),
                        pltpu.VMEM((seq * rows, SSM_UBLK), F32), pltpu.VMEM((seq * rows, SSM_UBLK), F32),
                        pltpu.VMEM((2, SSM_UBLK, 2 * SSM_SBLK), BF16), pltpu.VMEM((2, 2 * SSM_SBLK, SSM_UBLK), BF16)],
        compiler_params=_params(2),
        name="ssm_latent" if chunked else "ssm_context",
    )(*args)


def _merge_kernel(last, x_ref, sh_ref, sc_ref, ga_ref, g_ref, oa_ref, ob_ref, oc_ref, y_ref,
                  wg_ref, wm_ref, wglu_ref, bglu_ref, wbr_ref, wout_ref, fn_ref, o_ref):
    x = x_ref[...]
    h = _modulated(x, g_ref, sh_ref, sc_ref).astype(BF16)
    y = y_ref[...]
    o_d = y * _sigmoid(_dot(y.astype(BF16), wglu_ref[...]) + bglu_ref[...])
    branches = (oa_ref[...].astype(F32), ob_ref[...].astype(F32), oc_ref[...].astype(F32), o_d)
    acc = None
    for n, o in enumerate(branches):
        gate = _dot(h, wg_ref[:, n * BRANCH_WIDTH:(n + 1) * BRANCH_WIDTH])
        br = _dot((o * (gate * _sigmoid(gate))).astype(BF16), wbr_ref[n])
        term = _sigmoid(_dot(h, wm_ref[:, n * D_MODEL:(n + 1) * D_MODEL])) * br
        acc = term if acc is None else acc + term
    xn = x + ga_ref[0] * _dot(acc.astype(BF16), wout_ref[...])
    if last:
        xn = _rms(xn) * fn_ref[...]
    o_ref[...] = xn


def _merge(x, shift, scale, gate, seq, oa, ob, oc, y, l, w, final_norm, last):
    n = x.shape[0]
    tm = MERGE_TM
    row = lambda i: (i, 0)
    mod_spec = _mod_spec(shift.shape[0], tm, seq)
    in_specs = [pl.BlockSpec((tm, D_MODEL), row), mod_spec, mod_spec, mod_spec, _layer_spec((1, D_MODEL), l),
                pl.BlockSpec((tm, 512), row), pl.BlockSpec((tm, 512), row), pl.BlockSpec((tm, 512), row),
                pl.BlockSpec((tm, 512), row),
                _layer_spec((D_MODEL, N_BRANCH * BRANCH_WIDTH), l), _layer_spec((D_MODEL, N_BRANCH * D_MODEL), l),
                _layer_spec((SSM_WIDTH, SSM_WIDTH), l), _layer_spec((1, SSM_WIDTH), l),
                _layer_spec((N_BRANCH, BRANCH_WIDTH, D_MODEL), l), _layer_spec((D_MODEL, D_MODEL), l),
                _const_spec((1, D_MODEL))]
    return pl.pallas_call(
        functools.partial(_merge_kernel, last),
        grid=(n // tm,),
        in_specs=in_specs,
        out_specs=pl.BlockSpec((tm, D_MODEL), row),
        out_shape=jax.ShapeDtypeStruct((n, D_MODEL), F32),
        compiler_params=_params(1),
        name="merge",
    )(x, shift, scale, gate, w["norm_g"], oa, ob, oc, y, w["wg"], w["wm"], w["wglu"], w["bglu"],
      w["wbr"], w["wout"], final_norm)


def _rope_tables(n_tok, rot_dim):
    f32 = np.float32
    rows = n_tok // GRID_W
    row = np.repeat(np.arange(rows, dtype=f32), GRID_W)
    col = np.tile(np.arange(GRID_W, dtype=f32), rows)
    quarter = rot_dim // 4
    inv = (f32(ROPE_THETA) ** (-np.arange(quarter, dtype=f32) / f32(quarter))).astype(f32)
    ang = np.concatenate([row[:, None] * inv, col[:, None] * inv], axis=-1).astype(f32)
    return np.cos(ang).astype(f32), np.sin(ang).astype(f32)


def _rope_lane_tables(n_tok):
    c, s = _rope_tables(n_tok, MLA_ROPE)
    ones = np.ones((n_tok, MLA_NOPE), np.float32)
    pad = MLA_HEAD_PAD - MLA_NOPE - MLA_ROPE
    c_mla = np.concatenate([ones, c, c, np.ones((n_tok, pad), np.float32)], axis=1)
    s_mla = np.concatenate([0.0 * ones, -s, s, np.zeros((n_tok, pad), np.float32)], axis=1)
    c, s = _rope_tables(n_tok, DIFF_HEAD_DIM)
    c_64 = np.concatenate([c, c, c, c], axis=1)
    s_64 = np.concatenate([-s, s, -s, s], axis=1)
    return tuple(jnp.asarray(a, F32) for a in (c_mla, s_mla, c_64, s_64))


def _in_output_half(v):
    zeros = jnp.zeros_like(v[..., 0, :])
    return jnp.stack([jnp.concatenate([v[..., 0, :], zeros], axis=-1),
                      jnp.concatenate([zeros, v[..., 1, :]], axis=-1)], axis=-2)


def _weights(w_in, p):
    w_t = jnp.swapaxes(w_in, 1, 2).reshape(DEPTH * D_IN, D_MODEL)
    wmix = _transposed_rows(w_t, _IN_MIX_START, _M_WIDTH, _M_WIDTH // 2)
    wg = _transposed_rows(w_t, _IN_GATE_START, N_BRANCH * BRANCH_WIDTH, W_IN_BLOCK)
    wm = _transposed_rows(w_t, _IN_MERGE_START, N_BRANCH * D_MODEL, W_IN_BLOCK)
    zeros = lambda wd: jnp.zeros((DEPTH, D_MODEL, wd), F32)
    kpe_lo = MLA_Q_LORA + MLA_KV_LORA
    whead = jnp.concatenate([w_in[:, :, :kpe_lo], zeros(MLA_NOPE), w_in[:, :, kpe_lo:_IN_MIX_START],
                             zeros(MLA_HEAD_PAD - MLA_NOPE - MLA_ROPE)], axis=2).astype(BF16)

    hq = MLA_NOPE + MLA_ROPE
    wqb = p["w_mla_q_b"].reshape(DEPTH, MLA_Q_LORA, MLA_HEADS, hq)
    wqb = jnp.pad(wqb, ((0, 0), (0, 0), (0, 0), (0, MLA_HEAD_PAD - hq))).reshape(DEPTH, MLA_Q_LORA, -1).astype(BF16)
    wkv = p["w_mla_kv_b"].reshape(DEPTH, MLA_KV_LORA, MLA_HEADS, MLA_NOPE + MLA_V)
    wk = jnp.pad(wkv[..., :MLA_NOPE], ((0, 0), (0, 0), (0, 0), (0, MLA_HEAD_PAD - MLA_NOPE)))
    wk = wk.reshape(DEPTH, MLA_KV_LORA, -1).astype(BF16)
    wv = _in_output_half(wkv[..., MLA_NOPE:].reshape(DEPTH, MLA_KV_LORA, MLA_HEADS // 2, 2, MLA_V))
    wv = wv.reshape(DEPTH, MLA_KV_LORA, -1).astype(BF16)
    egv = np.zeros((GQA_KV_WIDTH, 2 * GQA_KV_HEADS * V_PAD), np.float32)
    for g in range(GQA_KV_HEADS):
        for d in range(GQA_HEAD_DIM):
            egv[g * GQA_HEAD_DIM + d, (2 * g) * V_PAD + d] = 1.0
            egv[g * GQA_HEAD_DIM + d, (2 * g + 1) * V_PAD + GQA_HEAD_DIM + d] = 1.0
    egv = jnp.asarray(egv, BF16)
    gmat = jnp.asarray(np.kron(np.eye(GQA_HEADS), np.full((GQA_HEAD_DIM, GQA_HEAD_DIM), 1.0 / GQA_HEAD_DIM)), BF16)

    lam = lax.complex(p["ssm_a_re"], p["ssm_a_im"])
    dt = jnp.exp(p["ssm_log_dt"])[..., None]
    abar = jnp.exp(lam * dt)
    bbar = ((abar - 1.0) / lam)[..., None] * lax.complex(p["ssm_b_re"], p["ssm_b_im"])
    blk = lambda a: a.reshape((DEPTH, 2, SSM_NBLK, SSM_GBLOCK) + a.shape[3:])
    b_t = jnp.swapaxes(blk(bbar), -1, -2)
    ssm_b = jnp.concatenate([jnp.real(b_t), jnp.imag(b_t)], axis=-1).astype(BF16)
    ab = blk(abar).reshape(DEPTH, 2, SSM_NBLK, 1, SSM_SBLK)
    ssm_a = jnp.concatenate([jnp.real(ab), jnp.imag(ab)], axis=-1)
    c_re = jnp.swapaxes(blk(p["ssm_c_re"]), -1, -2)
    c_im = jnp.swapaxes(blk(p["ssm_c_im"]), -1, -2)
    ssm_c = jnp.concatenate([c_re, -c_im], axis=-2).astype(BF16)

    row = lambda a: a[:, None, :]
    return dict(
        norm_g=row(p["norm_g"]), whead=whead, wmix=wmix, qn=row(p["mla_q_norm"]), wqb=wqb,
        kvn=row(p["mla_kv_norm"]), wk=wk, wv=wv,
        gqn=row(jnp.tile(p["gqa_q_norm"], (1, GQA_HEADS))), gkn=row(jnp.tile(p["gqa_k_norm"], (1, GQA_KV_HEADS))),
        gmat=gmat, egv=egv,
        lam_parts=jnp.stack([p["diff_lq1"], p["diff_lk1"], p["diff_lq2"], p["diff_lk2"]], axis=1),
        subln=row(p["diff_subln"]),
        ssm_b=ssm_b, ssm_a=ssm_a, ssm_c=ssm_c, ssm_d=row(p["ssm_d"]),
        wg=wg, wm=wm, wglu=p["ssm_glu_w"].astype(BF16), bglu=row(p["ssm_glu_b"]),
        wbr=p["w_branch_out"].astype(BF16), wout=p["w_out"].astype(BF16),
    )


def _mixers(pr, b, t, l, w, lam_init, s0):
    q, k, v, dq, dk, dv, gq, gk, gv, u = pr[:10]
    three = lambda a: a.reshape(b, a.shape[0] // b, a.shape[-1])
    k, v, dk, dv, gk, gv = (three(a) for a in (k, v, dk, dv, gk, gv))
    mla_heads = tuple((hd, hd) for hd in range(MLA_HEADS))
    gqa_heads = tuple((hd // GQA_GROUP, 2 * (hd // GQA_GROUP) + hd % 2) for hd in range(GQA_HEADS))
    diff_specs = (_layer_spec((4, DIFF_HEAD_DIM), l), _layer_spec((1, DIFF_V), l))
    diff_extra = (w["lam_parts"], w["subln"])
    if s0 is None:
        o_a, o_b, o_c = _attention_short(lam_init, mla_heads, gqa_heads,
                                         (three(q), k, v, three(dq), dk, dv, three(gq), gk, gv),
                                         diff_specs, diff_extra)
    else:
        o_a = _attention(functools.partial(_attn_heads_kernel, mla_heads, MLA_HEAD_PAD), "attn_mla",
                         three(q), k, v)
        o_b = _attention(functools.partial(_diff_attn_kernel, lam_init), "attn_diff", three(dq), dk, dv,
                         extra_specs=diff_specs, extra=diff_extra)
        o_c = _attention(functools.partial(_attn_heads_kernel, gqa_heads, GQA_HEAD_DIM), "attn_gqa",
                         three(gq), gk, gv)
    n = b * t
    u_rows = u.reshape(n // SSM_SEQ, SSM_SEQ, SSM_WIDTH)
    if s0 is None:
        y_t, fin = _ssm(u_rows, l, w, None, True)
    else:
        (y_t,) = _ssm(u_rows, l, w, s0, False)
        fin = None
    y = jnp.swapaxes(y_t, 0, 1).reshape(n, SSM_WIDTH)
    flat = lambda a: a.reshape(n, 512)
    return flat(o_a), flat(o_b), flat(o_c), y, fin


def kernel(x_prompt, x_sample, cache_mla_ckv, cache_mla_krope, cache_diff_k, cache_diff_v, cache_gqa_k, cache_gqa_v, state_ssm, c, c_ctx, norm_g, w_mod, b_mod, w_in, mla_q_norm, w_mla_q_b, mla_kv_norm, w_mla_kv_b, diff_lq1, diff_lk1, diff_lq2, diff_lk2, diff_subln, gqa_q_norm, gqa_k_norm, ssm_a_re, ssm_a_im, ssm_log_dt, ssm_b_re, ssm_b_im, ssm_c_re, ssm_c_im, ssm_d, ssm_glu_w, ssm_glu_b, w_branch_out, w_out, final_norm):
    p = dict(norm_g=norm_g, mla_q_norm=mla_q_norm, w_mla_q_b=w_mla_q_b, mla_kv_norm=mla_kv_norm,
             w_mla_kv_b=w_mla_kv_b, diff_lq1=diff_lq1, diff_lk1=diff_lk1, diff_lq2=diff_lq2, diff_lk2=diff_lk2,
             diff_subln=diff_subln, gqa_q_norm=gqa_q_norm, gqa_k_norm=gqa_k_norm, ssm_a_re=ssm_a_re,
             ssm_a_im=ssm_a_im, ssm_log_dt=ssm_log_dt, ssm_b_re=ssm_b_re, ssm_b_im=ssm_b_im, ssm_c_re=ssm_c_re,
             ssm_c_im=ssm_c_im, ssm_d=ssm_d, ssm_glu_w=ssm_glu_w, ssm_glu_b=ssm_glu_b,
             w_branch_out=w_branch_out, w_out=w_out)
    bp, tp, _ = x_prompt.shape
    bs, ts, _ = x_sample.shape
    past = cache_mla_ckv.shape[2]
    assert tp == SSM_SEQ and ts % SSM_SEQ == 0 and ts // SSM_SEQ == V7X_SUBLANES
    rope = _rope_lane_tables(ts)
    w = _weights(w_in, p)

    cvec = jnp.concatenate([c_ctx[None], c, jnp.zeros((V7X_SUBLANES - 1 - bs, D_MODEL), F32)], axis=0)
    mod = _modulation(cvec, w_mod, b_mod)

    feature_major = lambda a: jnp.moveaxis(a, 2, -1).reshape(bs, DEPTH, -1, past)
    pad_lo = MLA_NOPE
    pad_hi = MLA_HEAD_PAD - MLA_NOPE - MLA_ROPE
    kpe_ctx = jnp.pad(feature_major(cache_mla_krope), ((0, 0), (0, 0), (pad_lo, pad_hi), (0, 0)))
    dk_ctx, gk_ctx, gv_ctx = (feature_major(a) for a in (cache_diff_k, cache_gqa_k, cache_gqa_v))
    dv_ctx = cache_diff_v
    s0_all = state_ssm.reshape(bs, DEPTH, 2, SSM_NBLK, SSM_GBLOCK * SSM_STATE, 2)
    s0_all = jnp.transpose(s0_all, (1, 2, 0, 3, 5, 4)).reshape(DEPTH, 2, bs, SSM_NBLK, 1, 2 * SSM_SBLK)

    xp = x_prompt.reshape(bp * tp, D_MODEL)
    xs = x_sample.reshape(bs * ts, D_MODEL)
    fn = final_norm[None]
    sides = ()
    states = []
    for l in range(DEPTH):
        lam_init = 0.8 - 0.6 * math.exp(-0.3 * l)
        last = l == DEPTH - 1
        sh, sc, ga = (mod[l, :, i * D_MODEL:(i + 1) * D_MODEL][:, None, :] for i in range(3))

        pr = _proj(xp, sh[0:1], sc[0:1], tp, l, w, prev_sides=sides)
        o_a, o_b, o_c, y, fin = _mixers(pr, bp, tp, l, w, lam_init, None)
        xp = _merge(xp, sh[0:1], sc[0:1], ga[0:1], tp, o_a, o_b, o_c, y, l, w, fn, last)
        sides = tuple(pr[10:])
        fin = fin.reshape(2, SSM_NBLK, bp, 2, SSM_GBLOCK, SSM_STATE)
        states.append(jnp.transpose(fin, (2, 0, 1, 4, 5, 3)).reshape(bp, 2, SSM_GROUPS, SSM_STATE, 2))

        pr = _proj(xs, sh[1:1 + bs], sc[1:1 + bs], ts, l, w, rope=rope,
                   cached=(cache_mla_ckv, kpe_ctx, dk_ctx, dv_ctx, gk_ctx, gv_ctx))
        o_a, o_b, o_c, y, _ = _mixers(pr, bs, ts, l, w, lam_init, s0_all)
        xs = _merge(xs, sh[1:1 + bs], sc[1:1 + bs], ga[1:1 + bs], ts, o_a, o_b, o_c, y, l, w, fn, last)

    ckv_s, kpe_t, dk_t, dv_s, gk_t, gv_t = sides
    untranspose = lambda a: jnp.swapaxes(a, 2, 3)
    return (xp.reshape(bp, tp, D_MODEL), xs.reshape(bs, ts, D_MODEL),
            ckv_s, untranspose(kpe_t),
            untranspose(dk_t).reshape(bp, DEPTH, tp, DIFF_HEADS, 2, DIFF_HEAD_DIM), dv_s,
            untranspose(gk_t).reshape(bp, DEPTH, tp, GQA_KV_HEADS, GQA_HEAD_DIM),
            untranspose(gv_t).reshape(bp, DEPTH, tp, GQA_KV_HEADS, GQA_HEAD_DIM),
            jnp.stack(states, axis=1))
```

```python
import functools
import math

import jax
import jax.numpy as jnp
import numpy as np
from jax import lax
from jax.experimental import pallas as pl
from jax.experimental.pallas import tpu as pltpu

F32 = jnp.float32
BF16 = jnp.bfloat16

D_MODEL = 1024
DEPTH = 2
GRID_W = 64
ROPE_THETA = 10000.0
EPS = 1e-6

MLA_HEADS = 8
MLA_NOPE = 64
MLA_ROPE = 32
MLA_V = 64
MLA_Q_LORA = 256
MLA_KV_LORA = 128
MLA_SCALE = (MLA_NOPE + MLA_ROPE) ** -0.5
MLA_HEAD_PAD = 128
DIFF_HEADS = 4
DIFF_HEAD_DIM = 64
DIFF_V = 2 * DIFF_HEAD_DIM
DIFF_SCALE = DIFF_HEAD_DIM ** -0.5
GQA_HEADS = 8
GQA_KV_HEADS = 2
GQA_HEAD_DIM = 64
GQA_GROUP = GQA_HEADS // GQA_KV_HEADS
GQA_KV_WIDTH = GQA_KV_HEADS * GQA_HEAD_DIM
GQA_SCALE = GQA_HEAD_DIM ** -0.5
SSM_WIDTH = 512
SSM_GROUP = 16
SSM_GROUPS = SSM_WIDTH // SSM_GROUP
SSM_STATE = 64
N_BRANCH = 4
BRANCH_WIDTH = 512

IN_SPLITS = (MLA_Q_LORA, MLA_KV_LORA, MLA_ROPE, 512, 512, 512, 512, 128, 128, SSM_WIDTH,
             N_BRANCH * BRANCH_WIDTH, N_BRANCH * D_MODEL)
D_IN = sum(IN_SPLITS)

V7X_LANES = 128
V7X_SUBLANES = 8
V7X_VMEM_LIMIT_BYTES = 56 * 1024 * 1024

_H_QA = (0, 256)
_H_KVA = (256, 384)
_H_KPE = (384, 512)
_H_WIDTH = 512
_M_DQ = (0, 512)
_M_DK = (512, 1024)
_M_DV = (1024, 1536)
_M_GQ = (1536, 2048)
_M_GK = (2048, 2176)
_M_GV = (2176, 2304)
_M_U = (2304, 2816)
_M_WIDTH = 2816
_IN_MIX_START = MLA_Q_LORA + MLA_KV_LORA + MLA_ROPE
_IN_GATE_START = _IN_MIX_START + _M_WIDTH
_IN_MERGE_START = _IN_GATE_START + N_BRANCH * BRANCH_WIDTH

SSM_GBLOCK = 8
SSM_NBLK = SSM_GROUPS // SSM_GBLOCK
SSM_SBLK = SSM_GBLOCK * SSM_STATE
SSM_UBLK = SSM_GBLOCK * SSM_GROUP
SSM_SEQ = 256
SSM_SLICE = 32

V_PAD = 128

W_IN_BLOCK = 1024
PROJ_TM = 512
MERGE_TM = 512
ATTN_TQ_LONG = 512
ATTN_SEQ_PER_STEP = 2
ATTN_GROUP_LONG = 2


def _params(n_axes, vmem=V7X_VMEM_LIMIT_BYTES):
    return pltpu.CompilerParams(dimension_semantics=("arbitrary",) * n_axes, vmem_limit_bytes=vmem)


def _const_spec(shape):
    zeros = (0,) * len(shape)
    return pl.BlockSpec(shape, lambda *_: zeros, pipeline_mode=pl.Buffered(1))


def _layer_spec(shape, l):
    zeros = (0,) * len(shape)
    return pl.BlockSpec((None,) + tuple(shape), lambda *_: (l,) + zeros, pipeline_mode=pl.Buffered(1))


def _mod_spec(n_rows, tm, seq):
    if n_rows == 1:
        return pl.BlockSpec((1, 1, D_MODEL), lambda i: (0, 0, 0))
    return pl.BlockSpec((1, 1, D_MODEL), lambda i: (i * tm // seq, 0, 0))


def _dot(a, b):
    return jnp.dot(a, b, preferred_element_type=F32)


def _dot_nt(a, b):
    return lax.dot_general(a, b, (((1,), (1,)), ((), ())), preferred_element_type=F32)


def _rms(x):
    return x * lax.rsqrt(jnp.mean(x * x, axis=-1, keepdims=True) + EPS)


def _sigmoid(x):
    return 0.5 * jnp.tanh(0.5 * x) + 0.5


def _gelu_tanh(x):
    return 0.5 * x * (1.0 + jnp.tanh(math.sqrt(2.0 / math.pi) * (x + 0.044715 * (x * x * x))))


def _group_mean_sq(x, gmat):
    return _dot((x * x).astype(BF16), gmat)


def _rope_chunk(x, cos, sin, half):
    lane = lax.broadcasted_iota(jnp.int32, x.shape, 1)
    first = (lane % (2 * half)) < half
    swapped = jnp.where(first, pltpu.roll(x, V7X_LANES - half, 1), pltpu.roll(x, half, 1))
    return x * cos + swapped * sin


def _transpose_cast_kernel(row0, blk, n_blocks, w_hbm, o_ref, buf, sem):
    step = pl.program_id(0) * n_blocks + pl.program_id(1)
    n_steps = DEPTH * n_blocks
    slot = step % 2

    def fetch(step_, slot_):
        start = pl.multiple_of((step_ // n_blocks) * D_IN + row0 + (step_ % n_blocks) * blk, V7X_SUBLANES)
        return pltpu.make_async_copy(w_hbm.at[pl.ds(start, blk), :], buf.at[slot_], sem.at[slot_])

    @pl.when(step == 0)
    def _():
        fetch(step, slot).start()

    @pl.when(step + 1 < n_steps)
    def _():
        fetch(step + 1, 1 - slot).start()

    fetch(step, slot).wait()
    o_ref[...] = buf[slot].astype(BF16).T


def _transposed_rows(w_t, row0, n_rows, blk):
    n_blocks = n_rows // blk
    return pl.pallas_call(
        functools.partial(_transpose_cast_kernel, row0, blk, n_blocks),
        grid=(DEPTH, n_blocks),
        in_specs=[pl.BlockSpec(memory_space=pl.ANY)],
        out_specs=pl.BlockSpec((None, D_MODEL, blk), lambda l, j: (l, 0, j)),
        out_shape=jax.ShapeDtypeStruct((DEPTH, D_MODEL, n_rows), BF16),
        scratch_shapes=[pltpu.VMEM((2, blk, D_MODEL), F32), pltpu.SemaphoreType.DMA((2,))],
        compiler_params=_params(2),
        name="w_in_columns",
    )(w_t)


def _mod_kernel(c_ref, w_ref, b_ref, o_ref):
    c = c_ref[...]
    a = (c * _sigmoid(c)).astype(BF16)
    o_ref[0] = _dot(a, w_ref[0].astype(BF16)) + b_ref[0]


def _modulation(cvec, w_mod, b_mod):
    tn = 512
    return pl.pallas_call(
        _mod_kernel,
        grid=(DEPTH, 3 * D_MODEL // tn),
        in_specs=[pl.BlockSpec((V7X_SUBLANES, D_MODEL), lambda l, j: (0, 0)),
                  pl.BlockSpec((1, D_MODEL, tn), lambda l, j: (l, 0, j)),
                  pl.BlockSpec((1, 1, tn), lambda l, j: (l, 0, j))],
        out_specs=pl.BlockSpec((1, V7X_SUBLANES, tn), lambda l, j: (l, 0, j)),
        out_shape=jax.ShapeDtypeStruct((DEPTH, V7X_SUBLANES, 3 * D_MODEL), F32),
        compiler_params=_params(2),
        name="modulation",
    )(cvec, w_mod, b_mod.reshape(DEPTH, 1, 3 * D_MODEL))


def _modulated(x, g_ref, sh_ref, sc_ref):
    return (_rms(x) * g_ref[...]) * (1.0 + sc_ref[0]) + sh_ref[0]


def _proj_kernel(latent, n_prev, first_layer, *refs):
    (x_ref, sh_ref, sc_ref, g_ref, wh_ref, wx_ref, qn_ref, wqb_ref, kvn_ref, wk_ref, wv_ref,
     gqn_ref, gkn_ref, gm_ref, egv_ref) = refs[:15]
    pos = 15
    if latent:
        c_mla, s_mla, c_64, s_64 = refs[pos:pos + 4]
        cached = refs[pos + 4:pos + 10]
        pos += 10
    pos += n_prev
    (q_o, k_o, v_o, dq_o, dk_o, dv_o, gq_o, gk_o, gv_o, u_o) = refs[pos:pos + 10]
    pos += 10
    if not latent:
        ckv_f, kpe_t, dk_t, dv_f, gk_t, gv_t = refs[pos:pos + 6]

    def mla_keys_values(ckv_b, kpe):
        kn = _dot(ckv_b, wk_ref[...])
        for hd in range(MLA_HEADS):
            sl = slice(hd * MLA_HEAD_PAD, (hd + 1) * MLA_HEAD_PAD)
            k_o[:, sl] = (kn[:, sl] + kpe).astype(BF16)
        v_o[...] = _dot(ckv_b, wv_ref[...]).astype(BF16)

    def cached_tokens():
        ckv_c, kpe_t, dk_t, dv_c, gk_t, gv_t = cached
        mla_keys_values(ckv_c[...].astype(BF16), kpe_t[...].T)
        dk_o[...] = dk_t[...].T.astype(BF16)
        for hd in range(DIFF_HEADS):
            dv_o[:, hd * DIFF_V:(hd + 1) * DIFF_V] = dv_c[:, hd, :].astype(BF16)
        gk_o[...] = gk_t[...].T.astype(BF16)
        gv_o[...] = _dot(gv_t[...].T.astype(BF16), egv_ref[...]).astype(BF16)

    def new_tokens():
        _proj_new_tokens(latent, first_layer, refs[:15], (c_mla, s_mla, c_64, s_64) if latent else None,
                         refs[pos - 10:pos], refs[pos:pos + 6] if not latent else None, mla_keys_values)

    if latent:
        pl.when(pl.program_id(1) == 0)(cached_tokens)
        pl.when(pl.program_id(1) > 0)(new_tokens)
    else:
        new_tokens()


def _proj_new_tokens(latent, first_layer, in_refs, rope_refs, out_refs, side_refs, mla_keys_values):
    (x_ref, sh_ref, sc_ref, g_ref, wh_ref, wx_ref, qn_ref, wqb_ref, kvn_ref, wk_ref, wv_ref,
     gqn_ref, gkn_ref, gm_ref, egv_ref) = in_refs
    (q_o, k_o, v_o, dq_o, dk_o, dv_o, gq_o, gk_o, gv_o, u_o) = out_refs
    if latent:
        c_mla, s_mla, c_64, s_64 = rope_refs
    else:
        ckv_f, kpe_t, dk_t, dv_f, gk_t, gv_t = side_refs

    h = _modulated(x_ref[...], g_ref, sh_ref, sc_ref).astype(BF16)
    tm = h.shape[0]

    p_head = _dot(h, wh_ref[...])
    p_mix = _dot(h, wx_ref[...])

    def head(cols):
        return p_head[:, cols[0]:cols[1]]

    def mix(cols):
        return p_mix[:, cols[0]:cols[1]]

    def rope_mla(x):
        return _rope_chunk(x, c_mla[...], s_mla[...], MLA_ROPE // 2) if latent else x

    def rope_64(x):
        if not latent:
            return x
        c, s = c_64[...], s_64[...]
        chunks = [_rope_chunk(x[:, i:i + V7X_LANES], c, s, DIFF_HEAD_DIM // 2)
                  for i in range(0, x.shape[1], V7X_LANES)]
        return chunks[0] if len(chunks) == 1 else jnp.concatenate(chunks, axis=1)

    def per_sequence(x):
        return [x[b * SSM_SEQ:(b + 1) * SSM_SEQ] for b in range(tm // SSM_SEQ)]

    def slot(b):
        return (b,) if first_layer is None else (b, first_layer)

    def zero_other_slots(ref):
        if first_layer is not None:
            for b in range(ref.shape[0]):
                for other in range(ref.shape[1]):
                    if other != first_layer:
                        ref[b, other] = jnp.zeros(ref.shape[2:], ref.dtype)

    qn = (_rms(head(_H_QA)) * qn_ref[...]).astype(BF16)
    q = _dot(qn, wqb_ref[...])
    for hd in range(MLA_HEADS):
        sl = slice(hd * MLA_HEAD_PAD, (hd + 1) * MLA_HEAD_PAD)
        q_o[:, sl] = (rope_mla(q[:, sl]) * MLA_SCALE).astype(BF16)
    ckv = _rms(head(_H_KVA)) * kvn_ref[...]
    kpe = head(_H_KPE)
    if not latent:
        for b, (c_b, k_b) in enumerate(zip(per_sequence(ckv), per_sequence(kpe))):
            ckv_f[slot(b)] = c_b
            kpe_t[slot(b)] = k_b.T[MLA_NOPE:MLA_NOPE + MLA_ROPE, :]
        zero_other_slots(ckv_f)
        zero_other_slots(kpe_t)
    mla_keys_values(ckv.astype(BF16), rope_mla(kpe))

    dq_o[...] = (rope_64(mix(_M_DQ)) * DIFF_SCALE).astype(BF16)
    dk = rope_64(mix(_M_DK))
    dv = mix(_M_DV)
    dk_o[...] = dk.astype(BF16)
    dv_o[...] = dv.astype(BF16)

    gq = mix(_M_GQ)
    gq = gq * lax.rsqrt(_group_mean_sq(gq, gm_ref[...]) + EPS) * gqn_ref[...]
    gq_o[...] = (rope_64(gq) * GQA_SCALE).astype(BF16)
    gk = mix(_M_GK)
    kw = GQA_KV_WIDTH
    gk = gk * lax.rsqrt(_group_mean_sq(gk, gm_ref[:kw, :kw]) + EPS) * gkn_ref[...]
    gv = mix(_M_GV)
    if not latent:
        for b, (dk_b, dv_b, gk_b, gv_b) in enumerate(zip(*(per_sequence(a) for a in (dk, dv, gk, gv)))):
            dk_t[slot(b)] = dk_b.T
            for hd in range(DIFF_HEADS):
                dv_f[slot(b) + (slice(None), hd, slice(None))] = dv_b[:, hd * DIFF_V:(hd + 1) * DIFF_V]
            gk_t[slot(b)] = gk_b.T
            gv_t[slot(b)] = gv_b.T
        for ref in (dk_t, dv_f, gk_t, gv_t):
            zero_other_slots(ref)
    gk_o[...] = rope_64(gk).astype(BF16)
    gv_o[...] = _dot(gv.astype(BF16), egv_ref[...]).astype(BF16)

    u_o[...] = mix(_M_U)


def _side_shapes(n_seq):
    return [((n_seq, DEPTH, SSM_SEQ, MLA_KV_LORA), (SSM_SEQ, MLA_KV_LORA)),
            ((n_seq, DEPTH, MLA_ROPE, SSM_SEQ), (MLA_ROPE, SSM_SEQ)),
            ((n_seq, DEPTH, 512, SSM_SEQ), (512, SSM_SEQ)),
            ((n_seq, DEPTH, SSM_SEQ, DIFF_HEADS, DIFF_V), (SSM_SEQ, DIFF_HEADS, DIFF_V)),
            ((n_seq, DEPTH, GQA_KV_WIDTH, SSM_SEQ), (GQA_KV_WIDTH, SSM_SEQ)),
            ((n_seq, DEPTH, GQA_KV_WIDTH, SSM_SEQ), (GQA_KV_WIDTH, SSM_SEQ))]


def _proj(x, shift, scale, seq, l, w, rope=None, cached=None, prev_sides=()):
    n = x.shape[0]
    tm = PROJ_TM
    latent = cached is not None
    kw = GQA_KV_WIDTH
    if latent:
        tiles = seq // tm
        assert cached[0].shape[2] == tm
        grid = (n // seq, 1 + tiles)
        row = lambda b, j: (b * tiles + jnp.maximum(j - 1, 0), 0)
        kv_row = lambda b, j: (b * (1 + tiles) + j, 0)
        n_kv = (n // seq) * (cached[0].shape[2] + seq)
        mod_spec = pl.BlockSpec((1, 1, D_MODEL), lambda b, j: (b, 0, 0))
    else:
        grid = (n // tm,)
        row = kv_row = lambda i: (i, 0)
        n_kv = n
        mod_spec = _mod_spec(shift.shape[0], tm, seq)
    in_specs = [pl.BlockSpec((tm, D_MODEL), row), mod_spec, mod_spec,
                _layer_spec((1, D_MODEL), l), _layer_spec((D_MODEL, _H_WIDTH), l), _layer_spec((D_MODEL, _M_WIDTH), l),
                _layer_spec((1, MLA_Q_LORA), l), _layer_spec((MLA_Q_LORA, MLA_HEADS * MLA_HEAD_PAD), l),
                _layer_spec((1, MLA_KV_LORA), l), _layer_spec((MLA_KV_LORA, MLA_HEADS * MLA_HEAD_PAD), l),
                _layer_spec((MLA_KV_LORA, MLA_HEADS * V_PAD), l),
                _layer_spec((1, 512), l), _layer_spec((1, kw), l), _const_spec((512, 512)),
                _const_spec((kw, 2 * GQA_KV_HEADS * V_PAD))]
    args = [x, shift, scale, w["norm_g"], w["whead"], w["wmix"], w["qn"], w["wqb"], w["kvn"], w["wk"], w["wv"],
            w["gqn"], w["gkn"], w["gmat"], w["egv"]]
    if latent:
        tab = pl.BlockSpec((tm, V7X_LANES), lambda b, j: (jnp.maximum(j - 1, 0), 0))
        in_specs += [tab] * 4
        args += list(rope)
        in_specs += [pl.BlockSpec((None, None) + c.shape[2:], lambda b, j, nd=c.ndim: (b, l) + (0,) * (nd - 2))
                     for c in cached]
        args += list(cached)
    n_fixed = len(args)
    in_specs += [pl.BlockSpec(memory_space=pl.ANY)] * len(prev_sides)
    args += list(prev_sides)
    widths = [(1024, BF16, False), (1024, BF16, True), (MLA_HEADS * V_PAD, BF16, True), (512, BF16, False),
              (512, BF16, True), (512, BF16, True), (512, BF16, False), (kw, BF16, True),
              (2 * GQA_KV_HEADS * V_PAD, BF16, True), (512, F32, False)]
    out_specs = [pl.BlockSpec((tm, wd), kv_row if kv else row) for wd, _, kv in widths]
    out_shape = [jax.ShapeDtypeStruct((n_kv if kv else n, wd), dt) for wd, dt, kv in widths]
    aliases = {}
    first_layer = None
    if not latent:
        per_tile = tm // SSM_SEQ
        first_layer = None if prev_sides else l
        for full, blk in _side_shapes(n // SSM_SEQ):
            zeros = (0,) * len(blk)
            if prev_sides:
                out_specs.append(pl.BlockSpec((per_tile, None) + blk, lambda i, zeros=zeros: (i, l) + zeros))
            else:
                out_specs.append(pl.BlockSpec((per_tile, DEPTH) + blk, lambda i, zeros=zeros: (i, 0) + zeros))
            out_shape.append(jax.ShapeDtypeStruct(full, F32))
        aliases = {n_fixed + k: len(widths) + k for k in range(len(prev_sides))}
    return pl.pallas_call(
        functools.partial(_proj_kernel, latent, len(prev_sides), first_layer),
        grid=grid,
        in_specs=in_specs, out_specs=out_specs, out_shape=out_shape,
        input_output_aliases=aliases,
        compiler_params=_params(len(grid)),
        name="proj_latent" if latent else "proj_context",
    )(*args)


def _softmax_maps(maps):
    scores = [_dot_nt(q, k) for q, k, _ in maps]
    tops = [jnp.max(s, axis=-1, keepdims=True) for s in scores]
    weights = [jnp.exp(s - m).astype(BF16) for s, m in zip(scores, tops)]
    outs = []
    for p, (_, _, v) in zip(weights, maps):
        ones = jnp.ones((v.shape[0], V7X_LANES), BF16)
        o = _dot(p, jnp.concatenate([v, ones], axis=1))
        outs.append((o[:, :V7X_LANES], o[:, V7X_LANES:]))
    return outs


def _head_maps(heads, q_width, q_ref, k_ref, v_ref, part):
    return [(q_ref[b, :, hd * q_width:(hd + 1) * q_width],
             k_ref[b, :, heads[hd][0] * q_width:(heads[hd][0] + 1) * q_width],
             v_ref[b, :, heads[hd][1] * V_PAD:(heads[hd][1] + 1) * V_PAD]) for b, hd in part]


def _head_outputs(outs, part, o_ref):
    terms = [o * (1.0 / total) for o, total in outs]
    for (b, hd), low, high in zip(part[0::2], terms[0::2], terms[1::2]):
        o_ref[b, :, (hd // 2) * V7X_LANES:(hd // 2 + 1) * V7X_LANES] = (low + high).astype(BF16)


def _diff_maps(q_ref, k_ref, v_ref, part):
    d = DIFF_HEAD_DIM
    return [(q_ref[b, :, c * d:(c + 1) * d], k_ref[b, :, c * d:(c + 1) * d],
             v_ref[b, :, hd * DIFF_V:(hd + 1) * DIFF_V])
            for b, hd in part for c in (2 * hd, 2 * hd + 1)]


def _diff_lambda(lp_ref, lam_init):
    lp = lp_ref[...]
    return (jnp.exp(jnp.sum(lp[0:1] * lp[1:2], axis=-1, keepdims=True))
            - jnp.exp(jnp.sum(lp[2:3] * lp[3:4], axis=-1, keepdims=True)) + lam_init)


def _diff_outputs(outs, part, lam, lam_init, sub_ref, o_ref):
    mixed = [o1 * (1.0 / t1) - o2 * (lam / t2) for (o1, t1), (o2, t2) in zip(outs[0::2], outs[1::2])]
    normed = [_rms(o) * sub_ref[...] * (1.0 - lam_init) for o in mixed]
    for (b, hd), o in zip(part, normed):
        o_ref[b, :, hd * DIFF_V:(hd + 1) * DIFF_V] = o.astype(BF16)


def _items(q_ref, n_heads):
    return [(b, hd) for b in range(q_ref.shape[0]) for hd in range(n_heads)]


def _attn_heads_kernel(heads, q_width, q_ref, k_ref, v_ref, o_ref):
    items = _items(q_ref, len(heads))
    group = ATTN_GROUP_LONG
    for g0 in range(0, len(items), group):
        part = items[g0:g0 + group]
        _head_outputs(_softmax_maps(_head_maps(heads, q_width, q_ref, k_ref, v_ref, part)), part, o_ref)


def _diff_attn_kernel(lam_init, q_ref, k_ref, v_ref, lp_ref, sub_ref, o_ref):
    lam = _diff_lambda(lp_ref, lam_init)
    items = _items(q_ref, DIFF_HEADS)
    group = ATTN_GROUP_LONG // 2
    for g0 in range(0, len(items), group):
        part = items[g0:g0 + group]
        _diff_outputs(_softmax_maps(_diff_maps(q_ref, k_ref, v_ref, part)), part, lam, lam_init, sub_ref, o_ref)


def _attn_short_kernel(lam_init, mla_heads, gqa_heads, qa_ref, ka_ref, va_ref, qb_ref, kb_ref, vb_ref,
                       qc_ref, kc_ref, vc_ref, lp_ref, sub_ref, oa_ref, ob_ref, oc_ref):
    part_a, part_b, part_c = _items(qa_ref, len(mla_heads)), _items(qb_ref, DIFF_HEADS), _items(qc_ref, len(gqa_heads))
    maps_a = _head_maps(mla_heads, MLA_HEAD_PAD, qa_ref, ka_ref, va_ref, part_a)
    maps_b = _diff_maps(qb_ref, kb_ref, vb_ref, part_b)
    maps_c = _head_maps(gqa_heads, GQA_HEAD_DIM, qc_ref, kc_ref, vc_ref, part_c)
    outs = _softmax_maps(maps_a + maps_b + maps_c)
    n_a, n_b = len(maps_a), len(maps_b)
    _head_outputs(outs[:n_a], part_a, oa_ref)
    _diff_outputs(outs[n_a:n_a + n_b], part_b, _diff_lambda(lp_ref, lam_init), lam_init, sub_ref, ob_ref)
    _head_outputs(outs[n_a + n_b:], part_c, oc_ref)


def _attention_short(lam_init, mla_heads, gqa_heads, qkv, extra_specs, extra):
    b, t, _ = qkv[0].shape
    nb = ATTN_SEQ_PER_STEP
    whole = lambda i: (i, 0, 0)
    in_specs = [pl.BlockSpec((nb, t, a.shape[2]), whole) for a in qkv] + list(extra_specs)
    return pl.pallas_call(
        functools.partial(_attn_short_kernel, lam_init, mla_heads, gqa_heads),
        grid=(b // nb,),
        in_specs=in_specs,
        out_specs=[pl.BlockSpec((nb, t, 512), whole)] * 3,
        out_shape=[jax.ShapeDtypeStruct((b, t, 512), BF16)] * 3,
        compiler_params=_params(1),
        name="attn_context",
    )(*qkv, *extra)


def _attention(kernel, name, q, k, v, extra_specs=(), extra=()):
    b, t, wq = q.shape
    s = k.shape[1]
    tq = ATTN_TQ_LONG
    in_specs = [pl.BlockSpec((1, tq, wq), lambda i, j: (i, j, 0)),
                pl.BlockSpec((1, s, k.shape[2]), lambda i, j: (i, 0, 0)),
                pl.BlockSpec((1, s, v.shape[2]), lambda i, j: (i, 0, 0))]
    in_specs += list(extra_specs)
    return pl.pallas_call(
        kernel,
        grid=(b, t // tq),
        in_specs=in_specs,
        out_specs=pl.BlockSpec((1, tq, 512), lambda i, j: (i, j, 0)),
        out_shape=jax.ShapeDtypeStruct((b, t, 512), BF16),
        compiler_params=_params(2),
        name=name,
    )(q, k, v, *extra)


def _ssm_kernel(chunked, want_final, *refs):
    u_ref, bg_ref, ab_ref, cg_ref, d_ref = refs[:5]
    pos = 5
    if chunked:
        s0_ref = refs[pos]
        pos += 1
    y_ref = refs[pos]
    pos += 1
    if want_final:
        fin_ref = refs[pos]
        pos += 1
    hs_ref, hb_ref, yacc_ref, ut_ref, bm_ref, cm_ref = refs[pos:pos + 6]

    @pl.when(pl.program_id(1) == 0)
    def _():
        bm_ref[...] = jnp.zeros(bm_ref.shape, bm_ref.dtype)
        cm_ref[...] = jnp.zeros(cm_ref.shape, cm_ref.dtype)
        for dr in range(2):
            for g in range(SSM_GBLOCK):
                ins = slice(g * SSM_GROUP, (g + 1) * SSM_GROUP)
                for part in range(2):
                    states = slice(part * SSM_SBLK + g * SSM_STATE, part * SSM_SBLK + (g + 1) * SSM_STATE)
                    bm_ref[dr, ins, states] = bg_ref[dr, 0, g, :, part * SSM_STATE:(part + 1) * SSM_STATE]
                    cm_ref[dr, states, ins] = cg_ref[dr, 0, g, part * SSM_STATE:(part + 1) * SSM_STATE, :]

    seq, rows, sb = SSM_SEQ, V7X_SUBLANES, SSM_SBLK
    n_slices = seq // SSM_SLICE
    slice_rows = SSM_SLICE * rows

    def visited(dr, stage):
        return stage if dr == 0 else n_slices - 1 - stage

    def rows_of(k):
        return slice(k * slice_rows, (k + 1) * slice_rows)

    def u_rows(k, first_visit):
        if first_visit:
            block = u_ref[:, k * SSM_SLICE:(k + 1) * SSM_SLICE, :]
            ut_ref[rows_of(k), :] = jnp.swapaxes(block, 0, 1).reshape(slice_rows, SSM_UBLK)
        return ut_ref[rows_of(k), :]

    def input_proj(stage):
        for dr in range(2):
            k = visited(dr, stage)
            hs_ref[dr, rows_of(k), :] = _dot(u_rows(k, stage < n_slices // 2).astype(BF16), bm_ref[dr])

    def output_proj(stage):
        for dr in range(2):
            r = rows_of(visited(dr, stage))
            part = _dot(hb_ref[dr, r, :], cm_ref[dr])
            if stage < n_slices // 2:
                yacc_ref[r, :] = part
            else:
                yacc_ref[r, :] += part

    a_parts = []
    for dr in range(2):
        a = ab_ref[dr, 0]
        a_parts.append((jnp.broadcast_to(a[:, :sb], (rows, sb)), jnp.broadcast_to(a[:, sb:], (rows, sb))))

    def store_pair(dr, t_low, low, high):
        pair_rows = slice(t_low * rows, (t_low + 2) * rows)
        hb_ref[dr, pair_rows, :sb] = jnp.concatenate([low[0], high[0]], axis=0).astype(BF16)
        hb_ref[dr, pair_rows, sb:] = jnp.concatenate([low[1], high[1]], axis=0).astype(BF16)

    def step_pairs(stage, update, store):
        for step in range(0, SSM_SLICE, 2):
            i0 = stage * SSM_SLICE + step
            states = [[None, None], [None, None]]
            for sub in range(2):
                for dr, t in enumerate((i0 + sub, seq - 1 - i0 - sub)):
                    states[dr][sub] = update(dr, slice(t * rows, (t + 1) * rows))
            if store:
                store_pair(0, i0, states[0][0], states[0][1])
                store_pair(1, seq - 2 - i0, states[1][1], states[1][0])

    def scan_slice(stage, carry):
        carry = list(carry)

        def update(dr, sl):
            ar, ai = a_parts[dr]
            hr, hi = carry[2 * dr], carry[2 * dr + 1]
            b = hs_ref[dr, sl, :]
            nr = ar * hr - ai * hi + b[:, :sb]
            ni = ar * hi + ai * hr + b[:, sb:]
            if chunked:
                hs_ref[dr, sl, :sb] = nr
                hs_ref[dr, sl, sb:] = ni
            carry[2 * dr], carry[2 * dr + 1] = nr, ni
            return nr, ni

        step_pairs(stage, update, store=not chunked)
        return tuple(carry)

    def fix_slice(stage, carry):
        carry = list(carry)

        def update(dr, sl):
            ar, ai = a_parts[dr]
            zr, zi = carry[2 * dr], carry[2 * dr + 1]
            nr = ar * zr - ai * zi
            ni = ar * zi + ai * zr
            carry[2 * dr], carry[2 * dr + 1] = nr, ni
            h = hs_ref[dr, sl, :]
            return h[:, :sb] + nr, h[:, sb:] + ni

        step_pairs(stage, update, store=True)
        return tuple(carry)

    zero = jnp.zeros((rows, sb), F32)
    fin = (zero, zero, zero, zero)
    input_proj(0)
    for stage in range(n_slices):
        if stage + 1 < n_slices:
            input_proj(stage + 1)
        fin = scan_slice(stage, fin)
        if not chunked and stage >= 1:
            output_proj(stage - 1)

    if want_final:
        for dr in range(2):
            fin_ref[dr, 0, :, :sb] = fin[2 * dr]
            fin_ref[dr, 0, :, sb:] = fin[2 * dr + 1]

    if chunked:
        entry = []
        for dr in range(2):
            ar, ai = a_parts[dr]
            pr, pi = ar[0:1], ai[0:1]
            for _ in range(int(math.log2(seq))):
                pr, pi = pr * pr - pi * pi, 2.0 * pr * pi
            fr, fi = fin[2 * dr], fin[2 * dr + 1]
            s0 = s0_ref[dr, 0, 0]
            er, ei = s0[:, :sb], s0[:, sb:]
            order = range(rows) if dr == 0 else range(rows - 1, -1, -1)
            rows_r, rows_i = [None] * rows, [None] * rows
            for c in order:
                rows_r[c], rows_i[c] = er, ei
                er, ei = (pr * er - pi * ei + fr[c:c + 1], pr * ei + pi * er + fi[c:c + 1])
            entry += [jnp.concatenate(rows_r, axis=0), jnp.concatenate(rows_i, axis=0)]
        carry = tuple(entry)
        for stage in range(n_slices):
            carry = fix_slice(stage, carry)
            if stage >= 1:
                output_proj(stage - 1)
    output_proj(n_slices - 1)

    y = _gelu_tanh(yacc_ref[...] + d_ref[...] * ut_ref[...])
    y_ref[...] = y.reshape(seq, rows, SSM_UBLK)


def _ssm(u, l, w, s0, want_final):
    r, seq, _ = u.shape
    chunked = s0 is not None
    rows = V7X_SUBLANES
    in_specs = [pl.BlockSpec((rows, seq, SSM_UBLK), lambda j, i: (i, 0, j)),
                pl.BlockSpec((None, 2, 1, SSM_GBLOCK, SSM_GROUP, 2 * SSM_STATE), lambda j, i: (l, 0, j, 0, 0, 0)),
                pl.BlockSpec((None, 2, 1, 1, 2 * SSM_SBLK), lambda j, i: (l, 0, j, 0, 0)),
                pl.BlockSpec((None, 2, 1, SSM_GBLOCK, 2 * SSM_STATE, SSM_GROUP), lambda j, i: (l, 0, j, 0, 0, 0)),
                pl.BlockSpec((None, 1, SSM_UBLK), lambda j, i: (l, 0, j))]
    args = [u, w["ssm_b"], w["ssm_a"], w["ssm_c"], w["ssm_d"]]
    if chunked:
        in_specs.append(pl.BlockSpec((None, 2, 1, 1, 1, 2 * SSM_SBLK), lambda j, i: (l, 0, i, j, 0, 0)))
        args.append(s0)
    out_specs = [pl.BlockSpec((seq, rows, SSM_UBLK), lambda j, i: (0, i, j))]
    out_shape = [jax.ShapeDtypeStruct((seq, r, SSM_WIDTH), F32)]
    if want_final:
        out_specs.append(pl.BlockSpec((2, 1, rows, 2 * SSM_SBLK), lambda j, i: (0, j, i, 0)))
        out_shape.append(jax.ShapeDtypeStruct((2, SSM_NBLK, r, 2 * SSM_SBLK), F32))
    return pl.pallas_call(
        functools.partial(_ssm_kernel, chunked, want_final),
        grid=(SSM_NBLK, r // rows),
        in_specs=in_specs, out_specs=out_specs, out_shape=out_shape,
        scratch_shapes=[pltpu.VMEM((2, seq * rows, 2 * SSM_SBLK), F32), pltpu.VMEM((2, seq * rows, 2 * SSM_SBLK), BF16),
                        pltpu.VMEM((seq * rows, SSM_UBLK), F32), pltpu.VMEM((seq * rows, SSM_UBLK), F32),
                        pltpu.VMEM((2, SSM_UBLK, 2 * SSM_SBLK), BF16), pltpu.VMEM((2, 2 * SSM_SBLK, SSM_UBLK), BF16)],
        compiler_params=_params(2),
        name="ssm_latent" if chunked else "ssm_context",
    )(*args)


def _merge_kernel(last, x_ref, sh_ref, sc_ref, ga_ref, g_ref, oa_ref, ob_ref, oc_ref, y_ref,
                  wg_ref, wm_ref, wglu_ref, bglu_ref, wbr_ref, wout_ref, fn_ref, o_ref):
    x = x_ref[...]
    h = _modulated(x, g_ref, sh_ref, sc_ref).astype(BF16)
    y = y_ref[...]
    o_d = y * _sigmoid(_dot(y.astype(BF16), wglu_ref[...]) + bglu_ref[...])
    branches = (oa_ref[...].astype(F32), ob_ref[...].astype(F32), oc_ref[...].astype(F32), o_d)
    acc = None
    for n, o in enumerate(branches):
        gate = _dot(h, wg_ref[:, n * BRANCH_WIDTH:(n + 1) * BRANCH_WIDTH])
        br = _dot((o * (gate * _sigmoid(gate))).astype(BF16), wbr_ref[n])
        term = _sigmoid(_dot(h, wm_ref[:, n * D_MODEL:(n + 1) * D_MODEL])) * br
        acc = term if acc is None else acc + term
    xn = x + ga_ref[0] * _dot(acc.astype(BF16), wout_ref[...])
    if last:
        xn = _rms(xn) * fn_ref[...]
    o_ref[...] = xn


def _merge(x, shift, scale, gate, seq, oa, ob, oc, y, l, w, final_norm, last):
    n = x.shape[0]
    tm = MERGE_TM
    row = lambda i: (i, 0)
    mod_spec = _mod_spec(shift.shape[0], tm, seq)
    in_specs = [pl.BlockSpec((tm, D_MODEL), row), mod_spec, mod_spec, mod_spec, _layer_spec((1, D_MODEL), l),
                pl.BlockSpec((tm, 512), row), pl.BlockSpec((tm, 512), row), pl.BlockSpec((tm, 512), row),
                pl.BlockSpec((tm, 512), row),
                _layer_spec((D_MODEL, N_BRANCH * BRANCH_WIDTH), l), _layer_spec((D_MODEL, N_BRANCH * D_MODEL), l),
                _layer_spec((SSM_WIDTH, SSM_WIDTH), l), _layer_spec((1, SSM_WIDTH), l),
                _layer_spec((N_BRANCH, BRANCH_WIDTH, D_MODEL), l), _layer_spec((D_MODEL, D_MODEL), l),
                _const_spec((1, D_MODEL))]
    return pl.pallas_call(
        functools.partial(_merge_kernel, last),
        grid=(n // tm,),
        in_specs=in_specs,
        out_specs=pl.BlockSpec((tm, D_MODEL), row),
        out_shape=jax.ShapeDtypeStruct((n, D_MODEL), F32),
        compiler_params=_params(1),
        name="merge",
    )(x, shift, scale, gate, w["norm_g"], oa, ob, oc, y, w["wg"], w["wm"], w["wglu"], w["bglu"],
      w["wbr"], w["wout"], final_norm)


def _rope_tables(n_tok, rot_dim):
    f32 = np.float32
    rows = n_tok // GRID_W
    row = np.repeat(np.arange(rows, dtype=f32), GRID_W)
    col = np.tile(np.arange(GRID_W, dtype=f32), rows)
    quarter = rot_dim // 4
    inv = (f32(ROPE_THETA) ** (-np.arange(quarter, dtype=f32) / f32(quarter))).astype(f32)
    ang = np.concatenate([row[:, None] * inv, col[:, None] * inv], axis=-1).astype(f32)
    return np.cos(ang).astype(f32), np.sin(ang).astype(f32)


def _rope_lane_tables(n_tok):
    c, s = _rope_tables(n_tok, MLA_ROPE)
    ones = np.ones((n_tok, MLA_NOPE), np.float32)
    pad = MLA_HEAD_PAD - MLA_NOPE - MLA_ROPE
    c_mla = np.concatenate([ones, c, c, np.ones((n_tok, pad), np.float32)], axis=1)
    s_mla = np.concatenate([0.0 * ones, -s, s, np.zeros((n_tok, pad), np.float32)], axis=1)
    c, s = _rope_tables(n_tok, DIFF_HEAD_DIM)
    c_64 = np.concatenate([c, c, c, c], axis=1)
    s_64 = np.concatenate([-s, s, -s, s], axis=1)
    return tuple(jnp.asarray(a, F32) for a in (c_mla, s_mla, c_64, s_64))


def _in_output_half(v):
    zeros = jnp.zeros_like(v[..., 0, :])
    return jnp.stack([jnp.concatenate([v[..., 0, :], zeros], axis=-1),
                      jnp.concatenate([zeros, v[..., 1, :]], axis=-1)], axis=-2)


def _weights(w_in, p):
    w_t = jnp.swapaxes(w_in, 1, 2).reshape(DEPTH * D_IN, D_MODEL)
    wmix = _transposed_rows(w_t, _IN_MIX_START, _M_WIDTH, _M_WIDTH // 2)
    wg = _transposed_rows(w_t, _IN_GATE_START, N_BRANCH * BRANCH_WIDTH, W_IN_BLOCK)
    wm = _transposed_rows(w_t, _IN_MERGE_START, N_BRANCH * D_MODEL, W_IN_BLOCK)
    zeros = lambda wd: jnp.zeros((DEPTH, D_MODEL, wd), F32)
    kpe_lo = MLA_Q_LORA + MLA_KV_LORA
    whead = jnp.concatenate([w_in[:, :, :kpe_lo], zeros(MLA_NOPE), w_in[:, :, kpe_lo:_IN_MIX_START],
                             zeros(MLA_HEAD_PAD - MLA_NOPE - MLA_ROPE)], axis=2).astype(BF16)

    hq = MLA_NOPE + MLA_ROPE
    wqb = p["w_mla_q_b"].reshape(DEPTH, MLA_Q_LORA, MLA_HEADS, hq)
    wqb = jnp.pad(wqb, ((0, 0), (0, 0), (0, 0), (0, MLA_HEAD_PAD - hq))).reshape(DEPTH, MLA_Q_LORA, -1).astype(BF16)
    wkv = p["w_mla_kv_b"].reshape(DEPTH, MLA_KV_LORA, MLA_HEADS, MLA_NOPE + MLA_V)
    wk = jnp.pad(wkv[..., :MLA_NOPE], ((0, 0), (0, 0), (0, 0), (0, MLA_HEAD_PAD - MLA_NOPE)))
    wk = wk.reshape(DEPTH, MLA_KV_LORA, -1).astype(BF16)
    wv = _in_output_half(wkv[..., MLA_NOPE:].reshape(DEPTH, MLA_KV_LORA, MLA_HEADS // 2, 2, MLA_V))
    wv = wv.reshape(DEPTH, MLA_KV_LORA, -1).astype(BF16)
    egv = np.zeros((GQA_KV_WIDTH, 2 * GQA_KV_HEADS * V_PAD), np.float32)
    for g in range(GQA_KV_HEADS):
        for d in range(GQA_HEAD_DIM):
            egv[g * GQA_HEAD_DIM + d, (2 * g) * V_PAD + d] = 1.0
            egv[g * GQA_HEAD_DIM + d, (2 * g + 1) * V_PAD + GQA_HEAD_DIM + d] = 1.0
    egv = jnp.asarray(egv, BF16)
    gmat = jnp.asarray(np.kron(np.eye(GQA_HEADS), np.full((GQA_HEAD_DIM, GQA_HEAD_DIM), 1.0 / GQA_HEAD_DIM)), BF16)

    lam = lax.complex(p["ssm_a_re"], p["ssm_a_im"])
    dt = jnp.exp(p["ssm_log_dt"])[..., None]
    abar = jnp.exp(lam * dt)
    bbar = ((abar - 1.0) / lam)[..., None] * lax.complex(p["ssm_b_re"], p["ssm_b_im"])
    blk = lambda a: a.reshape((DEPTH, 2, SSM_NBLK, SSM_GBLOCK) + a.shape[3:])
    b_t = jnp.swapaxes(blk(bbar), -1, -2)
    ssm_b = jnp.concatenate([jnp.real(b_t), jnp.imag(b_t)], axis=-1).astype(BF16)
    ab = blk(abar).reshape(DEPTH, 2, SSM_NBLK, 1, SSM_SBLK)
    ssm_a = jnp.concatenate([jnp.real(ab), jnp.imag(ab)], axis=-1)
    c_re = jnp.swapaxes(blk(p["ssm_c_re"]), -1, -2)
    c_im = jnp.swapaxes(blk(p["ssm_c_im"]), -1, -2)
    ssm_c = jnp.concatenate([c_re, -c_im], axis=-2).astype(BF16)

    row = lambda a: a[:, None, :]
    return dict(
        norm_g=row(p["norm_g"]), whead=whead, wmix=wmix, qn=row(p["mla_q_norm"]), wqb=wqb,
        kvn=row(p["mla_kv_norm"]), wk=wk, wv=wv,
        gqn=row(jnp.tile(p["gqa_q_norm"], (1, GQA_HEADS))), gkn=row(jnp.tile(p["gqa_k_norm"], (1, GQA_KV_HEADS))),
        gmat=gmat, egv=egv,
        lam_parts=jnp.stack([p["diff_lq1"], p["diff_lk1"], p["diff_lq2"], p["diff_lk2"]], axis=1),
        subln=row(p["diff_subln"]),
        ssm_b=ssm_b, ssm_a=ssm_a, ssm_c=ssm_c, ssm_d=row(p["ssm_d"]),
        wg=wg, wm=wm, wglu=p["ssm_glu_w"].astype(BF16), bglu=row(p["ssm_glu_b"]),
        wbr=p["w_branch_out"].astype(BF16), wout=p["w_out"].astype(BF16),
    )


def _mixers(pr, b, t, l, w, lam_init, s0):
    q, k, v, dq, dk, dv, gq, gk, gv, u = pr[:10]
    three = lambda a: a.reshape(b, a.shape[0] // b, a.shape[-1])
    k, v, dk, dv, gk, gv = (three(a) for a in (k, v, dk, dv, gk, gv))
    mla_heads = tuple((hd, hd) for hd in range(MLA_HEADS))
    gqa_heads = tuple((hd // GQA_GROUP, 2 * (hd // GQA_GROUP) + hd % 2) for hd in range(GQA_HEADS))
    diff_specs = (_layer_spec((4, DIFF_HEAD_DIM), l), _layer_spec((1, DIFF_V), l))
    diff_extra = (w["lam_parts"], w["subln"])
    if s0 is None:
        o_a, o_b, o_c = _attention_short(lam_init, mla_heads, gqa_heads,
                                         (three(q), k, v, three(dq), dk, dv, three(gq), gk, gv),
                                         diff_specs, diff_extra)
    else:
        o_a = _attention(functools.partial(_attn_heads_kernel, mla_heads, MLA_HEAD_PAD), "attn_mla",
                         three(q), k, v)
        o_b = _attention(functools.partial(_diff_attn_kernel, lam_init), "attn_diff", three(dq), dk, dv,
                         extra_specs=diff_specs, extra=diff_extra)
        o_c = _attention(functools.partial(_attn_heads_kernel, gqa_heads, GQA_HEAD_DIM), "attn_gqa",
                         three(gq), gk, gv)
    n = b * t
    u_rows = u.reshape(n // SSM_SEQ, SSM_SEQ, SSM_WIDTH)
    if s0 is None:
        y_t, fin = _ssm(u_rows, l, w, None, True)
    else:
        (y_t,) = _ssm(u_rows, l, w, s0, False)
        fin = None
    y = jnp.swapaxes(y_t, 0, 1).reshape(n, SSM_WIDTH)
    flat = lambda a: a.reshape(n, 512)
    return flat(o_a), flat(o_b), flat(o_c), y, fin


def kernel(x_prompt, x_sample, cache_mla_ckv, cache_mla_krope, cache_diff_k, cache_diff_v, cache_gqa_k, cache_gqa_v, state_ssm, c, c_ctx, norm_g, w_mod, b_mod, w_in, mla_q_norm, w_mla_q_b, mla_kv_norm, w_mla_kv_b, diff_lq1, diff_lk1, diff_lq2, diff_lk2, diff_subln, gqa_q_norm, gqa_k_norm, ssm_a_re, ssm_a_im, ssm_log_dt, ssm_b_re, ssm_b_im, ssm_c_re, ssm_c_im, ssm_d, ssm_glu_w, ssm_glu_b, w_branch_out, w_out, final_norm):
    p = dict(norm_g=norm_g, mla_q_norm=mla_q_norm, w_mla_q_b=w_mla_q_b, mla_kv_norm=mla_kv_norm,
             w_mla_kv_b=w_mla_kv_b, diff_lq1=diff_lq1, diff_lk1=diff_lk1, diff_lq2=diff_lq2, diff_lk2=diff_lk2,
             diff_subln=diff_subln, gqa_q_norm=gqa_q_norm, gqa_k_norm=gqa_k_norm, ssm_a_re=ssm_a_re,
             ssm_a_im=ssm_a_im, ssm_log_dt=ssm_log_dt, ssm_b_re=ssm_b_re, ssm_b_im=ssm_b_im, ssm_c_re=ssm_c_re,
             ssm_c_im=ssm_c_im, ssm_d=ssm_d, ssm_glu_w=ssm_glu_w, ssm_glu_b=ssm_glu_b,
             w_branch_out=w_branch_out, w_out=w_out)
    bp, tp, _ = x_prompt.shape
    bs, ts, _ = x_sample.shape
    past = cache_mla_ckv.shape[2]
    assert tp == SSM_SEQ and ts % SSM_SEQ == 0 and ts // SSM_SEQ == V7X_SUBLANES
    rope = _rope_lane_tables(ts)
    w = _weights(w_in, p)

    cvec = jnp.concatenate([c_ctx[None], c, jnp.zeros((V7X_SUBLANES - 1 - bs, D_MODEL), F32)], axis=0)
    mod = _modulation(cvec, w_mod, b_mod)

    feature_major = lambda a: jnp.moveaxis(a, 2, -1).reshape(bs, DEPTH, -1, past)
    pad_lo = MLA_NOPE
    pad_hi = MLA_HEAD_PAD - MLA_NOPE - MLA_ROPE
    kpe_ctx = jnp.pad(feature_major(cache_mla_krope), ((0, 0), (0, 0), (pad_lo, pad_hi), (0, 0)))
    dk_ctx, gk_ctx, gv_ctx = (feature_major(a) for a in (cache_diff_k, cache_gqa_k, cache_gqa_v))
    dv_ctx = cache_diff_v
    s0_all = state_ssm.reshape(bs, DEPTH, 2, SSM_NBLK, SSM_GBLOCK * SSM_STATE, 2)
    s0_all = jnp.transpose(s0_all, (1, 2, 0, 3, 5, 4)).reshape(DEPTH, 2, bs, SSM_NBLK, 1, 2 * SSM_SBLK)

    xp = x_prompt.reshape(bp * tp, D_MODEL)
    xs = x_sample.reshape(bs * ts, D_MODEL)
    fn = final_norm[None]
    sides = ()
    states = []
    for l in range(DEPTH):
        lam_init = 0.8 - 0.6 * math.exp(-0.3 * l)
        last = l == DEPTH - 1
        sh, sc, ga = (mod[l, :, i * D_MODEL:(i + 1) * D_MODEL][:, None, :] for i in range(3))

        pr = _proj(xp, sh[0:1], sc[0:1], tp, l, w, prev_sides=sides)
        o_a, o_b, o_c, y, fin = _mixers(pr, bp, tp, l, w, lam_init, None)
        xp = _merge(xp, sh[0:1], sc[0:1], ga[0:1], tp, o_a, o_b, o_c, y, l, w, fn, last)
        sides = tuple(pr[10:])
        fin = fin.reshape(2, SSM_NBLK, bp, 2, SSM_GBLOCK, SSM_STATE)
        states.append(jnp.transpose(fin, (2, 0, 1, 4, 5, 3)).reshape(bp, 2, SSM_GROUPS, SSM_STATE, 2))

        pr = _proj(xs, sh[1:1 + bs], sc[1:1 + bs], ts, l, w, rope=rope,
                   cached=(cache_mla_ckv, kpe_ctx, dk_ctx, dv_ctx, gk_ctx, gv_ctx))
        o_a, o_b, o_c, y, _ = _mixers(pr, bs, ts, l, w, lam_init, s0_all)
        xs = _merge(xs, sh[1:1 + bs], sc[1:1 + bs], ga[1:1 + bs], ts, o_a, o_b, o_c, y, l, w, fn, last)

    ckv_s, kpe_t, dk_t, dv_s, gk_t, gv_t = sides
    untranspose = lambda a: jnp.swapaxes(a, 2, 3)
    return (xp.reshape(bp, tp, D_MODEL), xs.reshape(bs, ts, D_MODEL),
            ckv_s, untranspose(kpe_t),
            untranspose(dk_t).reshape(bp, DEPTH, tp, DIFF_HEADS, 2, DIFF_HEAD_DIM), dv_s,
            untranspose(gk_t).reshape(bp, DEPTH, tp, GQA_KV_HEADS, GQA_HEAD_DIM),
            untranspose(gv_t).reshape(bp, DEPTH, tp, GQA_KV_HEADS, GQA_HEAD_DIM),
            jnp.stack(states, axis=1))
```

```python
import functools
import math

import jax
import jax.numpy as jnp
import numpy as np
from jax import lax
from jax.experimental import pallas as pl
from jax.experimental.pallas import tpu as pltpu

F32 = jnp.float32
BF16 = jnp.bfloat16

D_MODEL = 1024
DEPTH = 2
GRID_W = 64
ROPE_THETA = 10000.0
EPS = 1e-6

MLA_HEADS = 8
MLA_NOPE = 64
MLA_ROPE = 32
MLA_V = 64
MLA_Q_LORA = 256
MLA_KV_LORA = 128
MLA_SCALE = (MLA_NOPE + MLA_ROPE) ** -0.5
MLA_HEAD_PAD = 128
DIFF_HEADS = 4
DIFF_HEAD_DIM = 64
DIFF_V = 2 * DIFF_HEAD_DIM
DIFF_SCALE = DIFF_HEAD_DIM ** -0.5
GQA_HEADS = 8
GQA_KV_HEADS = 2
GQA_HEAD_DIM = 64
GQA_GROUP = GQA_HEADS // GQA_KV_HEADS
GQA_KV_WIDTH = GQA_KV_HEADS * GQA_HEAD_DIM
GQA_SCALE = GQA_HEAD_DIM ** -0.5
SSM_WIDTH = 512
SSM_GROUP = 16
SSM_GROUPS = SSM_WIDTH // SSM_GROUP
SSM_STATE = 64
N_BRANCH = 4
BRANCH_WIDTH = 512

IN_SPLITS = (MLA_Q_LORA, MLA_KV_LORA, MLA_ROPE, 512, 512, 512, 512, 128, 128, SSM_WIDTH,
             N_BRANCH * BRANCH_WIDTH, N_BRANCH * D_MODEL)
D_IN = sum(IN_SPLITS)

V7X_LANES = 128
V7X_SUBLANES = 8
V7X_VMEM_LIMIT_BYTES = 56 * 1024 * 1024

_H_QA = (0, 256)
_H_KVA = (256, 384)
_H_KPE = (384, 512)
_H_WIDTH = 512
_M_DQ = (0, 512)
_M_DK = (512, 1024)
_M_DV = (1024, 1536)
_M_GQ = (1536, 2048)
_M_GK = (2048, 2176)
_M_GV = (2176, 2304)
_M_U = (2304, 2816)
_M_WIDTH = 2816
_IN_MIX_START = MLA_Q_LORA + MLA_KV_LORA + MLA_ROPE
_IN_GATE_START = _IN_MIX_START + _M_WIDTH
_IN_MERGE_START = _IN_GATE_START + N_BRANCH * BRANCH_WIDTH

SSM_GBLOCK = 8
SSM_NBLK = SSM_GROUPS // SSM_GBLOCK
SSM_SBLK = SSM_GBLOCK * SSM_STATE
SSM_UBLK = SSM_GBLOCK * SSM_GROUP
SSM_SEQ = 256
SSM_SLICE = 32

V_PAD = 128

W_IN_BLOCK = 1024
PROJ_TM = 512
MERGE_TM = 512
ATTN_TQ_LONG = 512
ATTN_SEQ_PER_STEP = 2
ATTN_GROUP_LONG = 2


def _params(n_axes, vmem=V7X_VMEM_LIMIT_BYTES):
    return pltpu.CompilerParams(dimension_semantics=("arbitrary",) * n_axes, vmem_limit_bytes=vmem)


def _const_spec(shape):
    zeros = (0,) * len(shape)
    return pl.BlockSpec(shape, lambda *_: zeros, pipeline_mode=pl.Buffered(1))


def _layer_spec(shape, l):
    zeros = (0,) * len(shape)
    return pl.BlockSpec((None,) + tuple(shape), lambda *_: (l,) + zeros, pipeline_mode=pl.Buffered(1))


def _mod_spec(n_rows, tm, seq):
    if n_rows == 1:
        return pl.BlockSpec((1, 1, D_MODEL), lambda i: (0, 0, 0))
    return pl.BlockSpec((1, 1, D_MODEL), lambda i: (i * tm // seq, 0, 0))


def _dot(a, b):
    return jnp.dot(a, b, preferred_element_type=F32)


def _dot_nt(a, b):
    return lax.dot_general(a, b, (((1,), (1,)), ((), ())), preferred_element_type=F32)


def _rms(x):
    return x * lax.rsqrt(jnp.mean(x * x, axis=-1, keepdims=True) + EPS)


def _sigmoid(x):
    return 0.5 * jnp.tanh(0.5 * x) + 0.5


def _gelu_tanh(x):
    return 0.5 * x * (1.0 + jnp.tanh(math.sqrt(2.0 / math.pi) * (x + 0.044715 * (x * x * x))))


def _low_half(shape):
    return lax.broadcasted_iota(jnp.int32, shape, 1) % V7X_LANES < V7X_LANES // 2


def _head_mean_sq(x):
    tiles = []
    for i in range(0, x.shape[1], V7X_LANES):
        sq = x[:, i:i + V7X_LANES] * x[:, i:i + V7X_LANES]
        low = _low_half(sq.shape)
        sum_low = jnp.sum(jnp.where(low, sq, 0.0), axis=-1, keepdims=True)
        sum_high = jnp.sum(jnp.where(low, 0.0, sq), axis=-1, keepdims=True)
        tiles.append(jnp.where(low, sum_low, sum_high) * (1.0 / (V7X_LANES // 2)))
    return tiles[0] if len(tiles) == 1 else jnp.concatenate(tiles, axis=1)


def _in_own_half(tile, low):
    return jnp.where(low, tile, 0.0), jnp.where(low, 0.0, tile)


def _rope_chunk(x, cos, sin, half):
    lane = lax.broadcasted_iota(jnp.int32, x.shape, 1)
    first = (lane % (2 * half)) < half
    swapped = jnp.where(first, pltpu.roll(x, V7X_LANES - half, 1), pltpu.roll(x, half, 1))
    return x * cos + swapped * sin


def _transpose_cast_kernel(w_ref, o_ref):
    o_ref[...] = w_ref[...].astype(BF16).T


def _transposed_rows(w_t, row0, n_rows, blk):
    return pl.pallas_call(
        _transpose_cast_kernel,
        grid=(DEPTH, n_rows // blk),
        in_specs=[pl.BlockSpec((pl.Element(blk), pl.Element(D_MODEL)),
                               lambda l, j: (pl.multiple_of(l * D_IN + row0 + j * blk, V7X_SUBLANES), 0))],
        out_specs=pl.BlockSpec((None, D_MODEL, blk), lambda l, j: (l, 0, j)),
        out_shape=jax.ShapeDtypeStruct((DEPTH, D_MODEL, n_rows), BF16),
        compiler_params=_params(2),
        name="w_in_columns",
    )(w_t)


def _mod_kernel(c_ref, w_ref, b_ref, o_ref):
    c = c_ref[...]
    a = (c * _sigmoid(c)).astype(BF16)
    o_ref[0] = _dot(a, w_ref[0].astype(BF16)) + b_ref[0]


def _modulation(cvec, w_mod, b_mod):
    tn = 512
    return pl.pallas_call(
        _mod_kernel,
        grid=(DEPTH, 3 * D_MODEL // tn),
        in_specs=[pl.BlockSpec((V7X_SUBLANES, D_MODEL), lambda l, j: (0, 0)),
                  pl.BlockSpec((1, D_MODEL, tn), lambda l, j: (l, 0, j)),
                  pl.BlockSpec((1, 1, tn), lambda l, j: (l, 0, j))],
        out_specs=pl.BlockSpec((1, V7X_SUBLANES, tn), lambda l, j: (l, 0, j)),
        out_shape=jax.ShapeDtypeStruct((DEPTH, V7X_SUBLANES, 3 * D_MODEL), F32),
        compiler_params=_params(2),
        name="modulation",
    )(cvec, w_mod, b_mod.reshape(DEPTH, 1, 3 * D_MODEL))


def _modulated(x, g_ref, sh_ref, sc_ref):
    return (_rms(x) * g_ref[...]) * (1.0 + sc_ref[0]) + sh_ref[0]


_PROJ_FIXED_INPUTS = 13


def _proj_kernel(latent, n_prev, first_layer, *refs):
    (x_ref, sh_ref, sc_ref, g_ref, wh_ref, wx_ref, qn_ref, wqb_ref, kvn_ref, wk_ref, wv_ref,
     gqn_ref, gkn_ref) = refs[:_PROJ_FIXED_INPUTS]
    pos = _PROJ_FIXED_INPUTS
    if latent:
        c_mla, s_mla, c_64, s_64 = refs[pos:pos + 4]
        cached = refs[pos + 4:pos + 10]
        pos += 10
    pos += n_prev
    (q_o, k_o, v_o, dq_o, dk_o, dv_o, gq_o, gk_o, gv_o, u_o) = refs[pos:pos + 10]
    pos += 10
    if not latent:
        ckv_f, kpe_t, dk_t, dv_f, gk_t, gv_t = refs[pos:pos + 6]

    def mla_keys_values(ckv_b, kpe):
        kn = _dot(ckv_b, wk_ref[...])
        for hd in range(MLA_HEADS):
            sl = slice(hd * MLA_HEAD_PAD, (hd + 1) * MLA_HEAD_PAD)
            k_o[:, sl] = (kn[:, sl] + kpe).astype(BF16)
        vc = _dot(ckv_b, wv_ref[...])
        low = _low_half((vc.shape[0], V7X_LANES))
        for pair in range(MLA_HEADS // 2):
            even, odd = _in_own_half(vc[:, pair * V7X_LANES:(pair + 1) * V7X_LANES], low)
            v_o[:, (2 * pair) * V_PAD:(2 * pair + 1) * V_PAD] = even.astype(BF16)
            v_o[:, (2 * pair + 1) * V_PAD:(2 * pair + 2) * V_PAD] = odd.astype(BF16)

    def gqa_values(gv):
        low = _low_half(gv.shape)
        swapped = pltpu.roll(gv, V7X_LANES // 2, 1)
        blocks = (jnp.where(low, gv, 0.0), jnp.where(low, 0.0, swapped),
                  jnp.where(low, swapped, 0.0), jnp.where(low, 0.0, gv))
        for i, block in enumerate(blocks):
            gv_o[:, i * V_PAD:(i + 1) * V_PAD] = block.astype(BF16)

    def cached_tokens():
        ckv_c, kpe_t, dk_t, dv_c, gk_t, gv_t = cached
        mla_keys_values(ckv_c[...].astype(BF16), kpe_t[...].T)
        dk_o[...] = dk_t[...].T.astype(BF16)
        for hd in range(DIFF_HEADS):
            dv_o[:, hd * DIFF_V:(hd + 1) * DIFF_V] = dv_c[:, hd, :].astype(BF16)
        gk_o[...] = gk_t[...].T.astype(BF16)
        gqa_values(gv_t[...].T)

    def new_tokens():
        _proj_new_tokens(latent, first_layer, refs[:_PROJ_FIXED_INPUTS],
                         (c_mla, s_mla, c_64, s_64) if latent else None, refs[pos - 10:pos],
                         refs[pos:pos + 6] if not latent else None, mla_keys_values, gqa_values)

    if latent:
        pl.when(pl.program_id(1) == 0)(cached_tokens)
        pl.when(pl.program_id(1) > 0)(new_tokens)
    else:
        new_tokens()


def _proj_new_tokens(latent, first_layer, in_refs, rope_refs, out_refs, side_refs, mla_keys_values, gqa_values):
    (x_ref, sh_ref, sc_ref, g_ref, wh_ref, wx_ref, qn_ref, wqb_ref, kvn_ref, wk_ref, wv_ref,
     gqn_ref, gkn_ref) = in_refs
    (q_o, k_o, v_o, dq_o, dk_o, dv_o, gq_o, gk_o, gv_o, u_o) = out_refs
    if latent:
        c_mla, s_mla, c_64, s_64 = rope_refs
    else:
        ckv_f, kpe_t, dk_t, dv_f, gk_t, gv_t = side_refs

    h = _modulated(x_ref[...], g_ref, sh_ref, sc_ref).astype(BF16)
    tm = h.shape[0]

    p_head = _dot(h, wh_ref[...])
    p_mix = _dot(h, wx_ref[...])

    def head(cols):
        return p_head[:, cols[0]:cols[1]]

    def mix(cols):
        return p_mix[:, cols[0]:cols[1]]

    def rope_mla(x):
        return _rope_chunk(x, c_mla[...], s_mla[...], MLA_ROPE // 2) if latent else x

    def rope_64(x):
        if not latent:
            return x
        c, s = c_64[...], s_64[...]
        chunks = [_rope_chunk(x[:, i:i + V7X_LANES], c, s, DIFF_HEAD_DIM // 2)
                  for i in range(0, x.shape[1], V7X_LANES)]
        return chunks[0] if len(chunks) == 1 else jnp.concatenate(chunks, axis=1)

    def per_sequence(x):
        return [x[b * SSM_SEQ:(b + 1) * SSM_SEQ] for b in range(tm // SSM_SEQ)]

    def slot(b):
        return (b,) if first_layer is None else (b, first_layer)

    def zero_other_slots(ref):
        if first_layer is not None:
            for b in range(ref.shape[0]):
                for other in range(ref.shape[1]):
                    if other != first_layer:
                        ref[b, other] = jnp.zeros(ref.shape[2:], ref.dtype)

    qn = (_rms(head(_H_QA)) * qn_ref[...]).astype(BF16)
    q = _dot(qn, wqb_ref[...])
    for hd in range(MLA_HEADS):
        sl = slice(hd * MLA_HEAD_PAD, (hd + 1) * MLA_HEAD_PAD)
        q_o[:, sl] = (rope_mla(q[:, sl]) * MLA_SCALE).astype(BF16)
    ckv = _rms(head(_H_KVA)) * kvn_ref[...]
    kpe = head(_H_KPE)
    if not latent:
        for b, (c_b, k_b) in enumerate(zip(per_sequence(ckv), per_sequence(kpe))):
            ckv_f[slot(b)] = c_b
            kpe_t[slot(b)] = k_b.T[MLA_NOPE:MLA_NOPE + MLA_ROPE, :]
        zero_other_slots(ckv_f)
        zero_other_slots(kpe_t)
    mla_keys_values(ckv.astype(BF16), rope_mla(kpe))

    dq_o[...] = (rope_64(mix(_M_DQ)) * DIFF_SCALE).astype(BF16)
    dk = rope_64(mix(_M_DK))
    dv = mix(_M_DV)
    dk_o[...] = dk.astype(BF16)
    dv_o[...] = dv.astype(BF16)

    gq = mix(_M_GQ)
    gq = gq * lax.rsqrt(_head_mean_sq(gq) + EPS) * gqn_ref[...]
    gq_o[...] = (rope_64(gq) * GQA_SCALE).astype(BF16)
    gk = mix(_M_GK)
    gk = gk * lax.rsqrt(_head_mean_sq(gk) + EPS) * gkn_ref[...]
    gv = mix(_M_GV)
    if not latent:
        for b, (dk_b, dv_b, gk_b, gv_b) in enumerate(zip(*(per_sequence(a) for a in (dk, dv, gk, gv)))):
            dk_t[slot(b)] = dk_b.T
            for hd in range(DIFF_HEADS):
                dv_f[slot(b) + (slice(None), hd, slice(None))] = dv_b[:, hd * DIFF_V:(hd + 1) * DIFF_V]
            gk_t[slot(b)] = gk_b.T
            gv_t[slot(b)] = gv_b.T
        for ref in (dk_t, dv_f, gk_t, gv_t):
            zero_other_slots(ref)
    gk_o[...] = rope_64(gk).astype(BF16)
    gqa_values(gv)

    u_o[...] = mix(_M_U)


def _side_shapes(n_seq):
    return [((n_seq, DEPTH, SSM_SEQ, MLA_KV_LORA), (SSM_SEQ, MLA_KV_LORA)),
            ((n_seq, DEPTH, MLA_ROPE, SSM_SEQ), (MLA_ROPE, SSM_SEQ)),
            ((n_seq, DEPTH, 512, SSM_SEQ), (512, SSM_SEQ)),
            ((n_seq, DEPTH, SSM_SEQ, DIFF_HEADS, DIFF_V), (SSM_SEQ, DIFF_HEADS, DIFF_V)),
            ((n_seq, DEPTH, GQA_KV_WIDTH, SSM_SEQ), (GQA_KV_WIDTH, SSM_SEQ)),
            ((n_seq, DEPTH, GQA_KV_WIDTH, SSM_SEQ), (GQA_KV_WIDTH, SSM_SEQ))]


def _proj(x, shift, scale, seq, l, w, rope=None, cached=None, prev_sides=()):
    n = x.shape[0]
    tm = PROJ_TM
    latent = cached is not None
    kw = GQA_KV_WIDTH
    if latent:
        tiles = seq // tm
        assert cached[0].shape[2] == tm
        grid = (n // seq, 1 + tiles)
        row = lambda b, j: (b * tiles + jnp.maximum(j - 1, 0), 0)
        kv_row = lambda b, j: (b * (1 + tiles) + j, 0)
        n_kv = (n // seq) * (cached[0].shape[2] + seq)
        mod_spec = pl.BlockSpec((1, 1, D_MODEL), lambda b, j: (b, 0, 0))
    else:
        grid = (n // tm,)
        row = kv_row = lambda i: (i, 0)
        n_kv = n
        mod_spec = _mod_spec(shift.shape[0], tm, seq)
    in_specs = [pl.BlockSpec((tm, D_MODEL), row), mod_spec, mod_spec,
                _layer_spec((1, D_MODEL), l), _layer_spec((D_MODEL, _H_WIDTH), l), _layer_spec((D_MODEL, _M_WIDTH), l),
                _layer_spec((1, MLA_Q_LORA), l), _layer_spec((MLA_Q_LORA, MLA_HEADS * MLA_HEAD_PAD), l),
                _layer_spec((1, MLA_KV_LORA), l), _layer_spec((MLA_KV_LORA, MLA_HEADS * MLA_HEAD_PAD), l),
                _layer_spec((MLA_KV_LORA, MLA_HEADS * MLA_V), l),
                _layer_spec((1, 512), l), _layer_spec((1, kw), l)]
    args = [x, shift, scale, w["norm_g"], w["whead"], w["wmix"], w["qn"], w["wqb"], w["kvn"], w["wk"], w["wv"],
            w["gqn"], w["gkn"]]
    assert len(args) == _PROJ_FIXED_INPUTS
    if latent:
        tab = pl.BlockSpec((tm, V7X_LANES), lambda b, j: (jnp.maximum(j - 1, 0), 0))
        in_specs += [tab] * 4
        args += list(rope)
        in_specs += [pl.BlockSpec((None, None) + c.shape[2:], lambda b, j, nd=c.ndim: (b, l) + (0,) * (nd - 2))
                     for c in cached]
        args += list(cached)
    n_fixed = len(args)
    in_specs += [pl.BlockSpec(memory_space=pl.ANY)] * len(prev_sides)
    args += list(prev_sides)
    widths = [(1024, BF16, False), (1024, BF16, True), (MLA_HEADS * V_PAD, BF16, True), (512, BF16, False),
              (512, BF16, True), (512, BF16, True), (512, BF16, False), (kw, BF16, True),
              (2 * GQA_KV_HEADS * V_PAD, BF16, True), (512, F32, False)]
    out_specs = [pl.BlockSpec((tm, wd), kv_row if kv else row) for wd, _, kv in widths]
    out_shape = [jax.ShapeDtypeStruct((n_kv if kv else n, wd), dt) for wd, dt, kv in widths]
    aliases = {}
    first_layer = None
    if not latent:
        per_tile = tm // SSM_SEQ
        first_layer = None if prev_sides else l
        for full, blk in _side_shapes(n // SSM_SEQ):
            zeros = (0,) * len(blk)
            if prev_sides:
                out_specs.append(pl.BlockSpec((per_tile, None) + blk, lambda i, zeros=zeros: (i, l) + zeros))
            else:
                out_specs.append(pl.BlockSpec((per_tile, DEPTH) + blk, lambda i, zeros=zeros: (i, 0) + zeros))
            out_shape.append(jax.ShapeDtypeStruct(full, F32))
        aliases = {n_fixed + k: len(widths) + k for k in range(len(prev_sides))}
    return pl.pallas_call(
        functools.partial(_proj_kernel, latent, len(prev_sides), first_layer),
        grid=grid,
        in_specs=in_specs, out_specs=out_specs, out_shape=out_shape,
        input_output_aliases=aliases,
        compiler_params=_params(len(grid)),
        name="proj_latent" if latent else "proj_context",
    )(*args)


def _softmax_maps(maps):
    scores = [_dot_nt(q, k) for q, k, _ in maps]
    tops = [jnp.max(s, axis=-1, keepdims=True) for s in scores]
    weights = [jnp.exp(s - m).astype(BF16) for s, m in zip(scores, tops)]
    outs = []
    for p, (_, _, v) in zip(weights, maps):
        ones = jnp.ones((v.shape[0], V7X_LANES), BF16)
        o = _dot(p, jnp.concatenate([v, ones], axis=1))
        outs.append((o[:, :V7X_LANES], o[:, V7X_LANES:]))
    return outs


def _head_maps(heads, q_width, q_ref, k_ref, v_ref, part):
    return [(q_ref[b, :, hd * q_width:(hd + 1) * q_width],
             k_ref[b, :, heads[hd][0] * q_width:(heads[hd][0] + 1) * q_width],
             v_ref[b, :, heads[hd][1] * V_PAD:(heads[hd][1] + 1) * V_PAD]) for b, hd in part]


def _head_outputs(outs, part, o_ref):
    terms = [o * (1.0 / total) for o, total in outs]
    for (b, hd), low, high in zip(part[0::2], terms[0::2], terms[1::2]):
        o_ref[b, :, (hd // 2) * V7X_LANES:(hd // 2 + 1) * V7X_LANES] = (low + high).astype(BF16)


def _diff_maps(q_ref, k_ref, v_ref, part):
    d = DIFF_HEAD_DIM
    return [(q_ref[b, :, c * d:(c + 1) * d], k_ref[b, :, c * d:(c + 1) * d],
             v_ref[b, :, hd * DIFF_V:(hd + 1) * DIFF_V])
            for b, hd in part for c in (2 * hd, 2 * hd + 1)]


def _diff_lambda(lp_ref, lam_init):
    lp = lp_ref[...]
    return (jnp.exp(jnp.sum(lp[0:1] * lp[1:2], axis=-1, keepdims=True))
            - jnp.exp(jnp.sum(lp[2:3] * lp[3:4], axis=-1, keepdims=True)) + lam_init)


def _diff_outputs(outs, part, lam, lam_init, sub_ref, o_ref):
    mixed = [o1 * (1.0 / t1) - o2 * (lam / t2) for (o1, t1), (o2, t2) in zip(outs[0::2], outs[1::2])]
    normed = [_rms(o) * sub_ref[...] * (1.0 - lam_init) for o in mixed]
    for (b, hd), o in zip(part, normed):
        o_ref[b, :, hd * DIFF_V:(hd + 1) * DIFF_V] = o.astype(BF16)


def _items(q_ref, n_heads):
    return [(b, hd) for b in range(q_ref.shape[0]) for hd in range(n_heads)]


def _attn_heads_kernel(heads, q_width, q_ref, k_ref, v_ref, o_ref):
    items = _items(q_ref, len(heads))
    group = ATTN_GROUP_LONG
    for g0 in range(0, len(items), group):
        part = items[g0:g0 + group]
        _head_outputs(_softmax_maps(_head_maps(heads, q_width, q_ref, k_ref, v_ref, part)), part, o_ref)


def _diff_attn_kernel(lam_init, q_ref, k_ref, v_ref, lp_ref, sub_ref, o_ref):
    lam = _diff_lambda(lp_ref, lam_init)
    items = _items(q_ref, DIFF_HEADS)
    group = ATTN_GROUP_LONG // 2
    for g0 in range(0, len(items), group):
        part = items[g0:g0 + group]
        _diff_outputs(_softmax_maps(_diff_maps(q_ref, k_ref, v_ref, part)), part, lam, lam_init, sub_ref, o_ref)


def _attn_short_kernel(lam_init, mla_heads, gqa_heads, qa_ref, ka_ref, va_ref, qb_ref, kb_ref, vb_ref,
                       qc_ref, kc_ref, vc_ref, lp_ref, sub_ref, oa_ref, ob_ref, oc_ref):
    part_a, part_b, part_c = _items(qa_ref, len(mla_heads)), _items(qb_ref, DIFF_HEADS), _items(qc_ref, len(gqa_heads))
    maps_a = _head_maps(mla_heads, MLA_HEAD_PAD, qa_ref, ka_ref, va_ref, part_a)
    maps_b = _diff_maps(qb_ref, kb_ref, vb_ref, part_b)
    maps_c = _head_maps(gqa_heads, GQA_HEAD_DIM, qc_ref, kc_ref, vc_ref, part_c)
    outs = _softmax_maps(maps_a + maps_b + maps_c)
    n_a, n_b = len(maps_a), len(maps_b)
    _head_outputs(outs[:n_a], part_a, oa_ref)
    _diff_outputs(outs[n_a:n_a + n_b], part_b, _diff_lambda(lp_ref, lam_init), lam_init, sub_ref, ob_ref)
    _head_outputs(outs[n_a + n_b:], part_c, oc_ref)


def _attention_short(lam_init, mla_heads, gqa_heads, qkv, extra_specs, extra):
    b, t, _ = qkv[0].shape
    nb = ATTN_SEQ_PER_STEP
    whole = lambda i: (i, 0, 0)
    in_specs = [pl.BlockSpec((nb, t, a.shape[2]), whole) for a in qkv] + list(extra_specs)
    return pl.pallas_call(
        functools.partial(_attn_short_kernel, lam_init, mla_heads, gqa_heads),
        grid=(b // nb,),
        in_specs=in_specs,
        out_specs=[pl.BlockSpec((nb, t, 512), whole)] * 3,
        out_shape=[jax.ShapeDtypeStruct((b, t, 512), BF16)] * 3,
        compiler_params=_params(1),
        name="attn_context",
    )(*qkv, *extra)


def _attention(kernel, name, q, k, v, extra_specs=(), extra=()):
    b, t, wq = q.shape
    s = k.shape[1]
    tq = ATTN_TQ_LONG
    in_specs = [pl.BlockSpec((1, tq, wq), lambda i, j: (i, j, 0)),
                pl.BlockSpec((1, s, k.shape[2]), lambda i, j: (i, 0, 0)),
                pl.BlockSpec((1, s, v.shape[2]), lambda i, j: (i, 0, 0))]
    in_specs += list(extra_specs)
    return pl.pallas_call(
        kernel,
        grid=(b, t // tq),
        in_specs=in_specs,
        out_specs=pl.BlockSpec((1, tq, 512), lambda i, j: (i, j, 0)),
        out_shape=jax.ShapeDtypeStruct((b, t, 512), BF16),
        compiler_params=_params(2),
        name=name,
    )(q, k, v, *extra)


def _ssm_kernel(chunked, want_final, *refs):
    u_ref, bg_ref, ab_ref, cg_ref, d_ref = refs[:5]
    pos = 5
    if chunked:
        s0_ref = refs[pos]
        pos += 1
    y_ref = refs[pos]
    pos += 1
    if want_final:
        fin_ref = refs[pos]
        pos += 1
    hs_ref, hb_ref, yacc_ref, ut_ref, bm_ref, cm_ref = refs[pos:pos + 6]

    @pl.when(pl.program_id(1) == 0)
    def _():
        bm_ref[...] = jnp.zeros(bm_ref.shape, bm_ref.dtype)
        cm_ref[...] = jnp.zeros(cm_ref.shape, cm_ref.dtype)
        for dr in range(2):
            for g in range(SSM_GBLOCK):
                ins = slice(g * SSM_GROUP, (g + 1) * SSM_GROUP)
                for part in range(2):
                    states = slice(part * SSM_SBLK + g * SSM_STATE, part * SSM_SBLK + (g + 1) * SSM_STATE)
                    bm_ref[dr, ins, states] = bg_ref[dr, 0, g, :, part * SSM_STATE:(part + 1) * SSM_STATE]
                    cm_ref[dr, states, ins] = cg_ref[dr, 0, g, part * SSM_STATE:(part + 1) * SSM_STATE, :]

    seq, rows, sb = SSM_SEQ, V7X_SUBLANES, SSM_SBLK
    n_slices = seq // SSM_SLICE
    slice_rows = SSM_SLICE * rows

    def visited(dr, stage):
        return stage if dr == 0 else n_slices - 1 - stage

    def rows_of(k):
        return slice(k * slice_rows, (k + 1) * slice_rows)

    def u_rows(k, first_visit):
        if first_visit:
            block = u_ref[:, k * SSM_SLICE:(k + 1) * SSM_SLICE, :]
            ut_ref[rows_of(k), :] = jnp.swapaxes(block, 0, 1).reshape(slice_rows, SSM_UBLK)
        return ut_ref[rows_of(k), :]

    def input_proj(stage):
        for dr in range(2):
            k = visited(dr, stage)
            hs_ref[dr, rows_of(k), :] = _dot(u_rows(k, stage < n_slices // 2).astype(BF16), bm_ref[dr])

    def output_proj(stage):
        for dr in range(2):
            r = rows_of(visited(dr, stage))
            part = _dot(hb_ref[dr, r, :], cm_ref[dr])
            if stage < n_slices // 2:
                yacc_ref[r, :] = part
            else:
                yacc_ref[r, :] += part

    a_parts = []
    for dr in range(2):
        a = ab_ref[dr, 0]
        a_parts.append((jnp.broadcast_to(a[:, :sb], (rows, sb)), jnp.broadcast_to(a[:, sb:], (rows, sb))))

    def store_pair(dr, t_low, low, high):
        pair_rows = slice(t_low * rows, (t_low + 2) * rows)
        hb_ref[dr, pair_rows, :sb] = jnp.concatenate([low[0], high[0]], axis=0).astype(BF16)
        hb_ref[dr, pair_rows, sb:] = jnp.concatenate([low[1], high[1]], axis=0).astype(BF16)

    def step_pairs(stage, update, store):
        for step in range(0, SSM_SLICE, 2):
            i0 = stage * SSM_SLICE + step
            states = [[None, None], [None, None]]
            for sub in range(2):
                for dr, t in enumerate((i0 + sub, seq - 1 - i0 - sub)):
                    states[dr][sub] = update(dr, slice(t * rows, (t + 1) * rows))
            if store:
                store_pair(0, i0, states[0][0], states[0][1])
                store_pair(1, seq - 2 - i0, states[1][1], states[1][0])

    def scan_slice(stage, carry):
        carry = list(carry)

        def update(dr, sl):
            ar, ai = a_parts[dr]
            hr, hi = carry[2 * dr], carry[2 * dr + 1]
            b = hs_ref[dr, sl, :]
            nr = ar * hr - ai * hi + b[:, :sb]
            ni = ar * hi + ai * hr + b[:, sb:]
            if chunked:
                hs_ref[dr, sl, :sb] = nr
                hs_ref[dr, sl, sb:] = ni
            carry[2 * dr], carry[2 * dr + 1] = nr, ni
            return nr, ni

        step_pairs(stage, update, store=not chunked)
        return tuple(carry)

    def fix_slice(stage, carry):
        carry = list(carry)

        def update(dr, sl):
            ar, ai = a_parts[dr]
            zr, zi = carry[2 * dr], carry[2 * dr + 1]
            nr = ar * zr - ai * zi
            ni = ar * zi + ai * zr
            carry[2 * dr], carry[2 * dr + 1] = nr, ni
            h = hs_ref[dr, sl, :]
            return h[:, :sb] + nr, h[:, sb:] + ni

        step_pairs(stage, update, store=True)
        return tuple(carry)

    zero = jnp.zeros((rows, sb), F32)
    fin = (zero, zero, zero, zero)
    input_proj(0)
    for stage in range(n_slices):
        if stage + 1 < n_slices:
            input_proj(stage + 1)
        fin = scan_slice(stage, fin)
        if not chunked and stage >= 1:
            output_proj(stage - 1)

    if want_final:
        for dr in range(2):
            fin_ref[dr, 0, :, :sb] = fin[2 * dr]
            fin_ref[dr, 0, :, sb:] = fin[2 * dr + 1]

    if chunked:
        entry = []
        for dr in range(2):
            ar, ai = a_parts[dr]
            pr, pi = ar[0:1], ai[0:1]
            for _ in range(int(math.log2(seq))):
                pr, pi = pr * pr - pi * pi, 2.0 * pr * pi
            fr, fi = fin[2 * dr], fin[2 * dr + 1]
            s0 = s0_ref[dr, 0, 0]
            er, ei = s0[:, :sb], s0[:, sb:]
            order = range(rows) if dr == 0 else range(rows - 1, -1, -1)
            rows_r, rows_i = [None] * rows, [None] * rows
            for c in order:
                rows_r[c], rows_i[c] = er, ei
                er, ei = (pr * er - pi * ei + fr[c:c + 1], pr * ei + pi * er + fi[c:c + 1])
            entry += [jnp.concatenate(rows_r, axis=0), jnp.concatenate(rows_i, axis=0)]
        carry = tuple(entry)
        for stage in range(n_slices):
            carry = fix_slice(stage, carry)
            if stage >= 1:
                output_proj(stage - 1)
    output_proj(n_slices - 1)

    y = _gelu_tanh(yacc_ref[...] + d_ref[...] * ut_ref[...])
    y_ref[...] = y.reshape(seq, rows, SSM_UBLK)


def _ssm(u, l, w, s0, want_final):
    r, seq, _ = u.shape
    chunked = s0 is not None
    rows = V7X_SUBLANES
    in_specs = [pl.BlockSpec((rows, seq, SSM_UBLK), lambda j, i: (i, 0, j)),
                pl.BlockSpec((None, 2, 1, SSM_GBLOCK, SSM_GROUP, 2 * SSM_STATE), lambda j, i: (l, 0, j, 0, 0, 0)),
                pl.BlockSpec((None, 2, 1, 1, 2 * SSM_SBLK), lambda j, i: (l, 0, j, 0, 0)),
                pl.BlockSpec((None, 2, 1, SSM_GBLOCK, 2 * SSM_STATE, SSM_GROUP), lambda j, i: (l, 0, j, 0, 0, 0)),
                pl.BlockSpec((None, 1, SSM_UBLK), lambda j, i: (l, 0, j))]
    args = [u, w["ssm_b"], w["ssm_a"], w["ssm_c"], w["ssm_d"]]
    if chunked:
        in_specs.append(pl.BlockSpec((None, 2, 1, 1, 1, 2 * SSM_SBLK), lambda j, i: (l, 0, i, j, 0, 0)))
        args.append(s0)
    out_specs = [pl.BlockSpec((seq, rows, SSM_UBLK), lambda j, i: (0, i, j))]
    out_shape = [jax.ShapeDtypeStruct((seq, r, SSM_WIDTH), F32)]
    if want_final:
        out_specs.append(pl.BlockSpec((2, 1, rows, 2 * SSM_SBLK), lambda j, i: (0, j, i, 0)))
        out_shape.append(jax.ShapeDtypeStruct((2, SSM_NBLK, r, 2 * SSM_SBLK), F32))
    return pl.pallas_call(
        functools.partial(_ssm_kernel, chunked, want_final),
        grid=(SSM_NBLK, r // rows),
        in_specs=in_specs, out_specs=out_specs, out_shape=out_shape,
        scratch_shapes=[pltpu.VMEM((2, seq * rows, 2 * SSM_SBLK), F32), pltpu.VMEM((2, seq * rows, 2 * SSM_SBLK), BF16),
                        pltpu.VMEM((seq * rows, SSM_UBLK), F32), pltpu.VMEM((seq * rows, SSM_UBLK), F32),
                        pltpu.VMEM((2, SSM_UBLK, 2 * SSM_SBLK), BF16), pltpu.VMEM((2, 2 * SSM_SBLK, SSM_UBLK), BF16)],
        compiler_params=_params(2),
        name="ssm_latent" if chunked else "ssm_context",
    )(*args)


def _merge_kernel(last, x_ref, sh_ref, sc_ref, ga_ref, g_ref, oa_ref, ob_ref, oc_ref, y_ref,
                  wg_ref, wm_ref, wglu_ref, bglu_ref, wbr_ref, wout_ref, fn_ref, o_ref):
    x = x_ref[...]
    h = _modulated(x, g_ref, sh_ref, sc_ref).astype(BF16)
    y = y_ref[...]
    o_d = y * _sigmoid(_dot(y.astype(BF16), wglu_ref[...]) + bglu_ref[...])
    branches = (oa_ref[...].astype(F32), ob_ref[...].astype(F32), oc_ref[...].astype(F32), o_d)
    acc = None
    for n, o in enumerate(branches):
        gate = _dot(h, wg_ref[:, n * BRANCH_WIDTH:(n + 1) * BRANCH_WIDTH])
        br = _dot((o * (gate * _sigmoid(gate))).astype(BF16), wbr_ref[n])
        term = _sigmoid(_dot(h, wm_ref[:, n * D_MODEL:(n + 1) * D_MODEL])) * br
        acc = term if acc is None else acc + term
    xn = x + ga_ref[0] * _dot(acc.astype(BF16), wout_ref[...])
    if last:
        xn = _rms(xn) * fn_ref[...]
    o_ref[...] = xn


def _merge(x, shift, scale, gate, seq, oa, ob, oc, y, l, w, final_norm, last):
    n = x.shape[0]
    tm = MERGE_TM
    row = lambda i: (i, 0)
    mod_spec = _mod_spec(shift.shape[0], tm, seq)
    in_specs = [pl.BlockSpec((tm, D_MODEL), row), mod_spec, mod_spec, mod_spec, _layer_spec((1, D_MODEL), l),
                pl.BlockSpec((tm, 512), row), pl.BlockSpec((tm, 512), row), pl.BlockSpec((tm, 512), row),
                pl.BlockSpec((tm, 512), row),
                _layer_spec((D_MODEL, N_BRANCH * BRANCH_WIDTH), l), _layer_spec((D_MODEL, N_BRANCH * D_MODEL), l),
                _layer_spec((SSM_WIDTH, SSM_WIDTH), l), _layer_spec((1, SSM_WIDTH), l),
                _layer_spec((N_BRANCH, BRANCH_WIDTH, D_MODEL), l), _layer_spec((D_MODEL, D_MODEL), l),
                _const_spec((1, D_MODEL))]
    return pl.pallas_call(
        functools.partial(_merge_kernel, last),
        grid=(n // tm,),
        in_specs=in_specs,
        out_specs=pl.BlockSpec((tm, D_MODEL), row),
        out_shape=jax.ShapeDtypeStruct((n, D_MODEL), F32),
        compiler_params=_params(1),
        name="merge",
    )(x, shift, scale, gate, w["norm_g"], oa, ob, oc, y, w["wg"], w["wm"], w["wglu"], w["bglu"],
      w["wbr"], w["wout"], final_norm)


def _rope_tables(n_tok, rot_dim):
    f32 = np.float32
    rows = n_tok // GRID_W
    row = np.repeat(np.arange(rows, dtype=f32), GRID_W)
    col = np.tile(np.arange(GRID_W, dtype=f32), rows)
    quarter = rot_dim // 4
    inv = (f32(ROPE_THETA) ** (-np.arange(quarter, dtype=f32) / f32(quarter))).astype(f32)
    ang = np.concatenate([row[:, None] * inv, col[:, None] * inv], axis=-1).astype(f32)
    return np.cos(ang).astype(f32), np.sin(ang).astype(f32)


def _rope_lane_tables(n_tok):
    c, s = _rope_tables(n_tok, MLA_ROPE)
    ones = np.ones((n_tok, MLA_NOPE), np.float32)
    pad = MLA_HEAD_PAD - MLA_NOPE - MLA_ROPE
    c_mla = np.concatenate([ones, c, c, np.ones((n_tok, pad), np.float32)], axis=1)
    s_mla = np.concatenate([0.0 * ones, -s, s, np.zeros((n_tok, pad), np.float32)], axis=1)
    c, s = _rope_tables(n_tok, DIFF_HEAD_DIM)
    c_64 = np.concatenate([c, c, c, c], axis=1)
    s_64 = np.concatenate([-s, s, -s, s], axis=1)
    return tuple(jnp.asarray(a, F32) for a in (c_mla, s_mla, c_64, s_64))


def _in_output_half(v):
    zeros = jnp.zeros_like(v[..., 0, :])
    return jnp.stack([jnp.concatenate([v[..., 0, :], zeros], axis=-1),
                      jnp.concatenate([zeros, v[..., 1, :]], axis=-1)], axis=-2)


def _weights(w_in, p):
    w_t = jnp.swapaxes(w_in, 1, 2).reshape(DEPTH * D_IN, D_MODEL)
    wmix = _transposed_rows(w_t, _IN_MIX_START, _M_WIDTH, _M_WIDTH // 2)
    wg = _transposed_rows(w_t, _IN_GATE_START, N_BRANCH * BRANCH_WIDTH, W_IN_BLOCK)
    wm = _transposed_rows(w_t, _IN_MERGE_START, N_BRANCH * D_MODEL, W_IN_BLOCK)
    zeros = lambda wd: jnp.zeros((DEPTH, D_MODEL, wd), F32)
    kpe_lo = MLA_Q_LORA + MLA_KV_LORA
    whead = jnp.concatenate([w_in[:, :, :kpe_lo], zeros(MLA_NOPE), w_in[:, :, kpe_lo:_IN_MIX_START],
                             zeros(MLA_HEAD_PAD - MLA_NOPE - MLA_ROPE)], axis=2).astype(BF16)

    hq = MLA_NOPE + MLA_ROPE
    wqb = p["w_mla_q_b"].reshape(DEPTH, MLA_Q_LORA, MLA_HEADS, hq)
    wqb = jnp.pad(wqb, ((0, 0), (0, 0), (0, 0), (0, MLA_HEAD_PAD - hq))).reshape(DEPTH, MLA_Q_LORA, -1).astype(BF16)
    wkv = p["w_mla_kv_b"].reshape(DEPTH, MLA_KV_LORA, MLA_HEADS, MLA_NOPE + MLA_V)
    wk = jnp.pad(wkv[..., :MLA_NOPE], ((0, 0), (0, 0), (0, 0), (0, MLA_HEAD_PAD - MLA_NOPE)))
    wk = wk.reshape(DEPTH, MLA_KV_LORA, -1).astype(BF16)
    wv = wkv[..., MLA_NOPE:].reshape(DEPTH, MLA_KV_LORA, -1).astype(BF16)

    lam = lax.complex(p["ssm_a_re"], p["ssm_a_im"])
    dt = jnp.exp(p["ssm_log_dt"])[..., None]
    abar = jnp.exp(lam * dt)
    bbar = ((abar - 1.0) / lam)[..., None] * lax.complex(p["ssm_b_re"], p["ssm_b_im"])
    blk = lambda a: a.reshape((DEPTH, 2, SSM_NBLK, SSM_GBLOCK) + a.shape[3:])
    b_t = jnp.swapaxes(blk(bbar), -1, -2)
    ssm_b = jnp.concatenate([jnp.real(b_t), jnp.imag(b_t)], axis=-1).astype(BF16)
    ab = blk(abar).reshape(DEPTH, 2, SSM_NBLK, 1, SSM_SBLK)
    ssm_a = jnp.concatenate([jnp.real(ab), jnp.imag(ab)], axis=-1)
    c_re = jnp.swapaxes(blk(p["ssm_c_re"]), -1, -2)
    c_im = jnp.swapaxes(blk(p["ssm_c_im"]), -1, -2)
    ssm_c = jnp.concatenate([c_re, -c_im], axis=-2).astype(BF16)

    row = lambda a: a[:, None, :]
    return dict(
        norm_g=row(p["norm_g"]), whead=whead, wmix=wmix, qn=row(p["mla_q_norm"]), wqb=wqb,
        kvn=row(p["mla_kv_norm"]), wk=wk, wv=wv,
        gqn=row(jnp.tile(p["gqa_q_norm"], (1, GQA_HEADS))), gkn=row(jnp.tile(p["gqa_k_norm"], (1, GQA_KV_HEADS))),
        lam_parts=jnp.stack([p["diff_lq1"], p["diff_lk1"], p["diff_lq2"], p["diff_lk2"]], axis=1),
        subln=row(p["diff_subln"]),
        ssm_b=ssm_b, ssm_a=ssm_a, ssm_c=ssm_c, ssm_d=row(p["ssm_d"]),
        wg=wg, wm=wm, wglu=p["ssm_glu_w"].astype(BF16), bglu=row(p["ssm_glu_b"]),
        wbr=p["w_branch_out"].astype(BF16), wout=p["w_out"].astype(BF16),
    )


def _mixers(pr, b, t, l, w, lam_init, s0):
    q, k, v, dq, dk, dv, gq, gk, gv, u = pr[:10]
    three = lambda a: a.reshape(b, a.shape[0] // b, a.shape[-1])
    k, v, dk, dv, gk, gv = (three(a) for a in (k, v, dk, dv, gk, gv))
    mla_heads = tuple((hd, hd) for hd in range(MLA_HEADS))
    gqa_heads = tuple((hd // GQA_GROUP, 2 * (hd // GQA_GROUP) + hd % 2) for hd in range(GQA_HEADS))
    diff_specs = (_layer_spec((4, DIFF_HEAD_DIM), l), _layer_spec((1, DIFF_V), l))
    diff_extra = (w["lam_parts"], w["subln"])
    if s0 is None:
        o_a, o_b, o_c = _attention_short(lam_init, mla_heads, gqa_heads,
                                         (three(q), k, v, three(dq), dk, dv, three(gq), gk, gv),
                                         diff_specs, diff_extra)
    else:
        o_a = _attention(functools.partial(_attn_heads_kernel, mla_heads, MLA_HEAD_PAD), "attn_mla",
                         three(q), k, v)
        o_b = _attention(functools.partial(_diff_attn_kernel, lam_init), "attn_diff", three(dq), dk, dv,
                         extra_specs=diff_specs, extra=diff_extra)
        o_c = _attention(functools.partial(_attn_heads_kernel, gqa_heads, GQA_HEAD_DIM), "attn_gqa",
                         three(gq), gk, gv)
    n = b * t
    u_rows = u.reshape(n // SSM_SEQ, SSM_SEQ, SSM_WIDTH)
    if s0 is None:
        y_t, fin = _ssm(u_rows, l, w, None, True)
    else:
        (y_t,) = _ssm(u_rows, l, w, s0, False)
        fin = None
    y = jnp.swapaxes(y_t, 0, 1).reshape(n, SSM_WIDTH)
    flat = lambda a: a.reshape(n, 512)
    return flat(o_a), flat(o_b), flat(o_c), y, fin


def kernel(x_prompt, x_sample, cache_mla_ckv, cache_mla_krope, cache_diff_k, cache_diff_v, cache_gqa_k, cache_gqa_v, state_ssm, c, c_ctx, norm_g, w_mod, b_mod, w_in, mla_q_norm, w_mla_q_b, mla_kv_norm, w_mla_kv_b, diff_lq1, diff_lk1, diff_lq2, diff_lk2, diff_subln, gqa_q_norm, gqa_k_norm, ssm_a_re, ssm_a_im, ssm_log_dt, ssm_b_re, ssm_b_im, ssm_c_re, ssm_c_im, ssm_d, ssm_glu_w, ssm_glu_b, w_branch_out, w_out, final_norm):
    p = dict(norm_g=norm_g, mla_q_norm=mla_q_norm, w_mla_q_b=w_mla_q_b, mla_kv_norm=mla_kv_norm,
             w_mla_kv_b=w_mla_kv_b, diff_lq1=diff_lq1, diff_lk1=diff_lk1, diff_lq2=diff_lq2, diff_lk2=diff_lk2,
             diff_subln=diff_subln, gqa_q_norm=gqa_q_norm, gqa_k_norm=gqa_k_norm, ssm_a_re=ssm_a_re,
             ssm_a_im=ssm_a_im, ssm_log_dt=ssm_log_dt, ssm_b_re=ssm_b_re, ssm_b_im=ssm_b_im, ssm_c_re=ssm_c_re,
             ssm_c_im=ssm_c_im, ssm_d=ssm_d, ssm_glu_w=ssm_glu_w, ssm_glu_b=ssm_glu_b,
             w_branch_out=w_branch_out, w_out=w_out)
    bp, tp, _ = x_prompt.shape
    bs, ts, _ = x_sample.shape
    past = cache_mla_ckv.shape[2]
    assert tp == SSM_SEQ and ts % SSM_SEQ == 0 and ts // SSM_SEQ == V7X_SUBLANES
    rope = _rope_lane_tables(ts)
    w = _weights(w_in, p)

    cvec = jnp.concatenate([c_ctx[None], c, jnp.zeros((V7X_SUBLANES - 1 - bs, D_MODEL), F32)], axis=0)
    mod = _modulation(cvec, w_mod, b_mod)

    feature_major = lambda a: jnp.moveaxis(a, 2, -1).reshape(bs, DEPTH, -1, past)
    pad_lo = MLA_NOPE
    pad_hi = MLA_HEAD_PAD - MLA_NOPE - MLA_ROPE
    kpe_ctx = jnp.pad(feature_major(cache_mla_krope), ((0, 0), (0, 0), (pad_lo, pad_hi), (0, 0)))
    dk_ctx, gk_ctx, gv_ctx = (feature_major(a) for a in (cache_diff_k, cache_gqa_k, cache_gqa_v))
    dv_ctx = cache_diff_v
    s0_all = state_ssm.reshape(bs, DEPTH, 2, SSM_NBLK, SSM_GBLOCK * SSM_STATE, 2)
    s0_all = jnp.transpose(s0_all, (1, 2, 0, 3, 5, 4)).reshape(DEPTH, 2, bs, SSM_NBLK, 1, 2 * SSM_SBLK)

    xp = x_prompt.reshape(bp * tp, D_MODEL)
    xs = x_sample.reshape(bs * ts, D_MODEL)
    fn = final_norm[None]
    sides = ()
    states = []
    for l in range(DEPTH):
        lam_init = 0.8 - 0.6 * math.exp(-0.3 * l)
        last = l == DEPTH - 1
        sh, sc, ga = (mod[l, :, i * D_MODEL:(i + 1) * D_MODEL][:, None, :] for i in range(3))

        pr = _proj(xp, sh[0:1], sc[0:1], tp, l, w, prev_sides=sides)
        o_a, o_b, o_c, y, fin = _mixers(pr, bp, tp, l, w, lam_init, None)
        xp = _merge(xp, sh[0:1], sc[0:1], ga[0:1], tp, o_a, o_b, o_c, y, l, w, fn, last)
        sides = tuple(pr[10:])
        fin = fin.reshape(2, SSM_NBLK, bp, 2, SSM_GBLOCK, SSM_STATE)
        states.append(jnp.transpose(fin, (2, 0, 1, 4, 5, 3)).reshape(bp, 2, SSM_GROUPS, SSM_STATE, 2))

        pr = _proj(xs, sh[1:1 + bs], sc[1:1 + bs], ts, l, w, rope=rope,
                   cached=(cache_mla_ckv, kpe_ctx, dk_ctx, dv_ctx, gk_ctx, gv_ctx))
        o_a, o_b, o_c, y, _ = _mixers(pr, bs, ts, l, w, lam_init, s0_all)
        xs = _merge(xs, sh[1:1 + bs], sc[1:1 + bs], ga[1:1 + bs], ts, o_a, o_b, o_c, y, l, w, fn, last)

    ckv_s, kpe_t, dk_t, dv_s, gk_t, gv_t = sides
    untranspose = lambda a: jnp.swapaxes(a, 2, 3)
    return (xp.reshape(bp, tp, D_MODEL), xs.reshape(bs, ts, D_MODEL),
            ckv_s, untranspose(kpe_t),
            untranspose(dk_t).reshape(bp, DEPTH, tp, DIFF_HEADS, 2, DIFF_HEAD_DIM), dv_s,
            untranspose(gk_t).reshape(bp, DEPTH, tp, GQA_KV_HEADS, GQA_HEAD_DIM),
            untranspose(gv_t).reshape(bp, DEPTH, tp, GQA_KV_HEADS, GQA_HEAD_DIM),
            jnp.stack(states, axis=1))
```

```python
import functools
import math

import jax
import jax.numpy as jnp
import numpy as np
from jax import lax
from jax.experimental import pallas as pl
from jax.experimental.pallas import tpu as pltpu

F32 = jnp.float32
BF16 = jnp.bfloat16

D_MODEL = 1024
DEPTH = 2
GRID_W = 64
ROPE_THETA = 10000.0
EPS = 1e-6

MLA_HEADS = 8
MLA_NOPE = 64
MLA_ROPE = 32
MLA_V = 64
MLA_Q_LORA = 256
MLA_KV_LORA = 128
MLA_SCALE = (MLA_NOPE + MLA_ROPE) ** -0.5
MLA_HEAD_PAD = 128
DIFF_HEADS = 4
DIFF_HEAD_DIM = 64
DIFF_V = 2 * DIFF_HEAD_DIM
DIFF_SCALE = DIFF_HEAD_DIM ** -0.5
GQA_HEADS = 8
GQA_KV_HEADS = 2
GQA_HEAD_DIM = 64
GQA_GROUP = GQA_HEADS // GQA_KV_HEADS
GQA_KV_WIDTH = GQA_KV_HEADS * GQA_HEAD_DIM
GQA_SCALE = GQA_HEAD_DIM ** -0.5
SSM_WIDTH = 512
SSM_GROUP = 16
SSM_GROUPS = SSM_WIDTH // SSM_GROUP
SSM_STATE = 64
N_BRANCH = 4
BRANCH_WIDTH = 512

IN_SPLITS = (MLA_Q_LORA, MLA_KV_LORA, MLA_ROPE, 512, 512, 512, 512, 128, 128, SSM_WIDTH,
             N_BRANCH * BRANCH_WIDTH, N_BRANCH * D_MODEL)
D_IN = sum(IN_SPLITS)

V7X_LANES = 128
V7X_SUBLANES = 8
V7X_VMEM_LIMIT_BYTES = 56 * 1024 * 1024

_H_QA = (0, 256)
_H_KVA = (256, 384)
_H_KPE = (384, 512)
_H_WIDTH = 512
_M_DQ = (0, 512)
_M_DK = (512, 1024)
_M_DV = (1024, 1536)
_M_GQ = (1536, 2048)
_M_GK = (2048, 2176)
_M_GV = (2176, 2304)
_M_U = (2304, 2816)
_M_WIDTH = 2816
_IN_MIX_START = MLA_Q_LORA + MLA_KV_LORA + MLA_ROPE
_IN_GATE_START = _IN_MIX_START + _M_WIDTH
_IN_MERGE_START = _IN_GATE_START + N_BRANCH * BRANCH_WIDTH

SSM_GBLOCK = 8
SSM_NBLK = SSM_GROUPS // SSM_GBLOCK
SSM_SBLK = SSM_GBLOCK * SSM_STATE
SSM_UBLK = SSM_GBLOCK * SSM_GROUP
SSM_SEQ = 256
SSM_SLICE = 32

V_PAD = 128

W_IN_BLOCK = 1024
PROJ_TM = 512
MERGE_TM = 512
ATTN_TQ_LONG = 512
ATTN_SEQ_PER_STEP = 2
ATTN_GROUP_LONG = 2


def _params(n_axes, vmem=V7X_VMEM_LIMIT_BYTES):
    return pltpu.CompilerParams(dimension_semantics=("arbitrary",) * n_axes, vmem_limit_bytes=vmem)


def _const_spec(shape):
    zeros = (0,) * len(shape)
    return pl.BlockSpec(shape, lambda *_: zeros, pipeline_mode=pl.Buffered(1))


def _layer_spec(shape, l):
    zeros = (0,) * len(shape)
    return pl.BlockSpec((None,) + tuple(shape), lambda *_: (l,) + zeros, pipeline_mode=pl.Buffered(1))


def _mod_spec(n_rows, tm, seq):
    if n_rows == 1:
        return pl.BlockSpec((1, 1, D_MODEL), lambda i: (0, 0, 0))
    return pl.BlockSpec((1, 1, D_MODEL), lambda i: (i * tm // seq, 0, 0))


def _dot(a, b):
    return jnp.dot(a, b, preferred_element_type=F32)


def _dot_nt(a, b):
    return lax.dot_general(a, b, (((1,), (1,)), ((), ())), preferred_element_type=F32)


def _rms(x):
    return x * lax.rsqrt(jnp.mean(x * x, axis=-1, keepdims=True) + EPS)


def _sigmoid(x):
    return 0.5 * jnp.tanh(0.5 * x) + 0.5


def _gelu_tanh(x):
    return 0.5 * x * (1.0 + jnp.tanh(math.sqrt(2.0 / math.pi) * (x + 0.044715 * (x * x * x))))


def _low_half(shape):
    return lax.broadcasted_iota(jnp.int32, shape, 1) % V7X_LANES < V7X_LANES // 2


def _head_mean_sq(x):
    tiles = []
    for i in range(0, x.shape[1], V7X_LANES):
        sq = x[:, i:i + V7X_LANES] * x[:, i:i + V7X_LANES]
        low = _low_half(sq.shape)
        sum_low = jnp.sum(jnp.where(low, sq, 0.0), axis=-1, keepdims=True)
        sum_high = jnp.sum(jnp.where(low, 0.0, sq), axis=-1, keepdims=True)
        tiles.append(jnp.where(low, sum_low, sum_high) * (1.0 / (V7X_LANES // 2)))
    return tiles[0] if len(tiles) == 1 else jnp.concatenate(tiles, axis=1)


def _in_own_half(tile, low):
    return jnp.where(low, tile, 0.0), jnp.where(low, 0.0, tile)


def _rope_chunk(x, cos, sin, half):
    lane = lax.broadcasted_iota(jnp.int32, x.shape, 1)
    first = (lane % (2 * half)) < half
    swapped = jnp.where(first, pltpu.roll(x, V7X_LANES - half, 1), pltpu.roll(x, half, 1))
    return x * cos + swapped * sin


def _transpose_cast_kernel(w_ref, o_ref):
    o_ref[...] = w_ref[...].astype(BF16).T


def _transposed_rows(w_t, row0, n_rows, blk):
    return pl.pallas_call(
        _transpose_cast_kernel,
        grid=(DEPTH, n_rows // blk),
        in_specs=[pl.BlockSpec((pl.Element(blk), pl.Element(D_MODEL)),
                               lambda l, j: (pl.multiple_of(l * D_IN + row0 + j * blk, V7X_SUBLANES), 0))],
        out_specs=pl.BlockSpec((None, D_MODEL, blk), lambda l, j: (l, 0, j)),
        out_shape=jax.ShapeDtypeStruct((DEPTH, D_MODEL, n_rows), BF16),
        compiler_params=_params(2),
        name="w_in_columns",
    )(w_t)


def _mod_kernel(c_ref, w_ref, b_ref, o_ref):
    c = c_ref[...]
    a = (c * _sigmoid(c)).astype(BF16)
    o_ref[0] = _dot(a, w_ref[0].astype(BF16)) + b_ref[0]


def _modulation(cvec, w_mod, b_mod):
    tn = 512
    return pl.pallas_call(
        _mod_kernel,
        grid=(DEPTH, 3 * D_MODEL // tn),
        in_specs=[pl.BlockSpec((V7X_SUBLANES, D_MODEL), lambda l, j: (0, 0)),
                  pl.BlockSpec((1, D_MODEL, tn), lambda l, j: (l, 0, j)),
                  pl.BlockSpec((1, 1, tn), lambda l, j: (l, 0, j))],
        out_specs=pl.BlockSpec((1, V7X_SUBLANES, tn), lambda l, j: (l, 0, j)),
        out_shape=jax.ShapeDtypeStruct((DEPTH, V7X_SUBLANES, 3 * D_MODEL), F32),
        compiler_params=_params(2),
        name="modulation",
    )(cvec, w_mod, b_mod.reshape(DEPTH, 1, 3 * D_MODEL))


def _modulated(x, g_ref, sh_ref, sc_ref):
    return (_rms(x) * g_ref[...]) * (1.0 + sc_ref[0]) + sh_ref[0]


_PROJ_FIXED_INPUTS = 13


def _proj_kernel(latent, n_prev, first_layer, *refs):
    (x_ref, sh_ref, sc_ref, g_ref, wh_ref, wx_ref, qn_ref, wqb_ref, kvn_ref, wk_ref, wv_ref,
     gqn_ref, gkn_ref) = refs[:_PROJ_FIXED_INPUTS]
    pos = _PROJ_FIXED_INPUTS
    if latent:
        c_mla, s_mla, c_64, s_64, gm_ref = refs[pos:pos + 5]
        cached = refs[pos + 5:pos + 11]
        pos += 11
    pos += n_prev
    (q_o, k_o, v_o, dq_o, dk_o, dv_o, gq_o, gk_o, gv_o, u_o) = refs[pos:pos + 10]
    pos += 10
    if not latent:
        ckv_f, kpe_t, dk_t, dv_f, gk_t, gv_t = refs[pos:pos + 6]

    def mla_keys_values(ckv_b, kpe):
        kn = _dot(ckv_b, wk_ref[...])
        for hd in range(MLA_HEADS):
            sl = slice(hd * MLA_HEAD_PAD, (hd + 1) * MLA_HEAD_PAD)
            k_o[:, sl] = (kn[:, sl] + kpe).astype(BF16)
        vc = _dot(ckv_b, wv_ref[...])
        low = _low_half((vc.shape[0], V7X_LANES))
        for pair in range(MLA_HEADS // 2):
            even, odd = _in_own_half(vc[:, pair * V7X_LANES:(pair + 1) * V7X_LANES], low)
            v_o[:, (2 * pair) * V_PAD:(2 * pair + 1) * V_PAD] = even.astype(BF16)
            v_o[:, (2 * pair + 1) * V_PAD:(2 * pair + 2) * V_PAD] = odd.astype(BF16)

    def gqa_values(gv):
        low = _low_half(gv.shape)
        swapped = pltpu.roll(gv, V7X_LANES // 2, 1)
        blocks = (jnp.where(low, gv, 0.0), jnp.where(low, 0.0, swapped),
                  jnp.where(low, swapped, 0.0), jnp.where(low, 0.0, gv))
        for i, block in enumerate(blocks):
            gv_o[:, i * V_PAD:(i + 1) * V_PAD] = block.astype(BF16)

    def cached_tokens():
        ckv_c, kpe_t, dk_t, dv_c, gk_t, gv_t = cached
        mla_keys_values(ckv_c[...].astype(BF16), kpe_t[...].T)
        dk_o[...] = dk_t[...].T.astype(BF16)
        for hd in range(DIFF_HEADS):
            dv_o[:, hd * DIFF_V:(hd + 1) * DIFF_V] = dv_c[:, hd, :].astype(BF16)
        gk_o[...] = gk_t[...].T.astype(BF16)
        gqa_values(gv_t[...].T)

    def new_tokens():
        _proj_new_tokens(latent, first_layer, refs[:_PROJ_FIXED_INPUTS],
                         (c_mla, s_mla, c_64, s_64, gm_ref) if latent else None, refs[pos - 10:pos],
                         refs[pos:pos + 6] if not latent else None, mla_keys_values, gqa_values)

    if latent:
        pl.when(pl.program_id(1) == 0)(cached_tokens)
        pl.when(pl.program_id(1) > 0)(new_tokens)
    else:
        new_tokens()


def _proj_new_tokens(latent, first_layer, in_refs, rope_refs, out_refs, side_refs, mla_keys_values, gqa_values):
    (x_ref, sh_ref, sc_ref, g_ref, wh_ref, wx_ref, qn_ref, wqb_ref, kvn_ref, wk_ref, wv_ref,
     gqn_ref, gkn_ref) = in_refs
    (q_o, k_o, v_o, dq_o, dk_o, dv_o, gq_o, gk_o, gv_o, u_o) = out_refs
    if latent:
        c_mla, s_mla, c_64, s_64, gm_ref = rope_refs
    else:
        ckv_f, kpe_t, dk_t, dv_f, gk_t, gv_t = side_refs

    h = _modulated(x_ref[...], g_ref, sh_ref, sc_ref).astype(BF16)
    tm = h.shape[0]

    p_head = _dot(h, wh_ref[...])
    p_mix = _dot(h, wx_ref[...])

    def head(cols):
        return p_head[:, cols[0]:cols[1]]

    def mix(cols):
        return p_mix[:, cols[0]:cols[1]]

    def rope_mla(x):
        return _rope_chunk(x, c_mla[...], s_mla[...], MLA_ROPE // 2) if latent else x

    def rope_64(x):
        if not latent:
            return x
        c, s = c_64[...], s_64[...]
        chunks = [_rope_chunk(x[:, i:i + V7X_LANES], c, s, DIFF_HEAD_DIM // 2)
                  for i in range(0, x.shape[1], V7X_LANES)]
        return chunks[0] if len(chunks) == 1 else jnp.concatenate(chunks, axis=1)

    def per_sequence(x):
        return [x[b * SSM_SEQ:(b + 1) * SSM_SEQ] for b in range(tm // SSM_SEQ)]

    def slot(b):
        return (b,) if first_layer is None else (b, first_layer)

    def zero_other_slots(ref):
        if first_layer is not None:
            for b in range(ref.shape[0]):
                for other in range(ref.shape[1]):
                    if other != first_layer:
                        ref[b, other] = jnp.zeros(ref.shape[2:], ref.dtype)

    qn = (_rms(head(_H_QA)) * qn_ref[...]).astype(BF16)
    q = _dot(qn, wqb_ref[...])
    for hd in range(MLA_HEADS):
        sl = slice(hd * MLA_HEAD_PAD, (hd + 1) * MLA_HEAD_PAD)
        q_o[:, sl] = (rope_mla(q[:, sl]) * MLA_SCALE).astype(BF16)
    ckv = _rms(head(_H_KVA)) * kvn_ref[...]
    kpe = head(_H_KPE)
    if not latent:
        for b, (c_b, k_b) in enumerate(zip(per_sequence(ckv), per_sequence(kpe))):
            ckv_f[slot(b)] = c_b
            kpe_t[slot(b)] = k_b.T[MLA_NOPE:MLA_NOPE + MLA_ROPE, :]
        zero_other_slots(ckv_f)
        zero_other_slots(kpe_t)
    mla_keys_values(ckv.astype(BF16), rope_mla(kpe))

    dq_o[...] = (rope_64(mix(_M_DQ)) * DIFF_SCALE).astype(BF16)
    dk = rope_64(mix(_M_DK))
    dv = mix(_M_DV)
    dk_o[...] = dk.astype(BF16)
    dv_o[...] = dv.astype(BF16)

    gq = mix(_M_GQ)
    def head_mean_sq(x):
        if latent:
            return _dot((x * x).astype(BF16), gm_ref[:x.shape[1], :x.shape[1]])
        return _head_mean_sq(x)

    gq = gq * lax.rsqrt(head_mean_sq(gq) + EPS) * gqn_ref[...]
    gq_o[...] = (rope_64(gq) * GQA_SCALE).astype(BF16)
    gk = mix(_M_GK)
    gk = gk * lax.rsqrt(head_mean_sq(gk) + EPS) * gkn_ref[...]
    gv = mix(_M_GV)
    if not latent:
        for b, (dk_b, dv_b, gk_b, gv_b) in enumerate(zip(*(per_sequence(a) for a in (dk, dv, gk, gv)))):
            dk_t[slot(b)] = dk_b.T
            for hd in range(DIFF_HEADS):
                dv_f[slot(b) + (slice(None), hd, slice(None))] = dv_b[:, hd * DIFF_V:(hd + 1) * DIFF_V]
            gk_t[slot(b)] = gk_b.T
            gv_t[slot(b)] = gv_b.T
        for ref in (dk_t, dv_f, gk_t, gv_t):
            zero_other_slots(ref)
    gk_o[...] = rope_64(gk).astype(BF16)
    gqa_values(gv)

    u_o[...] = mix(_M_U)


def _side_shapes(n_seq):
    return [((n_seq, DEPTH, SSM_SEQ, MLA_KV_LORA), (SSM_SEQ, MLA_KV_LORA)),
            ((n_seq, DEPTH, MLA_ROPE, SSM_SEQ), (MLA_ROPE, SSM_SEQ)),
            ((n_seq, DEPTH, 512, SSM_SEQ), (512, SSM_SEQ)),
            ((n_seq, DEPTH, SSM_SEQ, DIFF_HEADS, DIFF_V), (SSM_SEQ, DIFF_HEADS, DIFF_V)),
            ((n_seq, DEPTH, GQA_KV_WIDTH, SSM_SEQ), (GQA_KV_WIDTH, SSM_SEQ)),
            ((n_seq, DEPTH, GQA_KV_WIDTH, SSM_SEQ), (GQA_KV_WIDTH, SSM_SEQ))]


def _proj(x, shift, scale, seq, l, w, rope=None, cached=None, prev_sides=()):
    n = x.shape[0]
    tm = PROJ_TM
    latent = cached is not None
    kw = GQA_KV_WIDTH
    if latent:
        tiles = seq // tm
        assert cached[0].shape[2] == tm
        grid = (n // seq, 1 + tiles)
        row = lambda b, j: (b * tiles + jnp.maximum(j - 1, 0), 0)
        kv_row = lambda b, j: (b * (1 + tiles) + j, 0)
        n_kv = (n // seq) * (cached[0].shape[2] + seq)
        mod_spec = pl.BlockSpec((1, 1, D_MODEL), lambda b, j: (b, 0, 0))
    else:
        grid = (n // tm,)
        row = kv_row = lambda i: (i, 0)
        n_kv = n
        mod_spec = _mod_spec(shift.shape[0], tm, seq)
    in_specs = [pl.BlockSpec((tm, D_MODEL), row), mod_spec, mod_spec,
                _layer_spec((1, D_MODEL), l), _layer_spec((D_MODEL, _H_WIDTH), l), _layer_spec((D_MODEL, _M_WIDTH), l),
                _layer_spec((1, MLA_Q_LORA), l), _layer_spec((MLA_Q_LORA, MLA_HEADS * MLA_HEAD_PAD), l),
                _layer_spec((1, MLA_KV_LORA), l), _layer_spec((MLA_KV_LORA, MLA_HEADS * MLA_HEAD_PAD), l),
                _layer_spec((MLA_KV_LORA, MLA_HEADS * MLA_V), l),
                _layer_spec((1, 512), l), _layer_spec((1, kw), l)]
    args = [x, shift, scale, w["norm_g"], w["whead"], w["wmix"], w["qn"], w["wqb"], w["kvn"], w["wk"], w["wv"],
            w["gqn"], w["gkn"]]
    assert len(args) == _PROJ_FIXED_INPUTS
    if latent:
        tab = pl.BlockSpec((tm, V7X_LANES), lambda b, j: (jnp.maximum(j - 1, 0), 0))
        in_specs += [tab] * 4 + [_const_spec((512, 512))]
        args += list(rope) + [w["gmat"]]
        in_specs += [pl.BlockSpec((None, None) + c.shape[2:], lambda b, j, nd=c.ndim: (b, l) + (0,) * (nd - 2))
                     for c in cached]
        args += list(cached)
    n_fixed = len(args)
    in_specs += [pl.BlockSpec(memory_space=pl.ANY)] * len(prev_sides)
    args += list(prev_sides)
    widths = [(1024, BF16, False), (1024, BF16, True), (MLA_HEADS * V_PAD, BF16, True), (512, BF16, False),
              (512, BF16, True), (512, BF16, True), (512, BF16, False), (kw, BF16, True),
              (2 * GQA_KV_HEADS * V_PAD, BF16, True), (512, F32, False)]
    out_specs = [pl.BlockSpec((tm, wd), kv_row if kv else row) for wd, _, kv in widths]
    out_shape = [jax.ShapeDtypeStruct((n_kv if kv else n, wd), dt) for wd, dt, kv in widths]
    aliases = {}
    first_layer = None
    if not latent:
        per_tile = tm // SSM_SEQ
        first_layer = None if prev_sides else l
        for full, blk in _side_shapes(n // SSM_SEQ):
            zeros = (0,) * len(blk)
            if prev_sides:
                out_specs.append(pl.BlockSpec((per_tile, None) + blk, lambda i, zeros=zeros: (i, l) + zeros))
            else:
                out_specs.append(pl.BlockSpec((per_tile, DEPTH) + blk, lambda i, zeros=zeros: (i, 0) + zeros))
            out_shape.append(jax.ShapeDtypeStruct(full, F32))
        aliases = {n_fixed + k: len(widths) + k for k in range(len(prev_sides))}
    return pl.pallas_call(
        functools.partial(_proj_kernel, latent, len(prev_sides), first_layer),
        grid=grid,
        in_specs=in_specs, out_specs=out_specs, out_shape=out_shape,
        input_output_aliases=aliases,
        compiler_params=_params(len(grid)),
        name="proj_latent" if latent else "proj_context",
    )(*args)


def _softmax_maps(maps):
    scores = [_dot_nt(q, k) for q, k, _ in maps]
    tops = [jnp.max(s, axis=-1, keepdims=True) for s in scores]
    weights = [jnp.exp(s - m).astype(BF16) for s, m in zip(scores, tops)]
    outs = []
    for p, (_, _, v) in zip(weights, maps):
        ones = jnp.ones((v.shape[0], V7X_LANES), BF16)
        o = _dot(p, jnp.concatenate([v, ones], axis=1))
        outs.append((o[:, :V7X_LANES], o[:, V7X_LANES:]))
    return outs


def _head_maps(heads, q_width, q_ref, k_ref, v_ref, part):
    return [(q_ref[b, :, hd * q_width:(hd + 1) * q_width],
             k_ref[b, :, heads[hd][0] * q_width:(heads[hd][0] + 1) * q_width],
             v_ref[b, :, heads[hd][1] * V_PAD:(heads[hd][1] + 1) * V_PAD]) for b, hd in part]


def _head_outputs(outs, part, o_ref):
    terms = [o * (1.0 / total) for o, total in outs]
    for (b, hd), low, high in zip(part[0::2], terms[0::2], terms[1::2]):
        o_ref[b, :, (hd // 2) * V7X_LANES:(hd // 2 + 1) * V7X_LANES] = (low + high).astype(BF16)


def _diff_maps(q_ref, k_ref, v_ref, part):
    d = DIFF_HEAD_DIM
    return [(q_ref[b, :, c * d:(c + 1) * d], k_ref[b, :, c * d:(c + 1) * d],
             v_ref[b, :, hd * DIFF_V:(hd + 1) * DIFF_V])
            for b, hd in part for c in (2 * hd, 2 * hd + 1)]


def _diff_lambda(lp_ref, lam_init):
    lp = lp_ref[...]
    return (jnp.exp(jnp.sum(lp[0:1] * lp[1:2], axis=-1, keepdims=True))
            - jnp.exp(jnp.sum(lp[2:3] * lp[3:4], axis=-1, keepdims=True)) + lam_init)


def _diff_outputs(outs, part, lam, lam_init, sub_ref, o_ref):
    mixed = [o1 * (1.0 / t1) - o2 * (lam / t2) for (o1, t1), (o2, t2) in zip(outs[0::2], outs[1::2])]
    normed = [_rms(o) * sub_ref[...] * (1.0 - lam_init) for o in mixed]
    for (b, hd), o in zip(part, normed):
        o_ref[b, :, hd * DIFF_V:(hd + 1) * DIFF_V] = o.astype(BF16)


def _items(q_ref, n_heads):
    return [(b, hd) for b in range(q_ref.shape[0]) for hd in range(n_heads)]


def _attn_heads_kernel(heads, q_width, q_ref, k_ref, v_ref, o_ref):
    items = _items(q_ref, len(heads))
    group = ATTN_GROUP_LONG
    for g0 in range(0, len(items), group):
        part = items[g0:g0 + group]
        _head_outputs(_softmax_maps(_head_maps(heads, q_width, q_ref, k_ref, v_ref, part)), part, o_ref)


def _diff_attn_kernel(lam_init, q_ref, k_ref, v_ref, lp_ref, sub_ref, o_ref):
    lam = _diff_lambda(lp_ref, lam_init)
    items = _items(q_ref, DIFF_HEADS)
    group = ATTN_GROUP_LONG // 2
    for g0 in range(0, len(items), group):
        part = items[g0:g0 + group]
        _diff_outputs(_softmax_maps(_diff_maps(q_ref, k_ref, v_ref, part)), part, lam, lam_init, sub_ref, o_ref)


def _attn_short_kernel(lam_init, mla_heads, gqa_heads, qa_ref, ka_ref, va_ref, qb_ref, kb_ref, vb_ref,
                       qc_ref, kc_ref, vc_ref, lp_ref, sub_ref, oa_ref, ob_ref, oc_ref):
    part_a, part_b, part_c = _items(qa_ref, len(mla_heads)), _items(qb_ref, DIFF_HEADS), _items(qc_ref, len(gqa_heads))
    maps_a = _head_maps(mla_heads, MLA_HEAD_PAD, qa_ref, ka_ref, va_ref, part_a)
    maps_b = _diff_maps(qb_ref, kb_ref, vb_ref, part_b)
    maps_c = _head_maps(gqa_heads, GQA_HEAD_DIM, qc_ref, kc_ref, vc_ref, part_c)
    outs = _softmax_maps(maps_a + maps_b + maps_c)
    n_a, n_b = len(maps_a), len(maps_b)
    _head_outputs(outs[:n_a], part_a, oa_ref)
    _diff_outputs(outs[n_a:n_a + n_b], part_b, _diff_lambda(lp_ref, lam_init), lam_init, sub_ref, ob_ref)
    _head_outputs(outs[n_a + n_b:], part_c, oc_ref)


def _attention_short(lam_init, mla_heads, gqa_heads, qkv, extra_specs, extra):
    b, t, _ = qkv[0].shape
    nb = ATTN_SEQ_PER_STEP
    whole = lambda i: (i, 0, 0)
    in_specs = [pl.BlockSpec((nb, t, a.shape[2]), whole) for a in qkv] + list(extra_specs)
    return pl.pallas_call(
        functools.partial(_attn_short_kernel, lam_init, mla_heads, gqa_heads),
        grid=(b // nb,),
        in_specs=in_specs,
        out_specs=[pl.BlockSpec((nb, t, 512), whole)] * 3,
        out_shape=[jax.ShapeDtypeStruct((b, t, 512), BF16)] * 3,
        compiler_params=_params(1),
        name="attn_context",
    )(*qkv, *extra)


def _attention(kernel, name, q, k, v, extra_specs=(), extra=()):
    b, t, wq = q.shape
    s = k.shape[1]
    tq = ATTN_TQ_LONG
    in_specs = [pl.BlockSpec((1, tq, wq), lambda i, j: (i, j, 0)),
                pl.BlockSpec((1, s, k.shape[2]), lambda i, j: (i, 0, 0)),
                pl.BlockSpec((1, s, v.shape[2]), lambda i, j: (i, 0, 0))]
    in_specs += list(extra_specs)
    return pl.pallas_call(
        kernel,
        grid=(b, t // tq),
        in_specs=in_specs,
        out_specs=pl.BlockSpec((1, tq, 512), lambda i, j: (i, j, 0)),
        out_shape=jax.ShapeDtypeStruct((b, t, 512), BF16),
        compiler_params=_params(2),
        name=name,
    )(q, k, v, *extra)


def _ssm_kernel(chunked, want_final, *refs):
    u_ref, bg_ref, ab_ref, cg_ref, d_ref = refs[:5]
    pos = 5
    if chunked:
        s0_ref = refs[pos]
        pos += 1
    y_ref = refs[pos]
    pos += 1
    if want_final:
        fin_ref = refs[pos]
        pos += 1
    hs_ref, hb_ref, yacc_ref, ut_ref, bm_ref, cm_ref = refs[pos:pos + 6]

    @pl.when(pl.program_id(1) == 0)
    def _():
        bm_ref[...] = jnp.zeros(bm_ref.shape, bm_ref.dtype)
        cm_ref[...] = jnp.zeros(cm_ref.shape, cm_ref.dtype)
        for dr in range(2):
            for g in range(SSM_GBLOCK):
                ins = slice(g * SSM_GROUP, (g + 1) * SSM_GROUP)
                for part in range(2):
                    states = slice(part * SSM_SBLK + g * SSM_STATE, part * SSM_SBLK + (g + 1) * SSM_STATE)
                    bm_ref[dr, ins, states] = bg_ref[dr, 0, g, :, part * SSM_STATE:(part + 1) * SSM_STATE]
                    cm_ref[dr, states, ins] = cg_ref[dr, 0, g, part * SSM_STATE:(part + 1) * SSM_STATE, :]

    seq, rows, sb = SSM_SEQ, V7X_SUBLANES, SSM_SBLK
    n_slices = seq // SSM_SLICE
    slice_rows = SSM_SLICE * rows

    def visited(dr, stage):
        return stage if dr == 0 else n_slices - 1 - stage

    def rows_of(k):
        return slice(k * slice_rows, (k + 1) * slice_rows)

    def u_rows(k, first_visit):
        if first_visit:
            block = u_ref[:, k * SSM_SLICE:(k + 1) * SSM_SLICE, :]
            ut_ref[rows_of(k), :] = jnp.swapaxes(block, 0, 1).reshape(slice_rows, SSM_UBLK)
        return ut_ref[rows_of(k), :]

    def input_proj(stage):
        for dr in range(2):
            k = visited(dr, stage)
            hs_ref[dr, rows_of(k), :] = _dot(u_rows(k, stage < n_slices // 2).astype(BF16), bm_ref[dr])

    def output_proj(stage):
        for dr in range(2):
            r = rows_of(visited(dr, stage))
            part = _dot(hb_ref[dr, r, :], cm_ref[dr])
            if stage < n_slices // 2:
                yacc_ref[r, :] = part
            else:
                yacc_ref[r, :] += part

    a_parts = []
    for dr in range(2):
        a = ab_ref[dr, 0]
        a_parts.append((jnp.broadcast_to(a[:, :sb], (rows, sb)), jnp.broadcast_to(a[:, sb:], (rows, sb))))

    def store_pair(dr, t_low, low, high):
        pair_rows = slice(t_low * rows, (t_low + 2) * rows)
        hb_ref[dr, pair_rows, :sb] = jnp.concatenate([low[0], high[0]], axis=0).astype(BF16)
        hb_ref[dr, pair_rows, sb:] = jnp.concatenate([low[1], high[1]], axis=0).astype(BF16)

    def step_pairs(stage, update, store):
        for step in range(0, SSM_SLICE, 2):
            i0 = stage * SSM_SLICE + step
            states = [[None, None], [None, None]]
            for sub in range(2):
                for dr, t in enumerate((i0 + sub, seq - 1 - i0 - sub)):
                    states[dr][sub] = update(dr, slice(t * rows, (t + 1) * rows))
            if store:
                store_pair(0, i0, states[0][0], states[0][1])
                store_pair(1, seq - 2 - i0, states[1][1], states[1][0])

    def scan_slice(stage, carry):
        carry = list(carry)

        def update(dr, sl):
            ar, ai = a_parts[dr]
            hr, hi = carry[2 * dr], carry[2 * dr + 1]
            b = hs_ref[dr, sl, :]
            nr = ar * hr - ai * hi + b[:, :sb]
            ni = ar * hi + ai * hr + b[:, sb:]
            if chunked:
                hs_ref[dr, sl, :sb] = nr
                hs_ref[dr, sl, sb:] = ni
            carry[2 * dr], carry[2 * dr + 1] = nr, ni
            return nr, ni

        step_pairs(stage, update, store=not chunked)
        return tuple(carry)

    def fix_slice(stage, carry):
        carry = list(carry)

        def update(dr, sl):
            ar, ai = a_parts[dr]
            zr, zi = carry[2 * dr], carry[2 * dr + 1]
            nr = ar * zr - ai * zi
            ni = ar * zi + ai * zr
            carry[2 * dr], carry[2 * dr + 1] = nr, ni
            h = hs_ref[dr, sl, :]
            return h[:, :sb] + nr, h[:, sb:] + ni

        step_pairs(stage, update, store=True)
        return tuple(carry)

    zero = jnp.zeros((rows, sb), F32)
    fin = (zero, zero, zero, zero)
    input_proj(0)
    for stage in range(n_slices):
        if stage + 1 < n_slices:
            input_proj(stage + 1)
        fin = scan_slice(stage, fin)
        if not chunked and stage >= 1:
            output_proj(stage - 1)

    if want_final:
        for dr in range(2):
            fin_ref[dr, 0, :, :sb] = fin[2 * dr]
            fin_ref[dr, 0, :, sb:] = fin[2 * dr + 1]

    if chunked:
        entry = []
        for dr in range(2):
            ar, ai = a_parts[dr]
            pr, pi = ar[0:1], ai[0:1]
            for _ in range(int(math.log2(seq))):
                pr, pi = pr * pr - pi * pi, 2.0 * pr * pi
            fr, fi = fin[2 * dr], fin[2 * dr + 1]
            s0 = s0_ref[dr, 0, 0]
            er, ei = s0[:, :sb], s0[:, sb:]
            order = range(rows) if dr == 0 else range(rows - 1, -1, -1)
            rows_r, rows_i = [None] * rows, [None] * rows
            for c in order:
                rows_r[c], rows_i[c] = er, ei
                er, ei = (pr * er - pi * ei + fr[c:c + 1], pr * ei + pi * er + fi[c:c + 1])
            entry += [jnp.concatenate(rows_r, axis=0), jnp.concatenate(rows_i, axis=0)]
        carry = tuple(entry)
        for stage in range(n_slices):
            carry = fix_slice(stage, carry)
            if stage >= 1:
                output_proj(stage - 1)
    output_proj(n_slices - 1)

    y = _gelu_tanh(yacc_ref[...] + d_ref[...] * ut_ref[...])
    y_ref[...] = y.reshape(seq, rows, SSM_UBLK)


def _ssm(u, l, w, s0, want_final):
    r, seq, _ = u.shape
    chunked = s0 is not None
    rows = V7X_SUBLANES
    in_specs = [pl.BlockSpec((rows, seq, SSM_UBLK), lambda j, i: (i, 0, j)),
                pl.BlockSpec((None, 2, 1, SSM_GBLOCK, SSM_GROUP, 2 * SSM_STATE), lambda j, i: (l, 0, j, 0, 0, 0)),
                pl.BlockSpec((None, 2, 1, 1, 2 * SSM_SBLK), lambda j, i: (l, 0, j, 0, 0)),
                pl.BlockSpec((None, 2, 1, SSM_GBLOCK, 2 * SSM_STATE, SSM_GROUP), lambda j, i: (l, 0, j, 0, 0, 0)),
                pl.BlockSpec((None, 1, SSM_UBLK), lambda j, i: (l, 0, j))]
    args = [u, w["ssm_b"], w["ssm_a"], w["ssm_c"], w["ssm_d"]]
    if chunked:
        in_specs.append(pl.BlockSpec((None, 2, 1, 1, 1, 2 * SSM_SBLK), lambda j, i: (l, 0, i, j, 0, 0)))
        args.append(s0)
    out_specs = [pl.BlockSpec((seq, rows, SSM_UBLK), lambda j, i: (0, i, j))]
    out_shape = [jax.ShapeDtypeStruct((seq, r, SSM_WIDTH), F32)]
    if want_final:
        out_specs.append(pl.BlockSpec((2, 1, rows, 2 * SSM_SBLK), lambda j, i: (0, j, i, 0)))
        out_shape.append(jax.ShapeDtypeStruct((2, SSM_NBLK, r, 2 * SSM_SBLK), F32))
    return pl.pallas_call(
        functools.partial(_ssm_kernel, chunked, want_final),
        grid=(SSM_NBLK, r // rows),
        in_specs=in_specs, out_specs=out_specs, out_shape=out_shape,
        scratch_shapes=[pltpu.VMEM((2, seq * rows, 2 * SSM_SBLK), F32), pltpu.VMEM((2, seq * rows, 2 * SSM_SBLK), BF16),
                        pltpu.VMEM((seq * rows, SSM_UBLK), F32), pltpu.VMEM((seq * rows, SSM_UBLK), F32),
                        pltpu.VMEM((2, SSM_UBLK, 2 * SSM_SBLK), BF16), pltpu.VMEM((2, 2 * SSM_SBLK, SSM_UBLK), BF16)],
        compiler_params=_params(2),
        name="ssm_latent" if chunked else "ssm_context",
    )(*args)


def _merge_kernel(last, x_ref, sh_ref, sc_ref, ga_ref, g_ref, oa_ref, ob_ref, oc_ref, y_ref,
                  wg_ref, wm_ref, wglu_ref, bglu_ref, wbr_ref, wout_ref, fn_ref, o_ref):
    x = x_ref[...]
    h = _modulated(x, g_ref, sh_ref, sc_ref).astype(BF16)
    y = y_ref[...]
    o_d = y * _sigmoid(_dot(y.astype(BF16), wglu_ref[...]) + bglu_ref[...])
    branches = (oa_ref[...].astype(F32), ob_ref[...].astype(F32), oc_ref[...].astype(F32), o_d)
    acc = None
    for n, o in enumerate(branches):
        gate = _dot(h, wg_ref[:, n * BRANCH_WIDTH:(n + 1) * BRANCH_WIDTH])
        br = _dot((o * (gate * _sigmoid(gate))).astype(BF16), wbr_ref[n])
        term = _sigmoid(_dot(h, wm_ref[:, n * D_MODEL:(n + 1) * D_MODEL])) * br
        acc = term if acc is None else acc + term
    xn = x + ga_ref[0] * _dot(acc.astype(BF16), wout_ref[...])
    if last:
        xn = _rms(xn) * fn_ref[...]
    o_ref[...] = xn


def _merge(x, shift, scale, gate, seq, oa, ob, oc, y, l, w, final_norm, last):
    n = x.shape[0]
    tm = MERGE_TM
    row = lambda i: (i, 0)
    mod_spec = _mod_spec(shift.shape[0], tm, seq)
    in_specs = [pl.BlockSpec((tm, D_MODEL), row), mod_spec, mod_spec, mod_spec, _layer_spec((1, D_MODEL), l),
                pl.BlockSpec((tm, 512), row), pl.BlockSpec((tm, 512), row), pl.BlockSpec((tm, 512), row),
                pl.BlockSpec((tm, 512), row),
                _layer_spec((D_MODEL, N_BRANCH * BRANCH_WIDTH), l), _layer_spec((D_MODEL, N_BRANCH * D_MODEL), l),
                _layer_spec((SSM_WIDTH, SSM_WIDTH), l), _layer_spec((1, SSM_WIDTH), l),
                _layer_spec((N_BRANCH, BRANCH_WIDTH, D_MODEL), l), _layer_spec((D_MODEL, D_MODEL), l),
                _const_spec((1, D_MODEL))]
    return pl.pallas_call(
        functools.partial(_merge_kernel, last),
        grid=(n // tm,),
        in_specs=in_specs,
        out_specs=pl.BlockSpec((tm, D_MODEL), row),
        out_shape=jax.ShapeDtypeStruct((n, D_MODEL), F32),
        compiler_params=_params(1),
        name="merge",
    )(x, shift, scale, gate, w["norm_g"], oa, ob, oc, y, w["wg"], w["wm"], w["wglu"], w["bglu"],
      w["wbr"], w["wout"], final_norm)


def _rope_tables(n_tok, rot_dim):
    f32 = np.float32
    rows = n_tok // GRID_W
    row = np.repeat(np.arange(rows, dtype=f32), GRID_W)
    col = np.tile(np.arange(GRID_W, dtype=f32), rows)
    quarter = rot_dim // 4
    inv = (f32(ROPE_THETA) ** (-np.arange(quarter, dtype=f32) / f32(quarter))).astype(f32)
    ang = np.concatenate([row[:, None] * inv, col[:, None] * inv], axis=-1).astype(f32)
    return np.cos(ang).astype(f32), np.sin(ang).astype(f32)


def _rope_lane_tables(n_tok):
    c, s = _rope_tables(n_tok, MLA_ROPE)
    ones = np.ones((n_tok, MLA_NOPE), np.float32)
    pad = MLA_HEAD_PAD - MLA_NOPE - MLA_ROPE
    c_mla = np.concatenate([ones, c, c, np.ones((n_tok, pad), np.float32)], axis=1)
    s_mla = np.concatenate([0.0 * ones, -s, s, np.zeros((n_tok, pad), np.float32)], axis=1)
    c, s = _rope_tables(n_tok, DIFF_HEAD_DIM)
    c_64 = np.concatenate([c, c, c, c], axis=1)
    s_64 = np.concatenate([-s, s, -s, s], axis=1)
    return tuple(jnp.asarray(a, F32) for a in (c_mla, s_mla, c_64, s_64))


def _in_output_half(v):
    zeros = jnp.zeros_like(v[..., 0, :])
    return jnp.stack([jnp.concatenate([v[..., 0, :], zeros], axis=-1),
                      jnp.concatenate([zeros, v[..., 1, :]], axis=-1)], axis=-2)


def _weights(w_in, p):
    w_t = jnp.swapaxes(w_in, 1, 2).reshape(DEPTH * D_IN, D_MODEL)
    wmix = _transposed_rows(w_t, _IN_MIX_START, _M_WIDTH, _M_WIDTH // 2)
    wg = _transposed_rows(w_t, _IN_GATE_START, N_BRANCH * BRANCH_WIDTH, W_IN_BLOCK)
    wm = _transposed_rows(w_t, _IN_MERGE_START, N_BRANCH * D_MODEL, W_IN_BLOCK)
    zeros = lambda wd: jnp.zeros((DEPTH, D_MODEL, wd), F32)
    kpe_lo = MLA_Q_LORA + MLA_KV_LORA
    whead = jnp.concatenate([w_in[:, :, :kpe_lo], zeros(MLA_NOPE), w_in[:, :, kpe_lo:_IN_MIX_START],
                             zeros(MLA_HEAD_PAD - MLA_NOPE - MLA_ROPE)], axis=2).astype(BF16)

    hq = MLA_NOPE + MLA_ROPE
    wqb = p["w_mla_q_b"].reshape(DEPTH, MLA_Q_LORA, MLA_HEADS, hq)
    wqb = jnp.pad(wqb, ((0, 0), (0, 0), (0, 0), (0, MLA_HEAD_PAD - hq))).reshape(DEPTH, MLA_Q_LORA, -1).astype(BF16)
    wkv = p["w_mla_kv_b"].reshape(DEPTH, MLA_KV_LORA, MLA_HEADS, MLA_NOPE + MLA_V)
    wk = jnp.pad(wkv[..., :MLA_NOPE], ((0, 0), (0, 0), (0, 0), (0, MLA_HEAD_PAD - MLA_NOPE)))
    wk = wk.reshape(DEPTH, MLA_KV_LORA, -1).astype(BF16)
    wv = wkv[..., MLA_NOPE:].reshape(DEPTH, MLA_KV_LORA, -1).astype(BF16)
    gmat = jnp.asarray(np.kron(np.eye(GQA_HEADS), np.full((GQA_HEAD_DIM, GQA_HEAD_DIM), 1.0 / GQA_HEAD_DIM)), BF16)

    lam = lax.complex(p["ssm_a_re"], p["ssm_a_im"])
    dt = jnp.exp(p["ssm_log_dt"])[..., None]
    abar = jnp.exp(lam * dt)
    bbar = ((abar - 1.0) / lam)[..., None] * lax.complex(p["ssm_b_re"], p["ssm_b_im"])
    blk = lambda a: a.reshape((DEPTH, 2, SSM_NBLK, SSM_GBLOCK) + a.shape[3:])
    b_t = jnp.swapaxes(blk(bbar), -1, -2)
    ssm_b = jnp.concatenate([jnp.real(b_t), jnp.imag(b_t)], axis=-1).astype(BF16)
    ab = blk(abar).reshape(DEPTH, 2, SSM_NBLK, 1, SSM_SBLK)
    ssm_a = jnp.concatenate([jnp.real(ab), jnp.imag(ab)], axis=-1)
    c_re = jnp.swapaxes(blk(p["ssm_c_re"]), -1, -2)
    c_im = jnp.swapaxes(blk(p["ssm_c_im"]), -1, -2)
    ssm_c = jnp.concatenate([c_re, -c_im], axis=-2).astype(BF16)

    row = lambda a: a[:, None, :]
    return dict(
        norm_g=row(p["norm_g"]), whead=whead, wmix=wmix, qn=row(p["mla_q_norm"]), wqb=wqb,
        kvn=row(p["mla_kv_norm"]), wk=wk, wv=wv, gmat=gmat,
        gqn=row(jnp.tile(p["gqa_q_norm"], (1, GQA_HEADS))), gkn=row(jnp.tile(p["gqa_k_norm"], (1, GQA_KV_HEADS))),
        lam_parts=jnp.stack([p["diff_lq1"], p["diff_lk1"], p["diff_lq2"], p["diff_lk2"]], axis=1),
        subln=row(p["diff_subln"]),
        ssm_b=ssm_b, ssm_a=ssm_a, ssm_c=ssm_c, ssm_d=row(p["ssm_d"]),
        wg=wg, wm=wm, wglu=p["ssm_glu_w"].astype(BF16), bglu=row(p["ssm_glu_b"]),
        wbr=p["w_branch_out"].astype(BF16), wout=p["w_out"].astype(BF16),
    )


def _mixers(pr, b, t, l, w, lam_init, s0):
    q, k, v, dq, dk, dv, gq, gk, gv, u = pr[:10]
    three = lambda a: a.reshape(b, a.shape[0] // b, a.shape[-1])
    k, v, dk, dv, gk, gv = (three(a) for a in (k, v, dk, dv, gk, gv))
    mla_heads = tuple((hd, hd) for hd in range(MLA_HEADS))
    gqa_heads = tuple((hd // GQA_GROUP, 2 * (hd // GQA_GROUP) + hd % 2) for hd in range(GQA_HEADS))
    diff_specs = (_layer_spec((4, DIFF_HEAD_DIM), l), _layer_spec((1, DIFF_V), l))
    diff_extra = (w["lam_parts"], w["subln"])
    if s0 is None:
        o_a, o_b, o_c = _attention_short(lam_init, mla_heads, gqa_heads,
                                         (three(q), k, v, three(dq), dk, dv, three(gq), gk, gv),
                                         diff_specs, diff_extra)
    else:
        o_a = _attention(functools.partial(_attn_heads_kernel, mla_heads, MLA_HEAD_PAD), "attn_mla",
                         three(q), k, v)
        o_b = _attention(functools.partial(_diff_attn_kernel, lam_init), "attn_diff", three(dq), dk, dv,
                         extra_specs=diff_specs, extra=diff_extra)
        o_c = _attention(functools.partial(_attn_heads_kernel, gqa_heads, GQA_HEAD_DIM), "attn_gqa",
                         three(gq), gk, gv)
    n = b * t
    u_rows = u.reshape(n // SSM_SEQ, SSM_SEQ, SSM_WIDTH)
    if s0 is None:
        y_t, fin = _ssm(u_rows, l, w, None, True)
    else:
        (y_t,) = _ssm(u_rows, l, w, s0, False)
        fin = None
    y = jnp.swapaxes(y_t, 0, 1).reshape(n, SSM_WIDTH)
    flat = lambda a: a.reshape(n, 512)
    return flat(o_a), flat(o_b), flat(o_c), y, fin


def kernel(x_prompt, x_sample, cache_mla_ckv, cache_mla_krope, cache_diff_k, cache_diff_v, cache_gqa_k, cache_gqa_v, state_ssm, c, c_ctx, norm_g, w_mod, b_mod, w_in, mla_q_norm, w_mla_q_b, mla_kv_norm, w_mla_kv_b, diff_lq1, diff_lk1, diff_lq2, diff_lk2, diff_subln, gqa_q_norm, gqa_k_norm, ssm_a_re, ssm_a_im, ssm_log_dt, ssm_b_re, ssm_b_im, ssm_c_re, ssm_c_im, ssm_d, ssm_glu_w, ssm_glu_b, w_branch_out, w_out, final_norm):
    p = dict(norm_g=norm_g, mla_q_norm=mla_q_norm, w_mla_q_b=w_mla_q_b, mla_kv_norm=mla_kv_norm,
             w_mla_kv_b=w_mla_kv_b, diff_lq1=diff_lq1, diff_lk1=diff_lk1, diff_lq2=diff_lq2, diff_lk2=diff_lk2,
             diff_subln=diff_subln, gqa_q_norm=gqa_q_norm, gqa_k_norm=gqa_k_norm, ssm_a_re=ssm_a_re,
             ssm_a_im=ssm_a_im, ssm_log_dt=ssm_log_dt, ssm_b_re=ssm_b_re, ssm_b_im=ssm_b_im, ssm_c_re=ssm_c_re,
             ssm_c_im=ssm_c_im, ssm_d=ssm_d, ssm_glu_w=ssm_glu_w, ssm_glu_b=ssm_glu_b,
             w_branch_out=w_branch_out, w_out=w_out)
    bp, tp, _ = x_prompt.shape
    bs, ts, _ = x_sample.shape
    past = cache_mla_ckv.shape[2]
    assert tp == SSM_SEQ and ts % SSM_SEQ == 0 and ts // SSM_SEQ == V7X_SUBLANES
    rope = _rope_lane_tables(ts)
    w = _weights(w_in, p)

    cvec = jnp.concatenate([c_ctx[None], c, jnp.zeros((V7X_SUBLANES - 1 - bs, D_MODEL), F32)], axis=0)
    mod = _modulation(cvec, w_mod, b_mod)

    feature_major = lambda a: jnp.moveaxis(a, 2, -1).reshape(bs, DEPTH, -1, past)
    pad_lo = MLA_NOPE
    pad_hi = MLA_HEAD_PAD - MLA_NOPE - MLA_ROPE
    kpe_ctx = jnp.pad(feature_major(cache_mla_krope), ((0, 0), (0, 0), (pad_lo, pad_hi), (0, 0)))
    dk_ctx, gk_ctx, gv_ctx = (feature_major(a) for a in (cache_diff_k, cache_gqa_k, cache_gqa_v))
    dv_ctx = cache_diff_v
    s0_all = state_ssm.reshape(bs, DEPTH, 2, SSM_NBLK, SSM_GBLOCK * SSM_STATE, 2)
    s0_all = jnp.transpose(s0_all, (1, 2, 0, 3, 5, 4)).reshape(DEPTH, 2, bs, SSM_NBLK, 1, 2 * SSM_SBLK)

    xp = x_prompt.reshape(bp * tp, D_MODEL)
    xs = x_sample.reshape(bs * ts, D_MODEL)
    fn = final_norm[None]
    sides = ()
    states = []
    for l in range(DEPTH):
        lam_init = 0.8 - 0.6 * math.exp(-0.3 * l)
        last = l == DEPTH - 1
        sh, sc, ga = (mod[l, :, i * D_MODEL:(i + 1) * D_MODEL][:, None, :] for i in range(3))

        pr = _proj(xp, sh[0:1], sc[0:1], tp, l, w, prev_sides=sides)
        o_a, o_b, o_c, y, fin = _mixers(pr, bp, tp, l, w, lam_init, None)
        xp = _merge(xp, sh[0:1], sc[0:1], ga[0:1], tp, o_a, o_b, o_c, y, l, w, fn, last)
        sides = tuple(pr[10:])
        fin = fin.reshape(2, SSM_NBLK, bp, 2, SSM_GBLOCK, SSM_STATE)
        states.append(jnp.transpose(fin, (2, 0, 1, 4, 5, 3)).reshape(bp, 2, SSM_GROUPS, SSM_STATE, 2))

        pr = _proj(xs, sh[1:1 + bs], sc[1:1 + bs], ts, l, w, rope=rope,
                   cached=(cache_mla_ckv, kpe_ctx, dk_ctx, dv_ctx, gk_ctx, gv_ctx))
        o_a, o_b, o_c, y, _ = _mixers(pr, bs, ts, l, w, lam_init, s0_all)
        xs = _merge(xs, sh[1:1 + bs], sc[1:1 + bs], ga[1:1 + bs], ts, o_a, o_b, o_c, y, l, w, fn, last)

    ckv_s, kpe_t, dk_t, dv_s, gk_t, gv_t = sides
    untranspose = lambda a: jnp.swapaxes(a, 2, 3)
    return (xp.reshape(bp, tp, D_MODEL), xs.reshape(bs, ts, D_MODEL),
            ckv_s, untranspose(kpe_t),
            untranspose(dk_t).reshape(bp, DEPTH, tp, DIFF_HEADS, 2, DIFF_HEAD_DIM), dv_s,
            untranspose(gk_t).reshape(bp, DEPTH, tp, GQA_KV_HEADS, GQA_HEAD_DIM),
            untranspose(gv_t).reshape(bp, DEPTH, tp, GQA_KV_HEADS, GQA_HEAD_DIM),
            jnp.stack(states, axis=1))
```

```python
import functools
import math

import jax
import jax.numpy as jnp
import numpy as np
from jax import lax
from jax.experimental import pallas as pl
from jax.experimental.pallas import tpu as pltpu

F32 = jnp.float32
BF16 = jnp.bfloat16

D_MODEL = 1024
DEPTH = 2
GRID_W = 64
ROPE_THETA = 10000.0
EPS = 1e-6

MLA_HEADS = 8
MLA_NOPE = 64
MLA_ROPE = 32
MLA_V = 64
MLA_Q_LORA = 256
MLA_KV_LORA = 128
LOG2_E = 1.4426950408889634
MLA_SCALE = (MLA_NOPE + MLA_ROPE) ** -0.5 * LOG2_E
MLA_HEAD_PAD = 128
DIFF_HEADS = 4
DIFF_HEAD_DIM = 64
DIFF_V = 2 * DIFF_HEAD_DIM
DIFF_SCALE = DIFF_HEAD_DIM ** -0.5 * LOG2_E
GQA_HEADS = 8
GQA_KV_HEADS = 2
GQA_HEAD_DIM = 64
GQA_GROUP = GQA_HEADS // GQA_KV_HEADS
GQA_KV_WIDTH = GQA_KV_HEADS * GQA_HEAD_DIM
GQA_SCALE = GQA_HEAD_DIM ** -0.5 * LOG2_E
SSM_WIDTH = 512
SSM_GROUP = 16
SSM_GROUPS = SSM_WIDTH // SSM_GROUP
SSM_STATE = 64
N_BRANCH = 4
BRANCH_WIDTH = 512

IN_SPLITS = (MLA_Q_LORA, MLA_KV_LORA, MLA_ROPE, 512, 512, 512, 512, 128, 128, SSM_WIDTH,
             N_BRANCH * BRANCH_WIDTH, N_BRANCH * D_MODEL)
D_IN = sum(IN_SPLITS)

V7X_LANES = 128
V7X_SUBLANES = 8
V7X_VMEM_LIMIT_BYTES = 56 * 1024 * 1024

_H_QA = (0, 256)
_H_KVA = (256, 384)
_H_KPE = (384, 512)
_H_WIDTH = 512
_M_DQ = (0, 512)
_M_DK = (512, 1024)
_M_DV = (1024, 1536)
_M_GQ = (1536, 2048)
_M_GK = (2048, 2176)
_M_GV = (2176, 2304)
_M_U = (2304, 2816)
_M_WIDTH = 2816
_IN_MIX_START = MLA_Q_LORA + MLA_KV_LORA + MLA_ROPE
_IN_GATE_START = _IN_MIX_START + _M_WIDTH
_IN_MERGE_START = _IN_GATE_START + N_BRANCH * BRANCH_WIDTH

SSM_GBLOCK = 8
SSM_NBLK = SSM_GROUPS // SSM_GBLOCK
SSM_SBLK = SSM_GBLOCK * SSM_STATE
SSM_UBLK = SSM_GBLOCK * SSM_GROUP
SSM_SEQ = 256
SSM_SLICE = 32

V_PAD = 128

W_IN_BLOCK = 1024
PROJ_TM = 512
MERGE_TM = 512
ATTN_TQ_LONG = 512
ATTN_SEQ_PER_STEP = 2
ATTN_GROUP_LONG = 2


def _params(n_axes, vmem=V7X_VMEM_LIMIT_BYTES):
    return pltpu.CompilerParams(dimension_semantics=("arbitrary",) * n_axes, vmem_limit_bytes=vmem)


def _const_spec(shape):
    zeros = (0,) * len(shape)
    return pl.BlockSpec(shape, lambda *_: zeros, pipeline_mode=pl.Buffered(1))


def _layer_spec(shape, l):
    zeros = (0,) * len(shape)
    return pl.BlockSpec((None,) + tuple(shape), lambda *_: (l,) + zeros, pipeline_mode=pl.Buffered(1))


def _mod_spec(n_rows, tm, seq):
    if n_rows == 1:
        return pl.BlockSpec((1, 1, D_MODEL), lambda i: (0, 0, 0))
    return pl.BlockSpec((1, 1, D_MODEL), lambda i: (i * tm // seq, 0, 0))


def _dot(a, b):
    return jnp.dot(a, b, preferred_element_type=F32)


def _dot_nt(a, b):
    return lax.dot_general(a, b, (((1,), (1,)), ((), ())), preferred_element_type=F32)


def _rms(x):
    return x * lax.rsqrt(jnp.mean(x * x, axis=-1, keepdims=True) + EPS)


def _sigmoid(x):
    return 0.5 * jnp.tanh(0.5 * x) + 0.5


def _gelu_tanh(x):
    return 0.5 * x * (1.0 + jnp.tanh(math.sqrt(2.0 / math.pi) * (x + 0.044715 * (x * x * x))))


def _low_half(shape):
    return lax.broadcasted_iota(jnp.int32, shape, 1) % V7X_LANES < V7X_LANES // 2


def _head_mean_sq(x):
    tiles = []
    for i in range(0, x.shape[1], V7X_LANES):
        sq = x[:, i:i + V7X_LANES] * x[:, i:i + V7X_LANES]
        low = _low_half(sq.shape)
        sum_low = jnp.sum(jnp.where(low, sq, 0.0), axis=-1, keepdims=True)
        sum_high = jnp.sum(jnp.where(low, 0.0, sq), axis=-1, keepdims=True)
        tiles.append(jnp.where(low, sum_low, sum_high) * (1.0 / (V7X_LANES // 2)))
    return tiles[0] if len(tiles) == 1 else jnp.concatenate(tiles, axis=1)


def _in_own_half(tile, low):
    return jnp.where(low, tile, 0.0), jnp.where(low, 0.0, tile)


def _rope_chunk(x, cos, sin, half):
    lane = lax.broadcasted_iota(jnp.int32, x.shape, 1)
    first = (lane % (2 * half)) < half
    swapped = jnp.where(first, pltpu.roll(x, V7X_LANES - half, 1), pltpu.roll(x, half, 1))
    return x * cos + swapped * sin


def _transpose_cast_kernel(w_ref, o_ref):
    o_ref[...] = w_ref[...].astype(BF16).T


def _transposed_rows(w_t, row0, n_rows, blk):
    return pl.pallas_call(
        _transpose_cast_kernel,
        grid=(DEPTH, n_rows // blk),
        in_specs=[pl.BlockSpec((pl.Element(blk), pl.Element(D_MODEL)),
                               lambda l, j: (pl.multiple_of(l * D_IN + row0 + j * blk, V7X_SUBLANES), 0))],
        out_specs=pl.BlockSpec((None, D_MODEL, blk), lambda l, j: (l, 0, j)),
        out_shape=jax.ShapeDtypeStruct((DEPTH, D_MODEL, n_rows), BF16),
        compiler_params=_params(2),
        name="w_in_columns",
    )(w_t)


def _mod_kernel(c_ref, w_ref, b_ref, o_ref):
    c = c_ref[...]
    a = (c * _sigmoid(c)).astype(BF16)
    o_ref[0] = _dot(a, w_ref[0].astype(BF16)) + b_ref[0]


def _modulation(cvec, w_mod, b_mod):
    tn = 512
    return pl.pallas_call(
        _mod_kernel,
        grid=(DEPTH, 3 * D_MODEL // tn),
        in_specs=[pl.BlockSpec((V7X_SUBLANES, D_MODEL), lambda l, j: (0, 0)),
                  pl.BlockSpec((1, D_MODEL, tn), lambda l, j: (l, 0, j)),
                  pl.BlockSpec((1, 1, tn), lambda l, j: (l, 0, j))],
        out_specs=pl.BlockSpec((1, V7X_SUBLANES, tn), lambda l, j: (l, 0, j)),
        out_shape=jax.ShapeDtypeStruct((DEPTH, V7X_SUBLANES, 3 * D_MODEL), F32),
        compiler_params=_params(2),
        name="modulation",
    )(cvec, w_mod, b_mod.reshape(DEPTH, 1, 3 * D_MODEL))


def _modulated(x, g_ref, sh_ref, sc_ref):
    return (_rms(x) * g_ref[...]) * (1.0 + sc_ref[0]) + sh_ref[0]


_PROJ_FIXED_INPUTS = 13


def _proj_kernel(latent, n_prev, first_layer, *refs):
    (x_ref, sh_ref, sc_ref, g_ref, wh_ref, wx_ref, qn_ref, wqb_ref, kvn_ref, wk_ref, wv_ref,
     gqn_ref, gkn_ref) = refs[:_PROJ_FIXED_INPUTS]
    pos = _PROJ_FIXED_INPUTS
    if latent:
        c_mla, s_mla, c_64, s_64, gm_ref = refs[pos:pos + 5]
        cached = refs[pos + 5:pos + 11]
        pos += 11
    pos += n_prev
    (q_o, k_o, v_o, dq_o, dk_o, dv_o, gq_o, gk_o, gv_o, u_o) = refs[pos:pos + 10]
    pos += 10
    if not latent:
        ckv_f, kpe_t, dk_t, dv_f, gk_t, gv_t = refs[pos:pos + 6]

    def mla_keys_values(ckv_b, kpe):
        kn = _dot(ckv_b, wk_ref[...])
        for hd in range(MLA_HEADS):
            sl = slice(hd * MLA_HEAD_PAD, (hd + 1) * MLA_HEAD_PAD)
            k_o[:, sl] = (kn[:, sl] + kpe).astype(BF16)
        vc = _dot(ckv_b, wv_ref[...])
        low = _low_half((vc.shape[0], V7X_LANES))
        for pair in range(MLA_HEADS // 2):
            even, odd = _in_own_half(vc[:, pair * V7X_LANES:(pair + 1) * V7X_LANES], low)
            v_o[:, (2 * pair) * V_PAD:(2 * pair + 1) * V_PAD] = even.astype(BF16)
            v_o[:, (2 * pair + 1) * V_PAD:(2 * pair + 2) * V_PAD] = odd.astype(BF16)

    def gqa_values(gv):
        low = _low_half(gv.shape)
        swapped = pltpu.roll(gv, V7X_LANES // 2, 1)
        blocks = (jnp.where(low, gv, 0.0), jnp.where(low, 0.0, swapped),
                  jnp.where(low, swapped, 0.0), jnp.where(low, 0.0, gv))
        for i, block in enumerate(blocks):
            gv_o[:, i * V_PAD:(i + 1) * V_PAD] = block.astype(BF16)

    def cached_tokens():
        ckv_c, kpe_t, dk_t, dv_c, gk_t, gv_t = cached
        mla_keys_values(ckv_c[...].astype(BF16), kpe_t[...].T)
        dk_o[...] = dk_t[...].T.astype(BF16)
        for hd in range(DIFF_HEADS):
            dv_o[:, hd * DIFF_V:(hd + 1) * DIFF_V] = dv_c[:, hd, :].astype(BF16)
        gk_o[...] = gk_t[...].T.astype(BF16)
        gqa_values(gv_t[...].T)

    def new_tokens():
        _proj_new_tokens(latent, first_layer, refs[:_PROJ_FIXED_INPUTS],
                         (c_mla, s_mla, c_64, s_64, gm_ref) if latent else None, refs[pos - 10:pos],
                         refs[pos:pos + 6] if not latent else None, mla_keys_values, gqa_values)

    if latent:
        pl.when(pl.program_id(1) == 0)(cached_tokens)
        pl.when(pl.program_id(1) > 0)(new_tokens)
    else:
        new_tokens()


def _proj_new_tokens(latent, first_layer, in_refs, rope_refs, out_refs, side_refs, mla_keys_values, gqa_values):
    (x_ref, sh_ref, sc_ref, g_ref, wh_ref, wx_ref, qn_ref, wqb_ref, kvn_ref, wk_ref, wv_ref,
     gqn_ref, gkn_ref) = in_refs
    (q_o, k_o, v_o, dq_o, dk_o, dv_o, gq_o, gk_o, gv_o, u_o) = out_refs
    if latent:
        c_mla, s_mla, c_64, s_64, gm_ref = rope_refs
    else:
        ckv_f, kpe_t, dk_t, dv_f, gk_t, gv_t = side_refs

    h = _modulated(x_ref[...], g_ref, sh_ref, sc_ref).astype(BF16)
    tm = h.shape[0]

    p_head = _dot(h, wh_ref[...])
    p_mix = _dot(h, wx_ref[...])

    def head(cols):
        return p_head[:, cols[0]:cols[1]]

    def mix(cols):
        return p_mix[:, cols[0]:cols[1]]

    def rope_mla(x):
        return _rope_chunk(x, c_mla[...], s_mla[...], MLA_ROPE // 2) if latent else x

    def rope_64(x):
        if not latent:
            return x
        c, s = c_64[...], s_64[...]
        chunks = [_rope_chunk(x[:, i:i + V7X_LANES], c, s, DIFF_HEAD_DIM // 2)
                  for i in range(0, x.shape[1], V7X_LANES)]
        return chunks[0] if len(chunks) == 1 else jnp.concatenate(chunks, axis=1)

    def per_sequence(x):
        return [x[b * SSM_SEQ:(b + 1) * SSM_SEQ] for b in range(tm // SSM_SEQ)]

    def slot(b):
        return (b,) if first_layer is None else (b, first_layer)

    def zero_other_slots(ref):
        if first_layer is not None:
            for b in range(ref.shape[0]):
                for other in range(ref.shape[1]):
                    if other != first_layer:
                        ref[b, other] = jnp.zeros(ref.shape[2:], ref.dtype)

    qn = (_rms(head(_H_QA)) * qn_ref[...]).astype(BF16)
    q = _dot(qn, wqb_ref[...])
    for hd in range(MLA_HEADS):
        sl = slice(hd * MLA_HEAD_PAD, (hd + 1) * MLA_HEAD_PAD)
        q_o[:, sl] = (rope_mla(q[:, sl]) * MLA_SCALE).astype(BF16)
    ckv = _rms(head(_H_KVA)) * kvn_ref[...]
    kpe = head(_H_KPE)
    if not latent:
        for b, (c_b, k_b) in enumerate(zip(per_sequence(ckv), per_sequence(kpe))):
            ckv_f[slot(b)] = c_b
            kpe_t[slot(b)] = k_b.T[MLA_NOPE:MLA_NOPE + MLA_ROPE, :]
        zero_other_slots(ckv_f)
        zero_other_slots(kpe_t)
    mla_keys_values(ckv.astype(BF16), rope_mla(kpe))

    dq_o[...] = (rope_64(mix(_M_DQ)) * DIFF_SCALE).astype(BF16)
    dk = rope_64(mix(_M_DK))
    dv = mix(_M_DV)
    dk_o[...] = dk.astype(BF16)
    dv_o[...] = dv.astype(BF16)

    gq = mix(_M_GQ)
    def head_mean_sq(x):
        if latent:
            return _dot((x * x).astype(BF16), gm_ref[:x.shape[1], :x.shape[1]])
        return _head_mean_sq(x)

    gq = gq * lax.rsqrt(head_mean_sq(gq) + EPS) * gqn_ref[...]
    gq_o[...] = (rope_64(gq) * GQA_SCALE).astype(BF16)
    gk = mix(_M_GK)
    gk = gk * lax.rsqrt(head_mean_sq(gk) + EPS) * gkn_ref[...]
    gv = mix(_M_GV)
    if not latent:
        for b, (dk_b, dv_b, gk_b, gv_b) in enumerate(zip(*(per_sequence(a) for a in (dk, dv, gk, gv)))):
            dk_t[slot(b)] = dk_b.T
            for hd in range(DIFF_HEADS):
                dv_f[slot(b) + (slice(None), hd, slice(None))] = dv_b[:, hd * DIFF_V:(hd + 1) * DIFF_V]
            gk_t[slot(b)] = gk_b.T
            gv_t[slot(b)] = gv_b.T
        for ref in (dk_t, dv_f, gk_t, gv_t):
            zero_other_slots(ref)
    gk_o[...] = rope_64(gk).astype(BF16)
    gqa_values(gv)

    u_o[...] = mix(_M_U)


def _side_shapes(n_seq):
    return [((n_seq, DEPTH, SSM_SEQ, MLA_KV_LORA), (SSM_SEQ, MLA_KV_LORA)),
            ((n_seq, DEPTH, MLA_ROPE, SSM_SEQ), (MLA_ROPE, SSM_SEQ)),
            ((n_seq, DEPTH, 512, SSM_SEQ), (512, SSM_SEQ)),
            ((n_seq, DEPTH, SSM_SEQ, DIFF_HEADS, DIFF_V), (SSM_SEQ, DIFF_HEADS, DIFF_V)),
            ((n_seq, DEPTH, GQA_KV_WIDTH, SSM_SEQ), (GQA_KV_WIDTH, SSM_SEQ)),
            ((n_seq, DEPTH, GQA_KV_WIDTH, SSM_SEQ), (GQA_KV_WIDTH, SSM_SEQ))]


def _proj(x, shift, scale, seq, l, w, rope=None, cached=None, prev_sides=()):
    n = x.shape[0]
    tm = PROJ_TM
    latent = cached is not None
    kw = GQA_KV_WIDTH
    if latent:
        tiles = seq // tm
        assert cached[0].shape[2] == tm
        grid = (n // seq, 1 + tiles)
        row = lambda b, j: (b * tiles + jnp.maximum(j - 1, 0), 0)
        kv_row = lambda b, j: (b * (1 + tiles) + j, 0)
        n_kv = (n // seq) * (cached[0].shape[2] + seq)
        mod_spec = pl.BlockSpec((1, 1, D_MODEL), lambda b, j: (b, 0, 0))
    else:
        grid = (n // tm,)
        row = kv_row = lambda i: (i, 0)
        n_kv = n
        mod_spec = _mod_spec(shift.shape[0], tm, seq)
    in_specs = [pl.BlockSpec((tm, D_MODEL), row), mod_spec, mod_spec,
                _layer_spec((1, D_MODEL), l), _layer_spec((D_MODEL, _H_WIDTH), l), _layer_spec((D_MODEL, _M_WIDTH), l),
                _layer_spec((1, MLA_Q_LORA), l), _layer_spec((MLA_Q_LORA, MLA_HEADS * MLA_HEAD_PAD), l),
                _layer_spec((1, MLA_KV_LORA), l), _layer_spec((MLA_KV_LORA, MLA_HEADS * MLA_HEAD_PAD), l),
                _layer_spec((MLA_KV_LORA, MLA_HEADS * MLA_V), l),
                _layer_spec((1, 512), l), _layer_spec((1, kw), l)]
    args = [x, shift, scale, w["norm_g"], w["whead"], w["wmix"], w["qn"], w["wqb"], w["kvn"], w["wk"], w["wv"],
            w["gqn"], w["gkn"]]
    assert len(args) == _PROJ_FIXED_INPUTS
    if latent:
        tab = pl.BlockSpec((tm, V7X_LANES), lambda b, j: (jnp.maximum(j - 1, 0), 0))
        in_specs += [tab] * 4 + [_const_spec((512, 512))]
        args += list(rope) + [w["gmat"]]
        in_specs += [pl.BlockSpec((None, None) + c.shape[2:], lambda b, j, nd=c.ndim: (b, l) + (0,) * (nd - 2))
                     for c in cached]
        args += list(cached)
    n_fixed = len(args)
    in_specs += [pl.BlockSpec(memory_space=pl.ANY)] * len(prev_sides)
    args += list(prev_sides)
    widths = [(1024, BF16, False), (1024, BF16, True), (MLA_HEADS * V_PAD, BF16, True), (512, BF16, False),
              (512, BF16, True), (512, BF16, True), (512, BF16, False), (kw, BF16, True),
              (2 * GQA_KV_HEADS * V_PAD, BF16, True), (512, F32, False)]
    out_specs = [pl.BlockSpec((tm, wd), kv_row if kv else row) for wd, _, kv in widths]
    out_shape = [jax.ShapeDtypeStruct((n_kv if kv else n, wd), dt) for wd, dt, kv in widths]
    aliases = {}
    first_layer = None
    if not latent:
        per_tile = tm // SSM_SEQ
        first_layer = None if prev_sides else l
        for full, blk in _side_shapes(n // SSM_SEQ):
            zeros = (0,) * len(blk)
            if prev_sides:
                out_specs.append(pl.BlockSpec((per_tile, None) + blk, lambda i, zeros=zeros: (i, l) + zeros))
            else:
                out_specs.append(pl.BlockSpec((per_tile, DEPTH) + blk, lambda i, zeros=zeros: (i, 0) + zeros))
            out_shape.append(jax.ShapeDtypeStruct(full, F32))
        aliases = {n_fixed + k: len(widths) + k for k in range(len(prev_sides))}
    return pl.pallas_call(
        functools.partial(_proj_kernel, latent, len(prev_sides), first_layer),
        grid=grid,
        in_specs=in_specs, out_specs=out_specs, out_shape=out_shape,
        input_output_aliases=aliases,
        compiler_params=_params(len(grid)),
        name="proj_latent" if latent else "proj_context",
    )(*args)


def _softmax_maps(maps):
    scores = [_dot_nt(q, k) for q, k, _ in maps]
    tops = [jnp.max(s, axis=-1, keepdims=True) for s in scores]
    weights = [jnp.exp2(s - m).astype(BF16) for s, m in zip(scores, tops)]
    outs = []
    for p, (_, _, v) in zip(weights, maps):
        ones = jnp.ones((v.shape[0], V7X_LANES), BF16)
        o = _dot(p, jnp.concatenate([v, ones], axis=1))
        outs.append((o[:, :V7X_LANES], o[:, V7X_LANES:]))
    return outs


def _head_maps(heads, q_width, q_ref, k_ref, v_ref, part):
    return [(q_ref[b, :, hd * q_width:(hd + 1) * q_width],
             k_ref[b, :, heads[hd][0] * q_width:(heads[hd][0] + 1) * q_width],
             v_ref[b, :, heads[hd][1] * V_PAD:(heads[hd][1] + 1) * V_PAD]) for b, hd in part]


def _head_outputs(outs, part, o_ref):
    terms = [o * (1.0 / total) for o, total in outs]
    for (b, hd), low, high in zip(part[0::2], terms[0::2], terms[1::2]):
        o_ref[b, :, (hd // 2) * V7X_LANES:(hd // 2 + 1) * V7X_LANES] = (low + high).astype(BF16)


def _diff_maps(q_ref, k_ref, v_ref, part):
    d = DIFF_HEAD_DIM
    return [(q_ref[b, :, c * d:(c + 1) * d], k_ref[b, :, c * d:(c + 1) * d],
             v_ref[b, :, hd * DIFF_V:(hd + 1) * DIFF_V])
            for b, hd in part for c in (2 * hd, 2 * hd + 1)]


def _diff_lambda(lp_ref, lam_init):
    lp = lp_ref[...]
    return (jnp.exp(jnp.sum(lp[0:1] * lp[1:2], axis=-1, keepdims=True))
            - jnp.exp(jnp.sum(lp[2:3] * lp[3:4], axis=-1, keepdims=True)) + lam_init)


def _diff_outputs(outs, part, lam, lam_init, sub_ref, o_ref):
    mixed = [o1 * (1.0 / t1) - o2 * (lam / t2) for (o1, t1), (o2, t2) in zip(outs[0::2], outs[1::2])]
    normed = [_rms(o) * sub_ref[...] * (1.0 - lam_init) for o in mixed]
    for (b, hd), o in zip(part, normed):
        o_ref[b, :, hd * DIFF_V:(hd + 1) * DIFF_V] = o.astype(BF16)


def _items(q_ref, n_heads):
    return [(b, hd) for b in range(q_ref.shape[0]) for hd in range(n_heads)]


def _attn_heads_kernel(heads, q_width, q_ref, k_ref, v_ref, o_ref):
    items = _items(q_ref, len(heads))
    group = ATTN_GROUP_LONG
    for g0 in range(0, len(items), group):
        part = items[g0:g0 + group]
        _head_outputs(_softmax_maps(_head_maps(heads, q_width, q_ref, k_ref, v_ref, part)), part, o_ref)


def _diff_attn_kernel(lam_init, q_ref, k_ref, v_ref, lp_ref, sub_ref, o_ref):
    lam = _diff_lambda(lp_ref, lam_init)
    items = _items(q_ref, DIFF_HEADS)
    group = ATTN_GROUP_LONG // 2
    for g0 in range(0, len(items), group):
        part = items[g0:g0 + group]
        _diff_outputs(_softmax_maps(_diff_maps(q_ref, k_ref, v_ref, part)), part, lam, lam_init, sub_ref, o_ref)


def _attn_short_kernel(lam_init, mla_heads, gqa_heads, qa_ref, ka_ref, va_ref, qb_ref, kb_ref, vb_ref,
                       qc_ref, kc_ref, vc_ref, lp_ref, sub_ref, oa_ref, ob_ref, oc_ref):
    part_a, part_b, part_c = _items(qa_ref, len(mla_heads)), _items(qb_ref, DIFF_HEADS), _items(qc_ref, len(gqa_heads))
    maps_a = _head_maps(mla_heads, MLA_HEAD_PAD, qa_ref, ka_ref, va_ref, part_a)
    maps_b = _diff_maps(qb_ref, kb_ref, vb_ref, part_b)
    maps_c = _head_maps(gqa_heads, GQA_HEAD_DIM, qc_ref, kc_ref, vc_ref, part_c)
    outs = _softmax_maps(maps_a + maps_b + maps_c)
    n_a, n_b = len(maps_a), len(maps_b)
    _head_outputs(outs[:n_a], part_a, oa_ref)
    _diff_outputs(outs[n_a:n_a + n_b], part_b, _diff_lambda(lp_ref, lam_init), lam_init, sub_ref, ob_ref)
    _head_outputs(outs[n_a + n_b:], part_c, oc_ref)


def _attention_short(lam_init, mla_heads, gqa_heads, qkv, extra_specs, extra):
    b, t, _ = qkv[0].shape
    nb = ATTN_SEQ_PER_STEP
    whole = lambda i: (i, 0, 0)
    in_specs = [pl.BlockSpec((nb, t, a.shape[2]), whole) for a in qkv] + list(extra_specs)
    return pl.pallas_call(
        functools.partial(_attn_short_kernel, lam_init, mla_heads, gqa_heads),
        grid=(b // nb,),
        in_specs=in_specs,
        out_specs=[pl.BlockSpec((nb, t, 512), whole)] * 3,
        out_shape=[jax.ShapeDtypeStruct((b, t, 512), BF16)] * 3,
        compiler_params=_params(1),
        name="attn_context",
    )(*qkv, *extra)


def _attention(kernel, name, q, k, v, extra_specs=(), extra=()):
    b, t, wq = q.shape
    s = k.shape[1]
    tq = ATTN_TQ_LONG
    in_specs = [pl.BlockSpec((1, tq, wq), lambda i, j: (i, j, 0)),
                pl.BlockSpec((1, s, k.shape[2]), lambda i, j: (i, 0, 0)),
                pl.BlockSpec((1, s, v.shape[2]), lambda i, j: (i, 0, 0))]
    in_specs += list(extra_specs)
    return pl.pallas_call(
        kernel,
        grid=(b, t // tq),
        in_specs=in_specs,
        out_specs=pl.BlockSpec((1, tq, 512), lambda i, j: (i, j, 0)),
        out_shape=jax.ShapeDtypeStruct((b, t, 512), BF16),
        compiler_params=_params(2),
        name=name,
    )(q, k, v, *extra)


def _ssm_kernel(chunked, want_final, *refs):
    u_ref, bg_ref, ab_ref, cg_ref, d_ref = refs[:5]
    pos = 5
    if chunked:
        s0_ref = refs[pos]
        pos += 1
    y_ref = refs[pos]
    pos += 1
    if want_final:
        fin_ref = refs[pos]
        pos += 1
    hs_ref, hb_ref, yacc_ref, ut_ref, bm_ref, cm_ref = refs[pos:pos + 6]

    @pl.when(pl.program_id(1) == 0)
    def _():
        bm_ref[...] = jnp.zeros(bm_ref.shape, bm_ref.dtype)
        cm_ref[...] = jnp.zeros(cm_ref.shape, cm_ref.dtype)
        for dr in range(2):
            for g in range(SSM_GBLOCK):
                ins = slice(g * SSM_GROUP, (g + 1) * SSM_GROUP)
                for part in range(2):
                    states = slice(part * SSM_SBLK + g * SSM_STATE, part * SSM_SBLK + (g + 1) * SSM_STATE)
                    bm_ref[dr, ins, states] = bg_ref[dr, 0, g, :, part * SSM_STATE:(part + 1) * SSM_STATE]
                    cm_ref[dr, states, ins] = cg_ref[dr, 0, g, part * SSM_STATE:(part + 1) * SSM_STATE, :]

    seq, rows, sb = SSM_SEQ, V7X_SUBLANES, SSM_SBLK
    n_slices = seq // SSM_SLICE
    slice_rows = SSM_SLICE * rows

    def visited(dr, stage):
        return stage if dr == 0 else n_slices - 1 - stage

    def rows_of(k):
        return slice(k * slice_rows, (k + 1) * slice_rows)

    def u_rows(k, first_visit):
        if first_visit:
            block = u_ref[:, k * SSM_SLICE:(k + 1) * SSM_SLICE, :]
            ut_ref[rows_of(k), :] = jnp.swapaxes(block, 0, 1).reshape(slice_rows, SSM_UBLK)
        return ut_ref[rows_of(k), :]

    def input_proj(stage):
        for dr in range(2):
            k = visited(dr, stage)
            hs_ref[dr, rows_of(k), :] = _dot(u_rows(k, stage < n_slices // 2).astype(BF16), bm_ref[dr])

    def output_proj(stage):
        for dr in range(2):
            r = rows_of(visited(dr, stage))
            part = _dot(hb_ref[dr, r, :], cm_ref[dr])
            if stage < n_slices // 2:
                yacc_ref[r, :] = part
            else:
                yacc_ref[r, :] += part

    a_parts = []
    for dr in range(2):
        a = ab_ref[dr, 0]
        a_parts.append((jnp.broadcast_to(a[:, :sb], (rows, sb)), jnp.broadcast_to(a[:, sb:], (rows, sb))))

    def store_pair(dr, t_low, low, high):
        pair_rows = slice(t_low * rows, (t_low + 2) * rows)
        hb_ref[dr, pair_rows, :sb] = jnp.concatenate([low[0], high[0]], axis=0).astype(BF16)
        hb_ref[dr, pair_rows, sb:] = jnp.concatenate([low[1], high[1]], axis=0).astype(BF16)

    def step_pairs(stage, update, store):
        for step in range(0, SSM_SLICE, 2):
            i0 = stage * SSM_SLICE + step
            states = [[None, None], [None, None]]
            for sub in range(2):
                for dr, t in enumerate((i0 + sub, seq - 1 - i0 - sub)):
                    states[dr][sub] = update(dr, slice(t * rows, (t + 1) * rows))
            if store:
                store_pair(0, i0, states[0][0], states[0][1])
                store_pair(1, seq - 2 - i0, states[1][1], states[1][0])

    def scan_slice(stage, carry):
        carry = list(carry)

        def update(dr, sl):
            ar, ai = a_parts[dr]
            hr, hi = carry[2 * dr], carry[2 * dr + 1]
            b = hs_ref[dr, sl, :]
            nr = ar * hr - ai * hi + b[:, :sb]
            ni = ar * hi + ai * hr + b[:, sb:]
            if chunked:
                hs_ref[dr, sl, :sb] = nr
                hs_ref[dr, sl, sb:] = ni
            carry[2 * dr], carry[2 * dr + 1] = nr, ni
            return nr, ni

        step_pairs(stage, update, store=not chunked)
        return tuple(carry)

    def fix_slice(stage, carry):
        carry = list(carry)

        def update(dr, sl):
            ar, ai = a_parts[dr]
            zr, zi = carry[2 * dr], carry[2 * dr + 1]
            nr = ar * zr - ai * zi
            ni = ar * zi + ai * zr
            carry[2 * dr], carry[2 * dr + 1] = nr, ni
            h = hs_ref[dr, sl, :]
            return h[:, :sb] + nr, h[:, sb:] + ni

        step_pairs(stage, update, store=True)
        return tuple(carry)

    zero = jnp.zeros((rows, sb), F32)
    fin = (zero, zero, zero, zero)
    input_proj(0)
    for stage in range(n_slices):
        if stage + 1 < n_slices:
            input_proj(stage + 1)
        fin = scan_slice(stage, fin)
        if not chunked and stage >= 1:
            output_proj(stage - 1)

    if want_final:
        for dr in range(2):
            fin_ref[dr, 0, :, :sb] = fin[2 * dr]
            fin_ref[dr, 0, :, sb:] = fin[2 * dr + 1]

    if chunked:
        entry = []
        for dr in range(2):
            ar, ai = a_parts[dr]
            pr, pi = ar[0:1], ai[0:1]
            for _ in range(int(math.log2(seq))):
                pr, pi = pr * pr - pi * pi, 2.0 * pr * pi
            fr, fi = fin[2 * dr], fin[2 * dr + 1]
            s0 = s0_ref[dr, 0, 0]
            er, ei = s0[:, :sb], s0[:, sb:]
            order = range(rows) if dr == 0 else range(rows - 1, -1, -1)
            rows_r, rows_i = [None] * rows, [None] * rows
            for c in order:
                rows_r[c], rows_i[c] = er, ei
                er, ei = (pr * er - pi * ei + fr[c:c + 1], pr * ei + pi * er + fi[c:c + 1])
            entry += [jnp.concatenate(rows_r, axis=0), jnp.concatenate(rows_i, axis=0)]
        carry = tuple(entry)
        for stage in range(n_slices):
            carry = fix_slice(stage, carry)
            if stage >= 1:
                output_proj(stage - 1)
    output_proj(n_slices - 1)

    y = _gelu_tanh(yacc_ref[...] + d_ref[...] * ut_ref[...])
    y_ref[...] = y.reshape(seq, rows, SSM_UBLK)


def _ssm(u, l, w, s0, want_final):
    r, seq, _ = u.shape
    chunked = s0 is not None
    rows = V7X_SUBLANES
    in_specs = [pl.BlockSpec((rows, seq, SSM_UBLK), lambda j, i: (i, 0, j)),
                pl.BlockSpec((None, 2, 1, SSM_GBLOCK, SSM_GROUP, 2 * SSM_STATE), lambda j, i: (l, 0, j, 0, 0, 0)),
                pl.BlockSpec((None, 2, 1, 1, 2 * SSM_SBLK), lambda j, i: (l, 0, j, 0, 0)),
                pl.BlockSpec((None, 2, 1, SSM_GBLOCK, 2 * SSM_STATE, SSM_GROUP), lambda j, i: (l, 0, j, 0, 0, 0)),
                pl.BlockSpec((None, 1, SSM_UBLK), lambda j, i: (l, 0, j))]
    args = [u, w["ssm_b"], w["ssm_a"], w["ssm_c"], w["ssm_d"]]
    if chunked:
        in_specs.append(pl.BlockSpec((None, 2, 1, 1, 1, 2 * SSM_SBLK), lambda j, i: (l, 0, i, j, 0, 0)))
        args.append(s0)
    out_specs = [pl.BlockSpec((seq, rows, SSM_UBLK), lambda j, i: (0, i, j))]
    out_shape = [jax.ShapeDtypeStruct((seq, r, SSM_WIDTH), F32)]
    if want_final:
        out_specs.append(pl.BlockSpec((2, 1, rows, 2 * SSM_SBLK), lambda j, i: (0, j, i, 0)))
        out_shape.append(jax.ShapeDtypeStruct((2, SSM_NBLK, r, 2 * SSM_SBLK), F32))
    return pl.pallas_call(
        functools.partial(_ssm_kernel, chunked, want_final),
        grid=(SSM_NBLK, r // rows),
        in_specs=in_specs, out_specs=out_specs, out_shape=out_shape,
        scratch_shapes=[pltpu.VMEM((2, seq * rows, 2 * SSM_SBLK), F32), pltpu.VMEM((2, seq * rows, 2 * SSM_SBLK), BF16),
                        pltpu.VMEM((seq * rows, SSM_UBLK), F32), pltpu.VMEM((seq * rows, SSM_UBLK), F32),
                        pltpu.VMEM((2, SSM_UBLK, 2 * SSM_SBLK), BF16), pltpu.VMEM((2, 2 * SSM_SBLK, SSM_UBLK), BF16)],
        compiler_params=_params(2),
        name="ssm_latent" if chunked else "ssm_context",
    )(*args)


def _merge_kernel(last, x_ref, sh_ref, sc_ref, ga_ref, g_ref, oa_ref, ob_ref, oc_ref, y_ref,
                  wg_ref, wm_ref, wglu_ref, bglu_ref, wbr_ref, wout_ref, fn_ref, o_ref):
    x = x_ref[...]
    h = _modulated(x, g_ref, sh_ref, sc_ref).astype(BF16)
    y = y_ref[...]
    o_d = y * _sigmoid(_dot(y.astype(BF16), wglu_ref[...]) + bglu_ref[...])
    branches = (oa_ref[...].astype(F32), ob_ref[...].astype(F32), oc_ref[...].astype(F32), o_d)
    acc = None
    for n, o in enumerate(branches):
        gate = _dot(h, wg_ref[:, n * BRANCH_WIDTH:(n + 1) * BRANCH_WIDTH])
        br = _dot((o * (gate * _sigmoid(gate))).astype(BF16), wbr_ref[n])
        term = _sigmoid(_dot(h, wm_ref[:, n * D_MODEL:(n + 1) * D_MODEL])) * br
        acc = term if acc is None else acc + term
    xn = x + ga_ref[0] * _dot(acc.astype(BF16), wout_ref[...])
    if last:
        xn = _rms(xn) * fn_ref[...]
    o_ref[...] = xn


def _merge(x, shift, scale, gate, seq, oa, ob, oc, y, l, w, final_norm, last):
    n = x.shape[0]
    tm = MERGE_TM
    row = lambda i: (i, 0)
    mod_spec = _mod_spec(shift.shape[0], tm, seq)
    in_specs = [pl.BlockSpec((tm, D_MODEL), row), mod_spec, mod_spec, mod_spec, _layer_spec((1, D_MODEL), l),
                pl.BlockSpec((tm, 512), row), pl.BlockSpec((tm, 512), row), pl.BlockSpec((tm, 512), row),
                pl.BlockSpec((tm, 512), row),
                _layer_spec((D_MODEL, N_BRANCH * BRANCH_WIDTH), l), _layer_spec((D_MODEL, N_BRANCH * D_MODEL), l),
                _layer_spec((SSM_WIDTH, SSM_WIDTH), l), _layer_spec((1, SSM_WIDTH), l),
                _layer_spec((N_BRANCH, BRANCH_WIDTH, D_MODEL), l), _layer_spec((D_MODEL, D_MODEL), l),
                _const_spec((1, D_MODEL))]
    return pl.pallas_call(
        functools.partial(_merge_kernel, last),
        grid=(n // tm,),
        in_specs=in_specs,
        out_specs=pl.BlockSpec((tm, D_MODEL), row),
        out_shape=jax.ShapeDtypeStruct((n, D_MODEL), F32),
        compiler_params=_params(1),
        name="merge",
    )(x, shift, scale, gate, w["norm_g"], oa, ob, oc, y, w["wg"], w["wm"], w["wglu"], w["bglu"],
      w["wbr"], w["wout"], final_norm)


def _rope_tables(n_tok, rot_dim):
    f32 = np.float32
    rows = n_tok // GRID_W
    row = np.repeat(np.arange(rows, dtype=f32), GRID_W)
    col = np.tile(np.arange(GRID_W, dtype=f32), rows)
    quarter = rot_dim // 4
    inv = (f32(ROPE_THETA) ** (-np.arange(quarter, dtype=f32) / f32(quarter))).astype(f32)
    ang = np.concatenate([row[:, None] * inv, col[:, None] * inv], axis=-1).astype(f32)
    return np.cos(ang).astype(f32), np.sin(ang).astype(f32)


def _rope_lane_tables(n_tok):
    c, s = _rope_tables(n_tok, MLA_ROPE)
    ones = np.ones((n_tok, MLA_NOPE), np.float32)
    pad = MLA_HEAD_PAD - MLA_NOPE - MLA_ROPE
    c_mla = np.concatenate([ones, c, c, np.ones((n_tok, pad), np.float32)], axis=1)
    s_mla = np.concatenate([0.0 * ones, -s, s, np.zeros((n_tok, pad), np.float32)], axis=1)
    c, s = _rope_tables(n_tok, DIFF_HEAD_DIM)
    c_64 = np.concatenate([c, c, c, c], axis=1)
    s_64 = np.concatenate([-s, s, -s, s], axis=1)
    return tuple(jnp.asarray(a, F32) for a in (c_mla, s_mla, c_64, s_64))


def _in_output_half(v):
    zeros = jnp.zeros_like(v[..., 0, :])
    return jnp.stack([jnp.concatenate([v[..., 0, :], zeros], axis=-1),
                      jnp.concatenate([zeros, v[..., 1, :]], axis=-1)], axis=-2)


def _weights(w_in, p):
    w_t = jnp.swapaxes(w_in, 1, 2).reshape(DEPTH * D_IN, D_MODEL)
    wmix = _transposed_rows(w_t, _IN_MIX_START, _M_WIDTH, _M_WIDTH // 2)
    wg = _transposed_rows(w_t, _IN_GATE_START, N_BRANCH * BRANCH_WIDTH, W_IN_BLOCK)
    wm = _transposed_rows(w_t, _IN_MERGE_START, N_BRANCH * D_MODEL, W_IN_BLOCK)
    zeros = lambda wd: jnp.zeros((DEPTH, D_MODEL, wd), F32)
    kpe_lo = MLA_Q_LORA + MLA_KV_LORA
    whead = jnp.concatenate([w_in[:, :, :kpe_lo], zeros(MLA_NOPE), w_in[:, :, kpe_lo:_IN_MIX_START],
                             zeros(MLA_HEAD_PAD - MLA_NOPE - MLA_ROPE)], axis=2).astype(BF16)

    hq = MLA_NOPE + MLA_ROPE
    wqb = p["w_mla_q_b"].reshape(DEPTH, MLA_Q_LORA, MLA_HEADS, hq)
    wqb = jnp.pad(wqb, ((0, 0), (0, 0), (0, 0), (0, MLA_HEAD_PAD - hq))).reshape(DEPTH, MLA_Q_LORA, -1).astype(BF16)
    wkv = p["w_mla_kv_b"].reshape(DEPTH, MLA_KV_LORA, MLA_HEADS, MLA_NOPE + MLA_V)
    wk = jnp.pad(wkv[..., :MLA_NOPE], ((0, 0), (0, 0), (0, 0), (0, MLA_HEAD_PAD - MLA_NOPE)))
    wk = wk.reshape(DEPTH, MLA_KV_LORA, -1).astype(BF16)
    wv = wkv[..., MLA_NOPE:].reshape(DEPTH, MLA_KV_LORA, -1).astype(BF16)
    gmat = jnp.asarray(np.kron(np.eye(GQA_HEADS), np.full((GQA_HEAD_DIM, GQA_HEAD_DIM), 1.0 / GQA_HEAD_DIM)), BF16)

    lam = lax.complex(p["ssm_a_re"], p["ssm_a_im"])
    dt = jnp.exp(p["ssm_log_dt"])[..., None]
    abar = jnp.exp(lam * dt)
    bbar = ((abar - 1.0) / lam)[..., None] * lax.complex(p["ssm_b_re"], p["ssm_b_im"])
    blk = lambda a: a.reshape((DEPTH, 2, SSM_NBLK, SSM_GBLOCK) + a.shape[3:])
    b_t = jnp.swapaxes(blk(bbar), -1, -2)
    ssm_b = jnp.concatenate([jnp.real(b_t), jnp.imag(b_t)], axis=-1).astype(BF16)
    ab = blk(abar).reshape(DEPTH, 2, SSM_NBLK, 1, SSM_SBLK)
    ssm_a = jnp.concatenate([jnp.real(ab), jnp.imag(ab)], axis=-1)
    c_re = jnp.swapaxes(blk(p["ssm_c_re"]), -1, -2)
    c_im = jnp.swapaxes(blk(p["ssm_c_im"]), -1, -2)
    ssm_c = jnp.concatenate([c_re, -c_im], axis=-2).astype(BF16)

    row = lambda a: a[:, None, :]
    return dict(
        norm_g=row(p["norm_g"]), whead=whead, wmix=wmix, qn=row(p["mla_q_norm"]), wqb=wqb,
        kvn=row(p["mla_kv_norm"]), wk=wk, wv=wv, gmat=gmat,
        gqn=row(jnp.tile(p["gqa_q_norm"], (1, GQA_HEADS))), gkn=row(jnp.tile(p["gqa_k_norm"], (1, GQA_KV_HEADS))),
        lam_parts=jnp.stack([p["diff_lq1"], p["diff_lk1"], p["diff_lq2"], p["diff_lk2"]], axis=1),
        subln=row(p["diff_subln"]),
        ssm_b=ssm_b, ssm_a=ssm_a, ssm_c=ssm_c, ssm_d=row(p["ssm_d"]),
        wg=wg, wm=wm, wglu=p["ssm_glu_w"].astype(BF16), bglu=row(p["ssm_glu_b"]),
        wbr=p["w_branch_out"].astype(BF16), wout=p["w_out"].astype(BF16),
    )


def _mixers(pr, b, t, l, w, lam_init, s0):
    q, k, v, dq, dk, dv, gq, gk, gv, u = pr[:10]
    three = lambda a: a.reshape(b, a.shape[0] // b, a.shape[-1])
    k, v, dk, dv, gk, gv = (three(a) for a in (k, v, dk, dv, gk, gv))
    mla_heads = tuple((hd, hd) for hd in range(MLA_HEADS))
    gqa_heads = tuple((hd // GQA_GROUP, 2 * (hd // GQA_GROUP) + hd % 2) for hd in range(GQA_HEADS))
    diff_specs = (_layer_spec((4, DIFF_HEAD_DIM), l), _layer_spec((1, DIFF_V), l))
    diff_extra = (w["lam_parts"], w["subln"])
    if s0 is None:
        o_a, o_b, o_c = _attention_short(lam_init, mla_heads, gqa_heads,
                                         (three(q), k, v, three(dq), dk, dv, three(gq), gk, gv),
                                         diff_specs, diff_extra)
    else:
        o_a = _attention(functools.partial(_attn_heads_kernel, mla_heads, MLA_HEAD_PAD), "attn_mla",
                         three(q), k, v)
        o_b = _attention(functools.partial(_diff_attn_kernel, lam_init), "attn_diff", three(dq), dk, dv,
                         extra_specs=diff_specs, extra=diff_extra)
        o_c = _attention(functools.partial(_attn_heads_kernel, gqa_heads, GQA_HEAD_DIM), "attn_gqa",
                         three(gq), gk, gv)
    n = b * t
    u_rows = u.reshape(n // SSM_SEQ, SSM_SEQ, SSM_WIDTH)
    if s0 is None:
        y_t, fin = _ssm(u_rows, l, w, None, True)
    else:
        (y_t,) = _ssm(u_rows, l, w, s0, False)
        fin = None
    y = jnp.swapaxes(y_t, 0, 1).reshape(n, SSM_WIDTH)
    flat = lambda a: a.reshape(n, 512)
    return flat(o_a), flat(o_b), flat(o_c), y, fin


def kernel(x_prompt, x_sample, cache_mla_ckv, cache_mla_krope, cache_diff_k, cache_diff_v, cache_gqa_k, cache_gqa_v, state_ssm, c, c_ctx, norm_g, w_mod, b_mod, w_in, mla_q_norm, w_mla_q_b, mla_kv_norm, w_mla_kv_b, diff_lq1, diff_lk1, diff_lq2, diff_lk2, diff_subln, gqa_q_norm, gqa_k_norm, ssm_a_re, ssm_a_im, ssm_log_dt, ssm_b_re, ssm_b_im, ssm_c_re, ssm_c_im, ssm_d, ssm_glu_w, ssm_glu_b, w_branch_out, w_out, final_norm):
    p = dict(norm_g=norm_g, mla_q_norm=mla_q_norm, w_mla_q_b=w_mla_q_b, mla_kv_norm=mla_kv_norm,
             w_mla_kv_b=w_mla_kv_b, diff_lq1=diff_lq1, diff_lk1=diff_lk1, diff_lq2=diff_lq2, diff_lk2=diff_lk2,
             diff_subln=diff_subln, gqa_q_norm=gqa_q_norm, gqa_k_norm=gqa_k_norm, ssm_a_re=ssm_a_re,
             ssm_a_im=ssm_a_im, ssm_log_dt=ssm_log_dt, ssm_b_re=ssm_b_re, ssm_b_im=ssm_b_im, ssm_c_re=ssm_c_re,
             ssm_c_im=ssm_c_im, ssm_d=ssm_d, ssm_glu_w=ssm_glu_w, ssm_glu_b=ssm_glu_b,
             w_branch_out=w_branch_out, w_out=w_out)
    bp, tp, _ = x_prompt.shape
    bs, ts, _ = x_sample.shape
    past = cache_mla_ckv.shape[2]
    assert tp == SSM_SEQ and ts % SSM_SEQ == 0 and ts // SSM_SEQ == V7X_SUBLANES
    rope = _rope_lane_tables(ts)
    w = _weights(w_in, p)

    cvec = jnp.concatenate([c_ctx[None], c, jnp.zeros((V7X_SUBLANES - 1 - bs, D_MODEL), F32)], axis=0)
    mod = _modulation(cvec, w_mod, b_mod)

    feature_major = lambda a: jnp.moveaxis(a, 2, -1).reshape(bs, DEPTH, -1, past)
    pad_lo = MLA_NOPE
    pad_hi = MLA_HEAD_PAD - MLA_NOPE - MLA_ROPE
    kpe_ctx = jnp.pad(feature_major(cache_mla_krope), ((0, 0), (0, 0), (pad_lo, pad_hi), (0, 0)))
    dk_ctx, gk_ctx, gv_ctx = (feature_major(a) for a in (cache_diff_k, cache_gqa_k, cache_gqa_v))
    dv_ctx = cache_diff_v
    s0_all = state_ssm.reshape(bs, DEPTH, 2, SSM_NBLK, SSM_GBLOCK * SSM_STATE, 2)
    s0_all = jnp.transpose(s0_all, (1, 2, 0, 3, 5, 4)).reshape(DEPTH, 2, bs, SSM_NBLK, 1, 2 * SSM_SBLK)

    xp = x_prompt.reshape(bp * tp, D_MODEL)
    xs = x_sample.reshape(bs * ts, D_MODEL)
    fn = final_norm[None]
    sides = ()
    states = []
    for l in range(DEPTH):
        lam_init = 0.8 - 0.6 * math.exp(-0.3 * l)
        last = l == DEPTH - 1
        sh, sc, ga = (mod[l, :, i * D_MODEL:(i + 1) * D_MODEL][:, None, :] for i in range(3))

        pr = _proj(xp, sh[0:1], sc[0:1], tp, l, w, prev_sides=sides)
        o_a, o_b, o_c, y, fin = _mixers(pr, bp, tp, l, w, lam_init, None)
        xp = _merge(xp, sh[0:1], sc[0:1], ga[0:1], tp, o_a, o_b, o_c, y, l, w, fn, last)
        sides = tuple(pr[10:])
        fin = fin.reshape(2, SSM_NBLK, bp, 2, SSM_GBLOCK, SSM_STATE)
        states.append(jnp.transpose(fin, (2, 0, 1, 4, 5, 3)).reshape(bp, 2, SSM_GROUPS, SSM_STATE, 2))

        pr = _proj(xs, sh[1:1 + bs], sc[1:1 + bs], ts, l, w, rope=rope,
                   cached=(cache_mla_ckv, kpe_ctx, dk_ctx, dv_ctx, gk_ctx, gv_ctx))
        o_a, o_b, o_c, y, _ = _mixers(pr, bs, ts, l, w, lam_init, s0_all)
        xs = _merge(xs, sh[1:1 + bs], sc[1:1 + bs], ga[1:1 + bs], ts, o_a, o_b, o_c, y, l, w, fn, last)

    ckv_s, kpe_t, dk_t, dv_s, gk_t, gv_t = sides
    untranspose = lambda a: jnp.swapaxes(a, 2, 3)
    return (xp.reshape(bp, tp, D_MODEL), xs.reshape(bs, ts, D_MODEL),
            ckv_s, untranspose(kpe_t),
            untranspose(dk_t).reshape(bp, DEPTH, tp, DIFF_HEADS, 2, DIFF_HEAD_DIM), dv_s,
            untranspose(gk_t).reshape(bp, DEPTH, tp, GQA_KV_HEADS, GQA_HEAD_DIM),
            untranspose(gv_t).reshape(bp, DEPTH, tp, GQA_KV_HEADS, GQA_HEAD_DIM),
            jnp.stack(states, axis=1))
```
